```python
import jax, jax.numpy as jnp
from jax import lax
import numpy as np

D_MODEL = 1024
BATCH = 8
SEQ = 8192
DEPTH = 2

CHUNK = 64
NORM_EPS = 1e-6
L2_EPS = 1e-6
CONV_WIDTH = D_MODEL // 2
CONV_K = 3
HGRN_HEADS = 4
HGRN_DK = 128
HGRN_DV = 128
HGRN_WIDTH = HGRN_HEADS * HGRN_DK
GDN_QK_HEADS = 4
GDN_V_HEADS = 8
GDN_DK = 128
GDN_DV = 128
GDN_CONV_K = 4
GDN_QK_WIDTH = GDN_QK_HEADS * GDN_DK
GDN_V_WIDTH = GDN_V_HEADS * GDN_DV
N_BRANCHES = 3
IN_SPLITS = (CONV_WIDTH, CONV_WIDTH, CONV_WIDTH, CONV_WIDTH,
             HGRN_WIDTH, HGRN_WIDTH, HGRN_HEADS * HGRN_DV, HGRN_HEADS * HGRN_DV,
             GDN_QK_WIDTH, GDN_QK_WIDTH, GDN_V_WIDTH, GDN_V_HEADS, GDN_V_HEADS, GDN_V_WIDTH,
             N_BRANCHES * D_MODEL)
IN_COLS = sum(IN_SPLITS)
MIN_F = 1e-30

kernel_name = "hybrid_conv_hgrn2_gdn_gated_merge"


def rmsnorm(x, w):
    xf = x.astype(jnp.float32)
    y = xf * lax.rsqrt(jnp.mean(xf * xf, axis=-1, keepdims=True) + NORM_EPS)
    return (y * w.astype(jnp.float32)).astype(x.dtype)


def head_rmsnorm(o, w):
    return o * lax.rsqrt(jnp.mean(o * o, axis=-1, keepdims=True) + NORM_EPS) * w.astype(jnp.float32)


def l2norm(x):
    return x * lax.rsqrt(jnp.sum(x * x, axis=-1, keepdims=True) + L2_EPS)


def masked_exp(diff, mask):
    return jnp.where(mask, jnp.exp(jnp.where(mask, diff, 0.0)), 0.0)


def causal_dwconv(x, w):
    k = w.shape[0]
    return lax.conv_general_dilated(
        x, w[:, None, :].astype(x.dtype), window_strides=(1,), padding=((k - 1, 0),),
        dimension_numbers=('NWC', 'WIO', 'NWC'), feature_group_count=x.shape[-1])


def to_chunks(x, heads):
    b, s, hd = x.shape
    return x.reshape(b, s // CHUNK, CHUNK, heads, hd // heads).transpose(0, 3, 1, 2, 4)


def scalar_chunks(x):
    b, s, h = x.shape
    return x.reshape(b, s // CHUNK, CHUNK, h).transpose(0, 3, 1, 2)


def from_chunks(o):
    b, h, n, l, d = o.shape
    return o.transpose(0, 2, 3, 1, 4).reshape(b, n * l, h, d)


def hgrn2_chunked(q, k, v, log_f):
    g = jnp.cumsum(log_f, axis=-2)
    g_last = g[..., -1:, :]
    q_inter = q * jnp.exp(g)
    k_state = k * jnp.exp(g_last - g)
    causal = jnp.tril(jnp.ones((CHUNK, CHUNK), dtype=bool))[:, :, None]
    mv = lambda a: jnp.moveaxis(a, 2, 0)

    def step(state, inp):
        qc, kc, vc, gc, qi, ks, gl = inp
        diff = gc[..., :, None, :] - gc[..., None, :, :]
        decay = masked_exp(diff, causal)
        scores = jnp.einsum('bhtk,bhtsk,bhsk->bhts', qc, decay, kc)
        o = jnp.einsum('bhts,bhsv->bhtv', scores, vc) + jnp.einsum('bhtk,bhkv->bhtv', qi, state)
        state = jnp.exp(gl)[..., 0, :, None] * state + jnp.einsum('bhsk,bhsv->bhkv', ks, vc)
        return state, o

    b, h = q.shape[0], q.shape[1]
    s0 = jnp.zeros((b, h, q.shape[-1], v.shape[-1]), jnp.float32)
    _, o = lax.scan(step, s0, (mv(q), mv(k), mv(v), mv(g), mv(q_inter), mv(k_state), mv(g_last)))
    return jnp.moveaxis(o, 0, 2)


def gated_delta_chunked(q, k, v, log_a, beta):
    g = jnp.cumsum(log_a, axis=-1)
    g_last = g[..., -1]
    diff = g[..., :, None] - g[..., None, :]
    strict = jnp.tril(jnp.ones((CHUNK, CHUNK), dtype=bool), -1)
    causal = jnp.tril(jnp.ones((CHUNK, CHUNK), dtype=bool))
    decay_strict = masked_exp(diff, strict)
    decay_causal = masked_exp(diff, causal)
    kb = k * beta[..., None]
    m = jnp.einsum('bhntk,bhnsk->bhnts', k, kb) * decay_strict
    eye = jnp.eye(CHUNK, dtype=jnp.float32)
    rhs = jnp.concatenate([v, k * jnp.exp(g)[..., None]], axis=-1)
    sol = lax.linalg.triangular_solve(eye + m, rhs, left_side=True, lower=True, unit_diagonal=True)
    u, w = sol[..., :v.shape[-1]], sol[..., v.shape[-1]:]
    a_qk = jnp.einsum('bhntk,bhnsk->bhnts', q, kb) * decay_causal
    q_inter = q * jnp.exp(g)[..., None]
    k_state = kb * jnp.exp(g_last[..., None] - g)[..., None]
    mv = lambda a: jnp.moveaxis(a, 2, 0)

    def step(state, inp):
        uc, wc, aqk, qi, ks, gl = inp
        e = uc - jnp.einsum('bhlk,bhkv->bhlv', wc, state)
        o = jnp.einsum('bhlk,bhkv->bhlv', qi, state) + jnp.einsum('bhts,bhsv->bhtv', aqk, e)
        state = jnp.exp(gl)[..., None, None] * state + jnp.einsum('bhlk,bhlv->bhkv', ks, e)
        return state, o

    b, h = q.shape[0], q.shape[1]
    s0 = jnp.zeros((b, h, q.shape[-1], v.shape[-1]), jnp.float32)
    _, o = lax.scan(step, s0, (mv(u), mv(w), mv(a_qk), mv(q_inter), mv(k_state), mv(g_last)))
    return jnp.moveaxis(o, 0, 2)


def _fwd_setup_inputs(seed: int = 0) -> dict:
    key = jax.random.key(seed)
    ks = jax.random.split(key, 16)
    f32 = jnp.float32
    nrm = lambda k, shape, scale: jax.random.normal(k, shape, f32) * scale
    dt = jnp.exp(jax.random.uniform(ks[7], (DEPTH, GDN_V_HEADS), f32, np.log(1e-3), np.log(1e-1)))
    return {
        "x": nrm(ks[0], (BATCH, SEQ, D_MODEL), 1.0),
        "norm_w": 1.0 + nrm(ks[1], (DEPTH, D_MODEL), 0.02),
        "w_in": nrm(ks[2], (DEPTH, D_MODEL, IN_COLS), D_MODEL ** -0.5),
        "b_gate": nrm(ks[3], (DEPTH, N_BRANCHES * D_MODEL), 0.1),
        "conv_a": nrm(ks[4], (DEPTH, CONV_K, CONV_WIDTH), CONV_K ** -0.5),
        "conv_c": nrm(ks[5], (DEPTH, GDN_CONV_K, 2 * GDN_QK_WIDTH + GDN_V_WIDTH), GDN_CONV_K ** -0.5),
        "a_log": jnp.log(jax.random.uniform(ks[6], (DEPTH, GDN_V_HEADS), f32, 1.0, 16.0)),
        "dt_bias": dt + jnp.log(-jnp.expm1(-dt)),
        "lower_bounds": nrm(ks[8], (DEPTH, HGRN_WIDTH), 0.5),
        "hgrn_norm_w": 1.0 + nrm(ks[9], (DEPTH, HGRN_DV), 0.02),
        "gdn_norm_w": 1.0 + nrm(ks[10], (DEPTH, GDN_DV), 0.02),
        "w_out_a": nrm(ks[11], (DEPTH, CONV_WIDTH, D_MODEL), CONV_WIDTH ** -0.5),
        "w_out_b": nrm(ks[12], (DEPTH, HGRN_HEADS * HGRN_DV, D_MODEL), (HGRN_HEADS * HGRN_DV) ** -0.5),
        "w_out_c": nrm(ks[13], (DEPTH, GDN_V_WIDTH, D_MODEL), GDN_V_WIDTH ** -0.5),
        "w_o": nrm(ks[14], (DEPTH, D_MODEL, D_MODEL), D_MODEL ** -0.5),
        "final_norm_w": 1.0 + nrm(ks[15], (D_MODEL,), 0.02),
    }


def _fwd_reference(x, norm_w, w_in, b_gate, conv_a, conv_c, a_log, dt_bias, lower_bounds,
              hgrn_norm_w, gdn_norm_w, w_out_a, w_out_b, w_out_c, w_o, final_norm_w):
    f32 = jnp.float32
    bsz, seq = x.shape[0], x.shape[1]
    split_idx = tuple(int(s) for s in np.cumsum(IN_SPLITS)[:-1])
    p = jax.nn.softmax(lower_bounds.astype(f32), axis=0)
    lbs = jnp.cumsum(p, axis=0) - p[0]

    for l in range(DEPTH):
        h = rmsnorm(x, norm_w[l])
        proj = jnp.einsum('bsd,dc->bsc', h, w_in[l])
        (a_b, a_c, a_x, a_z, b_q, b_f, b_i, b_z,
         c_q, c_k, c_v, c_beta, c_a, c_z, g_all) = jnp.split(proj, split_idx, axis=-1)
        gates = jax.nn.sigmoid(g_all + b_gate[l]).reshape(bsz, seq, N_BRANCHES, D_MODEL)

        y_a = a_b * causal_dwconv(a_c * a_x, conv_a[l]) * jax.nn.silu(a_z)
        y_a = jnp.einsum('bsc,cd->bsd', y_a, w_out_a[l])

        lb = lbs[l].reshape(1, HGRN_HEADS, 1, 1, HGRN_DK)
        qb = to_chunks(jax.nn.silu(b_q.astype(f32)) * HGRN_DK ** -0.5, HGRN_HEADS)
        fz = to_chunks(b_f.astype(f32), HGRN_HEADS)
        f_gate = lb + (1.0 - lb) * jax.nn.sigmoid(fz)
        log_f = jnp.log(jnp.maximum(f_gate, MIN_F))
        kb = 1.0 - f_gate
        vb = to_chunks(b_i.astype(f32), HGRN_HEADS)
        o_b = from_chunks(hgrn2_chunked(qb, kb, vb, log_f))
        y_b = head_rmsnorm(o_b, hgrn_norm_w[l]).reshape(bsz, seq, -1) * jax.nn.silu(b_z.astype(f32))
        y_b = jnp.einsum('bsc,cd->bsd', y_b.astype(x.dtype), w_out_b[l])

        qkv = jax.nn.silu(causal_dwconv(jnp.concatenate([c_q, c_k, c_v], axis=-1), conv_c[l])).astype(f32)
        qc, kc, vc = jnp.split(qkv, (GDN_QK_WIDTH, 2 * GDN_QK_WIDTH), axis=-1)
        rep = GDN_V_HEADS // GDN_QK_HEADS
        qc = jnp.repeat(l2norm(qc.reshape(bsz, seq, GDN_QK_HEADS, GDN_DK)), rep, axis=2) * GDN_DK ** -0.5
        kc = jnp.repeat(l2norm(kc.reshape(bsz, seq, GDN_QK_HEADS, GDN_DK)), rep, axis=2)
        beta = jax.nn.sigmoid(c_beta.astype(f32))
        log_a = -jnp.exp(a_log[l].astype(f32)) * jax.nn.softplus(c_a.astype(f32) + dt_bias[l].astype(f32))
        o_c = gated_delta_chunked(
            to_chunks(qc.reshape(bsz, seq, -1), GDN_V_HEADS), to_chunks(kc.reshape(bsz, seq, -1), GDN_V_HEADS),
            to_chunks(vc, GDN_V_HEADS), scalar_chunks(log_a), scalar_chunks(beta))
        y_c = head_rmsnorm(from_chunks(o_c), gdn_norm_w[l]).reshape(bsz, seq, -1) * jax.nn.silu(c_z.astype(f32))
        y_c = jnp.einsum('bsc,cd->bsd', y_c.astype(x.dtype), w_out_c[l])

        merged = gates[:, :, 0] * y_a + gates[:, :, 1] * y_b + gates[:, :, 2] * y_c
        x = x + jnp.einsum('bsd,de->bse', merged, w_o[l])

    return rmsnorm(x, final_norm_w)


import jax as _jax
import jax.numpy as _jnp

TWIN_FORMAT = 'train_step'
FWD_PARAMS = ['x', 'norm_w', 'w_in', 'b_gate', 'conv_a', 'conv_c', 'a_log', 'dt_bias', 'lower_bounds', 'hgrn_norm_w', 'gdn_norm_w', 'w_out_a', 'w_out_b', 'w_out_c', 'w_o', 'final_norm_w']
TWIN_WEIGHTS = ['norm_w', 'w_in', 'b_gate', 'conv_a', 'conv_c', 'a_log', 'dt_bias', 'lower_bounds', 'hgrn_norm_w', 'gdn_norm_w', 'w_out_a', 'w_out_b', 'w_out_c', 'w_o', 'final_norm_w']
TWIN_DIFF_INPUT = 'x'
TWIN_INPUTS = ['x', 'norm_w', 'w_in', 'b_gate', 'conv_a', 'conv_c', 'a_log', 'dt_bias', 'lower_bounds', 'hgrn_norm_w', 'gdn_norm_w', 'w_out_a', 'w_out_b', 'w_out_c', 'w_o', 'final_norm_w', 'loss_target', 'm_norm_w', 'm_w_in', 'm_b_gate', 'm_conv_a', 'm_conv_c', 'm_a_log', 'm_dt_bias', 'm_lower_bounds', 'm_hgrn_norm_w', 'm_gdn_norm_w', 'm_w_out_a', 'm_w_out_b', 'm_w_out_c', 'm_w_o', 'm_final_norm_w', 'v_norm_w', 'v_w_in', 'v_b_gate', 'v_conv_a', 'v_conv_c', 'v_a_log', 'v_dt_bias', 'v_lower_bounds', 'v_hgrn_norm_w', 'v_gdn_norm_w', 'v_w_out_a', 'v_w_out_b', 'v_w_out_c', 'v_w_o', 'v_final_norm_w']
TWIN_OUTPUTS = ['loss', 'grad_x', 'grad_norm_w', 'grad_w_in', 'grad_b_gate', 'grad_conv_a', 'grad_conv_c', 'grad_a_log', 'grad_dt_bias', 'grad_lower_bounds', 'grad_hgrn_norm_w', 'grad_gdn_norm_w', 'grad_w_out_a', 'grad_w_out_b', 'grad_w_out_c', 'grad_w_o', 'grad_final_norm_w', 'delta_norm_w', 'delta_w_in', 'delta_b_gate', 'delta_conv_a', 'delta_conv_c', 'delta_a_log', 'delta_dt_bias', 'delta_lower_bounds', 'delta_hgrn_norm_w', 'delta_gdn_norm_w', 'delta_w_out_a', 'delta_w_out_b', 'delta_w_out_c', 'delta_w_o', 'delta_final_norm_w', 'new_m_norm_w', 'new_m_w_in', 'new_m_b_gate', 'new_m_conv_a', 'new_m_conv_c', 'new_m_a_log', 'new_m_dt_bias', 'new_m_lower_bounds', 'new_m_hgrn_norm_w', 'new_m_gdn_norm_w', 'new_m_w_out_a', 'new_m_w_out_b', 'new_m_w_out_c', 'new_m_w_o', 'new_m_final_norm_w', 'new_v_norm_w', 'new_v_w_in', 'new_v_b_gate', 'new_v_conv_a', 'new_v_conv_c', 'new_v_a_log', 'new_v_dt_bias', 'new_v_lower_bounds', 'new_v_hgrn_norm_w', 'new_v_gdn_norm_w', 'new_v_w_out_a', 'new_v_w_out_b', 'new_v_w_out_c', 'new_v_w_o', 'new_v_final_norm_w']
TWIN_LEAF_KINDS = {'loss': 'loss', 'grad_x': 'grad_x', 'grad_norm_w': 'grad_w', 'grad_w_in': 'grad_w', 'grad_b_gate': 'grad_w', 'grad_conv_a': 'grad_w', 'grad_conv_c': 'grad_w', 'grad_a_log': 'grad_w', 'grad_dt_bias': 'grad_w', 'grad_lower_bounds': 'grad_w', 'grad_hgrn_norm_w': 'grad_w', 'grad_gdn_norm_w': 'grad_w', 'grad_w_out_a': 'grad_w', 'grad_w_out_b': 'grad_w', 'grad_w_out_c': 'grad_w', 'grad_w_o': 'grad_w', 'grad_final_norm_w': 'grad_w', 'delta_norm_w': 'delta_w', 'delta_w_in': 'delta_w', 'delta_b_gate': 'delta_w', 'delta_conv_a': 'delta_w', 'delta_conv_c': 'delta_w', 'delta_a_log': 'delta_w', 'delta_dt_bias': 'delta_w', 'delta_lower_bounds': 'delta_w', 'delta_hgrn_norm_w': 'delta_w', 'delta_gdn_norm_w': 'delta_w', 'delta_w_out_a': 'delta_w', 'delta_w_out_b': 'delta_w', 'delta_w_out_c': 'delta_w', 'delta_w_o': 'delta_w', 'delta_final_norm_w': 'delta_w', 'new_m_norm_w': 'new_m', 'new_m_w_in': 'new_m', 'new_m_b_gate': 'new_m', 'new_m_conv_a': 'new_m', 'new_m_conv_c': 'new_m', 'new_m_a_log': 'new_m', 'new_m_dt_bias': 'new_m', 'new_m_lower_bounds': 'new_m', 'new_m_hgrn_norm_w': 'new_m', 'new_m_gdn_norm_w': 'new_m', 'new_m_w_out_a': 'new_m', 'new_m_w_out_b': 'new_m', 'new_m_w_out_c': 'new_m', 'new_m_w_o': 'new_m', 'new_m_final_norm_w': 'new_m', 'new_v_norm_w': 'new_v', 'new_v_w_in': 'new_v', 'new_v_b_gate': 'new_v', 'new_v_conv_a': 'new_v', 'new_v_conv_c': 'new_v', 'new_v_a_log': 'new_v', 'new_v_dt_bias': 'new_v', 'new_v_lower_bounds': 'new_v', 'new_v_hgrn_norm_w': 'new_v', 'new_v_gdn_norm_w': 'new_v', 'new_v_w_out_a': 'new_v', 'new_v_w_out_b': 'new_v', 'new_v_w_out_c': 'new_v', 'new_v_w_o': 'new_v', 'new_v_final_norm_w': 'new_v'}


def _forward(args):
    return _fwd_reference(*[args[k] for k in FWD_PARAMS])


def _output_shape():
    def fwd():
        inp = _fwd_setup_inputs(0)
        return _fwd_reference(*[inp[k] for k in FWD_PARAMS])
    out = _jax.eval_shape(fwd)
    return out.shape, out.dtype

N_MICROBATCH = 1
ADAM_LR = 0.001
ADAM_B1 = 0.9
ADAM_B2 = 0.999
ADAM_EPS = 1e-08
ADAM_WD = 0.01
ADAM_STEP = 10
PER_EXAMPLE_BATCH_AXIS = {'x': 0, 'loss_target': 0}
SHARED_INPUTS = []
_WEIGHT_DTYPES = {'norm_w': _jnp.float32, 'w_in': _jnp.float32, 'b_gate': _jnp.float32, 'conv_a': _jnp.float32, 'conv_c': _jnp.float32, 'a_log': _jnp.float32, 'dt_bias': _jnp.float32, 'lower_bounds': _jnp.float32, 'hgrn_norm_w': _jnp.float32, 'gdn_norm_w': _jnp.float32, 'w_out_a': _jnp.float32, 'w_out_b': _jnp.float32, 'w_out_c': _jnp.float32, 'w_o': _jnp.float32, 'final_norm_w': _jnp.float32}
MOMENT_SCALE = {'norm_w': 2.469837e-01, 'w_in': 7.265338e-02, 'b_gate': 2.890276e-02, 'conv_a': 1.107782e-01, 'conv_c': 7.005755e-02, 'a_log': 2.525346e-01, 'dt_bias': 2.450193e-01, 'lower_bounds': 8.306158e-03, 'hgrn_norm_w': 2.328576e-01, 'gdn_norm_w': 2.139783e-01, 'w_out_a': 7.443412e-02, 'w_out_b': 7.512135e-02, 'w_out_c': 7.470992e-02, 'w_o': 1.297881e-01, 'final_norm_w': 6.397428e+01}


def _to_microbatches(a, axis):
    t = _jnp.moveaxis(a, axis, 0)
    t = t.reshape((N_MICROBATCH, t.shape[0] // N_MICROBATCH) + t.shape[1:])
    return _jnp.moveaxis(t, 1, axis + 1)


def setup_inputs(seed: int = 0) -> dict:
    inp = _fwd_setup_inputs(seed)
    key = _jax.random.fold_in(_jax.random.key(seed), 7919)
    shape, _ = _output_shape()
    out = dict(inp)
    out["loss_target"] = _jax.random.normal(_jax.random.fold_in(key, 0), shape, _jnp.float32)
    for i, name in enumerate(TWIN_WEIGHTS):
        w = inp[name].astype(_jnp.float32)
        if MOMENT_SCALE is None:
            s = _jnp.sqrt(_jnp.mean(_jnp.square(w)) + 1e-30)
        else:
            s = MOMENT_SCALE[name]
        km, kv = _jax.random.split(_jax.random.fold_in(key, i + 1))
        out[name] = w
        out["m_" + name] = s * _jax.random.normal(km, w.shape, _jnp.float32)
        out["v_" + name] = (s * s) * _jax.random.uniform(kv, w.shape, _jnp.float32, 0.5, 1.5)
    if N_MICROBATCH > 1:
        for name, axis in PER_EXAMPLE_BATCH_AXIS.items():
            out[name] = _to_microbatches(out[name], axis)
    return {'x': out['x'], 'norm_w': out['norm_w'], 'w_in': out['w_in'], 'b_gate': out['b_gate'], 'conv_a': out['conv_a'], 'conv_c': out['conv_c'], 'a_log': out['a_log'], 'dt_bias': out['dt_bias'], 'lower_bounds': out['lower_bounds'], 'hgrn_norm_w': out['hgrn_norm_w'], 'gdn_norm_w': out['gdn_norm_w'], 'w_out_a': out['w_out_a'], 'w_out_b': out['w_out_b'], 'w_out_c': out['w_out_c'], 'w_o': out['w_o'], 'final_norm_w': out['final_norm_w'], 'loss_target': out['loss_target'], 'm_norm_w': out['m_norm_w'], 'm_w_in': out['m_w_in'], 'm_b_gate': out['m_b_gate'], 'm_conv_a': out['m_conv_a'], 'm_conv_c': out['m_conv_c'], 'm_a_log': out['m_a_log'], 'm_dt_bias': out['m_dt_bias'], 'm_lower_bounds': out['m_lower_bounds'], 'm_hgrn_norm_w': out['m_hgrn_norm_w'], 'm_gdn_norm_w': out['m_gdn_norm_w'], 'm_w_out_a': out['m_w_out_a'], 'm_w_out_b': out['m_w_out_b'], 'm_w_out_c': out['m_w_out_c'], 'm_w_o': out['m_w_o'], 'm_final_norm_w': out['m_final_norm_w'], 'v_norm_w': out['v_norm_w'], 'v_w_in': out['v_w_in'], 'v_b_gate': out['v_b_gate'], 'v_conv_a': out['v_conv_a'], 'v_conv_c': out['v_conv_c'], 'v_a_log': out['v_a_log'], 'v_dt_bias': out['v_dt_bias'], 'v_lower_bounds': out['v_lower_bounds'], 'v_hgrn_norm_w': out['v_hgrn_norm_w'], 'v_gdn_norm_w': out['v_gdn_norm_w'], 'v_w_out_a': out['v_w_out_a'], 'v_w_out_b': out['v_w_out_b'], 'v_w_out_c': out['v_w_out_c'], 'v_w_o': out['v_w_o'], 'v_final_norm_w': out['v_final_norm_w']}


def _loss(weights, diff, rest, loss_target):
    with _jax.named_scope("forward"):
        args = {**rest, TWIN_DIFF_INPUT: diff, **{k: w.astype(_WEIGHT_DTYPES[k]) for k, w in weights.items()}}
        y = _forward(args)
    with _jax.named_scope("loss_head"):
        err = _jnp.square(y.astype(_jnp.float32) - loss_target)
        return 0.5 * _jnp.sum(_jnp.mean(err, axis=-1)) if err.ndim else 0.5 * err


def _adamw(w, g, m, v):
    m = ADAM_B1 * m + (1.0 - ADAM_B1) * g
    v = ADAM_B2 * v + (1.0 - ADAM_B2) * _jnp.square(g)
    m_hat = m / (1.0 - ADAM_B1 ** ADAM_STEP)
    v_hat = v / (1.0 - ADAM_B2 ** ADAM_STEP)
    delta = -ADAM_LR * (m_hat / (_jnp.sqrt(v_hat) + ADAM_EPS) + ADAM_WD * w)
    return delta, m, v


def reference(x, norm_w, w_in, b_gate, conv_a, conv_c, a_log, dt_bias, lower_bounds, hgrn_norm_w, gdn_norm_w, w_out_a, w_out_b, w_out_c, w_o, final_norm_w, loss_target, m_norm_w, m_w_in, m_b_gate, m_conv_a, m_conv_c, m_a_log, m_dt_bias, m_lower_bounds, m_hgrn_norm_w, m_gdn_norm_w, m_w_out_a, m_w_out_b, m_w_out_c, m_w_o, m_final_norm_w, v_norm_w, v_w_in, v_b_gate, v_conv_a, v_conv_c, v_a_log, v_dt_bias, v_lower_bounds, v_hgrn_norm_w, v_gdn_norm_w, v_w_out_a, v_w_out_b, v_w_out_c, v_w_o, v_final_norm_w):
    given = dict(x=x, norm_w=norm_w, w_in=w_in, b_gate=b_gate, conv_a=conv_a, conv_c=conv_c, a_log=a_log, dt_bias=dt_bias, lower_bounds=lower_bounds, hgrn_norm_w=hgrn_norm_w, gdn_norm_w=gdn_norm_w, w_out_a=w_out_a, w_out_b=w_out_b, w_out_c=w_out_c, w_o=w_o, final_norm_w=final_norm_w, loss_target=loss_target, m_norm_w=m_norm_w, m_w_in=m_w_in, m_b_gate=m_b_gate, m_conv_a=m_conv_a, m_conv_c=m_conv_c, m_a_log=m_a_log, m_dt_bias=m_dt_bias, m_lower_bounds=m_lower_bounds, m_hgrn_norm_w=m_hgrn_norm_w, m_gdn_norm_w=m_gdn_norm_w, m_w_out_a=m_w_out_a, m_w_out_b=m_w_out_b, m_w_out_c=m_w_out_c, m_w_o=m_w_o, m_final_norm_w=m_final_norm_w, v_norm_w=v_norm_w, v_w_in=v_w_in, v_b_gate=v_b_gate, v_conv_a=v_conv_a, v_conv_c=v_conv_c, v_a_log=v_a_log, v_dt_bias=v_dt_bias, v_lower_bounds=v_lower_bounds, v_hgrn_norm_w=v_hgrn_norm_w, v_gdn_norm_w=v_gdn_norm_w, v_w_out_a=v_w_out_a, v_w_out_b=v_w_out_b, v_w_out_c=v_w_out_c, v_w_o=v_w_o, v_final_norm_w=v_final_norm_w)
    weights = {n: given[n] for n in TWIN_WEIGHTS}
    shared = {n: given[n] for n in SHARED_INPUTS}
    per_example = {n: given[n] for n in ['x']}
    grad_fn = _jax.value_and_grad(_loss, argnums=(0, 1))

    def one_microbatch(ex, loss_target):
        ex = dict(ex)
        diff = ex.pop(TWIN_DIFF_INPUT)
        return grad_fn(weights, diff, {**shared, **ex}, loss_target)

    if N_MICROBATCH == 1:
        loss, (grad_w, grad_x) = one_microbatch(per_example, given["loss_target"])
    else:
        def body(carry, xs):
            loss_sum, grad_sum = carry
            l_k, (gw_k, gx_k) = one_microbatch(xs[0], xs[1])
            with _jax.named_scope("update"):
                return (loss_sum + l_k, _jax.tree.map(_jnp.add, grad_sum, gw_k)), gx_k

        init = (_jnp.zeros((), _jnp.float32), _jax.tree.map(_jnp.zeros_like, weights))
        (loss, grad_w), grad_x = _jax.lax.scan(body, init, (per_example, given["loss_target"]))
    with _jax.named_scope("update"):
        delta_w, new_m, new_v = {}, {}, {}
        for n in TWIN_WEIGHTS:
            delta_w[n], new_m[n], new_v[n] = _adamw(weights[n], grad_w[n], given["m_" + n], given["v_" + n])
    return (loss, grad_x, *[grad_w[n] for n in TWIN_WEIGHTS], *[delta_w[n] for n in TWIN_WEIGHTS],
            *[new_m[n] for n in TWIN_WEIGHTS], *[new_v[n] for n in TWIN_WEIGHTS])
```

```python
import functools

import jax
import jax.numpy as jnp
from jax import lax
from jax.experimental import pallas as pl
from jax.experimental.pallas import tpu as pltpu

F32 = jnp.float32
BF16 = jnp.bfloat16
HI = lax.Precision.HIGHEST
MESH = pl.DeviceIdType.MESH

N_DEV = 8
D = 1024
DEPTH = 2
CHUNK = 64
SUB = 16
NORM_EPS = 1e-6
L2_EPS = 1e-6
MIN_F = 1e-30
HD = 128
HGRN_HEADS = 4
GDN_QK_HEADS = 4
CONV_W = 512
IN_COLS = 10256
OFF_A, OFF_B, OFF_CQ, OFF_CK, OFF_CV, OFF_BETA, OFF_CA, OFF_CZ, OFF_G = (
    0, 2048, 4096, 4608, 5120, 6144, 6152, 6160, 7184)
NA, NB, NC_COLS, NG = 2048, 2048, 3584, 3072
C_HEAD = 896

ADAM_LR, ADAM_B1, ADAM_B2, ADAM_EPS, ADAM_WD, ADAM_STEP = 0.001, 0.9, 0.999, 1e-08, 0.01, 10

VMEM_LIMIT = 56 * 1024 * 1024


def _cparams(*sem):
    return pltpu.CompilerParams(dimension_semantics=sem, vmem_limit_bytes=VMEM_LIMIT)


def _tile(dim, cap):
    if dim <= cap:
        return dim
    t = (cap // 128) * 128
    while dim % t:
        t -= 128
    return t


def _sigmoid(x):
    return 1.0 / (1.0 + jnp.exp(-x))


def _silu(x):
    return x * _sigmoid(x)


def _softplus(x):
    return jnp.maximum(x, 0.0) + jnp.log(1.0 + jnp.exp(-jnp.abs(x)))


def _dot(a, b, dims, precision=None):
    if precision is None:
        a, b = a.astype(BF16), b.astype(BF16)
    return lax.dot_general(a, b, (dims, ((), ())), precision=precision, preferred_element_type=F32)


def _nn(a, b, precision=None):
    return _dot(a, b, ((1,), (0,)), precision)


def _nt(a, b, precision=None):
    return _dot(a, b, ((1,), (1,)), precision)


def _tn(a, b, precision=None):
    return _dot(a, b, ((0,), (0,)), precision)


@functools.partial(jax.custom_vjp, nondiff_argnums=(1,))
def _shift_rows(x, d):
    return x if d == 0 else pltpu.roll(x, d, 0)


def _shift_rows_fwd(x, d):
    return _shift_rows(x, d), None


def _shift_rows_bwd(d, _, ct):
    return ((ct if d == 0 else pltpu.roll(ct, ct.shape[0] - d, 0)),)


_shift_rows.defvjp(_shift_rows_fwd, _shift_rows_bwd)


def _iota2(shape):
    return lax.broadcasted_iota(jnp.int32, shape, 0), lax.broadcasted_iota(jnp.int32, shape, 1)


def _lane_pick(x, i):
    lane = lax.broadcasted_iota(jnp.int32, x.shape, 1)
    return jnp.sum(jnp.where(lane == i, x, 0.0), axis=1, keepdims=True)


def _hgrn_chunk(qr, fr, ir, zr, st0, lb, nw):
    r, c = _iota2((CHUNK, CHUNK))
    sr, sc = r // SUB, c // SUB
    mats = [c <= r,
            (c <= r) & (sr == sc),
            (c > r) & (sr == sc),
            c > r,
            sc == sr - 1,
            (sc == sr - 1) | (sc == sr - 2)]
    stack = jnp.concatenate([m.astype(F32) for m in mats], axis=0)

    q = _silu(qr) * (HD ** -0.5)
    fg = lb + (1.0 - lb) * _sigmoid(fr)
    logf = jnp.log(jnp.maximum(fg, MIN_F))
    kk = 1.0 - fg
    v = ir

    cs = _nn(stack, logf, HI)
    g, gl, gr, grc, c2, c3 = [cs[i * CHUNK:(i + 1) * CHUNK] for i in range(6)]
    g_last = jnp.sum(logf, axis=0, keepdims=True)

    o = _nt(q * jnp.exp(g), st0)
    qt = q * jnp.exp(gl)
    kh = kk * jnp.exp(gr)
    dist = sr - sc
    scores = jnp.where(dist == 1, _nt(qt, kh), 0.0)
    scores += jnp.where(dist == 2, _nt(qt * jnp.exp(c2), kh), 0.0)
    scores += jnp.where(dist == 3, _nt(qt * jnp.exp(c3), kh), 0.0)
    o += _nn(scores, v)
    row = lax.broadcasted_iota(jnp.int32, (CHUNK, HD), 0) % SUB
    for d in range(SUB):
        ok = row >= d
        diff = gl - _shift_rows(gl, d)
        dec = jnp.where(ok, jnp.exp(jnp.where(ok, diff, 0.0)), 0.0)
        term = jnp.sum(q * dec * _shift_rows(kk, d), axis=1, keepdims=True)
        o += term * _shift_rows(v, d)

    st1 = st0 * jnp.exp(g_last) + _tn(v, kk * jnp.exp(grc))
    y = o * lax.rsqrt(jnp.mean(o * o, axis=1, keepdims=True) + NORM_EPS) * nw * _silu(zr)
    return y, st1


@jax.custom_vjp
def _unit_lower_inverse(m):
    r, c = _iota2((CHUNK, CHUNK))
    x = jnp.where(r == c, 1.0, 0.0) - jnp.where((r // 2) == (c // 2), m, 0.0)
    b = 2
    while b < CHUNK:
        off = jnp.where(((r // (2 * b)) == (c // (2 * b))) & ((r // b) != (c // b)), m, 0.0)
        x = x - _nn(_nn(x, off, HI), x, HI)
        b *= 2
    return x


def _unit_lower_inverse_fwd(m):
    x = _unit_lower_inverse(m)
    return x, x


def _unit_lower_inverse_bwd(x, ct):
    r, c = _iota2((CHUNK, CHUNK))
    dm = -_nt(_tn(x, ct, HI), x, HI)
    return (jnp.where(c < r, dm, 0.0),)


_unit_lower_inverse.defvjp(_unit_lower_inverse_fwd, _unit_lower_inverse_bwd)


def _gdn_chunk(x_ext, z, ba, s0a, s0b, w0, w1, w2, w3, alog, dtb, nw):
    conv = (w0 * _shift_rows(x_ext, 3) + w1 * _shift_rows(x_ext, 2) + w2 * _shift_rows(x_ext, 1) + w3 * x_ext)
    cc = _silu(conv[8:])
    qc, kc = cc[:, 0:HD], cc[:, HD:2 * HD]
    q = qc * lax.rsqrt(jnp.sum(qc * qc, axis=1, keepdims=True) + L2_EPS) * (HD ** -0.5)
    k = kc * lax.rsqrt(jnp.sum(kc * kc, axis=1, keepdims=True) + L2_EPS)

    r, c = _iota2((CHUNK, CHUNK))
    tri = (c <= r).astype(F32)
    rem = (c > r).astype(F32)
    eye = (c == r).astype(F32)
    causal, strict = c <= r, c < r

    ys, states = [], []
    for i, s0 in enumerate((s0a, s0b)):
        v = cc[:, (2 + i) * HD:(3 + i) * HD]
        beta = _sigmoid(_lane_pick(ba, i))
        a_neg = -jnp.exp(_lane_pick(alog, i))
        loga = a_neg * _softplus(_lane_pick(ba, 2 + i) + _lane_pick(dtb, i))
        g_sq = _nn(tri, jnp.broadcast_to(loga, (CHUNK, CHUNK)), HI)
        g_w = _nn(tri, jnp.broadcast_to(loga, (CHUNK, HD)), HI)
        g_rem = _nn(rem, jnp.broadcast_to(loga, (CHUNK, HD)), HI)
        g_last = jnp.sum(loga, axis=0, keepdims=True)
        g_row = jnp.sum(eye * g_sq, axis=0, keepdims=True)
        diff = g_sq - g_row
        dec_c = jnp.where(causal, jnp.exp(jnp.where(causal, diff, 0.0)), 0.0)
        dec_s = jnp.where(strict, dec_c, 0.0)

        kb = k * beta
        x = _unit_lower_inverse(_nt(k, kb) * dec_s)
        u = _nn(x, v)
        w = _nn(x, k * jnp.exp(g_w))
        e = u - _nn(w, s0)
        o = _nn(q * jnp.exp(g_w), s0) + _nn(_nt(q, kb) * dec_c, e)
        states.append(s0 * jnp.exp(g_last) + _tn(kb * jnp.exp(g_rem), e))
        zi = z[:, i * HD:(i + 1) * HD]
        ys.append(o * lax.rsqrt(jnp.mean(o * o, axis=1, keepdims=True) + NORM_EPS) * nw * _silu(zi))
    return jnp.concatenate(ys, axis=1), states[0], states[1]


def _conv_a_block(ab, ac_ext, ax_ext, az, w0, w1, w2):
    u = ac_ext * ax_ext
    conv = (w0 * _shift_rows(u, 2) + w1 * _shift_rows(u, 1) + w2 * u)[8:]
    return ab * conv * _silu(az)


def _matmul(a, b, mode, name, residual=None, out_dtype=F32):
    if mode == "nn":
        (m, k), n = a.shape, b.shape[1]
    elif mode == "nt":
        (m, k), n = a.shape, b.shape[0]
    else:
        (k, m), n = a.shape, b.shape[1]
    tm, tn, tk = _tile(m, 512 if mode != "tn" else 1024), _tile(n, 1024), _tile(k, 1024 if mode != "tn" else 512)
    nk = k // tk
    dims = {"nn": ((1,), (0,)), "nt": ((1,), (1,)), "tn": ((0,), (0,))}[mode]
    a_spec = pl.BlockSpec((tk, tm), lambda i, j, s: (s, i)) if mode == "tn" else pl.BlockSpec((tm, tk), lambda i, j, s: (i, s))
    b_spec = pl.BlockSpec((tn, tk), lambda i, j, s: (j, s)) if mode == "nt" else pl.BlockSpec((tk, tn), lambda i, j, s: (s, j))
    o_spec = pl.BlockSpec((tm, tn), lambda i, j, s: (i, j))
    has_res = residual is not None

    def body(*refs):
        a_ref, b_ref = refs[0], refs[1]
        r_ref = refs[2] if has_res else None
        o_ref, acc_ref = refs[-2], refs[-1]
        s = pl.program_id(2)

        @pl.when(s == 0)
        def _():
            acc_ref[...] = jnp.zeros_like(acc_ref)

        acc_ref[...] += _dot(a_ref[...], b_ref[...], dims)

        @pl.when(s == nk - 1)
        def _():
            out = acc_ref[...]
            if has_res:
                out = out + r_ref[...]
            o_ref[...] = out.astype(out_dtype)

    args, specs = [a, b], [a_spec, b_spec]
    if has_res:
        args.append(residual)
        specs.append(o_spec)
    return pl.pallas_call(
        body, name=name, grid=(m // tm, n // tn, nk), in_specs=specs, out_specs=o_spec,
        out_shape=jax.ShapeDtypeStruct((m, n), out_dtype), scratch_shapes=[pltpu.VMEM((tm, tn), F32)],
        compiler_params=_cparams("parallel", "parallel", "arbitrary"))(*args)


def _rmsnorm_fwd(x, w, name):
    t = x.shape[0]
    blk = _tile(t, 512)

    def body(x_ref, w_ref, h_ref):
        xv = x_ref[...]
        h_ref[...] = (xv * lax.rsqrt(jnp.mean(xv * xv, axis=1, keepdims=True) + NORM_EPS) * w_ref[...]).astype(BF16)

    return pl.pallas_call(
        body, name=name, grid=(t // blk,),
        in_specs=[pl.BlockSpec((blk, D), lambda i: (i, 0)), pl.BlockSpec((1, D), lambda i: (0, 0))],
        out_specs=pl.BlockSpec((blk, D), lambda i: (i, 0)), out_shape=jax.ShapeDtypeStruct((t, D), BF16),
        compiler_params=_cparams("parallel"))(x, w)


def _rmsnorm_bwd(dh, x, w, dxo, name):
    t = x.shape[0]
    blk = _tile(t, 512)

    def body(dh_ref, x_ref, w_ref, dxo_ref, dx_ref, dw_ref):
        @pl.when(pl.program_id(0) == 0)
        def _():
            dw_ref[...] = jnp.zeros_like(dw_ref)

        xv, dhv = x_ref[...], dh_ref[...]
        rs = lax.rsqrt(jnp.mean(xv * xv, axis=1, keepdims=True) + NORM_EPS)
        xh = xv * rs
        dw_ref[...] += jnp.sum(dhv * xh, axis=0, keepdims=True)
        dxh = dhv * w_ref[...]
        dx_ref[...] = rs * (dxh - xh * jnp.mean(dxh * xh, axis=1, keepdims=True)) + dxo_ref[...]

    row = pl.BlockSpec((blk, D), lambda i: (i, 0))
    vec = pl.BlockSpec((1, D), lambda i: (0, 0))
    return pl.pallas_call(
        body, name=name, grid=(t // blk,), in_specs=[row, row, vec, row], out_specs=[row, vec],
        out_shape=[jax.ShapeDtypeStruct((t, D), F32), jax.ShapeDtypeStruct((1, D), F32)],
        compiler_params=_cparams("arbitrary"))(dh, x, w, dxo)


def _loss_head(x, w, target, name):
    t = x.shape[0]
    blk = _tile(t, 512)

    def body(x_ref, w_ref, t_ref, loss_ref, dx_ref, dw_ref):
        @pl.when(pl.program_id(0) == 0)
        def _():
            dw_ref[...] = jnp.zeros_like(dw_ref)
            loss_ref[...] = jnp.zeros_like(loss_ref)

        xv = x_ref[...]
        rs = lax.rsqrt(jnp.mean(xv * xv, axis=1, keepdims=True) + NORM_EPS)
        xh = xv * rs
        err = xh * w_ref[...] - t_ref[...]
        loss_ref[...] += 0.5 * jnp.sum(jnp.mean(err * err, axis=1, keepdims=True), axis=0, keepdims=True)
        dy = err * (1.0 / D)
        dw_ref[...] += jnp.sum(dy * xh, axis=0, keepdims=True)
        dxh = dy * w_ref[...]
        dx_ref[...] = rs * (dxh - xh * jnp.mean(dxh * xh, axis=1, keepdims=True))

    row = pl.BlockSpec((blk, D), lambda i: (i, 0))
    vec = pl.BlockSpec((1, D), lambda i: (0, 0))
    return pl.pallas_call(
        body, name=name, grid=(t // blk,), in_specs=[row, vec, row],
        out_specs=[pl.BlockSpec((1, 1), lambda i: (0, 0)), row, vec],
        out_shape=[jax.ShapeDtypeStruct((1, 1), F32), jax.ShapeDtypeStruct((t, D), F32), jax.ShapeDtypeStruct((1, D), F32)],
        compiler_params=_cparams("arbitrary"))(x, w, target)


def _lbs_of(lb):
    r = lax.broadcasted_iota(jnp.int32, lb.shape, 0)
    real = r < DEPTH
    mx = lax.stop_gradient(jnp.max(jnp.where(real, lb, -jnp.inf), axis=0, keepdims=True))
    e = jnp.where(real, jnp.exp(jnp.where(real, lb - mx, 0.0)), 0.0)
    p = e / jnp.sum(e, axis=0, keepdims=True)
    out = jnp.zeros_like(lb)
    run = jnp.zeros_like(mx)
    for l in range(1, DEPTH):
        run = run + jnp.sum(jnp.where(r == l, p, 0.0), axis=0, keepdims=True)
        out = out + jnp.where(r == l, run, 0.0)
    return out


def _lower_bounds_fwd(lbp, name):
    def body(lb_ref, o_ref):
        o_ref[...] = _lbs_of(lb_ref[...])

    return pl.pallas_call(body, name=name, out_shape=jax.ShapeDtypeStruct(lbp.shape, F32))(lbp)


def _lower_bounds_bwd(lbp, dlbs, name):
    def body(lb_ref, d_ref, o_ref):
        _, vjp = jax.vjp(_lbs_of, lb_ref[...])
        o_ref[...] = vjp(d_ref[...])[0]

    return pl.pallas_call(body, name=name, out_shape=jax.ShapeDtypeStruct(lbp.shape, F32))(lbp, dlbs)


def _branch_a_fwd(pa, cw, name):
    t = pa.shape[0]
    blk = _tile(t, 512)
    W = CONV_W

    def body(p_ref, w_ref, y_ref, hc_ref, hx_ref):
        @pl.when(pl.program_id(0) == 0)
        def _():
            hc_ref[...] = jnp.zeros_like(hc_ref)
            hx_ref[...] = jnp.zeros_like(hx_ref)

        ac, ax = p_ref[:, W:2 * W], p_ref[:, 2 * W:3 * W]
        y_ref[...] = _conv_a_block(
            p_ref[:, 0:W], jnp.concatenate([hc_ref[...], ac], axis=0), jnp.concatenate([hx_ref[...], ax], axis=0),
            p_ref[:, 3 * W:4 * W], w_ref[0:1, :], w_ref[1:2, :], w_ref[2:3, :])
        hc_ref[...] = p_ref[blk - 8:blk, W:2 * W]
        hx_ref[...] = p_ref[blk - 8:blk, 2 * W:3 * W]

    return pl.pallas_call(
        body, name=name, grid=(t // blk,),
        in_specs=[pl.BlockSpec((blk, NA), lambda i: (i, 0)), pl.BlockSpec((3, W), lambda i: (0, 0))],
        out_specs=pl.BlockSpec((blk, W), lambda i: (i, 0)), out_shape=jax.ShapeDtypeStruct((t, W), F32),
        scratch_shapes=[pltpu.VMEM((8, W), F32), pltpu.VMEM((8, W), F32)],
        compiler_params=_cparams("arbitrary"))(pa, cw)


def _branch_a_bwd(pa, cw, dy, name):
    t = pa.shape[0]
    blk = _tile(t, 512)
    nt_ = t // blk
    W = CONV_W
    hb = blk // 8

    def body(p_ref, halo_ref, w_ref, dy_ref, dp_ref, dw_ref, chc_ref, chx_ref):
        i = pl.program_id(0)

        @pl.when(i == 0)
        def _():
            chc_ref[...] = jnp.zeros_like(chc_ref)
            chx_ref[...] = jnp.zeros_like(chx_ref)
            dw_ref[...] = jnp.zeros_like(dw_ref)

        keep = 1.0 - (i == nt_ - 1).astype(F32)
        hc = halo_ref[:, W:2 * W] * keep
        hx = halo_ref[:, 2 * W:3 * W] * keep
        ac_ext = jnp.concatenate([hc, p_ref[:, W:2 * W]], axis=0)
        ax_ext = jnp.concatenate([hx, p_ref[:, 2 * W:3 * W]], axis=0)
        _, vjp = jax.vjp(_conv_a_block, p_ref[:, 0:W], ac_ext, ax_ext, p_ref[:, 3 * W:4 * W],
                         w_ref[0:1, :], w_ref[1:2, :], w_ref[2:3, :])
        dab, dac, dax, daz, dw0, dw1, dw2 = vjp(dy_ref[...])
        dp_ref[:, 0:W] = dab
        dp_ref[:, W:2 * W] = dac[8:]
        dp_ref[:, 2 * W:3 * W] = dax[8:]
        dp_ref[:, 3 * W:4 * W] = daz
        dp_ref[blk - 8:blk, W:2 * W] += chc_ref[...]
        dp_ref[blk - 8:blk, 2 * W:3 * W] += chx_ref[...]
        chc_ref[...] = dac[:8] * keep
        chx_ref[...] = dax[:8] * keep
        dw_ref[0:1, :] += dw0
        dw_ref[1:2, :] += dw1
        dw_ref[2:3, :] += dw2

    rev = lambda i: (nt_ - 1 - i, 0)
    return pl.pallas_call(
        body, name=name, grid=(nt_,),
        in_specs=[pl.BlockSpec((blk, NA), rev),
                  pl.BlockSpec((8, NA), lambda i: (jnp.maximum((nt_ - 1 - i) * hb - 1, 0), 0)),
                  pl.BlockSpec((3, W), lambda i: (0, 0)),
                  pl.BlockSpec((blk, W), rev)],
        out_specs=[pl.BlockSpec((blk, NA), rev), pl.BlockSpec((3, W), lambda i: (0, 0))],
        out_shape=[jax.ShapeDtypeStruct((t, NA), F32), jax.ShapeDtypeStruct((3, W), F32)],
        scratch_shapes=[pltpu.VMEM((8, W), F32), pltpu.VMEM((8, W), F32)],
        compiler_params=_cparams("arbitrary"))(pa, pa, cw, dy)


def _branch_b_fwd(pb, lbs_row, nw, name):
    t = pb.shape[0]
    nch = t // CHUNK

    def body(p_ref, lb_ref, nw_ref, y_ref, s_ref, st_ref):
        @pl.when(pl.program_id(1) == 0)
        def _():
            st_ref[...] = jnp.zeros_like(st_ref)

        s_ref[0, 0] = st_ref[...]
        y, st1 = _hgrn_chunk(p_ref[:, 0:HD], p_ref[:, HD:2 * HD], p_ref[:, 2 * HD:3 * HD], p_ref[:, 3 * HD:4 * HD],
                             st_ref[...], lb_ref[...], nw_ref[...])
        y_ref[...] = y
        st_ref[...] = st1

    return pl.pallas_call(
        body, name=name, grid=(HGRN_HEADS, nch),
        in_specs=[pl.BlockSpec((CHUNK, 4 * HD), lambda h, i: (i, h)),
                  pl.BlockSpec((1, HD), lambda h, i: (0, h)),
                  pl.BlockSpec((1, HD), lambda h, i: (0, 0))],
        out_specs=[pl.BlockSpec((CHUNK, HD), lambda h, i: (i, h)),
                   pl.BlockSpec((1, 1, HD, HD), lambda h, i: (h, i, 0, 0))],
        out_shape=[jax.ShapeDtypeStruct((t, HGRN_HEADS * HD), F32),
                   jax.ShapeDtypeStruct((HGRN_HEADS, nch, HD, HD), F32)],
        scratch_shapes=[pltpu.VMEM((HD, HD), F32)],
        compiler_params=_cparams("arbitrary", "arbitrary"))(pb, lbs_row, nw)


def _branch_b_bwd(pb, states, lbs_row, nw, dy, name):
    t = pb.shape[0]
    nch = t // CHUNK

    def body(p_ref, s_ref, lb_ref, nw_ref, dy_ref, dp_ref, dlb_ref, dnw_ref, ds_ref):
        h, i = pl.program_id(0), pl.program_id(1)

        @pl.when(i == 0)
        def _():
            ds_ref[...] = jnp.zeros_like(ds_ref)
            dlb_ref[...] = jnp.zeros_like(dlb_ref)

        @pl.when((i == 0) & (h == 0))
        def _():
            dnw_ref[...] = jnp.zeros_like(dnw_ref)

        _, vjp = jax.vjp(_hgrn_chunk, p_ref[:, 0:HD], p_ref[:, HD:2 * HD], p_ref[:, 2 * HD:3 * HD],
                         p_ref[:, 3 * HD:4 * HD], s_ref[0, 0], lb_ref[...], nw_ref[...])
        dq, df, di, dz, ds0, dlb, dnw = vjp((dy_ref[...], ds_ref[...]))
        dp_ref[:, 0:HD] = dq
        dp_ref[:, HD:2 * HD] = df
        dp_ref[:, 2 * HD:3 * HD] = di
        dp_ref[:, 3 * HD:4 * HD] = dz
        ds_ref[...] = ds0
        dlb_ref[...] += dlb
        dnw_ref[...] += dnw

    rev = lambda h, i: (nch - 1 - i, h)
    return pl.pallas_call(
        body, name=name, grid=(HGRN_HEADS, nch),
        in_specs=[pl.BlockSpec((CHUNK, 4 * HD), rev),
                  pl.BlockSpec((1, 1, HD, HD), lambda h, i: (h, nch - 1 - i, 0, 0)),
                  pl.BlockSpec((1, HD), lambda h, i: (0, h)),
                  pl.BlockSpec((1, HD), lambda h, i: (0, 0)),
                  pl.BlockSpec((CHUNK, HD), rev)],
        out_specs=[pl.BlockSpec((CHUNK, 4 * HD), rev),
                   pl.BlockSpec((1, HD), lambda h, i: (0, h)),
                   pl.BlockSpec((1, HD), lambda h, i: (0, 0))],
        out_shape=[jax.ShapeDtypeStruct((t, NB), F32), jax.ShapeDtypeStruct((1, HGRN_HEADS * HD), F32),
                   jax.ShapeDtypeStruct((1, HD), F32)],
        scratch_shapes=[pltpu.VMEM((HD, HD), F32)],
        compiler_params=_cparams("arbitrary", "arbitrary"))(pb, states, lbs_row, nw, dy)


def _branch_c_fwd(pc, cw, cpar, nw, name):
    t = pc.shape[0]
    nch = t // CHUNK
    XW = 4 * HD

    def body(p_ref, w_ref, cp_ref, nw_ref, y_ref, s_ref, sa_ref, sb_ref, halo_ref):
        @pl.when(pl.program_id(1) == 0)
        def _():
            sa_ref[...] = jnp.zeros_like(sa_ref)
            sb_ref[...] = jnp.zeros_like(sb_ref)
            halo_ref[...] = jnp.zeros_like(halo_ref)

        s_ref[0, 0, 0] = sa_ref[...]
        s_ref[0, 0, 1] = sb_ref[...]
        x_ext = jnp.concatenate([halo_ref[...], p_ref[:, 0:XW]], axis=0)
        y, s1a, s1b = _gdn_chunk(x_ext, p_ref[:, XW:XW + 2 * HD], p_ref[:, XW + 2 * HD:XW + 3 * HD],
                                 sa_ref[...], sb_ref[...], w_ref[0:1, :], w_ref[1:2, :], w_ref[2:3, :], w_ref[3:4, :],
                                 cp_ref[0, 0:1, :], cp_ref[0, 1:2, :], nw_ref[...])
        y_ref[...] = y
        sa_ref[...] = s1a
        sb_ref[...] = s1b
        halo_ref[...] = p_ref[CHUNK - 8:CHUNK, 0:XW]

    return pl.pallas_call(
        body, name=name, grid=(GDN_QK_HEADS, nch),
        in_specs=[pl.BlockSpec((CHUNK, C_HEAD), lambda h, i: (i, h)),
                  pl.BlockSpec((4, XW), lambda h, i: (0, h)),
                  pl.BlockSpec((1, 8, HD), lambda h, i: (h, 0, 0)),
                  pl.BlockSpec((1, HD), lambda h, i: (0, 0))],
        out_specs=[pl.BlockSpec((CHUNK, 2 * HD), lambda h, i: (i, h)),
                   pl.BlockSpec((1, 1, 2, HD, HD), lambda h, i: (h, i, 0, 0, 0))],
        out_shape=[jax.ShapeDtypeStruct((t, 2 * GDN_QK_HEADS * HD), F32),
                   jax.ShapeDtypeStruct((GDN_QK_HEADS, nch, 2, HD, HD), F32)],
        scratch_shapes=[pltpu.VMEM((HD, HD), F32), pltpu.VMEM((HD, HD), F32), pltpu.VMEM((8, XW), F32)],
        compiler_params=_cparams("arbitrary", "arbitrary"))(pc, cw, cpar, nw)


def _branch_c_bwd(pc, states, cw, cpar, nw, dy, name):
    t = pc.shape[0]
    nch = t // CHUNK
    XW = 4 * HD
    hb = CHUNK // 8

    def body(p_ref, halo_ref, s_ref, w_ref, cp_ref, nw_ref, dy_ref, dp_ref, dw_ref, dcp_ref, dnw_ref,
             dsa_ref, dsb_ref, carry_ref):
        h, i = pl.program_id(0), pl.program_id(1)

        @pl.when(i == 0)
        def _():
            dsa_ref[...] = jnp.zeros_like(dsa_ref)
            dsb_ref[...] = jnp.zeros_like(dsb_ref)
            carry_ref[...] = jnp.zeros_like(carry_ref)
            dw_ref[...] = jnp.zeros_like(dw_ref)
            dcp_ref[...] = jnp.zeros_like(dcp_ref)

        @pl.when((i == 0) & (h == 0))
        def _():
            dnw_ref[...] = jnp.zeros_like(dnw_ref)

        keep = 1.0 - (i == nch - 1).astype(F32)
        x_ext = jnp.concatenate([halo_ref[:, 0:XW] * keep, p_ref[:, 0:XW]], axis=0)
        _, vjp = jax.vjp(_gdn_chunk, x_ext, p_ref[:, XW:XW + 2 * HD], p_ref[:, XW + 2 * HD:XW + 3 * HD],
                         s_ref[0, 0, 0], s_ref[0, 0, 1], w_ref[0:1, :], w_ref[1:2, :], w_ref[2:3, :], w_ref[3:4, :],
                         cp_ref[0, 0:1, :], cp_ref[0, 1:2, :], nw_ref[...])
        dx, dz, dba, dsa, dsb, dw0, dw1, dw2, dw3, dal, ddt, dnw = vjp((dy_ref[...], dsa_ref[...], dsb_ref[...]))
        dp_ref[:, 0:XW] = dx[8:]
        dp_ref[:, XW:XW + 2 * HD] = dz
        dp_ref[:, XW + 2 * HD:XW + 3 * HD] = dba
        dp_ref[CHUNK - 8:CHUNK, 0:XW] += carry_ref[...]
        carry_ref[...] = dx[:8] * keep
        dsa_ref[...] = dsa
        dsb_ref[...] = dsb
        dw_ref[0:1, :] += dw0
        dw_ref[1:2, :] += dw1
        dw_ref[2:3, :] += dw2
        dw_ref[3:4, :] += dw3
        dcp_ref[0, 0:1, :] += dal
        dcp_ref[0, 1:2, :] += ddt
        dnw_ref[...] += dnw

    rev = lambda h, i: (nch - 1 - i, h)
    return pl.pallas_call(
        body, name=name, grid=(GDN_QK_HEADS, nch),
        in_specs=[pl.BlockSpec((CHUNK, C_HEAD), rev),
                  pl.BlockSpec((8, C_HEAD), lambda h, i: (jnp.maximum((nch - 1 - i) * hb - 1, 0), h)),
                  pl.BlockSpec((1, 1, 2, HD, HD), lambda h, i: (h, nch - 1 - i, 0, 0, 0)),
                  pl.BlockSpec((4, XW), lambda h, i: (0, h)),
                  pl.BlockSpec((1, 8, HD), lambda h, i: (h, 0, 0)),
                  pl.BlockSpec((1, HD), lambda h, i: (0, 0)),
                  pl.BlockSpec((CHUNK, 2 * HD), rev)],
        out_specs=[pl.BlockSpec((CHUNK, C_HEAD), rev),
                   pl.BlockSpec((4, XW), lambda h, i: (0, h)),
                   pl.BlockSpec((1, 8, HD), lambda h, i: (h, 0, 0)),
                   pl.BlockSpec((1, HD), lambda h, i: (0, 0))],
        out_shape=[jax.ShapeDtypeStruct((t, NC_COLS), F32), jax.ShapeDtypeStruct((4, GDN_QK_HEADS * XW), F32),
                   jax.ShapeDtypeStruct((GDN_QK_HEADS, 8, HD), F32), jax.ShapeDtypeStruct((1, HD), F32)],
        scratch_shapes=[pltpu.VMEM((HD, HD), F32), pltpu.VMEM((HD, HD), F32), pltpu.VMEM((8, XW), F32)],
        compiler_params=_cparams("arbitrary", "arbitrary"))(pc, pc, states, cw, cpar, nw, dy)


def _merge_fwd(pg, bg, ya, yb, yc, name):
    t = pg.shape[0]
    blk = _tile(t, 256)

    def body(g_ref, b_ref, a_ref, b2_ref, c_ref, o_ref):
        gate = _sigmoid(g_ref[...] + b_ref[...])
        o_ref[...] = gate[:, 0:D] * a_ref[...] + gate[:, D:2 * D] * b2_ref[...] + gate[:, 2 * D:3 * D] * c_ref[...]

    row = pl.BlockSpec((blk, D), lambda i: (i, 0))
    return pl.pallas_call(
        body, name=name, grid=(t // blk,),
        in_specs=[pl.BlockSpec((blk, NG), lambda i: (i, 0)), pl.BlockSpec((1, NG), lambda i: (0, 0)), row, row, row],
        out_specs=row, out_shape=jax.ShapeDtypeStruct((t, D), F32),
        compiler_params=_cparams("parallel"))(pg, bg, ya, yb, yc)


def _merge_bwd(dm, pg, bg, ya, yb, yc, name):
    t = pg.shape[0]
    blk = _tile(t, 256)

    def body(dm_ref, g_ref, b_ref, a_ref, b2_ref, c_ref, dg_ref, da_ref, db_ref, dc_ref, dbg_ref):
        @pl.when(pl.program_id(0) == 0)
        def _():
            dbg_ref[...] = jnp.zeros_like(dbg_ref)

        gate = _sigmoid(g_ref[...] + b_ref[...])
        dmv = dm_ref[...]
        for j, (y_ref, dy_ref) in enumerate(((a_ref, da_ref), (b2_ref, db_ref), (c_ref, dc_ref))):
            gj = gate[:, j * D:(j + 1) * D]
            dy_ref[...] = dmv * gj
            dg_ref[:, j * D:(j + 1) * D] = dmv * y_ref[...] * gj * (1.0 - gj)
        dbg_ref[...] += jnp.sum(dg_ref[...], axis=0, keepdims=True)

    row = pl.BlockSpec((blk, D), lambda i: (i, 0))
    wide = pl.BlockSpec((blk, NG), lambda i: (i, 0))
    vec = pl.BlockSpec((1, NG), lambda i: (0, 0))
    return pl.pallas_call(
        body, name=name, grid=(t // blk,), in_specs=[row, wide, vec, row, row, row],
        out_specs=[wide, row, row, row, vec],
        out_shape=[jax.ShapeDtypeStruct((t, NG), F32)] + [jax.ShapeDtypeStruct((t, D), F32)] * 3
                  + [jax.ShapeDtypeStruct((1, NG), F32)],
        compiler_params=_cparams("arbitrary"))(dm, pg, bg, ya, yb, yc)


def _adamw_math(w, g, m, v):
    m = ADAM_B1 * m + (1.0 - ADAM_B1) * g
    v = ADAM_B2 * v + (1.0 - ADAM_B2) * (g * g)
    m_hat = m / (1.0 - ADAM_B1 ** ADAM_STEP)
    v_hat = v / (1.0 - ADAM_B2 ** ADAM_STEP)
    delta = -ADAM_LR * (m_hat / (jnp.sqrt(v_hat) + ADAM_EPS) + ADAM_WD * w)
    return delta, m, v


def _sum_adamw(parts, w, m, v, name):
    r, c = w.shape
    br = r if r <= 256 else 256
    assert r % br == 0

    def body(p_ref, w_ref, m_ref, v_ref, g_ref, d_ref, nm_ref, nv_ref):
        g = p_ref[0]
        for k in range(1, N_DEV):
            g = g + p_ref[k]
        g_ref[...] = g
        d_ref[...], nm_ref[...], nv_ref[...] = _adamw_math(w_ref[...], g, m_ref[...], v_ref[...])

    blk = pl.BlockSpec((br, c), lambda i: (i, 0))
    return pl.pallas_call(
        body, name=name, grid=(r // br,),
        in_specs=[pl.BlockSpec((N_DEV, br, c), lambda i: (0, i, 0)), blk, blk, blk], out_specs=[blk] * 4,
        out_shape=[jax.ShapeDtypeStruct((r, c), F32)] * 4, compiler_params=_cparams("parallel"))(parts, w, m, v)


def _adamw(g, w, m, v, name):
    def body(g_ref, w_ref, m_ref, v_ref, d_ref, nm_ref, nv_ref):
        d_ref[...], nm_ref[...], nv_ref[...] = _adamw_math(w_ref[...], g_ref[...], m_ref[...], v_ref[...])

    return pl.pallas_call(body, name=name, out_shape=[jax.ShapeDtypeStruct(w.shape, F32)] * 3)(g, w, m, v)


def _sum_slots(parts, name):
    def body(p_ref, o_ref):
        g = p_ref[0]
        for k in range(1, N_DEV):
            g = g + p_ref[k]
        o_ref[...] = g

    return pl.pallas_call(body, name=name, out_shape=jax.ShapeDtypeStruct(parts.shape[1:], F32))(parts)


def _exchange(srcs, name, broadcast):
    n = len(srcs)
    out_shapes = [jax.ShapeDtypeStruct((N_DEV,) + (s.shape if broadcast else s.shape[1:]), s.dtype) for s in srcs]

    def body(*refs):
        src_refs, dst_refs = refs[:n], refs[n:2 * n]
        send_sems, recv_sems, local_sems = refs[2 * n:]
        x, y, c = lax.axis_index("x"), lax.axis_index("y"), lax.axis_index("c")
        me = 4 * x + 2 * y + c
        copies = []
        for k in range(1, N_DEV):
            px = 1 - x if (k >> 2) & 1 else x
            py = 1 - y if (k >> 1) & 1 else y
            pc = 1 - c if k & 1 else c
            peer = 4 * px + 2 * py + pc
            for a in range(n):
                cp = pltpu.make_async_remote_copy(
                    src_ref=src_refs[a] if broadcast else src_refs[a].at[peer], dst_ref=dst_refs[a].at[me],
                    send_sem=send_sems.at[k - 1, a], recv_sem=recv_sems.at[k - 1, a],
                    device_id=(px, py, pc), device_id_type=MESH)
                cp.start()
                copies.append(cp)
        local = []
        for a in range(n):
            cp = pltpu.make_async_copy(src_refs[a] if broadcast else src_refs[a].at[me], dst_refs[a].at[me],
                                       local_sems.at[a])
            cp.start()
            local.append(cp)
        for cp in copies:
            cp.wait()
        for cp in local:
            cp.wait()

    hbm = pl.BlockSpec(memory_space=pltpu.HBM)
    return pl.pallas_call(
        body, name=name, in_specs=[hbm] * n, out_specs=[hbm] * n, out_shape=out_shapes,
        scratch_shapes=[pltpu.SemaphoreType.DMA((N_DEV - 1, n)), pltpu.SemaphoreType.DMA((N_DEV - 1, n)),
                        pltpu.SemaphoreType.DMA((n,))])(*srcs)


def _regroup_w_in(w):
    wa = w[:, OFF_A:OFF_A + NA]
    seg = lambda off, h, n=HD: w[:, off + h * n: off + (h + 1) * n]
    wb = jnp.concatenate([seg(OFF_B + s * 512, h) for h in range(HGRN_HEADS) for s in range(4)], axis=1)
    parts = []
    for h in range(GDN_QK_HEADS):
        small = jnp.concatenate(
            [w[:, OFF_BETA + 2 * h: OFF_BETA + 2 * h + 2], w[:, OFF_CA + 2 * h: OFF_CA + 2 * h + 2],
             jnp.zeros((w.shape[0], HD - 4), w.dtype)], axis=1)
        parts += [seg(OFF_CQ, h), seg(OFF_CK, h), seg(OFF_CV, h, 2 * HD), seg(OFF_CZ, h, 2 * HD), small]
    wc = jnp.concatenate(parts, axis=1)
    wg = w[:, OFF_G:OFF_G + NG]
    return wa, wb, wc, wg


def _ungroup_dw_in(da, db, dc, dg):
    bq = [jnp.concatenate([db[:, h * 512 + s * HD: h * 512 + (s + 1) * HD] for h in range(HGRN_HEADS)], axis=1)
          for s in range(4)]
    ch = lambda h, lo, hi: dc[:, h * C_HEAD + lo: h * C_HEAD + hi]
    heads = range(GDN_QK_HEADS)
    cq = jnp.concatenate([ch(h, 0, HD) for h in heads], axis=1)
    ck = jnp.concatenate([ch(h, HD, 2 * HD) for h in heads], axis=1)
    cv = jnp.concatenate([ch(h, 2 * HD, 4 * HD) for h in heads], axis=1)
    cz = jnp.concatenate([ch(h, 4 * HD, 6 * HD) for h in heads], axis=1)
    cbeta = jnp.concatenate([ch(h, 6 * HD, 6 * HD + 2) for h in heads], axis=1)
    ca = jnp.concatenate([ch(h, 6 * HD + 2, 6 * HD + 4) for h in heads], axis=1)
    return jnp.concatenate([da] + bq + [cq, ck, cv, cbeta, ca, cz, dg], axis=1)


def _regroup_conv_c(cw):
    parts = []
    for h in range(GDN_QK_HEADS):
        parts += [cw[:, h * HD:(h + 1) * HD], cw[:, 512 + h * HD: 512 + (h + 1) * HD],
                  cw[:, 1024 + 2 * h * HD: 1024 + (2 * h + 2) * HD]]
    return jnp.concatenate(parts, axis=1)


def _ungroup_conv_c(d):
    heads = range(GDN_QK_HEADS)
    q = jnp.concatenate([d[:, h * 512: h * 512 + HD] for h in heads], axis=1)
    k = jnp.concatenate([d[:, h * 512 + HD: h * 512 + 2 * HD] for h in heads], axis=1)
    v = jnp.concatenate([d[:, h * 512 + 2 * HD: h * 512 + 4 * HD] for h in heads], axis=1)
    return jnp.concatenate([q, k, v], axis=1)


def _numel(shape):
    n = 1
    for d in shape:
        n *= d
    return n


def _pack(arrays, rows):
    flat = jnp.concatenate([a.reshape(-1) for a in arrays])
    return jnp.pad(flat, (0, rows * 128 - flat.shape[0])).reshape(rows, 128)


def _unpack(packed, shapes):
    flat = packed.reshape(-1)
    out, off = [], 0
    for s in shapes:
        out.append(flat[off:off + _numel(s)].reshape(s))
        off += _numel(s)
    return out


def _rows_for(shapes):
    return -(-sum(_numel(s) for s in shapes) // 1024) * 8


def kernel(x, norm_w, w_in, b_gate, conv_a, conv_c, a_log, dt_bias, lower_bounds, hgrn_norm_w, gdn_norm_w, w_out_a, w_out_b, w_out_c, w_o, final_norm_w, loss_target, m_norm_w, m_w_in, m_b_gate, m_conv_a, m_conv_c, m_a_log, m_dt_bias, m_lower_bounds, m_hgrn_norm_w, m_gdn_norm_w, m_w_out_a, m_w_out_b, m_w_out_c, m_w_o, m_final_norm_w, v_norm_w, v_w_in, v_b_gate, v_conv_a, v_conv_c, v_a_log, v_dt_bias, v_lower_bounds, v_hgrn_norm_w, v_gdn_norm_w, v_w_out_a, v_w_out_b, v_w_out_c, v_w_o, v_final_norm_w):
    me = 4 * lax.axis_index("x") + 2 * lax.axis_index("y") + lax.axis_index("c")
    xs = x[0]
    target = loss_target[0]
    in_shard = w_in.shape[2]

    big = [w_in, w_out_a, w_out_b, w_out_c, w_o]
    g_in, g_oa, g_ob, g_oc, g_o = _exchange([w.astype(BF16) for w in big], "gather_weights", broadcast=True)
    conv_shapes = [(DEPTH, 3, CONV_W), (DEPTH, 4, 2048)]
    conv_rows = _rows_for(conv_shapes)
    ca_full = lax.dynamic_update_slice(jnp.zeros(conv_shapes[0], F32), conv_a, (0, 0, me * conv_a.shape[2]))
    cc_full = lax.dynamic_update_slice(jnp.zeros(conv_shapes[1], F32), conv_c, (0, 0, me * conv_c.shape[2]))
    conv_parts, = _exchange([_pack([ca_full, cc_full], conv_rows)], "gather_conv", broadcast=True)
    conv_a_full, conv_c_full = _unpack(_sum_slots(conv_parts, "sum_conv"), conv_shapes)

    lb_pad = jnp.pad(lower_bounds, ((0, 8 - DEPTH), (0, 0)))
    lbs = _lower_bounds_fwd(lb_pad, "lower_bounds_fwd")

    layers = []
    for l in range(DEPTH):
        w_full = jnp.concatenate([g_in[q, l] for q in range(N_DEV)], axis=1)
        wa, wb, wc, wg = _regroup_w_in(w_full)
        woa = jnp.concatenate([g_oa[q, l] for q in range(N_DEV)], axis=1)
        wob = jnp.concatenate([g_ob[q, l] for q in range(N_DEV)], axis=1)
        woc = g_oc[:, l].reshape(D, D)
        wo = g_o[:, l].reshape(D, D)
        lanes = lambda vec: jnp.pad(vec.reshape(GDN_QK_HEADS, 1, 2), ((0, 0), (0, 0), (0, HD - 2)))
        cpar = jnp.concatenate([lanes(a_log[l]), lanes(dt_bias[l]), jnp.zeros((GDN_QK_HEADS, 6, HD), F32)], axis=1)
        layers.append(dict(
            wa=wa, wb=wb, wc=wc, wg=wg, woa=woa, wob=wob, woc=woc, wo=wo, cpar=cpar,
            nw=norm_w[l:l + 1], bg=b_gate[l:l + 1], cwa=conv_a_full[l], cwc=_regroup_conv_c(conv_c_full[l]),
            lb=lbs[l:l + 1], hnw=hgrn_norm_w[l:l + 1], gnw=gdn_norm_w[l:l + 1]))

    saved = []
    cur = xs
    for l, L in enumerate(layers):
        n = f"l{l}_"
        h = _rmsnorm_fwd(cur, L["nw"], n + "rms")
        pa = _matmul(h, L["wa"], "nn", n + "proj_a")
        pb = _matmul(h, L["wb"], "nn", n + "proj_b")
        pc = _matmul(h, L["wc"], "nn", n + "proj_c")
        pg = _matmul(h, L["wg"], "nn", n + "proj_g")
        ua = _branch_a_fwd(pa, L["cwa"], n + "conv_fwd")
        ub, sb = _branch_b_fwd(pb, L["lb"], L["hnw"], n + "hgrn_fwd")
        uc, sc = _branch_c_fwd(pc, L["cwc"], L["cpar"], L["gnw"], n + "gdn_fwd")
        ya = _matmul(ua, L["woa"], "nn", n + "out_a")
        yb = _matmul(ub, L["wob"], "nn", n + "out_b")
        yc = _matmul(uc, L["woc"], "nn", n + "out_c")
        merged = _merge_fwd(pg, L["bg"], ya, yb, yc, n + "merge")
        nxt = _matmul(merged, L["wo"], "nn", n + "out_o", residual=cur)
        saved.append(dict(x=cur, h=h, pa=pa, pb=pb, pc=pc, pg=pg, ua=ua, ub=ub, uc=uc, sb=sb, sc=sc,
                          ya=ya, yb=yb, yc=yc, merged=merged))
        cur = nxt

    loss_part, dx, d_final = _loss_head(cur, final_norm_w.reshape(1, D), target, "loss_head")

    grads = [None] * DEPTH
    dlbs_rows = [None] * DEPTH
    for l in reversed(range(DEPTH)):
        L, S = layers[l], saved[l]
        n = f"l{l}_"
        dmerged = _matmul(dx, L["wo"], "nt", n + "d_merged")
        d_wo = _matmul(S["merged"], dx, "tn", n + "dw_o")
        dpg, dya, dyb, dyc, d_bg = _merge_bwd(dmerged, S["pg"], L["bg"], S["ya"], S["yb"], S["yc"], n + "merge_bwd")
        dua = _matmul(dya, L["woa"], "nt", n + "d_ua")
        dub = _matmul(dyb, L["wob"], "nt", n + "d_ub")
        duc = _matmul(dyc, L["woc"], "nt", n + "d_uc")
        d_woa = _matmul(S["ua"], dya, "tn", n + "dw_out_a")
        d_wob = _matmul(S["ub"], dyb, "tn", n + "dw_out_b")
        d_woc = _matmul(S["uc"], dyc, "tn", n + "dw_out_c")
        dpa, d_cwa = _branch_a_bwd(S["pa"], L["cwa"], dua, n + "conv_bwd")
        dpb, d_lb, d_hnw = _branch_b_bwd(S["pb"], S["sb"], L["lb"], L["hnw"], dub, n + "hgrn_bwd")
        dpc, d_cwc, d_cpar, d_gnw = _branch_c_bwd(S["pc"], S["sc"], L["cwc"], L["cpar"], L["gnw"], duc, n + "gdn_bwd")
        dh = _matmul(dpa, L["wa"], "nt", n + "dh_a")
        dh = _matmul(dpb, L["wb"], "nt", n + "dh_b", residual=dh)
        dh = _matmul(dpc, L["wc"], "nt", n + "dh_c", residual=dh)
        dh = _matmul(dpg, L["wg"], "nt", n + "dh_g", residual=dh)
        d_win = _ungroup_dw_in(_matmul(S["h"], dpa, "tn", n + "dw_a"), _matmul(S["h"], dpb, "tn", n + "dw_b"),
                               _matmul(S["h"], dpc, "tn", n + "dw_c"), _matmul(S["h"], dpg, "tn", n + "dw_g"))
        dx, d_nw = _rmsnorm_bwd(dh, S["x"], L["nw"], dx, n + "rms_bwd")
        dlbs_rows[l] = d_lb
        grads[l] = dict(w_in=d_win, w_out_a=d_woa, w_out_b=d_wob, w_out_c=d_woc, w_o=d_wo, norm_w=d_nw[0],
                        b_gate=d_bg[0], conv_a=d_cwa, conv_c=_ungroup_conv_c(d_cwc),
                        a_log=d_cpar[:, 0, 0:2].reshape(-1), dt_bias=d_cpar[:, 1, 0:2].reshape(-1),
                        hgrn_norm_w=d_hnw[0], gdn_norm_w=d_gnw[0])
    grad_x = dx[None]
    d_lower = _lower_bounds_bwd(lb_pad, jnp.pad(jnp.concatenate(dlbs_rows, axis=0), ((0, 8 - DEPTH), (0, 0))),
                                "lower_bounds_bwd")[:DEPTH]

    stack = lambda name: jnp.stack([grads[l][name] for l in range(DEPTH)])
    g_win = stack("w_in")
    send_in = jnp.stack([g_win[:, :, p * in_shard:(p + 1) * in_shard] for p in range(N_DEV)]).reshape(N_DEV, DEPTH * D, in_shard)
    g_a, g_b = stack("w_out_a"), stack("w_out_b")
    send_oa = jnp.stack([g_a[:, :, p * 128:(p + 1) * 128] for p in range(N_DEV)]).reshape(N_DEV, DEPTH * 512, 128)
    send_ob = jnp.stack([g_b[:, :, p * 128:(p + 1) * 128] for p in range(N_DEV)]).reshape(N_DEV, DEPTH * 512, 128)
    send_oc = stack("w_out_c").reshape(DEPTH, N_DEV, 128, D).transpose(1, 0, 2, 3).reshape(N_DEV, DEPTH * 128, D)
    send_o = stack("w_o").reshape(DEPTH, N_DEV, 128, D).transpose(1, 0, 2, 3).reshape(N_DEV, DEPTH * 128, D)
    recv = _exchange([send_in, send_oa, send_ob, send_oc, send_o], "exchange_grads", broadcast=False)

    big_out = {}
    for name, parts, w, m, v in (("w_in", recv[0], w_in, m_w_in, v_w_in), ("w_out_a", recv[1], w_out_a, m_w_out_a, v_w_out_a),
                                 ("w_out_b", recv[2], w_out_b, m_w_out_b, v_w_out_b), ("w_out_c", recv[3], w_out_c, m_w_out_c, v_w_out_c),
                                 ("w_o", recv[4], w_o, m_w_o, v_w_o)):
        r2 = lambda a: a.reshape(parts.shape[1], parts.shape[2])
        outs = _sum_adamw(parts, r2(w), r2(m), r2(v), "adamw_" + name)
        big_out[name] = [o.reshape(w.shape) for o in outs]

    small_names = ["norm_w", "b_gate", "conv_a", "conv_c", "a_log", "dt_bias", "lower_bounds", "hgrn_norm_w",
                   "gdn_norm_w", "final_norm_w", "loss"]
    small_vals = {k: stack(k) for k in ("norm_w", "b_gate", "conv_a", "conv_c", "a_log", "dt_bias", "hgrn_norm_w", "gdn_norm_w")}
    small_vals.update(lower_bounds=d_lower, final_norm_w=d_final[0], loss=loss_part.reshape(1))
    small_shapes = [small_vals[k].shape for k in small_names]
    small_rows = _rows_for(small_shapes)
    small_parts, = _exchange([_pack([small_vals[k] for k in small_names], small_rows)], "exchange_small", broadcast=True)
    total = dict(zip(small_names, _unpack(_sum_slots(small_parts, "sum_small"), small_shapes)))
    loss = total["loss"][0]
    g_conv_a = lax.dynamic_slice(total["conv_a"], (0, 0, me * conv_a.shape[2]), conv_a.shape)
    g_conv_c = lax.dynamic_slice(total["conv_c"], (0, 0, me * conv_c.shape[2]), conv_c.shape)

    small_w = dict(norm_w=(norm_w, m_norm_w, v_norm_w), b_gate=(b_gate, m_b_gate, v_b_gate),
                   conv_a=(conv_a, m_conv_a, v_conv_a), conv_c=(conv_c, m_conv_c, v_conv_c),
                   a_log=(a_log, m_a_log, v_a_log), dt_bias=(dt_bias, m_dt_bias, v_dt_bias),
                   lower_bounds=(lower_bounds, m_lower_bounds, v_lower_bounds),
                   hgrn_norm_w=(hgrn_norm_w, m_hgrn_norm_w, v_hgrn_norm_w), gdn_norm_w=(gdn_norm_w, m_gdn_norm_w, v_gdn_norm_w),
                   final_norm_w=(final_norm_w, m_final_norm_w, v_final_norm_w))
    small_g = dict(total, conv_a=g_conv_a, conv_c=g_conv_c)
    upd_names = small_names[:-1]
    upd_shapes = [small_w[k][0].shape for k in upd_names]
    upd_rows = _rows_for(upd_shapes)
    pk = lambda j: _pack([small_w[k][j] for k in upd_names], upd_rows)
    s_delta, s_m, s_v = _adamw(_pack([small_g[k] for k in upd_names], upd_rows), pk(0), pk(1), pk(2), "adamw_small")
    small_out = {k: [small_g[k], d, mm, vv] for k, d, mm, vv in
                 zip(upd_names, _unpack(s_delta, upd_shapes), _unpack(s_m, upd_shapes), _unpack(s_v, upd_shapes))}

    order = ["norm_w", "w_in", "b_gate", "conv_a", "conv_c", "a_log", "dt_bias", "lower_bounds", "hgrn_norm_w",
             "gdn_norm_w", "w_out_a", "w_out_b", "w_out_c", "w_o", "final_norm_w"]
    res = {**small_out, **big_out}
    outs = [loss, grad_x]
    for j in range(4):
        outs += [res[k][j] for k in order]
    return tuple(outs)
```

```python
import functools

import jax
import jax.numpy as jnp
from jax import lax
from jax.experimental import pallas as pl
from jax.experimental.pallas import tpu as pltpu

F32 = jnp.float32
BF16 = jnp.bfloat16
H3 = lax.Precision.HIGH
MESH = pl.DeviceIdType.MESH

N_DEV = 8
D = 1024
DEPTH = 2
CHUNK = 64
SUB = 16
HGRN_BLOCK_CHUNKS = 4
GDN_BLOCK_CHUNKS = 8
GROUP = 128
NORM_EPS = 1e-6
L2_EPS = 1e-6
MIN_F = 1e-30
HD = 128
HGRN_HEADS = 4
GDN_QK_HEADS = 4
CONV_W = 512
IN_COLS = 10256
OFF_A, OFF_B, OFF_CQ, OFF_CK, OFF_CV, OFF_BETA, OFF_CA, OFF_CZ, OFF_G = (
    0, 2048, 4096, 4608, 5120, 6144, 6152, 6160, 7184)
NA, NB, NC_COLS, NG = 2048, 2048, 3584, 3072
C_HEAD = 896

ADAM_LR, ADAM_B1, ADAM_B2, ADAM_EPS, ADAM_WD, ADAM_STEP = 0.001, 0.9, 0.999, 1e-08, 0.01, 10

VMEM_LIMIT = 56 * 1024 * 1024


def _cparams(*sem):
    return pltpu.CompilerParams(dimension_semantics=sem, vmem_limit_bytes=VMEM_LIMIT)


def _tile(dim, cap):
    if dim <= cap:
        return dim
    t = (cap // 128) * 128
    while dim % t:
        t -= 128
    return t


def _sigmoid(x):
    return 1.0 / (1.0 + jnp.exp(-x))


def _silu(x):
    return x * _sigmoid(x)


def _softplus(x):
    return jnp.maximum(x, 0.0) + jnp.log(1.0 + jnp.exp(-jnp.abs(x)))


def _dot(a, b, dims, precision=None):
    if precision is None:
        a, b = a.astype(BF16), b.astype(BF16)
    return lax.dot_general(a, b, (dims, ((), ())), precision=precision, preferred_element_type=F32)


def _nn(a, b, precision=None):
    return _dot(a, b, ((1,), (0,)), precision)


def _nt(a, b, precision=None):
    return _dot(a, b, ((1,), (1,)), precision)


def _tn(a, b, precision=None):
    return _dot(a, b, ((0,), (0,)), precision)


def _sum_rows_exact(mat01, x):
    m = mat01.astype(BF16)
    hi = x.astype(BF16)
    rest = x - hi.astype(F32)
    mid = rest.astype(BF16)
    low = (rest - mid.astype(F32)).astype(BF16)
    return _nn(m, hi) + _nn(m, mid) + _nn(m, low)


@functools.partial(jax.custom_vjp, nondiff_argnums=(1,))
def _shift_rows(x, d):
    return x if d == 0 else pltpu.roll(x, d, 0)


def _shift_rows_fwd(x, d):
    return _shift_rows(x, d), None


def _shift_rows_bwd(d, _, ct):
    return ((ct if d == 0 else pltpu.roll(ct, ct.shape[0] - d, 0)),)


_shift_rows.defvjp(_shift_rows_fwd, _shift_rows_bwd)


def _iota2(shape):
    return lax.broadcasted_iota(jnp.int32, shape, 0), lax.broadcasted_iota(jnp.int32, shape, 1)


def _lane_pick(x, i):
    lane = lax.broadcasted_iota(jnp.int32, x.shape, 1)
    return jnp.sum(jnp.where(lane == i, x, 0.0), axis=1, keepdims=True)


def _hgrn_block(qr, fr, ir, zr, st0, lb, nw):
    rows = qr.shape[0]
    r, c = _iota2((CHUNK, CHUNK))
    sr, sc = r // SUB, c // SUB
    mats = [c <= r,
            (c <= r) & (sr == sc),
            (c > r) & (sr == sc),
            c > r,
            sc == sr - 1,
            (sc == sr - 1) | (sc == sr - 2)]
    stack = jnp.concatenate([m.astype(F32) for m in mats], axis=0)
    dist = sr - sc

    q = _silu(qr) * (HD ** -0.5)
    fg = lb + (1.0 - lb) * _sigmoid(fr)
    logf = jnp.log(jnp.maximum(fg, MIN_F))
    kk = 1.0 - fg
    v = ir

    chunks = [slice(s, s + CHUNK) for s in range(0, rows, CHUNK)]
    cums = [_sum_rows_exact(stack, logf[sl]) for sl in chunks]
    part = lambda i: jnp.concatenate([cs[i * CHUNK:(i + 1) * CHUNK] for cs in cums], axis=0)
    g, gl, gr, grc, c2, c3 = [part(i) for i in range(6)]
    o = jnp.zeros_like(v)
    row = lax.broadcasted_iota(jnp.int32, (rows, HD), 0) % SUB
    for d in range(SUB):
        ok = row >= d
        diff = gl - _shift_rows(gl, d)
        dec = jnp.where(ok, jnp.exp(jnp.where(ok, diff, 0.0)), 0.0)
        term = jnp.sum(q * dec * _shift_rows(kk, d), axis=1, keepdims=True)
        o += term * _shift_rows(v, d)

    qg = q * jnp.exp(g)
    qt = q * jnp.exp(gl)
    qt2 = qt * jnp.exp(c2)
    qt3 = qt * jnp.exp(c3)
    kh = kk * jnp.exp(gr)
    ks = kk * jnp.exp(grc)
    st = st0
    outs = []
    for sl in chunks:
        scores = jnp.where(dist == 1, _nt(qt[sl], kh[sl]), 0.0)
        scores += jnp.where(dist == 2, _nt(qt2[sl], kh[sl]), 0.0)
        scores += jnp.where(dist == 3, _nt(qt3[sl], kh[sl]), 0.0)
        outs.append(_nn(scores, v[sl]) + _nt(qg[sl], st))
        st = st * jnp.exp(jnp.sum(logf[sl], axis=0, keepdims=True)) + _tn(v[sl], ks[sl])
    o += jnp.concatenate(outs, axis=0)
    y = o * lax.rsqrt(jnp.mean(o * o, axis=1, keepdims=True) + NORM_EPS) * nw * _silu(zr)
    return y, st


@jax.custom_vjp
def _unit_lower_inverses(ms):
    r, c = _iota2(ms[0].shape)
    xs = [jnp.where(r == c, 1.0, 0.0) - jnp.where((r // 2) == (c // 2), m, 0.0) for m in ms]
    b = 2
    while b < CHUNK:
        pick = ((r // (2 * b)) == (c // (2 * b))) & ((r // b) != (c // b))
        ts = [_nn(x, jnp.where(pick, m, 0.0), H3) for x, m in zip(xs, ms)]
        xs = [x - _nn(t, x, H3) for x, t in zip(xs, ts)]
        b *= 2
    return tuple(xs)


def _unit_lower_inverses_fwd(ms):
    xs = _unit_lower_inverses(ms)
    return xs, xs


def _unit_lower_inverses_bwd(xs, cts):
    r, c = _iota2(xs[0].shape)
    keep = (c < r) & ((r // CHUNK) == (c // CHUNK))
    ts = [_tn(x, ct, H3) for x, ct in zip(xs, cts)]
    return (tuple(jnp.where(keep, -_nt(t, x, H3), 0.0) for t, x in zip(ts, xs)),)


_unit_lower_inverses.defvjp(_unit_lower_inverses_fwd, _unit_lower_inverses_bwd)


def _chunk_cumsum(x):
    row = lax.broadcasted_iota(jnp.int32, x.shape, 0) % CHUNK
    d = 1
    while d < CHUNK:
        x = x + jnp.where(row >= d, _shift_rows(x, d), 0.0)
        d *= 2
    return x


def _gdn_block(x_ext, z, ba, s0a, s0b, w0, w1, w2, w3, alog, dtb, nw):
    rows = z.shape[0]
    conv = (w0 * _shift_rows(x_ext, 3) + w1 * _shift_rows(x_ext, 2) + w2 * _shift_rows(x_ext, 1) + w3 * x_ext)
    cc = _silu(conv[8:])
    qc, kc = cc[:, 0:HD], cc[:, HD:2 * HD]
    q = qc * lax.rsqrt(jnp.sum(qc * qc, axis=1, keepdims=True) + L2_EPS) * (HD ** -0.5)
    k = kc * lax.rsqrt(jnp.sum(kc * kc, axis=1, keepdims=True) + L2_EPS)

    r, c = _iota2((GROUP, GROUP))
    same = (r // CHUNK) == (c // CHUNK)
    causal, strict, eye = same & (c <= r), same & (c < r), r == c
    heads = (0, 1)
    groups = [slice(lo, lo + GROUP) for lo in range(0, rows, GROUP)]
    chunks = [slice(lo, lo + CHUNK) for lo in range(0, rows, CHUNK)]

    v, loga, g_w, kb, kg, qg = [], [], [], [], [], []
    for i in heads:
        v.append(cc[:, (2 + i) * HD:(3 + i) * HD])
        beta = _sigmoid(_lane_pick(ba, i))
        a_neg = -jnp.exp(_lane_pick(alog, i))
        loga.append(a_neg * _softplus(_lane_pick(ba, 2 + i) + _lane_pick(dtb, i)))
        g_w.append(_chunk_cumsum(jnp.broadcast_to(loga[i], (rows, HD))))
        kb.append(k * beta)
        kg.append(k * jnp.exp(g_w[i]))
        qg.append(q * jnp.exp(g_w[i]))

    systems = [(i, gs) for gs in groups for i in heads]
    dec_c, ms = [], []
    for i, gs in systems:
        g_sq = g_w[i][gs]
        g_row = jnp.sum(jnp.where(eye, g_sq, 0.0), axis=0, keepdims=True)
        diff = g_sq - g_row
        dec_c.append(jnp.where(causal, jnp.exp(jnp.where(causal, diff, 0.0)), 0.0))
        ms.append(jnp.where(strict, _nt(k[gs], kb[i][gs]) * dec_c[-1], 0.0))
    xs = _unit_lower_inverses(tuple(ms))
    u = [[None] * len(groups) for _ in heads]
    w = [[None] * len(groups) for _ in heads]
    qk = [[None] * len(groups) for _ in heads]
    for n, (i, gs) in enumerate(systems):
        j = n // len(heads)
        u[i][j] = _nn(xs[n], v[i][gs])
        w[i][j] = _nn(xs[n], kg[i][gs])
        qk[i][j] = _nt(q[gs], kb[i][gs]) * dec_c[n]
    u = [jnp.concatenate(p, axis=0) for p in u]
    w = [jnp.concatenate(p, axis=0) for p in w]

    decay, p_mat, q_mat = {}, {}, {}
    for n, sl in enumerate(chunks):
        for i in heads:
            g_last = jnp.sum(loga[i][sl], axis=0, keepdims=True)
            kd = kb[i][sl] * jnp.exp(g_last - g_w[i][sl])
            decay[n, i] = jnp.exp(g_last)
            p_mat[n, i] = -_tn(kd, w[i][sl])
            q_mat[n, i] = _tn(kd, u[i][sl])
    s = [s0a, s0b]
    s_at = {}
    for n in range(len(chunks)):
        for i in heads:
            s_at[n, i] = s[i]
            s[i] = s[i] * decay[n, i] + _nn(p_mat[n, i], s[i]) + q_mat[n, i]

    ys = []
    for i in heads:
        e = jnp.concatenate([u[i][sl] - _nn(w[i][sl], s_at[n, i]) for n, sl in enumerate(chunks)], axis=0)
        o_state = jnp.concatenate([_nn(qg[i][sl], s_at[n, i]) for n, sl in enumerate(chunks)], axis=0)
        o = o_state + jnp.concatenate([_nn(qk[i][j], e[gs]) for j, gs in enumerate(groups)], axis=0)
        zi = z[:, i * HD:(i + 1) * HD]
        ys.append(o * lax.rsqrt(jnp.mean(o * o, axis=1, keepdims=True) + NORM_EPS) * nw * _silu(zi))
    return jnp.concatenate(ys, axis=1), s[0], s[1]


def _conv_a_block(ab, ac_ext, ax_ext, az, w0, w1, w2):
    u = ac_ext * ax_ext
    conv = (w0 * _shift_rows(u, 2) + w1 * _shift_rows(u, 1) + w2 * u)[8:]
    return ab * conv * _silu(az)


def _matmul(a, b, mode, name, residual=None, out_dtype=F32):
    if mode == "nn":
        (m, k), n = a.shape, b.shape[1]
    elif mode == "nt":
        (m, k), n = a.shape, b.shape[0]
    else:
        (k, m), n = a.shape, b.shape[1]
    tm, tn, tk = _tile(m, 512 if mode != "tn" else 1024), _tile(n, 1024), _tile(k, 1024 if mode != "tn" else 512)
    nk = k // tk
    dims = {"nn": ((1,), (0,)), "nt": ((1,), (1,)), "tn": ((0,), (0,))}[mode]
    a_spec = pl.BlockSpec((tk, tm), lambda i, j, s: (s, i)) if mode == "tn" else pl.BlockSpec((tm, tk), lambda i, j, s: (i, s))
    b_spec = pl.BlockSpec((tn, tk), lambda i, j, s: (j, s)) if mode == "nt" else pl.BlockSpec((tk, tn), lambda i, j, s: (s, j))
    o_spec = pl.BlockSpec((tm, tn), lambda i, j, s: (i, j))
    has_res = residual is not None

    def body(*refs):
        a_ref, b_ref = refs[0], refs[1]
        r_ref = refs[2] if has_res else None
        o_ref, acc_ref = refs[-2], refs[-1]
        s = pl.program_id(2)

        @pl.when(s == 0)
        def _():
            acc_ref[...] = jnp.zeros_like(acc_ref)

        acc_ref[...] += _dot(a_ref[...], b_ref[...], dims)

        @pl.when(s == nk - 1)
        def _():
            out = acc_ref[...]
            if has_res:
                out = out + r_ref[...]
            o_ref[...] = out.astype(out_dtype)

    args, specs = [a, b], [a_spec, b_spec]
    if has_res:
        args.append(residual)
        specs.append(o_spec)
    return pl.pallas_call(
        body, name=name, grid=(m // tm, n // tn, nk), in_specs=specs, out_specs=o_spec,
        out_shape=jax.ShapeDtypeStruct((m, n), out_dtype), scratch_shapes=[pltpu.VMEM((tm, tn), F32)],
        compiler_params=_cparams("parallel", "parallel", "arbitrary"))(*args)


def _rmsnorm_fwd(x, w, name):
    t = x.shape[0]
    blk = _tile(t, 512)

    def body(x_ref, w_ref, h_ref):
        xv = x_ref[...]
        h_ref[...] = (xv * lax.rsqrt(jnp.mean(xv * xv, axis=1, keepdims=True) + NORM_EPS) * w_ref[...]).astype(BF16)

    return pl.pallas_call(
        body, name=name, grid=(t // blk,),
        in_specs=[pl.BlockSpec((blk, D), lambda i: (i, 0)), pl.BlockSpec((1, D), lambda i: (0, 0))],
        out_specs=pl.BlockSpec((blk, D), lambda i: (i, 0)), out_shape=jax.ShapeDtypeStruct((t, D), BF16),
        compiler_params=_cparams("parallel"))(x, w)


def _rmsnorm_bwd(dh, x, w, dxo, name):
    t = x.shape[0]
    blk = _tile(t, 512)

    def body(dh_ref, x_ref, w_ref, dxo_ref, dx_ref, dw_ref):
        @pl.when(pl.program_id(0) == 0)
        def _():
            dw_ref[...] = jnp.zeros_like(dw_ref)

        xv, dhv = x_ref[...], dh_ref[...]
        rs = lax.rsqrt(jnp.mean(xv * xv, axis=1, keepdims=True) + NORM_EPS)
        xh = xv * rs
        dw_ref[...] += jnp.sum(dhv * xh, axis=0, keepdims=True)
        dxh = dhv * w_ref[...]
        dx_ref[...] = rs * (dxh - xh * jnp.mean(dxh * xh, axis=1, keepdims=True)) + dxo_ref[...]

    row = pl.BlockSpec((blk, D), lambda i: (i, 0))
    vec = pl.BlockSpec((1, D), lambda i: (0, 0))
    return pl.pallas_call(
        body, name=name, grid=(t // blk,), in_specs=[row, row, vec, row], out_specs=[row, vec],
        out_shape=[jax.ShapeDtypeStruct((t, D), F32), jax.ShapeDtypeStruct((1, D), F32)],
        compiler_params=_cparams("arbitrary"))(dh, x, w, dxo)


def _loss_head(x, w, target, name):
    t = x.shape[0]
    blk = _tile(t, 512)

    def body(x_ref, w_ref, t_ref, loss_ref, dx_ref, dw_ref):
        @pl.when(pl.program_id(0) == 0)
        def _():
            dw_ref[...] = jnp.zeros_like(dw_ref)
            loss_ref[...] = jnp.zeros_like(loss_ref)

        xv = x_ref[...]
        rs = lax.rsqrt(jnp.mean(xv * xv, axis=1, keepdims=True) + NORM_EPS)
        xh = xv * rs
        err = xh * w_ref[...] - t_ref[...]
        loss_ref[...] += 0.5 * jnp.sum(jnp.mean(err * err, axis=1, keepdims=True), axis=0, keepdims=True)
        dy = err * (1.0 / D)
        dw_ref[...] += jnp.sum(dy * xh, axis=0, keepdims=True)
        dxh = dy * w_ref[...]
        dx_ref[...] = rs * (dxh - xh * jnp.mean(dxh * xh, axis=1, keepdims=True))

    row = pl.BlockSpec((blk, D), lambda i: (i, 0))
    vec = pl.BlockSpec((1, D), lambda i: (0, 0))
    return pl.pallas_call(
        body, name=name, grid=(t // blk,), in_specs=[row, vec, row],
        out_specs=[pl.BlockSpec((1, 1), lambda i: (0, 0)), row, vec],
        out_shape=[jax.ShapeDtypeStruct((1, 1), F32), jax.ShapeDtypeStruct((t, D), F32), jax.ShapeDtypeStruct((1, D), F32)],
        compiler_params=_cparams("arbitrary"))(x, w, target)


def _lbs_of(lb):
    r = lax.broadcasted_iota(jnp.int32, lb.shape, 0)
    real = r < DEPTH
    mx = lax.stop_gradient(jnp.max(jnp.where(real, lb, -jnp.inf), axis=0, keepdims=True))
    e = jnp.where(real, jnp.exp(jnp.where(real, lb - mx, 0.0)), 0.0)
    p = e / jnp.sum(e, axis=0, keepdims=True)
    out = jnp.zeros_like(lb)
    run = jnp.zeros_like(mx)
    for l in range(1, DEPTH):
        run = run + jnp.sum(jnp.where(r == l, p, 0.0), axis=0, keepdims=True)
        out = out + jnp.where(r == l, run, 0.0)
    return out


def _lower_bounds_fwd(lbp, name):
    def body(lb_ref, o_ref):
        o_ref[...] = _lbs_of(lb_ref[...])

    return pl.pallas_call(body, name=name, out_shape=jax.ShapeDtypeStruct(lbp.shape, F32))(lbp)


def _lower_bounds_bwd(lbp, dlbs, name):
    def body(lb_ref, d_ref, o_ref):
        _, vjp = jax.vjp(_lbs_of, lb_ref[...])
        o_ref[...] = vjp(d_ref[...])[0]

    return pl.pallas_call(body, name=name, out_shape=jax.ShapeDtypeStruct(lbp.shape, F32))(lbp, dlbs)


def _branch_a_fwd(pa, cw, name):
    t = pa.shape[0]
    blk = _tile(t, 512)
    W = CONV_W

    def body(p_ref, w_ref, y_ref, hc_ref, hx_ref):
        @pl.when(pl.program_id(0) == 0)
        def _():
            hc_ref[...] = jnp.zeros_like(hc_ref)
            hx_ref[...] = jnp.zeros_like(hx_ref)

        ac, ax = p_ref[:, W:2 * W], p_ref[:, 2 * W:3 * W]
        y_ref[...] = _conv_a_block(
            p_ref[:, 0:W], jnp.concatenate([hc_ref[...], ac], axis=0), jnp.concatenate([hx_ref[...], ax], axis=0),
            p_ref[:, 3 * W:4 * W], w_ref[0:1, :], w_ref[1:2, :], w_ref[2:3, :])
        hc_ref[...] = p_ref[blk - 8:blk, W:2 * W]
        hx_ref[...] = p_ref[blk - 8:blk, 2 * W:3 * W]

    return pl.pallas_call(
        body, name=name, grid=(t // blk,),
        in_specs=[pl.BlockSpec((blk, NA), lambda i: (i, 0)), pl.BlockSpec((3, W), lambda i: (0, 0))],
        out_specs=pl.BlockSpec((blk, W), lambda i: (i, 0)), out_shape=jax.ShapeDtypeStruct((t, W), F32),
        scratch_shapes=[pltpu.VMEM((8, W), F32), pltpu.VMEM((8, W), F32)],
        compiler_params=_cparams("arbitrary"))(pa, cw)


def _branch_a_bwd(pa, cw, dy, name):
    t = pa.shape[0]
    blk = _tile(t, 512)
    nt_ = t // blk
    W = CONV_W
    hb = blk // 8

    def body(p_ref, halo_ref, w_ref, dy_ref, dp_ref, dw_ref, chc_ref, chx_ref):
        i = pl.program_id(0)

        @pl.when(i == 0)
        def _():
            chc_ref[...] = jnp.zeros_like(chc_ref)
            chx_ref[...] = jnp.zeros_like(chx_ref)
            dw_ref[...] = jnp.zeros_like(dw_ref)

        keep = 1.0 - (i == nt_ - 1).astype(F32)
        hc = halo_ref[:, W:2 * W] * keep
        hx = halo_ref[:, 2 * W:3 * W] * keep
        ac_ext = jnp.concatenate([hc, p_ref[:, W:2 * W]], axis=0)
        ax_ext = jnp.concatenate([hx, p_ref[:, 2 * W:3 * W]], axis=0)
        _, vjp = jax.vjp(_conv_a_block, p_ref[:, 0:W], ac_ext, ax_ext, p_ref[:, 3 * W:4 * W],
                         w_ref[0:1, :], w_ref[1:2, :], w_ref[2:3, :])
        dab, dac, dax, daz, dw0, dw1, dw2 = vjp(dy_ref[...])
        dp_ref[:, 0:W] = dab
        dp_ref[:, W:2 * W] = dac[8:]
        dp_ref[:, 2 * W:3 * W] = dax[8:]
        dp_ref[:, 3 * W:4 * W] = daz
        dp_ref[blk - 8:blk, W:2 * W] += chc_ref[...]
        dp_ref[blk - 8:blk, 2 * W:3 * W] += chx_ref[...]
        chc_ref[...] = dac[:8] * keep
        chx_ref[...] = dax[:8] * keep
        dw_ref[0:1, :] += dw0
        dw_ref[1:2, :] += dw1
        dw_ref[2:3, :] += dw2

    rev = lambda i: (nt_ - 1 - i, 0)
    return pl.pallas_call(
        body, name=name, grid=(nt_,),
        in_specs=[pl.BlockSpec((blk, NA), rev),
                  pl.BlockSpec((8, NA), lambda i: (jnp.maximum((nt_ - 1 - i) * hb - 1, 0), 0)),
                  pl.BlockSpec((3, W), lambda i: (0, 0)),
                  pl.BlockSpec((blk, W), rev)],
        out_specs=[pl.BlockSpec((blk, NA), rev), pl.BlockSpec((3, W), lambda i: (0, 0))],
        out_shape=[jax.ShapeDtypeStruct((t, NA), F32), jax.ShapeDtypeStruct((3, W), F32)],
        scratch_shapes=[pltpu.VMEM((8, W), F32), pltpu.VMEM((8, W), F32)],
        compiler_params=_cparams("arbitrary"))(pa, pa, cw, dy)


def _block_rows(t, chunks):
    return min(t, chunks * CHUNK)


def _branch_b_fwd(pb, lbs_row, nw, name):
    t = pb.shape[0]
    rows = _block_rows(t, HGRN_BLOCK_CHUNKS)
    nch = t // rows

    def body(p_ref, lb_ref, nw_ref, y_ref, s_ref, st_ref):
        @pl.when(pl.program_id(1) == 0)
        def _():
            st_ref[...] = jnp.zeros_like(st_ref)

        s_ref[0, 0] = st_ref[...]
        y, st1 = _hgrn_block(p_ref[:, 0:HD], p_ref[:, HD:2 * HD], p_ref[:, 2 * HD:3 * HD], p_ref[:, 3 * HD:4 * HD],
                             st_ref[...], lb_ref[...], nw_ref[...])
        y_ref[...] = y
        st_ref[...] = st1

    return pl.pallas_call(
        body, name=name, grid=(HGRN_HEADS, nch),
        in_specs=[pl.BlockSpec((rows, 4 * HD), lambda h, i: (i, h)),
                  pl.BlockSpec((1, HD), lambda h, i: (0, h)),
                  pl.BlockSpec((1, HD), lambda h, i: (0, 0))],
        out_specs=[pl.BlockSpec((rows, HD), lambda h, i: (i, h)),
                   pl.BlockSpec((1, 1, HD, HD), lambda h, i: (h, i, 0, 0))],
        out_shape=[jax.ShapeDtypeStruct((t, HGRN_HEADS * HD), F32),
                   jax.ShapeDtypeStruct((HGRN_HEADS, nch, HD, HD), F32)],
        scratch_shapes=[pltpu.VMEM((HD, HD), F32)],
        compiler_params=_cparams("arbitrary", "arbitrary"))(pb, lbs_row, nw)


def _branch_b_bwd(pb, states, lbs_row, nw, dy, name):
    t = pb.shape[0]
    rows = _block_rows(t, HGRN_BLOCK_CHUNKS)
    nch = t // rows

    def body(p_ref, s_ref, lb_ref, nw_ref, dy_ref, dp_ref, dlb_ref, dnw_ref, ds_ref):
        h, i = pl.program_id(0), pl.program_id(1)

        @pl.when(i == 0)
        def _():
            ds_ref[...] = jnp.zeros_like(ds_ref)
            dlb_ref[...] = jnp.zeros_like(dlb_ref)

        @pl.when((i == 0) & (h == 0))
        def _():
            dnw_ref[...] = jnp.zeros_like(dnw_ref)

        _, vjp = jax.vjp(_hgrn_block, p_ref[:, 0:HD], p_ref[:, HD:2 * HD], p_ref[:, 2 * HD:3 * HD],
                         p_ref[:, 3 * HD:4 * HD], s_ref[0, 0], lb_ref[...], nw_ref[...])
        dq, df, di, dz, ds0, dlb, dnw = vjp((dy_ref[...], ds_ref[...]))
        dp_ref[:, 0:HD] = dq
        dp_ref[:, HD:2 * HD] = df
        dp_ref[:, 2 * HD:3 * HD] = di
        dp_ref[:, 3 * HD:4 * HD] = dz
        ds_ref[...] = ds0
        dlb_ref[...] += dlb
        dnw_ref[...] += dnw

    rev = lambda h, i: (nch - 1 - i, h)
    return pl.pallas_call(
        body, name=name, grid=(HGRN_HEADS, nch),
        in_specs=[pl.BlockSpec((rows, 4 * HD), rev),
                  pl.BlockSpec((1, 1, HD, HD), lambda h, i: (h, nch - 1 - i, 0, 0)),
                  pl.BlockSpec((1, HD), lambda h, i: (0, h)),
                  pl.BlockSpec((1, HD), lambda h, i: (0, 0)),
                  pl.BlockSpec((rows, HD), rev)],
        out_specs=[pl.BlockSpec((rows, 4 * HD), rev),
                   pl.BlockSpec((1, HD), lambda h, i: (0, h)),
                   pl.BlockSpec((1, HD), lambda h, i: (0, 0))],
        out_shape=[jax.ShapeDtypeStruct((t, NB), F32), jax.ShapeDtypeStruct((1, HGRN_HEADS * HD), F32),
                   jax.ShapeDtypeStruct((1, HD), F32)],
        scratch_shapes=[pltpu.VMEM((HD, HD), F32)],
        compiler_params=_cparams("arbitrary", "arbitrary"))(pb, states, lbs_row, nw, dy)


def _branch_c_fwd(pc, cw, cpar, nw, name):
    t = pc.shape[0]
    rows = _block_rows(t, GDN_BLOCK_CHUNKS)
    nch = t // rows
    XW = 4 * HD

    def body(p_ref, w_ref, cp_ref, nw_ref, y_ref, s_ref, sa_ref, sb_ref, halo_ref):
        @pl.when(pl.program_id(1) == 0)
        def _():
            sa_ref[...] = jnp.zeros_like(sa_ref)
            sb_ref[...] = jnp.zeros_like(sb_ref)
            halo_ref[...] = jnp.zeros_like(halo_ref)

        s_ref[0, 0, 0] = sa_ref[...]
        s_ref[0, 0, 1] = sb_ref[...]
        x_ext = jnp.concatenate([halo_ref[...], p_ref[:, 0:XW]], axis=0)
        y, s1a, s1b = _gdn_block(x_ext, p_ref[:, XW:XW + 2 * HD], p_ref[:, XW + 2 * HD:XW + 3 * HD],
                                 sa_ref[...], sb_ref[...], w_ref[0:1, :], w_ref[1:2, :], w_ref[2:3, :], w_ref[3:4, :],
                                 cp_ref[0, 0:1, :], cp_ref[0, 1:2, :], nw_ref[...])
        y_ref[...] = y
        sa_ref[...] = s1a
        sb_ref[...] = s1b
        halo_ref[...] = p_ref[rows - 8:rows, 0:XW]

    return pl.pallas_call(
        body, name=name, grid=(GDN_QK_HEADS, nch),
        in_specs=[pl.BlockSpec((rows, C_HEAD), lambda h, i: (i, h)),
                  pl.BlockSpec((4, XW), lambda h, i: (0, h)),
                  pl.BlockSpec((1, 8, HD), lambda h, i: (h, 0, 0)),
                  pl.BlockSpec((1, HD), lambda h, i: (0, 0))],
        out_specs=[pl.BlockSpec((rows, 2 * HD), lambda h, i: (i, h)),
                   pl.BlockSpec((1, 1, 2, HD, HD), lambda h, i: (h, i, 0, 0, 0))],
        out_shape=[jax.ShapeDtypeStruct((t, 2 * GDN_QK_HEADS * HD), F32),
                   jax.ShapeDtypeStruct((GDN_QK_HEADS, nch, 2, HD, HD), F32)],
        scratch_shapes=[pltpu.VMEM((HD, HD), F32), pltpu.VMEM((HD, HD), F32), pltpu.VMEM((8, XW), F32)],
        compiler_params=_cparams("arbitrary", "arbitrary"))(pc, cw, cpar, nw)


def _branch_c_bwd(pc, states, cw, cpar, nw, dy, name):
    t = pc.shape[0]
    rows = _block_rows(t, GDN_BLOCK_CHUNKS)
    nch = t // rows
    XW = 4 * HD
    hb = rows // 8

    def body(p_ref, halo_ref, s_ref, w_ref, cp_ref, nw_ref, dy_ref, dp_ref, dw_ref, dcp_ref, dnw_ref,
             dsa_ref, dsb_ref, carry_ref):
        h, i = pl.program_id(0), pl.program_id(1)

        @pl.when(i == 0)
        def _():
            dsa_ref[...] = jnp.zeros_like(dsa_ref)
            dsb_ref[...] = jnp.zeros_like(dsb_ref)
            carry_ref[...] = jnp.zeros_like(carry_ref)
            dw_ref[...] = jnp.zeros_like(dw_ref)
            dcp_ref[...] = jnp.zeros_like(dcp_ref)

        @pl.when((i == 0) & (h == 0))
        def _():
            dnw_ref[...] = jnp.zeros_like(dnw_ref)

        keep = 1.0 - (i == nch - 1).astype(F32)
        x_ext = jnp.concatenate([halo_ref[:, 0:XW] * keep, p_ref[:, 0:XW]], axis=0)
        _, vjp = jax.vjp(_gdn_block, x_ext, p_ref[:, XW:XW + 2 * HD], p_ref[:, XW + 2 * HD:XW + 3 * HD],
                         s_ref[0, 0, 0], s_ref[0, 0, 1], w_ref[0:1, :], w_ref[1:2, :], w_ref[2:3, :], w_ref[3:4, :],
                         cp_ref[0, 0:1, :], cp_ref[0, 1:2, :], nw_ref[...])
        dx, dz, dba, dsa, dsb, dw0, dw1, dw2, dw3, dal, ddt, dnw = vjp((dy_ref[...], dsa_ref[...], dsb_ref[...]))
        dp_ref[:, 0:XW] = dx[8:]
        dp_ref[:, XW:XW + 2 * HD] = dz
        dp_ref[:, XW + 2 * HD:XW + 3 * HD] = dba
        dp_ref[rows - 8:rows, 0:XW] += carry_ref[...]
        carry_ref[...] = dx[:8] * keep
        dsa_ref[...] = dsa
        dsb_ref[...] = dsb
        dw_ref[0:1, :] += dw0
        dw_ref[1:2, :] += dw1
        dw_ref[2:3, :] += dw2
        dw_ref[3:4, :] += dw3
        dcp_ref[0, 0:1, :] += dal
        dcp_ref[0, 1:2, :] += ddt
        dnw_ref[...] += dnw

    rev = lambda h, i: (nch - 1 - i, h)
    return pl.pallas_call(
        body, name=name, grid=(GDN_QK_HEADS, nch),
        in_specs=[pl.BlockSpec((rows, C_HEAD), rev),
                  pl.BlockSpec((8, C_HEAD), lambda h, i: (jnp.maximum((nch - 1 - i) * hb - 1, 0), h)),
                  pl.BlockSpec((1, 1, 2, HD, HD), lambda h, i: (h, nch - 1 - i, 0, 0, 0)),
                  pl.BlockSpec((4, XW), lambda h, i: (0, h)),
                  pl.BlockSpec((1, 8, HD), lambda h, i: (h, 0, 0)),
                  pl.BlockSpec((1, HD), lambda h, i: (0, 0)),
                  pl.BlockSpec((rows, 2 * HD), rev)],
        out_specs=[pl.BlockSpec((rows, C_HEAD), rev),
                   pl.BlockSpec((4, XW), lambda h, i: (0, h)),
                   pl.BlockSpec((1, 8, HD), lambda h, i: (h, 0, 0)),
                   pl.BlockSpec((1, HD), lambda h, i: (0, 0))],
        out_shape=[jax.ShapeDtypeStruct((t, NC_COLS), F32), jax.ShapeDtypeStruct((4, GDN_QK_HEADS * XW), F32),
                   jax.ShapeDtypeStruct((GDN_QK_HEADS, 8, HD), F32), jax.ShapeDtypeStruct((1, HD), F32)],
        scratch_shapes=[pltpu.VMEM((HD, HD), F32), pltpu.VMEM((HD, HD), F32), pltpu.VMEM((8, XW), F32)],
        compiler_params=_cparams("arbitrary", "arbitrary"))(pc, pc, states, cw, cpar, nw, dy)


def _merge_fwd(pg, bg, ya, yb, yc, name):
    t = pg.shape[0]
    blk = _tile(t, 256)

    def body(g_ref, b_ref, a_ref, b2_ref, c_ref, o_ref):
        gate = _sigmoid(g_ref[...] + b_ref[...])
        o_ref[...] = gate[:, 0:D] * a_ref[...] + gate[:, D:2 * D] * b2_ref[...] + gate[:, 2 * D:3 * D] * c_ref[...]

    row = pl.BlockSpec((blk, D), lambda i: (i, 0))
    return pl.pallas_call(
        body, name=name, grid=(t // blk,),
        in_specs=[pl.BlockSpec((blk, NG), lambda i: (i, 0)), pl.BlockSpec((1, NG), lambda i: (0, 0)), row, row, row],
        out_specs=row, out_shape=jax.ShapeDtypeStruct((t, D), F32),
        compiler_params=_cparams("parallel"))(pg, bg, ya, yb, yc)


def _merge_bwd(dm, pg, bg, ya, yb, yc, name):
    t = pg.shape[0]
    blk = _tile(t, 256)

    def body(dm_ref, g_ref, b_ref, a_ref, b2_ref, c_ref, dg_ref, da_ref, db_ref, dc_ref, dbg_ref):
        @pl.when(pl.program_id(0) == 0)
        def _():
            dbg_ref[...] = jnp.zeros_like(dbg_ref)

        gate = _sigmoid(g_ref[...] + b_ref[...])
        dmv = dm_ref[...]
        for j, (y_ref, dy_ref) in enumerate(((a_ref, da_ref), (b2_ref, db_ref), (c_ref, dc_ref))):
            gj = gate[:, j * D:(j + 1) * D]
            dy_ref[...] = dmv * gj
            dg_ref[:, j * D:(j + 1) * D] = dmv * y_ref[...] * gj * (1.0 - gj)
        dbg_ref[...] += jnp.sum(dg_ref[...], axis=0, keepdims=True)

    row = pl.BlockSpec((blk, D), lambda i: (i, 0))
    wide = pl.BlockSpec((blk, NG), lambda i: (i, 0))
    vec = pl.BlockSpec((1, NG), lambda i: (0, 0))
    return pl.pallas_call(
        body, name=name, grid=(t // blk,), in_specs=[row, wide, vec, row, row, row],
        out_specs=[wide, row, row, row, vec],
        out_shape=[jax.ShapeDtypeStruct((t, NG), F32)] + [jax.ShapeDtypeStruct((t, D), F32)] * 3
                  + [jax.ShapeDtypeStruct((1, NG), F32)],
        compiler_params=_cparams("arbitrary"))(dm, pg, bg, ya, yb, yc)


def _adamw_math(w, g, m, v):
    m = ADAM_B1 * m + (1.0 - ADAM_B1) * g
    v = ADAM_B2 * v + (1.0 - ADAM_B2) * (g * g)
    m_hat = m / (1.0 - ADAM_B1 ** ADAM_STEP)
    v_hat = v / (1.0 - ADAM_B2 ** ADAM_STEP)
    delta = -ADAM_LR * (m_hat / (jnp.sqrt(v_hat) + ADAM_EPS) + ADAM_WD * w)
    return delta, m, v


def _sum_adamw(parts, w, m, v, name):
    r, c = w.shape
    br = r if r <= 256 else 256
    assert r % br == 0

    def body(p_ref, w_ref, m_ref, v_ref, g_ref, d_ref, nm_ref, nv_ref):
        g = p_ref[0]
        for k in range(1, N_DEV):
            g = g + p_ref[k]
        g_ref[...] = g
        d_ref[...], nm_ref[...], nv_ref[...] = _adamw_math(w_ref[...], g, m_ref[...], v_ref[...])

    blk = pl.BlockSpec((br, c), lambda i: (i, 0))
    return pl.pallas_call(
        body, name=name, grid=(r // br,),
        in_specs=[pl.BlockSpec((N_DEV, br, c), lambda i: (0, i, 0)), blk, blk, blk], out_specs=[blk] * 4,
        out_shape=[jax.ShapeDtypeStruct((r, c), F32)] * 4, compiler_params=_cparams("parallel"))(parts, w, m, v)


def _adamw(g, w, m, v, name):
    def body(g_ref, w_ref, m_ref, v_ref, d_ref, nm_ref, nv_ref):
        d_ref[...], nm_ref[...], nv_ref[...] = _adamw_math(w_ref[...], g_ref[...], m_ref[...], v_ref[...])

    return pl.pallas_call(body, name=name, out_shape=[jax.ShapeDtypeStruct(w.shape, F32)] * 3)(g, w, m, v)


def _sum_slots(parts, name):
    def body(p_ref, o_ref):
        g = p_ref[0]
        for k in range(1, N_DEV):
            g = g + p_ref[k]
        o_ref[...] = g

    return pl.pallas_call(body, name=name, out_shape=jax.ShapeDtypeStruct(parts.shape[1:], F32))(parts)


def _exchange(srcs, name, broadcast):
    n = len(srcs)
    out_shapes = [jax.ShapeDtypeStruct((N_DEV,) + (s.shape if broadcast else s.shape[1:]), s.dtype) for s in srcs]

    def body(*refs):
        src_refs, dst_refs = refs[:n], refs[n:2 * n]
        send_sems, recv_sems, local_sems = refs[2 * n:]
        x, y, c = lax.axis_index("x"), lax.axis_index("y"), lax.axis_index("c")
        me = 4 * x + 2 * y + c
        copies = []
        for k in range(1, N_DEV):
            px = 1 - x if (k >> 2) & 1 else x
            py = 1 - y if (k >> 1) & 1 else y
            pc = 1 - c if k & 1 else c
            peer = 4 * px + 2 * py + pc
            for a in range(n):
                cp = pltpu.make_async_remote_copy(
                    src_ref=src_refs[a] if broadcast else src_refs[a].at[peer], dst_ref=dst_refs[a].at[me],
                    send_sem=send_sems.at[k - 1, a], recv_sem=recv_sems.at[k - 1, a],
                    device_id=(px, py, pc), device_id_type=MESH)
                cp.start()
                copies.append(cp)
        local = []
        for a in range(n):
            cp = pltpu.make_async_copy(src_refs[a] if broadcast else src_refs[a].at[me], dst_refs[a].at[me],
                                       local_sems.at[a])
            cp.start()
            local.append(cp)
        for cp in copies:
            cp.wait()
        for cp in local:
            cp.wait()

    hbm = pl.BlockSpec(memory_space=pltpu.HBM)
    return pl.pallas_call(
        body, name=name, in_specs=[hbm] * n, out_specs=[hbm] * n, out_shape=out_shapes,
        scratch_shapes=[pltpu.SemaphoreType.DMA((N_DEV - 1, n)), pltpu.SemaphoreType.DMA((N_DEV - 1, n)),
                        pltpu.SemaphoreType.DMA((n,))])(*srcs)


def _regroup_w_in(w):
    wa = w[:, OFF_A:OFF_A + NA]
    seg = lambda off, h, n=HD: w[:, off + h * n: off + (h + 1) * n]
    wb = jnp.concatenate([seg(OFF_B + s * 512, h) for h in range(HGRN_HEADS) for s in range(4)], axis=1)
    parts = []
    for h in range(GDN_QK_HEADS):
        small = jnp.concatenate(
            [w[:, OFF_BETA + 2 * h: OFF_BETA + 2 * h + 2], w[:, OFF_CA + 2 * h: OFF_CA + 2 * h + 2],
             jnp.zeros((w.shape[0], HD - 4), w.dtype)], axis=1)
        parts += [seg(OFF_CQ, h), seg(OFF_CK, h), seg(OFF_CV, h, 2 * HD), seg(OFF_CZ, h, 2 * HD), small]
    wc = jnp.concatenate(parts, axis=1)
    wg = w[:, OFF_G:OFF_G + NG]
    return wa, wb, wc, wg


def _ungroup_dw_in(da, db, dc, dg):
    bq = [jnp.concatenate([db[:, h * 512 + s * HD: h * 512 + (s + 1) * HD] for h in range(HGRN_HEADS)], axis=1)
          for s in range(4)]
    ch = lambda h, lo, hi: dc[:, h * C_HEAD + lo: h * C_HEAD + hi]
    heads = range(GDN_QK_HEADS)
    cq = jnp.concatenate([ch(h, 0, HD) for h in heads], axis=1)
    ck = jnp.concatenate([ch(h, HD, 2 * HD) for h in heads], axis=1)
    cv = jnp.concatenate([ch(h, 2 * HD, 4 * HD) for h in heads], axis=1)
    cz = jnp.concatenate([ch(h, 4 * HD, 6 * HD) for h in heads], axis=1)
    cbeta = jnp.concatenate([ch(h, 6 * HD, 6 * HD + 2) for h in heads], axis=1)
    ca = jnp.concatenate([ch(h, 6 * HD + 2, 6 * HD + 4) for h in heads], axis=1)
    return jnp.concatenate([da] + bq + [cq, ck, cv, cbeta, ca, cz, dg], axis=1)


def _regroup_conv_c(cw):
    parts = []
    for h in range(GDN_QK_HEADS):
        parts += [cw[:, h * HD:(h + 1) * HD], cw[:, 512 + h * HD: 512 + (h + 1) * HD],
                  cw[:, 1024 + 2 * h * HD: 1024 + (2 * h + 2) * HD]]
    return jnp.concatenate(parts, axis=1)


def _ungroup_conv_c(d):
    heads = range(GDN_QK_HEADS)
    q = jnp.concatenate([d[:, h * 512: h * 512 + HD] for h in heads], axis=1)
    k = jnp.concatenate([d[:, h * 512 + HD: h * 512 + 2 * HD] for h in heads], axis=1)
    v = jnp.concatenate([d[:, h * 512 + 2 * HD: h * 512 + 4 * HD] for h in heads], axis=1)
    return jnp.concatenate([q, k, v], axis=1)


def _numel(shape):
    n = 1
    for d in shape:
        n *= d
    return n


def _pack(arrays, rows):
    flat = jnp.concatenate([a.reshape(-1) for a in arrays])
    return jnp.pad(flat, (0, rows * 128 - flat.shape[0])).reshape(rows, 128)


def _unpack(packed, shapes):
    flat = packed.reshape(-1)
    out, off = [], 0
    for s in shapes:
        out.append(flat[off:off + _numel(s)].reshape(s))
        off += _numel(s)
    return out


def _rows_for(shapes):
    return -(-sum(_numel(s) for s in shapes) // 1024) * 8


def kernel(x, norm_w, w_in, b_gate, conv_a, conv_c, a_log, dt_bias, lower_bounds, hgrn_norm_w, gdn_norm_w, w_out_a, w_out_b, w_out_c, w_o, final_norm_w, loss_target, m_norm_w, m_w_in, m_b_gate, m_conv_a, m_conv_c, m_a_log, m_dt_bias, m_lower_bounds, m_hgrn_norm_w, m_gdn_norm_w, m_w_out_a, m_w_out_b, m_w_out_c, m_w_o, m_final_norm_w, v_norm_w, v_w_in, v_b_gate, v_conv_a, v_conv_c, v_a_log, v_dt_bias, v_lower_bounds, v_hgrn_norm_w, v_gdn_norm_w, v_w_out_a, v_w_out_b, v_w_out_c, v_w_o, v_final_norm_w):
    me = 4 * lax.axis_index("x") + 2 * lax.axis_index("y") + lax.axis_index("c")
    xs = x[0]
    target = loss_target[0]
    in_shard = w_in.shape[2]

    big = [w_in, w_out_a, w_out_b, w_out_c, w_o]
    g_in, g_oa, g_ob, g_oc, g_o = _exchange([w.astype(BF16) for w in big], "gather_weights", broadcast=True)
    conv_shapes = [(DEPTH, 3, CONV_W), (DEPTH, 4, 2048)]
    conv_rows = _rows_for(conv_shapes)
    ca_full = lax.dynamic_update_slice(jnp.zeros(conv_shapes[0], F32), conv_a, (0, 0, me * conv_a.shape[2]))
    cc_full = lax.dynamic_update_slice(jnp.zeros(conv_shapes[1], F32), conv_c, (0, 0, me * conv_c.shape[2]))
    conv_parts, = _exchange([_pack([ca_full, cc_full], conv_rows)], "gather_conv", broadcast=True)
    conv_a_full, conv_c_full = _unpack(_sum_slots(conv_parts, "sum_conv"), conv_shapes)

    lb_pad = jnp.pad(lower_bounds, ((0, 8 - DEPTH), (0, 0)))
    lbs = _lower_bounds_fwd(lb_pad, "lower_bounds_fwd")

    layers = []
    for l in range(DEPTH):
        w_full = jnp.concatenate([g_in[q, l] for q in range(N_DEV)], axis=1)
        wa, wb, wc, wg = _regroup_w_in(w_full)
        woa = jnp.concatenate([g_oa[q, l] for q in range(N_DEV)], axis=1)
        wob = jnp.concatenate([g_ob[q, l] for q in range(N_DEV)], axis=1)
        woc = g_oc[:, l].reshape(D, D)
        wo = g_o[:, l].reshape(D, D)
        lanes = lambda vec: jnp.pad(vec.reshape(GDN_QK_HEADS, 1, 2), ((0, 0), (0, 0), (0, HD - 2)))
        cpar = jnp.concatenate([lanes(a_log[l]), lanes(dt_bias[l]), jnp.zeros((GDN_QK_HEADS, 6, HD), F32)], axis=1)
        layers.append(dict(
            wa=wa, wb=wb, wc=wc, wg=wg, woa=woa, wob=wob, woc=woc, wo=wo, cpar=cpar,
            nw=norm_w[l:l + 1], bg=b_gate[l:l + 1], cwa=conv_a_full[l], cwc=_regroup_conv_c(conv_c_full[l]),
            lb=lbs[l:l + 1], hnw=hgrn_norm_w[l:l + 1], gnw=gdn_norm_w[l:l + 1]))

    saved = []
    cur = xs
    for l, L in enumerate(layers):
        n = f"l{l}_"
        h = _rmsnorm_fwd(cur, L["nw"], n + "rms")
        pa = _matmul(h, L["wa"], "nn", n + "proj_a")
        pb = _matmul(h, L["wb"], "nn", n + "proj_b")
        pc = _matmul(h, L["wc"], "nn", n + "proj_c")
        pg = _matmul(h, L["wg"], "nn", n + "proj_g")
        ua = _branch_a_fwd(pa, L["cwa"], n + "conv_fwd")
        ub, sb = _branch_b_fwd(pb, L["lb"], L["hnw"], n + "hgrn_fwd")
        uc, sc = _branch_c_fwd(pc, L["cwc"], L["cpar"], L["gnw"], n + "gdn_fwd")
        ya = _matmul(ua, L["woa"], "nn", n + "out_a")
        yb = _matmul(ub, L["wob"], "nn", n + "out_b")
        yc = _matmul(uc, L["woc"], "nn", n + "out_c")
        merged = _merge_fwd(pg, L["bg"], ya, yb, yc, n + "merge")
        nxt = _matmul(merged, L["wo"], "nn", n + "out_o", residual=cur)
        saved.append(dict(x=cur, h=h, pa=pa, pb=pb, pc=pc, pg=pg, ua=ua, ub=ub, uc=uc, sb=sb, sc=sc,
                          ya=ya, yb=yb, yc=yc, merged=merged))
        cur = nxt

    loss_part, dx, d_final = _loss_head(cur, final_norm_w.reshape(1, D), target, "loss_head")

    grads = [None] * DEPTH
    dlbs_rows = [None] * DEPTH
    for l in reversed(range(DEPTH)):
        L, S = layers[l], saved[l]
        n = f"l{l}_"
        dmerged = _matmul(dx, L["wo"], "nt", n + "d_merged")
        d_wo = _matmul(S["merged"], dx, "tn", n + "dw_o")
        dpg, dya, dyb, dyc, d_bg = _merge_bwd(dmerged, S["pg"], L["bg"], S["ya"], S["yb"], S["yc"], n + "merge_bwd")
        dua = _matmul(dya, L["woa"], "nt", n + "d_ua")
        dub = _matmul(dyb, L["wob"], "nt", n + "d_ub")
        duc = _matmul(dyc, L["woc"], "nt", n + "d_uc")
        d_woa = _matmul(S["ua"], dya, "tn", n + "dw_out_a")
        d_wob = _matmul(S["ub"], dyb, "tn", n + "dw_out_b")
        d_woc = _matmul(S["uc"], dyc, "tn", n + "dw_out_c")
        dpa, d_cwa = _branch_a_bwd(S["pa"], L["cwa"], dua, n + "conv_bwd")
        dpb, d_lb, d_hnw = _branch_b_bwd(S["pb"], S["sb"], L["lb"], L["hnw"], dub, n + "hgrn_bwd")
        dpc, d_cwc, d_cpar, d_gnw = _branch_c_bwd(S["pc"], S["sc"], L["cwc"], L["cpar"], L["gnw"], duc, n + "gdn_bwd")
        dh = _matmul(dpa, L["wa"], "nt", n + "dh_a")
        dh = _matmul(dpb, L["wb"], "nt", n + "dh_b", residual=dh)
        dh = _matmul(dpc, L["wc"], "nt", n + "dh_c", residual=dh)
        dh = _matmul(dpg, L["wg"], "nt", n + "dh_g", residual=dh)
        d_win = _ungroup_dw_in(_matmul(S["h"], dpa, "tn", n + "dw_a"), _matmul(S["h"], dpb, "tn", n + "dw_b"),
                               _matmul(S["h"], dpc, "tn", n + "dw_c"), _matmul(S["h"], dpg, "tn", n + "dw_g"))
        dx, d_nw = _rmsnorm_bwd(dh, S["x"], L["nw"], dx, n + "rms_bwd")
        dlbs_rows[l] = d_lb
        grads[l] = dict(w_in=d_win, w_out_a=d_woa, w_out_b=d_wob, w_out_c=d_woc, w_o=d_wo, norm_w=d_nw[0],
                        b_gate=d_bg[0], conv_a=d_cwa, conv_c=_ungroup_conv_c(d_cwc),
                        a_log=d_cpar[:, 0, 0:2].reshape(-1), dt_bias=d_cpar[:, 1, 0:2].reshape(-1),
                        hgrn_norm_w=d_hnw[0], gdn_norm_w=d_gnw[0])
    grad_x = dx[None]
    d_lower = _lower_bounds_bwd(lb_pad, jnp.pad(jnp.concatenate(dlbs_rows, axis=0), ((0, 8 - DEPTH), (0, 0))),
                                "lower_bounds_bwd")[:DEPTH]

    stack = lambda name: jnp.stack([grads[l][name] for l in range(DEPTH)])
    g_win = stack("w_in")
    send_in = jnp.stack([g_win[:, :, p * in_shard:(p + 1) * in_shard] for p in range(N_DEV)]).reshape(N_DEV, DEPTH * D, in_shard)
    g_a, g_b = stack("w_out_a"), stack("w_out_b")
    send_oa = jnp.stack([g_a[:, :, p * 128:(p + 1) * 128] for p in range(N_DEV)]).reshape(N_DEV, DEPTH * 512, 128)
    send_ob = jnp.stack([g_b[:, :, p * 128:(p + 1) * 128] for p in range(N_DEV)]).reshape(N_DEV, DEPTH * 512, 128)
    send_oc = stack("w_out_c").reshape(DEPTH, N_DEV, 128, D).transpose(1, 0, 2, 3).reshape(N_DEV, DEPTH * 128, D)
    send_o = stack("w_o").reshape(DEPTH, N_DEV, 128, D).transpose(1, 0, 2, 3).reshape(N_DEV, DEPTH * 128, D)
    recv = _exchange([send_in, send_oa, send_ob, send_oc, send_o], "exchange_grads", broadcast=False)

    big_out = {}
    for name, parts, w, m, v in (("w_in", recv[0], w_in, m_w_in, v_w_in), ("w_out_a", recv[1], w_out_a, m_w_out_a, v_w_out_a),
                                 ("w_out_b", recv[2], w_out_b, m_w_out_b, v_w_out_b), ("w_out_c", recv[3], w_out_c, m_w_out_c, v_w_out_c),
                                 ("w_o", recv[4], w_o, m_w_o, v_w_o)):
        r2 = lambda a: a.reshape(parts.shape[1], parts.shape[2])
        outs = _sum_adamw(parts, r2(w), r2(m), r2(v), "adamw_" + name)
        big_out[name] = [o.reshape(w.shape) for o in outs]

    small_names = ["norm_w", "b_gate", "conv_a", "conv_c", "a_log", "dt_bias", "lower_bounds", "hgrn_norm_w",
                   "gdn_norm_w", "final_norm_w", "loss"]
    small_vals = {k: stack(k) for k in ("norm_w", "b_gate", "conv_a", "conv_c", "a_log", "dt_bias", "hgrn_norm_w", "gdn_norm_w")}
    small_vals.update(lower_bounds=d_lower, final_norm_w=d_final[0], loss=loss_part.reshape(1))
    small_shapes = [small_vals[k].shape for k in small_names]
    small_rows = _rows_for(small_shapes)
    small_parts, = _exchange([_pack([small_vals[k] for k in small_names], small_rows)], "exchange_small", broadcast=True)
    total = dict(zip(small_names, _unpack(_sum_slots(small_parts, "sum_small"), small_shapes)))
    loss = total["loss"][0]
    g_conv_a = lax.dynamic_slice(total["conv_a"], (0, 0, me * conv_a.shape[2]), conv_a.shape)
    g_conv_c = lax.dynamic_slice(total["conv_c"], (0, 0, me * conv_c.shape[2]), conv_c.shape)

    small_w = dict(norm_w=(norm_w, m_norm_w, v_norm_w), b_gate=(b_gate, m_b_gate, v_b_gate),
                   conv_a=(conv_a, m_conv_a, v_conv_a), conv_c=(conv_c, m_conv_c, v_conv_c),
                   a_log=(a_log, m_a_log, v_a_log), dt_bias=(dt_bias, m_dt_bias, v_dt_bias),
                   lower_bounds=(lower_bounds, m_lower_bounds, v_lower_bounds),
                   hgrn_norm_w=(hgrn_norm_w, m_hgrn_norm_w, v_hgrn_norm_w), gdn_norm_w=(gdn_norm_w, m_gdn_norm_w, v_gdn_norm_w),
                   final_norm_w=(final_norm_w, m_final_norm_w, v_final_norm_w))
    small_g = dict(total, conv_a=g_conv_a, conv_c=g_conv_c)
    upd_names = small_names[:-1]
    upd_shapes = [small_w[k][0].shape for k in upd_names]
    upd_rows = _rows_for(upd_shapes)
    pk = lambda j: _pack([small_w[k][j] for k in upd_names], upd_rows)
    s_delta, s_m, s_v = _adamw(_pack([small_g[k] for k in upd_names], upd_rows), pk(0), pk(1), pk(2), "adamw_small")
    small_out = {k: [small_g[k], d, mm, vv] for k, d, mm, vv in
                 zip(upd_names, _unpack(s_delta, upd_shapes), _unpack(s_m, upd_shapes), _unpack(s_v, upd_shapes))}

    order = ["norm_w", "w_in", "b_gate", "conv_a", "conv_c", "a_log", "dt_bias", "lower_bounds", "hgrn_norm_w",
             "gdn_norm_w", "w_out_a", "w_out_b", "w_out_c", "w_o", "final_norm_w"]
    res = {**small_out, **big_out}
    outs = [loss, grad_x]
    for j in range(4):
        outs += [res[k][j] for k in order]
    return tuple(outs)
```

```python
import functools

import jax
import jax.numpy as jnp
from jax import lax
from jax.experimental import pallas as pl
from jax.experimental.pallas import tpu as pltpu

F32 = jnp.float32
BF16 = jnp.bfloat16
H3 = lax.Precision.HIGH
MESH = pl.DeviceIdType.MESH

N_DEV = 8
D = 1024
DEPTH = 2
CHUNK = 64
SUB = 16
HGRN_BLOCK_CHUNKS = 4
GDN_BLOCK_CHUNKS = 8
GROUP = 128
NORM_EPS = 1e-6
L2_EPS = 1e-6
MIN_F = 1e-30
HD = 128
HGRN_HEADS = 4
GDN_QK_HEADS = 4
CONV_W = 512
IN_COLS = 10256
OFF_A, OFF_B, OFF_CQ, OFF_CK, OFF_CV, OFF_BETA, OFF_CA, OFF_CZ, OFF_G = (
    0, 2048, 4096, 4608, 5120, 6144, 6152, 6160, 7184)
NA, NB, NC_COLS, NG = 2048, 2048, 3584, 3072
C_HEAD = 896

ADAM_LR, ADAM_B1, ADAM_B2, ADAM_EPS, ADAM_WD, ADAM_STEP = 0.001, 0.9, 0.999, 1e-08, 0.01, 10

VMEM_LIMIT = 56 * 1024 * 1024


def _cparams(*sem):
    return pltpu.CompilerParams(dimension_semantics=sem, vmem_limit_bytes=VMEM_LIMIT)


def _tile(dim, cap):
    if dim <= cap:
        return dim
    t = (cap // 128) * 128
    while dim % t:
        t -= 128
    return t


def _sigmoid(x):
    return 1.0 / (1.0 + jnp.exp(-x))


def _silu(x):
    return x * _sigmoid(x)


def _softplus(x):
    return jnp.maximum(x, 0.0) + jnp.log(1.0 + jnp.exp(-jnp.abs(x)))


def _dot(a, b, dims, precision=None):
    if precision is None:
        a, b = a.astype(BF16), b.astype(BF16)
    return lax.dot_general(a, b, (dims, ((), ())), precision=precision, preferred_element_type=F32)


def _nn(a, b, precision=None):
    return _dot(a, b, ((1,), (0,)), precision)


def _nt(a, b, precision=None):
    return _dot(a, b, ((1,), (1,)), precision)


def _tn(a, b, precision=None):
    return _dot(a, b, ((0,), (0,)), precision)


def _sum_rows_exact(mat01, x):
    m = mat01.astype(BF16)
    hi = x.astype(BF16)
    rest = x - hi.astype(F32)
    mid = rest.astype(BF16)
    low = (rest - mid.astype(F32)).astype(BF16)
    return _nn(m, hi) + _nn(m, mid) + _nn(m, low)


@functools.partial(jax.custom_vjp, nondiff_argnums=(1,))
def _shift_rows(x, d):
    return x if d == 0 else pltpu.roll(x, d, 0)


def _shift_rows_fwd(x, d):
    return _shift_rows(x, d), None


def _shift_rows_bwd(d, _, ct):
    return ((ct if d == 0 else pltpu.roll(ct, ct.shape[0] - d, 0)),)


_shift_rows.defvjp(_shift_rows_fwd, _shift_rows_bwd)


def _iota2(shape):
    return lax.broadcasted_iota(jnp.int32, shape, 0), lax.broadcasted_iota(jnp.int32, shape, 1)


def _lane_pick(x, i):
    lane = lax.broadcasted_iota(jnp.int32, x.shape, 1)
    return jnp.sum(jnp.where(lane == i, x, 0.0), axis=1, keepdims=True)


def _hgrn_block(qr, fr, ir, zr, st0, lb, nw):
    rows = qr.shape[0]
    r, c = _iota2((CHUNK, CHUNK))
    sr, sc = r // SUB, c // SUB
    mats = [c <= r,
            (c <= r) & (sr == sc),
            (c > r) & (sr == sc),
            c > r,
            sc == sr - 1,
            (sc == sr - 1) | (sc == sr - 2)]
    stack = jnp.concatenate([m.astype(F32) for m in mats], axis=0)
    dist = sr - sc

    q = _silu(qr) * (HD ** -0.5)
    fg = lb + (1.0 - lb) * _sigmoid(fr)
    logf = jnp.log(jnp.maximum(fg, MIN_F))
    kk = 1.0 - fg
    v = ir

    chunks = [slice(s, s + CHUNK) for s in range(0, rows, CHUNK)]
    cums = [_sum_rows_exact(stack, logf[sl]) for sl in chunks]
    part = lambda i: jnp.concatenate([cs[i * CHUNK:(i + 1) * CHUNK] for cs in cums], axis=0)
    g, gl, gr, grc, c2, c3 = [part(i) for i in range(6)]
    o = jnp.zeros_like(v)
    row = lax.broadcasted_iota(jnp.int32, (rows, HD), 0) % SUB
    for d in range(SUB):
        ok = row >= d
        diff = gl - _shift_rows(gl, d)
        dec = jnp.where(ok, jnp.exp(jnp.where(ok, diff, 0.0)), 0.0)
        term = jnp.sum(q * dec * _shift_rows(kk, d), axis=1, keepdims=True)
        o += term * _shift_rows(v, d)

    qg = q * jnp.exp(g)
    qt = q * jnp.exp(gl)
    qt2 = qt * jnp.exp(c2)
    qt3 = qt * jnp.exp(c3)
    kh = kk * jnp.exp(gr)
    ks = kk * jnp.exp(grc)
    st = st0
    outs = []
    for sl in chunks:
        scores = jnp.where(dist == 1, _nt(qt[sl], kh[sl]), 0.0)
        scores += jnp.where(dist == 2, _nt(qt2[sl], kh[sl]), 0.0)
        scores += jnp.where(dist == 3, _nt(qt3[sl], kh[sl]), 0.0)
        outs.append(_nn(scores, v[sl]) + _nt(qg[sl], st))
        st = st * jnp.exp(jnp.sum(logf[sl], axis=0, keepdims=True)) + _tn(v[sl], ks[sl])
    o += jnp.concatenate(outs, axis=0)
    y = o * lax.rsqrt(jnp.mean(o * o, axis=1, keepdims=True) + NORM_EPS) * nw * _silu(zr)
    return y, st


@jax.custom_vjp
def _unit_lower_inverses(ms):
    r, c = _iota2(ms[0].shape)
    xs = [jnp.where(r == c, 1.0, 0.0) - jnp.where((r // 2) == (c // 2), m, 0.0) for m in ms]
    b = 2
    while b < CHUNK:
        pick = ((r // (2 * b)) == (c // (2 * b))) & ((r // b) != (c // b))
        ts = [_nn(x, jnp.where(pick, m, 0.0), H3) for x, m in zip(xs, ms)]
        xs = [x - _nn(t, x, H3) for x, t in zip(xs, ts)]
        b *= 2
    return tuple(xs)


def _unit_lower_inverses_fwd(ms):
    xs = _unit_lower_inverses(ms)
    return xs, xs


def _unit_lower_inverses_bwd(xs, cts):
    r, c = _iota2(xs[0].shape)
    keep = (c < r) & ((r // CHUNK) == (c // CHUNK))
    ts = [_tn(x, ct, H3) for x, ct in zip(xs, cts)]
    return (tuple(jnp.where(keep, -_nt(t, x, H3), 0.0) for t, x in zip(ts, xs)),)


_unit_lower_inverses.defvjp(_unit_lower_inverses_fwd, _unit_lower_inverses_bwd)


def _chunk_cumsum(x):
    row = lax.broadcasted_iota(jnp.int32, x.shape, 0) % CHUNK
    d = 1
    while d < CHUNK:
        x = x + jnp.where(row >= d, _shift_rows(x, d), 0.0)
        d *= 2
    return x


def _gdn_block(x_ext, z, ba, s0a, s0b, w0, w1, w2, w3, alog, dtb, nw):
    rows = z.shape[0]
    conv = (w0 * _shift_rows(x_ext, 3) + w1 * _shift_rows(x_ext, 2) + w2 * _shift_rows(x_ext, 1) + w3 * x_ext)
    cc = _silu(conv[8:])
    qc, kc = cc[:, 0:HD], cc[:, HD:2 * HD]
    q = qc * lax.rsqrt(jnp.sum(qc * qc, axis=1, keepdims=True) + L2_EPS) * (HD ** -0.5)
    k = kc * lax.rsqrt(jnp.sum(kc * kc, axis=1, keepdims=True) + L2_EPS)

    r, c = _iota2((GROUP, GROUP))
    same = (r // CHUNK) == (c // CHUNK)
    causal, strict, eye = same & (c <= r), same & (c < r), r == c
    heads = (0, 1)
    groups = [slice(lo, lo + GROUP) for lo in range(0, rows, GROUP)]
    chunks = [slice(lo, lo + CHUNK) for lo in range(0, rows, CHUNK)]

    v, loga, g_w, kb, kg, qg = [], [], [], [], [], []
    for i in heads:
        v.append(cc[:, (2 + i) * HD:(3 + i) * HD])
        beta = _sigmoid(_lane_pick(ba, i))
        a_neg = -jnp.exp(_lane_pick(alog, i))
        loga.append(a_neg * _softplus(_lane_pick(ba, 2 + i) + _lane_pick(dtb, i)))
        g_w.append(_chunk_cumsum(jnp.broadcast_to(loga[i], (rows, HD))))
        kb.append(k * beta)
        kg.append(k * jnp.exp(g_w[i]))
        qg.append(q * jnp.exp(g_w[i]))

    systems = [(i, gs) for gs in groups for i in heads]
    dec_c, ms = [], []
    for i, gs in systems:
        g_sq = g_w[i][gs]
        g_row = jnp.sum(jnp.where(eye, g_sq, 0.0), axis=0, keepdims=True)
        diff = g_sq - g_row
        dec_c.append(jnp.where(causal, jnp.exp(jnp.where(causal, diff, 0.0)), 0.0))
        ms.append(jnp.where(strict, _nt(k[gs], kb[i][gs]) * dec_c[-1], 0.0))
    xs = _unit_lower_inverses(tuple(ms))
    u = [[None] * len(groups) for _ in heads]
    w = [[None] * len(groups) for _ in heads]
    qk = [[None] * len(groups) for _ in heads]
    for n, (i, gs) in enumerate(systems):
        j = n // len(heads)
        u[i][j] = _nn(xs[n], v[i][gs])
        w[i][j] = _nn(xs[n], kg[i][gs])
        qk[i][j] = _nt(q[gs], kb[i][gs]) * dec_c[n]
    u = [jnp.concatenate(p, axis=0) for p in u]
    w = [jnp.concatenate(p, axis=0) for p in w]

    decay, p_mat, q_mat = {}, {}, {}
    for n, sl in enumerate(chunks):
        for i in heads:
            g_last = jnp.sum(loga[i][sl], axis=0, keepdims=True)
            kd = kb[i][sl] * jnp.exp(g_last - g_w[i][sl])
            decay[n, i] = jnp.exp(g_last)
            p_mat[n, i] = -_tn(kd, w[i][sl])
            q_mat[n, i] = _tn(kd, u[i][sl])
    s = [s0a, s0b]
    s_at = {}
    for n in range(len(chunks)):
        for i in heads:
            s_at[n, i] = s[i]
            s[i] = s[i] * decay[n, i] + _nn(p_mat[n, i], s[i]) + q_mat[n, i]

    ys = []
    for i in heads:
        e = jnp.concatenate([u[i][sl] - _nn(w[i][sl], s_at[n, i]) for n, sl in enumerate(chunks)], axis=0)
        o_state = jnp.concatenate([_nn(qg[i][sl], s_at[n, i]) for n, sl in enumerate(chunks)], axis=0)
        o = o_state + jnp.concatenate([_nn(qk[i][j], e[gs]) for j, gs in enumerate(groups)], axis=0)
        zi = z[:, i * HD:(i + 1) * HD]
        ys.append(o * lax.rsqrt(jnp.mean(o * o, axis=1, keepdims=True) + NORM_EPS) * nw * _silu(zi))
    return jnp.concatenate(ys, axis=1), s[0], s[1]


def _conv_a_block(ab, ac_ext, ax_ext, az, w0, w1, w2):
    u = ac_ext * ax_ext
    conv = (w0 * _shift_rows(u, 2) + w1 * _shift_rows(u, 1) + w2 * u)[8:]
    return ab * conv * _silu(az)


def _matmul(a, b, mode, name, residual=None, out_dtype=F32):
    if mode == "nn":
        (m, k), n = a.shape, b.shape[1]
    elif mode == "nt":
        (m, k), n = a.shape, b.shape[0]
    else:
        (k, m), n = a.shape, b.shape[1]
    tm, tn, tk = _tile(m, 512 if mode != "tn" else 1024), _tile(n, 1024), _tile(k, 1024 if mode != "tn" else 512)
    nk = k // tk
    dims = {"nn": ((1,), (0,)), "nt": ((1,), (1,)), "tn": ((0,), (0,))}[mode]
    a_spec = pl.BlockSpec((tk, tm), lambda i, j, s: (s, i)) if mode == "tn" else pl.BlockSpec((tm, tk), lambda i, j, s: (i, s))
    b_spec = pl.BlockSpec((tn, tk), lambda i, j, s: (j, s)) if mode == "nt" else pl.BlockSpec((tk, tn), lambda i, j, s: (s, j))
    o_spec = pl.BlockSpec((tm, tn), lambda i, j, s: (i, j))
    has_res = residual is not None

    def body(*refs):
        a_ref, b_ref = refs[0], refs[1]
        r_ref = refs[2] if has_res else None
        o_ref, acc_ref = refs[-2], refs[-1]
        s = pl.program_id(2)

        @pl.when(s == 0)
        def _():
            acc_ref[...] = jnp.zeros_like(acc_ref)

        acc_ref[...] += _dot(a_ref[...], b_ref[...], dims)

        @pl.when(s == nk - 1)
        def _():
            out = acc_ref[...]
            if has_res:
                out = out + r_ref[...]
            o_ref[...] = out.astype(out_dtype)

    args, specs = [a, b], [a_spec, b_spec]
    if has_res:
        args.append(residual)
        specs.append(o_spec)
    return pl.pallas_call(
        body, name=name, grid=(m // tm, n // tn, nk), in_specs=specs, out_specs=o_spec,
        out_shape=jax.ShapeDtypeStruct((m, n), out_dtype), scratch_shapes=[pltpu.VMEM((tm, tn), F32)],
        compiler_params=_cparams("parallel", "parallel", "arbitrary"))(*args)


def _rmsnorm_fwd(x, w, name):
    t = x.shape[0]
    blk = _tile(t, 512)

    def body(x_ref, w_ref, h_ref):
        xv = x_ref[...]
        h_ref[...] = (xv * lax.rsqrt(jnp.mean(xv * xv, axis=1, keepdims=True) + NORM_EPS) * w_ref[...]).astype(BF16)

    return pl.pallas_call(
        body, name=name, grid=(t // blk,),
        in_specs=[pl.BlockSpec((blk, D), lambda i: (i, 0)), pl.BlockSpec((1, D), lambda i: (0, 0))],
        out_specs=pl.BlockSpec((blk, D), lambda i: (i, 0)), out_shape=jax.ShapeDtypeStruct((t, D), BF16),
        compiler_params=_cparams("parallel"))(x, w)


def _rmsnorm_bwd(dh, x, w, dxo, name):
    t = x.shape[0]
    blk = _tile(t, 512)

    def body(dh_ref, x_ref, w_ref, dxo_ref, dx_ref, dw_ref):
        @pl.when(pl.program_id(0) == 0)
        def _():
            dw_ref[...] = jnp.zeros_like(dw_ref)

        xv, dhv = x_ref[...], dh_ref[...]
        rs = lax.rsqrt(jnp.mean(xv * xv, axis=1, keepdims=True) + NORM_EPS)
        xh = xv * rs
        dw_ref[...] += jnp.sum(dhv * xh, axis=0, keepdims=True)
        dxh = dhv * w_ref[...]
        dx_ref[...] = rs * (dxh - xh * jnp.mean(dxh * xh, axis=1, keepdims=True)) + dxo_ref[...]

    row = pl.BlockSpec((blk, D), lambda i: (i, 0))
    vec = pl.BlockSpec((1, D), lambda i: (0, 0))
    return pl.pallas_call(
        body, name=name, grid=(t // blk,), in_specs=[row, row, vec, row], out_specs=[row, vec],
        out_shape=[jax.ShapeDtypeStruct((t, D), F32), jax.ShapeDtypeStruct((1, D), F32)],
        compiler_params=_cparams("arbitrary"))(dh, x, w, dxo)


def _loss_head(x, w, target, name):
    t = x.shape[0]
    blk = _tile(t, 512)

    def body(x_ref, w_ref, t_ref, loss_ref, dx_ref, dw_ref):
        @pl.when(pl.program_id(0) == 0)
        def _():
            dw_ref[...] = jnp.zeros_like(dw_ref)
            loss_ref[...] = jnp.zeros_like(loss_ref)

        xv = x_ref[...]
        rs = lax.rsqrt(jnp.mean(xv * xv, axis=1, keepdims=True) + NORM_EPS)
        xh = xv * rs
        err = xh * w_ref[...] - t_ref[...]
        loss_ref[...] += 0.5 * jnp.sum(jnp.mean(err * err, axis=1, keepdims=True), axis=0, keepdims=True)
        dy = err * (1.0 / D)
        dw_ref[...] += jnp.sum(dy * xh, axis=0, keepdims=True)
        dxh = dy * w_ref[...]
        dx_ref[...] = rs * (dxh - xh * jnp.mean(dxh * xh, axis=1, keepdims=True))

    row = pl.BlockSpec((blk, D), lambda i: (i, 0))
    vec = pl.BlockSpec((1, D), lambda i: (0, 0))
    return pl.pallas_call(
        body, name=name, grid=(t // blk,), in_specs=[row, vec, row],
        out_specs=[pl.BlockSpec((1, 1), lambda i: (0, 0)), row, vec],
        out_shape=[jax.ShapeDtypeStruct((1, 1), F32), jax.ShapeDtypeStruct((t, D), F32), jax.ShapeDtypeStruct((1, D), F32)],
        compiler_params=_cparams("arbitrary"))(x, w, target)


def _lbs_of(lb):
    r = lax.broadcasted_iota(jnp.int32, lb.shape, 0)
    real = r < DEPTH
    mx = lax.stop_gradient(jnp.max(jnp.where(real, lb, -jnp.inf), axis=0, keepdims=True))
    e = jnp.where(real, jnp.exp(jnp.where(real, lb - mx, 0.0)), 0.0)
    p = e / jnp.sum(e, axis=0, keepdims=True)
    out = jnp.zeros_like(lb)
    run = jnp.zeros_like(mx)
    for l in range(1, DEPTH):
        run = run + jnp.sum(jnp.where(r == l, p, 0.0), axis=0, keepdims=True)
        out = out + jnp.where(r == l, run, 0.0)
    return out


def _lower_bounds_fwd(lbp, name):
    def body(lb_ref, o_ref):
        o_ref[...] = _lbs_of(lb_ref[...])

    return pl.pallas_call(body, name=name, out_shape=jax.ShapeDtypeStruct(lbp.shape, F32))(lbp)


def _lower_bounds_bwd(lbp, dlbs, name):
    def body(lb_ref, d_ref, o_ref):
        _, vjp = jax.vjp(_lbs_of, lb_ref[...])
        o_ref[...] = vjp(d_ref[...])[0]

    return pl.pallas_call(body, name=name, out_shape=jax.ShapeDtypeStruct(lbp.shape, F32))(lbp, dlbs)


def _branch_a_fwd(pa, cw, name):
    t = pa.shape[0]
    blk = _tile(t, 512)
    W = CONV_W

    def body(p_ref, w_ref, y_ref, hc_ref, hx_ref):
        @pl.when(pl.program_id(0) == 0)
        def _():
            hc_ref[...] = jnp.zeros_like(hc_ref)
            hx_ref[...] = jnp.zeros_like(hx_ref)

        ac, ax = p_ref[:, W:2 * W], p_ref[:, 2 * W:3 * W]
        y_ref[...] = _conv_a_block(
            p_ref[:, 0:W], jnp.concatenate([hc_ref[...], ac], axis=0), jnp.concatenate([hx_ref[...], ax], axis=0),
            p_ref[:, 3 * W:4 * W], w_ref[0:1, :], w_ref[1:2, :], w_ref[2:3, :])
        hc_ref[...] = p_ref[blk - 8:blk, W:2 * W]
        hx_ref[...] = p_ref[blk - 8:blk, 2 * W:3 * W]

    return pl.pallas_call(
        body, name=name, grid=(t // blk,),
        in_specs=[pl.BlockSpec((blk, NA), lambda i: (i, 0)), pl.BlockSpec((3, W), lambda i: (0, 0))],
        out_specs=pl.BlockSpec((blk, W), lambda i: (i, 0)), out_shape=jax.ShapeDtypeStruct((t, W), F32),
        scratch_shapes=[pltpu.VMEM((8, W), F32), pltpu.VMEM((8, W), F32)],
        compiler_params=_cparams("arbitrary"))(pa, cw)


def _branch_a_bwd(pa, cw, dy, name):
    t = pa.shape[0]
    blk = _tile(t, 512)
    nt_ = t // blk
    W = CONV_W
    hb = blk // 8

    def body(p_ref, halo_ref, w_ref, dy_ref, dp_ref, dw_ref, chc_ref, chx_ref):
        i = pl.program_id(0)

        @pl.when(i == 0)
        def _():
            chc_ref[...] = jnp.zeros_like(chc_ref)
            chx_ref[...] = jnp.zeros_like(chx_ref)
            dw_ref[...] = jnp.zeros_like(dw_ref)

        keep = 1.0 - (i == nt_ - 1).astype(F32)
        hc = halo_ref[:, W:2 * W] * keep
        hx = halo_ref[:, 2 * W:3 * W] * keep
        ac_ext = jnp.concatenate([hc, p_ref[:, W:2 * W]], axis=0)
        ax_ext = jnp.concatenate([hx, p_ref[:, 2 * W:3 * W]], axis=0)
        _, vjp = jax.vjp(_conv_a_block, p_ref[:, 0:W], ac_ext, ax_ext, p_ref[:, 3 * W:4 * W],
                         w_ref[0:1, :], w_ref[1:2, :], w_ref[2:3, :])
        dab, dac, dax, daz, dw0, dw1, dw2 = vjp(dy_ref[...])
        dp_ref[:, 0:W] = dab
        dp_ref[:, W:2 * W] = dac[8:]
        dp_ref[:, 2 * W:3 * W] = dax[8:]
        dp_ref[:, 3 * W:4 * W] = daz
        dp_ref[blk - 8:blk, W:2 * W] += chc_ref[...]
        dp_ref[blk - 8:blk, 2 * W:3 * W] += chx_ref[...]
        chc_ref[...] = dac[:8] * keep
        chx_ref[...] = dax[:8] * keep
        dw_ref[0:1, :] += dw0
        dw_ref[1:2, :] += dw1
        dw_ref[2:3, :] += dw2

    rev = lambda i: (nt_ - 1 - i, 0)
    return pl.pallas_call(
        body, name=name, grid=(nt_,),
        in_specs=[pl.BlockSpec((blk, NA), rev),
                  pl.BlockSpec((8, NA), lambda i: (jnp.maximum((nt_ - 1 - i) * hb - 1, 0), 0)),
                  pl.BlockSpec((3, W), lambda i: (0, 0)),
                  pl.BlockSpec((blk, W), rev)],
        out_specs=[pl.BlockSpec((blk, NA), rev), pl.BlockSpec((3, W), lambda i: (0, 0))],
        out_shape=[jax.ShapeDtypeStruct((t, NA), F32), jax.ShapeDtypeStruct((3, W), F32)],
        scratch_shapes=[pltpu.VMEM((8, W), F32), pltpu.VMEM((8, W), F32)],
        compiler_params=_cparams("arbitrary"))(pa, pa, cw, dy)


def _block_rows(t, chunks):
    return min(t, chunks * CHUNK)


def _branch_b_fwd(pb, lbs_row, nw, name):
    t = pb.shape[0]
    rows = _block_rows(t, HGRN_BLOCK_CHUNKS)
    nch = t // rows

    def body(p_ref, lb_ref, nw_ref, y_ref, s_ref, st_ref):
        @pl.when(pl.program_id(1) == 0)
        def _():
            st_ref[...] = jnp.zeros_like(st_ref)

        s_ref[0, 0] = st_ref[...]
        y, st1 = _hgrn_block(p_ref[:, 0:HD], p_ref[:, HD:2 * HD], p_ref[:, 2 * HD:3 * HD], p_ref[:, 3 * HD:4 * HD],
                             st_ref[...], lb_ref[...], nw_ref[...])
        y_ref[...] = y
        st_ref[...] = st1

    return pl.pallas_call(
        body, name=name, grid=(HGRN_HEADS, nch),
        in_specs=[pl.BlockSpec((rows, 4 * HD), lambda h, i: (i, h)),
                  pl.BlockSpec((1, HD), lambda h, i: (0, h)),
                  pl.BlockSpec((1, HD), lambda h, i: (0, 0))],
        out_specs=[pl.BlockSpec((rows, HD), lambda h, i: (i, h)),
                   pl.BlockSpec((1, 1, HD, HD), lambda h, i: (h, i, 0, 0))],
        out_shape=[jax.ShapeDtypeStruct((t, HGRN_HEADS * HD), F32),
                   jax.ShapeDtypeStruct((HGRN_HEADS, nch, HD, HD), F32)],
        scratch_shapes=[pltpu.VMEM((HD, HD), F32)],
        compiler_params=_cparams("arbitrary", "arbitrary"))(pb, lbs_row, nw)


def _branch_b_bwd(pb, states, lbs_row, nw, dy, name):
    t = pb.shape[0]
    rows = _block_rows(t, HGRN_BLOCK_CHUNKS)
    nch = t // rows

    def body(p_ref, s_ref, lb_ref, nw_ref, dy_ref, dp_ref, dlb_ref, dnw_ref, ds_ref):
        h, i = pl.program_id(0), pl.program_id(1)

        @pl.when(i == 0)
        def _():
            ds_ref[...] = jnp.zeros_like(ds_ref)
            dlb_ref[...] = jnp.zeros_like(dlb_ref)

        @pl.when((i == 0) & (h == 0))
        def _():
            dnw_ref[...] = jnp.zeros_like(dnw_ref)

        _, vjp = jax.vjp(_hgrn_block, p_ref[:, 0:HD], p_ref[:, HD:2 * HD], p_ref[:, 2 * HD:3 * HD],
                         p_ref[:, 3 * HD:4 * HD], s_ref[0, 0], lb_ref[...], nw_ref[...])
        dq, df, di, dz, ds0, dlb, dnw = vjp((dy_ref[...], ds_ref[...]))
        dp_ref[:, 0:HD] = dq
        dp_ref[:, HD:2 * HD] = df
        dp_ref[:, 2 * HD:3 * HD] = di
        dp_ref[:, 3 * HD:4 * HD] = dz
        ds_ref[...] = ds0
        dlb_ref[...] += dlb
        dnw_ref[...] += dnw

    rev = lambda h, i: (nch - 1 - i, h)
    return pl.pallas_call(
        body, name=name, grid=(HGRN_HEADS, nch),
        in_specs=[pl.BlockSpec((rows, 4 * HD), rev),
                  pl.BlockSpec((1, 1, HD, HD), lambda h, i: (h, nch - 1 - i, 0, 0)),
                  pl.BlockSpec((1, HD), lambda h, i: (0, h)),
                  pl.BlockSpec((1, HD), lambda h, i: (0, 0)),
                  pl.BlockSpec((rows, HD), rev)],
        out_specs=[pl.BlockSpec((rows, 4 * HD), rev),
                   pl.BlockSpec((1, HD), lambda h, i: (0, h)),
                   pl.BlockSpec((1, HD), lambda h, i: (0, 0))],
        out_shape=[jax.ShapeDtypeStruct((t, NB), F32), jax.ShapeDtypeStruct((1, HGRN_HEADS * HD), F32),
                   jax.ShapeDtypeStruct((1, HD), F32)],
        scratch_shapes=[pltpu.VMEM((HD, HD), F32)],
        compiler_params=_cparams("arbitrary", "arbitrary"))(pb, states, lbs_row, nw, dy)


def _branch_c_fwd(pc, cw, cpar, nw, name, exchange=None):
    t = pc.shape[0]
    rows = _block_rows(t, GDN_BLOCK_CHUNKS)
    nch = t // rows
    XW = 4 * HD

    def body(p_ref, w_ref, cp_ref, nw_ref, y_ref, s_ref, sa_ref, sb_ref, halo_ref):
        @pl.when(pl.program_id(1) == 0)
        def _():
            sa_ref[...] = jnp.zeros_like(sa_ref)
            sb_ref[...] = jnp.zeros_like(sb_ref)
            halo_ref[...] = jnp.zeros_like(halo_ref)

        s_ref[0, 0, 0] = sa_ref[...]
        s_ref[0, 0, 1] = sb_ref[...]
        x_ext = jnp.concatenate([halo_ref[...], p_ref[:, 0:XW]], axis=0)
        y, s1a, s1b = _gdn_block(x_ext, p_ref[:, XW:XW + 2 * HD], p_ref[:, XW + 2 * HD:XW + 3 * HD],
                                 sa_ref[...], sb_ref[...], w_ref[0:1, :], w_ref[1:2, :], w_ref[2:3, :], w_ref[3:4, :],
                                 cp_ref[0, 0:1, :], cp_ref[0, 1:2, :], nw_ref[...])
        y_ref[...] = y
        sa_ref[...] = s1a
        sb_ref[...] = s1b
        halo_ref[...] = p_ref[rows - 8:rows, 0:XW]

    return _call_with_exchange(
        body, name=name, grid=(GDN_QK_HEADS, nch),
        in_specs=[pl.BlockSpec((rows, C_HEAD), lambda h, i: (i, h)),
                  pl.BlockSpec((4, XW), lambda h, i: (0, h)),
                  pl.BlockSpec((1, 8, HD), lambda h, i: (h, 0, 0)),
                  pl.BlockSpec((1, HD), lambda h, i: (0, 0))],
        out_specs=[pl.BlockSpec((rows, 2 * HD), lambda h, i: (i, h)),
                   pl.BlockSpec((1, 1, 2, HD, HD), lambda h, i: (h, i, 0, 0, 0))],
        out_shape=[jax.ShapeDtypeStruct((t, 2 * GDN_QK_HEADS * HD), F32),
                   jax.ShapeDtypeStruct((GDN_QK_HEADS, nch, 2, HD, HD), F32)],
        scratch_shapes=[pltpu.VMEM((HD, HD), F32), pltpu.VMEM((HD, HD), F32), pltpu.VMEM((8, XW), F32)],
        args=(pc, cw, cpar, nw), exchange=exchange)


def _branch_c_bwd(pc, states, cw, cpar, nw, dy, name, exchange=None):
    t = pc.shape[0]
    rows = _block_rows(t, GDN_BLOCK_CHUNKS)
    nch = t // rows
    XW = 4 * HD
    hb = rows // 8

    def body(p_ref, halo_ref, s_ref, w_ref, cp_ref, nw_ref, dy_ref, dp_ref, dw_ref, dcp_ref, dnw_ref,
             dsa_ref, dsb_ref, carry_ref):
        h, i = pl.program_id(0), pl.program_id(1)

        @pl.when(i == 0)
        def _():
            dsa_ref[...] = jnp.zeros_like(dsa_ref)
            dsb_ref[...] = jnp.zeros_like(dsb_ref)
            carry_ref[...] = jnp.zeros_like(carry_ref)
            dw_ref[...] = jnp.zeros_like(dw_ref)
            dcp_ref[...] = jnp.zeros_like(dcp_ref)

        @pl.when((i == 0) & (h == 0))
        def _():
            dnw_ref[...] = jnp.zeros_like(dnw_ref)

        keep = 1.0 - (i == nch - 1).astype(F32)
        x_ext = jnp.concatenate([halo_ref[:, 0:XW] * keep, p_ref[:, 0:XW]], axis=0)
        _, vjp = jax.vjp(_gdn_block, x_ext, p_ref[:, XW:XW + 2 * HD], p_ref[:, XW + 2 * HD:XW + 3 * HD],
                         s_ref[0, 0, 0], s_ref[0, 0, 1], w_ref[0:1, :], w_ref[1:2, :], w_ref[2:3, :], w_ref[3:4, :],
                         cp_ref[0, 0:1, :], cp_ref[0, 1:2, :], nw_ref[...])
        dx, dz, dba, dsa, dsb, dw0, dw1, dw2, dw3, dal, ddt, dnw = vjp((dy_ref[...], dsa_ref[...], dsb_ref[...]))
        dp_ref[:, 0:XW] = dx[8:]
        dp_ref[:, XW:XW + 2 * HD] = dz
        dp_ref[:, XW + 2 * HD:XW + 3 * HD] = dba
        dp_ref[rows - 8:rows, 0:XW] += carry_ref[...]
        carry_ref[...] = dx[:8] * keep
        dsa_ref[...] = dsa
        dsb_ref[...] = dsb
        dw_ref[0:1, :] += dw0
        dw_ref[1:2, :] += dw1
        dw_ref[2:3, :] += dw2
        dw_ref[3:4, :] += dw3
        dcp_ref[0, 0:1, :] += dal
        dcp_ref[0, 1:2, :] += ddt
        dnw_ref[...] += dnw

    rev = lambda h, i: (nch - 1 - i, h)
    return _call_with_exchange(
        body, name=name, grid=(GDN_QK_HEADS, nch),
        in_specs=[pl.BlockSpec((rows, C_HEAD), rev),
                  pl.BlockSpec((8, C_HEAD), lambda h, i: (jnp.maximum((nch - 1 - i) * hb - 1, 0), h)),
                  pl.BlockSpec((1, 1, 2, HD, HD), lambda h, i: (h, nch - 1 - i, 0, 0, 0)),
                  pl.BlockSpec((4, XW), lambda h, i: (0, h)),
                  pl.BlockSpec((1, 8, HD), lambda h, i: (h, 0, 0)),
                  pl.BlockSpec((1, HD), lambda h, i: (0, 0)),
                  pl.BlockSpec((rows, 2 * HD), rev)],
        out_specs=[pl.BlockSpec((rows, C_HEAD), rev),
                   pl.BlockSpec((4, XW), lambda h, i: (0, h)),
                   pl.BlockSpec((1, 8, HD), lambda h, i: (h, 0, 0)),
                   pl.BlockSpec((1, HD), lambda h, i: (0, 0))],
        out_shape=[jax.ShapeDtypeStruct((t, NC_COLS), F32), jax.ShapeDtypeStruct((4, GDN_QK_HEADS * XW), F32),
                   jax.ShapeDtypeStruct((GDN_QK_HEADS, 8, HD), F32), jax.ShapeDtypeStruct((1, HD), F32)],
        scratch_shapes=[pltpu.VMEM((HD, HD), F32), pltpu.VMEM((HD, HD), F32), pltpu.VMEM((8, XW), F32)],
        args=(pc, pc, states, cw, cpar, nw, dy), exchange=exchange)


def _merge_fwd(pg, bg, ya, yb, yc, name):
    t = pg.shape[0]
    blk = _tile(t, 256)

    def body(g_ref, b_ref, a_ref, b2_ref, c_ref, o_ref):
        gate = _sigmoid(g_ref[...] + b_ref[...])
        o_ref[...] = gate[:, 0:D] * a_ref[...] + gate[:, D:2 * D] * b2_ref[...] + gate[:, 2 * D:3 * D] * c_ref[...]

    row = pl.BlockSpec((blk, D), lambda i: (i, 0))
    return pl.pallas_call(
        body, name=name, grid=(t // blk,),
        in_specs=[pl.BlockSpec((blk, NG), lambda i: (i, 0)), pl.BlockSpec((1, NG), lambda i: (0, 0)), row, row, row],
        out_specs=row, out_shape=jax.ShapeDtypeStruct((t, D), F32),
        compiler_params=_cparams("parallel"))(pg, bg, ya, yb, yc)


def _merge_bwd(dm, pg, bg, ya, yb, yc, name):
    t = pg.shape[0]
    blk = _tile(t, 256)

    def body(dm_ref, g_ref, b_ref, a_ref, b2_ref, c_ref, dg_ref, da_ref, db_ref, dc_ref, dbg_ref):
        @pl.when(pl.program_id(0) == 0)
        def _():
            dbg_ref[...] = jnp.zeros_like(dbg_ref)

        gate = _sigmoid(g_ref[...] + b_ref[...])
        dmv = dm_ref[...]
        for j, (y_ref, dy_ref) in enumerate(((a_ref, da_ref), (b2_ref, db_ref), (c_ref, dc_ref))):
            gj = gate[:, j * D:(j + 1) * D]
            dy_ref[...] = dmv * gj
            dg_ref[:, j * D:(j + 1) * D] = dmv * y_ref[...] * gj * (1.0 - gj)
        dbg_ref[...] += jnp.sum(dg_ref[...], axis=0, keepdims=True)

    row = pl.BlockSpec((blk, D), lambda i: (i, 0))
    wide = pl.BlockSpec((blk, NG), lambda i: (i, 0))
    vec = pl.BlockSpec((1, NG), lambda i: (0, 0))
    return pl.pallas_call(
        body, name=name, grid=(t // blk,), in_specs=[row, wide, vec, row, row, row],
        out_specs=[wide, row, row, row, vec],
        out_shape=[jax.ShapeDtypeStruct((t, NG), F32)] + [jax.ShapeDtypeStruct((t, D), F32)] * 3
                  + [jax.ShapeDtypeStruct((1, NG), F32)],
        compiler_params=_cparams("arbitrary"))(dm, pg, bg, ya, yb, yc)


def _adamw_math(w, g, m, v):
    m = ADAM_B1 * m + (1.0 - ADAM_B1) * g
    v = ADAM_B2 * v + (1.0 - ADAM_B2) * (g * g)
    m_hat = m / (1.0 - ADAM_B1 ** ADAM_STEP)
    v_hat = v / (1.0 - ADAM_B2 ** ADAM_STEP)
    delta = -ADAM_LR * (m_hat / (jnp.sqrt(v_hat) + ADAM_EPS) + ADAM_WD * w)
    return delta, m, v


def _sum_adamw(parts, w, m, v, name):
    r, c = w.shape
    br = r if r <= 256 else 256
    assert r % br == 0

    def body(p_ref, w_ref, m_ref, v_ref, g_ref, d_ref, nm_ref, nv_ref):
        g = p_ref[0].astype(F32)
        for k in range(1, N_DEV):
            g = g + p_ref[k].astype(F32)
        g_ref[...] = g
        d_ref[...], nm_ref[...], nv_ref[...] = _adamw_math(w_ref[...], g, m_ref[...], v_ref[...])

    blk = pl.BlockSpec((br, c), lambda i: (i, 0))
    return pl.pallas_call(
        body, name=name, grid=(r // br,),
        in_specs=[pl.BlockSpec((N_DEV, br, c), lambda i: (0, i, 0)), blk, blk, blk], out_specs=[blk] * 4,
        out_shape=[jax.ShapeDtypeStruct((r, c), F32)] * 4, compiler_params=_cparams("parallel"))(parts, w, m, v)


def _adamw(g, w, m, v, name):
    def body(g_ref, w_ref, m_ref, v_ref, d_ref, nm_ref, nv_ref):
        d_ref[...], nm_ref[...], nv_ref[...] = _adamw_math(w_ref[...], g_ref[...], m_ref[...], v_ref[...])

    return pl.pallas_call(body, name=name, out_shape=[jax.ShapeDtypeStruct(w.shape, F32)] * 3)(g, w, m, v)


def _sum_slots(parts, name):
    def body(p_ref, o_ref):
        g = p_ref[0]
        for k in range(1, N_DEV):
            g = g + p_ref[k]
        o_ref[...] = g

    return pl.pallas_call(body, name=name, out_shape=jax.ShapeDtypeStruct(parts.shape[1:], F32))(parts)


def _exchange(srcs, name, broadcast):
    n = len(srcs)

    def body(*refs):
        copies = _exchange_copies(refs[:n], refs[n:2 * n], *refs[2 * n:], broadcast)
        for cp in copies:
            cp.start()
        for cp in copies:
            cp.wait()

    return pl.pallas_call(
        body, name=name, in_specs=[HBM_SPEC] * n, out_specs=[HBM_SPEC] * n, out_shape=_exchange_shapes(srcs, broadcast),
        scratch_shapes=_exchange_semaphores(n))(*srcs)


HBM_SPEC = pl.BlockSpec(memory_space=pltpu.HBM)


def _exchange_shapes(srcs, broadcast):
    return [jax.ShapeDtypeStruct((N_DEV,) + (s.shape if broadcast else s.shape[1:]), s.dtype) for s in srcs]


def _exchange_semaphores(n):
    return [pltpu.SemaphoreType.DMA((N_DEV - 1, n)), pltpu.SemaphoreType.DMA((N_DEV - 1, n)), pltpu.SemaphoreType.DMA((n,))]


def _exchange_copies(src_refs, dst_refs, send_sems, recv_sems, local_sems, broadcast):
    x, y, c = lax.axis_index("x"), lax.axis_index("y"), lax.axis_index("c")
    me = 4 * x + 2 * y + c
    copies = []
    for k in range(1, N_DEV):
        px = 1 - x if (k >> 2) & 1 else x
        py = 1 - y if (k >> 1) & 1 else y
        pc = 1 - c if k & 1 else c
        peer = 4 * px + 2 * py + pc
        for a, (src, dst) in enumerate(zip(src_refs, dst_refs)):
            copies.append(pltpu.make_async_remote_copy(
                src_ref=src if broadcast else src.at[peer], dst_ref=dst.at[me],
                send_sem=send_sems.at[k - 1, a], recv_sem=recv_sems.at[k - 1, a],
                device_id=(px, py, pc), device_id_type=MESH))
    for a, (src, dst) in enumerate(zip(src_refs, dst_refs)):
        copies.append(pltpu.make_async_copy(src if broadcast else src.at[me], dst.at[me], local_sems.at[a]))
    return copies


def _call_with_exchange(body, *, name, grid, in_specs, out_specs, out_shape, scratch_shapes, args, exchange):
    if exchange is None:
        outs = pl.pallas_call(body, name=name, grid=grid, in_specs=in_specs, out_specs=out_specs, out_shape=out_shape,
                              scratch_shapes=scratch_shapes,
                              compiler_params=_cparams(*["arbitrary"] * len(grid)))(*args)
        return outs, None
    srcs, broadcast = exchange
    n, n_in, n_out, n_scr = len(srcs), len(args), len(out_shape), len(scratch_shapes)
    steps = 1
    for g in grid:
        steps *= g

    def hosted(*refs):
        ins, src_refs = refs[:n_in], refs[n_in:n_in + n]
        outs, dst_refs = refs[n_in + n:n_in + n + n_out], refs[n_in + n + n_out:n_in + 2 * n + n_out]
        scratch = refs[n_in + 2 * n + n_out:]
        step = pl.program_id(0)
        for axis in range(1, len(grid)):
            step = step * grid[axis] + pl.program_id(axis)

        @pl.when(step == 0)
        def _():
            for cp in _exchange_copies(src_refs, dst_refs, *scratch[n_scr:], broadcast):
                cp.start()

        body(*ins, *outs, *scratch[:n_scr])

        @pl.when(step == steps - 1)
        def _():
            for cp in _exchange_copies(src_refs, dst_refs, *scratch[n_scr:], broadcast):
                cp.wait()

    outs = pl.pallas_call(
        hosted, name=name, grid=grid, in_specs=list(in_specs) + [HBM_SPEC] * n, out_specs=list(out_specs) + [HBM_SPEC] * n,
        out_shape=list(out_shape) + _exchange_shapes(srcs, broadcast),
        scratch_shapes=list(scratch_shapes) + _exchange_semaphores(n),
        compiler_params=_cparams(*["arbitrary"] * len(grid)))(*args, *srcs)
    return outs[:n_out], outs[n_out:]


def _regroup_w_in(w):
    wa = w[:, OFF_A:OFF_A + NA]
    seg = lambda off, h, n=HD: w[:, off + h * n: off + (h + 1) * n]
    wb = jnp.concatenate([seg(OFF_B + s * 512, h) for h in range(HGRN_HEADS) for s in range(4)], axis=1)
    parts = []
    for h in range(GDN_QK_HEADS):
        small = jnp.concatenate(
            [w[:, OFF_BETA + 2 * h: OFF_BETA + 2 * h + 2], w[:, OFF_CA + 2 * h: OFF_CA + 2 * h + 2],
             jnp.zeros((w.shape[0], HD - 4), w.dtype)], axis=1)
        parts += [seg(OFF_CQ, h), seg(OFF_CK, h), seg(OFF_CV, h, 2 * HD), seg(OFF_CZ, h, 2 * HD), small]
    wc = jnp.concatenate(parts, axis=1)
    wg = w[:, OFF_G:OFF_G + NG]
    return wa, wb, wc, wg


def _ungroup_dw_in(da, db, dc, dg):
    bq = [jnp.concatenate([db[:, h * 512 + s * HD: h * 512 + (s + 1) * HD] for h in range(HGRN_HEADS)], axis=1)
          for s in range(4)]
    ch = lambda h, lo, hi: dc[:, h * C_HEAD + lo: h * C_HEAD + hi]
    heads = range(GDN_QK_HEADS)
    cq = jnp.concatenate([ch(h, 0, HD) for h in heads], axis=1)
    ck = jnp.concatenate([ch(h, HD, 2 * HD) for h in heads], axis=1)
    cv = jnp.concatenate([ch(h, 2 * HD, 4 * HD) for h in heads], axis=1)
    cz = jnp.concatenate([ch(h, 4 * HD, 6 * HD) for h in heads], axis=1)
    cbeta = jnp.concatenate([ch(h, 6 * HD, 6 * HD + 2) for h in heads], axis=1)
    ca = jnp.concatenate([ch(h, 6 * HD + 2, 6 * HD + 4) for h in heads], axis=1)
    return jnp.concatenate([da] + bq + [cq, ck, cv, cbeta, ca, cz, dg], axis=1)


def _regroup_conv_c(cw):
    parts = []
    for h in range(GDN_QK_HEADS):
        parts += [cw[:, h * HD:(h + 1) * HD], cw[:, 512 + h * HD: 512 + (h + 1) * HD],
                  cw[:, 1024 + 2 * h * HD: 1024 + (2 * h + 2) * HD]]
    return jnp.concatenate(parts, axis=1)


def _ungroup_conv_c(d):
    heads = range(GDN_QK_HEADS)
    q = jnp.concatenate([d[:, h * 512: h * 512 + HD] for h in heads], axis=1)
    k = jnp.concatenate([d[:, h * 512 + HD: h * 512 + 2 * HD] for h in heads], axis=1)
    v = jnp.concatenate([d[:, h * 512 + 2 * HD: h * 512 + 4 * HD] for h in heads], axis=1)
    return jnp.concatenate([q, k, v], axis=1)


def _numel(shape):
    n = 1
    for d in shape:
        n *= d
    return n


def _pack(arrays, rows):
    flat = jnp.concatenate([a.reshape(-1) for a in arrays])
    return jnp.pad(flat, (0, rows * 128 - flat.shape[0])).reshape(rows, 128)


def _unpack(packed, shapes):
    flat = packed.reshape(-1)
    out, off = [], 0
    for s in shapes:
        out.append(flat[off:off + _numel(s)].reshape(s))
        off += _numel(s)
    return out


def _rows_for(shapes):
    return -(-sum(_numel(s) for s in shapes) // 1024) * 8


def kernel(x, norm_w, w_in, b_gate, conv_a, conv_c, a_log, dt_bias, lower_bounds, hgrn_norm_w, gdn_norm_w, w_out_a, w_out_b, w_out_c, w_o, final_norm_w, loss_target, m_norm_w, m_w_in, m_b_gate, m_conv_a, m_conv_c, m_a_log, m_dt_bias, m_lower_bounds, m_hgrn_norm_w, m_gdn_norm_w, m_w_out_a, m_w_out_b, m_w_out_c, m_w_o, m_final_norm_w, v_norm_w, v_w_in, v_b_gate, v_conv_a, v_conv_c, v_a_log, v_dt_bias, v_lower_bounds, v_hgrn_norm_w, v_gdn_norm_w, v_w_out_a, v_w_out_b, v_w_out_c, v_w_o, v_final_norm_w):
    me = 4 * lax.axis_index("x") + 2 * lax.axis_index("y") + lax.axis_index("c")
    xs = x[0]
    target = loss_target[0]
    in_shard = w_in.shape[2]

    big = [w_in, w_out_a, w_out_b, w_out_c, w_o]
    shards_of = lambda l: [w[l].astype(BF16) for w in big]
    conv_shapes = [(DEPTH, 3, CONV_W), (DEPTH, 4, 2048)]
    conv_rows = _rows_for(conv_shapes)
    ca_full = lax.dynamic_update_slice(jnp.zeros(conv_shapes[0], F32), conv_a, (0, 0, me * conv_a.shape[2]))
    cc_full = lax.dynamic_update_slice(jnp.zeros(conv_shapes[1], F32), conv_c, (0, 0, me * conv_c.shape[2]))
    conv_parts, = _exchange([_pack([ca_full, cc_full], conv_rows)], "gather_conv", broadcast=True)
    conv_a_full, conv_c_full = _unpack(_sum_slots(conv_parts, "sum_conv"), conv_shapes)

    lb_pad = jnp.pad(lower_bounds, ((0, 8 - DEPTH), (0, 0)))
    lbs = _lower_bounds_fwd(lb_pad, "lower_bounds_fwd")

    def layer_weights(l, gathered):
        g_in, g_oa, g_ob, g_oc, g_o = gathered
        wa, wb, wc, wg = _regroup_w_in(jnp.concatenate([g_in[q] for q in range(N_DEV)], axis=1))
        lanes = lambda vec: jnp.pad(vec.reshape(GDN_QK_HEADS, 1, 2), ((0, 0), (0, 0), (0, HD - 2)))
        cpar = jnp.concatenate([lanes(a_log[l]), lanes(dt_bias[l]), jnp.zeros((GDN_QK_HEADS, 6, HD), F32)], axis=1)
        return dict(
            wa=wa, wb=wb, wc=wc, wg=wg, cpar=cpar,
            woa=jnp.concatenate([g_oa[q] for q in range(N_DEV)], axis=1),
            wob=jnp.concatenate([g_ob[q] for q in range(N_DEV)], axis=1), woc=g_oc.reshape(D, D), wo=g_o.reshape(D, D),
            nw=norm_w[l:l + 1], bg=b_gate[l:l + 1], cwa=conv_a_full[l], cwc=_regroup_conv_c(conv_c_full[l]),
            lb=lbs[l:l + 1], hnw=hgrn_norm_w[l:l + 1], gnw=gdn_norm_w[l:l + 1])

    layers = [layer_weights(0, _exchange(shards_of(0), "gather_l0", broadcast=True))]

    saved = []
    cur = xs
    for l in range(DEPTH):
        L = layers[l]
        n = f"l{l}_"
        h = _rmsnorm_fwd(cur, L["nw"], n + "rms")
        pa = _matmul(h, L["wa"], "nn", n + "proj_a")
        pb = _matmul(h, L["wb"], "nn", n + "proj_b")
        pc = _matmul(h, L["wc"], "nn", n + "proj_c")
        pg = _matmul(h, L["wg"], "nn", n + "proj_g")
        ua = _branch_a_fwd(pa, L["cwa"], n + "conv_fwd")
        ub, sb = _branch_b_fwd(pb, L["lb"], L["hnw"], n + "hgrn_fwd")
        carry = (shards_of(l + 1), True) if l + 1 < DEPTH else None
        (uc, sc), gathered = _branch_c_fwd(pc, L["cwc"], L["cpar"], L["gnw"], n + "gdn_fwd", exchange=carry)
        if carry is not None:
            layers.append(layer_weights(l + 1, gathered))
        ya =_matmul(ua, L["woa"], "nn", n + "out_a")
        yb = _matmul(ub, L["wob"], "nn", n + "out_b")
        yc = _matmul(uc, L["woc"], "nn", n + "out_c")
        merged = _merge_fwd(pg, L["bg"], ya, yb, yc, n + "merge")
        nxt = _matmul(merged, L["wo"], "nn", n + "out_o", residual=cur)
        saved.append(dict(x=cur, h=h, pa=pa, pb=pb, pc=pc, pg=pg, ua=ua, ub=ub, uc=uc, sb=sb, sc=sc,
                          ya=ya, yb=yb, yc=yc, merged=merged))
        cur = nxt

    loss_part, dx, d_final = _loss_head(cur, final_norm_w.reshape(1, D), target, "loss_head")

    def outgoing(g):
        cols = lambda a, n: jnp.stack([a[:, p * n:(p + 1) * n] for p in range(N_DEV)]).astype(BF16)
        rows = lambda a: a.reshape(N_DEV, a.shape[0] // N_DEV, a.shape[1]).astype(BF16)
        return [cols(g["w_in"], in_shard), cols(g["w_out_a"], 128), cols(g["w_out_b"], 128), rows(g["w_out_c"]), rows(g["w_o"])]

    grads = [None] * DEPTH
    dlbs_rows = [None] * DEPTH
    incoming = [None] * DEPTH
    for l in reversed(range(DEPTH)):
        L, S = layers[l], saved[l]
        n = f"l{l}_"
        dmerged = _matmul(dx, L["wo"], "nt", n + "d_merged")
        d_wo = _matmul(S["merged"], dx, "tn", n + "dw_o")
        dpg, dya, dyb, dyc, d_bg = _merge_bwd(dmerged, S["pg"], L["bg"], S["ya"], S["yb"], S["yc"], n + "merge_bwd")
        dua = _matmul(dya, L["woa"], "nt", n + "d_ua")
        dub = _matmul(dyb, L["wob"], "nt", n + "d_ub")
        duc = _matmul(dyc, L["woc"], "nt", n + "d_uc")
        d_woa = _matmul(S["ua"], dya, "tn", n + "dw_out_a")
        d_wob = _matmul(S["ub"], dyb, "tn", n + "dw_out_b")
        d_woc = _matmul(S["uc"], dyc, "tn", n + "dw_out_c")
        dpa, d_cwa = _branch_a_bwd(S["pa"], L["cwa"], dua, n + "conv_bwd")
        dpb, d_lb, d_hnw = _branch_b_bwd(S["pb"], S["sb"], L["lb"], L["hnw"], dub, n + "hgrn_bwd")
        carry = (outgoing(grads[l + 1]), False) if l + 1 < DEPTH else None
        (dpc, d_cwc, d_cpar, d_gnw), arrived = _branch_c_bwd(S["pc"], S["sc"], L["cwc"], L["cpar"], L["gnw"], duc,
                                                             n + "gdn_bwd", exchange=carry)
        if carry is not None:
            incoming[l + 1] = arrived
        dh = _matmul(dpa, L["wa"], "nt", n + "dh_a")
        dh = _matmul(dpb, L["wb"], "nt", n + "dh_b", residual=dh)
        dh = _matmul(dpc, L["wc"], "nt", n + "dh_c", residual=dh)
        dh = _matmul(dpg, L["wg"], "nt", n + "dh_g", residual=dh)
        d_win = _ungroup_dw_in(_matmul(S["h"], dpa, "tn", n + "dw_a"), _matmul(S["h"], dpb, "tn", n + "dw_b"),
                               _matmul(S["h"], dpc, "tn", n + "dw_c"), _matmul(S["h"], dpg, "tn", n + "dw_g"))
        dx, d_nw = _rmsnorm_bwd(dh, S["x"], L["nw"], dx, n + "rms_bwd")
        dlbs_rows[l] = d_lb
        grads[l] = dict(w_in=d_win, w_out_a=d_woa, w_out_b=d_wob, w_out_c=d_woc, w_o=d_wo, norm_w=d_nw[0],
                        b_gate=d_bg[0], conv_a=d_cwa, conv_c=_ungroup_conv_c(d_cwc),
                        a_log=d_cpar[:, 0, 0:2].reshape(-1), dt_bias=d_cpar[:, 1, 0:2].reshape(-1),
                        hgrn_norm_w=d_hnw[0], gdn_norm_w=d_gnw[0])
    grad_x = dx[None]
    d_lower = _lower_bounds_bwd(lb_pad, jnp.pad(jnp.concatenate(dlbs_rows, axis=0), ((0, 8 - DEPTH), (0, 0))),
                                "lower_bounds_bwd")[:DEPTH]

    incoming[0] = _exchange(outgoing(grads[0]), "exchange_grads_l0", broadcast=False)
    stack = lambda name: jnp.stack([grads[l][name] for l in range(DEPTH)])
    big_out = {}
    for j, (name, w, m, v) in enumerate((("w_in", w_in, m_w_in, v_w_in), ("w_out_a", w_out_a, m_w_out_a, v_w_out_a),
                                         ("w_out_b", w_out_b, m_w_out_b, v_w_out_b), ("w_out_c", w_out_c, m_w_out_c, v_w_out_c),
                                         ("w_o", w_o, m_w_o, v_w_o))):
        parts = jnp.concatenate([incoming[l][j] for l in range(DEPTH)], axis=1)
        r2 = lambda a: a.reshape(parts.shape[1], parts.shape[2])
        outs = _sum_adamw(parts, r2(w), r2(m), r2(v), "adamw_" + name)
        big_out[name] = [o.reshape(w.shape) for o in outs]

    small_names = ["norm_w", "b_gate", "conv_a", "conv_c", "a_log", "dt_bias", "lower_bounds", "hgrn_norm_w",
                   "gdn_norm_w", "final_norm_w", "loss"]
    small_vals = {k: stack(k) for k in ("norm_w", "b_gate", "conv_a", "conv_c", "a_log", "dt_bias", "hgrn_norm_w", "gdn_norm_w")}
    small_vals.update(lower_bounds=d_lower, final_norm_w=d_final[0], loss=loss_part.reshape(1))
    small_shapes = [small_vals[k].shape for k in small_names]
    small_rows = _rows_for(small_shapes)
    small_parts, = _exchange([_pack([small_vals[k] for k in small_names], small_rows)], "exchange_small", broadcast=True)
    total = dict(zip(small_names, _unpack(_sum_slots(small_parts, "sum_small"), small_shapes)))
    loss = total["loss"][0]
    g_conv_a = lax.dynamic_slice(total["conv_a"], (0, 0, me * conv_a.shape[2]), conv_a.shape)
    g_conv_c = lax.dynamic_slice(total["conv_c"], (0, 0, me * conv_c.shape[2]), conv_c.shape)

    small_w = dict(norm_w=(norm_w, m_norm_w, v_norm_w), b_gate=(b_gate, m_b_gate, v_b_gate),
                   conv_a=(conv_a, m_conv_a, v_conv_a), conv_c=(conv_c, m_conv_c, v_conv_c),
                   a_log=(a_log, m_a_log, v_a_log), dt_bias=(dt_bias, m_dt_bias, v_dt_bias),
                   lower_bounds=(lower_bounds, m_lower_bounds, v_lower_bounds),
                   hgrn_norm_w=(hgrn_norm_w, m_hgrn_norm_w, v_hgrn_norm_w), gdn_norm_w=(gdn_norm_w, m_gdn_norm_w, v_gdn_norm_w),
                   final_norm_w=(final_norm_w, m_final_norm_w, v_final_norm_w))
    small_g = dict(total, conv_a=g_conv_a, conv_c=g_conv_c)
    upd_names = small_names[:-1]
    upd_shapes = [small_w[k][0].shape for k in upd_names]
    upd_rows = _rows_for(upd_shapes)
    pk = lambda j: _pack([small_w[k][j] for k in upd_names], upd_rows)
    s_delta, s_m, s_v = _adamw(_pack([small_g[k] for k in upd_names], upd_rows), pk(0), pk(1), pk(2), "adamw_small")
    small_out = {k: [small_g[k], d, mm, vv] for k, d, mm, vv in
                 zip(upd_names, _unpack(s_delta, upd_shapes), _unpack(s_m, upd_shapes), _unpack(s_v, upd_shapes))}

    order = ["norm_w", "w_in", "b_gate", "conv_a", "conv_c", "a_log", "dt_bias", "lower_bounds", "hgrn_norm_w",
             "gdn_norm_w", "w_out_a", "w_out_b", "w_out_c", "w_o", "final_norm_w"]
    res = {**small_out, **big_out}
    outs = [loss, grad_x]
    for j in range(4):
        outs += [res[k][j] for k in order]
    return tuple(outs)
```

```python
import functools

import jax
import jax.numpy as jnp
from jax import lax
from jax.experimental import pallas as pl
from jax.experimental.pallas import tpu as pltpu

F32 = jnp.float32
BF16 = jnp.bfloat16
H3 = lax.Precision.HIGH
MESH = pl.DeviceIdType.MESH

N_DEV = 8
D = 1024
DEPTH = 2
CHUNK = 64
SUB = 16
HGRN_BLOCK_CHUNKS = 8
GDN_BLOCK_CHUNKS = 8
GROUP = 128
NORM_EPS = 1e-6
L2_EPS = 1e-6
MIN_F = 1e-30
HD = 128
HGRN_HEADS = 4
GDN_QK_HEADS = 4
CONV_W = 512
IN_COLS = 10256
OFF_A, OFF_B, OFF_CQ, OFF_CK, OFF_CV, OFF_BETA, OFF_CA, OFF_CZ, OFF_G = (
    0, 2048, 4096, 4608, 5120, 6144, 6152, 6160, 7184)
NA, NB, NC_COLS, NG = 2048, 2048, 3584, 3072
C_HEAD = 896

ADAM_LR, ADAM_B1, ADAM_B2, ADAM_EPS, ADAM_WD, ADAM_STEP = 0.001, 0.9, 0.999, 1e-08, 0.01, 10

VMEM_LIMIT = 56 * 1024 * 1024
MM_TILE = 1024


def _cparams(*sem):
    return pltpu.CompilerParams(dimension_semantics=sem, vmem_limit_bytes=VMEM_LIMIT)


def _tile(dim, cap):
    if dim <= cap:
        return dim
    t = (cap // 128) * 128
    while dim % t:
        t -= 128
    return t


def _sigmoid(x):
    return 1.0 / (1.0 + jnp.exp(-x))


def _silu(x):
    return x * _sigmoid(x)


def _softplus(x):
    return jnp.maximum(x, 0.0) + jnp.log(1.0 + jnp.exp(-jnp.abs(x)))


def _dot(a, b, dims, precision=None):
    if precision is None:
        a, b = a.astype(BF16), b.astype(BF16)
    return lax.dot_general(a, b, (dims, ((), ())), precision=precision, preferred_element_type=F32)


def _nn(a, b, precision=None):
    return _dot(a, b, ((1,), (0,)), precision)


def _nt(a, b, precision=None):
    return _dot(a, b, ((1,), (1,)), precision)


def _tn(a, b, precision=None):
    return _dot(a, b, ((0,), (0,)), precision)


def _sum_rows_exact(mat01, x):
    m = mat01.astype(BF16)
    hi = x.astype(BF16)
    rest = x - hi.astype(F32)
    mid = rest.astype(BF16)
    low = (rest - mid.astype(F32)).astype(BF16)
    return _nn(m, hi) + _nn(m, mid) + _nn(m, low)


@functools.partial(jax.custom_vjp, nondiff_argnums=(1,))
def _shift_rows(x, d):
    return x if d == 0 else pltpu.roll(x, d, 0)


def _shift_rows_fwd(x, d):
    return _shift_rows(x, d), None


def _shift_rows_bwd(d, _, ct):
    return ((ct if d == 0 else pltpu.roll(ct, ct.shape[0] - d, 0)),)


_shift_rows.defvjp(_shift_rows_fwd, _shift_rows_bwd)


def _iota2(shape):
    return lax.broadcasted_iota(jnp.int32, shape, 0), lax.broadcasted_iota(jnp.int32, shape, 1)


def _lane_pick(x, i):
    lane = lax.broadcasted_iota(jnp.int32, x.shape, 1)
    return jnp.sum(jnp.where(lane == i, x, 0.0), axis=1, keepdims=True)


def _hgrn_block(qr, fr, ir, zr, st0, lb, nw):
    rows = qr.shape[0]
    r, c = _iota2((CHUNK, CHUNK))
    sr, sc = r // SUB, c // SUB
    mats = [c <= r,
            (c <= r) & (sr == sc),
            (c > r) & (sr == sc),
            c > r,
            sc == sr - 1,
            (sc == sr - 1) | (sc == sr - 2)]
    stack = jnp.concatenate([m.astype(F32) for m in mats], axis=0)
    dist = sr - sc

    q = _silu(qr) * (HD ** -0.5)
    fg = lb + (1.0 - lb) * _sigmoid(fr)
    logf = jnp.log(jnp.maximum(fg, MIN_F))
    kk = 1.0 - fg
    v = ir

    chunks = [slice(s, s + CHUNK) for s in range(0, rows, CHUNK)]
    cums = [_sum_rows_exact(stack, logf[sl]) for sl in chunks]
    part = lambda i: jnp.concatenate([cs[i * CHUNK:(i + 1) * CHUNK] for cs in cums], axis=0)
    g, gl, gr, grc, c2, c3 = [part(i) for i in range(6)]
    o = jnp.zeros_like(v)
    row = lax.broadcasted_iota(jnp.int32, (rows, HD), 0) % SUB
    for d in range(SUB):
        ok = row >= d
        diff = gl - _shift_rows(gl, d)
        dec = jnp.where(ok, jnp.exp(jnp.where(ok, diff, 0.0)), 0.0)
        term = jnp.sum(q * dec * _shift_rows(kk, d), axis=1, keepdims=True)
        o += term * _shift_rows(v, d)

    qg = q * jnp.exp(g)
    qt = q * jnp.exp(gl)
    qt2 = qt * jnp.exp(c2)
    qt3 = qt * jnp.exp(c3)
    kh = kk * jnp.exp(gr)
    ks = kk * jnp.exp(grc)
    st = st0
    outs = []
    for sl in chunks:
        scores = jnp.where(dist == 1, _nt(qt[sl], kh[sl]), 0.0)
        scores += jnp.where(dist == 2, _nt(qt2[sl], kh[sl]), 0.0)
        scores += jnp.where(dist == 3, _nt(qt3[sl], kh[sl]), 0.0)
        outs.append(_nn(scores, v[sl]) + _nt(qg[sl], st))
        st = st * jnp.exp(jnp.sum(logf[sl], axis=0, keepdims=True)) + _tn(v[sl], ks[sl])
    o += jnp.concatenate(outs, axis=0)
    y = o * lax.rsqrt(jnp.mean(o * o, axis=1, keepdims=True) + NORM_EPS) * nw * _silu(zr)
    return y, st


@jax.custom_vjp
def _unit_lower_inverses(ms):
    r, c = _iota2(ms[0].shape)
    xs = [jnp.where(r == c, 1.0, 0.0) - jnp.where((r // 2) == (c // 2), m, 0.0) for m in ms]
    b = 2
    while b < CHUNK:
        pick = ((r // (2 * b)) == (c // (2 * b))) & ((r // b) != (c // b))
        ts = [_nn(x, jnp.where(pick, m, 0.0), H3) for x, m in zip(xs, ms)]
        xs = [x - _nn(t, x, H3) for x, t in zip(xs, ts)]
        b *= 2
    return tuple(xs)


def _unit_lower_inverses_fwd(ms):
    xs = _unit_lower_inverses(ms)
    return xs, xs


def _unit_lower_inverses_bwd(xs, cts):
    r, c = _iota2(xs[0].shape)
    keep = (c < r) & ((r // CHUNK) == (c // CHUNK))
    ts = [_tn(x, ct, H3) for x, ct in zip(xs, cts)]
    return (tuple(jnp.where(keep, -_nt(t, x, H3), 0.0) for t, x in zip(ts, xs)),)


_unit_lower_inverses.defvjp(_unit_lower_inverses_fwd, _unit_lower_inverses_bwd)


def _chunk_cumsum(x):
    row = lax.broadcasted_iota(jnp.int32, x.shape, 0) % CHUNK
    d = 1
    while d < CHUNK:
        x = x + jnp.where(row >= d, _shift_rows(x, d), 0.0)
        d *= 2
    return x


def _gdn_block(x_ext, z, ba, s0a, s0b, w0, w1, w2, w3, alog, dtb, nw):
    rows = z.shape[0]
    conv = (w0 * _shift_rows(x_ext, 3) + w1 * _shift_rows(x_ext, 2) + w2 * _shift_rows(x_ext, 1) + w3 * x_ext)
    cc = _silu(conv[8:])
    qc, kc = cc[:, 0:HD], cc[:, HD:2 * HD]
    q = qc * lax.rsqrt(jnp.sum(qc * qc, axis=1, keepdims=True) + L2_EPS) * (HD ** -0.5)
    k = kc * lax.rsqrt(jnp.sum(kc * kc, axis=1, keepdims=True) + L2_EPS)

    r, c = _iota2((GROUP, GROUP))
    same = (r // CHUNK) == (c // CHUNK)
    causal, strict, eye = same & (c <= r), same & (c < r), r == c
    heads = (0, 1)
    groups = [slice(lo, lo + GROUP) for lo in range(0, rows, GROUP)]
    chunks = [slice(lo, lo + CHUNK) for lo in range(0, rows, CHUNK)]

    v, loga, g_w, kb, kg, qg = [], [], [], [], [], []
    for i in heads:
        v.append(cc[:, (2 + i) * HD:(3 + i) * HD])
        beta = _sigmoid(_lane_pick(ba, i))
        a_neg = -jnp.exp(_lane_pick(alog, i))
        loga.append(a_neg * _softplus(_lane_pick(ba, 2 + i) + _lane_pick(dtb, i)))
        g_w.append(_chunk_cumsum(jnp.broadcast_to(loga[i], (rows, HD))))
        kb.append(k * beta)
        kg.append(k * jnp.exp(g_w[i]))
        qg.append(q * jnp.exp(g_w[i]))

    systems = [(i, gs) for gs in groups for i in heads]
    dec_c, ms = [], []
    for i, gs in systems:
        g_sq = g_w[i][gs]
        g_row = jnp.sum(jnp.where(eye, g_sq, 0.0), axis=0, keepdims=True)
        diff = g_sq - g_row
        dec_c.append(jnp.where(causal, jnp.exp(jnp.where(causal, diff, 0.0)), 0.0))
        ms.append(jnp.where(strict, _nt(k[gs], kb[i][gs]) * dec_c[-1], 0.0))
    xs = _unit_lower_inverses(tuple(ms))
    u = [[None] * len(groups) for _ in heads]
    w = [[None] * len(groups) for _ in heads]
    qk = [[None] * len(groups) for _ in heads]
    for n, (i, gs) in enumerate(systems):
        j = n // len(heads)
        u[i][j] = _nn(xs[n], v[i][gs])
        w[i][j] = _nn(xs[n], kg[i][gs])
        qk[i][j] = _nt(q[gs], kb[i][gs]) * dec_c[n]
    u = [jnp.concatenate(p, axis=0) for p in u]
    w = [jnp.concatenate(p, axis=0) for p in w]

    decay, p_mat, q_mat = {}, {}, {}
    for n, sl in enumerate(chunks):
        for i in heads:
            g_last = jnp.sum(loga[i][sl], axis=0, keepdims=True)
            kd = kb[i][sl] * jnp.exp(g_last - g_w[i][sl])
            decay[n, i] = jnp.exp(g_last)
            p_mat[n, i] = -_tn(kd, w[i][sl])
            q_mat[n, i] = _tn(kd, u[i][sl])
    s = [s0a, s0b]
    s_at = {}
    for n in range(len(chunks)):
        for i in heads:
            s_at[n, i] = s[i]
            s[i] = s[i] * decay[n, i] + _nn(p_mat[n, i], s[i]) + q_mat[n, i]

    ys = []
    for i in heads:
        e = jnp.concatenate([u[i][sl] - _nn(w[i][sl], s_at[n, i]) for n, sl in enumerate(chunks)], axis=0)
        o_state = jnp.concatenate([_nn(qg[i][sl], s_at[n, i]) for n, sl in enumerate(chunks)], axis=0)
        o = o_state + jnp.concatenate([_nn(qk[i][j], e[gs]) for j, gs in enumerate(groups)], axis=0)
        zi = z[:, i * HD:(i + 1) * HD]
        ys.append(o * lax.rsqrt(jnp.mean(o * o, axis=1, keepdims=True) + NORM_EPS) * nw * _silu(zi))
    return jnp.concatenate(ys, axis=1), s[0], s[1]


def _add_to_tail(x, tail):
    return x + jnp.concatenate([jnp.zeros((x.shape[0] - 8, x.shape[1]), x.dtype), tail], axis=0)


def _conv_a_block(ab, ac_ext, ax_ext, az, w0, w1, w2):
    u = ac_ext * ax_ext
    conv = (w0 * _shift_rows(u, 2) + w1 * _shift_rows(u, 1) + w2 * u)[8:]
    return ab * conv * _silu(az)


def _matmul(a, b, mode, name, residual=None, out_dtype=F32):
    if mode == "nn":
        (m, k), n = a.shape, b.shape[1]
    elif mode == "nt":
        (m, k), n = a.shape, b.shape[0]
    else:
        (k, m), n = a.shape, b.shape[1]
    tm, tn, tk = _tile(m, MM_TILE), _tile(n, MM_TILE), _tile(k, MM_TILE)
    nk = k // tk
    dims = {"nn": ((1,), (0,)), "nt": ((1,), (1,)), "tn": ((0,), (0,))}[mode]
    a_spec = pl.BlockSpec((tk, tm), lambda i, j, s: (s, i)) if mode == "tn" else pl.BlockSpec((tm, tk), lambda i, j, s: (i, s))
    b_spec = pl.BlockSpec((tn, tk), lambda i, j, s: (j, s)) if mode == "nt" else pl.BlockSpec((tk, tn), lambda i, j, s: (s, j))
    o_spec = pl.BlockSpec((tm, tn), lambda i, j, s: (i, j))
    has_res = residual is not None

    def finish(out, r_ref, o_ref):
        if has_res:
            out = out + r_ref[...]
        o_ref[...] = out.astype(out_dtype)

    def body_one_pass(*refs):
        finish(_dot(refs[0][...], refs[1][...], dims), refs[2] if has_res else None, refs[-1])

    def body_reduce(*refs):
        a_ref, b_ref = refs[0], refs[1]
        r_ref = refs[2] if has_res else None
        o_ref, acc_ref = refs[-2], refs[-1]
        s = pl.program_id(2)

        @pl.when(s == 0)
        def _():
            acc_ref[...] = jnp.zeros_like(acc_ref)

        acc_ref[...] += _dot(a_ref[...], b_ref[...], dims)

        @pl.when(s == nk - 1)
        def _():
            finish(acc_ref[...], r_ref, o_ref)

    args, specs = [a, b], [a_spec, b_spec]
    if has_res:
        args.append(residual)
        specs.append(o_spec)
    return pl.pallas_call(
        body_one_pass if nk == 1 else body_reduce, name=name, grid=(m // tm, n // tn, nk), in_specs=specs, out_specs=o_spec,
        out_shape=jax.ShapeDtypeStruct((m, n), out_dtype),
        scratch_shapes=[] if nk == 1 else [pltpu.VMEM((tm, tn), F32)],
        compiler_params=_cparams("parallel", "parallel", "arbitrary"))(*args)


def _rmsnorm_fwd(x, w, name):
    t = x.shape[0]
    blk = _tile(t, 512)

    def body(x_ref, w_ref, h_ref):
        xv = x_ref[...]
        h_ref[...] = (xv * lax.rsqrt(jnp.mean(xv * xv, axis=1, keepdims=True) + NORM_EPS) * w_ref[...]).astype(BF16)

    return pl.pallas_call(
        body, name=name, grid=(t // blk,),
        in_specs=[pl.BlockSpec((blk, D), lambda i: (i, 0)), pl.BlockSpec((1, D), lambda i: (0, 0))],
        out_specs=pl.BlockSpec((blk, D), lambda i: (i, 0)), out_shape=jax.ShapeDtypeStruct((t, D), BF16),
        compiler_params=_cparams("parallel"))(x, w)


def _rmsnorm_bwd(dh, x, w, dxo, name):
    t = x.shape[0]
    blk = _tile(t, 512)

    def body(dh_ref, x_ref, w_ref, dxo_ref, dx_ref, dw_ref):
        @pl.when(pl.program_id(0) == 0)
        def _():
            dw_ref[...] = jnp.zeros_like(dw_ref)

        xv, dhv = x_ref[...], dh_ref[...]
        rs = lax.rsqrt(jnp.mean(xv * xv, axis=1, keepdims=True) + NORM_EPS)
        xh = xv * rs
        dw_ref[...] += jnp.sum(dhv * xh, axis=0, keepdims=True)
        dxh = dhv * w_ref[...]
        dx_ref[...] = rs * (dxh - xh * jnp.mean(dxh * xh, axis=1, keepdims=True)) + dxo_ref[...]

    row = pl.BlockSpec((blk, D), lambda i: (i, 0))
    vec = pl.BlockSpec((1, D), lambda i: (0, 0))
    return pl.pallas_call(
        body, name=name, grid=(t // blk,), in_specs=[row, row, vec, row], out_specs=[row, vec],
        out_shape=[jax.ShapeDtypeStruct((t, D), F32), jax.ShapeDtypeStruct((1, D), F32)],
        compiler_params=_cparams("arbitrary"))(dh, x, w, dxo)


def _loss_head(x, w, target, name):
    t = x.shape[0]
    blk = _tile(t, 512)

    def body(x_ref, w_ref, t_ref, loss_ref, dx_ref, dw_ref):
        @pl.when(pl.program_id(0) == 0)
        def _():
            dw_ref[...] = jnp.zeros_like(dw_ref)
            loss_ref[...] = jnp.zeros_like(loss_ref)

        xv = x_ref[...]
        rs = lax.rsqrt(jnp.mean(xv * xv, axis=1, keepdims=True) + NORM_EPS)
        xh = xv * rs
        err = xh * w_ref[...] - t_ref[...]
        loss_ref[...] += 0.5 * jnp.sum(jnp.mean(err * err, axis=1, keepdims=True), axis=0, keepdims=True)
        dy = err * (1.0 / D)
        dw_ref[...] += jnp.sum(dy * xh, axis=0, keepdims=True)
        dxh = dy * w_ref[...]
        dx_ref[...] = rs * (dxh - xh * jnp.mean(dxh * xh, axis=1, keepdims=True))

    row = pl.BlockSpec((blk, D), lambda i: (i, 0))
    vec = pl.BlockSpec((1, D), lambda i: (0, 0))
    return pl.pallas_call(
        body, name=name, grid=(t // blk,), in_specs=[row, vec, row],
        out_specs=[pl.BlockSpec((1, 1), lambda i: (0, 0)), row, vec],
        out_shape=[jax.ShapeDtypeStruct((1, 1), F32), jax.ShapeDtypeStruct((t, D), F32), jax.ShapeDtypeStruct((1, D), F32)],
        compiler_params=_cparams("arbitrary"))(x, w, target)


def _lbs_of(lb):
    r = lax.broadcasted_iota(jnp.int32, lb.shape, 0)
    real = r < DEPTH
    mx = lax.stop_gradient(jnp.max(jnp.where(real, lb, -jnp.inf), axis=0, keepdims=True))
    e = jnp.where(real, jnp.exp(jnp.where(real, lb - mx, 0.0)), 0.0)
    p = e / jnp.sum(e, axis=0, keepdims=True)
    out = jnp.zeros_like(lb)
    run = jnp.zeros_like(mx)
    for l in range(1, DEPTH):
        run = run + jnp.sum(jnp.where(r == l, p, 0.0), axis=0, keepdims=True)
        out = out + jnp.where(r == l, run, 0.0)
    return out


def _lower_bounds_fwd(lbp, name):
    def body(lb_ref, o_ref):
        o_ref[...] = _lbs_of(lb_ref[...])

    return pl.pallas_call(body, name=name, out_shape=jax.ShapeDtypeStruct(lbp.shape, F32))(lbp)


def _lower_bounds_bwd(lbp, dlbs, name):
    def body(lb_ref, d_ref, o_ref):
        _, vjp = jax.vjp(_lbs_of, lb_ref[...])
        o_ref[...] = vjp(d_ref[...])[0]

    return pl.pallas_call(body, name=name, out_shape=jax.ShapeDtypeStruct(lbp.shape, F32))(lbp, dlbs)


def _branch_a_fwd(pa, cw, name):
    t = pa.shape[0]
    blk = _tile(t, 512)
    W = CONV_W

    def body(p_ref, w_ref, y_ref, hc_ref, hx_ref):
        @pl.when(pl.program_id(0) == 0)
        def _():
            hc_ref[...] = jnp.zeros_like(hc_ref)
            hx_ref[...] = jnp.zeros_like(hx_ref)

        ac, ax = p_ref[:, W:2 * W], p_ref[:, 2 * W:3 * W]
        y_ref[...] = _conv_a_block(
            p_ref[:, 0:W], jnp.concatenate([hc_ref[...], ac], axis=0), jnp.concatenate([hx_ref[...], ax], axis=0),
            p_ref[:, 3 * W:4 * W], w_ref[0:1, :], w_ref[1:2, :], w_ref[2:3, :]).astype(BF16)
        hc_ref[...] = p_ref[blk - 8:blk, W:2 * W]
        hx_ref[...] = p_ref[blk - 8:blk, 2 * W:3 * W]

    return pl.pallas_call(
        body, name=name, grid=(t // blk,),
        in_specs=[pl.BlockSpec((blk, NA), lambda i: (i, 0)), pl.BlockSpec((3, W), lambda i: (0, 0))],
        out_specs=pl.BlockSpec((blk, W), lambda i: (i, 0)), out_shape=jax.ShapeDtypeStruct((t, W), BF16),
        scratch_shapes=[pltpu.VMEM((8, W), F32), pltpu.VMEM((8, W), F32)],
        compiler_params=_cparams("arbitrary"))(pa, cw)


def _branch_a_bwd(pa, cw, dy, name):
    t = pa.shape[0]
    blk = _tile(t, 512)
    nt_ = t // blk
    W = CONV_W
    hb = blk // 8

    def body(p_ref, halo_ref, w_ref, dy_ref, dp_ref, dw_ref, chc_ref, chx_ref):
        i = pl.program_id(0)

        @pl.when(i == 0)
        def _():
            chc_ref[...] = jnp.zeros_like(chc_ref)
            chx_ref[...] = jnp.zeros_like(chx_ref)
            dw_ref[...] = jnp.zeros_like(dw_ref)

        keep = 1.0 - (i == nt_ - 1).astype(F32)
        hc = halo_ref[:, W:2 * W] * keep
        hx = halo_ref[:, 2 * W:3 * W] * keep
        ac_ext = jnp.concatenate([hc, p_ref[:, W:2 * W]], axis=0)
        ax_ext = jnp.concatenate([hx, p_ref[:, 2 * W:3 * W]], axis=0)
        _, vjp = jax.vjp(_conv_a_block, p_ref[:, 0:W], ac_ext, ax_ext, p_ref[:, 3 * W:4 * W],
                         w_ref[0:1, :], w_ref[1:2, :], w_ref[2:3, :])
        dab, dac, dax, daz, dw0, dw1, dw2 = vjp(dy_ref[...])
        dp_ref[:, 0:W] = dab.astype(BF16)
        dp_ref[:, W:2 * W] = _add_to_tail(dac[8:], chc_ref[...]).astype(BF16)
        dp_ref[:, 2 * W:3 * W] = _add_to_tail(dax[8:], chx_ref[...]).astype(BF16)
        dp_ref[:, 3 * W:4 * W] = daz.astype(BF16)
        chc_ref[...] = dac[:8] * keep
        chx_ref[...] = dax[:8] * keep
        dw_ref[0:1, :] += dw0
        dw_ref[1:2, :] += dw1
        dw_ref[2:3, :] += dw2

    rev = lambda i: (nt_ - 1 - i, 0)
    return pl.pallas_call(
        body, name=name, grid=(nt_,),
        in_specs=[pl.BlockSpec((blk, NA), rev),
                  pl.BlockSpec((8, NA), lambda i: (jnp.maximum((nt_ - 1 - i) * hb - 1, 0), 0)),
                  pl.BlockSpec((3, W), lambda i: (0, 0)),
                  pl.BlockSpec((blk, W), rev)],
        out_specs=[pl.BlockSpec((blk, NA), rev), pl.BlockSpec((3, W), lambda i: (0, 0))],
        out_shape=[jax.ShapeDtypeStruct((t, NA), BF16), jax.ShapeDtypeStruct((3, W), F32)],
        scratch_shapes=[pltpu.VMEM((8, W), F32), pltpu.VMEM((8, W), F32)],
        compiler_params=_cparams("arbitrary"))(pa, pa, cw, dy)


def _block_rows(t, chunks):
    return min(t, chunks * CHUNK)


def _branch_b_fwd(pb, lbs_row, nw, name):
    t = pb.shape[0]
    rows = _block_rows(t, HGRN_BLOCK_CHUNKS)
    nch = t // rows

    def body(p_ref, lb_ref, nw_ref, y_ref, s_ref, st_ref):
        @pl.when(pl.program_id(1) == 0)
        def _():
            st_ref[...] = jnp.zeros_like(st_ref)

        s_ref[0, 0] = st_ref[...]
        y, st1 = _hgrn_block(p_ref[:, 0:HD], p_ref[:, HD:2 * HD], p_ref[:, 2 * HD:3 * HD], p_ref[:, 3 * HD:4 * HD],
                             st_ref[...], lb_ref[...], nw_ref[...])
        y_ref[...] = y.astype(BF16)
        st_ref[...] = st1

    return pl.pallas_call(
        body, name=name, grid=(HGRN_HEADS, nch),
        in_specs=[pl.BlockSpec((rows, 4 * HD), lambda h, i: (i, h)),
                  pl.BlockSpec((1, HD), lambda h, i: (0, h)),
                  pl.BlockSpec((1, HD), lambda h, i: (0, 0))],
        out_specs=[pl.BlockSpec((rows, HD), lambda h, i: (i, h)),
                   pl.BlockSpec((1, 1, HD, HD), lambda h, i: (h, i, 0, 0))],
        out_shape=[jax.ShapeDtypeStruct((t, HGRN_HEADS * HD), BF16),
                   jax.ShapeDtypeStruct((HGRN_HEADS, nch, HD, HD), F32)],
        scratch_shapes=[pltpu.VMEM((HD, HD), F32)],
        compiler_params=_cparams("arbitrary", "arbitrary"))(pb, lbs_row, nw)


def _branch_b_bwd(pb, states, lbs_row, nw, dy, name):
    t = pb.shape[0]
    rows = _block_rows(t, HGRN_BLOCK_CHUNKS)
    nch = t // rows

    def body(p_ref, s_ref, lb_ref, nw_ref, dy_ref, dp_ref, dlb_ref, dnw_ref, ds_ref):
        h, i = pl.program_id(0), pl.program_id(1)

        @pl.when(i == 0)
        def _():
            ds_ref[...] = jnp.zeros_like(ds_ref)
            dlb_ref[...] = jnp.zeros_like(dlb_ref)

        @pl.when((i == 0) & (h == 0))
        def _():
            dnw_ref[...] = jnp.zeros_like(dnw_ref)

        _, vjp = jax.vjp(_hgrn_block, p_ref[:, 0:HD], p_ref[:, HD:2 * HD], p_ref[:, 2 * HD:3 * HD],
                         p_ref[:, 3 * HD:4 * HD], s_ref[0, 0], lb_ref[...], nw_ref[...])
        dq, df, di, dz, ds0, dlb, dnw = vjp((dy_ref[...], ds_ref[...]))
        dp_ref[:, 0:HD] = dq.astype(BF16)
        dp_ref[:, HD:2 * HD] = df.astype(BF16)
        dp_ref[:, 2 * HD:3 * HD] = di.astype(BF16)
        dp_ref[:, 3 * HD:4 * HD] = dz.astype(BF16)
        ds_ref[...] = ds0
        dlb_ref[...] += dlb
        dnw_ref[...] += dnw

    rev = lambda h, i: (nch - 1 - i, h)
    return pl.pallas_call(
        body, name=name, grid=(HGRN_HEADS, nch),
        in_specs=[pl.BlockSpec((rows, 4 * HD), rev),
                  pl.BlockSpec((1, 1, HD, HD), lambda h, i: (h, nch - 1 - i, 0, 0)),
                  pl.BlockSpec((1, HD), lambda h, i: (0, h)),
                  pl.BlockSpec((1, HD), lambda h, i: (0, 0)),
                  pl.BlockSpec((rows, HD), rev)],
        out_specs=[pl.BlockSpec((rows, 4 * HD), rev),
                   pl.BlockSpec((1, HD), lambda h, i: (0, h)),
                   pl.BlockSpec((1, HD), lambda h, i: (0, 0))],
        out_shape=[jax.ShapeDtypeStruct((t, NB), BF16), jax.ShapeDtypeStruct((1, HGRN_HEADS * HD), F32),
                   jax.ShapeDtypeStruct((1, HD), F32)],
        scratch_shapes=[pltpu.VMEM((HD, HD), F32)],
        compiler_params=_cparams("arbitrary", "arbitrary"))(pb, states, lbs_row, nw, dy)


def _branch_c_fwd(pc, cw, cpar, nw, name, exchange=None):
    t = pc.shape[0]
    rows = _block_rows(t, GDN_BLOCK_CHUNKS)
    nch = t // rows
    XW = 4 * HD

    def body(p_ref, w_ref, cp_ref, nw_ref, y_ref, s_ref, sa_ref, sb_ref, halo_ref):
        @pl.when(pl.program_id(1) == 0)
        def _():
            sa_ref[...] = jnp.zeros_like(sa_ref)
            sb_ref[...] = jnp.zeros_like(sb_ref)
            halo_ref[...] = jnp.zeros_like(halo_ref)

        s_ref[0, 0, 0] = sa_ref[...]
        s_ref[0, 0, 1] = sb_ref[...]
        x_ext = jnp.concatenate([halo_ref[...], p_ref[:, 0:XW]], axis=0)
        y, s1a, s1b = _gdn_block(x_ext, p_ref[:, XW:XW + 2 * HD], p_ref[:, XW + 2 * HD:XW + 3 * HD],
                                 sa_ref[...], sb_ref[...], w_ref[0:1, :], w_ref[1:2, :], w_ref[2:3, :], w_ref[3:4, :],
                                 cp_ref[0, 0:1, :], cp_ref[0, 1:2, :], nw_ref[...])
        y_ref[...] = y.astype(BF16)
        sa_ref[...] = s1a
        sb_ref[...] = s1b
        halo_ref[...] = p_ref[rows - 8:rows, 0:XW]

    return _call_with_exchange(
        body, name=name, grid=(GDN_QK_HEADS, nch),
        in_specs=[pl.BlockSpec((rows, C_HEAD), lambda h, i: (i, h)),
                  pl.BlockSpec((4, XW), lambda h, i: (0, h)),
                  pl.BlockSpec((1, 8, HD), lambda h, i: (h, 0, 0)),
                  pl.BlockSpec((1, HD), lambda h, i: (0, 0))],
        out_specs=[pl.BlockSpec((rows, 2 * HD), lambda h, i: (i, h)),
                   pl.BlockSpec((1, 1, 2, HD, HD), lambda h, i: (h, i, 0, 0, 0))],
        out_shape=[jax.ShapeDtypeStruct((t, 2 * GDN_QK_HEADS * HD), BF16),
                   jax.ShapeDtypeStruct((GDN_QK_HEADS, nch, 2, HD, HD), F32)],
        scratch_shapes=[pltpu.VMEM((HD, HD), F32), pltpu.VMEM((HD, HD), F32), pltpu.VMEM((8, XW), F32)],
        args=(pc, cw, cpar, nw), exchange=exchange)


def _branch_c_bwd(pc, states, cw, cpar, nw, dy, name, exchange=None):
    t = pc.shape[0]
    rows = _block_rows(t, GDN_BLOCK_CHUNKS)
    nch = t // rows
    XW = 4 * HD
    hb = rows // 8

    def body(p_ref, halo_ref, s_ref, w_ref, cp_ref, nw_ref, dy_ref, dp_ref, dw_ref, dcp_ref, dnw_ref,
             dsa_ref, dsb_ref, carry_ref):
        h, i = pl.program_id(0), pl.program_id(1)

        @pl.when(i == 0)
        def _():
            dsa_ref[...] = jnp.zeros_like(dsa_ref)
            dsb_ref[...] = jnp.zeros_like(dsb_ref)
            carry_ref[...] = jnp.zeros_like(carry_ref)
            dw_ref[...] = jnp.zeros_like(dw_ref)
            dcp_ref[...] = jnp.zeros_like(dcp_ref)

        @pl.when((i == 0) & (h == 0))
        def _():
            dnw_ref[...] = jnp.zeros_like(dnw_ref)

        keep = 1.0 - (i == nch - 1).astype(F32)
        x_ext = jnp.concatenate([halo_ref[:, 0:XW] * keep, p_ref[:, 0:XW]], axis=0)
        _, vjp = jax.vjp(_gdn_block, x_ext, p_ref[:, XW:XW + 2 * HD], p_ref[:, XW + 2 * HD:XW + 3 * HD],
                         s_ref[0, 0, 0], s_ref[0, 0, 1], w_ref[0:1, :], w_ref[1:2, :], w_ref[2:3, :], w_ref[3:4, :],
                         cp_ref[0, 0:1, :], cp_ref[0, 1:2, :], nw_ref[...])
        dx, dz, dba, dsa, dsb, dw0, dw1, dw2, dw3, dal, ddt, dnw = vjp((dy_ref[...], dsa_ref[...], dsb_ref[...]))
        dp_ref[:, 0:XW] = _add_to_tail(dx[8:], carry_ref[...]).astype(BF16)
        dp_ref[:, XW:XW + 2 * HD] = dz.astype(BF16)
        dp_ref[:, XW + 2 * HD:XW + 3 * HD] = dba.astype(BF16)
        carry_ref[...] = dx[:8] * keep
        dsa_ref[...] = dsa
        dsb_ref[...] = dsb
        dw_ref[0:1, :] += dw0
        dw_ref[1:2, :] += dw1
        dw_ref[2:3, :] += dw2
        dw_ref[3:4, :] += dw3
        dcp_ref[0, 0:1, :] += dal
        dcp_ref[0, 1:2, :] += ddt
        dnw_ref[...] += dnw

    rev = lambda h, i: (nch - 1 - i, h)
    return _call_with_exchange(
        body, name=name, grid=(GDN_QK_HEADS, nch),
        in_specs=[pl.BlockSpec((rows, C_HEAD), rev),
                  pl.BlockSpec((8, C_HEAD), lambda h, i: (jnp.maximum((nch - 1 - i) * hb - 1, 0), h)),
                  pl.BlockSpec((1, 1, 2, HD, HD), lambda h, i: (h, nch - 1 - i, 0, 0, 0)),
                  pl.BlockSpec((4, XW), lambda h, i: (0, h)),
                  pl.BlockSpec((1, 8, HD), lambda h, i: (h, 0, 0)),
                  pl.BlockSpec((1, HD), lambda h, i: (0, 0)),
                  pl.BlockSpec((rows, 2 * HD), rev)],
        out_specs=[pl.BlockSpec((rows, C_HEAD), rev),
                   pl.BlockSpec((4, XW), lambda h, i: (0, h)),
                   pl.BlockSpec((1, 8, HD), lambda h, i: (h, 0, 0)),
                   pl.BlockSpec((1, HD), lambda h, i: (0, 0))],
        out_shape=[jax.ShapeDtypeStruct((t, NC_COLS), BF16), jax.ShapeDtypeStruct((4, GDN_QK_HEADS * XW), F32),
                   jax.ShapeDtypeStruct((GDN_QK_HEADS, 8, HD), F32), jax.ShapeDtypeStruct((1, HD), F32)],
        scratch_shapes=[pltpu.VMEM((HD, HD), F32), pltpu.VMEM((HD, HD), F32), pltpu.VMEM((8, XW), F32)],
        args=(pc, pc, states, cw, cpar, nw, dy), exchange=exchange)


def _merge_fwd(pg, bg, ya, yb, yc, name):
    t = pg.shape[0]
    blk = _tile(t, 256)

    def body(g_ref, b_ref, a_ref, b2_ref, c_ref, o_ref):
        gate = _sigmoid(g_ref[...] + b_ref[...])
        o_ref[...] = (gate[:, 0:D] * a_ref[...] + gate[:, D:2 * D] * b2_ref[...] + gate[:, 2 * D:3 * D] * c_ref[...]).astype(BF16)

    row = pl.BlockSpec((blk, D), lambda i: (i, 0))
    return pl.pallas_call(
        body, name=name, grid=(t // blk,),
        in_specs=[pl.BlockSpec((blk, NG), lambda i: (i, 0)), pl.BlockSpec((1, NG), lambda i: (0, 0)), row, row, row],
        out_specs=row, out_shape=jax.ShapeDtypeStruct((t, D), BF16),
        compiler_params=_cparams("parallel"))(pg, bg, ya, yb, yc)


def _merge_bwd(dm, pg, bg, ya, yb, yc, name):
    t = pg.shape[0]
    blk = _tile(t, 256)

    def body(dm_ref, g_ref, b_ref, a_ref, b2_ref, c_ref, dg_ref, da_ref, db_ref, dc_ref, dbg_ref):
        @pl.when(pl.program_id(0) == 0)
        def _():
            dbg_ref[...] = jnp.zeros_like(dbg_ref)

        gate = _sigmoid(g_ref[...] + b_ref[...])
        dmv = dm_ref[...]
        for j, (y_ref, dy_ref) in enumerate(((a_ref, da_ref), (b2_ref, db_ref), (c_ref, dc_ref))):
            gj = gate[:, j * D:(j + 1) * D]
            dy_ref[...] = (dmv * gj).astype(BF16)
            dgj = dmv * y_ref[...] * gj * (1.0 - gj)
            dg_ref[:, j * D:(j + 1) * D] = dgj.astype(BF16)
            dbg_ref[:, j * D:(j + 1) * D] += jnp.sum(dgj, axis=0, keepdims=True)

    row = pl.BlockSpec((blk, D), lambda i: (i, 0))
    wide = pl.BlockSpec((blk, NG), lambda i: (i, 0))
    vec = pl.BlockSpec((1, NG), lambda i: (0, 0))
    return pl.pallas_call(
        body, name=name, grid=(t // blk,), in_specs=[row, wide, vec, row, row, row],
        out_specs=[wide, row, row, row, vec],
        out_shape=[jax.ShapeDtypeStruct((t, NG), BF16)] + [jax.ShapeDtypeStruct((t, D), BF16)] * 3
                  + [jax.ShapeDtypeStruct((1, NG), F32)],
        compiler_params=_cparams("arbitrary"))(dm, pg, bg, ya, yb, yc)


def _adamw_math(w, g, m, v):
    m = ADAM_B1 * m + (1.0 - ADAM_B1) * g
    v = ADAM_B2 * v + (1.0 - ADAM_B2) * (g * g)
    m_hat = m / (1.0 - ADAM_B1 ** ADAM_STEP)
    v_hat = v / (1.0 - ADAM_B2 ** ADAM_STEP)
    delta = -ADAM_LR * (m_hat / (jnp.sqrt(v_hat) + ADAM_EPS) + ADAM_WD * w)
    return delta, m, v


def _sum_adamw(parts, w, m, v, name):
    r, c = w.shape
    br = r if r <= 256 else 256
    assert r % br == 0

    def body(p_ref, w_ref, m_ref, v_ref, g_ref, d_ref, nm_ref, nv_ref):
        g = p_ref[0].astype(F32)
        for k in range(1, N_DEV):
            g = g + p_ref[k].astype(F32)
        g_ref[...] = g
        d_ref[...], nm_ref[...], nv_ref[...] = _adamw_math(w_ref[...], g, m_ref[...], v_ref[...])

    blk = pl.BlockSpec((br, c), lambda i: (i, 0))
    return pl.pallas_call(
        body, name=name, grid=(r // br,),
        in_specs=[pl.BlockSpec((N_DEV, br, c), lambda i: (0, i, 0)), blk, blk, blk], out_specs=[blk] * 4,
        out_shape=[jax.ShapeDtypeStruct((r, c), F32)] * 4, compiler_params=_cparams("parallel"))(parts, w, m, v)


def _adamw(g, w, m, v, name):
    def body(g_ref, w_ref, m_ref, v_ref, d_ref, nm_ref, nv_ref):
        d_ref[...], nm_ref[...], nv_ref[...] = _adamw_math(w_ref[...], g_ref[...], m_ref[...], v_ref[...])

    return pl.pallas_call(body, name=name, out_shape=[jax.ShapeDtypeStruct(w.shape, F32)] * 3)(g, w, m, v)


def _sum_slots(parts, name):
    def body(p_ref, o_ref):
        g = p_ref[0]
        for k in range(1, N_DEV):
            g = g + p_ref[k]
        o_ref[...] = g

    return pl.pallas_call(body, name=name, out_shape=jax.ShapeDtypeStruct(parts.shape[1:], F32))(parts)


def _exchange(srcs, name, broadcast):
    n = len(srcs)

    def body(*refs):
        copies = _exchange_copies(refs[:n], refs[n:2 * n], *refs[2 * n:], broadcast)
        for cp in copies:
            cp.start()
        for cp in copies:
            cp.wait()

    return pl.pallas_call(
        body, name=name, in_specs=[HBM_SPEC] * n, out_specs=[HBM_SPEC] * n, out_shape=_exchange_shapes(srcs, broadcast),
        scratch_shapes=_exchange_semaphores(n))(*srcs)


HBM_SPEC = pl.BlockSpec(memory_space=pltpu.HBM)


def _exchange_shapes(srcs, broadcast):
    return [jax.ShapeDtypeStruct((N_DEV,) + (s.shape if broadcast else s.shape[1:]), s.dtype) for s in srcs]


def _exchange_semaphores(n):
    return [pltpu.SemaphoreType.DMA((N_DEV - 1, n)), pltpu.SemaphoreType.DMA((N_DEV - 1, n)), pltpu.SemaphoreType.DMA((n,))]


def _exchange_copies(src_refs, dst_refs, send_sems, recv_sems, local_sems, broadcast):
    x, y, c = lax.axis_index("x"), lax.axis_index("y"), lax.axis_index("c")
    me = 4 * x + 2 * y + c
    copies = []
    for k in range(1, N_DEV):
        px = 1 - x if (k >> 2) & 1 else x
        py = 1 - y if (k >> 1) & 1 else y
        pc = 1 - c if k & 1 else c
        peer = 4 * px + 2 * py + pc
        for a, (src, dst) in enumerate(zip(src_refs, dst_refs)):
            copies.append(pltpu.make_async_remote_copy(
                src_ref=src if broadcast else src.at[peer], dst_ref=dst.at[me],
                send_sem=send_sems.at[k - 1, a], recv_sem=recv_sems.at[k - 1, a],
                device_id=(px, py, pc), device_id_type=MESH))
    for a, (src, dst) in enumerate(zip(src_refs, dst_refs)):
        copies.append(pltpu.make_async_copy(src if broadcast else src.at[me], dst.at[me], local_sems.at[a]))
    return copies


def _call_with_exchange(body, *, name, grid, in_specs, out_specs, out_shape, scratch_shapes, args, exchange):
    if exchange is None:
        outs = pl.pallas_call(body, name=name, grid=grid, in_specs=in_specs, out_specs=out_specs, out_shape=out_shape,
                              scratch_shapes=scratch_shapes,
                              compiler_params=_cparams(*["arbitrary"] * len(grid)))(*args)
        return outs, None
    srcs, broadcast = exchange
    n, n_in, n_out, n_scr = len(srcs), len(args), len(out_shape), len(scratch_shapes)
    steps = 1
    for g in grid:
        steps *= g

    def hosted(*refs):
        ins, src_refs = refs[:n_in], refs[n_in:n_in + n]
        outs, dst_refs = refs[n_in + n:n_in + n + n_out], refs[n_in + n + n_out:n_in + 2 * n + n_out]
        scratch = refs[n_in + 2 * n + n_out:]
        step = pl.program_id(0)
        for axis in range(1, len(grid)):
            step = step * grid[axis] + pl.program_id(axis)

        @pl.when(step == 0)
        def _():
            for cp in _exchange_copies(src_refs, dst_refs, *scratch[n_scr:], broadcast):
                cp.start()

        body(*ins, *outs, *scratch[:n_scr])

        @pl.when(step == steps - 1)
        def _():
            for cp in _exchange_copies(src_refs, dst_refs, *scratch[n_scr:], broadcast):
                cp.wait()

    outs = pl.pallas_call(
        hosted, name=name, grid=grid, in_specs=list(in_specs) + [HBM_SPEC] * n, out_specs=list(out_specs) + [HBM_SPEC] * n,
        out_shape=list(out_shape) + _exchange_shapes(srcs, broadcast),
        scratch_shapes=list(scratch_shapes) + _exchange_semaphores(n),
        compiler_params=_cparams(*["arbitrary"] * len(grid)))(*args, *srcs)
    return outs[:n_out], outs[n_out:]


def _regroup_w_in(w):
    wa = w[:, OFF_A:OFF_A + NA]
    seg = lambda off, h, n=HD: w[:, off + h * n: off + (h + 1) * n]
    wb = jnp.concatenate([seg(OFF_B + s * 512, h) for h in range(HGRN_HEADS) for s in range(4)], axis=1)
    parts = []
    for h in range(GDN_QK_HEADS):
        small = jnp.concatenate(
            [w[:, OFF_BETA + 2 * h: OFF_BETA + 2 * h + 2], w[:, OFF_CA + 2 * h: OFF_CA + 2 * h + 2],
             jnp.zeros((w.shape[0], HD - 4), w.dtype)], axis=1)
        parts += [seg(OFF_CQ, h), seg(OFF_CK, h), seg(OFF_CV, h, 2 * HD), seg(OFF_CZ, h, 2 * HD), small]
    wc = jnp.concatenate(parts, axis=1)
    wg = w[:, OFF_G:OFF_G + NG]
    return wa, wb, wc, wg


def _ungroup_dw_in(da, db, dc, dg):
    bq = [jnp.concatenate([db[:, h * 512 + s * HD: h * 512 + (s + 1) * HD] for h in range(HGRN_HEADS)], axis=1)
          for s in range(4)]
    ch = lambda h, lo, hi: dc[:, h * C_HEAD + lo: h * C_HEAD + hi]
    heads = range(GDN_QK_HEADS)
    cq = jnp.concatenate([ch(h, 0, HD) for h in heads], axis=1)
    ck = jnp.concatenate([ch(h, HD, 2 * HD) for h in heads], axis=1)
    cv = jnp.concatenate([ch(h, 2 * HD, 4 * HD) for h in heads], axis=1)
    cz = jnp.concatenate([ch(h, 4 * HD, 6 * HD) for h in heads], axis=1)
    cbeta = jnp.concatenate([ch(h, 6 * HD, 6 * HD + 2) for h in heads], axis=1)
    ca = jnp.concatenate([ch(h, 6 * HD + 2, 6 * HD + 4) for h in heads], axis=1)
    return jnp.concatenate([da] + bq + [cq, ck, cv, cbeta, ca, cz, dg], axis=1)


def _regroup_conv_c(cw):
    parts = []
    for h in range(GDN_QK_HEADS):
        parts += [cw[:, h * HD:(h + 1) * HD], cw[:, 512 + h * HD: 512 + (h + 1) * HD],
                  cw[:, 1024 + 2 * h * HD: 1024 + (2 * h + 2) * HD]]
    return jnp.concatenate(parts, axis=1)


def _ungroup_conv_c(d):
    heads = range(GDN_QK_HEADS)
    q = jnp.concatenate([d[:, h * 512: h * 512 + HD] for h in heads], axis=1)
    k = jnp.concatenate([d[:, h * 512 + HD: h * 512 + 2 * HD] for h in heads], axis=1)
    v = jnp.concatenate([d[:, h * 512 + 2 * HD: h * 512 + 4 * HD] for h in heads], axis=1)
    return jnp.concatenate([q, k, v], axis=1)


def _numel(shape):
    n = 1
    for d in shape:
        n *= d
    return n


def _pack(arrays, rows):
    flat = jnp.concatenate([a.reshape(-1) for a in arrays])
    return jnp.pad(flat, (0, rows * 128 - flat.shape[0])).reshape(rows, 128)


def _unpack(packed, shapes):
    flat = packed.reshape(-1)
    out, off = [], 0
    for s in shapes:
        out.append(flat[off:off + _numel(s)].reshape(s))
        off += _numel(s)
    return out


def _rows_for(shapes):
    return -(-sum(_numel(s) for s in shapes) // 1024) * 8


def kernel(x, norm_w, w_in, b_gate, conv_a, conv_c, a_log, dt_bias, lower_bounds, hgrn_norm_w, gdn_norm_w, w_out_a, w_out_b, w_out_c, w_o, final_norm_w, loss_target, m_norm_w, m_w_in, m_b_gate, m_conv_a, m_conv_c, m_a_log, m_dt_bias, m_lower_bounds, m_hgrn_norm_w, m_gdn_norm_w, m_w_out_a, m_w_out_b, m_w_out_c, m_w_o, m_final_norm_w, v_norm_w, v_w_in, v_b_gate, v_conv_a, v_conv_c, v_a_log, v_dt_bias, v_lower_bounds, v_hgrn_norm_w, v_gdn_norm_w, v_w_out_a, v_w_out_b, v_w_out_c, v_w_o, v_final_norm_w):
    me = 4 * lax.axis_index("x") + 2 * lax.axis_index("y") + lax.axis_index("c")
    xs = x[0]
    target = loss_target[0]
    in_shard = w_in.shape[2]

    big = [w_in, w_out_a, w_out_b, w_out_c, w_o]
    shards_of = lambda l: [w[l].astype(BF16) for w in big]
    conv_shapes = [(DEPTH, 3, CONV_W), (DEPTH, 4, 2048)]
    conv_rows = _rows_for(conv_shapes)
    ca_full = lax.dynamic_update_slice(jnp.zeros(conv_shapes[0], F32), conv_a, (0, 0, me * conv_a.shape[2]))
    cc_full = lax.dynamic_update_slice(jnp.zeros(conv_shapes[1], F32), conv_c, (0, 0, me * conv_c.shape[2]))
    conv_parts, = _exchange([_pack([ca_full, cc_full], conv_rows)], "gather_conv", broadcast=True)
    conv_a_full, conv_c_full = _unpack(_sum_slots(conv_parts, "sum_conv"), conv_shapes)

    lb_pad = jnp.pad(lower_bounds, ((0, 8 - DEPTH), (0, 0)))
    lbs = _lower_bounds_fwd(lb_pad, "lower_bounds_fwd")

    def layer_weights(l, gathered):
        g_in, g_oa, g_ob, g_oc, g_o = gathered
        wa, wb, wc, wg = _regroup_w_in(jnp.concatenate([g_in[q] for q in range(N_DEV)], axis=1))
        lanes = lambda vec: jnp.pad(vec.reshape(GDN_QK_HEADS, 1, 2), ((0, 0), (0, 0), (0, HD - 2)))
        cpar = jnp.concatenate([lanes(a_log[l]), lanes(dt_bias[l]), jnp.zeros((GDN_QK_HEADS, 6, HD), F32)], axis=1)
        return dict(
            wa=wa, wb=wb, wc=wc, wg=wg, cpar=cpar,
            woa=jnp.concatenate([g_oa[q] for q in range(N_DEV)], axis=1),
            wob=jnp.concatenate([g_ob[q] for q in range(N_DEV)], axis=1), woc=g_oc.reshape(D, D), wo=g_o.reshape(D, D),
            nw=norm_w[l:l + 1], bg=b_gate[l:l + 1], cwa=conv_a_full[l], cwc=_regroup_conv_c(conv_c_full[l]),
            lb=lbs[l:l + 1], hnw=hgrn_norm_w[l:l + 1], gnw=gdn_norm_w[l:l + 1])

    layers = [layer_weights(0, _exchange(shards_of(0), "gather_l0", broadcast=True))]

    saved = []
    cur = xs
    for l in range(DEPTH):
        L = layers[l]
        n = f"l{l}_"
        h = _rmsnorm_fwd(cur, L["nw"], n + "rms")
        pa = _matmul(h, L["wa"], "nn", n + "proj_a")
        pb = _matmul(h, L["wb"], "nn", n + "proj_b")
        pc = _matmul(h, L["wc"], "nn", n + "proj_c")
        pg = _matmul(h, L["wg"], "nn", n + "proj_g")
        ua = _branch_a_fwd(pa, L["cwa"], n + "conv_fwd")
        ub, sb = _branch_b_fwd(pb, L["lb"], L["hnw"], n + "hgrn_fwd")
        carry = (shards_of(l + 1), True) if l + 1 < DEPTH else None
        (uc, sc), gathered = _branch_c_fwd(pc, L["cwc"], L["cpar"], L["gnw"], n + "gdn_fwd", exchange=carry)
        if carry is not None:
            layers.append(layer_weights(l + 1, gathered))
        ya =_matmul(ua, L["woa"], "nn", n + "out_a")
        yb = _matmul(ub, L["wob"], "nn", n + "out_b")
        yc = _matmul(uc, L["woc"], "nn", n + "out_c")
        merged = _merge_fwd(pg, L["bg"], ya, yb, yc, n + "merge")
        nxt = _matmul(merged, L["wo"], "nn", n + "out_o", residual=cur)
        saved.append(dict(x=cur, h=h, pa=pa, pb=pb, pc=pc, pg=pg, ua=ua, ub=ub, uc=uc, sb=sb, sc=sc,
                          ya=ya, yb=yb, yc=yc, merged=merged))
        cur = nxt

    loss_part, dx, d_final = _loss_head(cur, final_norm_w.reshape(1, D), target, "loss_head")

    def outgoing(g):
        cols = lambda a, n: jnp.stack([a[:, p * n:(p + 1) * n] for p in range(N_DEV)]).astype(BF16)
        rows = lambda a: a.reshape(N_DEV, a.shape[0] // N_DEV, a.shape[1]).astype(BF16)
        return [cols(g["w_in"], in_shard), cols(g["w_out_a"], 128), cols(g["w_out_b"], 128), rows(g["w_out_c"]), rows(g["w_o"])]

    grads = [None] * DEPTH
    dlbs_rows = [None] * DEPTH
    incoming = [None] * DEPTH
    for l in reversed(range(DEPTH)):
        L, S = layers[l], saved[l]
        n = f"l{l}_"
        dmerged = _matmul(dx, L["wo"], "nt", n + "d_merged")
        d_wo = _matmul(S["merged"], dx, "tn", n + "dw_o")
        dpg, dya, dyb, dyc, d_bg = _merge_bwd(dmerged, S["pg"], L["bg"], S["ya"], S["yb"], S["yc"], n + "merge_bwd")
        dua = _matmul(dya, L["woa"], "nt", n + "d_ua")
        dub = _matmul(dyb, L["wob"], "nt", n + "d_ub")
        duc = _matmul(dyc, L["woc"], "nt", n + "d_uc")
        d_woa = _matmul(S["ua"], dya, "tn", n + "dw_out_a")
        d_wob = _matmul(S["ub"], dyb, "tn", n + "dw_out_b")
        d_woc = _matmul(S["uc"], dyc, "tn", n + "dw_out_c")
        dpa, d_cwa = _branch_a_bwd(S["pa"], L["cwa"], dua, n + "conv_bwd")
        dpb, d_lb, d_hnw = _branch_b_bwd(S["pb"], S["sb"], L["lb"], L["hnw"], dub, n + "hgrn_bwd")
        carry = (outgoing(grads[l + 1]), False) if l + 1 < DEPTH else None
        (dpc, d_cwc, d_cpar, d_gnw), arrived = _branch_c_bwd(S["pc"], S["sc"], L["cwc"], L["cpar"], L["gnw"], duc,
                                                             n + "gdn_bwd", exchange=carry)
        if carry is not None:
            incoming[l + 1] = arrived
        dh = _matmul(dpa, L["wa"], "nt", n + "dh_a")
        dh = _matmul(dpb, L["wb"], "nt", n + "dh_b", residual=dh)
        dh = _matmul(dpc, L["wc"], "nt", n + "dh_c", residual=dh)
        dh = _matmul(dpg, L["wg"], "nt", n + "dh_g", residual=dh)
        d_win = _ungroup_dw_in(_matmul(S["h"], dpa, "tn", n + "dw_a"), _matmul(S["h"], dpb, "tn", n + "dw_b"),
                               _matmul(S["h"], dpc, "tn", n + "dw_c"), _matmul(S["h"], dpg, "tn", n + "dw_g"))
        dx, d_nw = _rmsnorm_bwd(dh, S["x"], L["nw"], dx, n + "rms_bwd")
        dlbs_rows[l] = d_lb
        grads[l] = dict(w_in=d_win, w_out_a=d_woa, w_out_b=d_wob, w_out_c=d_woc, w_o=d_wo, norm_w=d_nw[0],
                        b_gate=d_bg[0], conv_a=d_cwa, conv_c=_ungroup_conv_c(d_cwc),
                        a_log=d_cpar[:, 0, 0:2].reshape(-1), dt_bias=d_cpar[:, 1, 0:2].reshape(-1),
                        hgrn_norm_w=d_hnw[0], gdn_norm_w=d_gnw[0])
    grad_x = dx[None]
    d_lower = _lower_bounds_bwd(lb_pad, jnp.pad(jnp.concatenate(dlbs_rows, axis=0), ((0, 8 - DEPTH), (0, 0))),
                                "lower_bounds_bwd")[:DEPTH]

    incoming[0] = _exchange(outgoing(grads[0]), "exchange_grads_l0", broadcast=False)
    stack = lambda name: jnp.stack([grads[l][name] for l in range(DEPTH)])
    big_out = {}
    for j, (name, w, m, v) in enumerate((("w_in", w_in, m_w_in, v_w_in), ("w_out_a", w_out_a, m_w_out_a, v_w_out_a),
                                         ("w_out_b", w_out_b, m_w_out_b, v_w_out_b), ("w_out_c", w_out_c, m_w_out_c, v_w_out_c),
                                         ("w_o", w_o, m_w_o, v_w_o))):
        parts = jnp.concatenate([incoming[l][j] for l in range(DEPTH)], axis=1)
        r2 = lambda a: a.reshape(parts.shape[1], parts.shape[2])
        outs = _sum_adamw(parts, r2(w), r2(m), r2(v), "adamw_" + name)
        big_out[name] = [o.reshape(w.shape) for o in outs]

    small_names = ["norm_w", "b_gate", "conv_a", "conv_c", "a_log", "dt_bias", "lower_bounds", "hgrn_norm_w",
                   "gdn_norm_w", "final_norm_w", "loss"]
    small_vals = {k: stack(k) for k in ("norm_w", "b_gate", "conv_a", "conv_c", "a_log", "dt_bias", "hgrn_norm_w", "gdn_norm_w")}
    small_vals.update(lower_bounds=d_lower, final_norm_w=d_final[0], loss=loss_part.reshape(1))
    small_shapes = [small_vals[k].shape for k in small_names]
    small_rows = _rows_for(small_shapes)
    small_parts, = _exchange([_pack([small_vals[k] for k in small_names], small_rows)], "exchange_small", broadcast=True)
    total = dict(zip(small_names, _unpack(_sum_slots(small_parts, "sum_small"), small_shapes)))
    loss = total["loss"][0]
    g_conv_a = lax.dynamic_slice(total["conv_a"], (0, 0, me * conv_a.shape[2]), conv_a.shape)
    g_conv_c = lax.dynamic_slice(total["conv_c"], (0, 0, me * conv_c.shape[2]), conv_c.shape)

    small_w = dict(norm_w=(norm_w, m_norm_w, v_norm_w), b_gate=(b_gate, m_b_gate, v_b_gate),
                   conv_a=(conv_a, m_conv_a, v_conv_a), conv_c=(conv_c, m_conv_c, v_conv_c),
                   a_log=(a_log, m_a_log, v_a_log), dt_bias=(dt_bias, m_dt_bias, v_dt_bias),
                   lower_bounds=(lower_bounds, m_lower_bounds, v_lower_bounds),
                   hgrn_norm_w=(hgrn_norm_w, m_hgrn_norm_w, v_hgrn_norm_w), gdn_norm_w=(gdn_norm_w, m_gdn_norm_w, v_gdn_norm_w),
                   final_norm_w=(final_norm_w, m_final_norm_w, v_final_norm_w))
    small_g = dict(total, conv_a=g_conv_a, conv_c=g_conv_c)
    upd_names = small_names[:-1]
    upd_shapes = [small_w[k][0].shape for k in upd_names]
    upd_rows = _rows_for(upd_shapes)
    pk = lambda j: _pack([small_w[k][j] for k in upd_names], upd_rows)
    s_delta, s_m, s_v = _adamw(_pack([small_g[k] for k in upd_names], upd_rows), pk(0), pk(1), pk(2), "adamw_small")
    small_out = {k: [small_g[k], d, mm, vv] for k, d, mm, vv in
                 zip(upd_names, _unpack(s_delta, upd_shapes), _unpack(s_m, upd_shapes), _unpack(s_v, upd_shapes))}

    order = ["norm_w", "w_in", "b_gate", "conv_a", "conv_c", "a_log", "dt_bias", "lower_bounds", "hgrn_norm_w",
             "gdn_norm_w", "w_out_a", "w_out_b", "w_out_c", "w_o", "final_norm_w"]
    res = {**small_out, **big_out}
    outs = [loss, grad_x]
    for j in range(4):
        outs += [res[k][j] for k in order]
    return tuple(outs)
```

```python
import functools

import jax
import jax.numpy as jnp
from jax import lax
from jax.experimental import pallas as pl
from jax.experimental.pallas import tpu as pltpu

F32 = jnp.float32
BF16 = jnp.bfloat16
MESH = pl.DeviceIdType.MESH

N_DEV = 8
D = 1024
DEPTH = 2
CHUNK = 64
HGRN_BLOCK_CHUNKS = 8
GDN_BLOCK_CHUNKS = 8
GROUP = 128
NORM_EPS = 1e-6
L2_EPS = 1e-6
MIN_F = 1e-30
HD = 128
HGRN_HEADS = 4
GDN_QK_HEADS = 4
CONV_W = 512
IN_COLS = 10256
OFF_A, OFF_B, OFF_CQ, OFF_CK, OFF_CV, OFF_BETA, OFF_CA, OFF_CZ, OFF_G = (
    0, 2048, 4096, 4608, 5120, 6144, 6152, 6160, 7184)
NA, NB, NC_COLS, NG = 2048, 2048, 3584, 3072
C_HEAD = 896

ADAM_LR, ADAM_B1, ADAM_B2, ADAM_EPS, ADAM_WD, ADAM_STEP = 0.001, 0.9, 0.999, 1e-08, 0.01, 10

VMEM_LIMIT = 56 * 1024 * 1024
MM_TILE = 1024


def _cparams(*sem):
    return pltpu.CompilerParams(dimension_semantics=sem, vmem_limit_bytes=VMEM_LIMIT)


def _tile(dim, cap):
    if dim <= cap:
        return dim
    t = (cap // 128) * 128
    while dim % t:
        t -= 128
    return t


def _sigmoid(x):
    return 1.0 / (1.0 + jnp.exp(-x))


def _silu(x):
    return x * _sigmoid(x)


def _softplus(x):
    return jnp.maximum(x, 0.0) + jnp.log(1.0 + jnp.exp(-jnp.abs(x)))


def _dot(a, b, dims, precision=None):
    if precision is None:
        a, b = a.astype(BF16), b.astype(BF16)
    return lax.dot_general(a, b, (dims, ((), ())), precision=precision, preferred_element_type=F32)


def _nn(a, b, precision=None):
    return _dot(a, b, ((1,), (0,)), precision)


def _nt(a, b, precision=None):
    return _dot(a, b, ((1,), (1,)), precision)


def _tn(a, b, precision=None):
    return _dot(a, b, ((0,), (0,)), precision)


def _sum_rows_exact(mat01, x):
    m = mat01.astype(BF16)
    hi = x.astype(BF16)
    rest = x - hi.astype(F32)
    mid = rest.astype(BF16)
    low = (rest - mid.astype(F32)).astype(BF16)
    return _nn(m, hi) + _nn(m, mid) + _nn(m, low)


@functools.partial(jax.custom_vjp, nondiff_argnums=(1,))
def _shift_rows(x, d):
    return x if d == 0 else pltpu.roll(x, d, 0)


def _shift_rows_fwd(x, d):
    return _shift_rows(x, d), None


def _shift_rows_bwd(d, _, ct):
    return ((ct if d == 0 else pltpu.roll(ct, ct.shape[0] - d, 0)),)


_shift_rows.defvjp(_shift_rows_fwd, _shift_rows_bwd)


def _iota2(shape):
    return lax.broadcasted_iota(jnp.int32, shape, 0), lax.broadcasted_iota(jnp.int32, shape, 1)


def _lane_pick(x, i):
    lane = lax.broadcasted_iota(jnp.int32, x.shape, 1)
    return jnp.sum(jnp.where(lane == i, x, 0.0), axis=1, keepdims=True)


def _hgrn_block(qr, fr, ir, zr, st0, lb, nw):
    rows = qr.shape[0]
    r, c = _iota2((CHUNK, CHUNK))
    halves = [1 << j for j in range(CHUNK.bit_length() - 1)]
    mats = [c <= r, c > r]
    pairs = []
    for hb in halves:
        same = (r // hb) == (c // hb)
        if hb > 1:
            mats += [(c <= r) & same, (c > r) & same]
        pairs.append(((r // (2 * hb)) == (c // (2 * hb))) & ((r // hb) == (c // hb) + 1))
    stack = jnp.concatenate([m.astype(F32) for m in mats], axis=0)

    q = _silu(qr) * (HD ** -0.5)
    fg = lb + (1.0 - lb) * _sigmoid(fr)
    logf = jnp.log(jnp.maximum(fg, MIN_F))
    kk = 1.0 - fg
    v = ir

    chunks = [slice(s, s + CHUNK) for s in range(0, rows, CHUNK)]
    cums = [_sum_rows_exact(stack, logf[sl]) for sl in chunks]
    part = lambda i: jnp.concatenate([cs[i * CHUNK:(i + 1) * CHUNK] for cs in cums], axis=0)
    qg = q * jnp.exp(part(0))
    ks = kk * jnp.exp(part(1))
    q_lv = [q * jnp.exp(logf)] + [q * jnp.exp(part(2 * j)) for j in range(1, len(halves))]
    k_lv = [kk] + [kk * jnp.exp(part(2 * j + 1)) for j in range(1, len(halves))]
    st = st0
    outs = []
    for sl in chunks:
        scores = jnp.where(pairs[0], _nt(q_lv[0][sl], k_lv[0][sl]), 0.0)
        for j in range(1, len(halves)):
            scores += jnp.where(pairs[j], _nt(q_lv[j][sl], k_lv[j][sl]), 0.0)
        outs.append(_nn(scores, v[sl]) + _nt(qg[sl], st))
        st = st * jnp.exp(jnp.sum(logf[sl], axis=0, keepdims=True)) + _tn(v[sl], ks[sl])
    o = jnp.concatenate(outs, axis=0) + jnp.sum(q * kk, axis=1, keepdims=True) * v
    y = o * lax.rsqrt(jnp.mean(o * o, axis=1, keepdims=True) + NORM_EPS) * nw * _silu(zr)
    return y, st


def _unit_lower_inverses(ms):
    r, c = _iota2(ms[0].shape)
    xs = [jnp.where(r == c, 1.0, 0.0) - jnp.where((r // 2) == (c // 2), m, 0.0) for m in ms]
    b = 2
    while b < CHUNK:
        pick = ((r // (2 * b)) == (c // (2 * b))) & ((r // b) != (c // b))
        ts = [_nn(x, jnp.where(pick, m, 0.0)) for x, m in zip(xs, ms)]
        xs = [x - _nn(t, x) for x, t in zip(xs, ts)]
        b *= 2
    return tuple(x.astype(BF16) for x in xs)


@jax.custom_vjp
def _known_inverses(ms, xs):
    return xs


def _known_inverses_fwd(ms, xs):
    return xs, xs


def _known_inverses_bwd(xs, cts):
    r, c = _iota2(xs[0].shape)
    keep = (c < r) & ((r // CHUNK) == (c // CHUNK))
    ts = [_tn(x, ct) for x, ct in zip(xs, cts)]
    return (tuple(jnp.where(keep, -_nt(t, x), 0.0) for t, x in zip(ts, xs)), tuple(jnp.zeros_like(x) for x in xs))


_known_inverses.defvjp(_known_inverses_fwd, _known_inverses_bwd)


def _chunk_cumsum(x):
    row = lax.broadcasted_iota(jnp.int32, x.shape, 0) % CHUNK
    d = 1
    while d < CHUNK:
        x = x + jnp.where(row >= d, _shift_rows(x, d), 0.0)
        d *= 2
    return x


def _gdn_block(x_ext, z, ba, s0a, s0b, w0, w1, w2, w3, alog, dtb, nw, known=None):
    rows = z.shape[0]
    conv = (w0 * _shift_rows(x_ext, 3) + w1 * _shift_rows(x_ext, 2) + w2 * _shift_rows(x_ext, 1) + w3 * x_ext)
    cc = _silu(conv[8:])
    qc, kc = cc[:, 0:HD], cc[:, HD:2 * HD]
    q = qc * lax.rsqrt(jnp.sum(qc * qc, axis=1, keepdims=True) + L2_EPS) * (HD ** -0.5)
    k = kc * lax.rsqrt(jnp.sum(kc * kc, axis=1, keepdims=True) + L2_EPS)

    r, c = _iota2((GROUP, GROUP))
    same = (r // CHUNK) == (c // CHUNK)
    causal, strict, eye = same & (c <= r), same & (c < r), r == c
    heads = (0, 1)
    groups = [slice(lo, lo + GROUP) for lo in range(0, rows, GROUP)]
    chunks = [slice(lo, lo + CHUNK) for lo in range(0, rows, CHUNK)]

    v, loga, g_w, kb, kg, qg = [], [], [], [], [], []
    for i in heads:
        v.append(cc[:, (2 + i) * HD:(3 + i) * HD])
        beta = _sigmoid(_lane_pick(ba, i))
        a_neg = -jnp.exp(_lane_pick(alog, i))
        loga.append(a_neg * _softplus(_lane_pick(ba, 2 + i) + _lane_pick(dtb, i)))
        g_w.append(_chunk_cumsum(jnp.broadcast_to(loga[i], (rows, HD))))
        kb.append(k * beta)
        kg.append(k * jnp.exp(g_w[i]))
        qg.append(q * jnp.exp(g_w[i]))

    systems = [(i, gs) for gs in groups for i in heads]
    dec_c, ms = [], []
    for i, gs in systems:
        g_sq = g_w[i][gs]
        g_row = jnp.sum(jnp.where(eye, g_sq, 0.0), axis=0, keepdims=True)
        diff = g_sq - g_row
        dec_c.append(jnp.where(causal, jnp.exp(jnp.where(causal, diff, 0.0)), 0.0))
        ms.append(jnp.where(strict, _nt(k[gs], kb[i][gs]) * dec_c[-1], 0.0))
    xs = _unit_lower_inverses(tuple(ms)) if known is None else _known_inverses(tuple(ms), known)
    u = [[None] * len(groups) for _ in heads]
    w = [[None] * len(groups) for _ in heads]
    qk = [[None] * len(groups) for _ in heads]
    for n, (i, gs) in enumerate(systems):
        j = n // len(heads)
        u[i][j] = _nn(xs[n], v[i][gs])
        w[i][j] = _nn(xs[n], kg[i][gs])
        qk[i][j] = _nt(q[gs], kb[i][gs]) * dec_c[n]
    u = [jnp.concatenate(p, axis=0) for p in u]
    w = [jnp.concatenate(p, axis=0) for p in w]

    decay, p_mat, q_mat = {}, {}, {}
    for n, sl in enumerate(chunks):
        for i in heads:
            g_last = jnp.sum(loga[i][sl], axis=0, keepdims=True)
            kd = kb[i][sl] * jnp.exp(g_last - g_w[i][sl])
            decay[n, i] = jnp.exp(g_last)
            p_mat[n, i] = -_tn(kd, w[i][sl])
            q_mat[n, i] = _tn(kd, u[i][sl])
    s = [s0a, s0b]
    s_at = {}
    for n in range(len(chunks)):
        for i in heads:
            s_at[n, i] = s[i]
            s[i] = s[i] * decay[n, i] + _nn(p_mat[n, i], s[i]) + q_mat[n, i]

    ys = []
    for i in heads:
        e = jnp.concatenate([u[i][sl] - _nn(w[i][sl], s_at[n, i]) for n, sl in enumerate(chunks)], axis=0)
        o_state = jnp.concatenate([_nn(qg[i][sl], s_at[n, i]) for n, sl in enumerate(chunks)], axis=0)
        o = o_state + jnp.concatenate([_nn(qk[i][j], e[gs]) for j, gs in enumerate(groups)], axis=0)
        zi = z[:, i * HD:(i + 1) * HD]
        ys.append(o * lax.rsqrt(jnp.mean(o * o, axis=1, keepdims=True) + NORM_EPS) * nw * _silu(zi))
    return (jnp.concatenate(ys, axis=1), s[0], s[1]), xs


def _add_to_tail(x, tail):
    return x + jnp.concatenate([jnp.zeros((x.shape[0] - 8, x.shape[1]), x.dtype), tail], axis=0)


def _conv_a_block(ab, ac_ext, ax_ext, az, w0, w1, w2):
    u = ac_ext * ax_ext
    conv = (w0 * _shift_rows(u, 2) + w1 * _shift_rows(u, 1) + w2 * u)[8:]
    return ab * conv * _silu(az)


def _matmul(a, b, mode, name, residual=None, out_dtype=F32):
    if mode == "nn":
        (m, k), n = a.shape, b.shape[1]
    elif mode == "nt":
        (m, k), n = a.shape, b.shape[0]
    else:
        (k, m), n = a.shape, b.shape[1]
    tm, tn, tk = _tile(m, MM_TILE), _tile(n, MM_TILE), _tile(k, MM_TILE)
    nk = k // tk
    dims = {"nn": ((1,), (0,)), "nt": ((1,), (1,)), "tn": ((0,), (0,))}[mode]
    a_spec = pl.BlockSpec((tk, tm), lambda i, j, s: (s, i)) if mode == "tn" else pl.BlockSpec((tm, tk), lambda i, j, s: (i, s))
    b_spec = pl.BlockSpec((tn, tk), lambda i, j, s: (j, s)) if mode == "nt" else pl.BlockSpec((tk, tn), lambda i, j, s: (s, j))
    o_spec = pl.BlockSpec((tm, tn), lambda i, j, s: (i, j))
    has_res = residual is not None

    def finish(out, r_ref, o_ref):
        if has_res:
            out = out + r_ref[...]
        o_ref[...] = out.astype(out_dtype)

    def body_one_pass(*refs):
        finish(_dot(refs[0][...], refs[1][...], dims), refs[2] if has_res else None, refs[-1])

    def body_reduce(*refs):
        a_ref, b_ref = refs[0], refs[1]
        r_ref = refs[2] if has_res else None
        o_ref, acc_ref = refs[-2], refs[-1]
        s = pl.program_id(2)

        @pl.when(s == 0)
        def _():
            acc_ref[...] = jnp.zeros_like(acc_ref)

        acc_ref[...] += _dot(a_ref[...], b_ref[...], dims)

        @pl.when(s == nk - 1)
        def _():
            finish(acc_ref[...], r_ref, o_ref)

    args, specs = [a, b], [a_spec, b_spec]
    if has_res:
        args.append(residual)
        specs.append(o_spec)
    return pl.pallas_call(
        body_one_pass if nk == 1 else body_reduce, name=name, grid=(m // tm, n // tn, nk), in_specs=specs, out_specs=o_spec,
        out_shape=jax.ShapeDtypeStruct((m, n), out_dtype),
        scratch_shapes=[] if nk == 1 else [pltpu.VMEM((tm, tn), F32)],
        compiler_params=_cparams("parallel", "parallel", "arbitrary"))(*args)


def _rmsnorm_fwd(x, w, name):
    t = x.shape[0]
    blk = _tile(t, 512)

    def body(x_ref, w_ref, h_ref):
        xv = x_ref[...]
        h_ref[...] = (xv * lax.rsqrt(jnp.mean(xv * xv, axis=1, keepdims=True) + NORM_EPS) * w_ref[...]).astype(BF16)

    return pl.pallas_call(
        body, name=name, grid=(t // blk,),
        in_specs=[pl.BlockSpec((blk, D), lambda i: (i, 0)), pl.BlockSpec((1, D), lambda i: (0, 0))],
        out_specs=pl.BlockSpec((blk, D), lambda i: (i, 0)), out_shape=jax.ShapeDtypeStruct((t, D), BF16),
        compiler_params=_cparams("parallel"))(x, w)


def _rmsnorm_bwd(dh, x, w, dxo, name):
    t = x.shape[0]
    blk = _tile(t, 512)

    def body(dh_ref, x_ref, w_ref, dxo_ref, dx_ref, dw_ref):
        @pl.when(pl.program_id(0) == 0)
        def _():
            dw_ref[...] = jnp.zeros_like(dw_ref)

        xv, dhv = x_ref[...], dh_ref[...]
        rs = lax.rsqrt(jnp.mean(xv * xv, axis=1, keepdims=True) + NORM_EPS)
        xh = xv * rs
        dw_ref[...] += jnp.sum(dhv * xh, axis=0, keepdims=True)
        dxh = dhv * w_ref[...]
        dx_ref[...] = rs * (dxh - xh * jnp.mean(dxh * xh, axis=1, keepdims=True)) + dxo_ref[...]

    row = pl.BlockSpec((blk, D), lambda i: (i, 0))
    vec = pl.BlockSpec((1, D), lambda i: (0, 0))
    return pl.pallas_call(
        body, name=name, grid=(t // blk,), in_specs=[row, row, vec, row], out_specs=[row, vec],
        out_shape=[jax.ShapeDtypeStruct((t, D), F32), jax.ShapeDtypeStruct((1, D), F32)],
        compiler_params=_cparams("arbitrary"))(dh, x, w, dxo)


def _loss_head(x, w, target, name):
    t = x.shape[0]
    blk = _tile(t, 512)

    def body(x_ref, w_ref, t_ref, loss_ref, dx_ref, dw_ref):
        @pl.when(pl.program_id(0) == 0)
        def _():
            dw_ref[...] = jnp.zeros_like(dw_ref)
            loss_ref[...] = jnp.zeros_like(loss_ref)

        xv = x_ref[...]
        rs = lax.rsqrt(jnp.mean(xv * xv, axis=1, keepdims=True) + NORM_EPS)
        xh = xv * rs
        err = xh * w_ref[...] - t_ref[...]
        loss_ref[...] += 0.5 * jnp.sum(jnp.mean(err * err, axis=1, keepdims=True), axis=0, keepdims=True)
        dy = err * (1.0 / D)
        dw_ref[...] += jnp.sum(dy * xh, axis=0, keepdims=True)
        dxh = dy * w_ref[...]
        dx_ref[...] = rs * (dxh - xh * jnp.mean(dxh * xh, axis=1, keepdims=True))

    row = pl.BlockSpec((blk, D), lambda i: (i, 0))
    vec = pl.BlockSpec((1, D), lambda i: (0, 0))
    return pl.pallas_call(
        body, name=name, grid=(t // blk,), in_specs=[row, vec, row],
        out_specs=[pl.BlockSpec((1, 1), lambda i: (0, 0)), row, vec],
        out_shape=[jax.ShapeDtypeStruct((1, 1), F32), jax.ShapeDtypeStruct((t, D), F32), jax.ShapeDtypeStruct((1, D), F32)],
        compiler_params=_cparams("arbitrary"))(x, w, target)


def _lbs_of(lb):
    r = lax.broadcasted_iota(jnp.int32, lb.shape, 0)
    real = r < DEPTH
    mx = lax.stop_gradient(jnp.max(jnp.where(real, lb, -jnp.inf), axis=0, keepdims=True))
    e = jnp.where(real, jnp.exp(jnp.where(real, lb - mx, 0.0)), 0.0)
    p = e / jnp.sum(e, axis=0, keepdims=True)
    out = jnp.zeros_like(lb)
    run = jnp.zeros_like(mx)
    for l in range(1, DEPTH):
        run = run + jnp.sum(jnp.where(r == l, p, 0.0), axis=0, keepdims=True)
        out = out + jnp.where(r == l, run, 0.0)
    return out


def _lower_bounds_fwd(lbp, name):
    def body(lb_ref, o_ref):
        o_ref[...] = _lbs_of(lb_ref[...])

    return pl.pallas_call(body, name=name, out_shape=jax.ShapeDtypeStruct(lbp.shape, F32))(lbp)


def _lower_bounds_bwd(lbp, dlbs, name):
    def body(lb_ref, d_ref, o_ref):
        _, vjp = jax.vjp(_lbs_of, lb_ref[...])
        o_ref[...] = vjp(d_ref[...])[0]

    return pl.pallas_call(body, name=name, out_shape=jax.ShapeDtypeStruct(lbp.shape, F32))(lbp, dlbs)


def _branch_a_fwd(pa, cw, name):
    t = pa.shape[0]
    blk = _tile(t, 512)
    W = CONV_W

    def body(p_ref, w_ref, y_ref, hc_ref, hx_ref):
        @pl.when(pl.program_id(0) == 0)
        def _():
            hc_ref[...] = jnp.zeros_like(hc_ref)
            hx_ref[...] = jnp.zeros_like(hx_ref)

        ac, ax = p_ref[:, W:2 * W], p_ref[:, 2 * W:3 * W]
        y_ref[...] = _conv_a_block(
            p_ref[:, 0:W], jnp.concatenate([hc_ref[...], ac], axis=0), jnp.concatenate([hx_ref[...], ax], axis=0),
            p_ref[:, 3 * W:4 * W], w_ref[0:1, :], w_ref[1:2, :], w_ref[2:3, :]).astype(BF16)
        hc_ref[...] = p_ref[blk - 8:blk, W:2 * W]
        hx_ref[...] = p_ref[blk - 8:blk, 2 * W:3 * W]

    return pl.pallas_call(
        body, name=name, grid=(t // blk,),
        in_specs=[pl.BlockSpec((blk, NA), lambda i: (i, 0)), pl.BlockSpec((3, W), lambda i: (0, 0))],
        out_specs=pl.BlockSpec((blk, W), lambda i: (i, 0)), out_shape=jax.ShapeDtypeStruct((t, W), BF16),
        scratch_shapes=[pltpu.VMEM((8, W), F32), pltpu.VMEM((8, W), F32)],
        compiler_params=_cparams("arbitrary"))(pa, cw)


def _branch_a_bwd(pa, cw, dy, name):
    t = pa.shape[0]
    blk = _tile(t, 512)
    nt_ = t // blk
    W = CONV_W
    hb = blk // 8

    def body(p_ref, halo_ref, w_ref, dy_ref, dp_ref, dw_ref, chc_ref, chx_ref):
        i = pl.program_id(0)

        @pl.when(i == 0)
        def _():
            chc_ref[...] = jnp.zeros_like(chc_ref)
            chx_ref[...] = jnp.zeros_like(chx_ref)
            dw_ref[...] = jnp.zeros_like(dw_ref)

        keep = 1.0 - (i == nt_ - 1).astype(F32)
        hc = halo_ref[:, W:2 * W] * keep
        hx = halo_ref[:, 2 * W:3 * W] * keep
        ac_ext = jnp.concatenate([hc, p_ref[:, W:2 * W]], axis=0)
        ax_ext = jnp.concatenate([hx, p_ref[:, 2 * W:3 * W]], axis=0)
        _, vjp = jax.vjp(_conv_a_block, p_ref[:, 0:W], ac_ext, ax_ext, p_ref[:, 3 * W:4 * W],
                         w_ref[0:1, :], w_ref[1:2, :], w_ref[2:3, :])
        dab, dac, dax, daz, dw0, dw1, dw2 = vjp(dy_ref[...])
        dp_ref[:, 0:W] = dab.astype(BF16)
        dp_ref[:, W:2 * W] = _add_to_tail(dac[8:], chc_ref[...]).astype(BF16)
        dp_ref[:, 2 * W:3 * W] = _add_to_tail(dax[8:], chx_ref[...]).astype(BF16)
        dp_ref[:, 3 * W:4 * W] = daz.astype(BF16)
        chc_ref[...] = dac[:8] * keep
        chx_ref[...] = dax[:8] * keep
        dw_ref[0:1, :] += dw0
        dw_ref[1:2, :] += dw1
        dw_ref[2:3, :] += dw2

    rev = lambda i: (nt_ - 1 - i, 0)
    return pl.pallas_call(
        body, name=name, grid=(nt_,),
        in_specs=[pl.BlockSpec((blk, NA), rev),
                  pl.BlockSpec((8, NA), lambda i: (jnp.maximum((nt_ - 1 - i) * hb - 1, 0), 0)),
                  pl.BlockSpec((3, W), lambda i: (0, 0)),
                  pl.BlockSpec((blk, W), rev)],
        out_specs=[pl.BlockSpec((blk, NA), rev), pl.BlockSpec((3, W), lambda i: (0, 0))],
        out_shape=[jax.ShapeDtypeStruct((t, NA), BF16), jax.ShapeDtypeStruct((3, W), F32)],
        scratch_shapes=[pltpu.VMEM((8, W), F32), pltpu.VMEM((8, W), F32)],
        compiler_params=_cparams("arbitrary"))(pa, pa, cw, dy)


def _block_rows(t, chunks):
    return min(t, chunks * CHUNK)


def _branch_b_fwd(pb, lbs_row, nw, name):
    t = pb.shape[0]
    rows = _block_rows(t, HGRN_BLOCK_CHUNKS)
    nch = t // rows

    def body(p_ref, lb_ref, nw_ref, y_ref, s_ref, st_ref):
        @pl.when(pl.program_id(1) == 0)
        def _():
            st_ref[...] = jnp.zeros_like(st_ref)

        s_ref[0, 0] = st_ref[...]
        y, st1 = _hgrn_block(p_ref[:, 0:HD], p_ref[:, HD:2 * HD], p_ref[:, 2 * HD:3 * HD], p_ref[:, 3 * HD:4 * HD],
                             st_ref[...], lb_ref[...], nw_ref[...])
        y_ref[...] = y.astype(BF16)
        st_ref[...] = st1

    return pl.pallas_call(
        body, name=name, grid=(HGRN_HEADS, nch),
        in_specs=[pl.BlockSpec((rows, 4 * HD), lambda h, i: (i, h)),
                  pl.BlockSpec((1, HD), lambda h, i: (0, h)),
                  pl.BlockSpec((1, HD), lambda h, i: (0, 0))],
        out_specs=[pl.BlockSpec((rows, HD), lambda h, i: (i, h)),
                   pl.BlockSpec((1, 1, HD, HD), lambda h, i: (h, i, 0, 0))],
        out_shape=[jax.ShapeDtypeStruct((t, HGRN_HEADS * HD), BF16),
                   jax.ShapeDtypeStruct((HGRN_HEADS, nch, HD, HD), F32)],
        scratch_shapes=[pltpu.VMEM((HD, HD), F32)],
        compiler_params=_cparams("arbitrary", "arbitrary"))(pb, lbs_row, nw)


def _branch_b_bwd(pb, states, lbs_row, nw, dy, name):
    t = pb.shape[0]
    rows = _block_rows(t, HGRN_BLOCK_CHUNKS)
    nch = t // rows

    def body(p_ref, s_ref, lb_ref, nw_ref, dy_ref, dp_ref, dlb_ref, dnw_ref, ds_ref):
        h, i = pl.program_id(0), pl.program_id(1)

        @pl.when(i == 0)
        def _():
            ds_ref[...] = jnp.zeros_like(ds_ref)
            dlb_ref[...] = jnp.zeros_like(dlb_ref)

        @pl.when((i == 0) & (h == 0))
        def _():
            dnw_ref[...] = jnp.zeros_like(dnw_ref)

        _, vjp = jax.vjp(_hgrn_block, p_ref[:, 0:HD], p_ref[:, HD:2 * HD], p_ref[:, 2 * HD:3 * HD],
                         p_ref[:, 3 * HD:4 * HD], s_ref[0, 0], lb_ref[...], nw_ref[...])
        dq, df, di, dz, ds0, dlb, dnw = vjp((dy_ref[...], ds_ref[...]))
        dp_ref[:, 0:HD] = dq.astype(BF16)
        dp_ref[:, HD:2 * HD] = df.astype(BF16)
        dp_ref[:, 2 * HD:3 * HD] = di.astype(BF16)
        dp_ref[:, 3 * HD:4 * HD] = dz.astype(BF16)
        ds_ref[...] = ds0
        dlb_ref[...] += dlb
        dnw_ref[...] += dnw

    rev = lambda h, i: (nch - 1 - i, h)
    return pl.pallas_call(
        body, name=name, grid=(HGRN_HEADS, nch),
        in_specs=[pl.BlockSpec((rows, 4 * HD), rev),
                  pl.BlockSpec((1, 1, HD, HD), lambda h, i: (h, nch - 1 - i, 0, 0)),
                  pl.BlockSpec((1, HD), lambda h, i: (0, h)),
                  pl.BlockSpec((1, HD), lambda h, i: (0, 0)),
                  pl.BlockSpec((rows, HD), rev)],
        out_specs=[pl.BlockSpec((rows, 4 * HD), rev),
                   pl.BlockSpec((1, HD), lambda h, i: (0, h)),
                   pl.BlockSpec((1, HD), lambda h, i: (0, 0))],
        out_shape=[jax.ShapeDtypeStruct((t, NB), BF16), jax.ShapeDtypeStruct((1, HGRN_HEADS * HD), F32),
                   jax.ShapeDtypeStruct((1, HD), F32)],
        scratch_shapes=[pltpu.VMEM((HD, HD), F32)],
        compiler_params=_cparams("arbitrary", "arbitrary"))(pb, states, lbs_row, nw, dy)


def _branch_c_fwd(pc, cw, cpar, nw, name, exchange=None):
    t = pc.shape[0]
    rows = _block_rows(t, GDN_BLOCK_CHUNKS)
    nch = t // rows
    XW = 4 * HD

    nsys = 2 * rows // GROUP

    def body(p_ref, w_ref, cp_ref, nw_ref, y_ref, s_ref, x_ref, sa_ref, sb_ref, halo_ref):
        @pl.when(pl.program_id(1) == 0)
        def _():
            sa_ref[...] = jnp.zeros_like(sa_ref)
            sb_ref[...] = jnp.zeros_like(sb_ref)
            halo_ref[...] = jnp.zeros_like(halo_ref)

        s_ref[0, 0, 0] = sa_ref[...]
        s_ref[0, 0, 1] = sb_ref[...]
        x_ext = jnp.concatenate([halo_ref[...], p_ref[:, 0:XW]], axis=0)
        (y, s1a, s1b), xs = _gdn_block(x_ext, p_ref[:, XW:XW + 2 * HD], p_ref[:, XW + 2 * HD:XW + 3 * HD],
                                       sa_ref[...], sb_ref[...], w_ref[0:1, :], w_ref[1:2, :], w_ref[2:3, :], w_ref[3:4, :],
                                       cp_ref[0, 0:1, :], cp_ref[0, 1:2, :], nw_ref[...])
        for n in range(nsys):
            x_ref[0, 0, n] = xs[n]
        y_ref[...] = y.astype(BF16)
        sa_ref[...] = s1a
        sb_ref[...] = s1b
        halo_ref[...] = p_ref[rows - 8:rows, 0:XW]

    return _call_with_exchange(
        body, name=name, grid=(GDN_QK_HEADS, nch),
        in_specs=[pl.BlockSpec((rows, C_HEAD), lambda h, i: (i, h)),
                  pl.BlockSpec((4, XW), lambda h, i: (0, h)),
                  pl.BlockSpec((1, 8, HD), lambda h, i: (h, 0, 0)),
                  pl.BlockSpec((1, HD), lambda h, i: (0, 0))],
        out_specs=[pl.BlockSpec((rows, 2 * HD), lambda h, i: (i, h)),
                   pl.BlockSpec((1, 1, 2, HD, HD), lambda h, i: (h, i, 0, 0, 0)),
                   pl.BlockSpec((1, 1, nsys, GROUP, GROUP), lambda h, i: (h, i, 0, 0, 0))],
        out_shape=[jax.ShapeDtypeStruct((t, 2 * GDN_QK_HEADS * HD), BF16),
                   jax.ShapeDtypeStruct((GDN_QK_HEADS, nch, 2, HD, HD), F32),
                   jax.ShapeDtypeStruct((GDN_QK_HEADS, nch, nsys, GROUP, GROUP), BF16)],
        scratch_shapes=[pltpu.VMEM((HD, HD), F32), pltpu.VMEM((HD, HD), F32), pltpu.VMEM((8, XW), F32)],
        args=(pc, cw, cpar, nw), exchange=exchange)


def _branch_c_bwd(pc, states, inverses, cw, cpar, nw, dy, name, exchange=None):
    t = pc.shape[0]
    rows = _block_rows(t, GDN_BLOCK_CHUNKS)
    nch = t // rows
    XW = 4 * HD
    hb = rows // 8

    nsys = 2 * rows // GROUP

    def body(p_ref, halo_ref, s_ref, x_ref, w_ref, cp_ref, nw_ref, dy_ref, dp_ref, dw_ref, dcp_ref, dnw_ref,
             dsa_ref, dsb_ref, carry_ref):
        h, i = pl.program_id(0), pl.program_id(1)

        @pl.when(i == 0)
        def _():
            dsa_ref[...] = jnp.zeros_like(dsa_ref)
            dsb_ref[...] = jnp.zeros_like(dsb_ref)
            carry_ref[...] = jnp.zeros_like(carry_ref)
            dw_ref[...] = jnp.zeros_like(dw_ref)
            dcp_ref[...] = jnp.zeros_like(dcp_ref)

        @pl.when((i == 0) & (h == 0))
        def _():
            dnw_ref[...] = jnp.zeros_like(dnw_ref)

        keep = 1.0 - (i == nch - 1).astype(F32)
        x_ext = jnp.concatenate([halo_ref[:, 0:XW] * keep, p_ref[:, 0:XW]], axis=0)
        block = functools.partial(_gdn_block, known=tuple(x_ref[0, 0, n] for n in range(nsys)))
        _, vjp, _ = jax.vjp(block, x_ext, p_ref[:, XW:XW + 2 * HD], p_ref[:, XW + 2 * HD:XW + 3 * HD],
                            s_ref[0, 0, 0], s_ref[0, 0, 1], w_ref[0:1, :], w_ref[1:2, :], w_ref[2:3, :], w_ref[3:4, :],
                            cp_ref[0, 0:1, :], cp_ref[0, 1:2, :], nw_ref[...], has_aux=True)
        dx, dz, dba, dsa, dsb, dw0, dw1, dw2, dw3, dal, ddt, dnw = vjp((dy_ref[...], dsa_ref[...], dsb_ref[...]))
        dp_ref[:, 0:XW] = _add_to_tail(dx[8:], carry_ref[...]).astype(BF16)
        dp_ref[:, XW:XW + 2 * HD] = dz.astype(BF16)
        dp_ref[:, XW + 2 * HD:XW + 3 * HD] = dba.astype(BF16)
        carry_ref[...] = dx[:8] * keep
        dsa_ref[...] = dsa
        dsb_ref[...] = dsb
        dw_ref[0:1, :] += dw0
        dw_ref[1:2, :] += dw1
        dw_ref[2:3, :] += dw2
        dw_ref[3:4, :] += dw3
        dcp_ref[0, 0:1, :] += dal
        dcp_ref[0, 1:2, :] += ddt
        dnw_ref[...] += dnw

    rev = lambda h, i: (nch - 1 - i, h)
    return _call_with_exchange(
        body, name=name, grid=(GDN_QK_HEADS, nch),
        in_specs=[pl.BlockSpec((rows, C_HEAD), rev),
                  pl.BlockSpec((8, C_HEAD), lambda h, i: (jnp.maximum((nch - 1 - i) * hb - 1, 0), h)),
                  pl.BlockSpec((1, 1, 2, HD, HD), lambda h, i: (h, nch - 1 - i, 0, 0, 0)),
                  pl.BlockSpec((1, 1, nsys, GROUP, GROUP), lambda h, i: (h, nch - 1 - i, 0, 0, 0)),
                  pl.BlockSpec((4, XW), lambda h, i: (0, h)),
                  pl.BlockSpec((1, 8, HD), lambda h, i: (h, 0, 0)),
                  pl.BlockSpec((1, HD), lambda h, i: (0, 0)),
                  pl.BlockSpec((rows, 2 * HD), rev)],
        out_specs=[pl.BlockSpec((rows, C_HEAD), rev),
                   pl.BlockSpec((4, XW), lambda h, i: (0, h)),
                   pl.BlockSpec((1, 8, HD), lambda h, i: (h, 0, 0)),
                   pl.BlockSpec((1, HD), lambda h, i: (0, 0))],
        out_shape=[jax.ShapeDtypeStruct((t, NC_COLS), BF16), jax.ShapeDtypeStruct((4, GDN_QK_HEADS * XW), F32),
                   jax.ShapeDtypeStruct((GDN_QK_HEADS, 8, HD), F32), jax.ShapeDtypeStruct((1, HD), F32)],
        scratch_shapes=[pltpu.VMEM((HD, HD), F32), pltpu.VMEM((HD, HD), F32), pltpu.VMEM((8, XW), F32)],
        args=(pc, pc, states, inverses, cw, cpar, nw, dy), exchange=exchange)


def _merge_fwd(pg, bg, ya, yb, yc, name):
    t = pg.shape[0]
    blk = _tile(t, 256)

    def body(g_ref, b_ref, a_ref, b2_ref, c_ref, o_ref):
        gate = _sigmoid(g_ref[...] + b_ref[...])
        o_ref[...] = (gate[:, 0:D] * a_ref[...] + gate[:, D:2 * D] * b2_ref[...] + gate[:, 2 * D:3 * D] * c_ref[...]).astype(BF16)

    row = pl.BlockSpec((blk, D), lambda i: (i, 0))
    return pl.pallas_call(
        body, name=name, grid=(t // blk,),
        in_specs=[pl.BlockSpec((blk, NG), lambda i: (i, 0)), pl.BlockSpec((1, NG), lambda i: (0, 0)), row, row, row],
        out_specs=row, out_shape=jax.ShapeDtypeStruct((t, D), BF16),
        compiler_params=_cparams("parallel"))(pg, bg, ya, yb, yc)


def _merge_bwd(dm, pg, bg, ya, yb, yc, name):
    t = pg.shape[0]
    blk = _tile(t, 256)

    def body(dm_ref, g_ref, b_ref, a_ref, b2_ref, c_ref, dg_ref, da_ref, db_ref, dc_ref, dbg_ref):
        @pl.when(pl.program_id(0) == 0)
        def _():
            dbg_ref[...] = jnp.zeros_like(dbg_ref)

        gate = _sigmoid(g_ref[...] + b_ref[...])
        dmv = dm_ref[...]
        for j, (y_ref, dy_ref) in enumerate(((a_ref, da_ref), (b2_ref, db_ref), (c_ref, dc_ref))):
            gj = gate[:, j * D:(j + 1) * D]
            dy_ref[...] = (dmv * gj).astype(BF16)
            dgj = dmv * y_ref[...] * gj * (1.0 - gj)
            dg_ref[:, j * D:(j + 1) * D] = dgj.astype(BF16)
            dbg_ref[:, j * D:(j + 1) * D] += jnp.sum(dgj, axis=0, keepdims=True)

    row = pl.BlockSpec((blk, D), lambda i: (i, 0))
    wide = pl.BlockSpec((blk, NG), lambda i: (i, 0))
    vec = pl.BlockSpec((1, NG), lambda i: (0, 0))
    return pl.pallas_call(
        body, name=name, grid=(t // blk,), in_specs=[row, wide, vec, row, row, row],
        out_specs=[wide, row, row, row, vec],
        out_shape=[jax.ShapeDtypeStruct((t, NG), BF16)] + [jax.ShapeDtypeStruct((t, D), BF16)] * 3
                  + [jax.ShapeDtypeStruct((1, NG), F32)],
        compiler_params=_cparams("arbitrary"))(dm, pg, bg, ya, yb, yc)


def _adamw_math(w, g, m, v):
    m = ADAM_B1 * m + (1.0 - ADAM_B1) * g
    v = ADAM_B2 * v + (1.0 - ADAM_B2) * (g * g)
    m_hat = m / (1.0 - ADAM_B1 ** ADAM_STEP)
    v_hat = v / (1.0 - ADAM_B2 ** ADAM_STEP)
    delta = -ADAM_LR * (m_hat / (jnp.sqrt(v_hat) + ADAM_EPS) + ADAM_WD * w)
    return delta, m, v


def _sum_adamw(parts, w, m, v, name):
    r, c = w.shape
    br = r if r <= 256 else 256
    assert r % br == 0

    def body(p_ref, w_ref, m_ref, v_ref, g_ref, d_ref, nm_ref, nv_ref):
        g = p_ref[0].astype(F32)
        for k in range(1, N_DEV):
            g = g + p_ref[k].astype(F32)
        g_ref[...] = g
        d_ref[...], nm_ref[...], nv_ref[...] = _adamw_math(w_ref[...], g, m_ref[...], v_ref[...])

    blk = pl.BlockSpec((br, c), lambda i: (i, 0))
    return pl.pallas_call(
        body, name=name, grid=(r // br,),
        in_specs=[pl.BlockSpec((N_DEV, br, c), lambda i: (0, i, 0)), blk, blk, blk], out_specs=[blk] * 4,
        out_shape=[jax.ShapeDtypeStruct((r, c), F32)] * 4, compiler_params=_cparams("parallel"))(parts, w, m, v)


def _adamw(g, w, m, v, name):
    def body(g_ref, w_ref, m_ref, v_ref, d_ref, nm_ref, nv_ref):
        d_ref[...], nm_ref[...], nv_ref[...] = _adamw_math(w_ref[...], g_ref[...], m_ref[...], v_ref[...])

    return pl.pallas_call(body, name=name, out_shape=[jax.ShapeDtypeStruct(w.shape, F32)] * 3)(g, w, m, v)


def _sum_slots(parts, name):
    def body(p_ref, o_ref):
        g = p_ref[0]
        for k in range(1, N_DEV):
            g = g + p_ref[k]
        o_ref[...] = g

    return pl.pallas_call(body, name=name, out_shape=jax.ShapeDtypeStruct(parts.shape[1:], F32))(parts)


def _exchange(srcs, name, broadcast):
    n = len(srcs)

    def body(*refs):
        copies = _exchange_copies(refs[:n], refs[n:2 * n], *refs[2 * n:], broadcast)
        for cp in copies:
            cp.start()
        for cp in copies:
            cp.wait()

    return pl.pallas_call(
        body, name=name, in_specs=[HBM_SPEC] * n, out_specs=[HBM_SPEC] * n, out_shape=_exchange_shapes(srcs, broadcast),
        scratch_shapes=_exchange_semaphores(n))(*srcs)


HBM_SPEC = pl.BlockSpec(memory_space=pltpu.HBM)


def _exchange_shapes(srcs, broadcast):
    return [jax.ShapeDtypeStruct((N_DEV,) + (s.shape if broadcast else s.shape[1:]), s.dtype) for s in srcs]


def _exchange_semaphores(n):
    return [pltpu.SemaphoreType.DMA((N_DEV - 1, n)), pltpu.SemaphoreType.DMA((N_DEV - 1, n)), pltpu.SemaphoreType.DMA((n,))]


def _exchange_copies(src_refs, dst_refs, send_sems, recv_sems, local_sems, broadcast):
    x, y, c = lax.axis_index("x"), lax.axis_index("y"), lax.axis_index("c")
    me = 4 * x + 2 * y + c
    copies = []
    for k in range(1, N_DEV):
        px = 1 - x if (k >> 2) & 1 else x
        py = 1 - y if (k >> 1) & 1 else y
        pc = 1 - c if k & 1 else c
        peer = 4 * px + 2 * py + pc
        for a, (src, dst) in enumerate(zip(src_refs, dst_refs)):
            copies.append(pltpu.make_async_remote_copy(
                src_ref=src if broadcast else src.at[peer], dst_ref=dst.at[me],
                send_sem=send_sems.at[k - 1, a], recv_sem=recv_sems.at[k - 1, a],
                device_id=(px, py, pc), device_id_type=MESH))
    for a, (src, dst) in enumerate(zip(src_refs, dst_refs)):
        copies.append(pltpu.make_async_copy(src if broadcast else src.at[me], dst.at[me], local_sems.at[a]))
    return copies


def _call_with_exchange(body, *, name, grid, in_specs, out_specs, out_shape, scratch_shapes, args, exchange):
    if exchange is None:
        outs = pl.pallas_call(body, name=name, grid=grid, in_specs=in_specs, out_specs=out_specs, out_shape=out_shape,
                              scratch_shapes=scratch_shapes,
                              compiler_params=_cparams(*["arbitrary"] * len(grid)))(*args)
        return outs, None
    srcs, broadcast = exchange
    n, n_in, n_out, n_scr = len(srcs), len(args), len(out_shape), len(scratch_shapes)
    steps = 1
    for g in grid:
        steps *= g

    def hosted(*refs):
        ins, src_refs = refs[:n_in], refs[n_in:n_in + n]
        outs, dst_refs = refs[n_in + n:n_in + n + n_out], refs[n_in + n + n_out:n_in + 2 * n + n_out]
        scratch = refs[n_in + 2 * n + n_out:]
        step = pl.program_id(0)
        for axis in range(1, len(grid)):
            step = step * grid[axis] + pl.program_id(axis)

        @pl.when(step == 0)
        def _():
            for cp in _exchange_copies(src_refs, dst_refs, *scratch[n_scr:], broadcast):
                cp.start()

        body(*ins, *outs, *scratch[:n_scr])

        @pl.when(step == steps - 1)
        def _():
            for cp in _exchange_copies(src_refs, dst_refs, *scratch[n_scr:], broadcast):
                cp.wait()

    outs = pl.pallas_call(
        hosted, name=name, grid=grid, in_specs=list(in_specs) + [HBM_SPEC] * n, out_specs=list(out_specs) + [HBM_SPEC] * n,
        out_shape=list(out_shape) + _exchange_shapes(srcs, broadcast),
        scratch_shapes=list(scratch_shapes) + _exchange_semaphores(n),
        compiler_params=_cparams(*["arbitrary"] * len(grid)))(*args, *srcs)
    return outs[:n_out], outs[n_out:]


def _regroup_w_in(w):
    wa = w[:, OFF_A:OFF_A + NA]
    seg = lambda off, h, n=HD: w[:, off + h * n: off + (h + 1) * n]
    wb = jnp.concatenate([seg(OFF_B + s * 512, h) for h in range(HGRN_HEADS) for s in range(4)], axis=1)
    parts = []
    for h in range(GDN_QK_HEADS):
        small = jnp.concatenate(
            [w[:, OFF_BETA + 2 * h: OFF_BETA + 2 * h + 2], w[:, OFF_CA + 2 * h: OFF_CA + 2 * h + 2],
             jnp.zeros((w.shape[0], HD - 4), w.dtype)], axis=1)
        parts += [seg(OFF_CQ, h), seg(OFF_CK, h), seg(OFF_CV, h, 2 * HD), seg(OFF_CZ, h, 2 * HD), small]
    wc = jnp.concatenate(parts, axis=1)
    wg = w[:, OFF_G:OFF_G + NG]
    return wa, wb, wc, wg


def _ungroup_dw_in(da, db, dc, dg):
    bq = [jnp.concatenate([db[:, h * 512 + s * HD: h * 512 + (s + 1) * HD] for h in range(HGRN_HEADS)], axis=1)
          for s in range(4)]
    ch = lambda h, lo, hi: dc[:, h * C_HEAD + lo: h * C_HEAD + hi]
    heads = range(GDN_QK_HEADS)
    cq = jnp.concatenate([ch(h, 0, HD) for h in heads], axis=1)
    ck = jnp.concatenate([ch(h, HD, 2 * HD) for h in heads], axis=1)
    cv = jnp.concatenate([ch(h, 2 * HD, 4 * HD) for h in heads], axis=1)
    cz = jnp.concatenate([ch(h, 4 * HD, 6 * HD) for h in heads], axis=1)
    cbeta = jnp.concatenate([ch(h, 6 * HD, 6 * HD + 2) for h in heads], axis=1)
    ca = jnp.concatenate([ch(h, 6 * HD + 2, 6 * HD + 4) for h in heads], axis=1)
    return jnp.concatenate([da] + bq + [cq, ck, cv, cbeta, ca, cz, dg], axis=1)


def _regroup_conv_c(cw):
    parts = []
    for h in range(GDN_QK_HEADS):
        parts += [cw[:, h * HD:(h + 1) * HD], cw[:, 512 + h * HD: 512 + (h + 1) * HD],
                  cw[:, 1024 + 2 * h * HD: 1024 + (2 * h + 2) * HD]]
    return jnp.concatenate(parts, axis=1)


def _ungroup_conv_c(d):
    heads = range(GDN_QK_HEADS)
    q = jnp.concatenate([d[:, h * 512: h * 512 + HD] for h in heads], axis=1)
    k = jnp.concatenate([d[:, h * 512 + HD: h * 512 + 2 * HD] for h in heads], axis=1)
    v = jnp.concatenate([d[:, h * 512 + 2 * HD: h * 512 + 4 * HD] for h in heads], axis=1)
    return jnp.concatenate([q, k, v], axis=1)


def _numel(shape):
    n = 1
    for d in shape:
        n *= d
    return n


def _pack(arrays, rows):
    flat = jnp.concatenate([a.reshape(-1) for a in arrays])
    return jnp.pad(flat, (0, rows * 128 - flat.shape[0])).reshape(rows, 128)


def _unpack(packed, shapes):
    flat = packed.reshape(-1)
    out, off = [], 0
    for s in shapes:
        out.append(flat[off:off + _numel(s)].reshape(s))
        off += _numel(s)
    return out


def _rows_for(shapes):
    return -(-sum(_numel(s) for s in shapes) // 1024) * 8


def kernel(x, norm_w, w_in, b_gate, conv_a, conv_c, a_log, dt_bias, lower_bounds, hgrn_norm_w, gdn_norm_w, w_out_a, w_out_b, w_out_c, w_o, final_norm_w, loss_target, m_norm_w, m_w_in, m_b_gate, m_conv_a, m_conv_c, m_a_log, m_dt_bias, m_lower_bounds, m_hgrn_norm_w, m_gdn_norm_w, m_w_out_a, m_w_out_b, m_w_out_c, m_w_o, m_final_norm_w, v_norm_w, v_w_in, v_b_gate, v_conv_a, v_conv_c, v_a_log, v_dt_bias, v_lower_bounds, v_hgrn_norm_w, v_gdn_norm_w, v_w_out_a, v_w_out_b, v_w_out_c, v_w_o, v_final_norm_w):
    me = 4 * lax.axis_index("x") + 2 * lax.axis_index("y") + lax.axis_index("c")
    xs = x[0]
    target = loss_target[0]
    in_shard = w_in.shape[2]

    big = [w_in, w_out_a, w_out_b, w_out_c, w_o]
    shards_of = lambda l: [w[l].astype(BF16) for w in big]
    conv_shapes = [(DEPTH, 3, CONV_W), (DEPTH, 4, 2048)]
    conv_rows = _rows_for(conv_shapes)
    ca_full = lax.dynamic_update_slice(jnp.zeros(conv_shapes[0], F32), conv_a, (0, 0, me * conv_a.shape[2]))
    cc_full = lax.dynamic_update_slice(jnp.zeros(conv_shapes[1], F32), conv_c, (0, 0, me * conv_c.shape[2]))
    conv_parts, = _exchange([_pack([ca_full, cc_full], conv_rows)], "gather_conv", broadcast=True)
    conv_a_full, conv_c_full = _unpack(_sum_slots(conv_parts, "sum_conv"), conv_shapes)

    lb_pad = jnp.pad(lower_bounds, ((0, 8 - DEPTH), (0, 0)))
    lbs = _lower_bounds_fwd(lb_pad, "lower_bounds_fwd")

    def layer_weights(l, gathered):
        g_in, g_oa, g_ob, g_oc, g_o = gathered
        wa, wb, wc, wg = _regroup_w_in(jnp.concatenate([g_in[q] for q in range(N_DEV)], axis=1))
        lanes = lambda vec: jnp.pad(vec.reshape(GDN_QK_HEADS, 1, 2), ((0, 0), (0, 0), (0, HD - 2)))
        cpar = jnp.concatenate([lanes(a_log[l]), lanes(dt_bias[l]), jnp.zeros((GDN_QK_HEADS, 6, HD), F32)], axis=1)
        return dict(
            wa=wa, wb=wb, wc=wc, wg=wg, cpar=cpar,
            woa=jnp.concatenate([g_oa[q] for q in range(N_DEV)], axis=1),
            wob=jnp.concatenate([g_ob[q] for q in range(N_DEV)], axis=1), woc=g_oc.reshape(D, D), wo=g_o.reshape(D, D),
            nw=norm_w[l:l + 1], bg=b_gate[l:l + 1], cwa=conv_a_full[l], cwc=_regroup_conv_c(conv_c_full[l]),
            lb=lbs[l:l + 1], hnw=hgrn_norm_w[l:l + 1], gnw=gdn_norm_w[l:l + 1])

    layers = [layer_weights(0, _exchange(shards_of(0), "gather_l0", broadcast=True))]

    saved = []
    cur = xs
    for l in range(DEPTH):
        L = layers[l]
        n = f"l{l}_"
        h = _rmsnorm_fwd(cur, L["nw"], n + "rms")
        pa = _matmul(h, L["wa"], "nn", n + "proj_a")
        pb = _matmul(h, L["wb"], "nn", n + "proj_b")
        pc = _matmul(h, L["wc"], "nn", n + "proj_c")
        pg = _matmul(h, L["wg"], "nn", n + "proj_g")
        ua = _branch_a_fwd(pa, L["cwa"], n + "conv_fwd")
        ub, sb = _branch_b_fwd(pb, L["lb"], L["hnw"], n + "hgrn_fwd")
        carry = (shards_of(l + 1), True) if l + 1 < DEPTH else None
        (uc, sc, xc), gathered = _branch_c_fwd(pc, L["cwc"], L["cpar"], L["gnw"], n + "gdn_fwd", exchange=carry)
        if carry is not None:
            layers.append(layer_weights(l + 1, gathered))
        ya =_matmul(ua, L["woa"], "nn", n + "out_a")
        yb = _matmul(ub, L["wob"], "nn", n + "out_b")
        yc = _matmul(uc, L["woc"], "nn", n + "out_c")
        merged = _merge_fwd(pg, L["bg"], ya, yb, yc, n + "merge")
        nxt = _matmul(merged, L["wo"], "nn", n + "out_o", residual=cur)
        saved.append(dict(x=cur, h=h, pa=pa, pb=pb, pc=pc, pg=pg, ua=ua, ub=ub, uc=uc, sb=sb, sc=sc, xc=xc,
                          ya=ya, yb=yb, yc=yc, merged=merged))
        cur = nxt

    loss_part, dx, d_final = _loss_head(cur, final_norm_w.reshape(1, D), target, "loss_head")

    def outgoing(g):
        cols = lambda a, n: jnp.stack([a[:, p * n:(p + 1) * n] for p in range(N_DEV)]).astype(BF16)
        rows = lambda a: a.reshape(N_DEV, a.shape[0] // N_DEV, a.shape[1]).astype(BF16)
        return [cols(g["w_in"], in_shard), cols(g["w_out_a"], 128), cols(g["w_out_b"], 128), rows(g["w_out_c"]), rows(g["w_o"])]

    grads = [None] * DEPTH
    dlbs_rows = [None] * DEPTH
    incoming = [None] * DEPTH
    for l in reversed(range(DEPTH)):
        L, S = layers[l], saved[l]
        n = f"l{l}_"
        dmerged = _matmul(dx, L["wo"], "nt", n + "d_merged")
        d_wo = _matmul(S["merged"], dx, "tn", n + "dw_o")
        dpg, dya, dyb, dyc, d_bg = _merge_bwd(dmerged, S["pg"], L["bg"], S["ya"], S["yb"], S["yc"], n + "merge_bwd")
        dua = _matmul(dya, L["woa"], "nt", n + "d_ua")
        dub = _matmul(dyb, L["wob"], "nt", n + "d_ub")
        duc = _matmul(dyc, L["woc"], "nt", n + "d_uc")
        d_woa = _matmul(S["ua"], dya, "tn", n + "dw_out_a")
        d_wob = _matmul(S["ub"], dyb, "tn", n + "dw_out_b")
        d_woc = _matmul(S["uc"], dyc, "tn", n + "dw_out_c")
        dpa, d_cwa = _branch_a_bwd(S["pa"], L["cwa"], dua, n + "conv_bwd")
        dpb, d_lb, d_hnw = _branch_b_bwd(S["pb"], S["sb"], L["lb"], L["hnw"], dub, n + "hgrn_bwd")
        carry = (outgoing(grads[l + 1]), False) if l + 1 < DEPTH else None
        (dpc, d_cwc, d_cpar, d_gnw), arrived = _branch_c_bwd(S["pc"], S["sc"], S["xc"], L["cwc"], L["cpar"], L["gnw"], duc,
                                                             n + "gdn_bwd", exchange=carry)
        if carry is not None:
            incoming[l + 1] = arrived
        dh = _matmul(dpa, L["wa"], "nt", n + "dh_a")
        dh = _matmul(dpb, L["wb"], "nt", n + "dh_b", residual=dh)
        dh = _matmul(dpc, L["wc"], "nt", n + "dh_c", residual=dh)
        dh = _matmul(dpg, L["wg"], "nt", n + "dh_g", residual=dh)
        d_win = _ungroup_dw_in(_matmul(S["h"], dpa, "tn", n + "dw_a"), _matmul(S["h"], dpb, "tn", n + "dw_b"),
                               _matmul(S["h"], dpc, "tn", n + "dw_c"), _matmul(S["h"], dpg, "tn", n + "dw_g"))
        dx, d_nw = _rmsnorm_bwd(dh, S["x"], L["nw"], dx, n + "rms_bwd")
        dlbs_rows[l] = d_lb
        grads[l] = dict(w_in=d_win, w_out_a=d_woa, w_out_b=d_wob, w_out_c=d_woc, w_o=d_wo, norm_w=d_nw[0],
                        b_gate=d_bg[0], conv_a=d_cwa, conv_c=_ungroup_conv_c(d_cwc),
                        a_log=d_cpar[:, 0, 0:2].reshape(-1), dt_bias=d_cpar[:, 1, 0:2].reshape(-1),
                        hgrn_norm_w=d_hnw[0], gdn_norm_w=d_gnw[0])
    grad_x = dx[None]
    d_lower = _lower_bounds_bwd(lb_pad, jnp.pad(jnp.concatenate(dlbs_rows, axis=0), ((0, 8 - DEPTH), (0, 0))),
                                "lower_bounds_bwd")[:DEPTH]

    incoming[0] = _exchange(outgoing(grads[0]), "exchange_grads_l0", broadcast=False)
    stack = lambda name: jnp.stack([grads[l][name] for l in range(DEPTH)])
    big_out = {}
    for j, (name, w, m, v) in enumerate((("w_in", w_in, m_w_in, v_w_in), ("w_out_a", w_out_a, m_w_out_a, v_w_out_a),
                                         ("w_out_b", w_out_b, m_w_out_b, v_w_out_b), ("w_out_c", w_out_c, m_w_out_c, v_w_out_c),
                                         ("w_o", w_o, m_w_o, v_w_o))):
        parts = jnp.concatenate([incoming[l][j] for l in range(DEPTH)], axis=1)
        r2 = lambda a: a.reshape(parts.shape[1], parts.shape[2])
        outs = _sum_adamw(parts, r2(w), r2(m), r2(v), "adamw_" + name)
        big_out[name] = [o.reshape(w.shape) for o in outs]

    small_names = ["norm_w", "b_gate", "conv_a", "conv_c", "a_log", "dt_bias", "lower_bounds", "hgrn_norm_w",
                   "gdn_norm_w", "final_norm_w", "loss"]
    small_vals = {k: stack(k) for k in ("norm_w", "b_gate", "conv_a", "conv_c", "a_log", "dt_bias", "hgrn_norm_w", "gdn_norm_w")}
    small_vals.update(lower_bounds=d_lower, final_norm_w=d_final[0], loss=loss_part.reshape(1))
    small_shapes = [small_vals[k].shape for k in small_names]
    small_rows = _rows_for(small_shapes)
    small_parts, = _exchange([_pack([small_vals[k] for k in small_names], small_rows)], "exchange_small", broadcast=True)
    total = dict(zip(small_names, _unpack(_sum_slots(small_parts, "sum_small"), small_shapes)))
    loss = total["loss"][0]
    g_conv_a = lax.dynamic_slice(total["conv_a"], (0, 0, me * conv_a.shape[2]), conv_a.shape)
    g_conv_c = lax.dynamic_slice(total["conv_c"], (0, 0, me * conv_c.shape[2]), conv_c.shape)

    small_w = dict(norm_w=(norm_w, m_norm_w, v_norm_w), b_gate=(b_gate, m_b_gate, v_b_gate),
                   conv_a=(conv_a, m_conv_a, v_conv_a), conv_c=(conv_c, m_conv_c, v_conv_c),
                   a_log=(a_log, m_a_log, v_a_log), dt_bias=(dt_bias, m_dt_bias, v_dt_bias),
                   lower_bounds=(lower_bounds, m_lower_bounds, v_lower_bounds),
                   hgrn_norm_w=(hgrn_norm_w, m_hgrn_norm_w, v_hgrn_norm_w), gdn_norm_w=(gdn_norm_w, m_gdn_norm_w, v_gdn_norm_w),
                   final_norm_w=(final_norm_w, m_final_norm_w, v_final_norm_w))
    small_g = dict(total, conv_a=g_conv_a, conv_c=g_conv_c)
    upd_names = small_names[:-1]
    upd_shapes = [small_w[k][0].shape for k in upd_names]
    upd_rows = _rows_for(upd_shapes)
    pk = lambda j: _pack([small_w[k][j] for k in upd_names], upd_rows)
    s_delta, s_m, s_v = _adamw(_pack([small_g[k] for k in upd_names], upd_rows), pk(0), pk(1), pk(2), "adamw_small")
    small_out = {k: [small_g[k], d, mm, vv] for k, d, mm, vv in
                 zip(upd_names, _unpack(s_delta, upd_shapes), _unpack(s_m, upd_shapes), _unpack(s_v, upd_shapes))}

    order = ["norm_w", "w_in", "b_gate", "conv_a", "conv_c", "a_log", "dt_bias", "lower_bounds", "hgrn_norm_w",
             "gdn_norm_w", "w_out_a", "w_out_b", "w_out_c", "w_o", "final_norm_w"]
    res = {**small_out, **big_out}
    outs = [loss, grad_x]
    for j in range(4):
        outs += [res[k][j] for k in order]
    return tuple(outs)
```

```python
import functools

import jax
import jax.numpy as jnp
from jax import lax
from jax.experimental import pallas as pl
from jax.experimental.pallas import tpu as pltpu

F32 = jnp.float32
BF16 = jnp.bfloat16
MESH = pl.DeviceIdType.MESH

N_DEV = 8
D = 1024
DEPTH = 2
CHUNK = 64
HGRN_BLOCK_CHUNKS = 8
GDN_BLOCK_CHUNKS = 8
GROUP = 128
NORM_EPS = 1e-6
L2_EPS = 1e-6
MIN_F = 1e-30
HD = 128
HGRN_HEADS = 4
GDN_QK_HEADS = 4
CONV_W = 512
IN_COLS = 10256
OFF_A, OFF_B, OFF_CQ, OFF_CK, OFF_CV, OFF_BETA, OFF_CA, OFF_CZ, OFF_G = (
    0, 2048, 4096, 4608, 5120, 6144, 6152, 6160, 7184)
NA, NB, NC_COLS, NG = 2048, 2048, 3584, 3072
C_HEAD = 896

ADAM_LR, ADAM_B1, ADAM_B2, ADAM_EPS, ADAM_WD, ADAM_STEP = 0.001, 0.9, 0.999, 1e-08, 0.01, 10

VMEM_LIMIT = 56 * 1024 * 1024
MM_TILE = 1024


def _cparams(*sem):
    return pltpu.CompilerParams(dimension_semantics=sem, vmem_limit_bytes=VMEM_LIMIT)


def _tile(dim, cap):
    if dim <= cap:
        return dim
    t = (cap // 128) * 128
    while dim % t:
        t -= 128
    return t


def _sigmoid(x):
    return 1.0 / (1.0 + jnp.exp(-x))


def _silu(x):
    return x * _sigmoid(x)


def _softplus(x):
    return jnp.maximum(x, 0.0) + jnp.log(1.0 + jnp.exp(-jnp.abs(x)))


def _dot(a, b, dims, precision=None):
    if precision is None:
        a, b = a.astype(BF16), b.astype(BF16)
    return lax.dot_general(a, b, (dims, ((), ())), precision=precision, preferred_element_type=F32)


def _nn(a, b, precision=None):
    return _dot(a, b, ((1,), (0,)), precision)


def _nt(a, b, precision=None):
    return _dot(a, b, ((1,), (1,)), precision)


def _tn(a, b, precision=None):
    return _dot(a, b, ((0,), (0,)), precision)


def _sum_rows_split(mat01, x):
    m = mat01.astype(BF16)
    hi = x.astype(BF16)
    low = (x - hi.astype(F32)).astype(BF16)
    return _nn(m, hi) + _nn(m, low)


@functools.partial(jax.custom_vjp, nondiff_argnums=(1,))
def _shift_rows(x, d):
    return x if d == 0 else pltpu.roll(x, d, 0)


def _shift_rows_fwd(x, d):
    return _shift_rows(x, d), None


def _shift_rows_bwd(d, _, ct):
    return ((ct if d == 0 else pltpu.roll(ct, ct.shape[0] - d, 0)),)


_shift_rows.defvjp(_shift_rows_fwd, _shift_rows_bwd)


def _iota2(shape):
    return lax.broadcasted_iota(jnp.int32, shape, 0), lax.broadcasted_iota(jnp.int32, shape, 1)


def _lane_pick(x, i):
    lane = lax.broadcasted_iota(jnp.int32, x.shape, 1)
    return jnp.sum(jnp.where(lane == i, x, 0.0), axis=1, keepdims=True)


def _hgrn_block(qr, fr, ir, zr, st0, lb, nw):
    rows = qr.shape[0]
    r, c = _iota2((CHUNK, CHUNK))
    halves = [1 << j for j in range(CHUNK.bit_length() - 1)]
    mats = [c <= r, c > r]
    pairs = []
    for hb in halves:
        same = (r // hb) == (c // hb)
        if hb > 1:
            mats += [(c <= r) & same, (c > r) & same]
        pairs.append(((r // (2 * hb)) == (c // (2 * hb))) & ((r // hb) == (c // hb) + 1))
    stack = jnp.concatenate([m.astype(F32) for m in mats], axis=0)

    q = _silu(qr) * (HD ** -0.5)
    fg = lb + (1.0 - lb) * _sigmoid(fr)
    logf = jnp.log(jnp.maximum(fg, MIN_F))
    kk = 1.0 - fg
    v = ir

    chunks = [slice(s, s + CHUNK) for s in range(0, rows, CHUNK)]
    cums = [_sum_rows_split(stack, logf[sl]) for sl in chunks]
    part = lambda i: jnp.concatenate([cs[i * CHUNK:(i + 1) * CHUNK] for cs in cums], axis=0)
    qg = q * jnp.exp(part(0))
    ks = kk * jnp.exp(part(1))
    q_lv = [q * jnp.exp(logf)] + [q * jnp.exp(part(2 * j)) for j in range(1, len(halves))]
    k_lv = [kk] + [kk * jnp.exp(part(2 * j + 1)) for j in range(1, len(halves))]
    st = st0
    outs = []
    for sl in chunks:
        scores = jnp.where(pairs[0], _nt(q_lv[0][sl], k_lv[0][sl]), 0.0)
        for j in range(1, len(halves)):
            scores += jnp.where(pairs[j], _nt(q_lv[j][sl], k_lv[j][sl]), 0.0)
        outs.append(_nn(scores, v[sl]) + _nt(qg[sl], st))
        st = st * jnp.exp(jnp.sum(logf[sl], axis=0, keepdims=True)) + _tn(v[sl], ks[sl])
    o = jnp.concatenate(outs, axis=0) + jnp.sum(q * kk, axis=1, keepdims=True) * v
    y = o * lax.rsqrt(jnp.mean(o * o, axis=1, keepdims=True) + NORM_EPS) * nw * _silu(zr)
    return y, st


def _unit_lower_inverses(ms):
    r, c = _iota2(ms[0].shape)
    xs = [jnp.where(r == c, 1.0, 0.0) - jnp.where((r // 2) == (c // 2), m, 0.0) for m in ms]
    b = 2
    while b < CHUNK:
        pick = ((r // (2 * b)) == (c // (2 * b))) & ((r // b) != (c // b))
        ts = [_nn(x, jnp.where(pick, m, 0.0)) for x, m in zip(xs, ms)]
        xs = [x - _nn(t, x) for x, t in zip(xs, ts)]
        b *= 2
    return tuple(x.astype(BF16) for x in xs)


@jax.custom_vjp
def _known_inverses(ms, xs):
    return xs


def _known_inverses_fwd(ms, xs):
    return xs, xs


def _known_inverses_bwd(xs, cts):
    r, c = _iota2(xs[0].shape)
    keep = (c < r) & ((r // CHUNK) == (c // CHUNK))
    ts = [_tn(x, ct) for x, ct in zip(xs, cts)]
    return (tuple(jnp.where(keep, -_nt(t, x), 0.0) for t, x in zip(ts, xs)), tuple(jnp.zeros_like(x) for x in xs))


_known_inverses.defvjp(_known_inverses_fwd, _known_inverses_bwd)


def _chunk_cumsum(x):
    row = lax.broadcasted_iota(jnp.int32, x.shape, 0) % CHUNK
    d = 1
    while d < CHUNK:
        x = x + jnp.where(row >= d, _shift_rows(x, d), 0.0)
        d *= 2
    return x


def _gdn_block(x_ext, z, ba, s0a, s0b, w0, w1, w2, w3, alog, dtb, nw, known=None):
    rows = z.shape[0]
    conv = (w0 * _shift_rows(x_ext, 3) + w1 * _shift_rows(x_ext, 2) + w2 * _shift_rows(x_ext, 1) + w3 * x_ext)
    cc = _silu(conv[8:])
    qc, kc = cc[:, 0:HD], cc[:, HD:2 * HD]
    q = qc * lax.rsqrt(jnp.sum(qc * qc, axis=1, keepdims=True) + L2_EPS) * (HD ** -0.5)
    k = kc * lax.rsqrt(jnp.sum(kc * kc, axis=1, keepdims=True) + L2_EPS)

    r, c = _iota2((GROUP, GROUP))
    same = (r // CHUNK) == (c // CHUNK)
    causal, strict, eye = same & (c <= r), same & (c < r), r == c
    heads = (0, 1)
    groups = [slice(lo, lo + GROUP) for lo in range(0, rows, GROUP)]
    chunks = [slice(lo, lo + CHUNK) for lo in range(0, rows, CHUNK)]

    v, loga, g_w, kb, kg, qg = [], [], [], [], [], []
    for i in heads:
        v.append(cc[:, (2 + i) * HD:(3 + i) * HD])
        beta = _sigmoid(_lane_pick(ba, i))
        a_neg = -jnp.exp(_lane_pick(alog, i))
        loga.append(a_neg * _softplus(_lane_pick(ba, 2 + i) + _lane_pick(dtb, i)))
        g_w.append(_chunk_cumsum(jnp.broadcast_to(loga[i], (rows, HD))))
        kb.append(k * beta)
        kg.append(k * jnp.exp(g_w[i]))
        qg.append(q * jnp.exp(g_w[i]))

    systems = [(i, gs) for gs in groups for i in heads]
    dec_c, ms = [], []
    for i, gs in systems:
        g_sq = g_w[i][gs]
        g_row = jnp.sum(jnp.where(eye, g_sq, 0.0), axis=0, keepdims=True)
        diff = g_sq - g_row
        dec_c.append(jnp.where(causal, jnp.exp(jnp.where(causal, diff, 0.0)), 0.0))
        ms.append(jnp.where(strict, _nt(k[gs], kb[i][gs]) * dec_c[-1], 0.0))
    xs = _unit_lower_inverses(tuple(ms)) if known is None else _known_inverses(tuple(ms), known)
    u = [[None] * len(groups) for _ in heads]
    w = [[None] * len(groups) for _ in heads]
    qk = [[None] * len(groups) for _ in heads]
    for n, (i, gs) in enumerate(systems):
        j = n // len(heads)
        u[i][j] = _nn(xs[n], v[i][gs])
        w[i][j] = _nn(xs[n], kg[i][gs])
        qk[i][j] = _nt(q[gs], kb[i][gs]) * dec_c[n]
    u = [jnp.concatenate(p, axis=0) for p in u]
    w = [jnp.concatenate(p, axis=0) for p in w]

    decay, p_mat, q_mat = {}, {}, {}
    for n, sl in enumerate(chunks):
        for i in heads:
            g_last = jnp.sum(loga[i][sl], axis=0, keepdims=True)
            kd = kb[i][sl] * jnp.exp(g_last - g_w[i][sl])
            decay[n, i] = jnp.exp(g_last)
            p_mat[n, i] = -_tn(kd, w[i][sl])
            q_mat[n, i] = _tn(kd, u[i][sl])
    s = [s0a, s0b]
    s_at = {}
    for n in range(len(chunks)):
        for i in heads:
            s_at[n, i] = s[i]
            s[i] = s[i] * decay[n, i] + _nn(p_mat[n, i], s[i]) + q_mat[n, i]

    ys = []
    for i in heads:
        e = jnp.concatenate([u[i][sl] - _nn(w[i][sl], s_at[n, i]) for n, sl in enumerate(chunks)], axis=0)
        o_state = jnp.concatenate([_nn(qg[i][sl], s_at[n, i]) for n, sl in enumerate(chunks)], axis=0)
        o = o_state + jnp.concatenate([_nn(qk[i][j], e[gs]) for j, gs in enumerate(groups)], axis=0)
        zi = z[:, i * HD:(i + 1) * HD]
        ys.append(o * lax.rsqrt(jnp.mean(o * o, axis=1, keepdims=True) + NORM_EPS) * nw * _silu(zi))
    return (jnp.concatenate(ys, axis=1), s[0], s[1]), xs


def _add_to_tail(x, tail):
    return x + jnp.concatenate([jnp.zeros((x.shape[0] - 8, x.shape[1]), x.dtype), tail], axis=0)


def _conv_a_block(ab, ac_ext, ax_ext, az, w0, w1, w2):
    u = ac_ext * ax_ext
    conv = (w0 * _shift_rows(u, 2) + w1 * _shift_rows(u, 1) + w2 * u)[8:]
    return ab * conv * _silu(az)


def _matmul(a, b, mode, name, residual=None, out_dtype=F32):
    if mode == "nn":
        (m, k), n = a.shape, b.shape[1]
    elif mode == "nt":
        (m, k), n = a.shape, b.shape[0]
    else:
        (k, m), n = a.shape, b.shape[1]
    tm, tn, tk = _tile(m, MM_TILE), _tile(n, MM_TILE), _tile(k, MM_TILE)
    nk = k // tk
    dims = {"nn": ((1,), (0,)), "nt": ((1,), (1,)), "tn": ((0,), (0,))}[mode]
    a_spec = pl.BlockSpec((tk, tm), lambda i, j, s: (s, i)) if mode == "tn" else pl.BlockSpec((tm, tk), lambda i, j, s: (i, s))
    b_spec = pl.BlockSpec((tn, tk), lambda i, j, s: (j, s)) if mode == "nt" else pl.BlockSpec((tk, tn), lambda i, j, s: (s, j))
    o_spec = pl.BlockSpec((tm, tn), lambda i, j, s: (i, j))
    has_res = residual is not None

    def finish(out, r_ref, o_ref):
        if has_res:
            out = out + r_ref[...]
        o_ref[...] = out.astype(out_dtype)

    def body_one_pass(*refs):
        finish(_dot(refs[0][...], refs[1][...], dims), refs[2] if has_res else None, refs[-1])

    def body_reduce(*refs):
        a_ref, b_ref = refs[0], refs[1]
        r_ref = refs[2] if has_res else None
        o_ref, acc_ref = refs[-2], refs[-1]
        s = pl.program_id(2)

        @pl.when(s == 0)
        def _():
            acc_ref[...] = jnp.zeros_like(acc_ref)

        acc_ref[...] += _dot(a_ref[...], b_ref[...], dims)

        @pl.when(s == nk - 1)
        def _():
            finish(acc_ref[...], r_ref, o_ref)

    args, specs = [a, b], [a_spec, b_spec]
    if has_res:
        args.append(residual)
        specs.append(o_spec)
    return pl.pallas_call(
        body_one_pass if nk == 1 else body_reduce, name=name, grid=(m // tm, n // tn, nk), in_specs=specs, out_specs=o_spec,
        out_shape=jax.ShapeDtypeStruct((m, n), out_dtype),
        scratch_shapes=[] if nk == 1 else [pltpu.VMEM((tm, tn), F32)],
        compiler_params=_cparams("parallel", "parallel", "arbitrary"))(*args)


def _rmsnorm_fwd(x, w, name):
    t = x.shape[0]
    blk = _tile(t, 512)

    def body(x_ref, w_ref, h_ref):
        xv = x_ref[...]
        h_ref[...] = (xv * lax.rsqrt(jnp.mean(xv * xv, axis=1, keepdims=True) + NORM_EPS) * w_ref[...]).astype(BF16)

    return pl.pallas_call(
        body, name=name, grid=(t // blk,),
        in_specs=[pl.BlockSpec((blk, D), lambda i: (i, 0)), pl.BlockSpec((1, D), lambda i: (0, 0))],
        out_specs=pl.BlockSpec((blk, D), lambda i: (i, 0)), out_shape=jax.ShapeDtypeStruct((t, D), BF16),
        compiler_params=_cparams("parallel"))(x, w)


def _rmsnorm_bwd(dh, x, w, dxo, name):
    t = x.shape[0]
    blk = _tile(t, 512)

    def body(dh_ref, x_ref, w_ref, dxo_ref, dx_ref, dw_ref):
        @pl.when(pl.program_id(0) == 0)
        def _():
            dw_ref[...] = jnp.zeros_like(dw_ref)

        xv, dhv = x_ref[...], dh_ref[...]
        rs = lax.rsqrt(jnp.mean(xv * xv, axis=1, keepdims=True) + NORM_EPS)
        xh = xv * rs
        dw_ref[...] += jnp.sum(dhv * xh, axis=0, keepdims=True)
        dxh = dhv * w_ref[...]
        dx_ref[...] = rs * (dxh - xh * jnp.mean(dxh * xh, axis=1, keepdims=True)) + dxo_ref[...]

    row = pl.BlockSpec((blk, D), lambda i: (i, 0))
    vec = pl.BlockSpec((1, D), lambda i: (0, 0))
    return pl.pallas_call(
        body, name=name, grid=(t // blk,), in_specs=[row, row, vec, row], out_specs=[row, vec],
        out_shape=[jax.ShapeDtypeStruct((t, D), F32), jax.ShapeDtypeStruct((1, D), F32)],
        compiler_params=_cparams("arbitrary"))(dh, x, w, dxo)


def _loss_head(x, w, target, name):
    t = x.shape[0]
    blk = _tile(t, 512)

    def body(x_ref, w_ref, t_ref, loss_ref, dx_ref, dw_ref):
        @pl.when(pl.program_id(0) == 0)
        def _():
            dw_ref[...] = jnp.zeros_like(dw_ref)
            loss_ref[...] = jnp.zeros_like(loss_ref)

        xv = x_ref[...]
        rs = lax.rsqrt(jnp.mean(xv * xv, axis=1, keepdims=True) + NORM_EPS)
        xh = xv * rs
        err = xh * w_ref[...] - t_ref[...]
        loss_ref[...] += 0.5 * jnp.sum(jnp.mean(err * err, axis=1, keepdims=True), axis=0, keepdims=True)
        dy = err * (1.0 / D)
        dw_ref[...] += jnp.sum(dy * xh, axis=0, keepdims=True)
        dxh = dy * w_ref[...]
        dx_ref[...] = rs * (dxh - xh * jnp.mean(dxh * xh, axis=1, keepdims=True))

    row = pl.BlockSpec((blk, D), lambda i: (i, 0))
    vec = pl.BlockSpec((1, D), lambda i: (0, 0))
    return pl.pallas_call(
        body, name=name, grid=(t // blk,), in_specs=[row, vec, row],
        out_specs=[pl.BlockSpec((1, 1), lambda i: (0, 0)), row, vec],
        out_shape=[jax.ShapeDtypeStruct((1, 1), F32), jax.ShapeDtypeStruct((t, D), F32), jax.ShapeDtypeStruct((1, D), F32)],
        compiler_params=_cparams("arbitrary"))(x, w, target)


def _lbs_of(lb):
    r = lax.broadcasted_iota(jnp.int32, lb.shape, 0)
    real = r < DEPTH
    mx = lax.stop_gradient(jnp.max(jnp.where(real, lb, -jnp.inf), axis=0, keepdims=True))
    e = jnp.where(real, jnp.exp(jnp.where(real, lb - mx, 0.0)), 0.0)
    p = e / jnp.sum(e, axis=0, keepdims=True)
    out = jnp.zeros_like(lb)
    run = jnp.zeros_like(mx)
    for l in range(1, DEPTH):
        run = run + jnp.sum(jnp.where(r == l, p, 0.0), axis=0, keepdims=True)
        out = out + jnp.where(r == l, run, 0.0)
    return out


def _lower_bounds_fwd(lbp, name):
    def body(lb_ref, o_ref):
        o_ref[...] = _lbs_of(lb_ref[...])

    return pl.pallas_call(body, name=name, out_shape=jax.ShapeDtypeStruct(lbp.shape, F32))(lbp)


def _lower_bounds_bwd(lbp, dlbs, name):
    def body(lb_ref, d_ref, o_ref):
        _, vjp = jax.vjp(_lbs_of, lb_ref[...])
        o_ref[...] = vjp(d_ref[...])[0]

    return pl.pallas_call(body, name=name, out_shape=jax.ShapeDtypeStruct(lbp.shape, F32))(lbp, dlbs)


def _branch_a_fwd(pa, cw, name):
    t = pa.shape[0]
    blk = _tile(t, 512)
    W = CONV_W

    def body(p_ref, w_ref, y_ref, hc_ref, hx_ref):
        @pl.when(pl.program_id(0) == 0)
        def _():
            hc_ref[...] = jnp.zeros_like(hc_ref)
            hx_ref[...] = jnp.zeros_like(hx_ref)

        ac, ax = p_ref[:, W:2 * W], p_ref[:, 2 * W:3 * W]
        y_ref[...] = _conv_a_block(
            p_ref[:, 0:W], jnp.concatenate([hc_ref[...], ac], axis=0), jnp.concatenate([hx_ref[...], ax], axis=0),
            p_ref[:, 3 * W:4 * W], w_ref[0:1, :], w_ref[1:2, :], w_ref[2:3, :]).astype(BF16)
        hc_ref[...] = p_ref[blk - 8:blk, W:2 * W]
        hx_ref[...] = p_ref[blk - 8:blk, 2 * W:3 * W]

    return pl.pallas_call(
        body, name=name, grid=(t // blk,),
        in_specs=[pl.BlockSpec((blk, NA), lambda i: (i, 0)), pl.BlockSpec((3, W), lambda i: (0, 0))],
        out_specs=pl.BlockSpec((blk, W), lambda i: (i, 0)), out_shape=jax.ShapeDtypeStruct((t, W), BF16),
        scratch_shapes=[pltpu.VMEM((8, W), F32), pltpu.VMEM((8, W), F32)],
        compiler_params=_cparams("arbitrary"))(pa, cw)


def _branch_a_bwd(pa, cw, dy, name):
    t = pa.shape[0]
    blk = _tile(t, 512)
    nt_ = t // blk
    W = CONV_W
    hb = blk // 8

    def body(p_ref, halo_ref, w_ref, dy_ref, dp_ref, dw_ref, chc_ref, chx_ref):
        i = pl.program_id(0)

        @pl.when(i == 0)
        def _():
            chc_ref[...] = jnp.zeros_like(chc_ref)
            chx_ref[...] = jnp.zeros_like(chx_ref)
            dw_ref[...] = jnp.zeros_like(dw_ref)

        keep = 1.0 - (i == nt_ - 1).astype(F32)
        hc = halo_ref[:, W:2 * W] * keep
        hx = halo_ref[:, 2 * W:3 * W] * keep
        ac_ext = jnp.concatenate([hc, p_ref[:, W:2 * W]], axis=0)
        ax_ext = jnp.concatenate([hx, p_ref[:, 2 * W:3 * W]], axis=0)
        _, vjp = jax.vjp(_conv_a_block, p_ref[:, 0:W], ac_ext, ax_ext, p_ref[:, 3 * W:4 * W],
                         w_ref[0:1, :], w_ref[1:2, :], w_ref[2:3, :])
        dab, dac, dax, daz, dw0, dw1, dw2 = vjp(dy_ref[...])
        dp_ref[:, 0:W] = dab.astype(BF16)
        dp_ref[:, W:2 * W] = _add_to_tail(dac[8:], chc_ref[...]).astype(BF16)
        dp_ref[:, 2 * W:3 * W] = _add_to_tail(dax[8:], chx_ref[...]).astype(BF16)
        dp_ref[:, 3 * W:4 * W] = daz.astype(BF16)
        chc_ref[...] = dac[:8] * keep
        chx_ref[...] = dax[:8] * keep
        dw_ref[0:1, :] += dw0
        dw_ref[1:2, :] += dw1
        dw_ref[2:3, :] += dw2

    rev = lambda i: (nt_ - 1 - i, 0)
    return pl.pallas_call(
        body, name=name, grid=(nt_,),
        in_specs=[pl.BlockSpec((blk, NA), rev),
                  pl.BlockSpec((8, NA), lambda i: (jnp.maximum((nt_ - 1 - i) * hb - 1, 0), 0)),
                  pl.BlockSpec((3, W), lambda i: (0, 0)),
                  pl.BlockSpec((blk, W), rev)],
        out_specs=[pl.BlockSpec((blk, NA), rev), pl.BlockSpec((3, W), lambda i: (0, 0))],
        out_shape=[jax.ShapeDtypeStruct((t, NA), BF16), jax.ShapeDtypeStruct((3, W), F32)],
        scratch_shapes=[pltpu.VMEM((8, W), F32), pltpu.VMEM((8, W), F32)],
        compiler_params=_cparams("arbitrary"))(pa, pa, cw, dy)


def _block_rows(t, chunks):
    return min(t, chunks * CHUNK)


def _branch_b_fwd(pb, lbs_row, nw, name, exchange=None):
    t = pb.shape[0]
    rows = _block_rows(t, HGRN_BLOCK_CHUNKS)
    nch = t // rows

    def body(p_ref, lb_ref, nw_ref, y_ref, s_ref, st_ref):
        @pl.when(pl.program_id(1) == 0)
        def _():
            st_ref[...] = jnp.zeros_like(st_ref)

        s_ref[0, 0] = st_ref[...]
        y, st1 = _hgrn_block(p_ref[:, 0:HD], p_ref[:, HD:2 * HD], p_ref[:, 2 * HD:3 * HD], p_ref[:, 3 * HD:4 * HD],
                             st_ref[...], lb_ref[...], nw_ref[...])
        y_ref[...] = y.astype(BF16)
        st_ref[...] = st1

    return _call_with_exchange(
        body, name=name, grid=(HGRN_HEADS, nch),
        in_specs=[pl.BlockSpec((rows, 4 * HD), lambda h, i: (i, h)),
                  pl.BlockSpec((1, HD), lambda h, i: (0, h)),
                  pl.BlockSpec((1, HD), lambda h, i: (0, 0))],
        out_specs=[pl.BlockSpec((rows, HD), lambda h, i: (i, h)),
                   pl.BlockSpec((1, 1, HD, HD), lambda h, i: (h, i, 0, 0))],
        out_shape=[jax.ShapeDtypeStruct((t, HGRN_HEADS * HD), BF16),
                   jax.ShapeDtypeStruct((HGRN_HEADS, nch, HD, HD), F32)],
        scratch_shapes=[pltpu.VMEM((HD, HD), F32)],
        args=(pb, lbs_row, nw), exchange=exchange)


def _branch_b_bwd(pb, states, lbs_row, nw, dy, name, exchange=None):
    t = pb.shape[0]
    rows = _block_rows(t, HGRN_BLOCK_CHUNKS)
    nch = t // rows

    def body(p_ref, s_ref, lb_ref, nw_ref, dy_ref, dp_ref, dlb_ref, dnw_ref, ds_ref):
        h, i = pl.program_id(0), pl.program_id(1)

        @pl.when(i == 0)
        def _():
            ds_ref[...] = jnp.zeros_like(ds_ref)
            dlb_ref[...] = jnp.zeros_like(dlb_ref)

        @pl.when((i == 0) & (h == 0))
        def _():
            dnw_ref[...] = jnp.zeros_like(dnw_ref)

        _, vjp = jax.vjp(_hgrn_block, p_ref[:, 0:HD], p_ref[:, HD:2 * HD], p_ref[:, 2 * HD:3 * HD],
                         p_ref[:, 3 * HD:4 * HD], s_ref[0, 0], lb_ref[...], nw_ref[...])
        dq, df, di, dz, ds0, dlb, dnw = vjp((dy_ref[...], ds_ref[...]))
        dp_ref[:, 0:HD] = dq.astype(BF16)
        dp_ref[:, HD:2 * HD] = df.astype(BF16)
        dp_ref[:, 2 * HD:3 * HD] = di.astype(BF16)
        dp_ref[:, 3 * HD:4 * HD] = dz.astype(BF16)
        ds_ref[...] = ds0
        dlb_ref[...] += dlb
        dnw_ref[...] += dnw

    rev = lambda h, i: (nch - 1 - i, h)
    return _call_with_exchange(
        body, name=name, grid=(HGRN_HEADS, nch),
        in_specs=[pl.BlockSpec((rows, 4 * HD), rev),
                  pl.BlockSpec((1, 1, HD, HD), lambda h, i: (h, nch - 1 - i, 0, 0)),
                  pl.BlockSpec((1, HD), lambda h, i: (0, h)),
                  pl.BlockSpec((1, HD), lambda h, i: (0, 0)),
                  pl.BlockSpec((rows, HD), rev)],
        out_specs=[pl.BlockSpec((rows, 4 * HD), rev),
                   pl.BlockSpec((1, HD), lambda h, i: (0, h)),
                   pl.BlockSpec((1, HD), lambda h, i: (0, 0))],
        out_shape=[jax.ShapeDtypeStruct((t, NB), BF16), jax.ShapeDtypeStruct((1, HGRN_HEADS * HD), F32),
                   jax.ShapeDtypeStruct((1, HD), F32)],
        scratch_shapes=[pltpu.VMEM((HD, HD), F32)],
        args=(pb, states, lbs_row, nw, dy), exchange=exchange)


def _branch_c_fwd(pc, cw, cpar, nw, name, exchange=None):
    t = pc.shape[0]
    rows = _block_rows(t, GDN_BLOCK_CHUNKS)
    nch = t // rows
    XW = 4 * HD

    nsys = 2 * rows // GROUP

    def body(p_ref, w_ref, cp_ref, nw_ref, y_ref, s_ref, x_ref, sa_ref, sb_ref, halo_ref):
        @pl.when(pl.program_id(1) == 0)
        def _():
            sa_ref[...] = jnp.zeros_like(sa_ref)
            sb_ref[...] = jnp.zeros_like(sb_ref)
            halo_ref[...] = jnp.zeros_like(halo_ref)

        s_ref[0, 0, 0] = sa_ref[...]
        s_ref[0, 0, 1] = sb_ref[...]
        x_ext = jnp.concatenate([halo_ref[...], p_ref[:, 0:XW]], axis=0)
        (y, s1a, s1b), xs = _gdn_block(x_ext, p_ref[:, XW:XW + 2 * HD], p_ref[:, XW + 2 * HD:XW + 3 * HD],
                                       sa_ref[...], sb_ref[...], w_ref[0:1, :], w_ref[1:2, :], w_ref[2:3, :], w_ref[3:4, :],
                                       cp_ref[0, 0:1, :], cp_ref[0, 1:2, :], nw_ref[...])
        for n in range(nsys):
            x_ref[0, 0, n] = xs[n]
        y_ref[...] = y.astype(BF16)
        sa_ref[...] = s1a
        sb_ref[...] = s1b
        halo_ref[...] = p_ref[rows - 8:rows, 0:XW]

    return _call_with_exchange(
        body, name=name, grid=(GDN_QK_HEADS, nch),
        in_specs=[pl.BlockSpec((rows, C_HEAD), lambda h, i: (i, h)),
                  pl.BlockSpec((4, XW), lambda h, i: (0, h)),
                  pl.BlockSpec((1, 8, HD), lambda h, i: (h, 0, 0)),
                  pl.BlockSpec((1, HD), lambda h, i: (0, 0))],
        out_specs=[pl.BlockSpec((rows, 2 * HD), lambda h, i: (i, h)),
                   pl.BlockSpec((1, 1, 2, HD, HD), lambda h, i: (h, i, 0, 0, 0)),
                   pl.BlockSpec((1, 1, nsys, GROUP, GROUP), lambda h, i: (h, i, 0, 0, 0))],
        out_shape=[jax.ShapeDtypeStruct((t, 2 * GDN_QK_HEADS * HD), BF16),
                   jax.ShapeDtypeStruct((GDN_QK_HEADS, nch, 2, HD, HD), F32),
                   jax.ShapeDtypeStruct((GDN_QK_HEADS, nch, nsys, GROUP, GROUP), BF16)],
        scratch_shapes=[pltpu.VMEM((HD, HD), F32), pltpu.VMEM((HD, HD), F32), pltpu.VMEM((8, XW), F32)],
        args=(pc, cw, cpar, nw), exchange=exchange)


def _branch_c_bwd(pc, states, inverses, cw, cpar, nw, dy, name, exchange=None):
    t = pc.shape[0]
    rows = _block_rows(t, GDN_BLOCK_CHUNKS)
    nch = t // rows
    XW = 4 * HD
    hb = rows // 8

    nsys = 2 * rows // GROUP

    def body(p_ref, halo_ref, s_ref, x_ref, w_ref, cp_ref, nw_ref, dy_ref, dp_ref, dw_ref, dcp_ref, dnw_ref,
             dsa_ref, dsb_ref, carry_ref):
        h, i = pl.program_id(0), pl.program_id(1)

        @pl.when(i == 0)
        def _():
            dsa_ref[...] = jnp.zeros_like(dsa_ref)
            dsb_ref[...] = jnp.zeros_like(dsb_ref)
            carry_ref[...] = jnp.zeros_like(carry_ref)
            dw_ref[...] = jnp.zeros_like(dw_ref)
            dcp_ref[...] = jnp.zeros_like(dcp_ref)

        @pl.when((i == 0) & (h == 0))
        def _():
            dnw_ref[...] = jnp.zeros_like(dnw_ref)

        keep = 1.0 - (i == nch - 1).astype(F32)
        x_ext = jnp.concatenate([halo_ref[:, 0:XW] * keep, p_ref[:, 0:XW]], axis=0)
        block = functools.partial(_gdn_block, known=tuple(x_ref[0, 0, n] for n in range(nsys)))
        _, vjp, _ = jax.vjp(block, x_ext, p_ref[:, XW:XW + 2 * HD], p_ref[:, XW + 2 * HD:XW + 3 * HD],
                            s_ref[0, 0, 0], s_ref[0, 0, 1], w_ref[0:1, :], w_ref[1:2, :], w_ref[2:3, :], w_ref[3:4, :],
                            cp_ref[0, 0:1, :], cp_ref[0, 1:2, :], nw_ref[...], has_aux=True)
        dx, dz, dba, dsa, dsb, dw0, dw1, dw2, dw3, dal, ddt, dnw = vjp((dy_ref[...], dsa_ref[...], dsb_ref[...]))
        dp_ref[:, 0:XW] = _add_to_tail(dx[8:], carry_ref[...]).astype(BF16)
        dp_ref[:, XW:XW + 2 * HD] = dz.astype(BF16)
        dp_ref[:, XW + 2 * HD:XW + 3 * HD] = dba.astype(BF16)
        carry_ref[...] = dx[:8] * keep
        dsa_ref[...] = dsa
        dsb_ref[...] = dsb
        dw_ref[0:1, :] += dw0
        dw_ref[1:2, :] += dw1
        dw_ref[2:3, :] += dw2
        dw_ref[3:4, :] += dw3
        dcp_ref[0, 0:1, :] += dal
        dcp_ref[0, 1:2, :] += ddt
        dnw_ref[...] += dnw

    rev = lambda h, i: (nch - 1 - i, h)
    return _call_with_exchange(
        body, name=name, grid=(GDN_QK_HEADS, nch),
        in_specs=[pl.BlockSpec((rows, C_HEAD), rev),
                  pl.BlockSpec((8, C_HEAD), lambda h, i: (jnp.maximum((nch - 1 - i) * hb - 1, 0), h)),
                  pl.BlockSpec((1, 1, 2, HD, HD), lambda h, i: (h, nch - 1 - i, 0, 0, 0)),
                  pl.BlockSpec((1, 1, nsys, GROUP, GROUP), lambda h, i: (h, nch - 1 - i, 0, 0, 0)),
                  pl.BlockSpec((4, XW), lambda h, i: (0, h)),
                  pl.BlockSpec((1, 8, HD), lambda h, i: (h, 0, 0)),
                  pl.BlockSpec((1, HD), lambda h, i: (0, 0)),
                  pl.BlockSpec((rows, 2 * HD), rev)],
        out_specs=[pl.BlockSpec((rows, C_HEAD), rev),
                   pl.BlockSpec((4, XW), lambda h, i: (0, h)),
                   pl.BlockSpec((1, 8, HD), lambda h, i: (h, 0, 0)),
                   pl.BlockSpec((1, HD), lambda h, i: (0, 0))],
        out_shape=[jax.ShapeDtypeStruct((t, NC_COLS), BF16), jax.ShapeDtypeStruct((4, GDN_QK_HEADS * XW), F32),
                   jax.ShapeDtypeStruct((GDN_QK_HEADS, 8, HD), F32), jax.ShapeDtypeStruct((1, HD), F32)],
        scratch_shapes=[pltpu.VMEM((HD, HD), F32), pltpu.VMEM((HD, HD), F32), pltpu.VMEM((8, XW), F32)],
        args=(pc, pc, states, inverses, cw, cpar, nw, dy), exchange=exchange)


def _merge_fwd(pg, bg, ya, yb, yc, name):
    t = pg.shape[0]
    blk = _tile(t, 256)

    def body(g_ref, b_ref, a_ref, b2_ref, c_ref, o_ref):
        gate = _sigmoid(g_ref[...] + b_ref[...])
        o_ref[...] = (gate[:, 0:D] * a_ref[...] + gate[:, D:2 * D] * b2_ref[...] + gate[:, 2 * D:3 * D] * c_ref[...]).astype(BF16)

    row = pl.BlockSpec((blk, D), lambda i: (i, 0))
    return pl.pallas_call(
        body, name=name, grid=(t // blk,),
        in_specs=[pl.BlockSpec((blk, NG), lambda i: (i, 0)), pl.BlockSpec((1, NG), lambda i: (0, 0)), row, row, row],
        out_specs=row, out_shape=jax.ShapeDtypeStruct((t, D), BF16),
        compiler_params=_cparams("parallel"))(pg, bg, ya, yb, yc)


def _merge_bwd(dm, pg, bg, ya, yb, yc, name):
    t = pg.shape[0]
    blk = _tile(t, 256)

    def body(dm_ref, g_ref, b_ref, a_ref, b2_ref, c_ref, dg_ref, da_ref, db_ref, dc_ref, dbg_ref):
        @pl.when(pl.program_id(0) == 0)
        def _():
            dbg_ref[...] = jnp.zeros_like(dbg_ref)

        gate = _sigmoid(g_ref[...] + b_ref[...])
        dmv = dm_ref[...]
        for j, (y_ref, dy_ref) in enumerate(((a_ref, da_ref), (b2_ref, db_ref), (c_ref, dc_ref))):
            gj = gate[:, j * D:(j + 1) * D]
            dy_ref[...] = (dmv * gj).astype(BF16)
            dgj = dmv * y_ref[...] * gj * (1.0 - gj)
            dg_ref[:, j * D:(j + 1) * D] = dgj.astype(BF16)
            dbg_ref[:, j * D:(j + 1) * D] += jnp.sum(dgj, axis=0, keepdims=True)

    row = pl.BlockSpec((blk, D), lambda i: (i, 0))
    wide = pl.BlockSpec((blk, NG), lambda i: (i, 0))
    vec = pl.BlockSpec((1, NG), lambda i: (0, 0))
    return pl.pallas_call(
        body, name=name, grid=(t // blk,), in_specs=[row, wide, vec, row, row, row],
        out_specs=[wide, row, row, row, vec],
        out_shape=[jax.ShapeDtypeStruct((t, NG), BF16)] + [jax.ShapeDtypeStruct((t, D), BF16)] * 3
                  + [jax.ShapeDtypeStruct((1, NG), F32)],
        compiler_params=_cparams("arbitrary"))(dm, pg, bg, ya, yb, yc)


def _adamw_math(w, g, m, v):
    m = ADAM_B1 * m + (1.0 - ADAM_B1) * g
    v = ADAM_B2 * v + (1.0 - ADAM_B2) * (g * g)
    m_hat = m / (1.0 - ADAM_B1 ** ADAM_STEP)
    v_hat = v / (1.0 - ADAM_B2 ** ADAM_STEP)
    delta = -ADAM_LR * (m_hat / (jnp.sqrt(v_hat) + ADAM_EPS) + ADAM_WD * w)
    return delta, m, v


def _sum_adamw(parts, w, m, v, name):
    r, c = w.shape
    br = r if r <= 256 else 256
    assert r % br == 0

    def body(p_ref, w_ref, m_ref, v_ref, g_ref, d_ref, nm_ref, nv_ref):
        g = p_ref[0].astype(F32)
        for k in range(1, N_DEV):
            g = g + p_ref[k].astype(F32)
        g_ref[...] = g
        d_ref[...], nm_ref[...], nv_ref[...] = _adamw_math(w_ref[...], g, m_ref[...], v_ref[...])

    blk = pl.BlockSpec((br, c), lambda i: (i, 0))
    return pl.pallas_call(
        body, name=name, grid=(r // br,),
        in_specs=[pl.BlockSpec((N_DEV, br, c), lambda i: (0, i, 0)), blk, blk, blk], out_specs=[blk] * 4,
        out_shape=[jax.ShapeDtypeStruct((r, c), F32)] * 4, compiler_params=_cparams("parallel"))(parts, w, m, v)


def _adamw(g, w, m, v, name):
    def body(g_ref, w_ref, m_ref, v_ref, d_ref, nm_ref, nv_ref):
        d_ref[...], nm_ref[...], nv_ref[...] = _adamw_math(w_ref[...], g_ref[...], m_ref[...], v_ref[...])

    return pl.pallas_call(body, name=name, out_shape=[jax.ShapeDtypeStruct(w.shape, F32)] * 3)(g, w, m, v)


def _sum_slots(parts, name):
    def body(p_ref, o_ref):
        g = p_ref[0]
        for k in range(1, N_DEV):
            g = g + p_ref[k]
        o_ref[...] = g

    return pl.pallas_call(body, name=name, out_shape=jax.ShapeDtypeStruct(parts.shape[1:], F32))(parts)


def _exchange(srcs, name, broadcast):
    n = len(srcs)

    def body(*refs):
        copies = _exchange_copies(refs[:n], refs[n:2 * n], *refs[2 * n:], broadcast)
        for cp in copies:
            cp.start()
        for cp in copies:
            cp.wait()

    return pl.pallas_call(
        body, name=name, in_specs=[HBM_SPEC] * n, out_specs=[HBM_SPEC] * n, out_shape=_exchange_shapes(srcs, broadcast),
        scratch_shapes=_exchange_semaphores(n))(*srcs)


HBM_SPEC = pl.BlockSpec(memory_space=pltpu.HBM)


def _exchange_shapes(srcs, broadcast):
    return [jax.ShapeDtypeStruct((N_DEV,) + (s.shape if broadcast else s.shape[1:]), s.dtype) for s in srcs]


def _exchange_semaphores(n):
    return [pltpu.SemaphoreType.DMA((N_DEV - 1, n)), pltpu.SemaphoreType.DMA((N_DEV - 1, n)), pltpu.SemaphoreType.DMA((n,))]


def _exchange_copies(src_refs, dst_refs, send_sems, recv_sems, local_sems, broadcast):
    x, y, c = lax.axis_index("x"), lax.axis_index("y"), lax.axis_index("c")
    me = 4 * x + 2 * y + c
    copies = []
    for k in range(1, N_DEV):
        px = 1 - x if (k >> 2) & 1 else x
        py = 1 - y if (k >> 1) & 1 else y
        pc = 1 - c if k & 1 else c
        peer = 4 * px + 2 * py + pc
        for a, (src, dst) in enumerate(zip(src_refs, dst_refs)):
            copies.append(pltpu.make_async_remote_copy(
                src_ref=src if broadcast else src.at[peer], dst_ref=dst.at[me],
                send_sem=send_sems.at[k - 1, a], recv_sem=recv_sems.at[k - 1, a],
                device_id=(px, py, pc), device_id_type=MESH))
    for a, (src, dst) in enumerate(zip(src_refs, dst_refs)):
        copies.append(pltpu.make_async_copy(src if broadcast else src.at[me], dst.at[me], local_sems.at[a]))
    return copies


def _call_with_exchange(body, *, name, grid, in_specs, out_specs, out_shape, scratch_shapes, args, exchange):
    if exchange is None:
        outs = pl.pallas_call(body, name=name, grid=grid, in_specs=in_specs, out_specs=out_specs, out_shape=out_shape,
                              scratch_shapes=scratch_shapes,
                              compiler_params=_cparams(*["arbitrary"] * len(grid)))(*args)
        return outs, None
    srcs, broadcast = exchange
    n, n_in, n_out, n_scr = len(srcs), len(args), len(out_shape), len(scratch_shapes)
    steps = 1
    for g in grid:
        steps *= g

    def hosted(*refs):
        ins, src_refs = refs[:n_in], refs[n_in:n_in + n]
        outs, dst_refs = refs[n_in + n:n_in + n + n_out], refs[n_in + n + n_out:n_in + 2 * n + n_out]
        scratch = refs[n_in + 2 * n + n_out:]
        step = pl.program_id(0)
        for axis in range(1, len(grid)):
            step = step * grid[axis] + pl.program_id(axis)

        @pl.when(step == 0)
        def _():
            for cp in _exchange_copies(src_refs, dst_refs, *scratch[n_scr:], broadcast):
                cp.start()

        body(*ins, *outs, *scratch[:n_scr])

        @pl.when(step == steps - 1)
        def _():
            for cp in _exchange_copies(src_refs, dst_refs, *scratch[n_scr:], broadcast):
                cp.wait()

    outs = pl.pallas_call(
        hosted, name=name, grid=grid, in_specs=list(in_specs) + [HBM_SPEC] * n, out_specs=list(out_specs) + [HBM_SPEC] * n,
        out_shape=list(out_shape) + _exchange_shapes(srcs, broadcast),
        scratch_shapes=list(scratch_shapes) + _exchange_semaphores(n),
        compiler_params=_cparams(*["arbitrary"] * len(grid)))(*args, *srcs)
    return outs[:n_out], outs[n_out:]


def _regroup_w_in(w):
    wa = w[:, OFF_A:OFF_A + NA]
    seg = lambda off, h, n=HD: w[:, off + h * n: off + (h + 1) * n]
    wb = jnp.concatenate([seg(OFF_B + s * 512, h) for h in range(HGRN_HEADS) for s in range(4)], axis=1)
    parts = []
    for h in range(GDN_QK_HEADS):
        small = jnp.concatenate(
            [w[:, OFF_BETA + 2 * h: OFF_BETA + 2 * h + 2], w[:, OFF_CA + 2 * h: OFF_CA + 2 * h + 2],
             jnp.zeros((w.shape[0], HD - 4), w.dtype)], axis=1)
        parts += [seg(OFF_CQ, h), seg(OFF_CK, h), seg(OFF_CV, h, 2 * HD), seg(OFF_CZ, h, 2 * HD), small]
    wc = jnp.concatenate(parts, axis=1)
    wg = w[:, OFF_G:OFF_G + NG]
    return wa, wb, wc, wg


def _ungroup_dw_in(da, db, dc, dg):
    bq = [jnp.concatenate([db[:, h * 512 + s * HD: h * 512 + (s + 1) * HD] for h in range(HGRN_HEADS)], axis=1)
          for s in range(4)]
    ch = lambda h, lo, hi: dc[:, h * C_HEAD + lo: h * C_HEAD + hi]
    heads = range(GDN_QK_HEADS)
    cq = jnp.concatenate([ch(h, 0, HD) for h in heads], axis=1)
    ck = jnp.concatenate([ch(h, HD, 2 * HD) for h in heads], axis=1)
    cv = jnp.concatenate([ch(h, 2 * HD, 4 * HD) for h in heads], axis=1)
    cz = jnp.concatenate([ch(h, 4 * HD, 6 * HD) for h in heads], axis=1)
    cbeta = jnp.concatenate([ch(h, 6 * HD, 6 * HD + 2) for h in heads], axis=1)
    ca = jnp.concatenate([ch(h, 6 * HD + 2, 6 * HD + 4) for h in heads], axis=1)
    return jnp.concatenate([da] + bq + [cq, ck, cv, cbeta, ca, cz, dg], axis=1)


def _regroup_conv_c(cw):
    parts = []
    for h in range(GDN_QK_HEADS):
        parts += [cw[:, h * HD:(h + 1) * HD], cw[:, 512 + h * HD: 512 + (h + 1) * HD],
                  cw[:, 1024 + 2 * h * HD: 1024 + (2 * h + 2) * HD]]
    return jnp.concatenate(parts, axis=1)


def _ungroup_conv_c(d):
    heads = range(GDN_QK_HEADS)
    q = jnp.concatenate([d[:, h * 512: h * 512 + HD] for h in heads], axis=1)
    k = jnp.concatenate([d[:, h * 512 + HD: h * 512 + 2 * HD] for h in heads], axis=1)
    v = jnp.concatenate([d[:, h * 512 + 2 * HD: h * 512 + 4 * HD] for h in heads], axis=1)
    return jnp.concatenate([q, k, v], axis=1)


def _numel(shape):
    n = 1
    for d in shape:
        n *= d
    return n


def _pack(arrays, rows):
    flat = jnp.concatenate([a.reshape(-1) for a in arrays])
    return jnp.pad(flat, (0, rows * 128 - flat.shape[0])).reshape(rows, 128)


def _unpack(packed, shapes):
    flat = packed.reshape(-1)
    out, off = [], 0
    for s in shapes:
        out.append(flat[off:off + _numel(s)].reshape(s))
        off += _numel(s)
    return out


def _rows_for(shapes):
    return -(-sum(_numel(s) for s in shapes) // 1024) * 8


def kernel(x, norm_w, w_in, b_gate, conv_a, conv_c, a_log, dt_bias, lower_bounds, hgrn_norm_w, gdn_norm_w, w_out_a, w_out_b, w_out_c, w_o, final_norm_w, loss_target, m_norm_w, m_w_in, m_b_gate, m_conv_a, m_conv_c, m_a_log, m_dt_bias, m_lower_bounds, m_hgrn_norm_w, m_gdn_norm_w, m_w_out_a, m_w_out_b, m_w_out_c, m_w_o, m_final_norm_w, v_norm_w, v_w_in, v_b_gate, v_conv_a, v_conv_c, v_a_log, v_dt_bias, v_lower_bounds, v_hgrn_norm_w, v_gdn_norm_w, v_w_out_a, v_w_out_b, v_w_out_c, v_w_o, v_final_norm_w):
    me = 4 * lax.axis_index("x") + 2 * lax.axis_index("y") + lax.axis_index("c")
    xs = x[0]
    target = loss_target[0]
    in_shard = w_in.shape[2]

    big = [w_in, w_out_a, w_out_b, w_out_c, w_o]
    shards_of = lambda l: [w[l].astype(BF16) for w in big]
    conv_shapes = [(DEPTH, 3, CONV_W), (DEPTH, 4, 2048)]
    conv_rows = _rows_for(conv_shapes)
    ca_full = lax.dynamic_update_slice(jnp.zeros(conv_shapes[0], F32), conv_a, (0, 0, me * conv_a.shape[2]))
    cc_full = lax.dynamic_update_slice(jnp.zeros(conv_shapes[1], F32), conv_c, (0, 0, me * conv_c.shape[2]))
    conv_parts, = _exchange([_pack([ca_full, cc_full], conv_rows)], "gather_conv", broadcast=True)
    conv_a_full, conv_c_full = _unpack(_sum_slots(conv_parts, "sum_conv"), conv_shapes)

    lb_pad = jnp.pad(lower_bounds, ((0, 8 - DEPTH), (0, 0)))
    lbs = _lower_bounds_fwd(lb_pad, "lower_bounds_fwd")

    def input_weights(l, g_in):
        wa, wb, wc, wg = _regroup_w_in(jnp.concatenate([g_in[q] for q in range(N_DEV)], axis=1))
        lanes = lambda vec: jnp.pad(vec.reshape(GDN_QK_HEADS, 1, 2), ((0, 0), (0, 0), (0, HD - 2)))
        cpar = jnp.concatenate([lanes(a_log[l]), lanes(dt_bias[l]), jnp.zeros((GDN_QK_HEADS, 6, HD), F32)], axis=1)
        return dict(
            wa=wa, wb=wb, wc=wc, wg=wg, cpar=cpar,
            nw=norm_w[l:l + 1], bg=b_gate[l:l + 1], cwa=conv_a_full[l], cwc=_regroup_conv_c(conv_c_full[l]),
            lb=lbs[l:l + 1], hnw=hgrn_norm_w[l:l + 1], gnw=gdn_norm_w[l:l + 1])

    def output_weights(g_oa, g_ob, g_oc, g_o):
        return dict(woa=jnp.concatenate([g_oa[q] for q in range(N_DEV)], axis=1),
                    wob=jnp.concatenate([g_ob[q] for q in range(N_DEV)], axis=1), woc=g_oc.reshape(D, D), wo=g_o.reshape(D, D))

    layers = [input_weights(0, _exchange(shards_of(0)[:1], "gather_l0", broadcast=True)[0])]

    saved = []
    cur = xs
    for l in range(DEPTH):
        L = layers[l]
        n = f"l{l}_"
        h = _rmsnorm_fwd(cur, L["nw"], n + "rms")
        pa = _matmul(h, L["wa"], "nn", n + "proj_a")
        pb = _matmul(h, L["wb"], "nn", n + "proj_b")
        pc = _matmul(h, L["wc"], "nn", n + "proj_c")
        pg = _matmul(h, L["wg"], "nn", n + "proj_g")
        ua = _branch_a_fwd(pa, L["cwa"], n + "conv_fwd")
        carry = (shards_of(l)[1:], True) if l == 0 else None
        (ub, sb), gathered = _branch_b_fwd(pb, L["lb"], L["hnw"], n + "hgrn_fwd", exchange=carry)
        if carry is not None:
            L.update(output_weights(*gathered))
        carry = (shards_of(l + 1), True) if l + 1 < DEPTH else None
        (uc, sc, xc), gathered = _branch_c_fwd(pc, L["cwc"], L["cpar"], L["gnw"], n + "gdn_fwd", exchange=carry)
        if carry is not None:
            layers.append(dict(input_weights(l + 1, gathered[0]), **output_weights(*gathered[1:])))
        ya = _matmul(ua, L["woa"], "nn", n + "out_a")
        yb = _matmul(ub, L["wob"], "nn", n + "out_b")
        yc = _matmul(uc, L["woc"], "nn", n + "out_c")
        merged = _merge_fwd(pg, L["bg"], ya, yb, yc, n + "merge")
        nxt = _matmul(merged, L["wo"], "nn", n + "out_o", residual=cur)
        saved.append(dict(x=cur, h=h, pa=pa, pb=pb, pc=pc, pg=pg, ua=ua, ub=ub, uc=uc, sb=sb, sc=sc, xc=xc,
                          ya=ya, yb=yb, yc=yc, merged=merged))
        cur = nxt

    loss_part, dx, d_final = _loss_head(cur, final_norm_w.reshape(1, D), target, "loss_head")

    def outgoing(g):
        cols = lambda a, n: jnp.stack([a[:, p * n:(p + 1) * n] for p in range(N_DEV)]).astype(BF16)
        rows = lambda a: a.reshape(N_DEV, a.shape[0] // N_DEV, a.shape[1]).astype(BF16)
        first = [cols(g["w_in"], in_shard)] if "w_in" in g else [None]
        return first + [cols(g["w_out_a"], 128), cols(g["w_out_b"], 128), rows(g["w_out_c"]), rows(g["w_o"])]

    grads = [None] * DEPTH
    dlbs_rows = [None] * DEPTH
    incoming = [None] * DEPTH
    for l in reversed(range(DEPTH)):
        L, S = layers[l], saved[l]
        n = f"l{l}_"
        dmerged = _matmul(dx, L["wo"], "nt", n + "d_merged")
        d_wo = _matmul(S["merged"], dx, "tn", n + "dw_o")
        dpg, dya, dyb, dyc, d_bg = _merge_bwd(dmerged, S["pg"], L["bg"], S["ya"], S["yb"], S["yc"], n + "merge_bwd")
        dua = _matmul(dya, L["woa"], "nt", n + "d_ua")
        dub = _matmul(dyb, L["wob"], "nt", n + "d_ub")
        duc = _matmul(dyc, L["woc"], "nt", n + "d_uc")
        d_woa = _matmul(S["ua"], dya, "tn", n + "dw_out_a")
        d_wob = _matmul(S["ub"], dyb, "tn", n + "dw_out_b")
        d_woc = _matmul(S["uc"], dyc, "tn", n + "dw_out_c")
        dpa, d_cwa = _branch_a_bwd(S["pa"], L["cwa"], dua, n + "conv_bwd")
        out_grads = dict(w_out_a=d_woa, w_out_b=d_wob, w_out_c=d_woc, w_o=d_wo)
        carry = (outgoing(out_grads)[1:], False) if l == 0 else None
        (dpb, d_lb, d_hnw), arrived_out = _branch_b_bwd(S["pb"], S["sb"], L["lb"], L["hnw"], dub, n + "hgrn_bwd", exchange=carry)
        carry = (outgoing(grads[l + 1]), False) if l + 1 < DEPTH else None
        (dpc, d_cwc, d_cpar, d_gnw), arrived = _branch_c_bwd(S["pc"], S["sc"], S["xc"], L["cwc"], L["cpar"], L["gnw"], duc,
                                                             n + "gdn_bwd", exchange=carry)
        if carry is not None:
            incoming[l + 1] = arrived
        dh = _matmul(dpa, L["wa"], "nt", n + "dh_a")
        dh = _matmul(dpb, L["wb"], "nt", n + "dh_b", residual=dh)
        dh = _matmul(dpc, L["wc"], "nt", n + "dh_c", residual=dh)
        dh = _matmul(dpg, L["wg"], "nt", n + "dh_g", residual=dh)
        d_win = _ungroup_dw_in(_matmul(S["h"], dpa, "tn", n + "dw_a"), _matmul(S["h"], dpb, "tn", n + "dw_b"),
                               _matmul(S["h"], dpc, "tn", n + "dw_c"), _matmul(S["h"], dpg, "tn", n + "dw_g"))
        dx, d_nw = _rmsnorm_bwd(dh, S["x"], L["nw"], dx, n + "rms_bwd")
        dlbs_rows[l] = d_lb
        grads[l] = dict(w_in=d_win, w_out_a=d_woa, w_out_b=d_wob, w_out_c=d_woc, w_o=d_wo, norm_w=d_nw[0],
                        b_gate=d_bg[0], conv_a=d_cwa, conv_c=_ungroup_conv_c(d_cwc),
                        a_log=d_cpar[:, 0, 0:2].reshape(-1), dt_bias=d_cpar[:, 1, 0:2].reshape(-1),
                        hgrn_norm_w=d_hnw[0], gdn_norm_w=d_gnw[0])
    grad_x = dx[None]
    d_lower = _lower_bounds_bwd(lb_pad, jnp.pad(jnp.concatenate(dlbs_rows, axis=0), ((0, 8 - DEPTH), (0, 0))),
                                "lower_bounds_bwd")[:DEPTH]

    incoming[0] = _exchange(outgoing(grads[0])[:1], "exchange_grads_l0", broadcast=False) + arrived_out
    stack = lambda name: jnp.stack([grads[l][name] for l in range(DEPTH)])
    big_out = {}
    for j, (name, w, m, v) in enumerate((("w_in", w_in, m_w_in, v_w_in), ("w_out_a", w_out_a, m_w_out_a, v_w_out_a),
                                         ("w_out_b", w_out_b, m_w_out_b, v_w_out_b), ("w_out_c", w_out_c, m_w_out_c, v_w_out_c),
                                         ("w_o", w_o, m_w_o, v_w_o))):
        parts = jnp.concatenate([incoming[l][j] for l in range(DEPTH)], axis=1)
        r2 = lambda a: a.reshape(parts.shape[1], parts.shape[2])
        outs = _sum_adamw(parts, r2(w), r2(m), r2(v), "adamw_" + name)
        big_out[name] = [o.reshape(w.shape) for o in outs]

    small_names = ["norm_w", "b_gate", "conv_a", "conv_c", "a_log", "dt_bias", "lower_bounds", "hgrn_norm_w",
                   "gdn_norm_w", "final_norm_w", "loss"]
    small_vals = {k: stack(k) for k in ("norm_w", "b_gate", "conv_a", "conv_c", "a_log", "dt_bias", "hgrn_norm_w", "gdn_norm_w")}
    small_vals.update(lower_bounds=d_lower, final_norm_w=d_final[0], loss=loss_part.reshape(1))
    small_shapes = [small_vals[k].shape for k in small_names]
    small_rows = _rows_for(small_shapes)
    small_parts, = _exchange([_pack([small_vals[k] for k in small_names], small_rows)], "exchange_small", broadcast=True)
    total = dict(zip(small_names, _unpack(_sum_slots(small_parts, "sum_small"), small_shapes)))
    loss = total["loss"][0]
    g_conv_a = lax.dynamic_slice(total["conv_a"], (0, 0, me * conv_a.shape[2]), conv_a.shape)
    g_conv_c = lax.dynamic_slice(total["conv_c"], (0, 0, me * conv_c.shape[2]), conv_c.shape)

    small_w = dict(norm_w=(norm_w, m_norm_w, v_norm_w), b_gate=(b_gate, m_b_gate, v_b_gate),
                   conv_a=(conv_a, m_conv_a, v_conv_a), conv_c=(conv_c, m_conv_c, v_conv_c),
                   a_log=(a_log, m_a_log, v_a_log), dt_bias=(dt_bias, m_dt_bias, v_dt_bias),
                   lower_bounds=(lower_bounds, m_lower_bounds, v_lower_bounds),
                   hgrn_norm_w=(hgrn_norm_w, m_hgrn_norm_w, v_hgrn_norm_w), gdn_norm_w=(gdn_norm_w, m_gdn_norm_w, v_gdn_norm_w),
                   final_norm_w=(final_norm_w, m_final_norm_w, v_final_norm_w))
    small_g = dict(total, conv_a=g_conv_a, conv_c=g_conv_c)
    upd_names = small_names[:-1]
    upd_shapes = [small_w[k][0].shape for k in upd_names]
    upd_rows = _rows_for(upd_shapes)
    pk = lambda j: _pack([small_w[k][j] for k in upd_names], upd_rows)
    s_delta, s_m, s_v = _adamw(_pack([small_g[k] for k in upd_names], upd_rows), pk(0), pk(1), pk(2), "adamw_small")
    small_out = {k: [small_g[k], d, mm, vv] for k, d, mm, vv in
                 zip(upd_names, _unpack(s_delta, upd_shapes), _unpack(s_m, upd_shapes), _unpack(s_v, upd_shapes))}

    order = ["norm_w", "w_in", "b_gate", "conv_a", "conv_c", "a_log", "dt_bias", "lower_bounds", "hgrn_norm_w",
             "gdn_norm_w", "w_out_a", "w_out_b", "w_out_c", "w_o", "final_norm_w"]
    res = {**small_out, **big_out}
    outs = [loss, grad_x]
    for j in range(4):
        outs += [res[k][j] for k in order]
    return tuple(outs)
```

```python
import functools

import jax
import jax.numpy as jnp
from jax import lax
from jax.experimental import pallas as pl
from jax.experimental.pallas import tpu as pltpu

F32 = jnp.float32
BF16 = jnp.bfloat16
MESH = pl.DeviceIdType.MESH

N_DEV = 8
D = 1024
DEPTH = 2
CHUNK = 64
HGRN_BLOCK_CHUNKS = 16
GDN_BLOCK_CHUNKS = 8
GROUP = 128
NORM_EPS = 1e-6
L2_EPS = 1e-6
MIN_F = 1e-30
HD = 128
HGRN_HEADS = 4
GDN_QK_HEADS = 4
CONV_W = 512
IN_COLS = 10256
OFF_A, OFF_B, OFF_CQ, OFF_CK, OFF_CV, OFF_BETA, OFF_CA, OFF_CZ, OFF_G = (
    0, 2048, 4096, 4608, 5120, 6144, 6152, 6160, 7184)
NA, NB, NC_COLS, NG = 2048, 2048, 3584, 3072
C_HEAD = 896

ADAM_LR, ADAM_B1, ADAM_B2, ADAM_EPS, ADAM_WD, ADAM_STEP = 0.001, 0.9, 0.999, 1e-08, 0.01, 10

VMEM_LIMIT = 56 * 1024 * 1024
MM_TILE = 1024


def _cparams(*sem):
    return pltpu.CompilerParams(dimension_semantics=sem, vmem_limit_bytes=VMEM_LIMIT)


def _tile(dim, cap):
    if dim <= cap:
        return dim
    t = (cap // 128) * 128
    while dim % t:
        t -= 128
    return t


def _sigmoid(x):
    return 1.0 / (1.0 + jnp.exp(-x))


def _silu(x):
    return x * _sigmoid(x)


def _softplus(x):
    return jnp.maximum(x, 0.0) + jnp.log(1.0 + jnp.exp(-jnp.abs(x)))


def _dot(a, b, dims, precision=None):
    if precision is None:
        a, b = a.astype(BF16), b.astype(BF16)
    return lax.dot_general(a, b, (dims, ((), ())), precision=precision, preferred_element_type=F32)


def _nn(a, b, precision=None):
    return _dot(a, b, ((1,), (0,)), precision)


def _nt(a, b, precision=None):
    return _dot(a, b, ((1,), (1,)), precision)


def _tn(a, b, precision=None):
    return _dot(a, b, ((0,), (0,)), precision)


def _sum_rows_split(mat01, x):
    m = mat01.astype(BF16)
    hi = x.astype(BF16)
    low = (x - hi.astype(F32)).astype(BF16)
    return _nn(m, hi) + _nn(m, low)


@functools.partial(jax.custom_vjp, nondiff_argnums=(1,))
def _shift_rows(x, d):
    return x if d == 0 else pltpu.roll(x, d, 0)


def _shift_rows_fwd(x, d):
    return _shift_rows(x, d), None


def _shift_rows_bwd(d, _, ct):
    return ((ct if d == 0 else pltpu.roll(ct, ct.shape[0] - d, 0)),)


_shift_rows.defvjp(_shift_rows_fwd, _shift_rows_bwd)


def _iota2(shape):
    return lax.broadcasted_iota(jnp.int32, shape, 0), lax.broadcasted_iota(jnp.int32, shape, 1)


def _lane_pick(x, i):
    lane = lax.broadcasted_iota(jnp.int32, x.shape, 1)
    return jnp.sum(jnp.where(lane == i, x, 0.0), axis=1, keepdims=True)


def _hgrn_block(qr, fr, ir, zr, st0, lb, nw):
    rows = qr.shape[0]
    r, c = _iota2((CHUNK, CHUNK))
    halves = [1 << j for j in range(CHUNK.bit_length() - 1)]
    mats = [c <= r, c > r]
    pairs = []
    for hb in halves:
        same = (r // hb) == (c // hb)
        if hb > 1:
            mats += [(c <= r) & same, (c > r) & same]
        pairs.append(((r // (2 * hb)) == (c // (2 * hb))) & ((r // hb) == (c // hb) + 1))
    stack = jnp.concatenate([m.astype(F32) for m in mats], axis=0)

    q = _silu(qr) * (HD ** -0.5)
    fg = lb + (1.0 - lb) * _sigmoid(fr)
    logf = jnp.log(jnp.maximum(fg, MIN_F))
    kk = 1.0 - fg
    v = ir

    chunks = [slice(s, s + CHUNK) for s in range(0, rows, CHUNK)]
    cums = [_sum_rows_split(stack, logf[sl]) for sl in chunks]
    part = lambda i: jnp.concatenate([cs[i * CHUNK:(i + 1) * CHUNK] for cs in cums], axis=0)
    qg = q * jnp.exp(part(0))
    ks = kk * jnp.exp(part(1))
    q_lv = [q * jnp.exp(logf)] + [q * jnp.exp(part(2 * j)) for j in range(1, len(halves))]
    k_lv = [kk] + [kk * jnp.exp(part(2 * j + 1)) for j in range(1, len(halves))]
    st = st0
    outs = []
    for sl in chunks:
        scores = jnp.where(pairs[0], _nt(q_lv[0][sl], k_lv[0][sl]), 0.0)
        for j in range(1, len(halves)):
            scores += jnp.where(pairs[j], _nt(q_lv[j][sl], k_lv[j][sl]), 0.0)
        outs.append(_nn(scores, v[sl]) + _nt(qg[sl], st))
        st = st * jnp.exp(jnp.sum(logf[sl], axis=0, keepdims=True)) + _tn(v[sl], ks[sl])
    o = jnp.concatenate(outs, axis=0) + jnp.sum(q * kk, axis=1, keepdims=True) * v
    y = o * lax.rsqrt(jnp.mean(o * o, axis=1, keepdims=True) + NORM_EPS) * nw * _silu(zr)
    return y, st


def _unit_lower_inverses(ms):
    r, c = _iota2(ms[0].shape)
    xs = [jnp.where(r == c, 1.0, 0.0) - jnp.where((r // 2) == (c // 2), m, 0.0) for m in ms]
    b = 2
    while b < CHUNK:
        pick = ((r // (2 * b)) == (c // (2 * b))) & ((r // b) != (c // b))
        ts = [_nn(x, jnp.where(pick, m, 0.0)) for x, m in zip(xs, ms)]
        xs = [x - _nn(t, x) for x, t in zip(xs, ts)]
        b *= 2
    return tuple(x.astype(BF16) for x in xs)


@jax.custom_vjp
def _known_inverses(ms, xs):
    return xs


def _known_inverses_fwd(ms, xs):
    return xs, xs


def _known_inverses_bwd(xs, cts):
    r, c = _iota2(xs[0].shape)
    keep = (c < r) & ((r // CHUNK) == (c // CHUNK))
    ts = [_tn(x, ct) for x, ct in zip(xs, cts)]
    return (tuple(jnp.where(keep, -_nt(t, x), 0.0) for t, x in zip(ts, xs)), tuple(jnp.zeros_like(x) for x in xs))


_known_inverses.defvjp(_known_inverses_fwd, _known_inverses_bwd)


def _chunk_cumsum(x):
    row = lax.broadcasted_iota(jnp.int32, x.shape, 0) % CHUNK
    d = 1
    while d < CHUNK:
        x = x + jnp.where(row >= d, _shift_rows(x, d), 0.0)
        d *= 2
    return x


def _gdn_block(x_ext, z, ba, s0a, s0b, w0, w1, w2, w3, alog, dtb, nw, known=None):
    rows = z.shape[0]
    conv = (w0 * _shift_rows(x_ext, 3) + w1 * _shift_rows(x_ext, 2) + w2 * _shift_rows(x_ext, 1) + w3 * x_ext)
    cc = _silu(conv[8:])
    qc, kc = cc[:, 0:HD], cc[:, HD:2 * HD]
    q = qc * lax.rsqrt(jnp.sum(qc * qc, axis=1, keepdims=True) + L2_EPS) * (HD ** -0.5)
    k = kc * lax.rsqrt(jnp.sum(kc * kc, axis=1, keepdims=True) + L2_EPS)

    r, c = _iota2((GROUP, GROUP))
    same = (r // CHUNK) == (c // CHUNK)
    causal, strict, eye = same & (c <= r), same & (c < r), r == c
    heads = (0, 1)
    groups = [slice(lo, lo + GROUP) for lo in range(0, rows, GROUP)]
    chunks = [slice(lo, lo + CHUNK) for lo in range(0, rows, CHUNK)]

    v, loga, g_w, kb, kg, qg = [], [], [], [], [], []
    for i in heads:
        v.append(cc[:, (2 + i) * HD:(3 + i) * HD])
        beta = _sigmoid(_lane_pick(ba, i))
        a_neg = -jnp.exp(_lane_pick(alog, i))
        loga.append(a_neg * _softplus(_lane_pick(ba, 2 + i) + _lane_pick(dtb, i)))
        g_w.append(_chunk_cumsum(jnp.broadcast_to(loga[i], (rows, HD))))
        kb.append(k * beta)
        kg.append(k * jnp.exp(g_w[i]))
        qg.append(q * jnp.exp(g_w[i]))

    systems = [(i, gs) for gs in groups for i in heads]
    dec_c, ms = [], []
    for i, gs in systems:
        g_sq = g_w[i][gs]
        g_row = jnp.sum(jnp.where(eye, g_sq, 0.0), axis=0, keepdims=True)
        diff = g_sq - g_row
        dec_c.append(jnp.where(causal, jnp.exp(jnp.where(causal, diff, 0.0)), 0.0))
        ms.append(jnp.where(strict, _nt(k[gs], kb[i][gs]) * dec_c[-1], 0.0))
    xs = _unit_lower_inverses(tuple(ms)) if known is None else _known_inverses(tuple(ms), known)
    u = [[None] * len(groups) for _ in heads]
    w = [[None] * len(groups) for _ in heads]
    qk = [[None] * len(groups) for _ in heads]
    for n, (i, gs) in enumerate(systems):
        j = n // len(heads)
        u[i][j] = _nn(xs[n], v[i][gs])
        w[i][j] = _nn(xs[n], kg[i][gs])
        qk[i][j] = _nt(q[gs], kb[i][gs]) * dec_c[n]
    u = [jnp.concatenate(p, axis=0) for p in u]
    w = [jnp.concatenate(p, axis=0) for p in w]

    decay, p_mat, q_mat = {}, {}, {}
    for n, sl in enumerate(chunks):
        for i in heads:
            g_last = jnp.sum(loga[i][sl], axis=0, keepdims=True)
            kd = kb[i][sl] * jnp.exp(g_last - g_w[i][sl])
            decay[n, i] = jnp.exp(g_last)
            p_mat[n, i] = -_tn(kd, w[i][sl])
            q_mat[n, i] = _tn(kd, u[i][sl])
    s = [s0a, s0b]
    s_at = {}
    for n in range(len(chunks)):
        for i in heads:
            s_at[n, i] = s[i]
            s[i] = s[i] * decay[n, i] + _nn(p_mat[n, i], s[i]) + q_mat[n, i]

    ys = []
    for i in heads:
        e = jnp.concatenate([u[i][sl] - _nn(w[i][sl], s_at[n, i]) for n, sl in enumerate(chunks)], axis=0)
        o_state = jnp.concatenate([_nn(qg[i][sl], s_at[n, i]) for n, sl in enumerate(chunks)], axis=0)
        o = o_state + jnp.concatenate([_nn(qk[i][j], e[gs]) for j, gs in enumerate(groups)], axis=0)
        zi = z[:, i * HD:(i + 1) * HD]
        ys.append(o * lax.rsqrt(jnp.mean(o * o, axis=1, keepdims=True) + NORM_EPS) * nw * _silu(zi))
    return (jnp.concatenate(ys, axis=1), s[0], s[1]), xs


def _add_to_tail(x, tail):
    return x + jnp.concatenate([jnp.zeros((x.shape[0] - 8, x.shape[1]), x.dtype), tail], axis=0)


def _conv_a_block(ab, ac_ext, ax_ext, az, w0, w1, w2):
    u = ac_ext * ax_ext
    conv = (w0 * _shift_rows(u, 2) + w1 * _shift_rows(u, 1) + w2 * u)[8:]
    return ab * conv * _silu(az)


def _matmul(a, b, mode, name, residual=None, out_dtype=F32):
    if mode == "nn":
        (m, k), n = a.shape, b.shape[1]
    elif mode == "nt":
        (m, k), n = a.shape, b.shape[0]
    else:
        (k, m), n = a.shape, b.shape[1]
    tm, tn, tk = _tile(m, MM_TILE), _tile(n, MM_TILE), _tile(k, MM_TILE)
    nk = k // tk
    dims = {"nn": ((1,), (0,)), "nt": ((1,), (1,)), "tn": ((0,), (0,))}[mode]
    a_spec = pl.BlockSpec((tk, tm), lambda i, j, s: (s, i)) if mode == "tn" else pl.BlockSpec((tm, tk), lambda i, j, s: (i, s))
    b_spec = pl.BlockSpec((tn, tk), lambda i, j, s: (j, s)) if mode == "nt" else pl.BlockSpec((tk, tn), lambda i, j, s: (s, j))
    o_spec = pl.BlockSpec((tm, tn), lambda i, j, s: (i, j))
    has_res = residual is not None

    def finish(out, r_ref, o_ref):
        if has_res:
            out = out + r_ref[...]
        o_ref[...] = out.astype(out_dtype)

    def body_one_pass(*refs):
        finish(_dot(refs[0][...], refs[1][...], dims), refs[2] if has_res else None, refs[-1])

    def body_reduce(*refs):
        a_ref, b_ref = refs[0], refs[1]
        r_ref = refs[2] if has_res else None
        o_ref, acc_ref = refs[-2], refs[-1]
        s = pl.program_id(2)

        @pl.when(s == 0)
        def _():
            acc_ref[...] = jnp.zeros_like(acc_ref)

        acc_ref[...] += _dot(a_ref[...], b_ref[...], dims)

        @pl.when(s == nk - 1)
        def _():
            finish(acc_ref[...], r_ref, o_ref)

    args, specs = [a, b], [a_spec, b_spec]
    if has_res:
        args.append(residual)
        specs.append(o_spec)
    return pl.pallas_call(
        body_one_pass if nk == 1 else body_reduce, name=name, grid=(m // tm, n // tn, nk), in_specs=specs, out_specs=o_spec,
        out_shape=jax.ShapeDtypeStruct((m, n), out_dtype),
        scratch_shapes=[] if nk == 1 else [pltpu.VMEM((tm, tn), F32)],
        compiler_params=_cparams("parallel", "parallel", "arbitrary"))(*args)


def _matmul_nt_sum(pairs, name):
    m, n = pairs[0][0].shape[0], pairs[0][1].shape[0]
    tm, tn = _tile(m, MM_TILE // 2), _tile(n, MM_TILE)
    tks = [_tile(a.shape[1], MM_TILE) for a, _ in pairs]
    nks = [a.shape[1] // tk for (a, _), tk in zip(pairs, tks)]
    offs = [sum(nks[:i]) for i in range(len(pairs))]
    total = sum(nks)

    def body(*refs):
        o_ref, acc_ref = refs[-2], refs[-1]
        s = pl.program_id(2)

        @pl.when(s == 0)
        def _():
            acc_ref[...] = jnp.zeros_like(acc_ref)

        for i, (off, nk) in enumerate(zip(offs, nks)):
            @pl.when((s >= off) & (s < off + nk))
            def _(i=i):
                acc_ref[...] += _dot(refs[2 * i][...], refs[2 * i + 1][...], ((1,), (1,)))

        @pl.when(s == total - 1)
        def _():
            o_ref[...] = acc_ref[...]

    args, specs = [], []
    for (a, b), tk, off, nk in zip(pairs, tks, offs, nks):
        k_of = lambda s, off=off, nk=nk: jnp.clip(s - off, 0, nk - 1)
        args += [a, b]
        specs += [pl.BlockSpec((tm, tk), lambda i, j, s, k_of=k_of: (i, k_of(s))),
                  pl.BlockSpec((tn, tk), lambda i, j, s, k_of=k_of: (j, k_of(s)))]
    return pl.pallas_call(
        body, name=name, grid=(m // tm, n // tn, total), in_specs=specs,
        out_specs=pl.BlockSpec((tm, tn), lambda i, j, s: (i, j)), out_shape=jax.ShapeDtypeStruct((m, n), F32),
        scratch_shapes=[pltpu.VMEM((tm, tn), F32)],
        compiler_params=_cparams("parallel", "parallel", "arbitrary"))(*args)


def _rmsnorm_fwd(x, w, name):
    t = x.shape[0]
    blk = _tile(t, 512)

    def body(x_ref, w_ref, h_ref):
        xv = x_ref[...]
        h_ref[...] = (xv * lax.rsqrt(jnp.mean(xv * xv, axis=1, keepdims=True) + NORM_EPS) * w_ref[...]).astype(BF16)

    return pl.pallas_call(
        body, name=name, grid=(t // blk,),
        in_specs=[pl.BlockSpec((blk, D), lambda i: (i, 0)), pl.BlockSpec((1, D), lambda i: (0, 0))],
        out_specs=pl.BlockSpec((blk, D), lambda i: (i, 0)), out_shape=jax.ShapeDtypeStruct((t, D), BF16),
        compiler_params=_cparams("parallel"))(x, w)


def _rmsnorm_bwd(dh, x, w, dxo, name):
    t = x.shape[0]
    blk = _tile(t, 512)

    def body(dh_ref, x_ref, w_ref, dxo_ref, dx_ref, dw_ref):
        @pl.when(pl.program_id(0) == 0)
        def _():
            dw_ref[...] = jnp.zeros_like(dw_ref)

        xv, dhv = x_ref[...], dh_ref[...]
        rs = lax.rsqrt(jnp.mean(xv * xv, axis=1, keepdims=True) + NORM_EPS)
        xh = xv * rs
        dw_ref[...] += jnp.sum(dhv * xh, axis=0, keepdims=True)
        dxh = dhv * w_ref[...]
        dx_ref[...] = rs * (dxh - xh * jnp.mean(dxh * xh, axis=1, keepdims=True)) + dxo_ref[...]

    row = pl.BlockSpec((blk, D), lambda i: (i, 0))
    vec = pl.BlockSpec((1, D), lambda i: (0, 0))
    return pl.pallas_call(
        body, name=name, grid=(t // blk,), in_specs=[row, row, vec, row], out_specs=[row, vec],
        out_shape=[jax.ShapeDtypeStruct((t, D), F32), jax.ShapeDtypeStruct((1, D), F32)],
        compiler_params=_cparams("arbitrary"))(dh, x, w, dxo)


def _loss_head(x, w, target, name):
    t = x.shape[0]
    blk = _tile(t, 512)

    def body(x_ref, w_ref, t_ref, loss_ref, dx_ref, dw_ref):
        @pl.when(pl.program_id(0) == 0)
        def _():
            dw_ref[...] = jnp.zeros_like(dw_ref)
            loss_ref[...] = jnp.zeros_like(loss_ref)

        xv = x_ref[...]
        rs = lax.rsqrt(jnp.mean(xv * xv, axis=1, keepdims=True) + NORM_EPS)
        xh = xv * rs
        err = xh * w_ref[...] - t_ref[...]
        loss_ref[...] += 0.5 * jnp.sum(jnp.mean(err * err, axis=1, keepdims=True), axis=0, keepdims=True)
        dy = err * (1.0 / D)
        dw_ref[...] += jnp.sum(dy * xh, axis=0, keepdims=True)
        dxh = dy * w_ref[...]
        dx_ref[...] = rs * (dxh - xh * jnp.mean(dxh * xh, axis=1, keepdims=True))

    row = pl.BlockSpec((blk, D), lambda i: (i, 0))
    vec = pl.BlockSpec((1, D), lambda i: (0, 0))
    return pl.pallas_call(
        body, name=name, grid=(t // blk,), in_specs=[row, vec, row],
        out_specs=[pl.BlockSpec((1, 1), lambda i: (0, 0)), row, vec],
        out_shape=[jax.ShapeDtypeStruct((1, 1), F32), jax.ShapeDtypeStruct((t, D), F32), jax.ShapeDtypeStruct((1, D), F32)],
        compiler_params=_cparams("arbitrary"))(x, w, target)


def _lbs_of(lb):
    r = lax.broadcasted_iota(jnp.int32, lb.shape, 0)
    real = r < DEPTH
    mx = lax.stop_gradient(jnp.max(jnp.where(real, lb, -jnp.inf), axis=0, keepdims=True))
    e = jnp.where(real, jnp.exp(jnp.where(real, lb - mx, 0.0)), 0.0)
    p = e / jnp.sum(e, axis=0, keepdims=True)
    out = jnp.zeros_like(lb)
    run = jnp.zeros_like(mx)
    for l in range(1, DEPTH):
        run = run + jnp.sum(jnp.where(r == l, p, 0.0), axis=0, keepdims=True)
        out = out + jnp.where(r == l, run, 0.0)
    return out


def _lower_bounds_fwd(lbp, name):
    def body(lb_ref, o_ref):
        o_ref[...] = _lbs_of(lb_ref[...])

    return pl.pallas_call(body, name=name, out_shape=jax.ShapeDtypeStruct(lbp.shape, F32))(lbp)


def _lower_bounds_bwd(lbp, dlbs, name):
    def body(lb_ref, d_ref, o_ref):
        _, vjp = jax.vjp(_lbs_of, lb_ref[...])
        o_ref[...] = vjp(d_ref[...])[0]

    return pl.pallas_call(body, name=name, out_shape=jax.ShapeDtypeStruct(lbp.shape, F32))(lbp, dlbs)


def _branch_a_fwd(pa, cw, name):
    t = pa.shape[0]
    blk = _tile(t, 512)
    W = CONV_W

    def body(p_ref, w_ref, y_ref, hc_ref, hx_ref):
        @pl.when(pl.program_id(0) == 0)
        def _():
            hc_ref[...] = jnp.zeros_like(hc_ref)
            hx_ref[...] = jnp.zeros_like(hx_ref)

        ac, ax = p_ref[:, W:2 * W], p_ref[:, 2 * W:3 * W]
        y_ref[...] = _conv_a_block(
            p_ref[:, 0:W], jnp.concatenate([hc_ref[...], ac], axis=0), jnp.concatenate([hx_ref[...], ax], axis=0),
            p_ref[:, 3 * W:4 * W], w_ref[0:1, :], w_ref[1:2, :], w_ref[2:3, :]).astype(BF16)
        hc_ref[...] = p_ref[blk - 8:blk, W:2 * W]
        hx_ref[...] = p_ref[blk - 8:blk, 2 * W:3 * W]

    return pl.pallas_call(
        body, name=name, grid=(t // blk,),
        in_specs=[pl.BlockSpec((blk, NA), lambda i: (i, 0)), pl.BlockSpec((3, W), lambda i: (0, 0))],
        out_specs=pl.BlockSpec((blk, W), lambda i: (i, 0)), out_shape=jax.ShapeDtypeStruct((t, W), BF16),
        scratch_shapes=[pltpu.VMEM((8, W), F32), pltpu.VMEM((8, W), F32)],
        compiler_params=_cparams("arbitrary"))(pa, cw)


def _branch_a_bwd(pa, cw, dy, name):
    t = pa.shape[0]
    blk = _tile(t, 512)
    nt_ = t // blk
    W = CONV_W
    hb = blk // 8

    def body(p_ref, halo_ref, w_ref, dy_ref, dp_ref, dw_ref, chc_ref, chx_ref):
        i = pl.program_id(0)

        @pl.when(i == 0)
        def _():
            chc_ref[...] = jnp.zeros_like(chc_ref)
            chx_ref[...] = jnp.zeros_like(chx_ref)
            dw_ref[...] = jnp.zeros_like(dw_ref)

        keep = 1.0 - (i == nt_ - 1).astype(F32)
        hc = halo_ref[:, W:2 * W] * keep
        hx = halo_ref[:, 2 * W:3 * W] * keep
        ac_ext = jnp.concatenate([hc, p_ref[:, W:2 * W]], axis=0)
        ax_ext = jnp.concatenate([hx, p_ref[:, 2 * W:3 * W]], axis=0)
        _, vjp = jax.vjp(_conv_a_block, p_ref[:, 0:W], ac_ext, ax_ext, p_ref[:, 3 * W:4 * W],
                         w_ref[0:1, :], w_ref[1:2, :], w_ref[2:3, :])
        dab, dac, dax, daz, dw0, dw1, dw2 = vjp(dy_ref[...])
        dp_ref[:, 0:W] = dab.astype(BF16)
        dp_ref[:, W:2 * W] = _add_to_tail(dac[8:], chc_ref[...]).astype(BF16)
        dp_ref[:, 2 * W:3 * W] = _add_to_tail(dax[8:], chx_ref[...]).astype(BF16)
        dp_ref[:, 3 * W:4 * W] = daz.astype(BF16)
        chc_ref[...] = dac[:8] * keep
        chx_ref[...] = dax[:8] * keep
        dw_ref[0:1, :] += dw0
        dw_ref[1:2, :] += dw1
        dw_ref[2:3, :] += dw2

    rev = lambda i: (nt_ - 1 - i, 0)
    return pl.pallas_call(
        body, name=name, grid=(nt_,),
        in_specs=[pl.BlockSpec((blk, NA), rev),
                  pl.BlockSpec((8, NA), lambda i: (jnp.maximum((nt_ - 1 - i) * hb - 1, 0), 0)),
                  pl.BlockSpec((3, W), lambda i: (0, 0)),
                  pl.BlockSpec((blk, W), rev)],
        out_specs=[pl.BlockSpec((blk, NA), rev), pl.BlockSpec((3, W), lambda i: (0, 0))],
        out_shape=[jax.ShapeDtypeStruct((t, NA), BF16), jax.ShapeDtypeStruct((3, W), F32)],
        scratch_shapes=[pltpu.VMEM((8, W), F32), pltpu.VMEM((8, W), F32)],
        compiler_params=_cparams("arbitrary"))(pa, pa, cw, dy)


def _block_rows(t, chunks):
    return min(t, chunks * CHUNK)


def _branch_b_fwd(pb, lbs_row, nw, name, exchange=None):
    t = pb.shape[0]
    rows = _block_rows(t, HGRN_BLOCK_CHUNKS)
    nch = t // rows

    def body(p_ref, lb_ref, nw_ref, y_ref, s_ref, st_ref):
        @pl.when(pl.program_id(1) == 0)
        def _():
            st_ref[...] = jnp.zeros_like(st_ref)

        s_ref[0, 0] = st_ref[...]
        y, st1 = _hgrn_block(p_ref[:, 0:HD], p_ref[:, HD:2 * HD], p_ref[:, 2 * HD:3 * HD], p_ref[:, 3 * HD:4 * HD],
                             st_ref[...], lb_ref[...], nw_ref[...])
        y_ref[...] = y.astype(BF16)
        st_ref[...] = st1

    return _call_with_exchange(
        body, name=name, grid=(HGRN_HEADS, nch),
        in_specs=[pl.BlockSpec((rows, 4 * HD), lambda h, i: (i, h)),
                  pl.BlockSpec((1, HD), lambda h, i: (0, h)),
                  pl.BlockSpec((1, HD), lambda h, i: (0, 0))],
        out_specs=[pl.BlockSpec((rows, HD), lambda h, i: (i, h)),
                   pl.BlockSpec((1, 1, HD, HD), lambda h, i: (h, i, 0, 0))],
        out_shape=[jax.ShapeDtypeStruct((t, HGRN_HEADS * HD), BF16),
                   jax.ShapeDtypeStruct((HGRN_HEADS, nch, HD, HD), F32)],
        scratch_shapes=[pltpu.VMEM((HD, HD), F32)],
        args=(pb, lbs_row, nw), exchange=exchange)


def _branch_b_bwd(pb, states, lbs_row, nw, dy, name, exchange=None):
    t = pb.shape[0]
    rows = _block_rows(t, HGRN_BLOCK_CHUNKS)
    nch = t // rows

    def body(p_ref, s_ref, lb_ref, nw_ref, dy_ref, dp_ref, dlb_ref, dnw_ref, ds_ref):
        h, i = pl.program_id(0), pl.program_id(1)

        @pl.when(i == 0)
        def _():
            ds_ref[...] = jnp.zeros_like(ds_ref)
            dlb_ref[...] = jnp.zeros_like(dlb_ref)

        @pl.when((i == 0) & (h == 0))
        def _():
            dnw_ref[...] = jnp.zeros_like(dnw_ref)

        _, vjp = jax.vjp(_hgrn_block, p_ref[:, 0:HD], p_ref[:, HD:2 * HD], p_ref[:, 2 * HD:3 * HD],
                         p_ref[:, 3 * HD:4 * HD], s_ref[0, 0], lb_ref[...], nw_ref[...])
        dq, df, di, dz, ds0, dlb, dnw = vjp((dy_ref[...], ds_ref[...]))
        dp_ref[:, 0:HD] = dq.astype(BF16)
        dp_ref[:, HD:2 * HD] = df.astype(BF16)
        dp_ref[:, 2 * HD:3 * HD] = di.astype(BF16)
        dp_ref[:, 3 * HD:4 * HD] = dz.astype(BF16)
        ds_ref[...] = ds0
        dlb_ref[...] += dlb
        dnw_ref[...] += dnw

    rev = lambda h, i: (nch - 1 - i, h)
    return _call_with_exchange(
        body, name=name, grid=(HGRN_HEADS, nch),
        in_specs=[pl.BlockSpec((rows, 4 * HD), rev),
                  pl.BlockSpec((1, 1, HD, HD), lambda h, i: (h, nch - 1 - i, 0, 0)),
                  pl.BlockSpec((1, HD), lambda h, i: (0, h)),
                  pl.BlockSpec((1, HD), lambda h, i: (0, 0)),
                  pl.BlockSpec((rows, HD), rev)],
        out_specs=[pl.BlockSpec((rows, 4 * HD), rev),
                   pl.BlockSpec((1, HD), lambda h, i: (0, h)),
                   pl.BlockSpec((1, HD), lambda h, i: (0, 0))],
        out_shape=[jax.ShapeDtypeStruct((t, NB), BF16), jax.ShapeDtypeStruct((1, HGRN_HEADS * HD), F32),
                   jax.ShapeDtypeStruct((1, HD), F32)],
        scratch_shapes=[pltpu.VMEM((HD, HD), F32)],
        args=(pb, states, lbs_row, nw, dy), exchange=exchange)


def _branch_c_fwd(pc, cw, cpar, nw, name, exchange=None):
    t = pc.shape[0]
    rows = _block_rows(t, GDN_BLOCK_CHUNKS)
    nch = t // rows
    XW = 4 * HD

    nsys = 2 * rows // GROUP

    def body(p_ref, w_ref, cp_ref, nw_ref, y_ref, s_ref, x_ref, sa_ref, sb_ref, halo_ref):
        @pl.when(pl.program_id(1) == 0)
        def _():
            sa_ref[...] = jnp.zeros_like(sa_ref)
            sb_ref[...] = jnp.zeros_like(sb_ref)
            halo_ref[...] = jnp.zeros_like(halo_ref)

        s_ref[0, 0, 0] = sa_ref[...]
        s_ref[0, 0, 1] = sb_ref[...]
        x_ext = jnp.concatenate([halo_ref[...], p_ref[:, 0:XW]], axis=0)
        (y, s1a, s1b), xs = _gdn_block(x_ext, p_ref[:, XW:XW + 2 * HD], p_ref[:, XW + 2 * HD:XW + 3 * HD],
                                       sa_ref[...], sb_ref[...], w_ref[0:1, :], w_ref[1:2, :], w_ref[2:3, :], w_ref[3:4, :],
                                       cp_ref[0, 0:1, :], cp_ref[0, 1:2, :], nw_ref[...])
        for n in range(nsys):
            x_ref[0, 0, n] = xs[n]
        y_ref[...] = y.astype(BF16)
        sa_ref[...] = s1a
        sb_ref[...] = s1b
        halo_ref[...] = p_ref[rows - 8:rows, 0:XW]

    return _call_with_exchange(
        body, name=name, grid=(GDN_QK_HEADS, nch),
        in_specs=[pl.BlockSpec((rows, C_HEAD), lambda h, i: (i, h)),
                  pl.BlockSpec((4, XW), lambda h, i: (0, h)),
                  pl.BlockSpec((1, 8, HD), lambda h, i: (h, 0, 0)),
                  pl.BlockSpec((1, HD), lambda h, i: (0, 0))],
        out_specs=[pl.BlockSpec((rows, 2 * HD), lambda h, i: (i, h)),
                   pl.BlockSpec((1, 1, 2, HD, HD), lambda h, i: (h, i, 0, 0, 0)),
                   pl.BlockSpec((1, 1, nsys, GROUP, GROUP), lambda h, i: (h, i, 0, 0, 0))],
        out_shape=[jax.ShapeDtypeStruct((t, 2 * GDN_QK_HEADS * HD), BF16),
                   jax.ShapeDtypeStruct((GDN_QK_HEADS, nch, 2, HD, HD), F32),
                   jax.ShapeDtypeStruct((GDN_QK_HEADS, nch, nsys, GROUP, GROUP), BF16)],
        scratch_shapes=[pltpu.VMEM((HD, HD), F32), pltpu.VMEM((HD, HD), F32), pltpu.VMEM((8, XW), F32)],
        args=(pc, cw, cpar, nw), exchange=exchange)


def _branch_c_bwd(pc, states, inverses, cw, cpar, nw, dy, name, exchange=None):
    t = pc.shape[0]
    rows = _block_rows(t, GDN_BLOCK_CHUNKS)
    nch = t // rows
    XW = 4 * HD
    hb = rows // 8

    nsys = 2 * rows // GROUP

    def body(p_ref, halo_ref, s_ref, x_ref, w_ref, cp_ref, nw_ref, dy_ref, dp_ref, dw_ref, dcp_ref, dnw_ref,
             dsa_ref, dsb_ref, carry_ref):
        h, i = pl.program_id(0), pl.program_id(1)

        @pl.when(i == 0)
        def _():
            dsa_ref[...] = jnp.zeros_like(dsa_ref)
            dsb_ref[...] = jnp.zeros_like(dsb_ref)
            carry_ref[...] = jnp.zeros_like(carry_ref)
            dw_ref[...] = jnp.zeros_like(dw_ref)
            dcp_ref[...] = jnp.zeros_like(dcp_ref)

        @pl.when((i == 0) & (h == 0))
        def _():
            dnw_ref[...] = jnp.zeros_like(dnw_ref)

        keep = 1.0 - (i == nch - 1).astype(F32)
        x_ext = jnp.concatenate([halo_ref[:, 0:XW] * keep, p_ref[:, 0:XW]], axis=0)
        block = functools.partial(_gdn_block, known=tuple(x_ref[0, 0, n] for n in range(nsys)))
        _, vjp, _ = jax.vjp(block, x_ext, p_ref[:, XW:XW + 2 * HD], p_ref[:, XW + 2 * HD:XW + 3 * HD],
                            s_ref[0, 0, 0], s_ref[0, 0, 1], w_ref[0:1, :], w_ref[1:2, :], w_ref[2:3, :], w_ref[3:4, :],
                            cp_ref[0, 0:1, :], cp_ref[0, 1:2, :], nw_ref[...], has_aux=True)
        dx, dz, dba, dsa, dsb, dw0, dw1, dw2, dw3, dal, ddt, dnw = vjp((dy_ref[...], dsa_ref[...], dsb_ref[...]))
        dp_ref[:, 0:XW] = _add_to_tail(dx[8:], carry_ref[...]).astype(BF16)
        dp_ref[:, XW:XW + 2 * HD] = dz.astype(BF16)
        dp_ref[:, XW + 2 * HD:XW + 3 * HD] = dba.astype(BF16)
        carry_ref[...] = dx[:8] * keep
        dsa_ref[...] = dsa
        dsb_ref[...] = dsb
        dw_ref[0:1, :] += dw0
        dw_ref[1:2, :] += dw1
        dw_ref[2:3, :] += dw2
        dw_ref[3:4, :] += dw3
        dcp_ref[0, 0:1, :] += dal
        dcp_ref[0, 1:2, :] += ddt
        dnw_ref[...] += dnw

    rev = lambda h, i: (nch - 1 - i, h)
    return _call_with_exchange(
        body, name=name, grid=(GDN_QK_HEADS, nch),
        in_specs=[pl.BlockSpec((rows, C_HEAD), rev),
                  pl.BlockSpec((8, C_HEAD), lambda h, i: (jnp.maximum((nch - 1 - i) * hb - 1, 0), h)),
                  pl.BlockSpec((1, 1, 2, HD, HD), lambda h, i: (h, nch - 1 - i, 0, 0, 0)),
                  pl.BlockSpec((1, 1, nsys, GROUP, GROUP), lambda h, i: (h, nch - 1 - i, 0, 0, 0)),
                  pl.BlockSpec((4, XW), lambda h, i: (0, h)),
                  pl.BlockSpec((1, 8, HD), lambda h, i: (h, 0, 0)),
                  pl.BlockSpec((1, HD), lambda h, i: (0, 0)),
                  pl.BlockSpec((rows, 2 * HD), rev)],
        out_specs=[pl.BlockSpec((rows, C_HEAD), rev),
                   pl.BlockSpec((4, XW), lambda h, i: (0, h)),
                   pl.BlockSpec((1, 8, HD), lambda h, i: (h, 0, 0)),
                   pl.BlockSpec((1, HD), lambda h, i: (0, 0))],
        out_shape=[jax.ShapeDtypeStruct((t, NC_COLS), BF16), jax.ShapeDtypeStruct((4, GDN_QK_HEADS * XW), F32),
                   jax.ShapeDtypeStruct((GDN_QK_HEADS, 8, HD), F32), jax.ShapeDtypeStruct((1, HD), F32)],
        scratch_shapes=[pltpu.VMEM((HD, HD), F32), pltpu.VMEM((HD, HD), F32), pltpu.VMEM((8, XW), F32)],
        args=(pc, pc, states, inverses, cw, cpar, nw, dy), exchange=exchange)


def _merge_fwd(pg, bg, ya, yb, yc, name):
    t = pg.shape[0]
    blk = _tile(t, 256)

    def body(g_ref, b_ref, a_ref, b2_ref, c_ref, o_ref):
        gate = _sigmoid(g_ref[...] + b_ref[...])
        o_ref[...] = (gate[:, 0:D] * a_ref[...] + gate[:, D:2 * D] * b2_ref[...] + gate[:, 2 * D:3 * D] * c_ref[...]).astype(BF16)

    row = pl.BlockSpec((blk, D), lambda i: (i, 0))
    return pl.pallas_call(
        body, name=name, grid=(t // blk,),
        in_specs=[pl.BlockSpec((blk, NG), lambda i: (i, 0)), pl.BlockSpec((1, NG), lambda i: (0, 0)), row, row, row],
        out_specs=row, out_shape=jax.ShapeDtypeStruct((t, D), BF16),
        compiler_params=_cparams("parallel"))(pg, bg, ya, yb, yc)


def _merge_bwd(dm, pg, bg, ya, yb, yc, name):
    t = pg.shape[0]
    blk = _tile(t, 256)

    def body(dm_ref, g_ref, b_ref, a_ref, b2_ref, c_ref, dg_ref, da_ref, db_ref, dc_ref, dbg_ref):
        @pl.when(pl.program_id(0) == 0)
        def _():
            dbg_ref[...] = jnp.zeros_like(dbg_ref)

        gate = _sigmoid(g_ref[...] + b_ref[...])
        dmv = dm_ref[...]
        for j, (y_ref, dy_ref) in enumerate(((a_ref, da_ref), (b2_ref, db_ref), (c_ref, dc_ref))):
            gj = gate[:, j * D:(j + 1) * D]
            dy_ref[...] = (dmv * gj).astype(BF16)
            dgj = dmv * y_ref[...] * gj * (1.0 - gj)
            dg_ref[:, j * D:(j + 1) * D] = dgj.astype(BF16)
            dbg_ref[:, j * D:(j + 1) * D] += jnp.sum(dgj, axis=0, keepdims=True)

    row = pl.BlockSpec((blk, D), lambda i: (i, 0))
    wide = pl.BlockSpec((blk, NG), lambda i: (i, 0))
    vec = pl.BlockSpec((1, NG), lambda i: (0, 0))
    return pl.pallas_call(
        body, name=name, grid=(t // blk,), in_specs=[row, wide, vec, row, row, row],
        out_specs=[wide, row, row, row, vec],
        out_shape=[jax.ShapeDtypeStruct((t, NG), BF16)] + [jax.ShapeDtypeStruct((t, D), BF16)] * 3
                  + [jax.ShapeDtypeStruct((1, NG), F32)],
        compiler_params=_cparams("arbitrary"))(dm, pg, bg, ya, yb, yc)


def _adamw_math(w, g, m, v):
    m = ADAM_B1 * m + (1.0 - ADAM_B1) * g
    v = ADAM_B2 * v + (1.0 - ADAM_B2) * (g * g)
    m_hat = m / (1.0 - ADAM_B1 ** ADAM_STEP)
    v_hat = v / (1.0 - ADAM_B2 ** ADAM_STEP)
    delta = -ADAM_LR * (m_hat / (jnp.sqrt(v_hat) + ADAM_EPS) + ADAM_WD * w)
    return delta, m, v


def _sum_adamw(parts, w, m, v, name):
    r, c = w.shape
    br = r if r <= 256 else 256
    assert r % br == 0

    def body(p_ref, w_ref, m_ref, v_ref, g_ref, d_ref, nm_ref, nv_ref):
        g = p_ref[0].astype(F32)
        for k in range(1, N_DEV):
            g = g + p_ref[k].astype(F32)
        g_ref[...] = g
        d_ref[...], nm_ref[...], nv_ref[...] = _adamw_math(w_ref[...], g, m_ref[...], v_ref[...])

    blk = pl.BlockSpec((br, c), lambda i: (i, 0))
    return pl.pallas_call(
        body, name=name, grid=(r // br,),
        in_specs=[pl.BlockSpec((N_DEV, br, c), lambda i: (0, i, 0)), blk, blk, blk], out_specs=[blk] * 4,
        out_shape=[jax.ShapeDtypeStruct((r, c), F32)] * 4, compiler_params=_cparams("parallel"))(parts, w, m, v)


def _adamw(g, w, m, v, name):
    def body(g_ref, w_ref, m_ref, v_ref, d_ref, nm_ref, nv_ref):
        d_ref[...], nm_ref[...], nv_ref[...] = _adamw_math(w_ref[...], g_ref[...], m_ref[...], v_ref[...])

    return pl.pallas_call(body, name=name, out_shape=[jax.ShapeDtypeStruct(w.shape, F32)] * 3)(g, w, m, v)


def _sum_slots(parts, name):
    def body(p_ref, o_ref):
        g = p_ref[0]
        for k in range(1, N_DEV):
            g = g + p_ref[k]
        o_ref[...] = g

    return pl.pallas_call(body, name=name, out_shape=jax.ShapeDtypeStruct(parts.shape[1:], F32))(parts)


def _exchange(srcs, name, broadcast):
    n = len(srcs)

    def body(*refs):
        copies = _exchange_copies(refs[:n], refs[n:2 * n], *refs[2 * n:], broadcast)
        for cp in copies:
            cp.start()
        for cp in copies:
            cp.wait()

    return pl.pallas_call(
        body, name=name, in_specs=[HBM_SPEC] * n, out_specs=[HBM_SPEC] * n, out_shape=_exchange_shapes(srcs, broadcast),
        scratch_shapes=_exchange_semaphores(n))(*srcs)


def _gather_two_level(srcs, name):
    n = len(srcs)

    def body(*refs):
        src_refs, dst_refs = refs[:n], refs[n:2 * n]
        send_sems, recv_sems, local_sems = refs[2 * n:]
        x, y, c = lax.axis_index("x"), lax.axis_index("y"), lax.axis_index("c")
        index_of = lambda px, py, pc: 4 * px + 2 * py + pc
        me, other_core = index_of(x, y, c), (x, y, 1 - c)
        chips = [(1 - x, y), (x, 1 - y), (1 - x, 1 - y)]

        def copy(k, a, block, to, src=None):
            return pltpu.make_async_remote_copy(
                src_ref=dst_refs[a].at[block] if src is None else src, dst_ref=dst_refs[a].at[block],
                send_sem=send_sems.at[k, a], recv_sem=recv_sems.at[k, a], device_id=to, device_id_type=MESH)

        local = [pltpu.make_async_copy(src_refs[a], dst_refs[a].at[me], local_sems.at[a]) for a in range(n)]
        first = [copy(0, a, me, other_core, src=src_refs[a]) for a in range(n)]
        first += [copy(1 + j, a, me, (*chip, c), src=src_refs[a]) for j, chip in enumerate(chips) for a in range(n)]
        for cp in local + first:
            cp.start()
        passed = []
        for j, chip in enumerate(chips):
            block = index_of(*chip, c)
            for a in range(n):
                copy(1 + j, a, block, (x, y, c)).wait_recv()
            for a in range(n):
                passed.append(copy(4 + j, a, block, other_core))
                passed[-1].start()
        for a in range(n):
            copy(0, a, index_of(x, y, 1 - c), (x, y, c)).wait_recv()
        for j, chip in enumerate(chips):
            for a in range(n):
                copy(4 + j, a, index_of(*chip, 1 - c), (x, y, c)).wait_recv()
        for cp in first + passed:
            cp.wait_send()
        for cp in local:
            cp.wait()

    return pl.pallas_call(
        body, name=name, in_specs=[HBM_SPEC] * n, out_specs=[HBM_SPEC] * n, out_shape=_exchange_shapes(srcs, True),
        scratch_shapes=_exchange_semaphores(n))(*srcs)


HBM_SPEC = pl.BlockSpec(memory_space=pltpu.HBM)


def _exchange_shapes(srcs, broadcast):
    return [jax.ShapeDtypeStruct((N_DEV,) + (s.shape if broadcast else s.shape[1:]), s.dtype) for s in srcs]


def _exchange_semaphores(n):
    return [pltpu.SemaphoreType.DMA((N_DEV - 1, n)), pltpu.SemaphoreType.DMA((N_DEV - 1, n)), pltpu.SemaphoreType.DMA((n,))]


def _exchange_copies(src_refs, dst_refs, send_sems, recv_sems, local_sems, broadcast):
    x, y, c = lax.axis_index("x"), lax.axis_index("y"), lax.axis_index("c")
    me = 4 * x + 2 * y + c
    copies = []
    for k in range(1, N_DEV):
        px = 1 - x if (k >> 2) & 1 else x
        py = 1 - y if (k >> 1) & 1 else y
        pc = 1 - c if k & 1 else c
        peer = 4 * px + 2 * py + pc
        for a, (src, dst) in enumerate(zip(src_refs, dst_refs)):
            copies.append(pltpu.make_async_remote_copy(
                src_ref=src if broadcast else src.at[peer], dst_ref=dst.at[me],
                send_sem=send_sems.at[k - 1, a], recv_sem=recv_sems.at[k - 1, a],
                device_id=(px, py, pc), device_id_type=MESH))
    for a, (src, dst) in enumerate(zip(src_refs, dst_refs)):
        copies.append(pltpu.make_async_copy(src if broadcast else src.at[me], dst.at[me], local_sems.at[a]))
    return copies


def _call_with_exchange(body, *, name, grid, in_specs, out_specs, out_shape, scratch_shapes, args, exchange):
    if exchange is None:
        outs = pl.pallas_call(body, name=name, grid=grid, in_specs=in_specs, out_specs=out_specs, out_shape=out_shape,
                              scratch_shapes=scratch_shapes,
                              compiler_params=_cparams(*["arbitrary"] * len(grid)))(*args)
        return outs, None
    srcs, broadcast = exchange
    n, n_in, n_out, n_scr = len(srcs), len(args), len(out_shape), len(scratch_shapes)
    steps = 1
    for g in grid:
        steps *= g

    def hosted(*refs):
        ins, src_refs = refs[:n_in], refs[n_in:n_in + n]
        outs, dst_refs = refs[n_in + n:n_in + n + n_out], refs[n_in + n + n_out:n_in + 2 * n + n_out]
        scratch = refs[n_in + 2 * n + n_out:]
        step = pl.program_id(0)
        for axis in range(1, len(grid)):
            step = step * grid[axis] + pl.program_id(axis)

        @pl.when(step == 0)
        def _():
            for cp in _exchange_copies(src_refs, dst_refs, *scratch[n_scr:], broadcast):
                cp.start()

        body(*ins, *outs, *scratch[:n_scr])

        @pl.when(step == steps - 1)
        def _():
            for cp in _exchange_copies(src_refs, dst_refs, *scratch[n_scr:], broadcast):
                cp.wait()

    outs = pl.pallas_call(
        hosted, name=name, grid=grid, in_specs=list(in_specs) + [HBM_SPEC] * n, out_specs=list(out_specs) + [HBM_SPEC] * n,
        out_shape=list(out_shape) + _exchange_shapes(srcs, broadcast),
        scratch_shapes=list(scratch_shapes) + _exchange_semaphores(n),
        compiler_params=_cparams(*["arbitrary"] * len(grid)))(*args, *srcs)
    return outs[:n_out], outs[n_out:]


def _regroup_w_in(w):
    wa = w[:, OFF_A:OFF_A + NA]
    seg = lambda off, h, n=HD: w[:, off + h * n: off + (h + 1) * n]
    wb = jnp.concatenate([seg(OFF_B + s * 512, h) for h in range(HGRN_HEADS) for s in range(4)], axis=1)
    parts = []
    for h in range(GDN_QK_HEADS):
        small = jnp.concatenate(
            [w[:, OFF_BETA + 2 * h: OFF_BETA + 2 * h + 2], w[:, OFF_CA + 2 * h: OFF_CA + 2 * h + 2],
             jnp.zeros((w.shape[0], HD - 4), w.dtype)], axis=1)
        parts += [seg(OFF_CQ, h), seg(OFF_CK, h), seg(OFF_CV, h, 2 * HD), seg(OFF_CZ, h, 2 * HD), small]
    wc = jnp.concatenate(parts, axis=1)
    wg = w[:, OFF_G:OFF_G + NG]
    return wa, wb, wc, wg


def _ungroup_dw_in(da, db, dc, dg):
    bq = [jnp.concatenate([db[:, h * 512 + s * HD: h * 512 + (s + 1) * HD] for h in range(HGRN_HEADS)], axis=1)
          for s in range(4)]
    ch = lambda h, lo, hi: dc[:, h * C_HEAD + lo: h * C_HEAD + hi]
    heads = range(GDN_QK_HEADS)
    cq = jnp.concatenate([ch(h, 0, HD) for h in heads], axis=1)
    ck = jnp.concatenate([ch(h, HD, 2 * HD) for h in heads], axis=1)
    cv = jnp.concatenate([ch(h, 2 * HD, 4 * HD) for h in heads], axis=1)
    cz = jnp.concatenate([ch(h, 4 * HD, 6 * HD) for h in heads], axis=1)
    cbeta = jnp.concatenate([ch(h, 6 * HD, 6 * HD + 2) for h in heads], axis=1)
    ca = jnp.concatenate([ch(h, 6 * HD + 2, 6 * HD + 4) for h in heads], axis=1)
    return jnp.concatenate([da] + bq + [cq, ck, cv, cbeta, ca, cz, dg], axis=1)


def _regroup_conv_c(cw):
    parts = []
    for h in range(GDN_QK_HEADS):
        parts += [cw[:, h * HD:(h + 1) * HD], cw[:, 512 + h * HD: 512 + (h + 1) * HD],
                  cw[:, 1024 + 2 * h * HD: 1024 + (2 * h + 2) * HD]]
    return jnp.concatenate(parts, axis=1)


def _ungroup_conv_c(d):
    heads = range(GDN_QK_HEADS)
    q = jnp.concatenate([d[:, h * 512: h * 512 + HD] for h in heads], axis=1)
    k = jnp.concatenate([d[:, h * 512 + HD: h * 512 + 2 * HD] for h in heads], axis=1)
    v = jnp.concatenate([d[:, h * 512 + 2 * HD: h * 512 + 4 * HD] for h in heads], axis=1)
    return jnp.concatenate([q, k, v], axis=1)


def _numel(shape):
    n = 1
    for d in shape:
        n *= d
    return n


def _pack(arrays, rows):
    flat = jnp.concatenate([a.reshape(-1) for a in arrays])
    return jnp.pad(flat, (0, rows * 128 - flat.shape[0])).reshape(rows, 128)


def _unpack(packed, shapes):
    flat = packed.reshape(-1)
    out, off = [], 0
    for s in shapes:
        out.append(flat[off:off + _numel(s)].reshape(s))
        off += _numel(s)
    return out


def _rows_for(shapes):
    return -(-sum(_numel(s) for s in shapes) // 1024) * 8


def kernel(x, norm_w, w_in, b_gate, conv_a, conv_c, a_log, dt_bias, lower_bounds, hgrn_norm_w, gdn_norm_w, w_out_a, w_out_b, w_out_c, w_o, final_norm_w, loss_target, m_norm_w, m_w_in, m_b_gate, m_conv_a, m_conv_c, m_a_log, m_dt_bias, m_lower_bounds, m_hgrn_norm_w, m_gdn_norm_w, m_w_out_a, m_w_out_b, m_w_out_c, m_w_o, m_final_norm_w, v_norm_w, v_w_in, v_b_gate, v_conv_a, v_conv_c, v_a_log, v_dt_bias, v_lower_bounds, v_hgrn_norm_w, v_gdn_norm_w, v_w_out_a, v_w_out_b, v_w_out_c, v_w_o, v_final_norm_w):
    me = 4 * lax.axis_index("x") + 2 * lax.axis_index("y") + lax.axis_index("c")
    xs = x[0]
    target = loss_target[0]
    in_shard = w_in.shape[2]

    big = [w_in, w_out_a, w_out_b, w_out_c, w_o]
    shards_of = lambda l: [w[l].astype(BF16) for w in big]
    conv_shapes = [(DEPTH, 3, CONV_W), (DEPTH, 4, 2048)]
    conv_rows = _rows_for(conv_shapes)
    ca_full = lax.dynamic_update_slice(jnp.zeros(conv_shapes[0], F32), conv_a, (0, 0, me * conv_a.shape[2]))
    cc_full = lax.dynamic_update_slice(jnp.zeros(conv_shapes[1], F32), conv_c, (0, 0, me * conv_c.shape[2]))
    conv_parts, = _exchange([_pack([ca_full, cc_full], conv_rows)], "gather_conv", broadcast=True)
    conv_a_full, conv_c_full = _unpack(_sum_slots(conv_parts, "sum_conv"), conv_shapes)

    lb_pad = jnp.pad(lower_bounds, ((0, 8 - DEPTH), (0, 0)))
    lbs = _lower_bounds_fwd(lb_pad, "lower_bounds_fwd")

    def input_weights(l, g_in):
        wa, wb, wc, wg = _regroup_w_in(jnp.concatenate([g_in[q] for q in range(N_DEV)], axis=1))
        lanes = lambda vec: jnp.pad(vec.reshape(GDN_QK_HEADS, 1, 2), ((0, 0), (0, 0), (0, HD - 2)))
        cpar = jnp.concatenate([lanes(a_log[l]), lanes(dt_bias[l]), jnp.zeros((GDN_QK_HEADS, 6, HD), F32)], axis=1)
        return dict(
            wa=wa, wb=wb, wc=wc, wg=wg, cpar=cpar,
            nw=norm_w[l:l + 1], bg=b_gate[l:l + 1], cwa=conv_a_full[l], cwc=_regroup_conv_c(conv_c_full[l]),
            lb=lbs[l:l + 1], hnw=hgrn_norm_w[l:l + 1], gnw=gdn_norm_w[l:l + 1])

    def output_weights(g_oa, g_ob, g_oc, g_o):
        return dict(woa=jnp.concatenate([g_oa[q] for q in range(N_DEV)], axis=1),
                    wob=jnp.concatenate([g_ob[q] for q in range(N_DEV)], axis=1), woc=g_oc.reshape(D, D), wo=g_o.reshape(D, D))

    layers = [input_weights(0, _gather_two_level(shards_of(0)[:1], "gather_l0")[0])]

    saved = []
    cur = xs
    for l in range(DEPTH):
        L = layers[l]
        n = f"l{l}_"
        h = _rmsnorm_fwd(cur, L["nw"], n + "rms")
        pa = _matmul(h, L["wa"], "nn", n + "proj_a")
        pb = _matmul(h, L["wb"], "nn", n + "proj_b")
        pc = _matmul(h, L["wc"], "nn", n + "proj_c")
        pg = _matmul(h, L["wg"], "nn", n + "proj_g")
        ua = _branch_a_fwd(pa, L["cwa"], n + "conv_fwd")
        carry = (shards_of(l)[1:], True) if l == 0 else None
        (ub, sb), gathered = _branch_b_fwd(pb, L["lb"], L["hnw"], n + "hgrn_fwd", exchange=carry)
        if carry is not None:
            L.update(output_weights(*gathered))
        carry = (shards_of(l + 1), True) if l + 1 < DEPTH else None
        (uc, sc, xc), gathered = _branch_c_fwd(pc, L["cwc"], L["cpar"], L["gnw"], n + "gdn_fwd", exchange=carry)
        if carry is not None:
            layers.append(dict(input_weights(l + 1, gathered[0]), **output_weights(*gathered[1:])))
        ya = _matmul(ua, L["woa"], "nn", n + "out_a")
        yb = _matmul(ub, L["wob"], "nn", n + "out_b")
        yc = _matmul(uc, L["woc"], "nn", n + "out_c")
        merged = _merge_fwd(pg, L["bg"], ya, yb, yc, n + "merge")
        nxt = _matmul(merged, L["wo"], "nn", n + "out_o", residual=cur)
        saved.append(dict(x=cur, h=h, pa=pa, pb=pb, pc=pc, pg=pg, ua=ua, ub=ub, uc=uc, sb=sb, sc=sc, xc=xc,
                          ya=ya, yb=yb, yc=yc, merged=merged))
        cur = nxt

    loss_part, dx, d_final = _loss_head(cur, final_norm_w.reshape(1, D), target, "loss_head")

    def outgoing(g):
        cols = lambda a, n: jnp.stack([a[:, p * n:(p + 1) * n] for p in range(N_DEV)]).astype(BF16)
        rows = lambda a: a.reshape(N_DEV, a.shape[0] // N_DEV, a.shape[1]).astype(BF16)
        first = [cols(g["w_in"], in_shard)] if "w_in" in g else [None]
        return first + [cols(g["w_out_a"], 128), cols(g["w_out_b"], 128), rows(g["w_out_c"]), rows(g["w_o"])]

    grads = [None] * DEPTH
    dlbs_rows = [None] * DEPTH
    incoming = [None] * DEPTH
    for l in reversed(range(DEPTH)):
        L, S = layers[l], saved[l]
        n = f"l{l}_"
        dmerged = _matmul(dx, L["wo"], "nt", n + "d_merged")
        d_wo = _matmul(S["merged"], dx, "tn", n + "dw_o", out_dtype=BF16)
        dpg, dya, dyb, dyc, d_bg = _merge_bwd(dmerged, S["pg"], L["bg"], S["ya"], S["yb"], S["yc"], n + "merge_bwd")
        dua = _matmul(dya, L["woa"], "nt", n + "d_ua")
        dub = _matmul(dyb, L["wob"], "nt", n + "d_ub")
        duc = _matmul(dyc, L["woc"], "nt", n + "d_uc")
        d_woa = _matmul(S["ua"], dya, "tn", n + "dw_out_a", out_dtype=BF16)
        d_wob = _matmul(S["ub"], dyb, "tn", n + "dw_out_b", out_dtype=BF16)
        d_woc = _matmul(S["uc"], dyc, "tn", n + "dw_out_c", out_dtype=BF16)
        dpa, d_cwa = _branch_a_bwd(S["pa"], L["cwa"], dua, n + "conv_bwd")
        out_grads = dict(w_out_a=d_woa, w_out_b=d_wob, w_out_c=d_woc, w_o=d_wo)
        carry = (outgoing(out_grads)[1:], False) if l == 0 else None
        (dpb, d_lb, d_hnw), arrived_out = _branch_b_bwd(S["pb"], S["sb"], L["lb"], L["hnw"], dub, n + "hgrn_bwd", exchange=carry)
        carry = (outgoing(grads[l + 1]), False) if l + 1 < DEPTH else None
        (dpc, d_cwc, d_cpar, d_gnw), arrived = _branch_c_bwd(S["pc"], S["sc"], S["xc"], L["cwc"], L["cpar"], L["gnw"], duc,
                                                             n + "gdn_bwd", exchange=carry)
        if carry is not None:
            incoming[l + 1] = arrived
        dh = _matmul_nt_sum([(dpa, L["wa"]), (dpb, L["wb"]), (dpc, L["wc"]), (dpg, L["wg"])], n + "dh")
        d_win = _ungroup_dw_in(*[_matmul(S["h"], dp, "tn", n + "dw_" + tag, out_dtype=BF16)
                                 for dp, tag in ((dpa, "a"), (dpb, "b"), (dpc, "c"), (dpg, "g"))])
        dx, d_nw = _rmsnorm_bwd(dh, S["x"], L["nw"], dx, n + "rms_bwd")
        dlbs_rows[l] = d_lb
        grads[l] = dict(w_in=d_win, w_out_a=d_woa, w_out_b=d_wob, w_out_c=d_woc, w_o=d_wo, norm_w=d_nw[0],
                        b_gate=d_bg[0], conv_a=d_cwa, conv_c=_ungroup_conv_c(d_cwc),
                        a_log=d_cpar[:, 0, 0:2].reshape(-1), dt_bias=d_cpar[:, 1, 0:2].reshape(-1),
                        hgrn_norm_w=d_hnw[0], gdn_norm_w=d_gnw[0])
    grad_x = dx[None]
    d_lower = _lower_bounds_bwd(lb_pad, jnp.pad(jnp.concatenate(dlbs_rows, axis=0), ((0, 8 - DEPTH), (0, 0))),
                                "lower_bounds_bwd")[:DEPTH]

    incoming[0] = _exchange(outgoing(grads[0])[:1], "exchange_grads_l0", broadcast=False) + arrived_out
    stack = lambda name: jnp.stack([grads[l][name] for l in range(DEPTH)])
    big_out = {}
    for j, (name, w, m, v) in enumerate((("w_in", w_in, m_w_in, v_w_in), ("w_out_a", w_out_a, m_w_out_a, v_w_out_a),
                                         ("w_out_b", w_out_b, m_w_out_b, v_w_out_b), ("w_out_c", w_out_c, m_w_out_c, v_w_out_c),
                                         ("w_o", w_o, m_w_o, v_w_o))):
        parts = jnp.concatenate([incoming[l][j] for l in range(DEPTH)], axis=1)
        r2 = lambda a: a.reshape(parts.shape[1], parts.shape[2])
        outs = _sum_adamw(parts, r2(w), r2(m), r2(v), "adamw_" + name)
        big_out[name] = [o.reshape(w.shape) for o in outs]

    small_names = ["norm_w", "b_gate", "conv_a", "conv_c", "a_log", "dt_bias", "lower_bounds", "hgrn_norm_w",
                   "gdn_norm_w", "final_norm_w", "loss"]
    small_vals = {k: stack(k) for k in ("norm_w", "b_gate", "conv_a", "conv_c", "a_log", "dt_bias", "hgrn_norm_w", "gdn_norm_w")}
    small_vals.update(lower_bounds=d_lower, final_norm_w=d_final[0], loss=loss_part.reshape(1))
    small_shapes = [small_vals[k].shape for k in small_names]
    small_rows = _rows_for(small_shapes)
    small_parts, = _exchange([_pack([small_vals[k] for k in small_names], small_rows)], "exchange_small", broadcast=True)
    total = dict(zip(small_names, _unpack(_sum_slots(small_parts, "sum_small"), small_shapes)))
    loss = total["loss"][0]
    g_conv_a = lax.dynamic_slice(total["conv_a"], (0, 0, me * conv_a.shape[2]), conv_a.shape)
    g_conv_c = lax.dynamic_slice(total["conv_c"], (0, 0, me * conv_c.shape[2]), conv_c.shape)

    small_w = dict(norm_w=(norm_w, m_norm_w, v_norm_w), b_gate=(b_gate, m_b_gate, v_b_gate),
                   conv_a=(conv_a, m_conv_a, v_conv_a), conv_c=(conv_c, m_conv_c, v_conv_c),
                   a_log=(a_log, m_a_log, v_a_log), dt_bias=(dt_bias, m_dt_bias, v_dt_bias),
                   lower_bounds=(lower_bounds, m_lower_bounds, v_lower_bounds),
                   hgrn_norm_w=(hgrn_norm_w, m_hgrn_norm_w, v_hgrn_norm_w), gdn_norm_w=(gdn_norm_w, m_gdn_norm_w, v_gdn_norm_w),
                   final_norm_w=(final_norm_w, m_final_norm_w, v_final_norm_w))
    small_g = dict(total, conv_a=g_conv_a, conv_c=g_conv_c)
    upd_names = small_names[:-1]
    upd_shapes = [small_w[k][0].shape for k in upd_names]
    upd_rows = _rows_for(upd_shapes)
    pk = lambda j: _pack([small_w[k][j] for k in upd_names], upd_rows)
    s_delta, s_m, s_v = _adamw(_pack([small_g[k] for k in upd_names], upd_rows), pk(0), pk(1), pk(2), "adamw_small")
    small_out = {k: [small_g[k], d, mm, vv] for k, d, mm, vv in
                 zip(upd_names, _unpack(s_delta, upd_shapes), _unpack(s_m, upd_shapes), _unpack(s_v, upd_shapes))}

    order = ["norm_w", "w_in", "b_gate", "conv_a", "conv_c", "a_log", "dt_bias", "lower_bounds", "hgrn_norm_w",
             "gdn_norm_w", "w_out_a", "w_out_b", "w_out_c", "w_o", "final_norm_w"]
    res = {**small_out, **big_out}
    outs = [loss, grad_x]
    for j in range(4):
        outs += [res[k][j] for k in order]
    return tuple(outs)
```

```python
import functools

import jax
import jax.numpy as jnp
from jax import lax
from jax.experimental import pallas as pl
from jax.experimental.pallas import tpu as pltpu

F32 = jnp.float32
BF16 = jnp.bfloat16
MESH = pl.DeviceIdType.MESH

N_DEV = 8
D = 1024
DEPTH = 2
CHUNK = 64
HGRN_BLOCK_CHUNKS = 16
GDN_BLOCK_CHUNKS = 8
GROUP = 128
NORM_EPS = 1e-6
L2_EPS = 1e-6
MIN_F = 1e-30
HD = 128
HGRN_HEADS = 4
GDN_QK_HEADS = 4
CONV_W = 512
IN_COLS = 10256
OFF_A, OFF_B, OFF_CQ, OFF_CK, OFF_CV, OFF_BETA, OFF_CA, OFF_CZ, OFF_G = (
    0, 2048, 4096, 4608, 5120, 6144, 6152, 6160, 7184)
NA, NB, NC_COLS, NG = 2048, 2048, 3584, 3072
C_HEAD = 896

ADAM_LR, ADAM_B1, ADAM_B2, ADAM_EPS, ADAM_WD, ADAM_STEP = 0.001, 0.9, 0.999, 1e-08, 0.01, 10

VMEM_LIMIT = 56 * 1024 * 1024
MM_TILE = 1024


def _cparams(*sem):
    return pltpu.CompilerParams(dimension_semantics=sem, vmem_limit_bytes=VMEM_LIMIT)


def _tile(dim, cap):
    if dim <= cap:
        return dim
    t = (cap // 128) * 128
    while dim % t:
        t -= 128
    return t


def _sigmoid(x):
    return 1.0 / (1.0 + jnp.exp(-x))


def _silu(x):
    return x * _sigmoid(x)


def _softplus(x):
    return jnp.maximum(x, 0.0) + jnp.log(1.0 + jnp.exp(-jnp.abs(x)))


def _dot(a, b, dims, precision=None):
    if precision is None:
        a, b = a.astype(BF16), b.astype(BF16)
    return lax.dot_general(a, b, (dims, ((), ())), precision=precision, preferred_element_type=F32)


def _nn(a, b, precision=None):
    return _dot(a, b, ((1,), (0,)), precision)


def _nt(a, b, precision=None):
    return _dot(a, b, ((1,), (1,)), precision)


def _tn(a, b, precision=None):
    return _dot(a, b, ((0,), (0,)), precision)


def _sum_rows_split(mat01, x):
    m = mat01.astype(BF16)
    hi = x.astype(BF16)
    low = (x - hi.astype(F32)).astype(BF16)
    return _nn(m, hi) + _nn(m, low)


@functools.partial(jax.custom_vjp, nondiff_argnums=(1,))
def _shift_rows(x, d):
    return x if d == 0 else pltpu.roll(x, d, 0)


def _shift_rows_fwd(x, d):
    return _shift_rows(x, d), None


def _shift_rows_bwd(d, _, ct):
    return ((ct if d == 0 else pltpu.roll(ct, ct.shape[0] - d, 0)),)


_shift_rows.defvjp(_shift_rows_fwd, _shift_rows_bwd)


def _iota2(shape):
    return lax.broadcasted_iota(jnp.int32, shape, 0), lax.broadcasted_iota(jnp.int32, shape, 1)


def _lane_pick(x, i):
    lane = lax.broadcasted_iota(jnp.int32, x.shape, 1)
    return jnp.sum(jnp.where(lane == i, x, 0.0), axis=1, keepdims=True)


def _hgrn_block(qr, fr, ir, zr, st0, lb, nw):
    rows = qr.shape[0]
    r, c = _iota2((CHUNK, CHUNK))
    halves = [1 << j for j in range(CHUNK.bit_length() - 1)]
    mats = [c <= r, c > r]
    pairs = []
    for hb in halves:
        same = (r // hb) == (c // hb)
        if hb > 1:
            mats += [(c <= r) & same, (c > r) & same]
        pairs.append(((r // (2 * hb)) == (c // (2 * hb))) & ((r // hb) == (c // hb) + 1))
    stack = jnp.concatenate([m.astype(F32) for m in mats], axis=0)

    q = _silu(qr) * (HD ** -0.5)
    fg = lb + (1.0 - lb) * _sigmoid(fr)
    logf = jnp.log(jnp.maximum(fg, MIN_F))
    kk = 1.0 - fg
    v = ir

    chunks = [slice(s, s + CHUNK) for s in range(0, rows, CHUNK)]
    cums = [_sum_rows_split(stack, logf[sl]) for sl in chunks]
    part = lambda i: jnp.concatenate([cs[i * CHUNK:(i + 1) * CHUNK] for cs in cums], axis=0)
    qg = q * jnp.exp(part(0))
    ks = kk * jnp.exp(part(1))
    q_lv = [q * jnp.exp(logf)] + [q * jnp.exp(part(2 * j)) for j in range(1, len(halves))]
    k_lv = [kk] + [kk * jnp.exp(part(2 * j + 1)) for j in range(1, len(halves))]
    st = st0
    outs = []
    for sl in chunks:
        scores = jnp.where(pairs[0], _nt(q_lv[0][sl], k_lv[0][sl]), 0.0)
        for j in range(1, len(halves)):
            scores += jnp.where(pairs[j], _nt(q_lv[j][sl], k_lv[j][sl]), 0.0)
        outs.append(_nn(scores, v[sl]) + _nt(qg[sl], st))
        st = st * jnp.exp(jnp.sum(logf[sl], axis=0, keepdims=True)) + _tn(v[sl], ks[sl])
    o = jnp.concatenate(outs, axis=0) + jnp.sum(q * kk, axis=1, keepdims=True) * v
    y = o * lax.rsqrt(jnp.mean(o * o, axis=1, keepdims=True) + NORM_EPS) * nw * _silu(zr)
    return y, st


def _unit_lower_inverses(ms):
    r, c = _iota2(ms[0].shape)
    xs = [jnp.where(r == c, 1.0, 0.0) - jnp.where((r // 2) == (c // 2), m, 0.0) for m in ms]
    b = 2
    while b < CHUNK:
        pick = ((r // (2 * b)) == (c // (2 * b))) & ((r // b) != (c // b))
        ts = [_nn(x, jnp.where(pick, m, 0.0)) for x, m in zip(xs, ms)]
        xs = [x - _nn(t, x) for x, t in zip(xs, ts)]
        b *= 2
    return tuple(x.astype(BF16) for x in xs)


@jax.custom_vjp
def _known_inverses(ms, xs):
    return xs


def _known_inverses_fwd(ms, xs):
    return xs, xs


def _known_inverses_bwd(xs, cts):
    r, c = _iota2(xs[0].shape)
    keep = (c < r) & ((r // CHUNK) == (c // CHUNK))
    ts = [_tn(x, ct) for x, ct in zip(xs, cts)]
    return (tuple(jnp.where(keep, -_nt(t, x), 0.0) for t, x in zip(ts, xs)), tuple(jnp.zeros_like(x) for x in xs))


_known_inverses.defvjp(_known_inverses_fwd, _known_inverses_bwd)


def _chunk_cumsum(x):
    row = lax.broadcasted_iota(jnp.int32, x.shape, 0) % CHUNK
    d = 1
    while d < CHUNK:
        x = x + jnp.where(row >= d, _shift_rows(x, d), 0.0)
        d *= 2
    return x


def _gdn_block(x_ext, z, ba, s0a, s0b, w0, w1, w2, w3, alog, dtb, nw, known=None):
    rows = z.shape[0]
    conv = (w0 * _shift_rows(x_ext, 3) + w1 * _shift_rows(x_ext, 2) + w2 * _shift_rows(x_ext, 1) + w3 * x_ext)
    cc = _silu(conv[8:])
    qc, kc = cc[:, 0:HD], cc[:, HD:2 * HD]
    q = qc * lax.rsqrt(jnp.sum(qc * qc, axis=1, keepdims=True) + L2_EPS) * (HD ** -0.5)
    k = kc * lax.rsqrt(jnp.sum(kc * kc, axis=1, keepdims=True) + L2_EPS)

    r, c = _iota2((GROUP, GROUP))
    same = (r // CHUNK) == (c // CHUNK)
    causal, strict, eye = same & (c <= r), same & (c < r), r == c
    heads = (0, 1)
    groups = [slice(lo, lo + GROUP) for lo in range(0, rows, GROUP)]
    chunks = [slice(lo, lo + CHUNK) for lo in range(0, rows, CHUNK)]

    v, loga, g_w, kb, kg, qg = [], [], [], [], [], []
    for i in heads:
        v.append(cc[:, (2 + i) * HD:(3 + i) * HD])
        beta = _sigmoid(_lane_pick(ba, i))
        a_neg = -jnp.exp(_lane_pick(alog, i))
        loga.append(a_neg * _softplus(_lane_pick(ba, 2 + i) + _lane_pick(dtb, i)))
        g_w.append(_chunk_cumsum(jnp.broadcast_to(loga[i], (rows, HD))))
        kb.append(k * beta)
        kg.append(k * jnp.exp(g_w[i]))
        qg.append(q * jnp.exp(g_w[i]))

    systems = [(i, gs) for gs in groups for i in heads]
    dec_c, ms = [], []
    for i, gs in systems:
        g_sq = g_w[i][gs]
        g_row = jnp.sum(jnp.where(eye, g_sq, 0.0), axis=0, keepdims=True)
        diff = g_sq - g_row
        dec_c.append(jnp.where(causal, jnp.exp(jnp.where(causal, diff, 0.0)), 0.0))
        ms.append(jnp.where(strict, _nt(k[gs], kb[i][gs]) * dec_c[-1], 0.0))
    xs = _unit_lower_inverses(tuple(ms)) if known is None else _known_inverses(tuple(ms), known)
    u = [[None] * len(groups) for _ in heads]
    w = [[None] * len(groups) for _ in heads]
    qk = [[None] * len(groups) for _ in heads]
    for n, (i, gs) in enumerate(systems):
        j = n // len(heads)
        u[i][j] = _nn(xs[n], v[i][gs])
        w[i][j] = _nn(xs[n], kg[i][gs])
        qk[i][j] = _nt(q[gs], kb[i][gs]) * dec_c[n]
    u = [jnp.concatenate(p, axis=0) for p in u]
    w = [jnp.concatenate(p, axis=0) for p in w]

    decay, p_mat, q_mat = {}, {}, {}
    for n, sl in enumerate(chunks):
        for i in heads:
            g_last = jnp.sum(loga[i][sl], axis=0, keepdims=True)
            kd = kb[i][sl] * jnp.exp(g_last - g_w[i][sl])
            decay[n, i] = jnp.exp(g_last)
            p_mat[n, i] = -_tn(kd, w[i][sl])
            q_mat[n, i] = _tn(kd, u[i][sl])
    s = [s0a, s0b]
    s_at = {}
    for n in range(len(chunks)):
        for i in heads:
            s_at[n, i] = s[i]
            s[i] = s[i] * decay[n, i] + _nn(p_mat[n, i], s[i]) + q_mat[n, i]

    ys = []
    for i in heads:
        e = jnp.concatenate([u[i][sl] - _nn(w[i][sl], s_at[n, i]) for n, sl in enumerate(chunks)], axis=0)
        o_state = jnp.concatenate([_nn(qg[i][sl], s_at[n, i]) for n, sl in enumerate(chunks)], axis=0)
        o = o_state + jnp.concatenate([_nn(qk[i][j], e[gs]) for j, gs in enumerate(groups)], axis=0)
        zi = z[:, i * HD:(i + 1) * HD]
        ys.append(o * lax.rsqrt(jnp.mean(o * o, axis=1, keepdims=True) + NORM_EPS) * nw * _silu(zi))
    return (jnp.concatenate(ys, axis=1), s[0], s[1]), xs


def _add_to_tail(x, tail):
    return x + jnp.concatenate([jnp.zeros((x.shape[0] - 8, x.shape[1]), x.dtype), tail], axis=0)


def _conv_a_block(ab, ac_ext, ax_ext, az, w0, w1, w2):
    u = ac_ext * ax_ext
    conv = (w0 * _shift_rows(u, 2) + w1 * _shift_rows(u, 1) + w2 * u)[8:]
    return ab * conv * _silu(az)


def _matmul(a, b, mode, name, residual=None, out_dtype=F32):
    if mode == "nn":
        (m, k), n = a.shape, b.shape[1]
    elif mode == "nt":
        (m, k), n = a.shape, b.shape[0]
    else:
        (k, m), n = a.shape, b.shape[1]
    tm, tn, tk = _tile(m, MM_TILE), _tile(n, MM_TILE), _tile(k, MM_TILE)
    nk = k // tk
    dims = {"nn": ((1,), (0,)), "nt": ((1,), (1,)), "tn": ((0,), (0,))}[mode]
    a_spec = pl.BlockSpec((tk, tm), lambda i, j, s: (s, i)) if mode == "tn" else pl.BlockSpec((tm, tk), lambda i, j, s: (i, s))
    b_spec = pl.BlockSpec((tn, tk), lambda i, j, s: (j, s)) if mode == "nt" else pl.BlockSpec((tk, tn), lambda i, j, s: (s, j))
    o_spec = pl.BlockSpec((tm, tn), lambda i, j, s: (i, j))
    has_res = residual is not None

    def finish(out, r_ref, o_ref):
        if has_res:
            out = out + r_ref[...]
        o_ref[...] = out.astype(out_dtype)

    def body_one_pass(*refs):
        finish(_dot(refs[0][...], refs[1][...], dims), refs[2] if has_res else None, refs[-1])

    def body_reduce(*refs):
        a_ref, b_ref = refs[0], refs[1]
        r_ref = refs[2] if has_res else None
        o_ref, acc_ref = refs[-2], refs[-1]
        s = pl.program_id(2)

        @pl.when(s == 0)
        def _():
            acc_ref[...] = jnp.zeros_like(acc_ref)

        acc_ref[...] += _dot(a_ref[...], b_ref[...], dims)

        @pl.when(s == nk - 1)
        def _():
            finish(acc_ref[...], r_ref, o_ref)

    args, specs = [a, b], [a_spec, b_spec]
    if has_res:
        args.append(residual)
        specs.append(o_spec)
    return pl.pallas_call(
        body_one_pass if nk == 1 else body_reduce, name=name, grid=(m // tm, n // tn, nk), in_specs=specs, out_specs=o_spec,
        out_shape=jax.ShapeDtypeStruct((m, n), out_dtype),
        scratch_shapes=[] if nk == 1 else [pltpu.VMEM((tm, tn), F32)],
        compiler_params=_cparams("parallel", "parallel", "arbitrary"))(*args)


def _matmul_nt_sum(pairs, name, exchange=None):
    m, n = pairs[0][0].shape[0], pairs[0][1].shape[0]
    tm, tn = _tile(m, MM_TILE), _tile(n, MM_TILE)
    tks = [_tile(a.shape[1], MM_TILE) for a, _ in pairs]
    nks = [a.shape[1] // tk for (a, _), tk in zip(pairs, tks)]
    offs = [sum(nks[:i]) for i in range(len(pairs))]
    total = sum(nks)

    def body(*refs):
        o_ref, acc_ref = refs[-2], refs[-1]
        s = pl.program_id(2)

        @pl.when(s == 0)
        def _():
            acc_ref[...] = jnp.zeros_like(acc_ref)

        for i, (off, nk) in enumerate(zip(offs, nks)):
            @pl.when((s >= off) & (s < off + nk))
            def _(i=i):
                acc_ref[...] += _dot(refs[2 * i][...], refs[2 * i + 1][...], ((1,), (1,)))

        @pl.when(s == total - 1)
        def _():
            o_ref[...] = acc_ref[...]

    args, specs = [], []
    for (a, b), tk, off, nk in zip(pairs, tks, offs, nks):
        k_of = lambda s, off=off, nk=nk: jnp.clip(s - off, 0, nk - 1)
        args += [a, b]
        specs += [pl.BlockSpec((tm, tk), lambda i, j, s, k_of=k_of: (i, k_of(s))),
                  pl.BlockSpec((tn, tk), lambda i, j, s, k_of=k_of: (j, k_of(s)))]
    (out,), exchanged = _call_with_exchange(
        body, name=name, grid=(m // tm, n // tn, total), in_specs=specs,
        out_specs=[pl.BlockSpec((tm, tn), lambda i, j, s: (i, j))], out_shape=[jax.ShapeDtypeStruct((m, n), F32)],
        scratch_shapes=[pltpu.VMEM((tm, tn), F32)], args=args, exchange=exchange)
    return out, exchanged


def _rmsnorm_fwd(x, w, name):
    t = x.shape[0]
    blk = _tile(t, 512)

    def body(x_ref, w_ref, h_ref):
        xv = x_ref[...]
        h_ref[...] = (xv * lax.rsqrt(jnp.mean(xv * xv, axis=1, keepdims=True) + NORM_EPS) * w_ref[...]).astype(BF16)

    return pl.pallas_call(
        body, name=name, grid=(t // blk,),
        in_specs=[pl.BlockSpec((blk, D), lambda i: (i, 0)), pl.BlockSpec((1, D), lambda i: (0, 0))],
        out_specs=pl.BlockSpec((blk, D), lambda i: (i, 0)), out_shape=jax.ShapeDtypeStruct((t, D), BF16),
        compiler_params=_cparams("parallel"))(x, w)


def _rmsnorm_bwd(dh, x, w, dxo, name, exchange=None):
    t = x.shape[0]
    blk = _tile(t, 512)

    def body(dh_ref, x_ref, w_ref, dxo_ref, dx_ref, dw_ref):
        @pl.when(pl.program_id(0) == 0)
        def _():
            dw_ref[...] = jnp.zeros_like(dw_ref)

        xv, dhv = x_ref[...], dh_ref[...]
        rs = lax.rsqrt(jnp.mean(xv * xv, axis=1, keepdims=True) + NORM_EPS)
        xh = xv * rs
        dw_ref[...] += jnp.sum(dhv * xh, axis=0, keepdims=True)
        dxh = dhv * w_ref[...]
        dx_ref[...] = rs * (dxh - xh * jnp.mean(dxh * xh, axis=1, keepdims=True)) + dxo_ref[...]

    row = pl.BlockSpec((blk, D), lambda i: (i, 0))
    vec = pl.BlockSpec((1, D), lambda i: (0, 0))
    return _call_with_exchange(
        body, name=name, grid=(t // blk,), in_specs=[row, row, vec, row], out_specs=[row, vec],
        out_shape=[jax.ShapeDtypeStruct((t, D), F32), jax.ShapeDtypeStruct((1, D), F32)],
        scratch_shapes=[], args=(dh, x, w, dxo), exchange=exchange)


def _loss_head(x, w, target, name):
    t = x.shape[0]
    blk = _tile(t, 512)

    def body(x_ref, w_ref, t_ref, loss_ref, dx_ref, dw_ref):
        @pl.when(pl.program_id(0) == 0)
        def _():
            dw_ref[...] = jnp.zeros_like(dw_ref)
            loss_ref[...] = jnp.zeros_like(loss_ref)

        xv = x_ref[...]
        rs = lax.rsqrt(jnp.mean(xv * xv, axis=1, keepdims=True) + NORM_EPS)
        xh = xv * rs
        err = xh * w_ref[...] - t_ref[...]
        loss_ref[...] += 0.5 * jnp.sum(jnp.mean(err * err, axis=1, keepdims=True), axis=0, keepdims=True)
        dy = err * (1.0 / D)
        dw_ref[...] += jnp.sum(dy * xh, axis=0, keepdims=True)
        dxh = dy * w_ref[...]
        dx_ref[...] = rs * (dxh - xh * jnp.mean(dxh * xh, axis=1, keepdims=True))

    row = pl.BlockSpec((blk, D), lambda i: (i, 0))
    vec = pl.BlockSpec((1, D), lambda i: (0, 0))
    return pl.pallas_call(
        body, name=name, grid=(t // blk,), in_specs=[row, vec, row],
        out_specs=[pl.BlockSpec((1, 1), lambda i: (0, 0)), row, vec],
        out_shape=[jax.ShapeDtypeStruct((1, 1), F32), jax.ShapeDtypeStruct((t, D), F32), jax.ShapeDtypeStruct((1, D), F32)],
        compiler_params=_cparams("arbitrary"))(x, w, target)


def _lbs_of(lb):
    r = lax.broadcasted_iota(jnp.int32, lb.shape, 0)
    real = r < DEPTH
    mx = lax.stop_gradient(jnp.max(jnp.where(real, lb, -jnp.inf), axis=0, keepdims=True))
    e = jnp.where(real, jnp.exp(jnp.where(real, lb - mx, 0.0)), 0.0)
    p = e / jnp.sum(e, axis=0, keepdims=True)
    out = jnp.zeros_like(lb)
    run = jnp.zeros_like(mx)
    for l in range(1, DEPTH):
        run = run + jnp.sum(jnp.where(r == l, p, 0.0), axis=0, keepdims=True)
        out = out + jnp.where(r == l, run, 0.0)
    return out


def _lower_bounds_fwd(lbp, name):
    def body(lb_ref, o_ref):
        o_ref[...] = _lbs_of(lb_ref[...])

    return pl.pallas_call(body, name=name, out_shape=jax.ShapeDtypeStruct(lbp.shape, F32))(lbp)


def _lower_bounds_bwd(lbp, dlbs, name):
    def body(lb_ref, d_ref, o_ref):
        _, vjp = jax.vjp(_lbs_of, lb_ref[...])
        o_ref[...] = vjp(d_ref[...])[0]

    return pl.pallas_call(body, name=name, out_shape=jax.ShapeDtypeStruct(lbp.shape, F32))(lbp, dlbs)


def _branch_a_fwd(pa, cw, name):
    t = pa.shape[0]
    blk = _tile(t, 512)
    W = CONV_W

    def body(p_ref, w_ref, y_ref, hc_ref, hx_ref):
        @pl.when(pl.program_id(0) == 0)
        def _():
            hc_ref[...] = jnp.zeros_like(hc_ref)
            hx_ref[...] = jnp.zeros_like(hx_ref)

        ac, ax = p_ref[:, W:2 * W], p_ref[:, 2 * W:3 * W]
        y_ref[...] = _conv_a_block(
            p_ref[:, 0:W], jnp.concatenate([hc_ref[...], ac], axis=0), jnp.concatenate([hx_ref[...], ax], axis=0),
            p_ref[:, 3 * W:4 * W], w_ref[0:1, :], w_ref[1:2, :], w_ref[2:3, :]).astype(BF16)
        hc_ref[...] = p_ref[blk - 8:blk, W:2 * W]
        hx_ref[...] = p_ref[blk - 8:blk, 2 * W:3 * W]

    return pl.pallas_call(
        body, name=name, grid=(t // blk,),
        in_specs=[pl.BlockSpec((blk, NA), lambda i: (i, 0)), pl.BlockSpec((3, W), lambda i: (0, 0))],
        out_specs=pl.BlockSpec((blk, W), lambda i: (i, 0)), out_shape=jax.ShapeDtypeStruct((t, W), BF16),
        scratch_shapes=[pltpu.VMEM((8, W), F32), pltpu.VMEM((8, W), F32)],
        compiler_params=_cparams("arbitrary"))(pa, cw)


def _branch_a_bwd(pa, cw, dy, name):
    t = pa.shape[0]
    blk = _tile(t, 512)
    nt_ = t // blk
    W = CONV_W
    hb = blk // 8

    def body(p_ref, halo_ref, w_ref, dy_ref, dp_ref, dw_ref, chc_ref, chx_ref):
        i = pl.program_id(0)

        @pl.when(i == 0)
        def _():
            chc_ref[...] = jnp.zeros_like(chc_ref)
            chx_ref[...] = jnp.zeros_like(chx_ref)
            dw_ref[...] = jnp.zeros_like(dw_ref)

        keep = 1.0 - (i == nt_ - 1).astype(F32)
        hc = halo_ref[:, W:2 * W] * keep
        hx = halo_ref[:, 2 * W:3 * W] * keep
        ac_ext = jnp.concatenate([hc, p_ref[:, W:2 * W]], axis=0)
        ax_ext = jnp.concatenate([hx, p_ref[:, 2 * W:3 * W]], axis=0)
        _, vjp = jax.vjp(_conv_a_block, p_ref[:, 0:W], ac_ext, ax_ext, p_ref[:, 3 * W:4 * W],
                         w_ref[0:1, :], w_ref[1:2, :], w_ref[2:3, :])
        dab, dac, dax, daz, dw0, dw1, dw2 = vjp(dy_ref[...])
        dp_ref[:, 0:W] = dab.astype(BF16)
        dp_ref[:, W:2 * W] = _add_to_tail(dac[8:], chc_ref[...]).astype(BF16)
        dp_ref[:, 2 * W:3 * W] = _add_to_tail(dax[8:], chx_ref[...]).astype(BF16)
        dp_ref[:, 3 * W:4 * W] = daz.astype(BF16)
        chc_ref[...] = dac[:8] * keep
        chx_ref[...] = dax[:8] * keep
        dw_ref[0:1, :] += dw0
        dw_ref[1:2, :] += dw1
        dw_ref[2:3, :] += dw2

    rev = lambda i: (nt_ - 1 - i, 0)
    return pl.pallas_call(
        body, name=name, grid=(nt_,),
        in_specs=[pl.BlockSpec((blk, NA), rev),
                  pl.BlockSpec((8, NA), lambda i: (jnp.maximum((nt_ - 1 - i) * hb - 1, 0), 0)),
                  pl.BlockSpec((3, W), lambda i: (0, 0)),
                  pl.BlockSpec((blk, W), rev)],
        out_specs=[pl.BlockSpec((blk, NA), rev), pl.BlockSpec((3, W), lambda i: (0, 0))],
        out_shape=[jax.ShapeDtypeStruct((t, NA), BF16), jax.ShapeDtypeStruct((3, W), F32)],
        scratch_shapes=[pltpu.VMEM((8, W), F32), pltpu.VMEM((8, W), F32)],
        compiler_params=_cparams("arbitrary"))(pa, pa, cw, dy)


def _block_rows(t, chunks):
    return min(t, chunks * CHUNK)


def _branch_b_fwd(pb, lbs_row, nw, name, exchange=None):
    t = pb.shape[0]
    rows = _block_rows(t, HGRN_BLOCK_CHUNKS)
    nch = t // rows

    def body(p_ref, lb_ref, nw_ref, y_ref, s_ref, st_ref):
        @pl.when(pl.program_id(1) == 0)
        def _():
            st_ref[...] = jnp.zeros_like(st_ref)

        s_ref[0, 0] = st_ref[...]
        y, st1 = _hgrn_block(p_ref[:, 0:HD], p_ref[:, HD:2 * HD], p_ref[:, 2 * HD:3 * HD], p_ref[:, 3 * HD:4 * HD],
                             st_ref[...], lb_ref[...], nw_ref[...])
        y_ref[...] = y.astype(BF16)
        st_ref[...] = st1

    return _call_with_exchange(
        body, name=name, grid=(HGRN_HEADS, nch),
        in_specs=[pl.BlockSpec((rows, 4 * HD), lambda h, i: (i, h)),
                  pl.BlockSpec((1, HD), lambda h, i: (0, h)),
                  pl.BlockSpec((1, HD), lambda h, i: (0, 0))],
        out_specs=[pl.BlockSpec((rows, HD), lambda h, i: (i, h)),
                   pl.BlockSpec((1, 1, HD, HD), lambda h, i: (h, i, 0, 0))],
        out_shape=[jax.ShapeDtypeStruct((t, HGRN_HEADS * HD), BF16),
                   jax.ShapeDtypeStruct((HGRN_HEADS, nch, HD, HD), F32)],
        scratch_shapes=[pltpu.VMEM((HD, HD), F32)],
        args=(pb, lbs_row, nw), exchange=exchange)


def _branch_b_bwd(pb, states, lbs_row, nw, dy, name, exchange=None):
    t = pb.shape[0]
    rows = _block_rows(t, HGRN_BLOCK_CHUNKS)
    nch = t // rows

    def body(p_ref, s_ref, lb_ref, nw_ref, dy_ref, dp_ref, dlb_ref, dnw_ref, ds_ref):
        h, i = pl.program_id(0), pl.program_id(1)

        @pl.when(i == 0)
        def _():
            ds_ref[...] = jnp.zeros_like(ds_ref)
            dlb_ref[...] = jnp.zeros_like(dlb_ref)

        @pl.when((i == 0) & (h == 0))
        def _():
            dnw_ref[...] = jnp.zeros_like(dnw_ref)

        _, vjp = jax.vjp(_hgrn_block, p_ref[:, 0:HD], p_ref[:, HD:2 * HD], p_ref[:, 2 * HD:3 * HD],
                         p_ref[:, 3 * HD:4 * HD], s_ref[0, 0], lb_ref[...], nw_ref[...])
        dq, df, di, dz, ds0, dlb, dnw = vjp((dy_ref[...], ds_ref[...]))
        dp_ref[:, 0:HD] = dq.astype(BF16)
        dp_ref[:, HD:2 * HD] = df.astype(BF16)
        dp_ref[:, 2 * HD:3 * HD] = di.astype(BF16)
        dp_ref[:, 3 * HD:4 * HD] = dz.astype(BF16)
        ds_ref[...] = ds0
        dlb_ref[...] += dlb
        dnw_ref[...] += dnw

    rev = lambda h, i: (nch - 1 - i, h)
    return _call_with_exchange(
        body, name=name, grid=(HGRN_HEADS, nch),
        in_specs=[pl.BlockSpec((rows, 4 * HD), rev),
                  pl.BlockSpec((1, 1, HD, HD), lambda h, i: (h, nch - 1 - i, 0, 0)),
                  pl.BlockSpec((1, HD), lambda h, i: (0, h)),
                  pl.BlockSpec((1, HD), lambda h, i: (0, 0)),
                  pl.BlockSpec((rows, HD), rev)],
        out_specs=[pl.BlockSpec((rows, 4 * HD), rev),
                   pl.BlockSpec((1, HD), lambda h, i: (0, h)),
                   pl.BlockSpec((1, HD), lambda h, i: (0, 0))],
        out_shape=[jax.ShapeDtypeStruct((t, NB), BF16), jax.ShapeDtypeStruct((1, HGRN_HEADS * HD), F32),
                   jax.ShapeDtypeStruct((1, HD), F32)],
        scratch_shapes=[pltpu.VMEM((HD, HD), F32)],
        args=(pb, states, lbs_row, nw, dy), exchange=exchange)


def _branch_c_fwd(pc, cw, cpar, nw, name, exchange=None):
    t = pc.shape[0]
    rows = _block_rows(t, GDN_BLOCK_CHUNKS)
    nch = t // rows
    XW = 4 * HD

    nsys = 2 * rows // GROUP

    def body(p_ref, w_ref, cp_ref, nw_ref, y_ref, s_ref, x_ref, sa_ref, sb_ref, halo_ref):
        @pl.when(pl.program_id(1) == 0)
        def _():
            sa_ref[...] = jnp.zeros_like(sa_ref)
            sb_ref[...] = jnp.zeros_like(sb_ref)
            halo_ref[...] = jnp.zeros_like(halo_ref)

        s_ref[0, 0, 0] = sa_ref[...]
        s_ref[0, 0, 1] = sb_ref[...]
        x_ext = jnp.concatenate([halo_ref[...], p_ref[:, 0:XW]], axis=0)
        (y, s1a, s1b), xs = _gdn_block(x_ext, p_ref[:, XW:XW + 2 * HD], p_ref[:, XW + 2 * HD:XW + 3 * HD],
                                       sa_ref[...], sb_ref[...], w_ref[0:1, :], w_ref[1:2, :], w_ref[2:3, :], w_ref[3:4, :],
                                       cp_ref[0, 0:1, :], cp_ref[0, 1:2, :], nw_ref[...])
        for n in range(nsys):
            x_ref[0, 0, n] = xs[n]
        y_ref[...] = y.astype(BF16)
        sa_ref[...] = s1a
        sb_ref[...] = s1b
        halo_ref[...] = p_ref[rows - 8:rows, 0:XW]

    return _call_with_exchange(
        body, name=name, grid=(GDN_QK_HEADS, nch),
        in_specs=[pl.BlockSpec((rows, C_HEAD), lambda h, i: (i, h)),
                  pl.BlockSpec((4, XW), lambda h, i: (0, h)),
                  pl.BlockSpec((1, 8, HD), lambda h, i: (h, 0, 0)),
                  pl.BlockSpec((1, HD), lambda h, i: (0, 0))],
        out_specs=[pl.BlockSpec((rows, 2 * HD), lambda h, i: (i, h)),
                   pl.BlockSpec((1, 1, 2, HD, HD), lambda h, i: (h, i, 0, 0, 0)),
                   pl.BlockSpec((1, 1, nsys, GROUP, GROUP), lambda h, i: (h, i, 0, 0, 0))],
        out_shape=[jax.ShapeDtypeStruct((t, 2 * GDN_QK_HEADS * HD), BF16),
                   jax.ShapeDtypeStruct((GDN_QK_HEADS, nch, 2, HD, HD), F32),
                   jax.ShapeDtypeStruct((GDN_QK_HEADS, nch, nsys, GROUP, GROUP), BF16)],
        scratch_shapes=[pltpu.VMEM((HD, HD), F32), pltpu.VMEM((HD, HD), F32), pltpu.VMEM((8, XW), F32)],
        args=(pc, cw, cpar, nw), exchange=exchange)


def _branch_c_bwd(pc, states, inverses, cw, cpar, nw, dy, name, exchange=None):
    t = pc.shape[0]
    rows = _block_rows(t, GDN_BLOCK_CHUNKS)
    nch = t // rows
    XW = 4 * HD
    hb = rows // 8

    nsys = 2 * rows // GROUP

    def body(p_ref, halo_ref, s_ref, x_ref, w_ref, cp_ref, nw_ref, dy_ref, dp_ref, dw_ref, dcp_ref, dnw_ref,
             dsa_ref, dsb_ref, carry_ref):
        h, i = pl.program_id(0), pl.program_id(1)

        @pl.when(i == 0)
        def _():
            dsa_ref[...] = jnp.zeros_like(dsa_ref)
            dsb_ref[...] = jnp.zeros_like(dsb_ref)
            carry_ref[...] = jnp.zeros_like(carry_ref)
            dw_ref[...] = jnp.zeros_like(dw_ref)
            dcp_ref[...] = jnp.zeros_like(dcp_ref)

        @pl.when((i == 0) & (h == 0))
        def _():
            dnw_ref[...] = jnp.zeros_like(dnw_ref)

        keep = 1.0 - (i == nch - 1).astype(F32)
        x_ext = jnp.concatenate([halo_ref[:, 0:XW] * keep, p_ref[:, 0:XW]], axis=0)
        block = functools.partial(_gdn_block, known=tuple(x_ref[0, 0, n] for n in range(nsys)))
        _, vjp, _ = jax.vjp(block, x_ext, p_ref[:, XW:XW + 2 * HD], p_ref[:, XW + 2 * HD:XW + 3 * HD],
                            s_ref[0, 0, 0], s_ref[0, 0, 1], w_ref[0:1, :], w_ref[1:2, :], w_ref[2:3, :], w_ref[3:4, :],
                            cp_ref[0, 0:1, :], cp_ref[0, 1:2, :], nw_ref[...], has_aux=True)
        dx, dz, dba, dsa, dsb, dw0, dw1, dw2, dw3, dal, ddt, dnw = vjp((dy_ref[...], dsa_ref[...], dsb_ref[...]))
        dp_ref[:, 0:XW] = _add_to_tail(dx[8:], carry_ref[...]).astype(BF16)
        dp_ref[:, XW:XW + 2 * HD] = dz.astype(BF16)
        dp_ref[:, XW + 2 * HD:XW + 3 * HD] = dba.astype(BF16)
        carry_ref[...] = dx[:8] * keep
        dsa_ref[...] = dsa
        dsb_ref[...] = dsb
        dw_ref[0:1, :] += dw0
        dw_ref[1:2, :] += dw1
        dw_ref[2:3, :] += dw2
        dw_ref[3:4, :] += dw3
        dcp_ref[0, 0:1, :] += dal
        dcp_ref[0, 1:2, :] += ddt
        dnw_ref[...] += dnw

    rev = lambda h, i: (nch - 1 - i, h)
    return _call_with_exchange(
        body, name=name, grid=(GDN_QK_HEADS, nch),
        in_specs=[pl.BlockSpec((rows, C_HEAD), rev),
                  pl.BlockSpec((8, C_HEAD), lambda h, i: (jnp.maximum((nch - 1 - i) * hb - 1, 0), h)),
                  pl.BlockSpec((1, 1, 2, HD, HD), lambda h, i: (h, nch - 1 - i, 0, 0, 0)),
                  pl.BlockSpec((1, 1, nsys, GROUP, GROUP), lambda h, i: (h, nch - 1 - i, 0, 0, 0)),
                  pl.BlockSpec((4, XW), lambda h, i: (0, h)),
                  pl.BlockSpec((1, 8, HD), lambda h, i: (h, 0, 0)),
                  pl.BlockSpec((1, HD), lambda h, i: (0, 0)),
                  pl.BlockSpec((rows, 2 * HD), rev)],
        out_specs=[pl.BlockSpec((rows, C_HEAD), rev),
                   pl.BlockSpec((4, XW), lambda h, i: (0, h)),
                   pl.BlockSpec((1, 8, HD), lambda h, i: (h, 0, 0)),
                   pl.BlockSpec((1, HD), lambda h, i: (0, 0))],
        out_shape=[jax.ShapeDtypeStruct((t, NC_COLS), BF16), jax.ShapeDtypeStruct((4, GDN_QK_HEADS * XW), F32),
                   jax.ShapeDtypeStruct((GDN_QK_HEADS, 8, HD), F32), jax.ShapeDtypeStruct((1, HD), F32)],
        scratch_shapes=[pltpu.VMEM((HD, HD), F32), pltpu.VMEM((HD, HD), F32), pltpu.VMEM((8, XW), F32)],
        args=(pc, pc, states, inverses, cw, cpar, nw, dy), exchange=exchange)


def _merge_fwd(pg, bg, ya, yb, yc, name):
    t = pg.shape[0]
    blk = _tile(t, 256)

    def body(g_ref, b_ref, a_ref, b2_ref, c_ref, o_ref):
        gate = _sigmoid(g_ref[...] + b_ref[...])
        o_ref[...] = (gate[:, 0:D] * a_ref[...] + gate[:, D:2 * D] * b2_ref[...] + gate[:, 2 * D:3 * D] * c_ref[...]).astype(BF16)

    row = pl.BlockSpec((blk, D), lambda i: (i, 0))
    return pl.pallas_call(
        body, name=name, grid=(t // blk,),
        in_specs=[pl.BlockSpec((blk, NG), lambda i: (i, 0)), pl.BlockSpec((1, NG), lambda i: (0, 0)), row, row, row],
        out_specs=row, out_shape=jax.ShapeDtypeStruct((t, D), BF16),
        compiler_params=_cparams("parallel"))(pg, bg, ya, yb, yc)


def _merge_bwd(dm, pg, bg, ya, yb, yc, name):
    t = pg.shape[0]
    blk = _tile(t, 256)

    def body(dm_ref, g_ref, b_ref, a_ref, b2_ref, c_ref, dg_ref, da_ref, db_ref, dc_ref, dbg_ref):
        @pl.when(pl.program_id(0) == 0)
        def _():
            dbg_ref[...] = jnp.zeros_like(dbg_ref)

        gate = _sigmoid(g_ref[...] + b_ref[...])
        dmv = dm_ref[...]
        for j, (y_ref, dy_ref) in enumerate(((a_ref, da_ref), (b2_ref, db_ref), (c_ref, dc_ref))):
            gj = gate[:, j * D:(j + 1) * D]
            dy_ref[...] = (dmv * gj).astype(BF16)
            dgj = dmv * y_ref[...] * gj * (1.0 - gj)
            dg_ref[:, j * D:(j + 1) * D] = dgj.astype(BF16)
            dbg_ref[:, j * D:(j + 1) * D] += jnp.sum(dgj, axis=0, keepdims=True)

    row = pl.BlockSpec((blk, D), lambda i: (i, 0))
    wide = pl.BlockSpec((blk, NG), lambda i: (i, 0))
    vec = pl.BlockSpec((1, NG), lambda i: (0, 0))
    return pl.pallas_call(
        body, name=name, grid=(t // blk,), in_specs=[row, wide, vec, row, row, row],
        out_specs=[wide, row, row, row, vec],
        out_shape=[jax.ShapeDtypeStruct((t, NG), BF16)] + [jax.ShapeDtypeStruct((t, D), BF16)] * 3
                  + [jax.ShapeDtypeStruct((1, NG), F32)],
        compiler_params=_cparams("arbitrary"))(dm, pg, bg, ya, yb, yc)


def _adamw_math(w, g, m, v):
    m = ADAM_B1 * m + (1.0 - ADAM_B1) * g
    v = ADAM_B2 * v + (1.0 - ADAM_B2) * (g * g)
    m_hat = m / (1.0 - ADAM_B1 ** ADAM_STEP)
    v_hat = v / (1.0 - ADAM_B2 ** ADAM_STEP)
    delta = -ADAM_LR * (m_hat / (jnp.sqrt(v_hat) + ADAM_EPS) + ADAM_WD * w)
    return delta, m, v


def _sum_adamw(parts, w, m, v, name):
    r, c = w.shape
    br = r if r <= 256 else 256
    assert r % br == 0

    def body(p_ref, w_ref, m_ref, v_ref, g_ref, d_ref, nm_ref, nv_ref):
        g = p_ref[0].astype(F32)
        for k in range(1, N_DEV):
            g = g + p_ref[k].astype(F32)
        g_ref[...] = g
        d_ref[...], nm_ref[...], nv_ref[...] = _adamw_math(w_ref[...], g, m_ref[...], v_ref[...])

    blk = pl.BlockSpec((br, c), lambda i: (i, 0))
    return pl.pallas_call(
        body, name=name, grid=(r // br,),
        in_specs=[pl.BlockSpec((N_DEV, br, c), lambda i: (0, i, 0)), blk, blk, blk], out_specs=[blk] * 4,
        out_shape=[jax.ShapeDtypeStruct((r, c), F32)] * 4, compiler_params=_cparams("parallel"))(parts, w, m, v)


def _adamw(g, w, m, v, name):
    def body(g_ref, w_ref, m_ref, v_ref, d_ref, nm_ref, nv_ref):
        d_ref[...], nm_ref[...], nv_ref[...] = _adamw_math(w_ref[...], g_ref[...], m_ref[...], v_ref[...])

    return pl.pallas_call(body, name=name, out_shape=[jax.ShapeDtypeStruct(w.shape, F32)] * 3)(g, w, m, v)


def _sum_slots(parts, name):
    def body(p_ref, o_ref):
        g = p_ref[0]
        for k in range(1, N_DEV):
            g = g + p_ref[k]
        o_ref[...] = g

    return pl.pallas_call(body, name=name, out_shape=jax.ShapeDtypeStruct(parts.shape[1:], F32))(parts)


def _exchange(srcs, name, broadcast):
    n = len(srcs)

    def body(*refs):
        copies = _exchange_copies(refs[:n], refs[n:2 * n], *refs[2 * n:], broadcast)
        for cp in copies:
            cp.start()
        for cp in copies:
            cp.wait()

    return pl.pallas_call(
        body, name=name, in_specs=[HBM_SPEC] * n, out_specs=[HBM_SPEC] * n, out_shape=_exchange_shapes(srcs, broadcast),
        scratch_shapes=_exchange_semaphores(n))(*srcs)


def _gather_two_level(srcs, name):
    n = len(srcs)

    def body(*refs):
        src_refs, dst_refs = refs[:n], refs[n:2 * n]
        send_sems, recv_sems, local_sems = refs[2 * n:]
        x, y, c = lax.axis_index("x"), lax.axis_index("y"), lax.axis_index("c")
        index_of = lambda px, py, pc: 4 * px + 2 * py + pc
        me, other_core = index_of(x, y, c), (x, y, 1 - c)
        chips = [(1 - x, y), (x, 1 - y), (1 - x, 1 - y)]

        def copy(k, a, block, to, src=None):
            return pltpu.make_async_remote_copy(
                src_ref=dst_refs[a].at[block] if src is None else src, dst_ref=dst_refs[a].at[block],
                send_sem=send_sems.at[k, a], recv_sem=recv_sems.at[k, a], device_id=to, device_id_type=MESH)

        local = [pltpu.make_async_copy(src_refs[a], dst_refs[a].at[me], local_sems.at[a]) for a in range(n)]
        first = [copy(0, a, me, other_core, src=src_refs[a]) for a in range(n)]
        first += [copy(1 + j, a, me, (*chip, c), src=src_refs[a]) for j, chip in enumerate(chips) for a in range(n)]
        for cp in local + first:
            cp.start()
        passed = []
        for j, chip in enumerate(chips):
            block = index_of(*chip, c)
            for a in range(n):
                copy(1 + j, a, block, (x, y, c)).wait_recv()
            for a in range(n):
                passed.append(copy(4 + j, a, block, other_core))
                passed[-1].start()
        for a in range(n):
            copy(0, a, index_of(x, y, 1 - c), (x, y, c)).wait_recv()
        for j, chip in enumerate(chips):
            for a in range(n):
                copy(4 + j, a, index_of(*chip, 1 - c), (x, y, c)).wait_recv()
        for cp in first + passed:
            cp.wait_send()
        for cp in local:
            cp.wait()

    return pl.pallas_call(
        body, name=name, in_specs=[HBM_SPEC] * n, out_specs=[HBM_SPEC] * n, out_shape=_exchange_shapes(srcs, True),
        scratch_shapes=_exchange_semaphores(n))(*srcs)


HBM_SPEC = pl.BlockSpec(memory_space=pltpu.HBM)


def _exchange_shapes(srcs, broadcast):
    return [jax.ShapeDtypeStruct((N_DEV,) + (s.shape if broadcast else s.shape[1:]), s.dtype) for s in srcs]


def _exchange_semaphores(n):
    return [pltpu.SemaphoreType.DMA((N_DEV - 1, n)), pltpu.SemaphoreType.DMA((N_DEV - 1, n)), pltpu.SemaphoreType.DMA((n,))]


def _exchange_copies(src_refs, dst_refs, send_sems, recv_sems, local_sems, broadcast):
    x, y, c = lax.axis_index("x"), lax.axis_index("y"), lax.axis_index("c")
    me = 4 * x + 2 * y + c
    copies = []
    for k in range(1, N_DEV):
        px = 1 - x if (k >> 2) & 1 else x
        py = 1 - y if (k >> 1) & 1 else y
        pc = 1 - c if k & 1 else c
        peer = 4 * px + 2 * py + pc
        for a, (src, dst) in enumerate(zip(src_refs, dst_refs)):
            copies.append(pltpu.make_async_remote_copy(
                src_ref=src if broadcast else src.at[peer], dst_ref=dst.at[me],
                send_sem=send_sems.at[k - 1, a], recv_sem=recv_sems.at[k - 1, a],
                device_id=(px, py, pc), device_id_type=MESH))
    for a, (src, dst) in enumerate(zip(src_refs, dst_refs)):
        copies.append(pltpu.make_async_copy(src if broadcast else src.at[me], dst.at[me], local_sems.at[a]))
    return copies


def _call_with_exchange(body, *, name, grid, in_specs, out_specs, out_shape, scratch_shapes, args, exchange):
    if exchange is None:
        outs = pl.pallas_call(body, name=name, grid=grid, in_specs=in_specs, out_specs=out_specs, out_shape=out_shape,
                              scratch_shapes=scratch_shapes,
                              compiler_params=_cparams(*["arbitrary"] * len(grid)))(*args)
        return outs, None
    srcs, broadcast = exchange
    n, n_in, n_out, n_scr = len(srcs), len(args), len(out_shape), len(scratch_shapes)
    steps = 1
    for g in grid:
        steps *= g

    def hosted(*refs):
        ins, src_refs = refs[:n_in], refs[n_in:n_in + n]
        outs, dst_refs = refs[n_in + n:n_in + n + n_out], refs[n_in + n + n_out:n_in + 2 * n + n_out]
        scratch = refs[n_in + 2 * n + n_out:]
        step = pl.program_id(0)
        for axis in range(1, len(grid)):
            step = step * grid[axis] + pl.program_id(axis)

        @pl.when(step == 0)
        def _():
            for cp in _exchange_copies(src_refs, dst_refs, *scratch[n_scr:], broadcast):
                cp.start()

        body(*ins, *outs, *scratch[:n_scr])

        @pl.when(step == steps - 1)
        def _():
            for cp in _exchange_copies(src_refs, dst_refs, *scratch[n_scr:], broadcast):
                cp.wait()

    outs = pl.pallas_call(
        hosted, name=name, grid=grid, in_specs=list(in_specs) + [HBM_SPEC] * n, out_specs=list(out_specs) + [HBM_SPEC] * n,
        out_shape=list(out_shape) + _exchange_shapes(srcs, broadcast),
        scratch_shapes=list(scratch_shapes) + _exchange_semaphores(n),
        compiler_params=_cparams(*["arbitrary"] * len(grid)))(*args, *srcs)
    return outs[:n_out], outs[n_out:]


def _regroup_w_in(w):
    wa = w[:, OFF_A:OFF_A + NA]
    seg = lambda off, h, n=HD: w[:, off + h * n: off + (h + 1) * n]
    wb = jnp.concatenate([seg(OFF_B + s * 512, h) for h in range(HGRN_HEADS) for s in range(4)], axis=1)
    parts = []
    for h in range(GDN_QK_HEADS):
        small = jnp.concatenate(
            [w[:, OFF_BETA + 2 * h: OFF_BETA + 2 * h + 2], w[:, OFF_CA + 2 * h: OFF_CA + 2 * h + 2],
             jnp.zeros((w.shape[0], HD - 4), w.dtype)], axis=1)
        parts += [seg(OFF_CQ, h), seg(OFF_CK, h), seg(OFF_CV, h, 2 * HD), seg(OFF_CZ, h, 2 * HD), small]
    wc = jnp.concatenate(parts, axis=1)
    wg = w[:, OFF_G:OFF_G + NG]
    return wa, wb, wc, wg


def _ungroup_dw_in(da, db, dc, dg):
    bq = [jnp.concatenate([db[:, h * 512 + s * HD: h * 512 + (s + 1) * HD] for h in range(HGRN_HEADS)], axis=1)
          for s in range(4)]
    ch = lambda h, lo, hi: dc[:, h * C_HEAD + lo: h * C_HEAD + hi]
    heads = range(GDN_QK_HEADS)
    cq = jnp.concatenate([ch(h, 0, HD) for h in heads], axis=1)
    ck = jnp.concatenate([ch(h, HD, 2 * HD) for h in heads], axis=1)
    cv = jnp.concatenate([ch(h, 2 * HD, 4 * HD) for h in heads], axis=1)
    cz = jnp.concatenate([ch(h, 4 * HD, 6 * HD) for h in heads], axis=1)
    cbeta = jnp.concatenate([ch(h, 6 * HD, 6 * HD + 2) for h in heads], axis=1)
    ca = jnp.concatenate([ch(h, 6 * HD + 2, 6 * HD + 4) for h in heads], axis=1)
    return jnp.concatenate([da] + bq + [cq, ck, cv, cbeta, ca, cz, dg], axis=1)


def _regroup_conv_c(cw):
    parts = []
    for h in range(GDN_QK_HEADS):
        parts += [cw[:, h * HD:(h + 1) * HD], cw[:, 512 + h * HD: 512 + (h + 1) * HD],
                  cw[:, 1024 + 2 * h * HD: 1024 + (2 * h + 2) * HD]]
    return jnp.concatenate(parts, axis=1)


def _ungroup_conv_c(d):
    heads = range(GDN_QK_HEADS)
    q = jnp.concatenate([d[:, h * 512: h * 512 + HD] for h in heads], axis=1)
    k = jnp.concatenate([d[:, h * 512 + HD: h * 512 + 2 * HD] for h in heads], axis=1)
    v = jnp.concatenate([d[:, h * 512 + 2 * HD: h * 512 + 4 * HD] for h in heads], axis=1)
    return jnp.concatenate([q, k, v], axis=1)


def _numel(shape):
    n = 1
    for d in shape:
        n *= d
    return n


def _pack(arrays, rows):
    flat = jnp.concatenate([a.reshape(-1) for a in arrays])
    return jnp.pad(flat, (0, rows * 128 - flat.shape[0])).reshape(rows, 128)


def _unpack(packed, shapes):
    flat = packed.reshape(-1)
    out, off = [], 0
    for s in shapes:
        out.append(flat[off:off + _numel(s)].reshape(s))
        off += _numel(s)
    return out


def _rows_for(shapes):
    return -(-sum(_numel(s) for s in shapes) // 1024) * 8


def kernel(x, norm_w, w_in, b_gate, conv_a, conv_c, a_log, dt_bias, lower_bounds, hgrn_norm_w, gdn_norm_w, w_out_a, w_out_b, w_out_c, w_o, final_norm_w, loss_target, m_norm_w, m_w_in, m_b_gate, m_conv_a, m_conv_c, m_a_log, m_dt_bias, m_lower_bounds, m_hgrn_norm_w, m_gdn_norm_w, m_w_out_a, m_w_out_b, m_w_out_c, m_w_o, m_final_norm_w, v_norm_w, v_w_in, v_b_gate, v_conv_a, v_conv_c, v_a_log, v_dt_bias, v_lower_bounds, v_hgrn_norm_w, v_gdn_norm_w, v_w_out_a, v_w_out_b, v_w_out_c, v_w_o, v_final_norm_w):
    me = 4 * lax.axis_index("x") + 2 * lax.axis_index("y") + lax.axis_index("c")
    xs = x[0]
    target = loss_target[0]
    in_shard = w_in.shape[2]

    big = [w_in, w_out_a, w_out_b, w_out_c, w_o]
    shards_of = lambda l: [w[l].astype(BF16) for w in big]
    conv_shapes = [(DEPTH, 3, CONV_W), (DEPTH, 4, 2048)]
    conv_rows = _rows_for(conv_shapes)
    ca_full = lax.dynamic_update_slice(jnp.zeros(conv_shapes[0], F32), conv_a, (0, 0, me * conv_a.shape[2]))
    cc_full = lax.dynamic_update_slice(jnp.zeros(conv_shapes[1], F32), conv_c, (0, 0, me * conv_c.shape[2]))
    conv_parts, = _exchange([_pack([ca_full, cc_full], conv_rows)], "gather_conv", broadcast=True)
    conv_a_full, conv_c_full = _unpack(_sum_slots(conv_parts, "sum_conv"), conv_shapes)

    lb_pad = jnp.pad(lower_bounds, ((0, 8 - DEPTH), (0, 0)))
    lbs = _lower_bounds_fwd(lb_pad, "lower_bounds_fwd")

    def input_weights(l, g_in):
        wa, wb, wc, wg = _regroup_w_in(jnp.concatenate([g_in[q] for q in range(N_DEV)], axis=1))
        lanes = lambda vec: jnp.pad(vec.reshape(GDN_QK_HEADS, 1, 2), ((0, 0), (0, 0), (0, HD - 2)))
        cpar = jnp.concatenate([lanes(a_log[l]), lanes(dt_bias[l]), jnp.zeros((GDN_QK_HEADS, 6, HD), F32)], axis=1)
        return dict(
            wa=wa, wb=wb, wc=wc, wg=wg, cpar=cpar,
            nw=norm_w[l:l + 1], bg=b_gate[l:l + 1], cwa=conv_a_full[l], cwc=_regroup_conv_c(conv_c_full[l]),
            lb=lbs[l:l + 1], hnw=hgrn_norm_w[l:l + 1], gnw=gdn_norm_w[l:l + 1])

    def output_weights(g_oa, g_ob, g_oc, g_o):
        return dict(woa=jnp.concatenate([g_oa[q] for q in range(N_DEV)], axis=1),
                    wob=jnp.concatenate([g_ob[q] for q in range(N_DEV)], axis=1), woc=g_oc.reshape(D, D), wo=g_o.reshape(D, D))

    layers = [input_weights(0, _gather_two_level(shards_of(0)[:1], "gather_l0")[0])]

    saved = []
    cur = xs
    for l in range(DEPTH):
        L = layers[l]
        n = f"l{l}_"
        h = _rmsnorm_fwd(cur, L["nw"], n + "rms")
        pa = _matmul(h, L["wa"], "nn", n + "proj_a")
        pb = _matmul(h, L["wb"], "nn", n + "proj_b")
        pc = _matmul(h, L["wc"], "nn", n + "proj_c")
        pg = _matmul(h, L["wg"], "nn", n + "proj_g")
        ua = _branch_a_fwd(pa, L["cwa"], n + "conv_fwd")
        carry = (shards_of(l)[1:], True) if l == 0 else None
        (ub, sb), gathered = _branch_b_fwd(pb, L["lb"], L["hnw"], n + "hgrn_fwd", exchange=carry)
        if carry is not None:
            L.update(output_weights(*gathered))
        carry = (shards_of(l + 1), True) if l + 1 < DEPTH else None
        (uc, sc, xc), gathered = _branch_c_fwd(pc, L["cwc"], L["cpar"], L["gnw"], n + "gdn_fwd", exchange=carry)
        if carry is not None:
            layers.append(dict(input_weights(l + 1, gathered[0]), **output_weights(*gathered[1:])))
        ya = _matmul(ua, L["woa"], "nn", n + "out_a")
        yb = _matmul(ub, L["wob"], "nn", n + "out_b")
        yc = _matmul(uc, L["woc"], "nn", n + "out_c")
        merged = _merge_fwd(pg, L["bg"], ya, yb, yc, n + "merge")
        nxt = _matmul(merged, L["wo"], "nn", n + "out_o", residual=cur)
        saved.append(dict(x=cur, h=h, pa=pa, pb=pb, pc=pc, pg=pg, ua=ua, ub=ub, uc=uc, sb=sb, sc=sc, xc=xc,
                          ya=ya, yb=yb, yc=yc, merged=merged))
        cur = nxt

    loss_part, dx, d_final = _loss_head(cur, final_norm_w.reshape(1, D), target, "loss_head")

    def outgoing(g):
        cols = lambda a, n: jnp.stack([a[:, p * n:(p + 1) * n] for p in range(N_DEV)]).astype(BF16)
        rows = lambda a: a.reshape(N_DEV, a.shape[0] // N_DEV, a.shape[1]).astype(BF16)
        first = [cols(g["w_in"], in_shard)] if "w_in" in g else [None]
        if "w_o" not in g:
            return first
        return first + [cols(g["w_out_a"], 128), cols(g["w_out_b"], 128), rows(g["w_out_c"]), rows(g["w_o"])]

    grads = [None] * DEPTH
    dlbs_rows = [None] * DEPTH
    incoming = [None] * DEPTH
    for l in reversed(range(DEPTH)):
        L, S = layers[l], saved[l]
        n = f"l{l}_"
        dmerged = _matmul(dx, L["wo"], "nt", n + "d_merged")
        d_wo = _matmul(S["merged"], dx, "tn", n + "dw_o", out_dtype=BF16)
        dpg, dya, dyb, dyc, d_bg = _merge_bwd(dmerged, S["pg"], L["bg"], S["ya"], S["yb"], S["yc"], n + "merge_bwd")
        dua = _matmul(dya, L["woa"], "nt", n + "d_ua")
        dub = _matmul(dyb, L["wob"], "nt", n + "d_ub")
        duc = _matmul(dyc, L["woc"], "nt", n + "d_uc")
        d_woa = _matmul(S["ua"], dya, "tn", n + "dw_out_a", out_dtype=BF16)
        d_wob = _matmul(S["ub"], dyb, "tn", n + "dw_out_b", out_dtype=BF16)
        d_woc = _matmul(S["uc"], dyc, "tn", n + "dw_out_c", out_dtype=BF16)
        dpa, d_cwa = _branch_a_bwd(S["pa"], L["cwa"], dua, n + "conv_bwd")
        out_grads = dict(w_out_a=d_woa, w_out_b=d_wob, w_out_c=d_woc, w_o=d_wo)
        carry = (outgoing(out_grads)[1:], False) if l == 0 else None
        (dpb, d_lb, d_hnw), arrived_out = _branch_b_bwd(S["pb"], S["sb"], L["lb"], L["hnw"], dub, n + "hgrn_bwd", exchange=carry)
        carry = (outgoing(grads[l + 1]), False) if l + 1 < DEPTH else None
        (dpc, d_cwc, d_cpar, d_gnw), arrived = _branch_c_bwd(S["pc"], S["sc"], S["xc"], L["cwc"], L["cpar"], L["gnw"], duc,
                                                             n + "gdn_bwd", exchange=carry)
        if carry is not None:
            incoming[l + 1] = arrived
        dw_in_rows = lambda h_part, tag: _ungroup_dw_in(*[
            _matmul(h_part, dp, "tn", f"{n}dw_{piece}{tag}", out_dtype=BF16)
            for dp, piece in ((dpa, "a"), (dpb, "b"), (dpc, "c"), (dpg, "g"))])
        dh_pairs = [(dpa, L["wa"]), (dpb, L["wb"]), (dpc, L["wc"]), (dpg, L["wg"])]
        if l == 0:
            half = D // 2
            d_win_top = dw_in_rows(S["h"][:, :half], "_top")
            dh, arrived_top = _matmul_nt_sum(dh_pairs, n + "dh", exchange=(outgoing(dict(w_in=d_win_top))[:1], False))
            d_win_bottom = dw_in_rows(S["h"][:, half:], "_bottom")
            (dx, d_nw), arrived_bottom = _rmsnorm_bwd(dh, S["x"], L["nw"], dx, n + "rms_bwd",
                                                      exchange=(outgoing(dict(w_in=d_win_bottom))[:1], False))
            d_win = None
        else:
            dh, _ = _matmul_nt_sum(dh_pairs, n + "dh")
            d_win = dw_in_rows(S["h"], "")
            (dx, d_nw), _ = _rmsnorm_bwd(dh, S["x"], L["nw"], dx, n + "rms_bwd")
        dlbs_rows[l] = d_lb
        grads[l] = dict(w_in=d_win, w_out_a=d_woa, w_out_b=d_wob, w_out_c=d_woc, w_o=d_wo, norm_w=d_nw[0],
                        b_gate=d_bg[0], conv_a=d_cwa, conv_c=_ungroup_conv_c(d_cwc),
                        a_log=d_cpar[:, 0, 0:2].reshape(-1), dt_bias=d_cpar[:, 1, 0:2].reshape(-1),
                        hgrn_norm_w=d_hnw[0], gdn_norm_w=d_gnw[0])
    grad_x = dx[None]
    d_lower = _lower_bounds_bwd(lb_pad, jnp.pad(jnp.concatenate(dlbs_rows, axis=0), ((0, 8 - DEPTH), (0, 0))),
                                "lower_bounds_bwd")[:DEPTH]

    incoming[0] = [jnp.concatenate([arrived_top[0], arrived_bottom[0]], axis=1)] + list(arrived_out)
    stack = lambda name: jnp.stack([grads[l][name] for l in range(DEPTH)])
    big_out = {}
    for j, (name, w, m, v) in enumerate((("w_in", w_in, m_w_in, v_w_in), ("w_out_a", w_out_a, m_w_out_a, v_w_out_a),
                                         ("w_out_b", w_out_b, m_w_out_b, v_w_out_b), ("w_out_c", w_out_c, m_w_out_c, v_w_out_c),
                                         ("w_o", w_o, m_w_o, v_w_o))):
        parts = jnp.concatenate([incoming[l][j] for l in range(DEPTH)], axis=1)
        r2 = lambda a: a.reshape(parts.shape[1], parts.shape[2])
        outs = _sum_adamw(parts, r2(w), r2(m), r2(v), "adamw_" + name)
        big_out[name] = [o.reshape(w.shape) for o in outs]

    small_names = ["norm_w", "b_gate", "conv_a", "conv_c", "a_log", "dt_bias", "lower_bounds", "hgrn_norm_w",
                   "gdn_norm_w", "final_norm_w", "loss"]
    small_vals = {k: stack(k) for k in ("norm_w", "b_gate", "conv_a", "conv_c", "a_log", "dt_bias", "hgrn_norm_w", "gdn_norm_w")}
    small_vals.update(lower_bounds=d_lower, final_norm_w=d_final[0], loss=loss_part.reshape(1))
    small_shapes = [small_vals[k].shape for k in small_names]
    small_rows = _rows_for(small_shapes)
    small_parts, = _exchange([_pack([small_vals[k] for k in small_names], small_rows)], "exchange_small", broadcast=True)
    total = dict(zip(small_names, _unpack(_sum_slots(small_parts, "sum_small"), small_shapes)))
    loss = total["loss"][0]
    g_conv_a = lax.dynamic_slice(total["conv_a"], (0, 0, me * conv_a.shape[2]), conv_a.shape)
    g_conv_c = lax.dynamic_slice(total["conv_c"], (0, 0, me * conv_c.shape[2]), conv_c.shape)

    small_w = dict(norm_w=(norm_w, m_norm_w, v_norm_w), b_gate=(b_gate, m_b_gate, v_b_gate),
                   conv_a=(conv_a, m_conv_a, v_conv_a), conv_c=(conv_c, m_conv_c, v_conv_c),
                   a_log=(a_log, m_a_log, v_a_log), dt_bias=(dt_bias, m_dt_bias, v_dt_bias),
                   lower_bounds=(lower_bounds, m_lower_bounds, v_lower_bounds),
                   hgrn_norm_w=(hgrn_norm_w, m_hgrn_norm_w, v_hgrn_norm_w), gdn_norm_w=(gdn_norm_w, m_gdn_norm_w, v_gdn_norm_w),
                   final_norm_w=(final_norm_w, m_final_norm_w, v_final_norm_w))
    small_g = dict(total, conv_a=g_conv_a, conv_c=g_conv_c)
    upd_names = small_names[:-1]
    upd_shapes = [small_w[k][0].shape for k in upd_names]
    upd_rows = _rows_for(upd_shapes)
    pk = lambda j: _pack([small_w[k][j] for k in upd_names], upd_rows)
    s_delta, s_m, s_v = _adamw(_pack([small_g[k] for k in upd_names], upd_rows), pk(0), pk(1), pk(2), "adamw_small")
    small_out = {k: [small_g[k], d, mm, vv] for k, d, mm, vv in
                 zip(upd_names, _unpack(s_delta, upd_shapes), _unpack(s_m, upd_shapes), _unpack(s_v, upd_shapes))}

    order = ["norm_w", "w_in", "b_gate", "conv_a", "conv_c", "a_log", "dt_bias", "lower_bounds", "hgrn_norm_w",
             "gdn_norm_w", "w_out_a", "w_out_b", "w_out_c", "w_o", "final_norm_w"]
    res = {**small_out, **big_out}
    outs = [loss, grad_x]
    for j in range(4):
        outs += [res[k][j] for k in order]
    return tuple(outs)
```

```python
import functools

import jax
import jax.numpy as jnp
from jax import lax
from jax.experimental import pallas as pl
from jax.experimental.pallas import tpu as pltpu

F32 = jnp.float32
BF16 = jnp.bfloat16
MESH = pl.DeviceIdType.MESH

N_DEV = 8
D = 1024
DEPTH = 2
CHUNK = 64
HGRN_BLOCK_CHUNKS = 16
GDN_BLOCK_CHUNKS = 8
GROUP = 128
NORM_EPS = 1e-6
L2_EPS = 1e-6
MIN_F = 1e-30
HD = 128
HGRN_HEADS = 4
GDN_QK_HEADS = 4
CONV_W = 512
IN_COLS = 10256
OFF_A, OFF_B, OFF_CQ, OFF_CK, OFF_CV, OFF_BETA, OFF_CA, OFF_CZ, OFF_G = (
    0, 2048, 4096, 4608, 5120, 6144, 6152, 6160, 7184)
NA, NB, NC_COLS, NG = 2048, 2048, 3584, 3072
C_HEAD = 896

ADAM_LR, ADAM_B1, ADAM_B2, ADAM_EPS, ADAM_WD, ADAM_STEP = 0.001, 0.9, 0.999, 1e-08, 0.01, 10

VMEM_LIMIT = 56 * 1024 * 1024
MM_TILE = 1024


def _cparams(*sem):
    return pltpu.CompilerParams(dimension_semantics=sem, vmem_limit_bytes=VMEM_LIMIT)


def _tile(dim, cap):
    if dim <= cap:
        return dim
    t = (cap // 128) * 128
    while dim % t:
        t -= 128
    return t


def _sigmoid(x):
    return 1.0 / (1.0 + jnp.exp(-x))


def _silu(x):
    return x * _sigmoid(x)


def _softplus(x):
    return jnp.maximum(x, 0.0) + jnp.log(1.0 + jnp.exp(-jnp.abs(x)))


def _dot(a, b, dims, precision=None):
    if precision is None:
        a, b = a.astype(BF16), b.astype(BF16)
    return lax.dot_general(a, b, (dims, ((), ())), precision=precision, preferred_element_type=F32)


def _nn(a, b, precision=None):
    return _dot(a, b, ((1,), (0,)), precision)


def _nt(a, b, precision=None):
    return _dot(a, b, ((1,), (1,)), precision)


def _tn(a, b, precision=None):
    return _dot(a, b, ((0,), (0,)), precision)


def _sum_rows_split(mat01, x):
    m = mat01.astype(BF16)
    hi = x.astype(BF16)
    low = (x - hi.astype(F32)).astype(BF16)
    return _nn(m, hi) + _nn(m, low)


@functools.partial(jax.custom_vjp, nondiff_argnums=(1,))
def _shift_rows(x, d):
    return x if d == 0 else pltpu.roll(x, d, 0)


def _shift_rows_fwd(x, d):
    return _shift_rows(x, d), None


def _shift_rows_bwd(d, _, ct):
    return ((ct if d == 0 else pltpu.roll(ct, ct.shape[0] - d, 0)),)


_shift_rows.defvjp(_shift_rows_fwd, _shift_rows_bwd)


def _iota2(shape):
    return lax.broadcasted_iota(jnp.int32, shape, 0), lax.broadcasted_iota(jnp.int32, shape, 1)


def _lane_pick(x, i):
    lane = lax.broadcasted_iota(jnp.int32, x.shape, 1)
    return jnp.sum(jnp.where(lane == i, x, 0.0), axis=1, keepdims=True)


def _hgrn_block(qr, fr, ir, zr, st0, lb, nw):
    rows = qr.shape[0]
    r, c = _iota2((CHUNK, CHUNK))
    halves = [1 << j for j in range(CHUNK.bit_length() - 1)]
    mats = [c <= r, c > r]
    pairs = []
    for hb in halves:
        same = (r // hb) == (c // hb)
        if hb > 1:
            mats += [(c <= r) & same, (c > r) & same]
        pairs.append(((r // (2 * hb)) == (c // (2 * hb))) & ((r // hb) == (c // hb) + 1))
    stack = jnp.concatenate([m.astype(F32) for m in mats], axis=0)

    q = _silu(qr) * (HD ** -0.5)
    fg = lb + (1.0 - lb) * _sigmoid(fr)
    logf = jnp.log(jnp.maximum(fg, MIN_F))
    kk = 1.0 - fg
    v = ir

    chunks = [slice(s, s + CHUNK) for s in range(0, rows, CHUNK)]
    cums = [_sum_rows_split(stack, logf[sl]) for sl in chunks]
    part = lambda i: jnp.concatenate([cs[i * CHUNK:(i + 1) * CHUNK] for cs in cums], axis=0)
    qg = q * jnp.exp(part(0))
    ks = kk * jnp.exp(part(1))
    q_lv = [q * jnp.exp(logf)] + [q * jnp.exp(part(2 * j)) for j in range(1, len(halves))]
    k_lv = [kk] + [kk * jnp.exp(part(2 * j + 1)) for j in range(1, len(halves))]
    st = st0
    outs = []
    for sl in chunks:
        scores = jnp.where(pairs[0], _nt(q_lv[0][sl], k_lv[0][sl]), 0.0)
        for j in range(1, len(halves)):
            scores += jnp.where(pairs[j], _nt(q_lv[j][sl], k_lv[j][sl]), 0.0)
        outs.append(_nn(scores, v[sl]) + _nt(qg[sl], st))
        st = st * jnp.exp(jnp.sum(logf[sl], axis=0, keepdims=True)) + _tn(v[sl], ks[sl])
    o = jnp.concatenate(outs, axis=0) + jnp.sum(q * kk, axis=1, keepdims=True) * v
    y = o * lax.rsqrt(jnp.mean(o * o, axis=1, keepdims=True) + NORM_EPS) * nw * _silu(zr)
    return y, st


def _unit_lower_inverses(ms):
    r, c = _iota2(ms[0].shape)
    xs = [jnp.where(r == c, 1.0, 0.0) - jnp.where((r // 2) == (c // 2), m, 0.0) for m in ms]
    b = 2
    while b < CHUNK:
        pick = ((r // (2 * b)) == (c // (2 * b))) & ((r // b) != (c // b))
        ts = [_nn(x, jnp.where(pick, m, 0.0)) for x, m in zip(xs, ms)]
        xs = [x - _nn(t, x) for x, t in zip(xs, ts)]
        b *= 2
    return tuple(x.astype(BF16) for x in xs)


@jax.custom_vjp
def _known_inverses(ms, xs):
    return xs


def _known_inverses_fwd(ms, xs):
    return xs, xs


def _known_inverses_bwd(xs, cts):
    r, c = _iota2(xs[0].shape)
    keep = (c < r) & ((r // CHUNK) == (c // CHUNK))
    ts = [_tn(x, ct) for x, ct in zip(xs, cts)]
    return (tuple(jnp.where(keep, -_nt(t, x), 0.0) for t, x in zip(ts, xs)), tuple(jnp.zeros_like(x) for x in xs))


_known_inverses.defvjp(_known_inverses_fwd, _known_inverses_bwd)


def _chunk_cumsum(x):
    row = lax.broadcasted_iota(jnp.int32, x.shape, 0) % CHUNK
    d = 1
    while d < CHUNK:
        x = x + jnp.where(row >= d, _shift_rows(x, d), 0.0)
        d *= 2
    return x


def _gdn_block(x_ext, z, ba, s0a, s0b, w0, w1, w2, w3, alog, dtb, nw, known=None):
    rows = z.shape[0]
    conv = (w0 * _shift_rows(x_ext, 3) + w1 * _shift_rows(x_ext, 2) + w2 * _shift_rows(x_ext, 1) + w3 * x_ext)
    cc = _silu(conv[8:])
    qc, kc = cc[:, 0:HD], cc[:, HD:2 * HD]
    q = qc * lax.rsqrt(jnp.sum(qc * qc, axis=1, keepdims=True) + L2_EPS) * (HD ** -0.5)
    k = kc * lax.rsqrt(jnp.sum(kc * kc, axis=1, keepdims=True) + L2_EPS)

    r, c = _iota2((GROUP, GROUP))
    same = (r // CHUNK) == (c // CHUNK)
    causal, strict, eye = same & (c <= r), same & (c < r), r == c
    heads = (0, 1)
    groups = [slice(lo, lo + GROUP) for lo in range(0, rows, GROUP)]
    chunks = [slice(lo, lo + CHUNK) for lo in range(0, rows, CHUNK)]

    v, loga, g_w, kb, kg, qg = [], [], [], [], [], []
    for i in heads:
        v.append(cc[:, (2 + i) * HD:(3 + i) * HD])
        beta = _sigmoid(_lane_pick(ba, i))
        a_neg = -jnp.exp(_lane_pick(alog, i))
        loga.append(a_neg * _softplus(_lane_pick(ba, 2 + i) + _lane_pick(dtb, i)))
        g_w.append(_chunk_cumsum(jnp.broadcast_to(loga[i], (rows, HD))))
        kb.append(k * beta)
        kg.append(k * jnp.exp(g_w[i]))
        qg.append(q * jnp.exp(g_w[i]))

    systems = [(i, gs) for gs in groups for i in heads]
    dec_c, ms = [], []
    for i, gs in systems:
        g_sq = g_w[i][gs]
        g_row = jnp.sum(jnp.where(eye, g_sq, 0.0), axis=0, keepdims=True)
        diff = g_sq - g_row
        dec_c.append(jnp.where(causal, jnp.exp(jnp.where(causal, diff, 0.0)), 0.0))
        ms.append(jnp.where(strict, _nt(k[gs], kb[i][gs]) * dec_c[-1], 0.0))
    xs = _unit_lower_inverses(tuple(ms)) if known is None else _known_inverses(tuple(ms), known)
    u = [[None] * len(groups) for _ in heads]
    w = [[None] * len(groups) for _ in heads]
    qk = [[None] * len(groups) for _ in heads]
    for n, (i, gs) in enumerate(systems):
        j = n // len(heads)
        u[i][j] = _nn(xs[n], v[i][gs])
        w[i][j] = _nn(xs[n], kg[i][gs])
        qk[i][j] = _nt(q[gs], kb[i][gs]) * dec_c[n]
    u = [jnp.concatenate(p, axis=0) for p in u]
    w = [jnp.concatenate(p, axis=0) for p in w]

    decay, p_mat, q_mat = {}, {}, {}
    for n, sl in enumerate(chunks):
        for i in heads:
            g_last = jnp.sum(loga[i][sl], axis=0, keepdims=True)
            kd = kb[i][sl] * jnp.exp(g_last - g_w[i][sl])
            decay[n, i] = jnp.exp(g_last)
            p_mat[n, i] = -_tn(kd, w[i][sl])
            q_mat[n, i] = _tn(kd, u[i][sl])
    s = [s0a, s0b]
    s_at = {}
    for n in range(len(chunks)):
        for i in heads:
            s_at[n, i] = s[i]
            s[i] = s[i] * decay[n, i] + _nn(p_mat[n, i], s[i]) + q_mat[n, i]

    ys = []
    for i in heads:
        e = jnp.concatenate([u[i][sl] - _nn(w[i][sl], s_at[n, i]) for n, sl in enumerate(chunks)], axis=0)
        o_state = jnp.concatenate([_nn(qg[i][sl], s_at[n, i]) for n, sl in enumerate(chunks)], axis=0)
        o = o_state + jnp.concatenate([_nn(qk[i][j], e[gs]) for j, gs in enumerate(groups)], axis=0)
        zi = z[:, i * HD:(i + 1) * HD]
        ys.append(o * lax.rsqrt(jnp.mean(o * o, axis=1, keepdims=True) + NORM_EPS) * nw * _silu(zi))
    return (jnp.concatenate(ys, axis=1), s[0], s[1]), xs


def _add_to_tail(x, tail):
    return x + jnp.concatenate([jnp.zeros((x.shape[0] - 8, x.shape[1]), x.dtype), tail], axis=0)


def _conv_a_block(ab, ac_ext, ax_ext, az, w0, w1, w2):
    u = ac_ext * ax_ext
    conv = (w0 * _shift_rows(u, 2) + w1 * _shift_rows(u, 1) + w2 * u)[8:]
    return ab * conv * _silu(az)


def _matmul(a, b, mode, name, residual=None, out_dtype=F32):
    if mode == "nn":
        (m, k), n = a.shape, b.shape[1]
    elif mode == "nt":
        (m, k), n = a.shape, b.shape[0]
    else:
        (k, m), n = a.shape, b.shape[1]
    tm, tn, tk = _tile(m, MM_TILE), _tile(n, MM_TILE), _tile(k, MM_TILE)
    nk = k // tk
    dims = {"nn": ((1,), (0,)), "nt": ((1,), (1,)), "tn": ((0,), (0,))}[mode]
    a_spec = pl.BlockSpec((tk, tm), lambda i, j, s: (s, i)) if mode == "tn" else pl.BlockSpec((tm, tk), lambda i, j, s: (i, s))
    b_spec = pl.BlockSpec((tn, tk), lambda i, j, s: (j, s)) if mode == "nt" else pl.BlockSpec((tk, tn), lambda i, j, s: (s, j))
    o_spec = pl.BlockSpec((tm, tn), lambda i, j, s: (i, j))
    has_res = residual is not None

    def finish(out, r_ref, o_ref):
        if has_res:
            out = out + r_ref[...]
        o_ref[...] = out.astype(out_dtype)

    def body_one_pass(*refs):
        finish(_dot(refs[0][...], refs[1][...], dims), refs[2] if has_res else None, refs[-1])

    def body_reduce(*refs):
        a_ref, b_ref = refs[0], refs[1]
        r_ref = refs[2] if has_res else None
        o_ref, acc_ref = refs[-2], refs[-1]
        s = pl.program_id(2)

        @pl.when(s == 0)
        def _():
            acc_ref[...] = jnp.zeros_like(acc_ref)

        acc_ref[...] += _dot(a_ref[...], b_ref[...], dims)

        @pl.when(s == nk - 1)
        def _():
            finish(acc_ref[...], r_ref, o_ref)

    args, specs = [a, b], [a_spec, b_spec]
    if has_res:
        args.append(residual)
        specs.append(o_spec)
    return pl.pallas_call(
        body_one_pass if nk == 1 else body_reduce, name=name, grid=(m // tm, n // tn, nk), in_specs=specs, out_specs=o_spec,
        out_shape=jax.ShapeDtypeStruct((m, n), out_dtype),
        scratch_shapes=[] if nk == 1 else [pltpu.VMEM((tm, tn), F32)],
        compiler_params=_cparams("parallel", "parallel", "arbitrary"))(*args)


def _matmul_nt_sum(pairs, name, exchange=None):
    m, n = pairs[0][0].shape[0], pairs[0][1].shape[0]
    tm, tn = _tile(m, MM_TILE), _tile(n, MM_TILE)
    tks = [_tile(a.shape[1], MM_TILE) for a, _ in pairs]
    nks = [a.shape[1] // tk for (a, _), tk in zip(pairs, tks)]
    offs = [sum(nks[:i]) for i in range(len(pairs))]
    total = sum(nks)

    def body(*refs):
        o_ref, acc_ref = refs[-2], refs[-1]
        s = pl.program_id(2)

        @pl.when(s == 0)
        def _():
            acc_ref[...] = jnp.zeros_like(acc_ref)

        for i, (off, nk) in enumerate(zip(offs, nks)):
            @pl.when((s >= off) & (s < off + nk))
            def _(i=i):
                acc_ref[...] += _dot(refs[2 * i][...], refs[2 * i + 1][...], ((1,), (1,)))

        @pl.when(s == total - 1)
        def _():
            o_ref[...] = acc_ref[...]

    args, specs = [], []
    for (a, b), tk, off, nk in zip(pairs, tks, offs, nks):
        k_of = lambda s, off=off, nk=nk: jnp.clip(s - off, 0, nk - 1)
        args += [a, b]
        specs += [pl.BlockSpec((tm, tk), lambda i, j, s, k_of=k_of: (i, k_of(s))),
                  pl.BlockSpec((tn, tk), lambda i, j, s, k_of=k_of: (j, k_of(s)))]
    (out,), exchanged = _call_with_exchange(
        body, name=name, grid=(m // tm, n // tn, total), in_specs=specs,
        out_specs=[pl.BlockSpec((tm, tn), lambda i, j, s: (i, j))], out_shape=[jax.ShapeDtypeStruct((m, n), F32)],
        scratch_shapes=[pltpu.VMEM((tm, tn), F32)], args=args, exchange=exchange)
    return out, exchanged


def _rmsnorm_fwd(x, w, name):
    t = x.shape[0]
    blk = _tile(t, 512)

    def body(x_ref, w_ref, h_ref):
        xv = x_ref[...]
        h_ref[...] = (xv * lax.rsqrt(jnp.mean(xv * xv, axis=1, keepdims=True) + NORM_EPS) * w_ref[...]).astype(BF16)

    return pl.pallas_call(
        body, name=name, grid=(t // blk,),
        in_specs=[pl.BlockSpec((blk, D), lambda i: (i, 0)), pl.BlockSpec((1, D), lambda i: (0, 0))],
        out_specs=pl.BlockSpec((blk, D), lambda i: (i, 0)), out_shape=jax.ShapeDtypeStruct((t, D), BF16),
        compiler_params=_cparams("parallel"))(x, w)


def _rmsnorm_bwd(dh, x, w, dxo, name, exchange=None):
    t = x.shape[0]
    blk = _tile(t, 512)

    def body(dh_ref, x_ref, w_ref, dxo_ref, dx_ref, dw_ref):
        @pl.when(pl.program_id(0) == 0)
        def _():
            dw_ref[...] = jnp.zeros_like(dw_ref)

        xv, dhv = x_ref[...], dh_ref[...]
        rs = lax.rsqrt(jnp.mean(xv * xv, axis=1, keepdims=True) + NORM_EPS)
        xh = xv * rs
        dw_ref[...] += jnp.sum(dhv * xh, axis=0, keepdims=True)
        dxh = dhv * w_ref[...]
        dx_ref[...] = rs * (dxh - xh * jnp.mean(dxh * xh, axis=1, keepdims=True)) + dxo_ref[...]

    row = pl.BlockSpec((blk, D), lambda i: (i, 0))
    vec = pl.BlockSpec((1, D), lambda i: (0, 0))
    return _call_with_exchange(
        body, name=name, grid=(t // blk,), in_specs=[row, row, vec, row], out_specs=[row, vec],
        out_shape=[jax.ShapeDtypeStruct((t, D), F32), jax.ShapeDtypeStruct((1, D), F32)],
        scratch_shapes=[], args=(dh, x, w, dxo), exchange=exchange)


def _loss_head(x, w, target, name):
    t = x.shape[0]
    blk = _tile(t, 512)

    def body(x_ref, w_ref, t_ref, loss_ref, dx_ref, dw_ref):
        @pl.when(pl.program_id(0) == 0)
        def _():
            dw_ref[...] = jnp.zeros_like(dw_ref)
            loss_ref[...] = jnp.zeros_like(loss_ref)

        xv = x_ref[...]
        rs = lax.rsqrt(jnp.mean(xv * xv, axis=1, keepdims=True) + NORM_EPS)
        xh = xv * rs
        err = xh * w_ref[...] - t_ref[...]
        loss_ref[...] += 0.5 * jnp.sum(jnp.mean(err * err, axis=1, keepdims=True), axis=0, keepdims=True)
        dy = err * (1.0 / D)
        dw_ref[...] += jnp.sum(dy * xh, axis=0, keepdims=True)
        dxh = dy * w_ref[...]
        dx_ref[...] = rs * (dxh - xh * jnp.mean(dxh * xh, axis=1, keepdims=True))

    row = pl.BlockSpec((blk, D), lambda i: (i, 0))
    vec = pl.BlockSpec((1, D), lambda i: (0, 0))
    return pl.pallas_call(
        body, name=name, grid=(t // blk,), in_specs=[row, vec, row],
        out_specs=[pl.BlockSpec((1, 1), lambda i: (0, 0)), row, vec],
        out_shape=[jax.ShapeDtypeStruct((1, 1), F32), jax.ShapeDtypeStruct((t, D), F32), jax.ShapeDtypeStruct((1, D), F32)],
        compiler_params=_cparams("arbitrary"))(x, w, target)


def _lbs_of(lb):
    r = lax.broadcasted_iota(jnp.int32, lb.shape, 0)
    real = r < DEPTH
    mx = lax.stop_gradient(jnp.max(jnp.where(real, lb, -jnp.inf), axis=0, keepdims=True))
    e = jnp.where(real, jnp.exp(jnp.where(real, lb - mx, 0.0)), 0.0)
    p = e / jnp.sum(e, axis=0, keepdims=True)
    out = jnp.zeros_like(lb)
    run = jnp.zeros_like(mx)
    for l in range(1, DEPTH):
        run = run + jnp.sum(jnp.where(r == l, p, 0.0), axis=0, keepdims=True)
        out = out + jnp.where(r == l, run, 0.0)
    return out


def _lower_bounds_fwd(lbp, name):
    def body(lb_ref, o_ref):
        o_ref[...] = _lbs_of(lb_ref[...])

    return pl.pallas_call(body, name=name, out_shape=jax.ShapeDtypeStruct(lbp.shape, F32))(lbp)


def _lower_bounds_bwd(lbp, dlbs, name):
    def body(lb_ref, d_ref, o_ref):
        _, vjp = jax.vjp(_lbs_of, lb_ref[...])
        o_ref[...] = vjp(d_ref[...])[0]

    return pl.pallas_call(body, name=name, out_shape=jax.ShapeDtypeStruct(lbp.shape, F32))(lbp, dlbs)


def _branch_a_fwd(pa, cw, name):
    t = pa.shape[0]
    blk = _tile(t, 512)
    W = CONV_W

    def body(p_ref, w_ref, y_ref, hc_ref, hx_ref):
        @pl.when(pl.program_id(0) == 0)
        def _():
            hc_ref[...] = jnp.zeros_like(hc_ref)
            hx_ref[...] = jnp.zeros_like(hx_ref)

        ac, ax = p_ref[:, W:2 * W], p_ref[:, 2 * W:3 * W]
        y_ref[...] = _conv_a_block(
            p_ref[:, 0:W], jnp.concatenate([hc_ref[...], ac], axis=0), jnp.concatenate([hx_ref[...], ax], axis=0),
            p_ref[:, 3 * W:4 * W], w_ref[0:1, :], w_ref[1:2, :], w_ref[2:3, :]).astype(BF16)
        hc_ref[...] = p_ref[blk - 8:blk, W:2 * W]
        hx_ref[...] = p_ref[blk - 8:blk, 2 * W:3 * W]

    return pl.pallas_call(
        body, name=name, grid=(t // blk,),
        in_specs=[pl.BlockSpec((blk, NA), lambda i: (i, 0)), pl.BlockSpec((3, W), lambda i: (0, 0))],
        out_specs=pl.BlockSpec((blk, W), lambda i: (i, 0)), out_shape=jax.ShapeDtypeStruct((t, W), BF16),
        scratch_shapes=[pltpu.VMEM((8, W), F32), pltpu.VMEM((8, W), F32)],
        compiler_params=_cparams("arbitrary"))(pa, cw)


def _branch_a_bwd(pa, cw, dy, name):
    t = pa.shape[0]
    blk = _tile(t, 512)
    nt_ = t // blk
    W = CONV_W
    hb = blk // 8

    def body(p_ref, halo_ref, w_ref, dy_ref, dp_ref, dw_ref, chc_ref, chx_ref):
        i = pl.program_id(0)

        @pl.when(i == 0)
        def _():
            chc_ref[...] = jnp.zeros_like(chc_ref)
            chx_ref[...] = jnp.zeros_like(chx_ref)
            dw_ref[...] = jnp.zeros_like(dw_ref)

        keep = 1.0 - (i == nt_ - 1).astype(F32)
        hc = halo_ref[:, W:2 * W] * keep
        hx = halo_ref[:, 2 * W:3 * W] * keep
        ac_ext = jnp.concatenate([hc, p_ref[:, W:2 * W]], axis=0)
        ax_ext = jnp.concatenate([hx, p_ref[:, 2 * W:3 * W]], axis=0)
        _, vjp = jax.vjp(_conv_a_block, p_ref[:, 0:W], ac_ext, ax_ext, p_ref[:, 3 * W:4 * W],
                         w_ref[0:1, :], w_ref[1:2, :], w_ref[2:3, :])
        dab, dac, dax, daz, dw0, dw1, dw2 = vjp(dy_ref[...])
        dp_ref[:, 0:W] = dab.astype(BF16)
        dp_ref[:, W:2 * W] = _add_to_tail(dac[8:], chc_ref[...]).astype(BF16)
        dp_ref[:, 2 * W:3 * W] = _add_to_tail(dax[8:], chx_ref[...]).astype(BF16)
        dp_ref[:, 3 * W:4 * W] = daz.astype(BF16)
        chc_ref[...] = dac[:8] * keep
        chx_ref[...] = dax[:8] * keep
        dw_ref[0:1, :] += dw0
        dw_ref[1:2, :] += dw1
        dw_ref[2:3, :] += dw2

    rev = lambda i: (nt_ - 1 - i, 0)
    return pl.pallas_call(
        body, name=name, grid=(nt_,),
        in_specs=[pl.BlockSpec((blk, NA), rev),
                  pl.BlockSpec((8, NA), lambda i: (jnp.maximum((nt_ - 1 - i) * hb - 1, 0), 0)),
                  pl.BlockSpec((3, W), lambda i: (0, 0)),
                  pl.BlockSpec((blk, W), rev)],
        out_specs=[pl.BlockSpec((blk, NA), rev), pl.BlockSpec((3, W), lambda i: (0, 0))],
        out_shape=[jax.ShapeDtypeStruct((t, NA), BF16), jax.ShapeDtypeStruct((3, W), F32)],
        scratch_shapes=[pltpu.VMEM((8, W), F32), pltpu.VMEM((8, W), F32)],
        compiler_params=_cparams("arbitrary"))(pa, pa, cw, dy)


def _block_rows(t, chunks):
    return min(t, chunks * CHUNK)


def _branch_b_fwd(pb, lbs_row, nw, name, exchange=None):
    t = pb.shape[0]
    rows = _block_rows(t, HGRN_BLOCK_CHUNKS)
    nch = t // rows

    def body(p_ref, lb_ref, nw_ref, y_ref, s_ref, st_ref):
        @pl.when(pl.program_id(1) == 0)
        def _():
            st_ref[...] = jnp.zeros_like(st_ref)

        s_ref[0, 0] = st_ref[...]
        y, st1 = _hgrn_block(p_ref[:, 0:HD], p_ref[:, HD:2 * HD], p_ref[:, 2 * HD:3 * HD], p_ref[:, 3 * HD:4 * HD],
                             st_ref[...], lb_ref[...], nw_ref[...])
        y_ref[...] = y.astype(BF16)
        st_ref[...] = st1

    return _call_with_exchange(
        body, name=name, grid=(HGRN_HEADS, nch),
        in_specs=[pl.BlockSpec((rows, 4 * HD), lambda h, i: (i, h)),
                  pl.BlockSpec((1, HD), lambda h, i: (0, h)),
                  pl.BlockSpec((1, HD), lambda h, i: (0, 0))],
        out_specs=[pl.BlockSpec((rows, HD), lambda h, i: (i, h)),
                   pl.BlockSpec((1, 1, HD, HD), lambda h, i: (h, i, 0, 0))],
        out_shape=[jax.ShapeDtypeStruct((t, HGRN_HEADS * HD), BF16),
                   jax.ShapeDtypeStruct((HGRN_HEADS, nch, HD, HD), F32)],
        scratch_shapes=[pltpu.VMEM((HD, HD), F32)],
        args=(pb, lbs_row, nw), exchange=exchange)


def _branch_b_bwd(pb, states, lbs_row, nw, dy, name, exchange=None):
    t = pb.shape[0]
    rows = _block_rows(t, HGRN_BLOCK_CHUNKS)
    nch = t // rows

    def body(p_ref, s_ref, lb_ref, nw_ref, dy_ref, dp_ref, dlb_ref, dnw_ref, ds_ref):
        h, i = pl.program_id(0), pl.program_id(1)

        @pl.when(i == 0)
        def _():
            ds_ref[...] = jnp.zeros_like(ds_ref)
            dlb_ref[...] = jnp.zeros_like(dlb_ref)

        @pl.when((i == 0) & (h == 0))
        def _():
            dnw_ref[...] = jnp.zeros_like(dnw_ref)

        _, vjp = jax.vjp(_hgrn_block, p_ref[:, 0:HD], p_ref[:, HD:2 * HD], p_ref[:, 2 * HD:3 * HD],
                         p_ref[:, 3 * HD:4 * HD], s_ref[0, 0], lb_ref[...], nw_ref[...])
        dq, df, di, dz, ds0, dlb, dnw = vjp((dy_ref[...], ds_ref[...]))
        dp_ref[:, 0:HD] = dq.astype(BF16)
        dp_ref[:, HD:2 * HD] = df.astype(BF16)
        dp_ref[:, 2 * HD:3 * HD] = di.astype(BF16)
        dp_ref[:, 3 * HD:4 * HD] = dz.astype(BF16)
        ds_ref[...] = ds0
        dlb_ref[...] += dlb
        dnw_ref[...] += dnw

    rev = lambda h, i: (nch - 1 - i, h)
    return _call_with_exchange(
        body, name=name, grid=(HGRN_HEADS, nch),
        in_specs=[pl.BlockSpec((rows, 4 * HD), rev),
                  pl.BlockSpec((1, 1, HD, HD), lambda h, i: (h, nch - 1 - i, 0, 0)),
                  pl.BlockSpec((1, HD), lambda h, i: (0, h)),
                  pl.BlockSpec((1, HD), lambda h, i: (0, 0)),
                  pl.BlockSpec((rows, HD), rev)],
        out_specs=[pl.BlockSpec((rows, 4 * HD), rev),
                   pl.BlockSpec((1, HD), lambda h, i: (0, h)),
                   pl.BlockSpec((1, HD), lambda h, i: (0, 0))],
        out_shape=[jax.ShapeDtypeStruct((t, NB), BF16), jax.ShapeDtypeStruct((1, HGRN_HEADS * HD), F32),
                   jax.ShapeDtypeStruct((1, HD), F32)],
        scratch_shapes=[pltpu.VMEM((HD, HD), F32)],
        args=(pb, states, lbs_row, nw, dy), exchange=exchange)


def _branch_c_fwd(pc, cw, cpar, nw, name, exchange=None):
    t = pc.shape[0]
    rows = _block_rows(t, GDN_BLOCK_CHUNKS)
    nch = t // rows
    XW = 4 * HD

    nsys = 2 * rows // GROUP

    def body(p_ref, w_ref, cp_ref, nw_ref, y_ref, s_ref, x_ref, sa_ref, sb_ref, halo_ref):
        @pl.when(pl.program_id(1) == 0)
        def _():
            sa_ref[...] = jnp.zeros_like(sa_ref)
            sb_ref[...] = jnp.zeros_like(sb_ref)
            halo_ref[...] = jnp.zeros_like(halo_ref)

        s_ref[0, 0, 0] = sa_ref[...]
        s_ref[0, 0, 1] = sb_ref[...]
        x_ext = jnp.concatenate([halo_ref[...], p_ref[:, 0:XW]], axis=0)
        (y, s1a, s1b), xs = _gdn_block(x_ext, p_ref[:, XW:XW + 2 * HD], p_ref[:, XW + 2 * HD:XW + 3 * HD],
                                       sa_ref[...], sb_ref[...], w_ref[0:1, :], w_ref[1:2, :], w_ref[2:3, :], w_ref[3:4, :],
                                       cp_ref[0, 0:1, :], cp_ref[0, 1:2, :], nw_ref[...])
        for n in range(nsys):
            x_ref[0, 0, n] = xs[n]
        y_ref[...] = y.astype(BF16)
        sa_ref[...] = s1a
        sb_ref[...] = s1b
        halo_ref[...] = p_ref[rows - 8:rows, 0:XW]

    return _call_with_exchange(
        body, name=name, grid=(GDN_QK_HEADS, nch),
        in_specs=[pl.BlockSpec((rows, C_HEAD), lambda h, i: (i, h)),
                  pl.BlockSpec((4, XW), lambda h, i: (0, h)),
                  pl.BlockSpec((1, 8, HD), lambda h, i: (h, 0, 0)),
                  pl.BlockSpec((1, HD), lambda h, i: (0, 0))],
        out_specs=[pl.BlockSpec((rows, 2 * HD), lambda h, i: (i, h)),
                   pl.BlockSpec((1, 1, 2, HD, HD), lambda h, i: (h, i, 0, 0, 0)),
                   pl.BlockSpec((1, 1, nsys, GROUP, GROUP), lambda h, i: (h, i, 0, 0, 0))],
        out_shape=[jax.ShapeDtypeStruct((t, 2 * GDN_QK_HEADS * HD), BF16),
                   jax.ShapeDtypeStruct((GDN_QK_HEADS, nch, 2, HD, HD), F32),
                   jax.ShapeDtypeStruct((GDN_QK_HEADS, nch, nsys, GROUP, GROUP), BF16)],
        scratch_shapes=[pltpu.VMEM((HD, HD), F32), pltpu.VMEM((HD, HD), F32), pltpu.VMEM((8, XW), F32)],
        args=(pc, cw, cpar, nw), exchange=exchange)


def _branch_c_bwd(pc, states, inverses, cw, cpar, nw, dy, name, exchange=None):
    t = pc.shape[0]
    rows = _block_rows(t, GDN_BLOCK_CHUNKS)
    nch = t // rows
    XW = 4 * HD
    hb = rows // 8

    nsys = 2 * rows // GROUP

    def body(p_ref, halo_ref, s_ref, x_ref, w_ref, cp_ref, nw_ref, dy_ref, dp_ref, dw_ref, dcp_ref, dnw_ref,
             dsa_ref, dsb_ref, carry_ref):
        h, i = pl.program_id(0), pl.program_id(1)

        @pl.when(i == 0)
        def _():
            dsa_ref[...] = jnp.zeros_like(dsa_ref)
            dsb_ref[...] = jnp.zeros_like(dsb_ref)
            carry_ref[...] = jnp.zeros_like(carry_ref)
            dw_ref[...] = jnp.zeros_like(dw_ref)
            dcp_ref[...] = jnp.zeros_like(dcp_ref)

        @pl.when((i == 0) & (h == 0))
        def _():
            dnw_ref[...] = jnp.zeros_like(dnw_ref)

        keep = 1.0 - (i == nch - 1).astype(F32)
        x_ext = jnp.concatenate([halo_ref[:, 0:XW] * keep, p_ref[:, 0:XW]], axis=0)
        block = functools.partial(_gdn_block, known=tuple(x_ref[0, 0, n] for n in range(nsys)))
        _, vjp, _ = jax.vjp(block, x_ext, p_ref[:, XW:XW + 2 * HD], p_ref[:, XW + 2 * HD:XW + 3 * HD],
                            s_ref[0, 0, 0], s_ref[0, 0, 1], w_ref[0:1, :], w_ref[1:2, :], w_ref[2:3, :], w_ref[3:4, :],
                            cp_ref[0, 0:1, :], cp_ref[0, 1:2, :], nw_ref[...], has_aux=True)
        dx, dz, dba, dsa, dsb, dw0, dw1, dw2, dw3, dal, ddt, dnw = vjp((dy_ref[...], dsa_ref[...], dsb_ref[...]))
        dp_ref[:, 0:XW] = _add_to_tail(dx[8:], carry_ref[...]).astype(BF16)
        dp_ref[:, XW:XW + 2 * HD] = dz.astype(BF16)
        dp_ref[:, XW + 2 * HD:XW + 3 * HD] = dba.astype(BF16)
        carry_ref[...] = dx[:8] * keep
        dsa_ref[...] = dsa
        dsb_ref[...] = dsb
        dw_ref[0:1, :] += dw0
        dw_ref[1:2, :] += dw1
        dw_ref[2:3, :] += dw2
        dw_ref[3:4, :] += dw3
        dcp_ref[0, 0:1, :] += dal
        dcp_ref[0, 1:2, :] += ddt
        dnw_ref[...] += dnw

    rev = lambda h, i: (nch - 1 - i, h)
    return _call_with_exchange(
        body, name=name, grid=(GDN_QK_HEADS, nch),
        in_specs=[pl.BlockSpec((rows, C_HEAD), rev),
                  pl.BlockSpec((8, C_HEAD), lambda h, i: (jnp.maximum((nch - 1 - i) * hb - 1, 0), h)),
                  pl.BlockSpec((1, 1, 2, HD, HD), lambda h, i: (h, nch - 1 - i, 0, 0, 0)),
                  pl.BlockSpec((1, 1, nsys, GROUP, GROUP), lambda h, i: (h, nch - 1 - i, 0, 0, 0)),
                  pl.BlockSpec((4, XW), lambda h, i: (0, h)),
                  pl.BlockSpec((1, 8, HD), lambda h, i: (h, 0, 0)),
                  pl.BlockSpec((1, HD), lambda h, i: (0, 0)),
                  pl.BlockSpec((rows, 2 * HD), rev)],
        out_specs=[pl.BlockSpec((rows, C_HEAD), rev),
                   pl.BlockSpec((4, XW), lambda h, i: (0, h)),
                   pl.BlockSpec((1, 8, HD), lambda h, i: (h, 0, 0)),
                   pl.BlockSpec((1, HD), lambda h, i: (0, 0))],
        out_shape=[jax.ShapeDtypeStruct((t, NC_COLS), BF16), jax.ShapeDtypeStruct((4, GDN_QK_HEADS * XW), F32),
                   jax.ShapeDtypeStruct((GDN_QK_HEADS, 8, HD), F32), jax.ShapeDtypeStruct((1, HD), F32)],
        scratch_shapes=[pltpu.VMEM((HD, HD), F32), pltpu.VMEM((HD, HD), F32), pltpu.VMEM((8, XW), F32)],
        args=(pc, pc, states, inverses, cw, cpar, nw, dy), exchange=exchange)


def _merge_fwd(pg, bg, ya, yb, yc, name):
    t = pg.shape[0]
    blk = _tile(t, 256)

    def body(g_ref, b_ref, a_ref, b2_ref, c_ref, o_ref):
        gate = _sigmoid(g_ref[...] + b_ref[...])
        o_ref[...] = (gate[:, 0:D] * a_ref[...] + gate[:, D:2 * D] * b2_ref[...] + gate[:, 2 * D:3 * D] * c_ref[...]).astype(BF16)

    row = pl.BlockSpec((blk, D), lambda i: (i, 0))
    return pl.pallas_call(
        body, name=name, grid=(t // blk,),
        in_specs=[pl.BlockSpec((blk, NG), lambda i: (i, 0)), pl.BlockSpec((1, NG), lambda i: (0, 0)), row, row, row],
        out_specs=row, out_shape=jax.ShapeDtypeStruct((t, D), BF16),
        compiler_params=_cparams("parallel"))(pg, bg, ya, yb, yc)


def _merge_bwd(dm, pg, bg, ya, yb, yc, name):
    t = pg.shape[0]
    blk = _tile(t, 256)

    def body(dm_ref, g_ref, b_ref, a_ref, b2_ref, c_ref, dg_ref, da_ref, db_ref, dc_ref, dbg_ref):
        @pl.when(pl.program_id(0) == 0)
        def _():
            dbg_ref[...] = jnp.zeros_like(dbg_ref)

        gate = _sigmoid(g_ref[...] + b_ref[...])
        dmv = dm_ref[...]
        for j, (y_ref, dy_ref) in enumerate(((a_ref, da_ref), (b2_ref, db_ref), (c_ref, dc_ref))):
            gj = gate[:, j * D:(j + 1) * D]
            dy_ref[...] = (dmv * gj).astype(BF16)
            dgj = dmv * y_ref[...] * gj * (1.0 - gj)
            dg_ref[:, j * D:(j + 1) * D] = dgj.astype(BF16)
            dbg_ref[:, j * D:(j + 1) * D] += jnp.sum(dgj, axis=0, keepdims=True)

    row = pl.BlockSpec((blk, D), lambda i: (i, 0))
    wide = pl.BlockSpec((blk, NG), lambda i: (i, 0))
    vec = pl.BlockSpec((1, NG), lambda i: (0, 0))
    return pl.pallas_call(
        body, name=name, grid=(t // blk,), in_specs=[row, wide, vec, row, row, row],
        out_specs=[wide, row, row, row, vec],
        out_shape=[jax.ShapeDtypeStruct((t, NG), BF16)] + [jax.ShapeDtypeStruct((t, D), BF16)] * 3
                  + [jax.ShapeDtypeStruct((1, NG), F32)],
        compiler_params=_cparams("arbitrary"))(dm, pg, bg, ya, yb, yc)


def _adamw_math(w, g, m, v):
    m = ADAM_B1 * m + (1.0 - ADAM_B1) * g
    v = ADAM_B2 * v + (1.0 - ADAM_B2) * (g * g)
    m_hat = m / (1.0 - ADAM_B1 ** ADAM_STEP)
    v_hat = v / (1.0 - ADAM_B2 ** ADAM_STEP)
    delta = -ADAM_LR * (m_hat / (jnp.sqrt(v_hat) + ADAM_EPS) + ADAM_WD * w)
    return delta, m, v


def _sum_adamw(parts, w, m, v, name):
    r, c = w.shape
    br = r if r <= 256 else 256
    assert r % br == 0

    def body(p_ref, w_ref, m_ref, v_ref, g_ref, d_ref, nm_ref, nv_ref):
        g = p_ref[0].astype(F32)
        for k in range(1, N_DEV):
            g = g + p_ref[k].astype(F32)
        g_ref[...] = g
        d_ref[...], nm_ref[...], nv_ref[...] = _adamw_math(w_ref[...], g, m_ref[...], v_ref[...])

    blk = pl.BlockSpec((br, c), lambda i: (i, 0))
    return pl.pallas_call(
        body, name=name, grid=(r // br,),
        in_specs=[pl.BlockSpec((N_DEV, br, c), lambda i: (0, i, 0)), blk, blk, blk], out_specs=[blk] * 4,
        out_shape=[jax.ShapeDtypeStruct((r, c), F32)] * 4, compiler_params=_cparams("parallel"))(parts, w, m, v)


def _adamw(g, w, m, v, name):
    def body(g_ref, w_ref, m_ref, v_ref, d_ref, nm_ref, nv_ref):
        d_ref[...], nm_ref[...], nv_ref[...] = _adamw_math(w_ref[...], g_ref[...], m_ref[...], v_ref[...])

    return pl.pallas_call(body, name=name, out_shape=[jax.ShapeDtypeStruct(w.shape, F32)] * 3)(g, w, m, v)


def _sum_slots(parts, name):
    def body(p_ref, o_ref):
        g = p_ref[0]
        for k in range(1, N_DEV):
            g = g + p_ref[k]
        o_ref[...] = g

    return pl.pallas_call(body, name=name, out_shape=jax.ShapeDtypeStruct(parts.shape[1:], F32))(parts)


def _exchange(srcs, name, broadcast):
    n = len(srcs)

    def body(*refs):
        copies = _exchange_copies(refs[:n], refs[n:2 * n], *refs[2 * n:], broadcast)
        for cp in copies:
            cp.start()
        for cp in copies:
            cp.wait()

    return pl.pallas_call(
        body, name=name, in_specs=[HBM_SPEC] * n, out_specs=[HBM_SPEC] * n, out_shape=_exchange_shapes(srcs, broadcast),
        scratch_shapes=_exchange_semaphores(n))(*srcs)


def _gather_two_level(srcs, name):
    n = len(srcs)

    def body(*refs):
        src_refs, dst_refs = refs[:n], refs[n:2 * n]
        send_sems, recv_sems, local_sems = refs[2 * n:]
        x, y, c = lax.axis_index("x"), lax.axis_index("y"), lax.axis_index("c")
        index_of = lambda px, py, pc: 4 * px + 2 * py + pc
        me, other_core = index_of(x, y, c), (x, y, 1 - c)
        chips = [(1 - x, y), (x, 1 - y), (1 - x, 1 - y)]

        def copy(k, a, block, to, src=None):
            return pltpu.make_async_remote_copy(
                src_ref=dst_refs[a].at[block] if src is None else src, dst_ref=dst_refs[a].at[block],
                send_sem=send_sems.at[k, a], recv_sem=recv_sems.at[k, a], device_id=to, device_id_type=MESH)

        local = [pltpu.make_async_copy(src_refs[a], dst_refs[a].at[me], local_sems.at[a]) for a in range(n)]
        first = [copy(0, a, me, other_core, src=src_refs[a]) for a in range(n)]
        first += [copy(1 + j, a, me, (*chip, c), src=src_refs[a]) for j, chip in enumerate(chips) for a in range(n)]
        for cp in local + first:
            cp.start()
        passed = []
        for j, chip in enumerate(chips):
            block = index_of(*chip, c)
            for a in range(n):
                copy(1 + j, a, block, (x, y, c)).wait_recv()
            for a in range(n):
                passed.append(copy(4 + j, a, block, other_core))
                passed[-1].start()
        for a in range(n):
            copy(0, a, index_of(x, y, 1 - c), (x, y, c)).wait_recv()
        for j, chip in enumerate(chips):
            for a in range(n):
                copy(4 + j, a, index_of(*chip, 1 - c), (x, y, c)).wait_recv()
        for cp in first + passed:
            cp.wait_send()
        for cp in local:
            cp.wait()

    return pl.pallas_call(
        body, name=name, in_specs=[HBM_SPEC] * n, out_specs=[HBM_SPEC] * n, out_shape=_exchange_shapes(srcs, True),
        scratch_shapes=_exchange_semaphores(n))(*srcs)


HBM_SPEC = pl.BlockSpec(memory_space=pltpu.HBM)


def _exchange_shapes(srcs, broadcast):
    return [jax.ShapeDtypeStruct((N_DEV,) + (s.shape if broadcast else s.shape[1:]), s.dtype) for s in srcs]


def _exchange_semaphores(n):
    return [pltpu.SemaphoreType.DMA((N_DEV - 1, n)), pltpu.SemaphoreType.DMA((N_DEV - 1, n)), pltpu.SemaphoreType.DMA((n,))]


def _exchange_copies(src_refs, dst_refs, send_sems, recv_sems, local_sems, broadcast):
    x, y, c = lax.axis_index("x"), lax.axis_index("y"), lax.axis_index("c")
    me = 4 * x + 2 * y + c
    copies = []
    for k in range(1, N_DEV):
        px = 1 - x if (k >> 2) & 1 else x
        py = 1 - y if (k >> 1) & 1 else y
        pc = 1 - c if k & 1 else c
        peer = 4 * px + 2 * py + pc
        for a, (src, dst) in enumerate(zip(src_refs, dst_refs)):
            copies.append(pltpu.make_async_remote_copy(
                src_ref=src if broadcast else src.at[peer], dst_ref=dst.at[me],
                send_sem=send_sems.at[k - 1, a], recv_sem=recv_sems.at[k - 1, a],
                device_id=(px, py, pc), device_id_type=MESH))
    for a, (src, dst) in enumerate(zip(src_refs, dst_refs)):
        copies.append(pltpu.make_async_copy(src if broadcast else src.at[me], dst.at[me], local_sems.at[a]))
    return copies


def _call_with_exchange(body, *, name, grid, in_specs, out_specs, out_shape, scratch_shapes, args, exchange):
    if exchange is None:
        outs = pl.pallas_call(body, name=name, grid=grid, in_specs=in_specs, out_specs=out_specs, out_shape=out_shape,
                              scratch_shapes=scratch_shapes,
                              compiler_params=_cparams(*["arbitrary"] * len(grid)))(*args)
        return outs, None
    srcs, broadcast = exchange
    n, n_in, n_out, n_scr = len(srcs), len(args), len(out_shape), len(scratch_shapes)
    steps = 1
    for g in grid:
        steps *= g

    def hosted(*refs):
        ins, src_refs = refs[:n_in], refs[n_in:n_in + n]
        outs, dst_refs = refs[n_in + n:n_in + n + n_out], refs[n_in + n + n_out:n_in + 2 * n + n_out]
        scratch = refs[n_in + 2 * n + n_out:]
        step = pl.program_id(0)
        for axis in range(1, len(grid)):
            step = step * grid[axis] + pl.program_id(axis)

        @pl.when(step == 0)
        def _():
            for cp in _exchange_copies(src_refs, dst_refs, *scratch[n_scr:], broadcast):
                cp.start()

        body(*ins, *outs, *scratch[:n_scr])

        @pl.when(step == steps - 1)
        def _():
            for cp in _exchange_copies(src_refs, dst_refs, *scratch[n_scr:], broadcast):
                cp.wait()

    outs = pl.pallas_call(
        hosted, name=name, grid=grid, in_specs=list(in_specs) + [HBM_SPEC] * n, out_specs=list(out_specs) + [HBM_SPEC] * n,
        out_shape=list(out_shape) + _exchange_shapes(srcs, broadcast),
        scratch_shapes=list(scratch_shapes) + _exchange_semaphores(n),
        compiler_params=_cparams(*["arbitrary"] * len(grid)))(*args, *srcs)
    return outs[:n_out], outs[n_out:]


def _regroup_w_in(w):
    wa = w[:, OFF_A:OFF_A + NA]
    seg = lambda off, h, n=HD: w[:, off + h * n: off + (h + 1) * n]
    wb = jnp.concatenate([seg(OFF_B + s * 512, h) for h in range(HGRN_HEADS) for s in range(4)], axis=1)
    parts = []
    for h in range(GDN_QK_HEADS):
        small = jnp.concatenate(
            [w[:, OFF_BETA + 2 * h: OFF_BETA + 2 * h + 2], w[:, OFF_CA + 2 * h: OFF_CA + 2 * h + 2],
             jnp.zeros((w.shape[0], HD - 4), w.dtype)], axis=1)
        parts += [seg(OFF_CQ, h), seg(OFF_CK, h), seg(OFF_CV, h, 2 * HD), seg(OFF_CZ, h, 2 * HD), small]
    wc = jnp.concatenate(parts, axis=1)
    wg = w[:, OFF_G:OFF_G + NG]
    return wa, wb, wc, wg


def _ungroup_dw_in(da, db, dc, dg):
    bq = [jnp.concatenate([db[:, h * 512 + s * HD: h * 512 + (s + 1) * HD] for h in range(HGRN_HEADS)], axis=1)
          for s in range(4)]
    ch = lambda h, lo, hi: dc[:, h * C_HEAD + lo: h * C_HEAD + hi]
    heads = range(GDN_QK_HEADS)
    cq = jnp.concatenate([ch(h, 0, HD) for h in heads], axis=1)
    ck = jnp.concatenate([ch(h, HD, 2 * HD) for h in heads], axis=1)
    cv = jnp.concatenate([ch(h, 2 * HD, 4 * HD) for h in heads], axis=1)
    cz = jnp.concatenate([ch(h, 4 * HD, 6 * HD) for h in heads], axis=1)
    cbeta = jnp.concatenate([ch(h, 6 * HD, 6 * HD + 2) for h in heads], axis=1)
    ca = jnp.concatenate([ch(h, 6 * HD + 2, 6 * HD + 4) for h in heads], axis=1)
    return jnp.concatenate([da] + bq + [cq, ck, cv, cbeta, ca, cz, dg], axis=1)


def _regroup_conv_c(cw):
    parts = []
    for h in range(GDN_QK_HEADS):
        parts += [cw[:, h * HD:(h + 1) * HD], cw[:, 512 + h * HD: 512 + (h + 1) * HD],
                  cw[:, 1024 + 2 * h * HD: 1024 + (2 * h + 2) * HD]]
    return jnp.concatenate(parts, axis=1)


def _ungroup_conv_c(d):
    heads = range(GDN_QK_HEADS)
    q = jnp.concatenate([d[:, h * 512: h * 512 + HD] for h in heads], axis=1)
    k = jnp.concatenate([d[:, h * 512 + HD: h * 512 + 2 * HD] for h in heads], axis=1)
    v = jnp.concatenate([d[:, h * 512 + 2 * HD: h * 512 + 4 * HD] for h in heads], axis=1)
    return jnp.concatenate([q, k, v], axis=1)


def _numel(shape):
    n = 1
    for d in shape:
        n *= d
    return n


def _pack(arrays, rows):
    flat = jnp.concatenate([a.reshape(-1) for a in arrays])
    return jnp.pad(flat, (0, rows * 128 - flat.shape[0])).reshape(rows, 128)


def _unpack(packed, shapes):
    flat = packed.reshape(-1)
    out, off = [], 0
    for s in shapes:
        out.append(flat[off:off + _numel(s)].reshape(s))
        off += _numel(s)
    return out


def _rows_for(shapes):
    return -(-sum(_numel(s) for s in shapes) // 1024) * 8


def kernel(x, norm_w, w_in, b_gate, conv_a, conv_c, a_log, dt_bias, lower_bounds, hgrn_norm_w, gdn_norm_w, w_out_a, w_out_b, w_out_c, w_o, final_norm_w, loss_target, m_norm_w, m_w_in, m_b_gate, m_conv_a, m_conv_c, m_a_log, m_dt_bias, m_lower_bounds, m_hgrn_norm_w, m_gdn_norm_w, m_w_out_a, m_w_out_b, m_w_out_c, m_w_o, m_final_norm_w, v_norm_w, v_w_in, v_b_gate, v_conv_a, v_conv_c, v_a_log, v_dt_bias, v_lower_bounds, v_hgrn_norm_w, v_gdn_norm_w, v_w_out_a, v_w_out_b, v_w_out_c, v_w_o, v_final_norm_w):
    me = 4 * lax.axis_index("x") + 2 * lax.axis_index("y") + lax.axis_index("c")
    xs = x[0]
    target = loss_target[0]
    in_shard = w_in.shape[2]

    big = [w_in, w_out_a, w_out_b, w_out_c, w_o]
    shards_of = lambda l: [w[l].astype(BF16) for w in big]
    conv_shapes = [(DEPTH, 3, CONV_W), (DEPTH, 4, 2048)]
    conv_rows = _rows_for(conv_shapes)
    ca_full = lax.dynamic_update_slice(jnp.zeros(conv_shapes[0], F32), conv_a, (0, 0, me * conv_a.shape[2]))
    cc_full = lax.dynamic_update_slice(jnp.zeros(conv_shapes[1], F32), conv_c, (0, 0, me * conv_c.shape[2]))
    conv_parts, = _exchange([_pack([ca_full, cc_full], conv_rows)], "gather_conv", broadcast=True)
    conv_a_full, conv_c_full = _unpack(_sum_slots(conv_parts, "sum_conv"), conv_shapes)

    lb_pad = jnp.pad(lower_bounds, ((0, 8 - DEPTH), (0, 0)))
    lbs = _lower_bounds_fwd(lb_pad, "lower_bounds_fwd")

    def input_weights(l, g_in):
        wa, wb, wc, wg = _regroup_w_in(jnp.concatenate([g_in[q] for q in range(N_DEV)], axis=1))
        lanes = lambda vec: jnp.pad(vec.reshape(GDN_QK_HEADS, 1, 2), ((0, 0), (0, 0), (0, HD - 2)))
        cpar = jnp.concatenate([lanes(a_log[l]), lanes(dt_bias[l]), jnp.zeros((GDN_QK_HEADS, 6, HD), F32)], axis=1)
        return dict(
            wa=wa, wb=wb, wc=wc, wg=wg, cpar=cpar,
            nw=norm_w[l:l + 1], bg=b_gate[l:l + 1], cwa=conv_a_full[l], cwc=_regroup_conv_c(conv_c_full[l]),
            lb=lbs[l:l + 1], hnw=hgrn_norm_w[l:l + 1], gnw=gdn_norm_w[l:l + 1])

    def output_weights(g_oa, g_ob, g_oc, g_o):
        return dict(woa=jnp.concatenate([g_oa[q] for q in range(N_DEV)], axis=1),
                    wob=jnp.concatenate([g_ob[q] for q in range(N_DEV)], axis=1), woc=g_oc.reshape(D, D), wo=g_o.reshape(D, D))

    layers = [input_weights(0, _gather_two_level(shards_of(0)[:1], "gather_l0")[0])]

    saved = []
    cur = xs
    for l in range(DEPTH):
        L = layers[l]
        n = f"l{l}_"
        h = _rmsnorm_fwd(cur, L["nw"], n + "rms")
        pa = _matmul(h, L["wa"], "nn", n + "proj_a")
        pb = _matmul(h, L["wb"], "nn", n + "proj_b")
        pc = _matmul(h, L["wc"], "nn", n + "proj_c")
        pg = _matmul(h, L["wg"], "nn", n + "proj_g")
        ua = _branch_a_fwd(pa, L["cwa"], n + "conv_fwd")
        carry = (shards_of(l)[1:], True) if l == 0 else None
        (ub, sb), gathered = _branch_b_fwd(pb, L["lb"], L["hnw"], n + "hgrn_fwd", exchange=carry)
        if carry is not None:
            L.update(output_weights(*gathered))
        carry = (shards_of(l + 1), True) if l + 1 < DEPTH else None
        (uc, sc, xc), gathered = _branch_c_fwd(pc, L["cwc"], L["cpar"], L["gnw"], n + "gdn_fwd", exchange=carry)
        if carry is not None:
            layers.append(dict(input_weights(l + 1, gathered[0]), **output_weights(*gathered[1:])))
        ya = _matmul(ua, L["woa"], "nn", n + "out_a")
        yb = _matmul(ub, L["wob"], "nn", n + "out_b")
        yc = _matmul(uc, L["woc"], "nn", n + "out_c")
        merged = _merge_fwd(pg, L["bg"], ya, yb, yc, n + "merge")
        nxt = _matmul(merged, L["wo"], "nn", n + "out_o", residual=cur)
        saved.append(dict(x=cur, h=h, pa=pa, pb=pb, pc=pc, pg=pg, ua=ua, ub=ub, uc=uc, sb=sb, sc=sc, xc=xc,
                          ya=ya, yb=yb, yc=yc, merged=merged))
        cur = nxt

    loss_part, dx, d_final = _loss_head(cur, final_norm_w.reshape(1, D), target, "loss_head")

    def outgoing(g):
        cols = lambda a, n: jnp.stack([a[:, p * n:(p + 1) * n] for p in range(N_DEV)]).astype(BF16)
        rows = lambda a: a.reshape(N_DEV, a.shape[0] // N_DEV, a.shape[1]).astype(BF16)
        first = [cols(g["w_in"], in_shard)] if "w_in" in g else [None]
        if "w_o" not in g:
            return first
        return first + [cols(g["w_out_a"], 128), cols(g["w_out_b"], 128), rows(g["w_out_c"]), rows(g["w_o"])]

    grads = [None] * DEPTH
    dlbs_rows = [None] * DEPTH
    incoming = [None] * DEPTH
    for l in reversed(range(DEPTH)):
        L, S = layers[l], saved[l]
        n = f"l{l}_"
        dmerged = _matmul(dx, L["wo"], "nt", n + "d_merged")
        d_wo = _matmul(S["merged"], dx, "tn", n + "dw_o", out_dtype=BF16)
        dpg, dya, dyb, dyc, d_bg = _merge_bwd(dmerged, S["pg"], L["bg"], S["ya"], S["yb"], S["yc"], n + "merge_bwd")
        dua = _matmul(dya, L["woa"], "nt", n + "d_ua")
        dub = _matmul(dyb, L["wob"], "nt", n + "d_ub")
        duc = _matmul(dyc, L["woc"], "nt", n + "d_uc")
        d_woa = _matmul(S["ua"], dya, "tn", n + "dw_out_a", out_dtype=BF16)
        d_wob = _matmul(S["ub"], dyb, "tn", n + "dw_out_b", out_dtype=BF16)
        d_woc = _matmul(S["uc"], dyc, "tn", n + "dw_out_c", out_dtype=BF16)
        dpa, d_cwa = _branch_a_bwd(S["pa"], L["cwa"], dua, n + "conv_bwd")
        out_grads = dict(w_out_a=d_woa, w_out_b=d_wob, w_out_c=d_woc, w_o=d_wo)
        carry = (outgoing(out_grads)[1:], False) if l == 0 else None
        (dpb, d_lb, d_hnw), arrived_out = _branch_b_bwd(S["pb"], S["sb"], L["lb"], L["hnw"], dub, n + "hgrn_bwd", exchange=carry)
        carry = (outgoing(grads[l + 1]), False) if l + 1 < DEPTH else None
        (dpc, d_cwc, d_cpar, d_gnw), arrived = _branch_c_bwd(S["pc"], S["sc"], S["xc"], L["cwc"], L["cpar"], L["gnw"], duc,
                                                             n + "gdn_bwd", exchange=carry)
        if carry is not None:
            incoming[l + 1] = arrived
        d_win = _ungroup_dw_in(*[_matmul(S["h"], dp, "tn", f"{n}dw_{piece}", out_dtype=BF16)
                                 for dp, piece in ((dpa, "a"), (dpb, "b"), (dpc, "c"), (dpg, "g"))])
        carry = (outgoing(dict(w_in=d_win)), False) if l == 0 else None
        dh, arrived_in = _matmul_nt_sum([(dpa, L["wa"]), (dpb, L["wb"]), (dpc, L["wc"]), (dpg, L["wg"])], n + "dh",
                                        exchange=carry)
        (dx, d_nw), _ = _rmsnorm_bwd(dh, S["x"], L["nw"], dx, n + "rms_bwd")
        dlbs_rows[l] = d_lb
        grads[l] = dict(w_in=d_win, w_out_a=d_woa, w_out_b=d_wob, w_out_c=d_woc, w_o=d_wo, norm_w=d_nw[0],
                        b_gate=d_bg[0], conv_a=d_cwa, conv_c=_ungroup_conv_c(d_cwc),
                        a_log=d_cpar[:, 0, 0:2].reshape(-1), dt_bias=d_cpar[:, 1, 0:2].reshape(-1),
                        hgrn_norm_w=d_hnw[0], gdn_norm_w=d_gnw[0])
    grad_x = dx[None]
    d_lower = _lower_bounds_bwd(lb_pad, jnp.pad(jnp.concatenate(dlbs_rows, axis=0), ((0, 8 - DEPTH), (0, 0))),
                                "lower_bounds_bwd")[:DEPTH]

    incoming[0] = list(arrived_in) + list(arrived_out)
    stack = lambda name: jnp.stack([grads[l][name] for l in range(DEPTH)])
    big_out = {}
    for j, (name, w, m, v) in enumerate((("w_in", w_in, m_w_in, v_w_in), ("w_out_a", w_out_a, m_w_out_a, v_w_out_a),
                                         ("w_out_b", w_out_b, m_w_out_b, v_w_out_b), ("w_out_c", w_out_c, m_w_out_c, v_w_out_c),
                                         ("w_o", w_o, m_w_o, v_w_o))):
        parts = jnp.concatenate([incoming[l][j] for l in range(DEPTH)], axis=1)
        r2 = lambda a: a.reshape(parts.shape[1], parts.shape[2])
        outs = _sum_adamw(parts, r2(w), r2(m), r2(v), "adamw_" + name)
        big_out[name] = [o.reshape(w.shape) for o in outs]

    small_names = ["norm_w", "b_gate", "conv_a", "conv_c", "a_log", "dt_bias", "lower_bounds", "hgrn_norm_w",
                   "gdn_norm_w", "final_norm_w", "loss"]
    small_vals = {k: stack(k) for k in ("norm_w", "b_gate", "conv_a", "conv_c", "a_log", "dt_bias", "hgrn_norm_w", "gdn_norm_w")}
    small_vals.update(lower_bounds=d_lower, final_norm_w=d_final[0], loss=loss_part.reshape(1))
    small_shapes = [small_vals[k].shape for k in small_names]
    small_rows = _rows_for(small_shapes)
    small_parts, = _exchange([_pack([small_vals[k] for k in small_names], small_rows)], "exchange_small", broadcast=True)
    total = dict(zip(small_names, _unpack(_sum_slots(small_parts, "sum_small"), small_shapes)))
    loss = total["loss"][0]
    g_conv_a = lax.dynamic_slice(total["conv_a"], (0, 0, me * conv_a.shape[2]), conv_a.shape)
    g_conv_c = lax.dynamic_slice(total["conv_c"], (0, 0, me * conv_c.shape[2]), conv_c.shape)

    small_w = dict(norm_w=(norm_w, m_norm_w, v_norm_w), b_gate=(b_gate, m_b_gate, v_b_gate),
                   conv_a=(conv_a, m_conv_a, v_conv_a), conv_c=(conv_c, m_conv_c, v_conv_c),
                   a_log=(a_log, m_a_log, v_a_log), dt_bias=(dt_bias, m_dt_bias, v_dt_bias),
                   lower_bounds=(lower_bounds, m_lower_bounds, v_lower_bounds),
                   hgrn_norm_w=(hgrn_norm_w, m_hgrn_norm_w, v_hgrn_norm_w), gdn_norm_w=(gdn_norm_w, m_gdn_norm_w, v_gdn_norm_w),
                   final_norm_w=(final_norm_w, m_final_norm_w, v_final_norm_w))
    small_g = dict(total, conv_a=g_conv_a, conv_c=g_conv_c)
    upd_names = small_names[:-1]
    upd_shapes = [small_w[k][0].shape for k in upd_names]
    upd_rows = _rows_for(upd_shapes)
    pk = lambda j: _pack([small_w[k][j] for k in upd_names], upd_rows)
    s_delta, s_m, s_v = _adamw(_pack([small_g[k] for k in upd_names], upd_rows), pk(0), pk(1), pk(2), "adamw_small")
    small_out = {k: [small_g[k], d, mm, vv] for k, d, mm, vv in
                 zip(upd_names, _unpack(s_delta, upd_shapes), _unpack(s_m, upd_shapes), _unpack(s_v, upd_shapes))}

    order = ["norm_w", "w_in", "b_gate", "conv_a", "conv_c", "a_log", "dt_bias", "lower_bounds", "hgrn_norm_w",
             "gdn_norm_w", "w_out_a", "w_out_b", "w_out_c", "w_o", "final_norm_w"]
    res = {**small_out, **big_out}
    outs = [loss, grad_x]
    for j in range(4):
        outs += [res[k][j] for k in order]
    return tuple(outs)
```

```python
import functools

import jax
import jax.numpy as jnp
from jax import lax
from jax.experimental import pallas as pl
from jax.experimental.pallas import tpu as pltpu

F32 = jnp.float32
BF16 = jnp.bfloat16
MESH = pl.DeviceIdType.MESH

N_DEV = 8
D = 1024
DEPTH = 2
CHUNK = 64
HGRN_BLOCK_CHUNKS = 16
GDN_BLOCK_CHUNKS = 8
GROUP = 128
NORM_EPS = 1e-6
L2_EPS = 1e-6
MIN_F = 1e-30
HD = 128
HGRN_HEADS = 4
GDN_QK_HEADS = 4
CONV_W = 512
IN_COLS = 10256
OFF_A, OFF_B, OFF_CQ, OFF_CK, OFF_CV, OFF_BETA, OFF_CA, OFF_CZ, OFF_G = (
    0, 2048, 4096, 4608, 5120, 6144, 6152, 6160, 7184)
NA, NB, NC_COLS, NG = 2048, 2048, 3584, 3072
C_HEAD = 896

ADAM_LR, ADAM_B1, ADAM_B2, ADAM_EPS, ADAM_WD, ADAM_STEP = 0.001, 0.9, 0.999, 1e-08, 0.01, 10

VMEM_LIMIT = 56 * 1024 * 1024
MM_TILE = 1024


def _cparams(*sem):
    return pltpu.CompilerParams(dimension_semantics=sem, vmem_limit_bytes=VMEM_LIMIT)


def _tile(dim, cap):
    if dim <= cap:
        return dim
    t = (cap // 128) * 128
    while dim % t:
        t -= 128
    return t


def _sigmoid(x):
    return 1.0 / (1.0 + jnp.exp(-x))


def _silu(x):
    return x * _sigmoid(x)


def _softplus(x):
    return jnp.maximum(x, 0.0) + jnp.log(1.0 + jnp.exp(-jnp.abs(x)))


def _dot(a, b, dims, precision=None):
    if precision is None:
        a, b = a.astype(BF16), b.astype(BF16)
    return lax.dot_general(a, b, (dims, ((), ())), precision=precision, preferred_element_type=F32)


def _nn(a, b, precision=None):
    return _dot(a, b, ((1,), (0,)), precision)


def _nt(a, b, precision=None):
    return _dot(a, b, ((1,), (1,)), precision)


def _tn(a, b, precision=None):
    return _dot(a, b, ((0,), (0,)), precision)


def _sum_rows_split(mat01, x):
    m = mat01.astype(BF16)
    hi = x.astype(BF16)
    low = (x - hi.astype(F32)).astype(BF16)
    return _nn(m, hi) + _nn(m, low)


@functools.partial(jax.custom_vjp, nondiff_argnums=(1,))
def _shift_rows(x, d):
    return x if d == 0 else pltpu.roll(x, d, 0)


def _shift_rows_fwd(x, d):
    return _shift_rows(x, d), None


def _shift_rows_bwd(d, _, ct):
    return ((ct if d == 0 else pltpu.roll(ct, ct.shape[0] - d, 0)),)


_shift_rows.defvjp(_shift_rows_fwd, _shift_rows_bwd)


def _iota2(shape):
    return lax.broadcasted_iota(jnp.int32, shape, 0), lax.broadcasted_iota(jnp.int32, shape, 1)


def _lane_pick(x, i):
    lane = lax.broadcasted_iota(jnp.int32, x.shape, 1)
    return jnp.sum(jnp.where(lane == i, x, 0.0), axis=1, keepdims=True)


def _hgrn_block(qr, fr, ir, zr, st0, lb, nw):
    rows = qr.shape[0]
    r, c = _iota2((CHUNK, CHUNK))
    halves = [1 << j for j in range(CHUNK.bit_length() - 1)]
    mats = [c <= r, c > r]
    pairs = []
    for hb in halves:
        same = (r // hb) == (c // hb)
        if hb > 1:
            mats += [(c <= r) & same, (c > r) & same]
        pairs.append(((r // (2 * hb)) == (c // (2 * hb))) & ((r // hb) == (c // hb) + 1))
    stack = jnp.concatenate([m.astype(F32) for m in mats], axis=0)

    q = _silu(qr) * (HD ** -0.5)
    fg = lb + (1.0 - lb) * _sigmoid(fr)
    logf = jnp.log(jnp.maximum(fg, MIN_F))
    kk = 1.0 - fg
    v = ir

    chunks = [slice(s, s + CHUNK) for s in range(0, rows, CHUNK)]
    cums = [_sum_rows_split(stack, logf[sl]) for sl in chunks]
    part = lambda i: jnp.concatenate([cs[i * CHUNK:(i + 1) * CHUNK] for cs in cums], axis=0)
    qg = q * jnp.exp(part(0))
    ks = kk * jnp.exp(part(1))
    q_lv = [q * jnp.exp(logf)] + [q * jnp.exp(part(2 * j)) for j in range(1, len(halves))]
    k_lv = [kk] + [kk * jnp.exp(part(2 * j + 1)) for j in range(1, len(halves))]
    st = st0
    outs = []
    for sl in chunks:
        scores = jnp.where(pairs[0], _nt(q_lv[0][sl], k_lv[0][sl]), 0.0)
        for j in range(1, len(halves)):
            scores += jnp.where(pairs[j], _nt(q_lv[j][sl], k_lv[j][sl]), 0.0)
        outs.append(_nn(scores, v[sl]) + _nt(qg[sl], st))
        st = st * jnp.exp(jnp.sum(logf[sl], axis=0, keepdims=True)) + _tn(v[sl], ks[sl])
    o = jnp.concatenate(outs, axis=0) + jnp.sum(q * kk, axis=1, keepdims=True) * v
    y = o * lax.rsqrt(jnp.mean(o * o, axis=1, keepdims=True) + NORM_EPS) * nw * _silu(zr)
    return y, st


def _unit_lower_inverses(ms):
    r, c = _iota2(ms[0].shape)
    xs = [jnp.where(r == c, 1.0, 0.0) - jnp.where((r // 2) == (c // 2), m, 0.0) for m in ms]
    b = 2
    while b < CHUNK:
        pick = ((r // (2 * b)) == (c // (2 * b))) & ((r // b) != (c // b))
        ts = [_nn(x, jnp.where(pick, m, 0.0)) for x, m in zip(xs, ms)]
        xs = [x - _nn(t, x) for x, t in zip(xs, ts)]
        b *= 2
    return tuple(x.astype(BF16) for x in xs)


@jax.custom_vjp
def _known_inverses(ms, xs):
    return xs


def _known_inverses_fwd(ms, xs):
    return xs, xs


def _known_inverses_bwd(xs, cts):
    r, c = _iota2(xs[0].shape)
    keep = (c < r) & ((r // CHUNK) == (c // CHUNK))
    ts = [_tn(x, ct) for x, ct in zip(xs, cts)]
    return (tuple(jnp.where(keep, -_nt(t, x), 0.0) for t, x in zip(ts, xs)), tuple(jnp.zeros_like(x) for x in xs))


_known_inverses.defvjp(_known_inverses_fwd, _known_inverses_bwd)


def _chunk_cumsum(x):
    row = lax.broadcasted_iota(jnp.int32, x.shape, 0) % CHUNK
    d = 1
    while d < CHUNK:
        x = x + jnp.where(row >= d, _shift_rows(x, d), 0.0)
        d *= 2
    return x


def _gdn_block(x_ext, z, ba, s0a, s0b, w0, w1, w2, w3, alog, dtb, nw, known=None):
    rows = z.shape[0]
    conv = (w0 * _shift_rows(x_ext, 3) + w1 * _shift_rows(x_ext, 2) + w2 * _shift_rows(x_ext, 1) + w3 * x_ext)
    cc = _silu(conv[8:])
    qc, kc = cc[:, 0:HD], cc[:, HD:2 * HD]
    q = qc * lax.rsqrt(jnp.sum(qc * qc, axis=1, keepdims=True) + L2_EPS) * (HD ** -0.5)
    k = kc * lax.rsqrt(jnp.sum(kc * kc, axis=1, keepdims=True) + L2_EPS)

    r, c = _iota2((GROUP, GROUP))
    same = (r // CHUNK) == (c // CHUNK)
    causal, strict, eye = same & (c <= r), same & (c < r), r == c
    heads = (0, 1)
    groups = [slice(lo, lo + GROUP) for lo in range(0, rows, GROUP)]
    chunks = [slice(lo, lo + CHUNK) for lo in range(0, rows, CHUNK)]

    v, loga, g_w, kb, kg, qg = [], [], [], [], [], []
    for i in heads:
        v.append(cc[:, (2 + i) * HD:(3 + i) * HD])
        beta = _sigmoid(_lane_pick(ba, i))
        a_neg = -jnp.exp(_lane_pick(alog, i))
        loga.append(a_neg * _softplus(_lane_pick(ba, 2 + i) + _lane_pick(dtb, i)))
        g_w.append(_chunk_cumsum(jnp.broadcast_to(loga[i], (rows, HD))))
        kb.append(k * beta)
        kg.append(k * jnp.exp(g_w[i]))
        qg.append(q * jnp.exp(g_w[i]))

    systems = [(i, gs) for gs in groups for i in heads]
    dec_c, ms = [], []
    for i, gs in systems:
        g_sq = g_w[i][gs]
        g_row = jnp.sum(jnp.where(eye, g_sq, 0.0), axis=0, keepdims=True)
        diff = g_sq - g_row
        dec_c.append(jnp.where(causal, jnp.exp(jnp.where(causal, diff, 0.0)), 0.0))
        ms.append(jnp.where(strict, _nt(k[gs], kb[i][gs]) * dec_c[-1], 0.0))
    xs = _unit_lower_inverses(tuple(ms)) if known is None else _known_inverses(tuple(ms), known)
    u = [[None] * len(groups) for _ in heads]
    w = [[None] * len(groups) for _ in heads]
    qk = [[None] * len(groups) for _ in heads]
    for n, (i, gs) in enumerate(systems):
        j = n // len(heads)
        u[i][j] = _nn(xs[n], v[i][gs])
        w[i][j] = _nn(xs[n], kg[i][gs])
        qk[i][j] = _nt(q[gs], kb[i][gs]) * dec_c[n]
    u = [jnp.concatenate(p, axis=0) for p in u]
    w = [jnp.concatenate(p, axis=0) for p in w]

    decay, p_mat, q_mat = {}, {}, {}
    for n, sl in enumerate(chunks):
        for i in heads:
            g_last = jnp.sum(loga[i][sl], axis=0, keepdims=True)
            kd = kb[i][sl] * jnp.exp(g_last - g_w[i][sl])
            decay[n, i] = jnp.exp(g_last)
            p_mat[n, i] = -_tn(kd, w[i][sl])
            q_mat[n, i] = _tn(kd, u[i][sl])
    s = [s0a, s0b]
    s_at = {}
    for n in range(len(chunks)):
        for i in heads:
            s_at[n, i] = s[i]
            s[i] = s[i] * decay[n, i] + _nn(p_mat[n, i], s[i]) + q_mat[n, i]

    ys = []
    for i in heads:
        e = jnp.concatenate([u[i][sl] - _nn(w[i][sl], s_at[n, i]) for n, sl in enumerate(chunks)], axis=0)
        o_state = jnp.concatenate([_nn(qg[i][sl], s_at[n, i]) for n, sl in enumerate(chunks)], axis=0)
        o = o_state + jnp.concatenate([_nn(qk[i][j], e[gs]) for j, gs in enumerate(groups)], axis=0)
        zi = z[:, i * HD:(i + 1) * HD]
        ys.append(o * lax.rsqrt(jnp.mean(o * o, axis=1, keepdims=True) + NORM_EPS) * nw * _silu(zi))
    return (jnp.concatenate(ys, axis=1), s[0], s[1]), xs


def _add_to_tail(x, tail):
    return x + jnp.concatenate([jnp.zeros((x.shape[0] - 8, x.shape[1]), x.dtype), tail], axis=0)


def _conv_a_block(ab, ac_ext, ax_ext, az, w0, w1, w2):
    u = ac_ext * ax_ext
    conv = (w0 * _shift_rows(u, 2) + w1 * _shift_rows(u, 1) + w2 * u)[8:]
    return ab * conv * _silu(az)


def _matmul(a, b, mode, name, residual=None, out_dtype=F32):
    if mode == "nn":
        (m, k), n = a.shape, b.shape[1]
    elif mode == "nt":
        (m, k), n = a.shape, b.shape[0]
    else:
        (k, m), n = a.shape, b.shape[1]
    tm, tn, tk = _tile(m, MM_TILE), _tile(n, MM_TILE), _tile(k, MM_TILE)
    nk = k // tk
    dims = {"nn": ((1,), (0,)), "nt": ((1,), (1,)), "tn": ((0,), (0,))}[mode]
    a_spec = pl.BlockSpec((tk, tm), lambda i, j, s: (s, i)) if mode == "tn" else pl.BlockSpec((tm, tk), lambda i, j, s: (i, s))
    b_spec = pl.BlockSpec((tn, tk), lambda i, j, s: (j, s)) if mode == "nt" else pl.BlockSpec((tk, tn), lambda i, j, s: (s, j))
    o_spec = pl.BlockSpec((tm, tn), lambda i, j, s: (i, j))
    has_res = residual is not None

    def finish(out, r_ref, o_ref):
        if has_res:
            out = out + r_ref[...]
        o_ref[...] = out.astype(out_dtype)

    def body_one_pass(*refs):
        finish(_dot(refs[0][...], refs[1][...], dims), refs[2] if has_res else None, refs[-1])

    def body_reduce(*refs):
        a_ref, b_ref = refs[0], refs[1]
        r_ref = refs[2] if has_res else None
        o_ref, acc_ref = refs[-2], refs[-1]
        s = pl.program_id(2)

        @pl.when(s == 0)
        def _():
            acc_ref[...] = jnp.zeros_like(acc_ref)

        acc_ref[...] += _dot(a_ref[...], b_ref[...], dims)

        @pl.when(s == nk - 1)
        def _():
            finish(acc_ref[...], r_ref, o_ref)

    args, specs = [a, b], [a_spec, b_spec]
    if has_res:
        args.append(residual)
        specs.append(o_spec)
    return pl.pallas_call(
        body_one_pass if nk == 1 else body_reduce, name=name, grid=(m // tm, n // tn, nk), in_specs=specs, out_specs=o_spec,
        out_shape=jax.ShapeDtypeStruct((m, n), out_dtype),
        scratch_shapes=[] if nk == 1 else [pltpu.VMEM((tm, tn), F32)],
        compiler_params=_cparams("parallel", "parallel", "arbitrary"))(*args)


def _matmul_nt_sum(pairs, name, exchange=None):
    m, n = pairs[0][0].shape[0], pairs[0][1].shape[0]
    tm, tn = _tile(m, MM_TILE), _tile(n, MM_TILE)
    tks = [_tile(a.shape[1], MM_TILE) for a, _ in pairs]
    nks = [a.shape[1] // tk for (a, _), tk in zip(pairs, tks)]
    offs = [sum(nks[:i]) for i in range(len(pairs))]
    total = sum(nks)

    def body(*refs):
        o_ref, acc_ref = refs[-2], refs[-1]
        s = pl.program_id(2)

        @pl.when(s == 0)
        def _():
            acc_ref[...] = jnp.zeros_like(acc_ref)

        for i, (off, nk) in enumerate(zip(offs, nks)):
            @pl.when((s >= off) & (s < off + nk))
            def _(i=i):
                acc_ref[...] += _dot(refs[2 * i][...], refs[2 * i + 1][...], ((1,), (1,)))

        @pl.when(s == total - 1)
        def _():
            o_ref[...] = acc_ref[...]

    args, specs = [], []
    for (a, b), tk, off, nk in zip(pairs, tks, offs, nks):
        k_of = lambda s, off=off, nk=nk: jnp.clip(s - off, 0, nk - 1)
        args += [a, b]
        specs += [pl.BlockSpec((tm, tk), lambda i, j, s, k_of=k_of: (i, k_of(s))),
                  pl.BlockSpec((tn, tk), lambda i, j, s, k_of=k_of: (j, k_of(s)))]
    (out,), exchanged = _call_with_exchange(
        body, name=name, grid=(m // tm, n // tn, total), in_specs=specs,
        out_specs=[pl.BlockSpec((tm, tn), lambda i, j, s: (i, j))], out_shape=[jax.ShapeDtypeStruct((m, n), F32)],
        scratch_shapes=[pltpu.VMEM((tm, tn), F32)], args=args, exchange=exchange)
    return out, exchanged


def _rmsnorm_fwd(x, w, name):
    t = x.shape[0]
    blk = _tile(t, 512)

    def body(x_ref, w_ref, h_ref):
        xv = x_ref[...]
        h_ref[...] = (xv * lax.rsqrt(jnp.mean(xv * xv, axis=1, keepdims=True) + NORM_EPS) * w_ref[...]).astype(BF16)

    return pl.pallas_call(
        body, name=name, grid=(t // blk,),
        in_specs=[pl.BlockSpec((blk, D), lambda i: (i, 0)), pl.BlockSpec((1, D), lambda i: (0, 0))],
        out_specs=pl.BlockSpec((blk, D), lambda i: (i, 0)), out_shape=jax.ShapeDtypeStruct((t, D), BF16),
        compiler_params=_cparams("parallel"))(x, w)


def _rmsnorm_bwd(dh, x, w, dxo, name, exchange=None):
    t = x.shape[0]
    blk = _tile(t, 512)

    def body(dh_ref, x_ref, w_ref, dxo_ref, dx_ref, dw_ref):
        @pl.when(pl.program_id(0) == 0)
        def _():
            dw_ref[...] = jnp.zeros_like(dw_ref)

        xv, dhv = x_ref[...], dh_ref[...]
        rs = lax.rsqrt(jnp.mean(xv * xv, axis=1, keepdims=True) + NORM_EPS)
        xh = xv * rs
        dw_ref[...] += jnp.sum(dhv * xh, axis=0, keepdims=True)
        dxh = dhv * w_ref[...]
        dx_ref[...] = rs * (dxh - xh * jnp.mean(dxh * xh, axis=1, keepdims=True)) + dxo_ref[...]

    row = pl.BlockSpec((blk, D), lambda i: (i, 0))
    vec = pl.BlockSpec((1, D), lambda i: (0, 0))
    return _call_with_exchange(
        body, name=name, grid=(t // blk,), in_specs=[row, row, vec, row], out_specs=[row, vec],
        out_shape=[jax.ShapeDtypeStruct((t, D), F32), jax.ShapeDtypeStruct((1, D), F32)],
        scratch_shapes=[], args=(dh, x, w, dxo), exchange=exchange)


def _loss_head(x, w, target, name):
    t = x.shape[0]
    blk = _tile(t, 512)

    def body(x_ref, w_ref, t_ref, loss_ref, dx_ref, dw_ref):
        @pl.when(pl.program_id(0) == 0)
        def _():
            dw_ref[...] = jnp.zeros_like(dw_ref)
            loss_ref[...] = jnp.zeros_like(loss_ref)

        xv = x_ref[...]
        rs = lax.rsqrt(jnp.mean(xv * xv, axis=1, keepdims=True) + NORM_EPS)
        xh = xv * rs
        err = xh * w_ref[...] - t_ref[...]
        loss_ref[...] += 0.5 * jnp.sum(jnp.mean(err * err, axis=1, keepdims=True), axis=0, keepdims=True)
        dy = err * (1.0 / D)
        dw_ref[...] += jnp.sum(dy * xh, axis=0, keepdims=True)
        dxh = dy * w_ref[...]
        dx_ref[...] = rs * (dxh - xh * jnp.mean(dxh * xh, axis=1, keepdims=True))

    row = pl.BlockSpec((blk, D), lambda i: (i, 0))
    vec = pl.BlockSpec((1, D), lambda i: (0, 0))
    return pl.pallas_call(
        body, name=name, grid=(t // blk,), in_specs=[row, vec, row],
        out_specs=[pl.BlockSpec((1, 1), lambda i: (0, 0)), row, vec],
        out_shape=[jax.ShapeDtypeStruct((1, 1), F32), jax.ShapeDtypeStruct((t, D), F32), jax.ShapeDtypeStruct((1, D), F32)],
        compiler_params=_cparams("arbitrary"))(x, w, target)


def _lbs_of(lb):
    r = lax.broadcasted_iota(jnp.int32, lb.shape, 0)
    real = r < DEPTH
    mx = lax.stop_gradient(jnp.max(jnp.where(real, lb, -jnp.inf), axis=0, keepdims=True))
    e = jnp.where(real, jnp.exp(jnp.where(real, lb - mx, 0.0)), 0.0)
    p = e / jnp.sum(e, axis=0, keepdims=True)
    out = jnp.zeros_like(lb)
    run = jnp.zeros_like(mx)
    for l in range(1, DEPTH):
        run = run + jnp.sum(jnp.where(r == l, p, 0.0), axis=0, keepdims=True)
        out = out + jnp.where(r == l, run, 0.0)
    return out


def _lower_bounds_fwd(lbp, name):
    def body(lb_ref, o_ref):
        o_ref[...] = _lbs_of(lb_ref[...])

    return pl.pallas_call(body, name=name, out_shape=jax.ShapeDtypeStruct(lbp.shape, F32))(lbp)


def _lower_bounds_bwd(lbp, dlbs, name):
    def body(lb_ref, d_ref, o_ref):
        _, vjp = jax.vjp(_lbs_of, lb_ref[...])
        o_ref[...] = vjp(d_ref[...])[0]

    return pl.pallas_call(body, name=name, out_shape=jax.ShapeDtypeStruct(lbp.shape, F32))(lbp, dlbs)


def _branch_a_fwd(pa, cw, name):
    t = pa.shape[0]
    blk = _tile(t, 512)
    W = CONV_W

    def body(p_ref, w_ref, y_ref, hc_ref, hx_ref):
        @pl.when(pl.program_id(0) == 0)
        def _():
            hc_ref[...] = jnp.zeros_like(hc_ref)
            hx_ref[...] = jnp.zeros_like(hx_ref)

        ac, ax = p_ref[:, W:2 * W], p_ref[:, 2 * W:3 * W]
        y_ref[...] = _conv_a_block(
            p_ref[:, 0:W], jnp.concatenate([hc_ref[...], ac], axis=0), jnp.concatenate([hx_ref[...], ax], axis=0),
            p_ref[:, 3 * W:4 * W], w_ref[0:1, :], w_ref[1:2, :], w_ref[2:3, :]).astype(BF16)
        hc_ref[...] = p_ref[blk - 8:blk, W:2 * W]
        hx_ref[...] = p_ref[blk - 8:blk, 2 * W:3 * W]

    return pl.pallas_call(
        body, name=name, grid=(t // blk,),
        in_specs=[pl.BlockSpec((blk, NA), lambda i: (i, 0)), pl.BlockSpec((3, W), lambda i: (0, 0))],
        out_specs=pl.BlockSpec((blk, W), lambda i: (i, 0)), out_shape=jax.ShapeDtypeStruct((t, W), BF16),
        scratch_shapes=[pltpu.VMEM((8, W), F32), pltpu.VMEM((8, W), F32)],
        compiler_params=_cparams("arbitrary"))(pa, cw)


def _branch_a_bwd(pa, cw, dy, name):
    t = pa.shape[0]
    blk = _tile(t, 512)
    nt_ = t // blk
    W = CONV_W
    hb = blk // 8

    def body(p_ref, halo_ref, w_ref, dy_ref, dp_ref, dw_ref, chc_ref, chx_ref):
        i = pl.program_id(0)

        @pl.when(i == 0)
        def _():
            chc_ref[...] = jnp.zeros_like(chc_ref)
            chx_ref[...] = jnp.zeros_like(chx_ref)
            dw_ref[...] = jnp.zeros_like(dw_ref)

        keep = 1.0 - (i == nt_ - 1).astype(F32)
        hc = halo_ref[:, W:2 * W] * keep
        hx = halo_ref[:, 2 * W:3 * W] * keep
        ac_ext = jnp.concatenate([hc, p_ref[:, W:2 * W]], axis=0)
        ax_ext = jnp.concatenate([hx, p_ref[:, 2 * W:3 * W]], axis=0)
        _, vjp = jax.vjp(_conv_a_block, p_ref[:, 0:W], ac_ext, ax_ext, p_ref[:, 3 * W:4 * W],
                         w_ref[0:1, :], w_ref[1:2, :], w_ref[2:3, :])
        dab, dac, dax, daz, dw0, dw1, dw2 = vjp(dy_ref[...])
        dp_ref[:, 0:W] = dab.astype(BF16)
        dp_ref[:, W:2 * W] = _add_to_tail(dac[8:], chc_ref[...]).astype(BF16)
        dp_ref[:, 2 * W:3 * W] = _add_to_tail(dax[8:], chx_ref[...]).astype(BF16)
        dp_ref[:, 3 * W:4 * W] = daz.astype(BF16)
        chc_ref[...] = dac[:8] * keep
        chx_ref[...] = dax[:8] * keep
        dw_ref[0:1, :] += dw0
        dw_ref[1:2, :] += dw1
        dw_ref[2:3, :] += dw2

    rev = lambda i: (nt_ - 1 - i, 0)
    return pl.pallas_call(
        body, name=name, grid=(nt_,),
        in_specs=[pl.BlockSpec((blk, NA), rev),
                  pl.BlockSpec((8, NA), lambda i: (jnp.maximum((nt_ - 1 - i) * hb - 1, 0), 0)),
                  pl.BlockSpec((3, W), lambda i: (0, 0)),
                  pl.BlockSpec((blk, W), rev)],
        out_specs=[pl.BlockSpec((blk, NA), rev), pl.BlockSpec((3, W), lambda i: (0, 0))],
        out_shape=[jax.ShapeDtypeStruct((t, NA), BF16), jax.ShapeDtypeStruct((3, W), F32)],
        scratch_shapes=[pltpu.VMEM((8, W), F32), pltpu.VMEM((8, W), F32)],
        compiler_params=_cparams("arbitrary"))(pa, pa, cw, dy)


def _block_rows(t, chunks):
    return min(t, chunks * CHUNK)


def _branch_b_fwd(pb, lbs_row, nw, name, exchange=None):
    t = pb.shape[0]
    rows = _block_rows(t, HGRN_BLOCK_CHUNKS)
    nch = t // rows

    def body(p_ref, lb_ref, nw_ref, y_ref, s_ref, st_ref):
        @pl.when(pl.program_id(1) == 0)
        def _():
            st_ref[...] = jnp.zeros_like(st_ref)

        s_ref[0, 0] = st_ref[...]
        y, st1 = _hgrn_block(p_ref[:, 0:HD], p_ref[:, HD:2 * HD], p_ref[:, 2 * HD:3 * HD], p_ref[:, 3 * HD:4 * HD],
                             st_ref[...], lb_ref[...], nw_ref[...])
        y_ref[...] = y.astype(BF16)
        st_ref[...] = st1

    return _call_with_exchange(
        body, name=name, grid=(HGRN_HEADS, nch),
        in_specs=[pl.BlockSpec((rows, 4 * HD), lambda h, i: (i, h)),
                  pl.BlockSpec((1, HD), lambda h, i: (0, h)),
                  pl.BlockSpec((1, HD), lambda h, i: (0, 0))],
        out_specs=[pl.BlockSpec((rows, HD), lambda h, i: (i, h)),
                   pl.BlockSpec((1, 1, HD, HD), lambda h, i: (h, i, 0, 0))],
        out_shape=[jax.ShapeDtypeStruct((t, HGRN_HEADS * HD), BF16),
                   jax.ShapeDtypeStruct((HGRN_HEADS, nch, HD, HD), F32)],
        scratch_shapes=[pltpu.VMEM((HD, HD), F32)],
        args=(pb, lbs_row, nw), exchange=exchange)


def _branch_b_bwd(pb, states, lbs_row, nw, dy, name, exchange=None):
    t = pb.shape[0]
    rows = _block_rows(t, HGRN_BLOCK_CHUNKS)
    nch = t // rows

    def body(p_ref, s_ref, lb_ref, nw_ref, dy_ref, dp_ref, dlb_ref, dnw_ref, ds_ref):
        h, i = pl.program_id(0), pl.program_id(1)

        @pl.when(i == 0)
        def _():
            ds_ref[...] = jnp.zeros_like(ds_ref)
            dlb_ref[...] = jnp.zeros_like(dlb_ref)

        @pl.when((i == 0) & (h == 0))
        def _():
            dnw_ref[...] = jnp.zeros_like(dnw_ref)

        _, vjp = jax.vjp(_hgrn_block, p_ref[:, 0:HD], p_ref[:, HD:2 * HD], p_ref[:, 2 * HD:3 * HD],
                         p_ref[:, 3 * HD:4 * HD], s_ref[0, 0], lb_ref[...], nw_ref[...])
        dq, df, di, dz, ds0, dlb, dnw = vjp((dy_ref[...], ds_ref[...]))
        dp_ref[:, 0:HD] = dq.astype(BF16)
        dp_ref[:, HD:2 * HD] = df.astype(BF16)
        dp_ref[:, 2 * HD:3 * HD] = di.astype(BF16)
        dp_ref[:, 3 * HD:4 * HD] = dz.astype(BF16)
        ds_ref[...] = ds0
        dlb_ref[...] += dlb
        dnw_ref[...] += dnw

    rev = lambda h, i: (nch - 1 - i, h)
    return _call_with_exchange(
        body, name=name, grid=(HGRN_HEADS, nch),
        in_specs=[pl.BlockSpec((rows, 4 * HD), rev),
                  pl.BlockSpec((1, 1, HD, HD), lambda h, i: (h, nch - 1 - i, 0, 0)),
                  pl.BlockSpec((1, HD), lambda h, i: (0, h)),
                  pl.BlockSpec((1, HD), lambda h, i: (0, 0)),
                  pl.BlockSpec((rows, HD), rev)],
        out_specs=[pl.BlockSpec((rows, 4 * HD), rev),
                   pl.BlockSpec((1, HD), lambda h, i: (0, h)),
                   pl.BlockSpec((1, HD), lambda h, i: (0, 0))],
        out_shape=[jax.ShapeDtypeStruct((t, NB), BF16), jax.ShapeDtypeStruct((1, HGRN_HEADS * HD), F32),
                   jax.ShapeDtypeStruct((1, HD), F32)],
        scratch_shapes=[pltpu.VMEM((HD, HD), F32)],
        args=(pb, states, lbs_row, nw, dy), exchange=exchange)


def _branch_c_fwd(pc, cw, cpar, nw, name, exchange=None):
    t = pc.shape[0]
    rows = _block_rows(t, GDN_BLOCK_CHUNKS)
    nch = t // rows
    XW = 4 * HD

    nsys = 2 * rows // GROUP

    def body(p_ref, w_ref, cp_ref, nw_ref, y_ref, s_ref, x_ref, sa_ref, sb_ref, halo_ref):
        @pl.when(pl.program_id(1) == 0)
        def _():
            sa_ref[...] = jnp.zeros_like(sa_ref)
            sb_ref[...] = jnp.zeros_like(sb_ref)
            halo_ref[...] = jnp.zeros_like(halo_ref)

        s_ref[0, 0, 0] = sa_ref[...]
        s_ref[0, 0, 1] = sb_ref[...]
        x_ext = jnp.concatenate([halo_ref[...], p_ref[:, 0:XW]], axis=0)
        (y, s1a, s1b), xs = _gdn_block(x_ext, p_ref[:, XW:XW + 2 * HD], p_ref[:, XW + 2 * HD:XW + 3 * HD],
                                       sa_ref[...], sb_ref[...], w_ref[0:1, :], w_ref[1:2, :], w_ref[2:3, :], w_ref[3:4, :],
                                       cp_ref[0, 0:1, :], cp_ref[0, 1:2, :], nw_ref[...])
        for n in range(nsys):
            x_ref[0, 0, n] = xs[n]
        y_ref[...] = y.astype(BF16)
        sa_ref[...] = s1a
        sb_ref[...] = s1b
        halo_ref[...] = p_ref[rows - 8:rows, 0:XW]

    return _call_with_exchange(
        body, name=name, grid=(GDN_QK_HEADS, nch),
        in_specs=[pl.BlockSpec((rows, C_HEAD), lambda h, i: (i, h)),
                  pl.BlockSpec((4, XW), lambda h, i: (0, h)),
                  pl.BlockSpec((1, 8, HD), lambda h, i: (h, 0, 0)),
                  pl.BlockSpec((1, HD), lambda h, i: (0, 0))],
        out_specs=[pl.BlockSpec((rows, 2 * HD), lambda h, i: (i, h)),
                   pl.BlockSpec((1, 1, 2, HD, HD), lambda h, i: (h, i, 0, 0, 0)),
                   pl.BlockSpec((1, 1, nsys, GROUP, GROUP), lambda h, i: (h, i, 0, 0, 0))],
        out_shape=[jax.ShapeDtypeStruct((t, 2 * GDN_QK_HEADS * HD), BF16),
                   jax.ShapeDtypeStruct((GDN_QK_HEADS, nch, 2, HD, HD), F32),
                   jax.ShapeDtypeStruct((GDN_QK_HEADS, nch, nsys, GROUP, GROUP), BF16)],
        scratch_shapes=[pltpu.VMEM((HD, HD), F32), pltpu.VMEM((HD, HD), F32), pltpu.VMEM((8, XW), F32)],
        args=(pc, cw, cpar, nw), exchange=exchange)


def _branch_c_bwd(pc, states, inverses, cw, cpar, nw, dy, name, exchange=None):
    t = pc.shape[0]
    rows = _block_rows(t, GDN_BLOCK_CHUNKS)
    nch = t // rows
    XW = 4 * HD
    hb = rows // 8

    nsys = 2 * rows // GROUP

    def body(p_ref, halo_ref, s_ref, x_ref, w_ref, cp_ref, nw_ref, dy_ref, dp_ref, dw_ref, dcp_ref, dnw_ref,
             dsa_ref, dsb_ref, carry_ref):
        h, i = pl.program_id(0), pl.program_id(1)

        @pl.when(i == 0)
        def _():
            dsa_ref[...] = jnp.zeros_like(dsa_ref)
            dsb_ref[...] = jnp.zeros_like(dsb_ref)
            carry_ref[...] = jnp.zeros_like(carry_ref)
            dw_ref[...] = jnp.zeros_like(dw_ref)
            dcp_ref[...] = jnp.zeros_like(dcp_ref)

        @pl.when((i == 0) & (h == 0))
        def _():
            dnw_ref[...] = jnp.zeros_like(dnw_ref)

        keep = 1.0 - (i == nch - 1).astype(F32)
        x_ext = jnp.concatenate([halo_ref[:, 0:XW] * keep, p_ref[:, 0:XW]], axis=0)
        block = functools.partial(_gdn_block, known=tuple(x_ref[0, 0, n] for n in range(nsys)))
        _, vjp, _ = jax.vjp(block, x_ext, p_ref[:, XW:XW + 2 * HD], p_ref[:, XW + 2 * HD:XW + 3 * HD],
                            s_ref[0, 0, 0], s_ref[0, 0, 1], w_ref[0:1, :], w_ref[1:2, :], w_ref[2:3, :], w_ref[3:4, :],
                            cp_ref[0, 0:1, :], cp_ref[0, 1:2, :], nw_ref[...], has_aux=True)
        dx, dz, dba, dsa, dsb, dw0, dw1, dw2, dw3, dal, ddt, dnw = vjp((dy_ref[...], dsa_ref[...], dsb_ref[...]))
        dp_ref[:, 0:XW] = _add_to_tail(dx[8:], carry_ref[...]).astype(BF16)
        dp_ref[:, XW:XW + 2 * HD] = dz.astype(BF16)
        dp_ref[:, XW + 2 * HD:XW + 3 * HD] = dba.astype(BF16)
        carry_ref[...] = dx[:8] * keep
        dsa_ref[...] = dsa
        dsb_ref[...] = dsb
        dw_ref[0:1, :] += dw0
        dw_ref[1:2, :] += dw1
        dw_ref[2:3, :] += dw2
        dw_ref[3:4, :] += dw3
        dcp_ref[0, 0:1, :] += dal
        dcp_ref[0, 1:2, :] += ddt
        dnw_ref[...] += dnw

    rev = lambda h, i: (nch - 1 - i, h)
    return _call_with_exchange(
        body, name=name, grid=(GDN_QK_HEADS, nch),
        in_specs=[pl.BlockSpec((rows, C_HEAD), rev),
                  pl.BlockSpec((8, C_HEAD), lambda h, i: (jnp.maximum((nch - 1 - i) * hb - 1, 0), h)),
                  pl.BlockSpec((1, 1, 2, HD, HD), lambda h, i: (h, nch - 1 - i, 0, 0, 0)),
                  pl.BlockSpec((1, 1, nsys, GROUP, GROUP), lambda h, i: (h, nch - 1 - i, 0, 0, 0)),
                  pl.BlockSpec((4, XW), lambda h, i: (0, h)),
                  pl.BlockSpec((1, 8, HD), lambda h, i: (h, 0, 0)),
                  pl.BlockSpec((1, HD), lambda h, i: (0, 0)),
                  pl.BlockSpec((rows, 2 * HD), rev)],
        out_specs=[pl.BlockSpec((rows, C_HEAD), rev),
                   pl.BlockSpec((4, XW), lambda h, i: (0, h)),
                   pl.BlockSpec((1, 8, HD), lambda h, i: (h, 0, 0)),
                   pl.BlockSpec((1, HD), lambda h, i: (0, 0))],
        out_shape=[jax.ShapeDtypeStruct((t, NC_COLS), BF16), jax.ShapeDtypeStruct((4, GDN_QK_HEADS * XW), F32),
                   jax.ShapeDtypeStruct((GDN_QK_HEADS, 8, HD), F32), jax.ShapeDtypeStruct((1, HD), F32)],
        scratch_shapes=[pltpu.VMEM((HD, HD), F32), pltpu.VMEM((HD, HD), F32), pltpu.VMEM((8, XW), F32)],
        args=(pc, pc, states, inverses, cw, cpar, nw, dy), exchange=exchange)


def _merge_fwd(pg, bg, ya, yb, yc, name):
    t = pg.shape[0]
    blk = _tile(t, 256)

    def body(g_ref, b_ref, a_ref, b2_ref, c_ref, o_ref):
        gate = _sigmoid(g_ref[...] + b_ref[...])
        o_ref[...] = (gate[:, 0:D] * a_ref[...] + gate[:, D:2 * D] * b2_ref[...] + gate[:, 2 * D:3 * D] * c_ref[...]).astype(BF16)

    row = pl.BlockSpec((blk, D), lambda i: (i, 0))
    return pl.pallas_call(
        body, name=name, grid=(t // blk,),
        in_specs=[pl.BlockSpec((blk, NG), lambda i: (i, 0)), pl.BlockSpec((1, NG), lambda i: (0, 0)), row, row, row],
        out_specs=row, out_shape=jax.ShapeDtypeStruct((t, D), BF16),
        compiler_params=_cparams("parallel"))(pg, bg, ya, yb, yc)


def _merge_bwd(dm, pg, bg, ya, yb, yc, name):
    t = pg.shape[0]
    blk = _tile(t, 256)

    def body(dm_ref, g_ref, b_ref, a_ref, b2_ref, c_ref, dg_ref, da_ref, db_ref, dc_ref, dbg_ref):
        @pl.when(pl.program_id(0) == 0)
        def _():
            dbg_ref[...] = jnp.zeros_like(dbg_ref)

        gate = _sigmoid(g_ref[...] + b_ref[...])
        dmv = dm_ref[...].astype(F32)
        for j, (y_ref, dy_ref) in enumerate(((a_ref, da_ref), (b2_ref, db_ref), (c_ref, dc_ref))):
            gj = gate[:, j * D:(j + 1) * D]
            dy_ref[...] = (dmv * gj).astype(BF16)
            dgj = dmv * y_ref[...] * gj * (1.0 - gj)
            dg_ref[:, j * D:(j + 1) * D] = dgj.astype(BF16)
            dbg_ref[:, j * D:(j + 1) * D] += jnp.sum(dgj, axis=0, keepdims=True)

    row = pl.BlockSpec((blk, D), lambda i: (i, 0))
    wide = pl.BlockSpec((blk, NG), lambda i: (i, 0))
    vec = pl.BlockSpec((1, NG), lambda i: (0, 0))
    return pl.pallas_call(
        body, name=name, grid=(t // blk,), in_specs=[row, wide, vec, row, row, row],
        out_specs=[wide, row, row, row, vec],
        out_shape=[jax.ShapeDtypeStruct((t, NG), BF16)] + [jax.ShapeDtypeStruct((t, D), BF16)] * 3
                  + [jax.ShapeDtypeStruct((1, NG), F32)],
        compiler_params=_cparams("arbitrary"))(dm, pg, bg, ya, yb, yc)


def _adamw_math(w, g, m, v):
    m = ADAM_B1 * m + (1.0 - ADAM_B1) * g
    v = ADAM_B2 * v + (1.0 - ADAM_B2) * (g * g)
    m_hat = m / (1.0 - ADAM_B1 ** ADAM_STEP)
    v_hat = v / (1.0 - ADAM_B2 ** ADAM_STEP)
    delta = -ADAM_LR * (m_hat / (jnp.sqrt(v_hat) + ADAM_EPS) + ADAM_WD * w)
    return delta, m, v


def _sum_adamw(parts, w, m, v, name):
    layers = len(parts)
    r, c = parts[0].shape[1:]
    br = r if r <= 256 else 256
    nb = r // br
    assert r % br == 0 and w.shape == (layers * r, c)

    def body(*refs):
        w_ref, m_ref, v_ref, g_ref, d_ref, nm_ref, nv_ref = refs[layers:]
        for l in range(layers):
            @pl.when(pl.program_id(0) == l)
            def _(p_ref=refs[l]):
                g = p_ref[0].astype(F32)
                for k in range(1, N_DEV):
                    g = g + p_ref[k].astype(F32)
                g_ref[...] = g
                d_ref[...], nm_ref[...], nv_ref[...] = _adamw_math(w_ref[...], g, m_ref[...], v_ref[...])

    blk = pl.BlockSpec((br, c), lambda l, i: (l * nb + i, 0))
    part_specs = [pl.BlockSpec((N_DEV, br, c), lambda l, i, q=q: (0, jnp.where(l == q, i, jnp.where(l < q, 0, nb - 1)), 0))
                  for q in range(layers)]
    return pl.pallas_call(
        body, name=name, grid=(layers, nb), in_specs=part_specs + [blk, blk, blk], out_specs=[blk] * 4,
        out_shape=[jax.ShapeDtypeStruct((layers * r, c), F32)] * 4,
        compiler_params=_cparams("arbitrary", "arbitrary"))(*parts, w, m, v)


def _adamw(g, w, m, v, name):
    def body(g_ref, w_ref, m_ref, v_ref, d_ref, nm_ref, nv_ref):
        d_ref[...], nm_ref[...], nv_ref[...] = _adamw_math(w_ref[...], g_ref[...], m_ref[...], v_ref[...])

    return pl.pallas_call(body, name=name, out_shape=[jax.ShapeDtypeStruct(w.shape, F32)] * 3)(g, w, m, v)


def _sum_slots(parts, name):
    def body(p_ref, o_ref):
        g = p_ref[0]
        for k in range(1, N_DEV):
            g = g + p_ref[k]
        o_ref[...] = g

    return pl.pallas_call(body, name=name, out_shape=jax.ShapeDtypeStruct(parts.shape[1:], F32))(parts)


def _exchange(srcs, name, broadcast):
    n = len(srcs)

    def body(*refs):
        copies = _exchange_copies(refs[:n], refs[n:2 * n], *refs[2 * n:], broadcast)
        for cp in copies:
            cp.start()
        for cp in copies:
            cp.wait()

    return pl.pallas_call(
        body, name=name, in_specs=[HBM_SPEC] * n, out_specs=[HBM_SPEC] * n, out_shape=_exchange_shapes(srcs, broadcast),
        scratch_shapes=_exchange_semaphores(n))(*srcs)


def _gather_two_level(srcs, name):
    n = len(srcs)

    def body(*refs):
        src_refs, dst_refs = refs[:n], refs[n:2 * n]
        send_sems, recv_sems, local_sems = refs[2 * n:]
        x, y, c = lax.axis_index("x"), lax.axis_index("y"), lax.axis_index("c")
        index_of = lambda px, py, pc: 4 * px + 2 * py + pc
        me, other_core = index_of(x, y, c), (x, y, 1 - c)
        chips = [(1 - x, y), (x, 1 - y), (1 - x, 1 - y)]

        def copy(k, a, block, to, src=None):
            return pltpu.make_async_remote_copy(
                src_ref=dst_refs[a].at[block] if src is None else src, dst_ref=dst_refs[a].at[block],
                send_sem=send_sems.at[k, a], recv_sem=recv_sems.at[k, a], device_id=to, device_id_type=MESH)

        local = [pltpu.make_async_copy(src_refs[a], dst_refs[a].at[me], local_sems.at[a]) for a in range(n)]
        first = [copy(0, a, me, other_core, src=src_refs[a]) for a in range(n)]
        first += [copy(1 + j, a, me, (*chip, c), src=src_refs[a]) for j, chip in enumerate(chips) for a in range(n)]
        for cp in local + first:
            cp.start()
        passed = []
        for j, chip in enumerate(chips):
            block = index_of(*chip, c)
            for a in range(n):
                copy(1 + j, a, block, (x, y, c)).wait_recv()
            for a in range(n):
                passed.append(copy(4 + j, a, block, other_core))
                passed[-1].start()
        for a in range(n):
            copy(0, a, index_of(x, y, 1 - c), (x, y, c)).wait_recv()
        for j, chip in enumerate(chips):
            for a in range(n):
                copy(4 + j, a, index_of(*chip, 1 - c), (x, y, c)).wait_recv()
        for cp in first + passed:
            cp.wait_send()
        for cp in local:
            cp.wait()

    return pl.pallas_call(
        body, name=name, in_specs=[HBM_SPEC] * n, out_specs=[HBM_SPEC] * n, out_shape=_exchange_shapes(srcs, True),
        scratch_shapes=_exchange_semaphores(n))(*srcs)


HBM_SPEC = pl.BlockSpec(memory_space=pltpu.HBM)


def _exchange_shapes(srcs, broadcast):
    return [jax.ShapeDtypeStruct((N_DEV,) + (s.shape if broadcast else s.shape[1:]), s.dtype) for s in srcs]


def _exchange_semaphores(n):
    return [pltpu.SemaphoreType.DMA((N_DEV - 1, n)), pltpu.SemaphoreType.DMA((N_DEV - 1, n)), pltpu.SemaphoreType.DMA((n,))]


def _exchange_copies(src_refs, dst_refs, send_sems, recv_sems, local_sems, broadcast):
    x, y, c = lax.axis_index("x"), lax.axis_index("y"), lax.axis_index("c")
    me = 4 * x + 2 * y + c
    copies = []
    for k in range(1, N_DEV):
        px = 1 - x if (k >> 2) & 1 else x
        py = 1 - y if (k >> 1) & 1 else y
        pc = 1 - c if k & 1 else c
        peer = 4 * px + 2 * py + pc
        for a, (src, dst) in enumerate(zip(src_refs, dst_refs)):
            copies.append(pltpu.make_async_remote_copy(
                src_ref=src if broadcast else src.at[peer], dst_ref=dst.at[me],
                send_sem=send_sems.at[k - 1, a], recv_sem=recv_sems.at[k - 1, a],
                device_id=(px, py, pc), device_id_type=MESH))
    for a, (src, dst) in enumerate(zip(src_refs, dst_refs)):
        copies.append(pltpu.make_async_copy(src if broadcast else src.at[me], dst.at[me], local_sems.at[a]))
    return copies


def _call_with_exchange(body, *, name, grid, in_specs, out_specs, out_shape, scratch_shapes, args, exchange):
    if exchange is None:
        outs = pl.pallas_call(body, name=name, grid=grid, in_specs=in_specs, out_specs=out_specs, out_shape=out_shape,
                              scratch_shapes=scratch_shapes,
                              compiler_params=_cparams(*["arbitrary"] * len(grid)))(*args)
        return outs, None
    srcs, broadcast = exchange
    n, n_in, n_out, n_scr = len(srcs), len(args), len(out_shape), len(scratch_shapes)
    steps = 1
    for g in grid:
        steps *= g

    def hosted(*refs):
        ins, src_refs = refs[:n_in], refs[n_in:n_in + n]
        outs, dst_refs = refs[n_in + n:n_in + n + n_out], refs[n_in + n + n_out:n_in + 2 * n + n_out]
        scratch = refs[n_in + 2 * n + n_out:]
        step = pl.program_id(0)
        for axis in range(1, len(grid)):
            step = step * grid[axis] + pl.program_id(axis)

        @pl.when(step == 0)
        def _():
            for cp in _exchange_copies(src_refs, dst_refs, *scratch[n_scr:], broadcast):
                cp.start()

        body(*ins, *outs, *scratch[:n_scr])

        @pl.when(step == steps - 1)
        def _():
            for cp in _exchange_copies(src_refs, dst_refs, *scratch[n_scr:], broadcast):
                cp.wait()

    outs = pl.pallas_call(
        hosted, name=name, grid=grid, in_specs=list(in_specs) + [HBM_SPEC] * n, out_specs=list(out_specs) + [HBM_SPEC] * n,
        out_shape=list(out_shape) + _exchange_shapes(srcs, broadcast),
        scratch_shapes=list(scratch_shapes) + _exchange_semaphores(n),
        compiler_params=_cparams(*["arbitrary"] * len(grid)))(*args, *srcs)
    return outs[:n_out], outs[n_out:]


def _regroup_w_in(w):
    wa = w[:, OFF_A:OFF_A + NA]
    seg = lambda off, h, n=HD: w[:, off + h * n: off + (h + 1) * n]
    wb = jnp.concatenate([seg(OFF_B + s * 512, h) for h in range(HGRN_HEADS) for s in range(4)], axis=1)
    parts = []
    for h in range(GDN_QK_HEADS):
        small = jnp.concatenate(
            [w[:, OFF_BETA + 2 * h: OFF_BETA + 2 * h + 2], w[:, OFF_CA + 2 * h: OFF_CA + 2 * h + 2],
             jnp.zeros((w.shape[0], HD - 4), w.dtype)], axis=1)
        parts += [seg(OFF_CQ, h), seg(OFF_CK, h), seg(OFF_CV, h, 2 * HD), seg(OFF_CZ, h, 2 * HD), small]
    wc = jnp.concatenate(parts, axis=1)
    wg = w[:, OFF_G:OFF_G + NG]
    return wa, wb, wc, wg


def _ungroup_dw_in(da, db, dc, dg):
    bq = [jnp.concatenate([db[:, h * 512 + s * HD: h * 512 + (s + 1) * HD] for h in range(HGRN_HEADS)], axis=1)
          for s in range(4)]
    ch = lambda h, lo, hi: dc[:, h * C_HEAD + lo: h * C_HEAD + hi]
    heads = range(GDN_QK_HEADS)
    cq = jnp.concatenate([ch(h, 0, HD) for h in heads], axis=1)
    ck = jnp.concatenate([ch(h, HD, 2 * HD) for h in heads], axis=1)
    cv = jnp.concatenate([ch(h, 2 * HD, 4 * HD) for h in heads], axis=1)
    cz = jnp.concatenate([ch(h, 4 * HD, 6 * HD) for h in heads], axis=1)
    cbeta = jnp.concatenate([ch(h, 6 * HD, 6 * HD + 2) for h in heads], axis=1)
    ca = jnp.concatenate([ch(h, 6 * HD + 2, 6 * HD + 4) for h in heads], axis=1)
    return jnp.concatenate([da] + bq + [cq, ck, cv, cbeta, ca, cz, dg], axis=1)


def _regroup_conv_c(cw):
    parts = []
    for h in range(GDN_QK_HEADS):
        parts += [cw[:, h * HD:(h + 1) * HD], cw[:, 512 + h * HD: 512 + (h + 1) * HD],
                  cw[:, 1024 + 2 * h * HD: 1024 + (2 * h + 2) * HD]]
    return jnp.concatenate(parts, axis=1)


def _ungroup_conv_c(d):
    heads = range(GDN_QK_HEADS)
    q = jnp.concatenate([d[:, h * 512: h * 512 + HD] for h in heads], axis=1)
    k = jnp.concatenate([d[:, h * 512 + HD: h * 512 + 2 * HD] for h in heads], axis=1)
    v = jnp.concatenate([d[:, h * 512 + 2 * HD: h * 512 + 4 * HD] for h in heads], axis=1)
    return jnp.concatenate([q, k, v], axis=1)


def _numel(shape):
    n = 1
    for d in shape:
        n *= d
    return n


def _pack(arrays, rows):
    flat = jnp.concatenate([a.reshape(-1) for a in arrays])
    return jnp.pad(flat, (0, rows * 128 - flat.shape[0])).reshape(rows, 128)


def _unpack(packed, shapes):
    flat = packed.reshape(-1)
    out, off = [], 0
    for s in shapes:
        out.append(flat[off:off + _numel(s)].reshape(s))
        off += _numel(s)
    return out


def _rows_for(shapes):
    return -(-sum(_numel(s) for s in shapes) // 1024) * 8


def kernel(x, norm_w, w_in, b_gate, conv_a, conv_c, a_log, dt_bias, lower_bounds, hgrn_norm_w, gdn_norm_w, w_out_a, w_out_b, w_out_c, w_o, final_norm_w, loss_target, m_norm_w, m_w_in, m_b_gate, m_conv_a, m_conv_c, m_a_log, m_dt_bias, m_lower_bounds, m_hgrn_norm_w, m_gdn_norm_w, m_w_out_a, m_w_out_b, m_w_out_c, m_w_o, m_final_norm_w, v_norm_w, v_w_in, v_b_gate, v_conv_a, v_conv_c, v_a_log, v_dt_bias, v_lower_bounds, v_hgrn_norm_w, v_gdn_norm_w, v_w_out_a, v_w_out_b, v_w_out_c, v_w_o, v_final_norm_w):
    me = 4 * lax.axis_index("x") + 2 * lax.axis_index("y") + lax.axis_index("c")
    xs = x[0]
    target = loss_target[0]
    in_shard = w_in.shape[2]

    big = [w_in, w_out_a, w_out_b, w_out_c, w_o]
    shards_of = lambda l: [w[l].astype(BF16) for w in big]
    conv_shapes = [(DEPTH, 3, CONV_W), (DEPTH, 4, 2048)]
    conv_rows = _rows_for(conv_shapes)
    ca_full = lax.dynamic_update_slice(jnp.zeros(conv_shapes[0], F32), conv_a, (0, 0, me * conv_a.shape[2]))
    cc_full = lax.dynamic_update_slice(jnp.zeros(conv_shapes[1], F32), conv_c, (0, 0, me * conv_c.shape[2]))
    conv_parts, = _exchange([_pack([ca_full, cc_full], conv_rows)], "gather_conv", broadcast=True)
    conv_a_full, conv_c_full = _unpack(_sum_slots(conv_parts, "sum_conv"), conv_shapes)

    lb_pad = jnp.pad(lower_bounds, ((0, 8 - DEPTH), (0, 0)))
    lbs = _lower_bounds_fwd(lb_pad, "lower_bounds_fwd")

    def input_weights(l, g_in):
        wa, wb, wc, wg = _regroup_w_in(jnp.concatenate([g_in[q] for q in range(N_DEV)], axis=1))
        lanes = lambda vec: jnp.pad(vec.reshape(GDN_QK_HEADS, 1, 2), ((0, 0), (0, 0), (0, HD - 2)))
        cpar = jnp.concatenate([lanes(a_log[l]), lanes(dt_bias[l]), jnp.zeros((GDN_QK_HEADS, 6, HD), F32)], axis=1)
        return dict(
            wa=wa, wb=wb, wc=wc, wg=wg, cpar=cpar,
            nw=norm_w[l:l + 1], bg=b_gate[l:l + 1], cwa=conv_a_full[l], cwc=_regroup_conv_c(conv_c_full[l]),
            lb=lbs[l:l + 1], hnw=hgrn_norm_w[l:l + 1], gnw=gdn_norm_w[l:l + 1])

    def output_weights(g_oa, g_ob, g_oc, g_o):
        return dict(woa=jnp.concatenate([g_oa[q] for q in range(N_DEV)], axis=1),
                    wob=jnp.concatenate([g_ob[q] for q in range(N_DEV)], axis=1), woc=g_oc.reshape(D, D), wo=g_o.reshape(D, D))

    layers = [input_weights(0, _gather_two_level(shards_of(0)[:1], "gather_l0")[0])]

    saved = []
    cur = xs
    for l in range(DEPTH):
        L = layers[l]
        n = f"l{l}_"
        h = _rmsnorm_fwd(cur, L["nw"], n + "rms")
        pa = _matmul(h, L["wa"], "nn", n + "proj_a")
        pb = _matmul(h, L["wb"], "nn", n + "proj_b")
        pc = _matmul(h, L["wc"], "nn", n + "proj_c")
        pg = _matmul(h, L["wg"], "nn", n + "proj_g")
        ua = _branch_a_fwd(pa, L["cwa"], n + "conv_fwd")
        carry = (shards_of(l)[1:], True) if l == 0 else None
        (ub, sb), gathered = _branch_b_fwd(pb, L["lb"], L["hnw"], n + "hgrn_fwd", exchange=carry)
        if carry is not None:
            L.update(output_weights(*gathered))
        carry = (shards_of(l + 1), True) if l + 1 < DEPTH else None
        (uc, sc, xc), gathered = _branch_c_fwd(pc, L["cwc"], L["cpar"], L["gnw"], n + "gdn_fwd", exchange=carry)
        if carry is not None:
            layers.append(dict(input_weights(l + 1, gathered[0]), **output_weights(*gathered[1:])))
        ya = _matmul(ua, L["woa"], "nn", n + "out_a", out_dtype=BF16)
        yb = _matmul(ub, L["wob"], "nn", n + "out_b", out_dtype=BF16)
        yc = _matmul(uc, L["woc"], "nn", n + "out_c", out_dtype=BF16)
        merged = _merge_fwd(pg, L["bg"], ya, yb, yc, n + "merge")
        nxt = _matmul(merged, L["wo"], "nn", n + "out_o", residual=cur)
        saved.append(dict(x=cur, h=h, pa=pa, pb=pb, pc=pc, pg=pg, ua=ua, ub=ub, uc=uc, sb=sb, sc=sc, xc=xc,
                          ya=ya, yb=yb, yc=yc, merged=merged))
        cur = nxt

    loss_part, dx, d_final = _loss_head(cur, final_norm_w.reshape(1, D), target, "loss_head")

    def outgoing(g):
        cols = lambda a, n: jnp.stack([a[:, p * n:(p + 1) * n] for p in range(N_DEV)]).astype(BF16)
        rows = lambda a: a.reshape(N_DEV, a.shape[0] // N_DEV, a.shape[1]).astype(BF16)
        first = [cols(g["w_in"], in_shard)] if "w_in" in g else [None]
        if "w_o" not in g:
            return first
        return first + [cols(g["w_out_a"], 128), cols(g["w_out_b"], 128), rows(g["w_out_c"]), rows(g["w_o"])]

    grads = [None] * DEPTH
    dlbs_rows = [None] * DEPTH
    incoming = [None] * DEPTH
    for l in reversed(range(DEPTH)):
        L, S = layers[l], saved[l]
        n = f"l{l}_"
        dmerged = _matmul(dx, L["wo"], "nt", n + "d_merged", out_dtype=BF16)
        d_wo = _matmul(S["merged"], dx, "tn", n + "dw_o", out_dtype=BF16)
        dpg, dya, dyb, dyc, d_bg = _merge_bwd(dmerged, S["pg"], L["bg"], S["ya"], S["yb"], S["yc"], n + "merge_bwd")
        dua = _matmul(dya, L["woa"], "nt", n + "d_ua")
        dub = _matmul(dyb, L["wob"], "nt", n + "d_ub")
        duc = _matmul(dyc, L["woc"], "nt", n + "d_uc")
        d_woa = _matmul(S["ua"], dya, "tn", n + "dw_out_a", out_dtype=BF16)
        d_wob = _matmul(S["ub"], dyb, "tn", n + "dw_out_b", out_dtype=BF16)
        d_woc = _matmul(S["uc"], dyc, "tn", n + "dw_out_c", out_dtype=BF16)
        dpa, d_cwa = _branch_a_bwd(S["pa"], L["cwa"], dua, n + "conv_bwd")
        out_grads = dict(w_out_a=d_woa, w_out_b=d_wob, w_out_c=d_woc, w_o=d_wo)
        carry = (outgoing(out_grads)[1:], False) if l == 0 else None
        (dpb, d_lb, d_hnw), arrived_out = _branch_b_bwd(S["pb"], S["sb"], L["lb"], L["hnw"], dub, n + "hgrn_bwd", exchange=carry)
        carry = (outgoing(grads[l + 1]), False) if l + 1 < DEPTH else None
        (dpc, d_cwc, d_cpar, d_gnw), arrived = _branch_c_bwd(S["pc"], S["sc"], S["xc"], L["cwc"], L["cpar"], L["gnw"], duc,
                                                             n + "gdn_bwd", exchange=carry)
        if carry is not None:
            incoming[l + 1] = arrived
        d_win = _ungroup_dw_in(*[_matmul(S["h"], dp, "tn", f"{n}dw_{piece}", out_dtype=BF16)
                                 for dp, piece in ((dpa, "a"), (dpb, "b"), (dpc, "c"), (dpg, "g"))])
        carry = (outgoing(dict(w_in=d_win)), False) if l == 0 else None
        dh, arrived_in = _matmul_nt_sum([(dpa, L["wa"]), (dpb, L["wb"]), (dpc, L["wc"]), (dpg, L["wg"])], n + "dh",
                                        exchange=carry)
        (dx, d_nw), _ = _rmsnorm_bwd(dh, S["x"], L["nw"], dx, n + "rms_bwd")
        dlbs_rows[l] = d_lb
        grads[l] = dict(w_in=d_win, w_out_a=d_woa, w_out_b=d_wob, w_out_c=d_woc, w_o=d_wo, norm_w=d_nw[0],
                        b_gate=d_bg[0], conv_a=d_cwa, conv_c=_ungroup_conv_c(d_cwc),
                        a_log=d_cpar[:, 0, 0:2].reshape(-1), dt_bias=d_cpar[:, 1, 0:2].reshape(-1),
                        hgrn_norm_w=d_hnw[0], gdn_norm_w=d_gnw[0])
    grad_x = dx[None]
    d_lower = _lower_bounds_bwd(lb_pad, jnp.pad(jnp.concatenate(dlbs_rows, axis=0), ((0, 8 - DEPTH), (0, 0))),
                                "lower_bounds_bwd")[:DEPTH]

    incoming[0] = list(arrived_in) + list(arrived_out)
    stack = lambda name: jnp.stack([grads[l][name] for l in range(DEPTH)])
    big_out = {}
    for j, (name, w, m, v) in enumerate((("w_in", w_in, m_w_in, v_w_in), ("w_out_a", w_out_a, m_w_out_a, v_w_out_a),
                                         ("w_out_b", w_out_b, m_w_out_b, v_w_out_b), ("w_out_c", w_out_c, m_w_out_c, v_w_out_c),
                                         ("w_o", w_o, m_w_o, v_w_o))):
        parts = [incoming[l][j] for l in range(DEPTH)]
        r2 = lambda a: a.reshape(DEPTH * parts[0].shape[1], parts[0].shape[2])
        outs = _sum_adamw(parts, r2(w), r2(m), r2(v), "adamw_" + name)
        big_out[name] = [o.reshape(w.shape) for o in outs]

    small_names = ["norm_w", "b_gate", "conv_a", "conv_c", "a_log", "dt_bias", "lower_bounds", "hgrn_norm_w",
                   "gdn_norm_w", "final_norm_w", "loss"]
    small_vals = {k: stack(k) for k in ("norm_w", "b_gate", "conv_a", "conv_c", "a_log", "dt_bias", "hgrn_norm_w", "gdn_norm_w")}
    small_vals.update(lower_bounds=d_lower, final_norm_w=d_final[0], loss=loss_part.reshape(1))
    small_shapes = [small_vals[k].shape for k in small_names]
    small_rows = _rows_for(small_shapes)
    small_parts, = _exchange([_pack([small_vals[k] for k in small_names], small_rows)], "exchange_small", broadcast=True)
    total = dict(zip(small_names, _unpack(_sum_slots(small_parts, "sum_small"), small_shapes)))
    loss = total["loss"][0]
    g_conv_a = lax.dynamic_slice(total["conv_a"], (0, 0, me * conv_a.shape[2]), conv_a.shape)
    g_conv_c = lax.dynamic_slice(total["conv_c"], (0, 0, me * conv_c.shape[2]), conv_c.shape)

    small_w = dict(norm_w=(norm_w, m_norm_w, v_norm_w), b_gate=(b_gate, m_b_gate, v_b_gate),
                   conv_a=(conv_a, m_conv_a, v_conv_a), conv_c=(conv_c, m_conv_c, v_conv_c),
                   a_log=(a_log, m_a_log, v_a_log), dt_bias=(dt_bias, m_dt_bias, v_dt_bias),
                   lower_bounds=(lower_bounds, m_lower_bounds, v_lower_bounds),
                   hgrn_norm_w=(hgrn_norm_w, m_hgrn_norm_w, v_hgrn_norm_w), gdn_norm_w=(gdn_norm_w, m_gdn_norm_w, v_gdn_norm_w),
                   final_norm_w=(final_norm_w, m_final_norm_w, v_final_norm_w))
    small_g = dict(total, conv_a=g_conv_a, conv_c=g_conv_c)
    upd_names = small_names[:-1]
    upd_shapes = [small_w[k][0].shape for k in upd_names]
    upd_rows = _rows_for(upd_shapes)
    pk = lambda j: _pack([small_w[k][j] for k in upd_names], upd_rows)
    s_delta, s_m, s_v = _adamw(_pack([small_g[k] for k in upd_names], upd_rows), pk(0), pk(1), pk(2), "adamw_small")
    small_out = {k: [small_g[k], d, mm, vv] for k, d, mm, vv in
                 zip(upd_names, _unpack(s_delta, upd_shapes), _unpack(s_m, upd_shapes), _unpack(s_v, upd_shapes))}

    order = ["norm_w", "w_in", "b_gate", "conv_a", "conv_c", "a_log", "dt_bias", "lower_bounds", "hgrn_norm_w",
             "gdn_norm_w", "w_out_a", "w_out_b", "w_out_c", "w_o", "final_norm_w"]
    res = {**small_out, **big_out}
    outs = [loss, grad_x]
    for j in range(4):
        outs += [res[k][j] for k in order]
    return tuple(outs)
```

```python
import functools

import jax
import jax.numpy as jnp
from jax import lax
from jax.experimental import pallas as pl
from jax.experimental.pallas import tpu as pltpu

F32 = jnp.float32
BF16 = jnp.bfloat16
MESH = pl.DeviceIdType.MESH

N_DEV = 8
D = 1024
DEPTH = 2
CHUNK = 64
HGRN_BLOCK_CHUNKS = 16
GDN_BLOCK_CHUNKS = 8
GROUP = 128
NORM_EPS = 1e-6
L2_EPS = 1e-6
MIN_F = 1e-30
HD = 128
HGRN_HEADS = 4
GDN_QK_HEADS = 4
CONV_W = 512
IN_COLS = 10256
OFF_A, OFF_B, OFF_CQ, OFF_CK, OFF_CV, OFF_BETA, OFF_CA, OFF_CZ, OFF_G = (
    0, 2048, 4096, 4608, 5120, 6144, 6152, 6160, 7184)
NA, NB, NC_COLS, NG = 2048, 2048, 3584, 3072
C_HEAD = 896

ADAM_LR, ADAM_B1, ADAM_B2, ADAM_EPS, ADAM_WD, ADAM_STEP = 0.001, 0.9, 0.999, 1e-08, 0.01, 10

VMEM_LIMIT = 56 * 1024 * 1024
MM_TILE = 1024


def _cparams(*sem):
    return pltpu.CompilerParams(dimension_semantics=sem, vmem_limit_bytes=VMEM_LIMIT)


def _tile(dim, cap):
    if dim <= cap:
        return dim
    t = (cap // 128) * 128
    while dim % t:
        t -= 128
    return t


def _sigmoid(x):
    return 1.0 / (1.0 + jnp.exp(-x))


def _silu(x):
    return x * _sigmoid(x)


def _softplus(x):
    return jnp.maximum(x, 0.0) + jnp.log(1.0 + jnp.exp(-jnp.abs(x)))


def _dot(a, b, dims, precision=None):
    if precision is None:
        a, b = a.astype(BF16), b.astype(BF16)
    return lax.dot_general(a, b, (dims, ((), ())), precision=precision, preferred_element_type=F32)


def _nn(a, b, precision=None):
    return _dot(a, b, ((1,), (0,)), precision)


def _nt(a, b, precision=None):
    return _dot(a, b, ((1,), (1,)), precision)


def _tn(a, b, precision=None):
    return _dot(a, b, ((0,), (0,)), precision)


def _sum_rows_split(mat01, x):
    m = mat01.astype(BF16)
    hi = x.astype(BF16)
    low = (x - hi.astype(F32)).astype(BF16)
    return _nn(m, hi) + _nn(m, low)


@functools.partial(jax.custom_vjp, nondiff_argnums=(1,))
def _shift_rows(x, d):
    return x if d == 0 else pltpu.roll(x, d, 0)


def _shift_rows_fwd(x, d):
    return _shift_rows(x, d), None


def _shift_rows_bwd(d, _, ct):
    return ((ct if d == 0 else pltpu.roll(ct, ct.shape[0] - d, 0)),)


_shift_rows.defvjp(_shift_rows_fwd, _shift_rows_bwd)


def _iota2(shape):
    return lax.broadcasted_iota(jnp.int32, shape, 0), lax.broadcasted_iota(jnp.int32, shape, 1)


def _lane_pick(x, i):
    lane = lax.broadcasted_iota(jnp.int32, x.shape, 1)
    return jnp.sum(jnp.where(lane == i, x, 0.0), axis=1, keepdims=True)


def _hgrn_block(qr, fr, ir, zr, st0, lb, nw):
    rows = qr.shape[0]
    r, c = _iota2((CHUNK, CHUNK))
    halves = [1 << j for j in range(CHUNK.bit_length() - 1)]
    mats = [c <= r, c > r]
    pairs = []
    for hb in halves:
        same = (r // hb) == (c // hb)
        if hb > 1:
            mats += [(c <= r) & same, (c > r) & same]
        pairs.append(((r // (2 * hb)) == (c // (2 * hb))) & ((r // hb) == (c // hb) + 1))
    stack = jnp.concatenate([m.astype(F32) for m in mats], axis=0)

    q = _silu(qr) * (HD ** -0.5)
    fg = lb + (1.0 - lb) * _sigmoid(fr)
    logf = jnp.log(jnp.maximum(fg, MIN_F))
    kk = 1.0 - fg
    v = ir

    chunks = [slice(s, s + CHUNK) for s in range(0, rows, CHUNK)]
    cums = [_sum_rows_split(stack, logf[sl]) for sl in chunks]
    part = lambda i: jnp.concatenate([cs[i * CHUNK:(i + 1) * CHUNK] for cs in cums], axis=0)
    qg = q * jnp.exp(part(0))
    ks = kk * jnp.exp(part(1))
    q_lv = [q * jnp.exp(logf)] + [q * jnp.exp(part(2 * j)) for j in range(1, len(halves))]
    k_lv = [kk] + [kk * jnp.exp(part(2 * j + 1)) for j in range(1, len(halves))]
    st = st0
    outs = []
    for sl in chunks:
        scores = jnp.where(pairs[0], _nt(q_lv[0][sl], k_lv[0][sl]), 0.0)
        for j in range(1, len(halves)):
            scores += jnp.where(pairs[j], _nt(q_lv[j][sl], k_lv[j][sl]), 0.0)
        outs.append(_nn(scores, v[sl]) + _nt(qg[sl], st))
        st = st * jnp.exp(jnp.sum(logf[sl], axis=0, keepdims=True)) + _tn(v[sl], ks[sl])
    o = jnp.concatenate(outs, axis=0) + jnp.sum(q * kk, axis=1, keepdims=True) * v
    y = o * lax.rsqrt(jnp.mean(o * o, axis=1, keepdims=True) + NORM_EPS) * nw * _silu(zr)
    return y, st


def _unit_lower_inverses(ms):
    r, c = _iota2(ms[0].shape)
    xs = [jnp.where(r == c, 1.0, 0.0) - jnp.where((r // 2) == (c // 2), m, 0.0) for m in ms]
    b = 2
    while b < CHUNK:
        pick = ((r // (2 * b)) == (c // (2 * b))) & ((r // b) != (c // b))
        ts = [_nn(x, jnp.where(pick, m, 0.0)) for x, m in zip(xs, ms)]
        xs = [x - _nn(t, x) for x, t in zip(xs, ts)]
        b *= 2
    return tuple(x.astype(BF16) for x in xs)


@jax.custom_vjp
def _known_inverses(ms, xs):
    return xs


def _known_inverses_fwd(ms, xs):
    return xs, xs


def _known_inverses_bwd(xs, cts):
    r, c = _iota2(xs[0].shape)
    keep = (c < r) & ((r // CHUNK) == (c // CHUNK))
    ts = [_tn(x, ct) for x, ct in zip(xs, cts)]
    return (tuple(jnp.where(keep, -_nt(t, x), 0.0) for t, x in zip(ts, xs)), tuple(jnp.zeros_like(x) for x in xs))


_known_inverses.defvjp(_known_inverses_fwd, _known_inverses_bwd)


def _chunk_cumsum(x):
    row = lax.broadcasted_iota(jnp.int32, x.shape, 0) % CHUNK
    d = 1
    while d < CHUNK:
        x = x + jnp.where(row >= d, _shift_rows(x, d), 0.0)
        d *= 2
    return x


def _gdn_block(x_ext, z, ba, s0a, s0b, w0, w1, w2, w3, alog, dtb, nw, known=None):
    rows = z.shape[0]
    conv = (w0 * _shift_rows(x_ext, 3) + w1 * _shift_rows(x_ext, 2) + w2 * _shift_rows(x_ext, 1) + w3 * x_ext)
    cc = _silu(conv[8:])
    qc, kc = cc[:, 0:HD], cc[:, HD:2 * HD]
    q = qc * lax.rsqrt(jnp.sum(qc * qc, axis=1, keepdims=True) + L2_EPS) * (HD ** -0.5)
    k = kc * lax.rsqrt(jnp.sum(kc * kc, axis=1, keepdims=True) + L2_EPS)

    r, c = _iota2((GROUP, GROUP))
    same = (r // CHUNK) == (c // CHUNK)
    causal, strict, eye = same & (c <= r), same & (c < r), r == c
    heads = (0, 1)
    groups = [slice(lo, lo + GROUP) for lo in range(0, rows, GROUP)]
    chunks = [slice(lo, lo + CHUNK) for lo in range(0, rows, CHUNK)]

    v, loga, g_w, kb, kg, qg = [], [], [], [], [], []
    for i in heads:
        v.append(cc[:, (2 + i) * HD:(3 + i) * HD])
        beta = _sigmoid(_lane_pick(ba, i))
        a_neg = -jnp.exp(_lane_pick(alog, i))
        loga.append(a_neg * _softplus(_lane_pick(ba, 2 + i) + _lane_pick(dtb, i)))
        g_w.append(_chunk_cumsum(jnp.broadcast_to(loga[i], (rows, HD))))
        kb.append(k * beta)
        kg.append(k * jnp.exp(g_w[i]))
        qg.append(q * jnp.exp(g_w[i]))

    systems = [(i, gs) for gs in groups for i in heads]
    dec_c, ms = [], []
    for i, gs in systems:
        g_sq = g_w[i][gs]
        g_row = jnp.sum(jnp.where(eye, g_sq, 0.0), axis=0, keepdims=True)
        diff = g_sq - g_row
        dec_c.append(jnp.where(causal, jnp.exp(jnp.where(causal, diff, 0.0)), 0.0))
        ms.append(jnp.where(strict, _nt(k[gs], kb[i][gs]) * dec_c[-1], 0.0))
    xs = _unit_lower_inverses(tuple(ms)) if known is None else _known_inverses(tuple(ms), known)
    u = [[None] * len(groups) for _ in heads]
    w = [[None] * len(groups) for _ in heads]
    qk = [[None] * len(groups) for _ in heads]
    for n, (i, gs) in enumerate(systems):
        j = n // len(heads)
        u[i][j] = _nn(xs[n], v[i][gs])
        w[i][j] = _nn(xs[n], kg[i][gs])
        qk[i][j] = _nt(q[gs], kb[i][gs]) * dec_c[n]
    u = [jnp.concatenate(p, axis=0) for p in u]
    w = [jnp.concatenate(p, axis=0) for p in w]

    decay, p_mat, q_mat = {}, {}, {}
    for n, sl in enumerate(chunks):
        for i in heads:
            g_last = jnp.sum(loga[i][sl], axis=0, keepdims=True)
            kd = kb[i][sl] * jnp.exp(g_last - g_w[i][sl])
            decay[n, i] = jnp.exp(g_last)
            p_mat[n, i] = -_tn(kd, w[i][sl])
            q_mat[n, i] = _tn(kd, u[i][sl])
    s = [s0a, s0b]
    s_at = {}
    for n in range(len(chunks)):
        for i in heads:
            s_at[n, i] = s[i]
            s[i] = s[i] * decay[n, i] + _nn(p_mat[n, i], s[i]) + q_mat[n, i]

    ys = []
    for i in heads:
        e = jnp.concatenate([u[i][sl] - _nn(w[i][sl], s_at[n, i]) for n, sl in enumerate(chunks)], axis=0)
        o_state = jnp.concatenate([_nn(qg[i][sl], s_at[n, i]) for n, sl in enumerate(chunks)], axis=0)
        o = o_state + jnp.concatenate([_nn(qk[i][j], e[gs]) for j, gs in enumerate(groups)], axis=0)
        zi = z[:, i * HD:(i + 1) * HD]
        ys.append(o * lax.rsqrt(jnp.mean(o * o, axis=1, keepdims=True) + NORM_EPS) * nw * _silu(zi))
    return (jnp.concatenate(ys, axis=1), s[0], s[1]), xs


def _add_to_tail(x, tail):
    return x + jnp.concatenate([jnp.zeros((x.shape[0] - 8, x.shape[1]), x.dtype), tail], axis=0)


def _conv_a_block(ab, ac_ext, ax_ext, az, w0, w1, w2):
    u = ac_ext * ax_ext
    conv = (w0 * _shift_rows(u, 2) + w1 * _shift_rows(u, 1) + w2 * u)[8:]
    return ab * conv * _silu(az)


def _matmul(a, b, mode, name, residual=None, out_dtype=F32):
    if mode == "nn":
        (m, k), n = a.shape, b.shape[1]
    elif mode == "nt":
        (m, k), n = a.shape, b.shape[0]
    else:
        (k, m), n = a.shape, b.shape[1]
    tm, tn, tk = _tile(m, MM_TILE), _tile(n, MM_TILE), _tile(k, MM_TILE)
    nk = k // tk
    dims = {"nn": ((1,), (0,)), "nt": ((1,), (1,)), "tn": ((0,), (0,))}[mode]
    a_spec = pl.BlockSpec((tk, tm), lambda i, j, s: (s, i)) if mode == "tn" else pl.BlockSpec((tm, tk), lambda i, j, s: (i, s))
    b_spec = pl.BlockSpec((tn, tk), lambda i, j, s: (j, s)) if mode == "nt" else pl.BlockSpec((tk, tn), lambda i, j, s: (s, j))
    o_spec = pl.BlockSpec((tm, tn), lambda i, j, s: (i, j))
    has_res = residual is not None

    def finish(out, r_ref, o_ref):
        if has_res:
            out = out + r_ref[...]
        o_ref[...] = out.astype(out_dtype)

    def body_one_pass(*refs):
        finish(_dot(refs[0][...], refs[1][...], dims), refs[2] if has_res else None, refs[-1])

    def body_reduce(*refs):
        a_ref, b_ref = refs[0], refs[1]
        r_ref = refs[2] if has_res else None
        o_ref, acc_ref = refs[-2], refs[-1]
        s = pl.program_id(2)

        @pl.when(s == 0)
        def _():
            acc_ref[...] = jnp.zeros_like(acc_ref)

        acc_ref[...] += _dot(a_ref[...], b_ref[...], dims)

        @pl.when(s == nk - 1)
        def _():
            finish(acc_ref[...], r_ref, o_ref)

    args, specs = [a, b], [a_spec, b_spec]
    if has_res:
        args.append(residual)
        specs.append(o_spec)
    return pl.pallas_call(
        body_one_pass if nk == 1 else body_reduce, name=name, grid=(m // tm, n // tn, nk), in_specs=specs, out_specs=o_spec,
        out_shape=jax.ShapeDtypeStruct((m, n), out_dtype),
        scratch_shapes=[] if nk == 1 else [pltpu.VMEM((tm, tn), F32)],
        compiler_params=_cparams("parallel", "parallel", "arbitrary"))(*args)


def _matmul_nt_sum(pairs, name, exchange=None):
    m, n = pairs[0][0].shape[0], pairs[0][1].shape[0]
    tm, tn = _tile(m, MM_TILE), _tile(n, MM_TILE)
    tks = [_tile(a.shape[1], MM_TILE) for a, _ in pairs]
    nks = [a.shape[1] // tk for (a, _), tk in zip(pairs, tks)]
    offs = [sum(nks[:i]) for i in range(len(pairs))]
    total = sum(nks)

    def body(*refs):
        o_ref, acc_ref = refs[-2], refs[-1]
        s = pl.program_id(2)

        @pl.when(s == 0)
        def _():
            acc_ref[...] = jnp.zeros_like(acc_ref)

        for i, (off, nk) in enumerate(zip(offs, nks)):
            @pl.when((s >= off) & (s < off + nk))
            def _(i=i):
                acc_ref[...] += _dot(refs[2 * i][...], refs[2 * i + 1][...], ((1,), (1,)))

        @pl.when(s == total - 1)
        def _():
            o_ref[...] = acc_ref[...]

    args, specs = [], []
    for (a, b), tk, off, nk in zip(pairs, tks, offs, nks):
        k_of = lambda s, off=off, nk=nk: jnp.clip(s - off, 0, nk - 1)
        args += [a, b]
        specs += [pl.BlockSpec((tm, tk), lambda i, j, s, k_of=k_of: (i, k_of(s))),
                  pl.BlockSpec((tn, tk), lambda i, j, s, k_of=k_of: (j, k_of(s)))]
    (out,), exchanged = _call_with_exchange(
        body, name=name, grid=(m // tm, n // tn, total), in_specs=specs,
        out_specs=[pl.BlockSpec((tm, tn), lambda i, j, s: (i, j))], out_shape=[jax.ShapeDtypeStruct((m, n), F32)],
        scratch_shapes=[pltpu.VMEM((tm, tn), F32)], args=args, exchange=exchange)
    return out, exchanged


def _rmsnorm_fwd(x, w, name):
    t = x.shape[0]
    blk = _tile(t, 512)

    def body(x_ref, w_ref, h_ref):
        xv = x_ref[...]
        h_ref[...] = (xv * lax.rsqrt(jnp.mean(xv * xv, axis=1, keepdims=True) + NORM_EPS) * w_ref[...]).astype(BF16)

    return pl.pallas_call(
        body, name=name, grid=(t // blk,),
        in_specs=[pl.BlockSpec((blk, D), lambda i: (i, 0)), pl.BlockSpec((1, D), lambda i: (0, 0))],
        out_specs=pl.BlockSpec((blk, D), lambda i: (i, 0)), out_shape=jax.ShapeDtypeStruct((t, D), BF16),
        compiler_params=_cparams("parallel"))(x, w)


def _rmsnorm_bwd(dh, x, w, dxo, name, exchange=None):
    t = x.shape[0]
    blk = _tile(t, 512)

    def body(dh_ref, x_ref, w_ref, dxo_ref, dx_ref, dw_ref):
        @pl.when(pl.program_id(0) == 0)
        def _():
            dw_ref[...] = jnp.zeros_like(dw_ref)

        xv, dhv = x_ref[...], dh_ref[...]
        rs = lax.rsqrt(jnp.mean(xv * xv, axis=1, keepdims=True) + NORM_EPS)
        xh = xv * rs
        dw_ref[...] += jnp.sum(dhv * xh, axis=0, keepdims=True)
        dxh = dhv * w_ref[...]
        dx_ref[...] = rs * (dxh - xh * jnp.mean(dxh * xh, axis=1, keepdims=True)) + dxo_ref[...]

    row = pl.BlockSpec((blk, D), lambda i: (i, 0))
    vec = pl.BlockSpec((1, D), lambda i: (0, 0))
    return _call_with_exchange(
        body, name=name, grid=(t // blk,), in_specs=[row, row, vec, row], out_specs=[row, vec],
        out_shape=[jax.ShapeDtypeStruct((t, D), F32), jax.ShapeDtypeStruct((1, D), F32)],
        scratch_shapes=[], args=(dh, x, w, dxo), exchange=exchange)


def _loss_head(x, w, target, name):
    t = x.shape[0]
    blk = _tile(t, 512)

    def body(x_ref, w_ref, t_ref, loss_ref, dx_ref, dw_ref):
        @pl.when(pl.program_id(0) == 0)
        def _():
            dw_ref[...] = jnp.zeros_like(dw_ref)
            loss_ref[...] = jnp.zeros_like(loss_ref)

        xv = x_ref[...]
        rs = lax.rsqrt(jnp.mean(xv * xv, axis=1, keepdims=True) + NORM_EPS)
        xh = xv * rs
        err = xh * w_ref[...] - t_ref[...]
        loss_ref[...] += 0.5 * jnp.sum(jnp.mean(err * err, axis=1, keepdims=True), axis=0, keepdims=True)
        dy = err * (1.0 / D)
        dw_ref[...] += jnp.sum(dy * xh, axis=0, keepdims=True)
        dxh = dy * w_ref[...]
        dx_ref[...] = rs * (dxh - xh * jnp.mean(dxh * xh, axis=1, keepdims=True))

    row = pl.BlockSpec((blk, D), lambda i: (i, 0))
    vec = pl.BlockSpec((1, D), lambda i: (0, 0))
    return pl.pallas_call(
        body, name=name, grid=(t // blk,), in_specs=[row, vec, row],
        out_specs=[pl.BlockSpec((1, 1), lambda i: (0, 0)), row, vec],
        out_shape=[jax.ShapeDtypeStruct((1, 1), F32), jax.ShapeDtypeStruct((t, D), F32), jax.ShapeDtypeStruct((1, D), F32)],
        compiler_params=_cparams("arbitrary"))(x, w, target)


def _lbs_of(lb):
    r = lax.broadcasted_iota(jnp.int32, lb.shape, 0)
    real = r < DEPTH
    mx = lax.stop_gradient(jnp.max(jnp.where(real, lb, -jnp.inf), axis=0, keepdims=True))
    e = jnp.where(real, jnp.exp(jnp.where(real, lb - mx, 0.0)), 0.0)
    p = e / jnp.sum(e, axis=0, keepdims=True)
    out = jnp.zeros_like(lb)
    run = jnp.zeros_like(mx)
    for l in range(1, DEPTH):
        run = run + jnp.sum(jnp.where(r == l, p, 0.0), axis=0, keepdims=True)
        out = out + jnp.where(r == l, run, 0.0)
    return out


def _lower_bounds_fwd(lbp, name):
    def body(lb_ref, o_ref):
        o_ref[...] = _lbs_of(lb_ref[...])

    return pl.pallas_call(body, name=name, out_shape=jax.ShapeDtypeStruct(lbp.shape, F32))(lbp)


def _lower_bounds_bwd(lbp, dlbs, name):
    def body(lb_ref, d_ref, o_ref):
        _, vjp = jax.vjp(_lbs_of, lb_ref[...])
        o_ref[...] = vjp(d_ref[...])[0]

    return pl.pallas_call(body, name=name, out_shape=jax.ShapeDtypeStruct(lbp.shape, F32))(lbp, dlbs)


def _branch_a_fwd(pa, cw, name):
    t = pa.shape[0]
    blk = _tile(t, 512)
    W = CONV_W

    def body(p_ref, w_ref, y_ref, hc_ref, hx_ref):
        @pl.when(pl.program_id(0) == 0)
        def _():
            hc_ref[...] = jnp.zeros_like(hc_ref)
            hx_ref[...] = jnp.zeros_like(hx_ref)

        ac, ax = p_ref[:, W:2 * W], p_ref[:, 2 * W:3 * W]
        y_ref[...] = _conv_a_block(
            p_ref[:, 0:W], jnp.concatenate([hc_ref[...], ac], axis=0), jnp.concatenate([hx_ref[...], ax], axis=0),
            p_ref[:, 3 * W:4 * W], w_ref[0:1, :], w_ref[1:2, :], w_ref[2:3, :]).astype(BF16)
        hc_ref[...] = p_ref[blk - 8:blk, W:2 * W]
        hx_ref[...] = p_ref[blk - 8:blk, 2 * W:3 * W]

    return pl.pallas_call(
        body, name=name, grid=(t // blk,),
        in_specs=[pl.BlockSpec((blk, NA), lambda i: (i, 0)), pl.BlockSpec((3, W), lambda i: (0, 0))],
        out_specs=pl.BlockSpec((blk, W), lambda i: (i, 0)), out_shape=jax.ShapeDtypeStruct((t, W), BF16),
        scratch_shapes=[pltpu.VMEM((8, W), F32), pltpu.VMEM((8, W), F32)],
        compiler_params=_cparams("arbitrary"))(pa, cw)


def _branch_a_bwd(pa, cw, dy, name):
    t = pa.shape[0]
    blk = _tile(t, 512)
    nt_ = t // blk
    W = CONV_W
    hb = blk // 8

    def body(p_ref, halo_ref, w_ref, dy_ref, dp_ref, dw_ref, chc_ref, chx_ref):
        i = pl.program_id(0)

        @pl.when(i == 0)
        def _():
            chc_ref[...] = jnp.zeros_like(chc_ref)
            chx_ref[...] = jnp.zeros_like(chx_ref)
            dw_ref[...] = jnp.zeros_like(dw_ref)

        keep = 1.0 - (i == nt_ - 1).astype(F32)
        hc = halo_ref[:, W:2 * W] * keep
        hx = halo_ref[:, 2 * W:3 * W] * keep
        ac_ext = jnp.concatenate([hc, p_ref[:, W:2 * W]], axis=0)
        ax_ext = jnp.concatenate([hx, p_ref[:, 2 * W:3 * W]], axis=0)
        _, vjp = jax.vjp(_conv_a_block, p_ref[:, 0:W], ac_ext, ax_ext, p_ref[:, 3 * W:4 * W],
                         w_ref[0:1, :], w_ref[1:2, :], w_ref[2:3, :])
        dab, dac, dax, daz, dw0, dw1, dw2 = vjp(dy_ref[...])
        dp_ref[:, 0:W] = dab.astype(BF16)
        dp_ref[:, W:2 * W] = _add_to_tail(dac[8:], chc_ref[...]).astype(BF16)
        dp_ref[:, 2 * W:3 * W] = _add_to_tail(dax[8:], chx_ref[...]).astype(BF16)
        dp_ref[:, 3 * W:4 * W] = daz.astype(BF16)
        chc_ref[...] = dac[:8] * keep
        chx_ref[...] = dax[:8] * keep
        dw_ref[0:1, :] += dw0
        dw_ref[1:2, :] += dw1
        dw_ref[2:3, :] += dw2

    rev = lambda i: (nt_ - 1 - i, 0)
    return pl.pallas_call(
        body, name=name, grid=(nt_,),
        in_specs=[pl.BlockSpec((blk, NA), rev),
                  pl.BlockSpec((8, NA), lambda i: (jnp.maximum((nt_ - 1 - i) * hb - 1, 0), 0)),
                  pl.BlockSpec((3, W), lambda i: (0, 0)),
                  pl.BlockSpec((blk, W), rev)],
        out_specs=[pl.BlockSpec((blk, NA), rev), pl.BlockSpec((3, W), lambda i: (0, 0))],
        out_shape=[jax.ShapeDtypeStruct((t, NA), BF16), jax.ShapeDtypeStruct((3, W), F32)],
        scratch_shapes=[pltpu.VMEM((8, W), F32), pltpu.VMEM((8, W), F32)],
        compiler_params=_cparams("arbitrary"))(pa, pa, cw, dy)


def _block_rows(t, chunks):
    return min(t, chunks * CHUNK)


def _branch_b_fwd(pb, lbs_row, nw, name, exchange=None):
    t = pb.shape[0]
    rows = _block_rows(t, HGRN_BLOCK_CHUNKS)
    nch = t // rows

    def body(p_ref, lb_ref, nw_ref, y_ref, s_ref, st_ref):
        @pl.when(pl.program_id(1) == 0)
        def _():
            st_ref[...] = jnp.zeros_like(st_ref)

        s_ref[0, 0] = st_ref[...]
        y, st1 = _hgrn_block(p_ref[:, 0:HD], p_ref[:, HD:2 * HD], p_ref[:, 2 * HD:3 * HD], p_ref[:, 3 * HD:4 * HD],
                             st_ref[...], lb_ref[...], nw_ref[...])
        y_ref[...] = y.astype(BF16)
        st_ref[...] = st1

    return _call_with_exchange(
        body, name=name, grid=(HGRN_HEADS, nch),
        in_specs=[pl.BlockSpec((rows, 4 * HD), lambda h, i: (i, h)),
                  pl.BlockSpec((1, HD), lambda h, i: (0, h)),
                  pl.BlockSpec((1, HD), lambda h, i: (0, 0))],
        out_specs=[pl.BlockSpec((rows, HD), lambda h, i: (i, h)),
                   pl.BlockSpec((1, 1, HD, HD), lambda h, i: (h, i, 0, 0))],
        out_shape=[jax.ShapeDtypeStruct((t, HGRN_HEADS * HD), BF16),
                   jax.ShapeDtypeStruct((HGRN_HEADS, nch, HD, HD), F32)],
        scratch_shapes=[pltpu.VMEM((HD, HD), F32)],
        args=(pb, lbs_row, nw), exchange=exchange)


def _branch_b_bwd(pb, states, lbs_row, nw, dy, name, exchange=None):
    t = pb.shape[0]
    rows = _block_rows(t, HGRN_BLOCK_CHUNKS)
    nch = t // rows

    def body(p_ref, s_ref, lb_ref, nw_ref, dy_ref, dp_ref, dlb_ref, dnw_ref, ds_ref):
        h, i = pl.program_id(0), pl.program_id(1)

        @pl.when(i == 0)
        def _():
            ds_ref[...] = jnp.zeros_like(ds_ref)
            dlb_ref[...] = jnp.zeros_like(dlb_ref)

        @pl.when((i == 0) & (h == 0))
        def _():
            dnw_ref[...] = jnp.zeros_like(dnw_ref)

        _, vjp = jax.vjp(_hgrn_block, p_ref[:, 0:HD], p_ref[:, HD:2 * HD], p_ref[:, 2 * HD:3 * HD],
                         p_ref[:, 3 * HD:4 * HD], s_ref[0, 0], lb_ref[...], nw_ref[...])
        dq, df, di, dz, ds0, dlb, dnw = vjp((dy_ref[...], ds_ref[...]))
        dp_ref[:, 0:HD] = dq.astype(BF16)
        dp_ref[:, HD:2 * HD] = df.astype(BF16)
        dp_ref[:, 2 * HD:3 * HD] = di.astype(BF16)
        dp_ref[:, 3 * HD:4 * HD] = dz.astype(BF16)
        ds_ref[...] = ds0
        dlb_ref[...] += dlb
        dnw_ref[...] += dnw

    rev = lambda h, i: (nch - 1 - i, h)
    return _call_with_exchange(
        body, name=name, grid=(HGRN_HEADS, nch),
        in_specs=[pl.BlockSpec((rows, 4 * HD), rev),
                  pl.BlockSpec((1, 1, HD, HD), lambda h, i: (h, nch - 1 - i, 0, 0)),
                  pl.BlockSpec((1, HD), lambda h, i: (0, h)),
                  pl.BlockSpec((1, HD), lambda h, i: (0, 0)),
                  pl.BlockSpec((rows, HD), rev)],
        out_specs=[pl.BlockSpec((rows, 4 * HD), rev),
                   pl.BlockSpec((1, HD), lambda h, i: (0, h)),
                   pl.BlockSpec((1, HD), lambda h, i: (0, 0))],
        out_shape=[jax.ShapeDtypeStruct((t, NB), BF16), jax.ShapeDtypeStruct((1, HGRN_HEADS * HD), F32),
                   jax.ShapeDtypeStruct((1, HD), F32)],
        scratch_shapes=[pltpu.VMEM((HD, HD), F32)],
        args=(pb, states, lbs_row, nw, dy), exchange=exchange)


def _branch_c_fwd(pc, cw, cpar, nw, name, exchange=None):
    t = pc.shape[0]
    rows = _block_rows(t, GDN_BLOCK_CHUNKS)
    nch = t // rows
    XW = 4 * HD

    nsys = 2 * rows // GROUP

    def body(p_ref, w_ref, cp_ref, nw_ref, y_ref, s_ref, x_ref, sa_ref, sb_ref, halo_ref):
        @pl.when(pl.program_id(1) == 0)
        def _():
            sa_ref[...] = jnp.zeros_like(sa_ref)
            sb_ref[...] = jnp.zeros_like(sb_ref)
            halo_ref[...] = jnp.zeros_like(halo_ref)

        s_ref[0, 0, 0] = sa_ref[...]
        s_ref[0, 0, 1] = sb_ref[...]
        x_ext = jnp.concatenate([halo_ref[...], p_ref[:, 0:XW]], axis=0)
        (y, s1a, s1b), xs = _gdn_block(x_ext, p_ref[:, XW:XW + 2 * HD], p_ref[:, XW + 2 * HD:XW + 3 * HD],
                                       sa_ref[...], sb_ref[...], w_ref[0:1, :], w_ref[1:2, :], w_ref[2:3, :], w_ref[3:4, :],
                                       cp_ref[0, 0:1, :], cp_ref[0, 1:2, :], nw_ref[...])
        for n in range(nsys):
            x_ref[0, 0, n] = xs[n]
        y_ref[...] = y.astype(BF16)
        sa_ref[...] = s1a
        sb_ref[...] = s1b
        halo_ref[...] = p_ref[rows - 8:rows, 0:XW]

    return _call_with_exchange(
        body, name=name, grid=(GDN_QK_HEADS, nch),
        in_specs=[pl.BlockSpec((rows, C_HEAD), lambda h, i: (i, h)),
                  pl.BlockSpec((4, XW), lambda h, i: (0, h)),
                  pl.BlockSpec((1, 8, HD), lambda h, i: (h, 0, 0)),
                  pl.BlockSpec((1, HD), lambda h, i: (0, 0))],
        out_specs=[pl.BlockSpec((rows, 2 * HD), lambda h, i: (i, h)),
                   pl.BlockSpec((1, 1, 2, HD, HD), lambda h, i: (h, i, 0, 0, 0)),
                   pl.BlockSpec((1, 1, nsys, GROUP, GROUP), lambda h, i: (h, i, 0, 0, 0))],
        out_shape=[jax.ShapeDtypeStruct((t, 2 * GDN_QK_HEADS * HD), BF16),
                   jax.ShapeDtypeStruct((GDN_QK_HEADS, nch, 2, HD, HD), F32),
                   jax.ShapeDtypeStruct((GDN_QK_HEADS, nch, nsys, GROUP, GROUP), BF16)],
        scratch_shapes=[pltpu.VMEM((HD, HD), F32), pltpu.VMEM((HD, HD), F32), pltpu.VMEM((8, XW), F32)],
        args=(pc, cw, cpar, nw), exchange=exchange)


def _branch_c_bwd(pc, states, inverses, cw, cpar, nw, dy, name, exchange=None):
    t = pc.shape[0]
    rows = _block_rows(t, GDN_BLOCK_CHUNKS)
    nch = t // rows
    XW = 4 * HD
    hb = rows // 8

    nsys = 2 * rows // GROUP

    def body(p_ref, halo_ref, s_ref, x_ref, w_ref, cp_ref, nw_ref, dy_ref, dp_ref, dw_ref, dcp_ref, dnw_ref,
             dsa_ref, dsb_ref, carry_ref):
        h, i = pl.program_id(0), pl.program_id(1)

        @pl.when(i == 0)
        def _():
            dsa_ref[...] = jnp.zeros_like(dsa_ref)
            dsb_ref[...] = jnp.zeros_like(dsb_ref)
            carry_ref[...] = jnp.zeros_like(carry_ref)
            dw_ref[...] = jnp.zeros_like(dw_ref)
            dcp_ref[...] = jnp.zeros_like(dcp_ref)

        @pl.when((i == 0) & (h == 0))
        def _():
            dnw_ref[...] = jnp.zeros_like(dnw_ref)

        keep = 1.0 - (i == nch - 1).astype(F32)
        x_ext = jnp.concatenate([halo_ref[:, 0:XW] * keep, p_ref[:, 0:XW]], axis=0)
        block = functools.partial(_gdn_block, known=tuple(x_ref[0, 0, n] for n in range(nsys)))
        _, vjp, _ = jax.vjp(block, x_ext, p_ref[:, XW:XW + 2 * HD], p_ref[:, XW + 2 * HD:XW + 3 * HD],
                            s_ref[0, 0, 0], s_ref[0, 0, 1], w_ref[0:1, :], w_ref[1:2, :], w_ref[2:3, :], w_ref[3:4, :],
                            cp_ref[0, 0:1, :], cp_ref[0, 1:2, :], nw_ref[...], has_aux=True)
        dx, dz, dba, dsa, dsb, dw0, dw1, dw2, dw3, dal, ddt, dnw = vjp((dy_ref[...], dsa_ref[...], dsb_ref[...]))
        dp_ref[:, 0:XW] = _add_to_tail(dx[8:], carry_ref[...]).astype(BF16)
        dp_ref[:, XW:XW + 2 * HD] = dz.astype(BF16)
        dp_ref[:, XW + 2 * HD:XW + 3 * HD] = dba.astype(BF16)
        carry_ref[...] = dx[:8] * keep
        dsa_ref[...] = dsa
        dsb_ref[...] = dsb
        dw_ref[0:1, :] += dw0
        dw_ref[1:2, :] += dw1
        dw_ref[2:3, :] += dw2
        dw_ref[3:4, :] += dw3
        dcp_ref[0, 0:1, :] += dal
        dcp_ref[0, 1:2, :] += ddt
        dnw_ref[...] += dnw

    rev = lambda h, i: (nch - 1 - i, h)
    return _call_with_exchange(
        body, name=name, grid=(GDN_QK_HEADS, nch),
        in_specs=[pl.BlockSpec((rows, C_HEAD), rev),
                  pl.BlockSpec((8, C_HEAD), lambda h, i: (jnp.maximum((nch - 1 - i) * hb - 1, 0), h)),
                  pl.BlockSpec((1, 1, 2, HD, HD), lambda h, i: (h, nch - 1 - i, 0, 0, 0)),
                  pl.BlockSpec((1, 1, nsys, GROUP, GROUP), lambda h, i: (h, nch - 1 - i, 0, 0, 0)),
                  pl.BlockSpec((4, XW), lambda h, i: (0, h)),
                  pl.BlockSpec((1, 8, HD), lambda h, i: (h, 0, 0)),
                  pl.BlockSpec((1, HD), lambda h, i: (0, 0)),
                  pl.BlockSpec((rows, 2 * HD), rev)],
        out_specs=[pl.BlockSpec((rows, C_HEAD), rev),
                   pl.BlockSpec((4, XW), lambda h, i: (0, h)),
                   pl.BlockSpec((1, 8, HD), lambda h, i: (h, 0, 0)),
                   pl.BlockSpec((1, HD), lambda h, i: (0, 0))],
        out_shape=[jax.ShapeDtypeStruct((t, NC_COLS), BF16), jax.ShapeDtypeStruct((4, GDN_QK_HEADS * XW), F32),
                   jax.ShapeDtypeStruct((GDN_QK_HEADS, 8, HD), F32), jax.ShapeDtypeStruct((1, HD), F32)],
        scratch_shapes=[pltpu.VMEM((HD, HD), F32), pltpu.VMEM((HD, HD), F32), pltpu.VMEM((8, XW), F32)],
        args=(pc, pc, states, inverses, cw, cpar, nw, dy), exchange=exchange)


def _merge_fwd(pg, bg, ya, yb, yc, name):
    t = pg.shape[0]
    blk = _tile(t, 256)

    def body(g_ref, b_ref, a_ref, b2_ref, c_ref, o_ref):
        gate = _sigmoid(g_ref[...] + b_ref[...])
        o_ref[...] = (gate[:, 0:D] * a_ref[...] + gate[:, D:2 * D] * b2_ref[...] + gate[:, 2 * D:3 * D] * c_ref[...]).astype(BF16)

    row = pl.BlockSpec((blk, D), lambda i: (i, 0))
    return pl.pallas_call(
        body, name=name, grid=(t // blk,),
        in_specs=[pl.BlockSpec((blk, NG), lambda i: (i, 0)), pl.BlockSpec((1, NG), lambda i: (0, 0)), row, row, row],
        out_specs=row, out_shape=jax.ShapeDtypeStruct((t, D), BF16),
        compiler_params=_cparams("parallel"))(pg, bg, ya, yb, yc)


def _merge_bwd(dm, pg, bg, ya, yb, yc, name):
    t = pg.shape[0]
    blk = _tile(t, 256)

    def body(dm_ref, g_ref, b_ref, a_ref, b2_ref, c_ref, dg_ref, da_ref, db_ref, dc_ref, dbg_ref):
        @pl.when(pl.program_id(0) == 0)
        def _():
            dbg_ref[...] = jnp.zeros_like(dbg_ref)

        gate = _sigmoid(g_ref[...] + b_ref[...])
        dmv = dm_ref[...].astype(F32)
        for j, (y_ref, dy_ref) in enumerate(((a_ref, da_ref), (b2_ref, db_ref), (c_ref, dc_ref))):
            gj = gate[:, j * D:(j + 1) * D]
            dy_ref[...] = (dmv * gj).astype(BF16)
            dgj = dmv * y_ref[...] * gj * (1.0 - gj)
            dg_ref[:, j * D:(j + 1) * D] = dgj.astype(BF16)
            dbg_ref[:, j * D:(j + 1) * D] += jnp.sum(dgj, axis=0, keepdims=True)

    row = pl.BlockSpec((blk, D), lambda i: (i, 0))
    wide = pl.BlockSpec((blk, NG), lambda i: (i, 0))
    vec = pl.BlockSpec((1, NG), lambda i: (0, 0))
    return pl.pallas_call(
        body, name=name, grid=(t // blk,), in_specs=[row, wide, vec, row, row, row],
        out_specs=[wide, row, row, row, vec],
        out_shape=[jax.ShapeDtypeStruct((t, NG), BF16)] + [jax.ShapeDtypeStruct((t, D), BF16)] * 3
                  + [jax.ShapeDtypeStruct((1, NG), F32)],
        compiler_params=_cparams("arbitrary"))(dm, pg, bg, ya, yb, yc)


def _adamw_math(w, g, m, v):
    m = ADAM_B1 * m + (1.0 - ADAM_B1) * g
    v = ADAM_B2 * v + (1.0 - ADAM_B2) * (g * g)
    m_hat = m / (1.0 - ADAM_B1 ** ADAM_STEP)
    v_hat = v / (1.0 - ADAM_B2 ** ADAM_STEP)
    delta = -ADAM_LR * (m_hat / (jnp.sqrt(v_hat) + ADAM_EPS) + ADAM_WD * w)
    return delta, m, v


def _sum_adamw(parts, w, m, v, name):
    layers = len(parts)
    r, c = parts[0].shape[1:]
    br = r if r <= 256 else 256
    nb = r // br
    assert r % br == 0 and w.shape == (layers * r, c)

    def body(*refs):
        w_ref, m_ref, v_ref, g_ref, d_ref, nm_ref, nv_ref = refs[layers:]
        for l in range(layers):
            @pl.when(pl.program_id(0) == l)
            def _(p_ref=refs[l]):
                g = p_ref[0].astype(F32)
                for k in range(1, N_DEV):
                    g = g + p_ref[k].astype(F32)
                g_ref[...] = g
                d_ref[...], nm_ref[...], nv_ref[...] = _adamw_math(w_ref[...], g, m_ref[...], v_ref[...])

    blk = pl.BlockSpec((br, c), lambda l, i: (l * nb + i, 0))
    part_specs = [pl.BlockSpec((N_DEV, br, c), lambda l, i, q=q: (0, jnp.where(l == q, i, jnp.where(l < q, 0, nb - 1)), 0))
                  for q in range(layers)]
    return pl.pallas_call(
        body, name=name, grid=(layers, nb), in_specs=part_specs + [blk, blk, blk], out_specs=[blk] * 4,
        out_shape=[jax.ShapeDtypeStruct((layers * r, c), F32)] * 4,
        compiler_params=_cparams("arbitrary", "arbitrary"))(*parts, w, m, v)


def _adamw(g, w, m, v, name):
    def body(g_ref, w_ref, m_ref, v_ref, d_ref, nm_ref, nv_ref):
        d_ref[...], nm_ref[...], nv_ref[...] = _adamw_math(w_ref[...], g_ref[...], m_ref[...], v_ref[...])

    return pl.pallas_call(body, name=name, out_shape=[jax.ShapeDtypeStruct(w.shape, F32)] * 3)(g, w, m, v)


def _sum_slots(parts, name):
    def body(p_ref, o_ref):
        g = p_ref[0]
        for k in range(1, N_DEV):
            g = g + p_ref[k]
        o_ref[...] = g

    return pl.pallas_call(body, name=name, out_shape=jax.ShapeDtypeStruct(parts.shape[1:], F32))(parts)


def _exchange(srcs, name, broadcast):
    n = len(srcs)

    def body(*refs):
        copies = _exchange_copies(refs[:n], refs[n:2 * n], *refs[2 * n:], broadcast)
        for cp in copies:
            cp.start()
        for cp in copies:
            cp.wait()

    return pl.pallas_call(
        body, name=name, in_specs=[HBM_SPEC] * n, out_specs=[HBM_SPEC] * n, out_shape=_exchange_shapes(srcs, broadcast),
        scratch_shapes=_exchange_semaphores(n))(*srcs)


def _gather_two_level(srcs, name):
    n = len(srcs)

    def body(*refs):
        src_refs, dst_refs = refs[:n], refs[n:2 * n]
        send_sems, recv_sems, local_sems = refs[2 * n:]
        x, y, c = lax.axis_index("x"), lax.axis_index("y"), lax.axis_index("c")
        index_of = lambda px, py, pc: 4 * px + 2 * py + pc
        me, other_core = index_of(x, y, c), (x, y, 1 - c)
        chips = [(1 - x, y), (x, 1 - y), (1 - x, 1 - y)]

        def copy(k, a, block, to, src=None):
            return pltpu.make_async_remote_copy(
                src_ref=dst_refs[a].at[block] if src is None else src, dst_ref=dst_refs[a].at[block],
                send_sem=send_sems.at[k, a], recv_sem=recv_sems.at[k, a], device_id=to, device_id_type=MESH)

        local = [pltpu.make_async_copy(src_refs[a], dst_refs[a].at[me], local_sems.at[a]) for a in range(n)]
        first = [copy(0, a, me, other_core, src=src_refs[a]) for a in range(n)]
        first += [copy(1 + j, a, me, (*chip, c), src=src_refs[a]) for j, chip in enumerate(chips) for a in range(n)]
        for cp in local + first:
            cp.start()
        passed = []
        for j, chip in enumerate(chips):
            block = index_of(*chip, c)
            for a in range(n):
                copy(1 + j, a, block, (x, y, c)).wait_recv()
            for a in range(n):
                passed.append(copy(4 + j, a, block, other_core))
                passed[-1].start()
        for a in range(n):
            copy(0, a, index_of(x, y, 1 - c), (x, y, c)).wait_recv()
        for j, chip in enumerate(chips):
            for a in range(n):
                copy(4 + j, a, index_of(*chip, 1 - c), (x, y, c)).wait_recv()
        for cp in first + passed:
            cp.wait_send()
        for cp in local:
            cp.wait()

    return pl.pallas_call(
        body, name=name, in_specs=[HBM_SPEC] * n, out_specs=[HBM_SPEC] * n, out_shape=_exchange_shapes(srcs, True),
        scratch_shapes=_exchange_semaphores(n))(*srcs)


HBM_SPEC = pl.BlockSpec(memory_space=pltpu.HBM)


def _exchange_shapes(srcs, broadcast):
    return [jax.ShapeDtypeStruct((N_DEV,) + (s.shape if broadcast else s.shape[1:]), s.dtype) for s in srcs]


def _exchange_semaphores(n):
    return [pltpu.SemaphoreType.DMA((N_DEV - 1, n)), pltpu.SemaphoreType.DMA((N_DEV - 1, n)), pltpu.SemaphoreType.DMA((n,))]


def _exchange_copies(src_refs, dst_refs, send_sems, recv_sems, local_sems, broadcast):
    x, y, c = lax.axis_index("x"), lax.axis_index("y"), lax.axis_index("c")
    me = 4 * x + 2 * y + c
    copies = []
    for k in range(1, N_DEV):
        px = 1 - x if (k >> 2) & 1 else x
        py = 1 - y if (k >> 1) & 1 else y
        pc = 1 - c if k & 1 else c
        peer = 4 * px + 2 * py + pc
        for a, (src, dst) in enumerate(zip(src_refs, dst_refs)):
            copies.append(pltpu.make_async_remote_copy(
                src_ref=src if broadcast else src.at[peer], dst_ref=dst.at[me],
                send_sem=send_sems.at[k - 1, a], recv_sem=recv_sems.at[k - 1, a],
                device_id=(px, py, pc), device_id_type=MESH))
    for a, (src, dst) in enumerate(zip(src_refs, dst_refs)):
        copies.append(pltpu.make_async_copy(src if broadcast else src.at[me], dst.at[me], local_sems.at[a]))
    return copies


def _call_with_exchange(body, *, name, grid, in_specs, out_specs, out_shape, scratch_shapes, args, exchange):
    if exchange is None:
        outs = pl.pallas_call(body, name=name, grid=grid, in_specs=in_specs, out_specs=out_specs, out_shape=out_shape,
                              scratch_shapes=scratch_shapes,
                              compiler_params=_cparams(*["arbitrary"] * len(grid)))(*args)
        return outs, None
    srcs, broadcast = exchange
    n, n_in, n_out, n_scr = len(srcs), len(args), len(out_shape), len(scratch_shapes)
    steps = 1
    for g in grid:
        steps *= g

    def hosted(*refs):
        ins, src_refs = refs[:n_in], refs[n_in:n_in + n]
        outs, dst_refs = refs[n_in + n:n_in + n + n_out], refs[n_in + n + n_out:n_in + 2 * n + n_out]
        scratch = refs[n_in + 2 * n + n_out:]
        step = pl.program_id(0)
        for axis in range(1, len(grid)):
            step = step * grid[axis] + pl.program_id(axis)

        @pl.when(step == 0)
        def _():
            for cp in _exchange_copies(src_refs, dst_refs, *scratch[n_scr:], broadcast):
                cp.start()

        body(*ins, *outs, *scratch[:n_scr])

        @pl.when(step == steps - 1)
        def _():
            for cp in _exchange_copies(src_refs, dst_refs, *scratch[n_scr:], broadcast):
                cp.wait()

    outs = pl.pallas_call(
        hosted, name=name, grid=grid, in_specs=list(in_specs) + [HBM_SPEC] * n, out_specs=list(out_specs) + [HBM_SPEC] * n,
        out_shape=list(out_shape) + _exchange_shapes(srcs, broadcast),
        scratch_shapes=list(scratch_shapes) + _exchange_semaphores(n),
        compiler_params=_cparams(*["arbitrary"] * len(grid)))(*args, *srcs)
    return outs[:n_out], outs[n_out:]


def _regroup_w_in(w):
    wa = w[:, OFF_A:OFF_A + NA]
    seg = lambda off, h, n=HD: w[:, off + h * n: off + (h + 1) * n]
    wb = jnp.concatenate([seg(OFF_B + s * 512, h) for h in range(HGRN_HEADS) for s in range(4)], axis=1)
    parts = []
    for h in range(GDN_QK_HEADS):
        small = jnp.concatenate(
            [w[:, OFF_BETA + 2 * h: OFF_BETA + 2 * h + 2], w[:, OFF_CA + 2 * h: OFF_CA + 2 * h + 2],
             jnp.zeros((w.shape[0], HD - 4), w.dtype)], axis=1)
        parts += [seg(OFF_CQ, h), seg(OFF_CK, h), seg(OFF_CV, h, 2 * HD), seg(OFF_CZ, h, 2 * HD), small]
    wc = jnp.concatenate(parts, axis=1)
    wg = w[:, OFF_G:OFF_G + NG]
    return wa, wb, wc, wg


def _ungroup_dw_in(da, db, dc, dg):
    bq = [jnp.concatenate([db[:, h * 512 + s * HD: h * 512 + (s + 1) * HD] for h in range(HGRN_HEADS)], axis=1)
          for s in range(4)]
    ch = lambda h, lo, hi: dc[:, h * C_HEAD + lo: h * C_HEAD + hi]
    heads = range(GDN_QK_HEADS)
    cq = jnp.concatenate([ch(h, 0, HD) for h in heads], axis=1)
    ck = jnp.concatenate([ch(h, HD, 2 * HD) for h in heads], axis=1)
    cv = jnp.concatenate([ch(h, 2 * HD, 4 * HD) for h in heads], axis=1)
    cz = jnp.concatenate([ch(h, 4 * HD, 6 * HD) for h in heads], axis=1)
    cbeta = jnp.concatenate([ch(h, 6 * HD, 6 * HD + 2) for h in heads], axis=1)
    ca = jnp.concatenate([ch(h, 6 * HD + 2, 6 * HD + 4) for h in heads], axis=1)
    return jnp.concatenate([da] + bq + [cq, ck, cv, cbeta, ca, cz, dg], axis=1)


def _regroup_conv_c(cw):
    parts = []
    for h in range(GDN_QK_HEADS):
        parts += [cw[:, h * HD:(h + 1) * HD], cw[:, 512 + h * HD: 512 + (h + 1) * HD],
                  cw[:, 1024 + 2 * h * HD: 1024 + (2 * h + 2) * HD]]
    return jnp.concatenate(parts, axis=1)


def _ungroup_conv_c(d):
    heads = range(GDN_QK_HEADS)
    q = jnp.concatenate([d[:, h * 512: h * 512 + HD] for h in heads], axis=1)
    k = jnp.concatenate([d[:, h * 512 + HD: h * 512 + 2 * HD] for h in heads], axis=1)
    v = jnp.concatenate([d[:, h * 512 + 2 * HD: h * 512 + 4 * HD] for h in heads], axis=1)
    return jnp.concatenate([q, k, v], axis=1)


def _numel(shape):
    n = 1
    for d in shape:
        n *= d
    return n


def _pack(arrays, rows):
    flat = jnp.concatenate([a.reshape(-1) for a in arrays])
    return jnp.pad(flat, (0, rows * 128 - flat.shape[0])).reshape(rows, 128)


def _unpack(packed, shapes):
    flat = packed.reshape(-1)
    out, off = [], 0
    for s in shapes:
        out.append(flat[off:off + _numel(s)].reshape(s))
        off += _numel(s)
    return out


def _rows_for(shapes):
    return -(-sum(_numel(s) for s in shapes) // 1024) * 8


def kernel(x, norm_w, w_in, b_gate, conv_a, conv_c, a_log, dt_bias, lower_bounds, hgrn_norm_w, gdn_norm_w, w_out_a, w_out_b, w_out_c, w_o, final_norm_w, loss_target, m_norm_w, m_w_in, m_b_gate, m_conv_a, m_conv_c, m_a_log, m_dt_bias, m_lower_bounds, m_hgrn_norm_w, m_gdn_norm_w, m_w_out_a, m_w_out_b, m_w_out_c, m_w_o, m_final_norm_w, v_norm_w, v_w_in, v_b_gate, v_conv_a, v_conv_c, v_a_log, v_dt_bias, v_lower_bounds, v_hgrn_norm_w, v_gdn_norm_w, v_w_out_a, v_w_out_b, v_w_out_c, v_w_o, v_final_norm_w):
    me = 4 * lax.axis_index("x") + 2 * lax.axis_index("y") + lax.axis_index("c")
    xs = x[0]
    target = loss_target[0]
    in_shard = w_in.shape[2]

    big = [w_in, w_out_a, w_out_b, w_out_c, w_o]
    shards_of = lambda l: [w[l].astype(BF16) for w in big]
    conv_shapes = [(DEPTH, 3, CONV_W), (DEPTH, 4, 2048)]
    conv_rows = _rows_for(conv_shapes)
    ca_full = lax.dynamic_update_slice(jnp.zeros(conv_shapes[0], F32), conv_a, (0, 0, me * conv_a.shape[2]))
    cc_full = lax.dynamic_update_slice(jnp.zeros(conv_shapes[1], F32), conv_c, (0, 0, me * conv_c.shape[2]))
    g_in0, conv_parts = _gather_two_level([w_in[0].astype(BF16), _pack([ca_full, cc_full], conv_rows)], "gather_l0")
    conv_a_full, conv_c_full = _unpack(_sum_slots(conv_parts, "sum_conv"), conv_shapes)

    lb_pad = jnp.pad(lower_bounds, ((0, 8 - DEPTH), (0, 0)))
    lbs = _lower_bounds_fwd(lb_pad, "lower_bounds_fwd")

    def input_weights(l, g_in):
        wa, wb, wc, wg = _regroup_w_in(jnp.concatenate([g_in[q] for q in range(N_DEV)], axis=1))
        lanes = lambda vec: jnp.pad(vec.reshape(GDN_QK_HEADS, 1, 2), ((0, 0), (0, 0), (0, HD - 2)))
        cpar = jnp.concatenate([lanes(a_log[l]), lanes(dt_bias[l]), jnp.zeros((GDN_QK_HEADS, 6, HD), F32)], axis=1)
        return dict(
            wa=wa, wb=wb, wc=wc, wg=wg, cpar=cpar,
            nw=norm_w[l:l + 1], bg=b_gate[l:l + 1], cwa=conv_a_full[l], cwc=_regroup_conv_c(conv_c_full[l]),
            lb=lbs[l:l + 1], hnw=hgrn_norm_w[l:l + 1], gnw=gdn_norm_w[l:l + 1])

    def output_weights(g_oa, g_ob, g_oc, g_o):
        return dict(woa=jnp.concatenate([g_oa[q] for q in range(N_DEV)], axis=1),
                    wob=jnp.concatenate([g_ob[q] for q in range(N_DEV)], axis=1), woc=g_oc.reshape(D, D), wo=g_o.reshape(D, D))

    layers = [input_weights(0, g_in0)]

    saved = []
    cur = xs
    for l in range(DEPTH):
        L = layers[l]
        n = f"l{l}_"
        h = _rmsnorm_fwd(cur, L["nw"], n + "rms")
        pa = _matmul(h, L["wa"], "nn", n + "proj_a")
        pb = _matmul(h, L["wb"], "nn", n + "proj_b")
        pc = _matmul(h, L["wc"], "nn", n + "proj_c")
        pg = _matmul(h, L["wg"], "nn", n + "proj_g", out_dtype=BF16)
        ua = _branch_a_fwd(pa, L["cwa"], n + "conv_fwd")
        carry = (shards_of(l)[1:], True) if l == 0 else None
        (ub, sb), gathered = _branch_b_fwd(pb, L["lb"], L["hnw"], n + "hgrn_fwd", exchange=carry)
        if carry is not None:
            L.update(output_weights(*gathered))
        carry = (shards_of(l + 1), True) if l + 1 < DEPTH else None
        (uc, sc, xc), gathered = _branch_c_fwd(pc, L["cwc"], L["cpar"], L["gnw"], n + "gdn_fwd", exchange=carry)
        if carry is not None:
            layers.append(dict(input_weights(l + 1, gathered[0]), **output_weights(*gathered[1:])))
        ya = _matmul(ua, L["woa"], "nn", n + "out_a", out_dtype=BF16)
        yb = _matmul(ub, L["wob"], "nn", n + "out_b", out_dtype=BF16)
        yc = _matmul(uc, L["woc"], "nn", n + "out_c", out_dtype=BF16)
        merged = _merge_fwd(pg, L["bg"], ya, yb, yc, n + "merge")
        nxt = _matmul(merged, L["wo"], "nn", n + "out_o", residual=cur)
        saved.append(dict(x=cur, h=h, pa=pa, pb=pb, pc=pc, pg=pg, ua=ua, ub=ub, uc=uc, sb=sb, sc=sc, xc=xc,
                          ya=ya, yb=yb, yc=yc, merged=merged))
        cur = nxt

    loss_part, dx, d_final = _loss_head(cur, final_norm_w.reshape(1, D), target, "loss_head")

    def outgoing(g):
        cols = lambda a, n: jnp.stack([a[:, p * n:(p + 1) * n] for p in range(N_DEV)]).astype(BF16)
        rows = lambda a: a.reshape(N_DEV, a.shape[0] // N_DEV, a.shape[1]).astype(BF16)
        first = [cols(g["w_in"], in_shard)] if "w_in" in g else [None]
        if "w_o" not in g:
            return first
        return first + [cols(g["w_out_a"], 128), cols(g["w_out_b"], 128), rows(g["w_out_c"]), rows(g["w_o"])]

    grads = [None] * DEPTH
    dlbs_rows = [None] * DEPTH
    incoming = [None] * DEPTH
    for l in reversed(range(DEPTH)):
        L, S = layers[l], saved[l]
        n = f"l{l}_"
        dmerged = _matmul(dx, L["wo"], "nt", n + "d_merged", out_dtype=BF16)
        d_wo = _matmul(S["merged"], dx, "tn", n + "dw_o", out_dtype=BF16)
        dpg, dya, dyb, dyc, d_bg = _merge_bwd(dmerged, S["pg"], L["bg"], S["ya"], S["yb"], S["yc"], n + "merge_bwd")
        dua = _matmul(dya, L["woa"], "nt", n + "d_ua")
        dub = _matmul(dyb, L["wob"], "nt", n + "d_ub")
        duc = _matmul(dyc, L["woc"], "nt", n + "d_uc")
        d_woa = _matmul(S["ua"], dya, "tn", n + "dw_out_a", out_dtype=BF16)
        d_wob = _matmul(S["ub"], dyb, "tn", n + "dw_out_b", out_dtype=BF16)
        d_woc = _matmul(S["uc"], dyc, "tn", n + "dw_out_c", out_dtype=BF16)
        dpa, d_cwa = _branch_a_bwd(S["pa"], L["cwa"], dua, n + "conv_bwd")
        out_grads = dict(w_out_a=d_woa, w_out_b=d_wob, w_out_c=d_woc, w_o=d_wo)
        carry = (outgoing(out_grads)[1:], False) if l == 0 else None
        (dpb, d_lb, d_hnw), arrived_out = _branch_b_bwd(S["pb"], S["sb"], L["lb"], L["hnw"], dub, n + "hgrn_bwd", exchange=carry)
        carry = (outgoing(grads[l + 1]), False) if l + 1 < DEPTH else None
        (dpc, d_cwc, d_cpar, d_gnw), arrived = _branch_c_bwd(S["pc"], S["sc"], S["xc"], L["cwc"], L["cpar"], L["gnw"], duc,
                                                             n + "gdn_bwd", exchange=carry)
        if carry is not None:
            incoming[l + 1] = arrived
        d_win = _ungroup_dw_in(*[_matmul(S["h"], dp, "tn", f"{n}dw_{piece}", out_dtype=BF16)
                                 for dp, piece in ((dpa, "a"), (dpb, "b"), (dpc, "c"), (dpg, "g"))])
        carry = (outgoing(dict(w_in=d_win)), False) if l == 0 else None
        dh, arrived_in = _matmul_nt_sum([(dpa, L["wa"]), (dpb, L["wb"]), (dpc, L["wc"]), (dpg, L["wg"])], n + "dh",
                                        exchange=carry)
        (dx, d_nw), _ = _rmsnorm_bwd(dh, S["x"], L["nw"], dx, n + "rms_bwd")
        dlbs_rows[l] = d_lb
        grads[l] = dict(w_in=d_win, w_out_a=d_woa, w_out_b=d_wob, w_out_c=d_woc, w_o=d_wo, norm_w=d_nw[0],
                        b_gate=d_bg[0], conv_a=d_cwa, conv_c=_ungroup_conv_c(d_cwc),
                        a_log=d_cpar[:, 0, 0:2].reshape(-1), dt_bias=d_cpar[:, 1, 0:2].reshape(-1),
                        hgrn_norm_w=d_hnw[0], gdn_norm_w=d_gnw[0])
    grad_x = dx[None]
    d_lower = _lower_bounds_bwd(lb_pad, jnp.pad(jnp.concatenate(dlbs_rows, axis=0), ((0, 8 - DEPTH), (0, 0))),
                                "lower_bounds_bwd")[:DEPTH]

    incoming[0] = list(arrived_in) + list(arrived_out)
    stack = lambda name: jnp.stack([grads[l][name] for l in range(DEPTH)])
    big_out = {}
    for j, (name, w, m, v) in enumerate((("w_in", w_in, m_w_in, v_w_in), ("w_out_a", w_out_a, m_w_out_a, v_w_out_a),
                                         ("w_out_b", w_out_b, m_w_out_b, v_w_out_b), ("w_out_c", w_out_c, m_w_out_c, v_w_out_c),
                                         ("w_o", w_o, m_w_o, v_w_o))):
        parts = [incoming[l][j] for l in range(DEPTH)]
        r2 = lambda a: a.reshape(DEPTH * parts[0].shape[1], parts[0].shape[2])
        outs = _sum_adamw(parts, r2(w), r2(m), r2(v), "adamw_" + name)
        big_out[name] = [o.reshape(w.shape) for o in outs]

    small_names = ["norm_w", "b_gate", "conv_a", "conv_c", "a_log", "dt_bias", "lower_bounds", "hgrn_norm_w",
                   "gdn_norm_w", "final_norm_w", "loss"]
    small_vals = {k: stack(k) for k in ("norm_w", "b_gate", "conv_a", "conv_c", "a_log", "dt_bias", "hgrn_norm_w", "gdn_norm_w")}
    small_vals.update(lower_bounds=d_lower, final_norm_w=d_final[0], loss=loss_part.reshape(1))
    small_shapes = [small_vals[k].shape for k in small_names]
    small_rows = _rows_for(small_shapes)
    small_parts, = _exchange([_pack([small_vals[k] for k in small_names], small_rows)], "exchange_small", broadcast=True)
    total = dict(zip(small_names, _unpack(_sum_slots(small_parts, "sum_small"), small_shapes)))
    loss = total["loss"][0]
    g_conv_a = lax.dynamic_slice(total["conv_a"], (0, 0, me * conv_a.shape[2]), conv_a.shape)
    g_conv_c = lax.dynamic_slice(total["conv_c"], (0, 0, me * conv_c.shape[2]), conv_c.shape)

    small_w = dict(norm_w=(norm_w, m_norm_w, v_norm_w), b_gate=(b_gate, m_b_gate, v_b_gate),
                   conv_a=(conv_a, m_conv_a, v_conv_a), conv_c=(conv_c, m_conv_c, v_conv_c),
                   a_log=(a_log, m_a_log, v_a_log), dt_bias=(dt_bias, m_dt_bias, v_dt_bias),
                   lower_bounds=(lower_bounds, m_lower_bounds, v_lower_bounds),
                   hgrn_norm_w=(hgrn_norm_w, m_hgrn_norm_w, v_hgrn_norm_w), gdn_norm_w=(gdn_norm_w, m_gdn_norm_w, v_gdn_norm_w),
                   final_norm_w=(final_norm_w, m_final_norm_w, v_final_norm_w))
    small_g = dict(total, conv_a=g_conv_a, conv_c=g_conv_c)
    upd_names = small_names[:-1]
    upd_shapes = [small_w[k][0].shape for k in upd_names]
    upd_rows = _rows_for(upd_shapes)
    pk = lambda j: _pack([small_w[k][j] for k in upd_names], upd_rows)
    s_delta, s_m, s_v = _adamw(_pack([small_g[k] for k in upd_names], upd_rows), pk(0), pk(1), pk(2), "adamw_small")
    small_out = {k: [small_g[k], d, mm, vv] for k, d, mm, vv in
                 zip(upd_names, _unpack(s_delta, upd_shapes), _unpack(s_m, upd_shapes), _unpack(s_v, upd_shapes))}

    order = ["norm_w", "w_in", "b_gate", "conv_a", "conv_c", "a_log", "dt_bias", "lower_bounds", "hgrn_norm_w",
             "gdn_norm_w", "w_out_a", "w_out_b", "w_out_c", "w_o", "final_norm_w"]
    res = {**small_out, **big_out}
    outs = [loss, grad_x]
    for j in range(4):
        outs += [res[k][j] for k in order]
    return tuple(outs)
```

```python
import functools

import jax
import jax.numpy as jnp
from jax import lax
from jax.experimental import pallas as pl
from jax.experimental.pallas import tpu as pltpu

F32 = jnp.float32
BF16 = jnp.bfloat16
MESH = pl.DeviceIdType.MESH

N_DEV = 8
D = 1024
DEPTH = 2
CHUNK = 64
HGRN_BLOCK_CHUNKS = 16
GDN_BLOCK_CHUNKS = 8
GROUP = 128
NORM_EPS = 1e-6
L2_EPS = 1e-6
MIN_F = 1e-30
HD = 128
HGRN_HEADS = 4
GDN_QK_HEADS = 4
CONV_W = 512
IN_COLS = 10256
OFF_A, OFF_B, OFF_CQ, OFF_CK, OFF_CV, OFF_BETA, OFF_CA, OFF_CZ, OFF_G = (
    0, 2048, 4096, 4608, 5120, 6144, 6152, 6160, 7184)
NA, NB, NC_COLS, NG = 2048, 2048, 3584, 3072
C_HEAD = 896

ADAM_LR, ADAM_B1, ADAM_B2, ADAM_EPS, ADAM_WD, ADAM_STEP = 0.001, 0.9, 0.999, 1e-08, 0.01, 10

VMEM_LIMIT = 56 * 1024 * 1024
MM_TILE = 1024


def _cparams(*sem):
    return pltpu.CompilerParams(dimension_semantics=sem, vmem_limit_bytes=VMEM_LIMIT)


def _tile(dim, cap):
    if dim <= cap:
        return dim
    t = (cap // 128) * 128
    while dim % t:
        t -= 128
    return t


def _sigmoid(x):
    return 1.0 / (1.0 + jnp.exp(-x))


def _silu(x):
    return x * _sigmoid(x)


def _softplus(x):
    return jnp.maximum(x, 0.0) + jnp.log(1.0 + jnp.exp(-jnp.abs(x)))


def _dot(a, b, dims, precision=None):
    if precision is None:
        a, b = a.astype(BF16), b.astype(BF16)
    return lax.dot_general(a, b, (dims, ((), ())), precision=precision, preferred_element_type=F32)


def _nn(a, b, precision=None):
    return _dot(a, b, ((1,), (0,)), precision)


def _nt(a, b, precision=None):
    return _dot(a, b, ((1,), (1,)), precision)


def _tn(a, b, precision=None):
    return _dot(a, b, ((0,), (0,)), precision)


def _sum_rows_split(mat01, x):
    m = mat01.astype(BF16)
    hi = x.astype(BF16)
    low = (x - hi.astype(F32)).astype(BF16)
    return _nn(m, hi) + _nn(m, low)


@functools.partial(jax.custom_vjp, nondiff_argnums=(1,))
def _shift_rows(x, d):
    return x if d == 0 else pltpu.roll(x, d, 0)


def _shift_rows_fwd(x, d):
    return _shift_rows(x, d), None


def _shift_rows_bwd(d, _, ct):
    return ((ct if d == 0 else pltpu.roll(ct, ct.shape[0] - d, 0)),)


_shift_rows.defvjp(_shift_rows_fwd, _shift_rows_bwd)


def _iota2(shape):
    return lax.broadcasted_iota(jnp.int32, shape, 0), lax.broadcasted_iota(jnp.int32, shape, 1)


def _lane_pick(x, i):
    lane = lax.broadcasted_iota(jnp.int32, x.shape, 1)
    return jnp.sum(jnp.where(lane == i, x, 0.0), axis=1, keepdims=True)


def _hgrn_block(qr, fr, ir, zr, st0, lb, nw):
    rows = qr.shape[0]
    r, c = _iota2((CHUNK, CHUNK))
    halves = [1 << j for j in range(CHUNK.bit_length() - 1)]
    mats = [c <= r, c > r]
    pairs = []
    for hb in halves:
        same = (r // hb) == (c // hb)
        if hb > 1:
            mats += [(c <= r) & same, (c > r) & same]
        pairs.append(((r // (2 * hb)) == (c // (2 * hb))) & ((r // hb) == (c // hb) + 1))
    stack = jnp.concatenate([m.astype(F32) for m in mats], axis=0)

    q = _silu(qr) * (HD ** -0.5)
    fg = lb + (1.0 - lb) * _sigmoid(fr)
    logf = jnp.log(jnp.maximum(fg, MIN_F))
    kk = 1.0 - fg
    v = ir

    chunks = [slice(s, s + CHUNK) for s in range(0, rows, CHUNK)]
    cums = [_sum_rows_split(stack, logf[sl]) for sl in chunks]
    part = lambda i: jnp.concatenate([cs[i * CHUNK:(i + 1) * CHUNK] for cs in cums], axis=0)
    qg = q * jnp.exp(part(0))
    ks = kk * jnp.exp(part(1))
    q_lv = [q * jnp.exp(logf)] + [q * jnp.exp(part(2 * j)) for j in range(1, len(halves))]
    k_lv = [kk] + [kk * jnp.exp(part(2 * j + 1)) for j in range(1, len(halves))]
    st = st0
    outs = []
    for sl in chunks:
        scores = jnp.where(pairs[0], _nt(q_lv[0][sl], k_lv[0][sl]), 0.0)
        for j in range(1, len(halves)):
            scores += jnp.where(pairs[j], _nt(q_lv[j][sl], k_lv[j][sl]), 0.0)
        outs.append(_nn(scores, v[sl]) + _nt(qg[sl], st))
        st = st * jnp.exp(jnp.sum(logf[sl], axis=0, keepdims=True)) + _tn(v[sl], ks[sl])
    o = jnp.concatenate(outs, axis=0) + jnp.sum(q * kk, axis=1, keepdims=True) * v
    y = o * lax.rsqrt(jnp.mean(o * o, axis=1, keepdims=True) + NORM_EPS) * nw * _silu(zr)
    return y, st


def _unit_lower_inverses(ms):
    r, c = _iota2(ms[0].shape)
    xs = [jnp.where(r == c, 1.0, 0.0) - jnp.where((r // 2) == (c // 2), m, 0.0) for m in ms]
    b = 2
    while b < CHUNK:
        pick = ((r // (2 * b)) == (c // (2 * b))) & ((r // b) != (c // b))
        ts = [_nn(x, jnp.where(pick, m, 0.0)) for x, m in zip(xs, ms)]
        xs = [x - _nn(t, x) for x, t in zip(xs, ts)]
        b *= 2
    return tuple(x.astype(BF16) for x in xs)


@jax.custom_vjp
def _known_inverses(ms, xs):
    return xs


def _known_inverses_fwd(ms, xs):
    return xs, xs


def _known_inverses_bwd(xs, cts):
    r, c = _iota2(xs[0].shape)
    keep = (c < r) & ((r // CHUNK) == (c // CHUNK))
    ts = [_tn(x, ct) for x, ct in zip(xs, cts)]
    return (tuple(jnp.where(keep, -_nt(t, x), 0.0) for t, x in zip(ts, xs)), tuple(jnp.zeros_like(x) for x in xs))


_known_inverses.defvjp(_known_inverses_fwd, _known_inverses_bwd)


def _chunk_cumsum(x):
    row = lax.broadcasted_iota(jnp.int32, x.shape, 0) % CHUNK
    d = 1
    while d < CHUNK:
        x = x + jnp.where(row >= d, _shift_rows(x, d), 0.0)
        d *= 2
    return x


def _gdn_block(x_ext, z, ba, s0a, s0b, w0, w1, w2, w3, alog, dtb, nw, known=None):
    rows = z.shape[0]
    conv = (w0 * _shift_rows(x_ext, 3) + w1 * _shift_rows(x_ext, 2) + w2 * _shift_rows(x_ext, 1) + w3 * x_ext)
    cc = _silu(conv[8:])
    qc, kc = cc[:, 0:HD], cc[:, HD:2 * HD]
    q = qc * lax.rsqrt(jnp.sum(qc * qc, axis=1, keepdims=True) + L2_EPS) * (HD ** -0.5)
    k = kc * lax.rsqrt(jnp.sum(kc * kc, axis=1, keepdims=True) + L2_EPS)

    r, c = _iota2((GROUP, GROUP))
    same = (r // CHUNK) == (c // CHUNK)
    causal, strict, eye = same & (c <= r), same & (c < r), r == c
    heads = (0, 1)
    groups = [slice(lo, lo + GROUP) for lo in range(0, rows, GROUP)]
    chunks = [slice(lo, lo + CHUNK) for lo in range(0, rows, CHUNK)]

    v, loga, g_w, kb, kg, qg = [], [], [], [], [], []
    for i in heads:
        v.append(cc[:, (2 + i) * HD:(3 + i) * HD])
        beta = _sigmoid(_lane_pick(ba, i))
        a_neg = -jnp.exp(_lane_pick(alog, i))
        loga.append(a_neg * _softplus(_lane_pick(ba, 2 + i) + _lane_pick(dtb, i)))
        g_w.append(_chunk_cumsum(jnp.broadcast_to(loga[i], (rows, HD))))
        kb.append(k * beta)
        kg.append(k * jnp.exp(g_w[i]))
        qg.append(q * jnp.exp(g_w[i]))

    systems = [(i, gs) for gs in groups for i in heads]
    dec_c, ms = [], []
    for i, gs in systems:
        g_sq = g_w[i][gs]
        g_row = jnp.sum(jnp.where(eye, g_sq, 0.0), axis=0, keepdims=True)
        diff = g_sq - g_row
        dec_c.append(jnp.where(causal, jnp.exp(jnp.where(causal, diff, 0.0)), 0.0))
        ms.append(jnp.where(strict, _nt(k[gs], kb[i][gs]) * dec_c[-1], 0.0))
    xs = _unit_lower_inverses(tuple(ms)) if known is None else _known_inverses(tuple(ms), known)
    u = [[None] * len(groups) for _ in heads]
    w = [[None] * len(groups) for _ in heads]
    qk = [[None] * len(groups) for _ in heads]
    for n, (i, gs) in enumerate(systems):
        j = n // len(heads)
        u[i][j] = _nn(xs[n], v[i][gs])
        w[i][j] = _nn(xs[n], kg[i][gs])
        qk[i][j] = _nt(q[gs], kb[i][gs]) * dec_c[n]
    u = [jnp.concatenate(p, axis=0) for p in u]
    w = [jnp.concatenate(p, axis=0) for p in w]

    decay, p_mat, q_mat = {}, {}, {}
    for n, sl in enumerate(chunks):
        for i in heads:
            g_last = jnp.sum(loga[i][sl], axis=0, keepdims=True)
            kd = kb[i][sl] * jnp.exp(g_last - g_w[i][sl])
            decay[n, i] = jnp.exp(g_last)
            p_mat[n, i] = -_tn(kd, w[i][sl])
            q_mat[n, i] = _tn(kd, u[i][sl])
    s = [s0a, s0b]
    s_at = {}
    for n in range(len(chunks)):
        for i in heads:
            s_at[n, i] = s[i]
            s[i] = s[i] * decay[n, i] + _nn(p_mat[n, i], s[i]) + q_mat[n, i]

    ys = []
    for i in heads:
        e = jnp.concatenate([u[i][sl] - _nn(w[i][sl], s_at[n, i]) for n, sl in enumerate(chunks)], axis=0)
        o_state = jnp.concatenate([_nn(qg[i][sl], s_at[n, i]) for n, sl in enumerate(chunks)], axis=0)
        o = o_state + jnp.concatenate([_nn(qk[i][j], e[gs]) for j, gs in enumerate(groups)], axis=0)
        zi = z[:, i * HD:(i + 1) * HD]
        ys.append(o * lax.rsqrt(jnp.mean(o * o, axis=1, keepdims=True) + NORM_EPS) * nw * _silu(zi))
    return (jnp.concatenate(ys, axis=1), s[0], s[1]), xs


def _add_to_tail(x, tail):
    return x + jnp.concatenate([jnp.zeros((x.shape[0] - 8, x.shape[1]), x.dtype), tail], axis=0)


def _conv_a_block(ab, ac_ext, ax_ext, az, w0, w1, w2):
    u = ac_ext * ax_ext
    conv = (w0 * _shift_rows(u, 2) + w1 * _shift_rows(u, 1) + w2 * u)[8:]
    return ab * conv * _silu(az)


def _matmul(a, b, mode, name, residual=None, out_dtype=F32):
    if mode == "nn":
        (m, k), n = a.shape, b.shape[1]
    elif mode == "nt":
        (m, k), n = a.shape, b.shape[0]
    else:
        (k, m), n = a.shape, b.shape[1]
    tm, tn, tk = _tile(m, MM_TILE), _tile(n, MM_TILE), _tile(k, MM_TILE)
    if mode == "tn":
        tk = _tile(k, 2 * MM_TILE)
    elif k == tk:
        tm = _tile(m, 2 * MM_TILE)
    nk = k // tk
    dims = {"nn": ((1,), (0,)), "nt": ((1,), (1,)), "tn": ((0,), (0,))}[mode]
    a_spec = pl.BlockSpec((tk, tm), lambda i, j, s: (s, i)) if mode == "tn" else pl.BlockSpec((tm, tk), lambda i, j, s: (i, s))
    b_spec = pl.BlockSpec((tn, tk), lambda i, j, s: (j, s)) if mode == "nt" else pl.BlockSpec((tk, tn), lambda i, j, s: (s, j))
    o_spec = pl.BlockSpec((tm, tn), lambda i, j, s: (i, j))
    has_res = residual is not None

    def finish(out, r_ref, o_ref):
        if has_res:
            out = out + r_ref[...]
        o_ref[...] = out.astype(out_dtype)

    def body_one_pass(*refs):
        finish(_dot(refs[0][...], refs[1][...], dims), refs[2] if has_res else None, refs[-1])

    def body_reduce(*refs):
        a_ref, b_ref = refs[0], refs[1]
        r_ref = refs[2] if has_res else None
        o_ref, acc_ref = refs[-2], refs[-1]
        s = pl.program_id(2)

        @pl.when(s == 0)
        def _():
            acc_ref[...] = jnp.zeros_like(acc_ref)

        acc_ref[...] += _dot(a_ref[...], b_ref[...], dims)

        @pl.when(s == nk - 1)
        def _():
            finish(acc_ref[...], r_ref, o_ref)

    args, specs = [a, b], [a_spec, b_spec]
    if has_res:
        args.append(residual)
        specs.append(o_spec)
    return pl.pallas_call(
        body_one_pass if nk == 1 else body_reduce, name=name, grid=(m // tm, n // tn, nk), in_specs=specs, out_specs=o_spec,
        out_shape=jax.ShapeDtypeStruct((m, n), out_dtype),
        scratch_shapes=[] if nk == 1 else [pltpu.VMEM((tm, tn), F32)],
        compiler_params=_cparams("parallel", "parallel", "arbitrary"))(*args)


def _matmul_nt_sum(pairs, name, exchange=None):
    m, n = pairs[0][0].shape[0], pairs[0][1].shape[0]
    tm, tn = _tile(m, MM_TILE), _tile(n, MM_TILE)
    tks = [_tile(a.shape[1], MM_TILE) for a, _ in pairs]
    nks = [a.shape[1] // tk for (a, _), tk in zip(pairs, tks)]
    offs = [sum(nks[:i]) for i in range(len(pairs))]
    total = sum(nks)

    def body(*refs):
        o_ref, acc_ref = refs[-2], refs[-1]
        s = pl.program_id(2)

        @pl.when(s == 0)
        def _():
            acc_ref[...] = jnp.zeros_like(acc_ref)

        for i, (off, nk) in enumerate(zip(offs, nks)):
            @pl.when((s >= off) & (s < off + nk))
            def _(i=i):
                acc_ref[...] += _dot(refs[2 * i][...], refs[2 * i + 1][...], ((1,), (1,)))

        @pl.when(s == total - 1)
        def _():
            o_ref[...] = acc_ref[...]

    args, specs = [], []
    for (a, b), tk, off, nk in zip(pairs, tks, offs, nks):
        k_of = lambda s, off=off, nk=nk: jnp.clip(s - off, 0, nk - 1)
        args += [a, b]
        specs += [pl.BlockSpec((tm, tk), lambda i, j, s, k_of=k_of: (i, k_of(s))),
                  pl.BlockSpec((tn, tk), lambda i, j, s, k_of=k_of: (j, k_of(s)))]
    (out,), exchanged = _call_with_exchange(
        body, name=name, grid=(m // tm, n // tn, total), in_specs=specs,
        out_specs=[pl.BlockSpec((tm, tn), lambda i, j, s: (i, j))], out_shape=[jax.ShapeDtypeStruct((m, n), F32)],
        scratch_shapes=[pltpu.VMEM((tm, tn), F32)], args=args, exchange=exchange)
    return out, exchanged


def _rmsnorm_fwd(x, w, name):
    t = x.shape[0]
    blk = _tile(t, 512)

    def body(x_ref, w_ref, h_ref):
        xv = x_ref[...]
        h_ref[...] = (xv * lax.rsqrt(jnp.mean(xv * xv, axis=1, keepdims=True) + NORM_EPS) * w_ref[...]).astype(BF16)

    return pl.pallas_call(
        body, name=name, grid=(t // blk,),
        in_specs=[pl.BlockSpec((blk, D), lambda i: (i, 0)), pl.BlockSpec((1, D), lambda i: (0, 0))],
        out_specs=pl.BlockSpec((blk, D), lambda i: (i, 0)), out_shape=jax.ShapeDtypeStruct((t, D), BF16),
        compiler_params=_cparams("parallel"))(x, w)


def _rmsnorm_bwd(dh, x, w, dxo, name, exchange=None):
    t = x.shape[0]
    blk = _tile(t, 512)

    def body(dh_ref, x_ref, w_ref, dxo_ref, dx_ref, dw_ref):
        @pl.when(pl.program_id(0) == 0)
        def _():
            dw_ref[...] = jnp.zeros_like(dw_ref)

        xv, dhv = x_ref[...], dh_ref[...]
        rs = lax.rsqrt(jnp.mean(xv * xv, axis=1, keepdims=True) + NORM_EPS)
        xh = xv * rs
        dw_ref[...] += jnp.sum(dhv * xh, axis=0, keepdims=True)
        dxh = dhv * w_ref[...]
        dx_ref[...] = rs * (dxh - xh * jnp.mean(dxh * xh, axis=1, keepdims=True)) + dxo_ref[...]

    row = pl.BlockSpec((blk, D), lambda i: (i, 0))
    vec = pl.BlockSpec((1, D), lambda i: (0, 0))
    return _call_with_exchange(
        body, name=name, grid=(t // blk,), in_specs=[row, row, vec, row], out_specs=[row, vec],
        out_shape=[jax.ShapeDtypeStruct((t, D), F32), jax.ShapeDtypeStruct((1, D), F32)],
        scratch_shapes=[], args=(dh, x, w, dxo), exchange=exchange)


def _loss_head(x, w, target, name):
    t = x.shape[0]
    blk = _tile(t, 512)

    def body(x_ref, w_ref, t_ref, loss_ref, dx_ref, dw_ref):
        @pl.when(pl.program_id(0) == 0)
        def _():
            dw_ref[...] = jnp.zeros_like(dw_ref)
            loss_ref[...] = jnp.zeros_like(loss_ref)

        xv = x_ref[...]
        rs = lax.rsqrt(jnp.mean(xv * xv, axis=1, keepdims=True) + NORM_EPS)
        xh = xv * rs
        err = xh * w_ref[...] - t_ref[...]
        loss_ref[...] += 0.5 * jnp.sum(jnp.mean(err * err, axis=1, keepdims=True), axis=0, keepdims=True)
        dy = err * (1.0 / D)
        dw_ref[...] += jnp.sum(dy * xh, axis=0, keepdims=True)
        dxh = dy * w_ref[...]
        dx_ref[...] = rs * (dxh - xh * jnp.mean(dxh * xh, axis=1, keepdims=True))

    row = pl.BlockSpec((blk, D), lambda i: (i, 0))
    vec = pl.BlockSpec((1, D), lambda i: (0, 0))
    return pl.pallas_call(
        body, name=name, grid=(t // blk,), in_specs=[row, vec, row],
        out_specs=[pl.BlockSpec((1, 1), lambda i: (0, 0)), row, vec],
        out_shape=[jax.ShapeDtypeStruct((1, 1), F32), jax.ShapeDtypeStruct((t, D), F32), jax.ShapeDtypeStruct((1, D), F32)],
        compiler_params=_cparams("arbitrary"))(x, w, target)


def _lbs_of(lb):
    r = lax.broadcasted_iota(jnp.int32, lb.shape, 0)
    real = r < DEPTH
    mx = lax.stop_gradient(jnp.max(jnp.where(real, lb, -jnp.inf), axis=0, keepdims=True))
    e = jnp.where(real, jnp.exp(jnp.where(real, lb - mx, 0.0)), 0.0)
    p = e / jnp.sum(e, axis=0, keepdims=True)
    out = jnp.zeros_like(lb)
    run = jnp.zeros_like(mx)
    for l in range(1, DEPTH):
        run = run + jnp.sum(jnp.where(r == l, p, 0.0), axis=0, keepdims=True)
        out = out + jnp.where(r == l, run, 0.0)
    return out


def _lower_bounds_fwd(lbp, name):
    def body(lb_ref, o_ref):
        o_ref[...] = _lbs_of(lb_ref[...])

    return pl.pallas_call(body, name=name, out_shape=jax.ShapeDtypeStruct(lbp.shape, F32))(lbp)


def _lower_bounds_bwd(lbp, dlbs, name):
    def body(lb_ref, d_ref, o_ref):
        _, vjp = jax.vjp(_lbs_of, lb_ref[...])
        o_ref[...] = vjp(d_ref[...])[0]

    return pl.pallas_call(body, name=name, out_shape=jax.ShapeDtypeStruct(lbp.shape, F32))(lbp, dlbs)


def _branch_a_fwd(pa, cw, name):
    t = pa.shape[0]
    blk = _tile(t, 512)
    W = CONV_W

    def body(p_ref, w_ref, y_ref, hc_ref, hx_ref):
        @pl.when(pl.program_id(0) == 0)
        def _():
            hc_ref[...] = jnp.zeros_like(hc_ref)
            hx_ref[...] = jnp.zeros_like(hx_ref)

        ac, ax = p_ref[:, W:2 * W], p_ref[:, 2 * W:3 * W]
        y_ref[...] = _conv_a_block(
            p_ref[:, 0:W], jnp.concatenate([hc_ref[...], ac], axis=0), jnp.concatenate([hx_ref[...], ax], axis=0),
            p_ref[:, 3 * W:4 * W], w_ref[0:1, :], w_ref[1:2, :], w_ref[2:3, :]).astype(BF16)
        hc_ref[...] = p_ref[blk - 8:blk, W:2 * W]
        hx_ref[...] = p_ref[blk - 8:blk, 2 * W:3 * W]

    return pl.pallas_call(
        body, name=name, grid=(t // blk,),
        in_specs=[pl.BlockSpec((blk, NA), lambda i: (i, 0)), pl.BlockSpec((3, W), lambda i: (0, 0))],
        out_specs=pl.BlockSpec((blk, W), lambda i: (i, 0)), out_shape=jax.ShapeDtypeStruct((t, W), BF16),
        scratch_shapes=[pltpu.VMEM((8, W), F32), pltpu.VMEM((8, W), F32)],
        compiler_params=_cparams("arbitrary"))(pa, cw)


def _branch_a_bwd(pa, cw, dy, name):
    t = pa.shape[0]
    blk = _tile(t, 512)
    nt_ = t // blk
    W = CONV_W
    hb = blk // 8

    def body(p_ref, halo_ref, w_ref, dy_ref, dp_ref, dw_ref, chc_ref, chx_ref):
        i = pl.program_id(0)

        @pl.when(i == 0)
        def _():
            chc_ref[...] = jnp.zeros_like(chc_ref)
            chx_ref[...] = jnp.zeros_like(chx_ref)
            dw_ref[...] = jnp.zeros_like(dw_ref)

        keep = 1.0 - (i == nt_ - 1).astype(F32)
        hc = halo_ref[:, W:2 * W] * keep
        hx = halo_ref[:, 2 * W:3 * W] * keep
        ac_ext = jnp.concatenate([hc, p_ref[:, W:2 * W]], axis=0)
        ax_ext = jnp.concatenate([hx, p_ref[:, 2 * W:3 * W]], axis=0)
        _, vjp = jax.vjp(_conv_a_block, p_ref[:, 0:W], ac_ext, ax_ext, p_ref[:, 3 * W:4 * W],
                         w_ref[0:1, :], w_ref[1:2, :], w_ref[2:3, :])
        dab, dac, dax, daz, dw0, dw1, dw2 = vjp(dy_ref[...])
        dp_ref[:, 0:W] = dab.astype(BF16)
        dp_ref[:, W:2 * W] = _add_to_tail(dac[8:], chc_ref[...]).astype(BF16)
        dp_ref[:, 2 * W:3 * W] = _add_to_tail(dax[8:], chx_ref[...]).astype(BF16)
        dp_ref[:, 3 * W:4 * W] = daz.astype(BF16)
        chc_ref[...] = dac[:8] * keep
        chx_ref[...] = dax[:8] * keep
        dw_ref[0:1, :] += dw0
        dw_ref[1:2, :] += dw1
        dw_ref[2:3, :] += dw2

    rev = lambda i: (nt_ - 1 - i, 0)
    return pl.pallas_call(
        body, name=name, grid=(nt_,),
        in_specs=[pl.BlockSpec((blk, NA), rev),
                  pl.BlockSpec((8, NA), lambda i: (jnp.maximum((nt_ - 1 - i) * hb - 1, 0), 0)),
                  pl.BlockSpec((3, W), lambda i: (0, 0)),
                  pl.BlockSpec((blk, W), rev)],
        out_specs=[pl.BlockSpec((blk, NA), rev), pl.BlockSpec((3, W), lambda i: (0, 0))],
        out_shape=[jax.ShapeDtypeStruct((t, NA), BF16), jax.ShapeDtypeStruct((3, W), F32)],
        scratch_shapes=[pltpu.VMEM((8, W), F32), pltpu.VMEM((8, W), F32)],
        compiler_params=_cparams("arbitrary"))(pa, pa, cw, dy)


def _block_rows(t, chunks):
    return min(t, chunks * CHUNK)


def _branch_b_fwd(pb, lbs_row, nw, name, exchange=None):
    t = pb.shape[0]
    rows = _block_rows(t, HGRN_BLOCK_CHUNKS)
    nch = t // rows

    def body(p_ref, lb_ref, nw_ref, y_ref, s_ref, st_ref):
        @pl.when(pl.program_id(1) == 0)
        def _():
            st_ref[...] = jnp.zeros_like(st_ref)

        s_ref[0, 0] = st_ref[...]
        y, st1 = _hgrn_block(p_ref[:, 0:HD], p_ref[:, HD:2 * HD], p_ref[:, 2 * HD:3 * HD], p_ref[:, 3 * HD:4 * HD],
                             st_ref[...], lb_ref[...], nw_ref[...])
        y_ref[...] = y.astype(BF16)
        st_ref[...] = st1

    return _call_with_exchange(
        body, name=name, grid=(HGRN_HEADS, nch),
        in_specs=[pl.BlockSpec((rows, 4 * HD), lambda h, i: (i, h)),
                  pl.BlockSpec((1, HD), lambda h, i: (0, h)),
                  pl.BlockSpec((1, HD), lambda h, i: (0, 0))],
        out_specs=[pl.BlockSpec((rows, HD), lambda h, i: (i, h)),
                   pl.BlockSpec((1, 1, HD, HD), lambda h, i: (h, i, 0, 0))],
        out_shape=[jax.ShapeDtypeStruct((t, HGRN_HEADS * HD), BF16),
                   jax.ShapeDtypeStruct((HGRN_HEADS, nch, HD, HD), F32)],
        scratch_shapes=[pltpu.VMEM((HD, HD), F32)],
        args=(pb, lbs_row, nw), exchange=exchange)


def _branch_b_bwd(pb, states, lbs_row, nw, dy, name, exchange=None):
    t = pb.shape[0]
    rows = _block_rows(t, HGRN_BLOCK_CHUNKS)
    nch = t // rows

    def body(p_ref, s_ref, lb_ref, nw_ref, dy_ref, dp_ref, dlb_ref, dnw_ref, ds_ref):
        h, i = pl.program_id(0), pl.program_id(1)

        @pl.when(i == 0)
        def _():
            ds_ref[...] = jnp.zeros_like(ds_ref)
            dlb_ref[...] = jnp.zeros_like(dlb_ref)

        @pl.when((i == 0) & (h == 0))
        def _():
            dnw_ref[...] = jnp.zeros_like(dnw_ref)

        _, vjp = jax.vjp(_hgrn_block, p_ref[:, 0:HD], p_ref[:, HD:2 * HD], p_ref[:, 2 * HD:3 * HD],
                         p_ref[:, 3 * HD:4 * HD], s_ref[0, 0], lb_ref[...], nw_ref[...])
        dq, df, di, dz, ds0, dlb, dnw = vjp((dy_ref[...], ds_ref[...]))
        dp_ref[:, 0:HD] = dq.astype(BF16)
        dp_ref[:, HD:2 * HD] = df.astype(BF16)
        dp_ref[:, 2 * HD:3 * HD] = di.astype(BF16)
        dp_ref[:, 3 * HD:4 * HD] = dz.astype(BF16)
        ds_ref[...] = ds0
        dlb_ref[...] += dlb
        dnw_ref[...] += dnw

    rev = lambda h, i: (nch - 1 - i, h)
    return _call_with_exchange(
        body, name=name, grid=(HGRN_HEADS, nch),
        in_specs=[pl.BlockSpec((rows, 4 * HD), rev),
                  pl.BlockSpec((1, 1, HD, HD), lambda h, i: (h, nch - 1 - i, 0, 0)),
                  pl.BlockSpec((1, HD), lambda h, i: (0, h)),
                  pl.BlockSpec((1, HD), lambda h, i: (0, 0)),
                  pl.BlockSpec((rows, HD), rev)],
        out_specs=[pl.BlockSpec((rows, 4 * HD), rev),
                   pl.BlockSpec((1, HD), lambda h, i: (0, h)),
                   pl.BlockSpec((1, HD), lambda h, i: (0, 0))],
        out_shape=[jax.ShapeDtypeStruct((t, NB), BF16), jax.ShapeDtypeStruct((1, HGRN_HEADS * HD), F32),
                   jax.ShapeDtypeStruct((1, HD), F32)],
        scratch_shapes=[pltpu.VMEM((HD, HD), F32)],
        args=(pb, states, lbs_row, nw, dy), exchange=exchange)


def _branch_c_fwd(pc, cw, cpar, nw, name, exchange=None):
    t = pc.shape[0]
    rows = _block_rows(t, GDN_BLOCK_CHUNKS)
    nch = t // rows
    XW = 4 * HD

    nsys = 2 * rows // GROUP

    def body(p_ref, w_ref, cp_ref, nw_ref, y_ref, s_ref, x_ref, sa_ref, sb_ref, halo_ref):
        @pl.when(pl.program_id(1) == 0)
        def _():
            sa_ref[...] = jnp.zeros_like(sa_ref)
            sb_ref[...] = jnp.zeros_like(sb_ref)
            halo_ref[...] = jnp.zeros_like(halo_ref)

        s_ref[0, 0, 0] = sa_ref[...]
        s_ref[0, 0, 1] = sb_ref[...]
        x_ext = jnp.concatenate([halo_ref[...], p_ref[:, 0:XW]], axis=0)
        (y, s1a, s1b), xs = _gdn_block(x_ext, p_ref[:, XW:XW + 2 * HD], p_ref[:, XW + 2 * HD:XW + 3 * HD],
                                       sa_ref[...], sb_ref[...], w_ref[0:1, :], w_ref[1:2, :], w_ref[2:3, :], w_ref[3:4, :],
                                       cp_ref[0, 0:1, :], cp_ref[0, 1:2, :], nw_ref[...])
        for n in range(nsys):
            x_ref[0, 0, n] = xs[n]
        y_ref[...] = y.astype(BF16)
        sa_ref[...] = s1a
        sb_ref[...] = s1b
        halo_ref[...] = p_ref[rows - 8:rows, 0:XW]

    return _call_with_exchange(
        body, name=name, grid=(GDN_QK_HEADS, nch),
        in_specs=[pl.BlockSpec((rows, C_HEAD), lambda h, i: (i, h)),
                  pl.BlockSpec((4, XW), lambda h, i: (0, h)),
                  pl.BlockSpec((1, 8, HD), lambda h, i: (h, 0, 0)),
                  pl.BlockSpec((1, HD), lambda h, i: (0, 0))],
        out_specs=[pl.BlockSpec((rows, 2 * HD), lambda h, i: (i, h)),
                   pl.BlockSpec((1, 1, 2, HD, HD), lambda h, i: (h, i, 0, 0, 0)),
                   pl.BlockSpec((1, 1, nsys, GROUP, GROUP), lambda h, i: (h, i, 0, 0, 0))],
        out_shape=[jax.ShapeDtypeStruct((t, 2 * GDN_QK_HEADS * HD), BF16),
                   jax.ShapeDtypeStruct((GDN_QK_HEADS, nch, 2, HD, HD), F32),
                   jax.ShapeDtypeStruct((GDN_QK_HEADS, nch, nsys, GROUP, GROUP), BF16)],
        scratch_shapes=[pltpu.VMEM((HD, HD), F32), pltpu.VMEM((HD, HD), F32), pltpu.VMEM((8, XW), F32)],
        args=(pc, cw, cpar, nw), exchange=exchange)


def _branch_c_bwd(pc, states, inverses, cw, cpar, nw, dy, name, exchange=None):
    t = pc.shape[0]
    rows = _block_rows(t, GDN_BLOCK_CHUNKS)
    nch = t // rows
    XW = 4 * HD
    hb = rows // 8

    nsys = 2 * rows // GROUP

    def body(p_ref, halo_ref, s_ref, x_ref, w_ref, cp_ref, nw_ref, dy_ref, dp_ref, dw_ref, dcp_ref, dnw_ref,
             dsa_ref, dsb_ref, carry_ref):
        h, i = pl.program_id(0), pl.program_id(1)

        @pl.when(i == 0)
        def _():
            dsa_ref[...] = jnp.zeros_like(dsa_ref)
            dsb_ref[...] = jnp.zeros_like(dsb_ref)
            carry_ref[...] = jnp.zeros_like(carry_ref)
            dw_ref[...] = jnp.zeros_like(dw_ref)
            dcp_ref[...] = jnp.zeros_like(dcp_ref)

        @pl.when((i == 0) & (h == 0))
        def _():
            dnw_ref[...] = jnp.zeros_like(dnw_ref)

        keep = 1.0 - (i == nch - 1).astype(F32)
        x_ext = jnp.concatenate([halo_ref[:, 0:XW] * keep, p_ref[:, 0:XW]], axis=0)
        block = functools.partial(_gdn_block, known=tuple(x_ref[0, 0, n] for n in range(nsys)))
        _, vjp, _ = jax.vjp(block, x_ext, p_ref[:, XW:XW + 2 * HD], p_ref[:, XW + 2 * HD:XW + 3 * HD],
                            s_ref[0, 0, 0], s_ref[0, 0, 1], w_ref[0:1, :], w_ref[1:2, :], w_ref[2:3, :], w_ref[3:4, :],
                            cp_ref[0, 0:1, :], cp_ref[0, 1:2, :], nw_ref[...], has_aux=True)
        dx, dz, dba, dsa, dsb, dw0, dw1, dw2, dw3, dal, ddt, dnw = vjp((dy_ref[...], dsa_ref[...], dsb_ref[...]))
        dp_ref[:, 0:XW] = _add_to_tail(dx[8:], carry_ref[...]).astype(BF16)
        dp_ref[:, XW:XW + 2 * HD] = dz.astype(BF16)
        dp_ref[:, XW + 2 * HD:XW + 3 * HD] = dba.astype(BF16)
        carry_ref[...] = dx[:8] * keep
        dsa_ref[...] = dsa
        dsb_ref[...] = dsb
        dw_ref[0:1, :] += dw0
        dw_ref[1:2, :] += dw1
        dw_ref[2:3, :] += dw2
        dw_ref[3:4, :] += dw3
        dcp_ref[0, 0:1, :] += dal
        dcp_ref[0, 1:2, :] += ddt
        dnw_ref[...] += dnw

    rev = lambda h, i: (nch - 1 - i, h)
    return _call_with_exchange(
        body, name=name, grid=(GDN_QK_HEADS, nch),
        in_specs=[pl.BlockSpec((rows, C_HEAD), rev),
                  pl.BlockSpec((8, C_HEAD), lambda h, i: (jnp.maximum((nch - 1 - i) * hb - 1, 0), h)),
                  pl.BlockSpec((1, 1, 2, HD, HD), lambda h, i: (h, nch - 1 - i, 0, 0, 0)),
                  pl.BlockSpec((1, 1, nsys, GROUP, GROUP), lambda h, i: (h, nch - 1 - i, 0, 0, 0)),
                  pl.BlockSpec((4, XW), lambda h, i: (0, h)),
                  pl.BlockSpec((1, 8, HD), lambda h, i: (h, 0, 0)),
                  pl.BlockSpec((1, HD), lambda h, i: (0, 0)),
                  pl.BlockSpec((rows, 2 * HD), rev)],
        out_specs=[pl.BlockSpec((rows, C_HEAD), rev),
                   pl.BlockSpec((4, XW), lambda h, i: (0, h)),
                   pl.BlockSpec((1, 8, HD), lambda h, i: (h, 0, 0)),
                   pl.BlockSpec((1, HD), lambda h, i: (0, 0))],
        out_shape=[jax.ShapeDtypeStruct((t, NC_COLS), BF16), jax.ShapeDtypeStruct((4, GDN_QK_HEADS * XW), F32),
                   jax.ShapeDtypeStruct((GDN_QK_HEADS, 8, HD), F32), jax.ShapeDtypeStruct((1, HD), F32)],
        scratch_shapes=[pltpu.VMEM((HD, HD), F32), pltpu.VMEM((HD, HD), F32), pltpu.VMEM((8, XW), F32)],
        args=(pc, pc, states, inverses, cw, cpar, nw, dy), exchange=exchange)


def _merge_fwd(pg, bg, ya, yb, yc, name):
    t = pg.shape[0]
    blk = _tile(t, 256)

    def body(g_ref, b_ref, a_ref, b2_ref, c_ref, o_ref):
        gate = _sigmoid(g_ref[...] + b_ref[...])
        o_ref[...] = (gate[:, 0:D] * a_ref[...] + gate[:, D:2 * D] * b2_ref[...] + gate[:, 2 * D:3 * D] * c_ref[...]).astype(BF16)

    row = pl.BlockSpec((blk, D), lambda i: (i, 0))
    return pl.pallas_call(
        body, name=name, grid=(t // blk,),
        in_specs=[pl.BlockSpec((blk, NG), lambda i: (i, 0)), pl.BlockSpec((1, NG), lambda i: (0, 0)), row, row, row],
        out_specs=row, out_shape=jax.ShapeDtypeStruct((t, D), BF16),
        compiler_params=_cparams("parallel"))(pg, bg, ya, yb, yc)


def _merge_bwd(dm, pg, bg, ya, yb, yc, name):
    t = pg.shape[0]
    blk = _tile(t, 256)

    def body(dm_ref, g_ref, b_ref, a_ref, b2_ref, c_ref, dg_ref, da_ref, db_ref, dc_ref, dbg_ref):
        @pl.when(pl.program_id(0) == 0)
        def _():
            dbg_ref[...] = jnp.zeros_like(dbg_ref)

        gate = _sigmoid(g_ref[...] + b_ref[...])
        dmv = dm_ref[...].astype(F32)
        for j, (y_ref, dy_ref) in enumerate(((a_ref, da_ref), (b2_ref, db_ref), (c_ref, dc_ref))):
            gj = gate[:, j * D:(j + 1) * D]
            dy_ref[...] = (dmv * gj).astype(BF16)
            dgj = dmv * y_ref[...] * gj * (1.0 - gj)
            dg_ref[:, j * D:(j + 1) * D] = dgj.astype(BF16)
            dbg_ref[:, j * D:(j + 1) * D] += jnp.sum(dgj, axis=0, keepdims=True)

    row = pl.BlockSpec((blk, D), lambda i: (i, 0))
    wide = pl.BlockSpec((blk, NG), lambda i: (i, 0))
    vec = pl.BlockSpec((1, NG), lambda i: (0, 0))
    return pl.pallas_call(
        body, name=name, grid=(t // blk,), in_specs=[row, wide, vec, row, row, row],
        out_specs=[wide, row, row, row, vec],
        out_shape=[jax.ShapeDtypeStruct((t, NG), BF16)] + [jax.ShapeDtypeStruct((t, D), BF16)] * 3
                  + [jax.ShapeDtypeStruct((1, NG), F32)],
        compiler_params=_cparams("arbitrary"))(dm, pg, bg, ya, yb, yc)


def _adamw_math(w, g, m, v):
    m = ADAM_B1 * m + (1.0 - ADAM_B1) * g
    v = ADAM_B2 * v + (1.0 - ADAM_B2) * (g * g)
    m_hat = m / (1.0 - ADAM_B1 ** ADAM_STEP)
    v_hat = v / (1.0 - ADAM_B2 ** ADAM_STEP)
    delta = -ADAM_LR * (m_hat / (jnp.sqrt(v_hat) + ADAM_EPS) + ADAM_WD * w)
    return delta, m, v


def _sum_adamw(parts, w, m, v, name):
    layers = len(parts)
    r, c = parts[0].shape[1:]
    br = r if r <= 256 else 256
    nb = r // br
    assert r % br == 0 and w.shape == (layers * r, c)

    def body(*refs):
        w_ref, m_ref, v_ref, g_ref, d_ref, nm_ref, nv_ref = refs[layers:]
        for l in range(layers):
            @pl.when(pl.program_id(0) == l)
            def _(p_ref=refs[l]):
                g = p_ref[0].astype(F32)
                for k in range(1, N_DEV):
                    g = g + p_ref[k].astype(F32)
                g_ref[...] = g
                d_ref[...], nm_ref[...], nv_ref[...] = _adamw_math(w_ref[...], g, m_ref[...], v_ref[...])

    blk = pl.BlockSpec((br, c), lambda l, i: (l * nb + i, 0))
    part_specs = [pl.BlockSpec((N_DEV, br, c), lambda l, i, q=q: (0, jnp.where(l == q, i, jnp.where(l < q, 0, nb - 1)), 0))
                  for q in range(layers)]
    return pl.pallas_call(
        body, name=name, grid=(layers, nb), in_specs=part_specs + [blk, blk, blk], out_specs=[blk] * 4,
        out_shape=[jax.ShapeDtypeStruct((layers * r, c), F32)] * 4,
        compiler_params=_cparams("arbitrary", "arbitrary"))(*parts, w, m, v)


def _adamw(g, w, m, v, name):
    def body(g_ref, w_ref, m_ref, v_ref, d_ref, nm_ref, nv_ref):
        d_ref[...], nm_ref[...], nv_ref[...] = _adamw_math(w_ref[...], g_ref[...], m_ref[...], v_ref[...])

    return pl.pallas_call(body, name=name, out_shape=[jax.ShapeDtypeStruct(w.shape, F32)] * 3)(g, w, m, v)


def _sum_slots(parts, name):
    def body(p_ref, o_ref):
        g = p_ref[0]
        for k in range(1, N_DEV):
            g = g + p_ref[k]
        o_ref[...] = g

    return pl.pallas_call(body, name=name, out_shape=jax.ShapeDtypeStruct(parts.shape[1:], F32))(parts)


def _exchange(srcs, name, broadcast):
    n = len(srcs)

    def body(*refs):
        copies = _exchange_copies(refs[:n], refs[n:2 * n], *refs[2 * n:], broadcast)
        for cp in copies:
            cp.start()
        for cp in copies:
            cp.wait()

    return pl.pallas_call(
        body, name=name, in_specs=[HBM_SPEC] * n, out_specs=[HBM_SPEC] * n, out_shape=_exchange_shapes(srcs, broadcast),
        scratch_shapes=_exchange_semaphores(n))(*srcs)


def _gather_two_level(srcs, name):
    n = len(srcs)

    def body(*refs):
        src_refs, dst_refs = refs[:n], refs[n:2 * n]
        send_sems, recv_sems, local_sems = refs[2 * n:]
        x, y, c = lax.axis_index("x"), lax.axis_index("y"), lax.axis_index("c")
        index_of = lambda px, py, pc: 4 * px + 2 * py + pc
        me, other_core = index_of(x, y, c), (x, y, 1 - c)
        chips = [(1 - x, y), (x, 1 - y), (1 - x, 1 - y)]

        def copy(k, a, block, to, src=None):
            return pltpu.make_async_remote_copy(
                src_ref=dst_refs[a].at[block] if src is None else src, dst_ref=dst_refs[a].at[block],
                send_sem=send_sems.at[k, a], recv_sem=recv_sems.at[k, a], device_id=to, device_id_type=MESH)

        local = [pltpu.make_async_copy(src_refs[a], dst_refs[a].at[me], local_sems.at[a]) for a in range(n)]
        first = [copy(0, a, me, other_core, src=src_refs[a]) for a in range(n)]
        first += [copy(1 + j, a, me, (*chip, c), src=src_refs[a]) for j, chip in enumerate(chips) for a in range(n)]
        for cp in local + first:
            cp.start()
        passed = []
        for j, chip in enumerate(chips):
            block = index_of(*chip, c)
            for a in range(n):
                copy(1 + j, a, block, (x, y, c)).wait_recv()
            for a in range(n):
                passed.append(copy(4 + j, a, block, other_core))
                passed[-1].start()
        for a in range(n):
            copy(0, a, index_of(x, y, 1 - c), (x, y, c)).wait_recv()
        for j, chip in enumerate(chips):
            for a in range(n):
                copy(4 + j, a, index_of(*chip, 1 - c), (x, y, c)).wait_recv()
        for cp in first + passed:
            cp.wait_send()
        for cp in local:
            cp.wait()

    return pl.pallas_call(
        body, name=name, in_specs=[HBM_SPEC] * n, out_specs=[HBM_SPEC] * n, out_shape=_exchange_shapes(srcs, True),
        scratch_shapes=_exchange_semaphores(n))(*srcs)


HBM_SPEC = pl.BlockSpec(memory_space=pltpu.HBM)


def _exchange_shapes(srcs, broadcast):
    return [jax.ShapeDtypeStruct((N_DEV,) + (s.shape if broadcast else s.shape[1:]), s.dtype) for s in srcs]


def _exchange_semaphores(n):
    return [pltpu.SemaphoreType.DMA((N_DEV - 1, n)), pltpu.SemaphoreType.DMA((N_DEV - 1, n)), pltpu.SemaphoreType.DMA((n,))]


def _exchange_copies(src_refs, dst_refs, send_sems, recv_sems, local_sems, broadcast):
    x, y, c = lax.axis_index("x"), lax.axis_index("y"), lax.axis_index("c")
    me = 4 * x + 2 * y + c
    copies = []
    for k in range(1, N_DEV):
        px = 1 - x if (k >> 2) & 1 else x
        py = 1 - y if (k >> 1) & 1 else y
        pc = 1 - c if k & 1 else c
        peer = 4 * px + 2 * py + pc
        for a, (src, dst) in enumerate(zip(src_refs, dst_refs)):
            copies.append(pltpu.make_async_remote_copy(
                src_ref=src if broadcast else src.at[peer], dst_ref=dst.at[me],
                send_sem=send_sems.at[k - 1, a], recv_sem=recv_sems.at[k - 1, a],
                device_id=(px, py, pc), device_id_type=MESH))
    for a, (src, dst) in enumerate(zip(src_refs, dst_refs)):
        copies.append(pltpu.make_async_copy(src if broadcast else src.at[me], dst.at[me], local_sems.at[a]))
    return copies


def _call_with_exchange(body, *, name, grid, in_specs, out_specs, out_shape, scratch_shapes, args, exchange):
    if exchange is None:
        outs = pl.pallas_call(body, name=name, grid=grid, in_specs=in_specs, out_specs=out_specs, out_shape=out_shape,
                              scratch_shapes=scratch_shapes,
                              compiler_params=_cparams(*["arbitrary"] * len(grid)))(*args)
        return outs, None
    srcs, broadcast = exchange
    n, n_in, n_out, n_scr = len(srcs), len(args), len(out_shape), len(scratch_shapes)
    steps = 1
    for g in grid:
        steps *= g

    def hosted(*refs):
        ins, src_refs = refs[:n_in], refs[n_in:n_in + n]
        outs, dst_refs = refs[n_in + n:n_in + n + n_out], refs[n_in + n + n_out:n_in + 2 * n + n_out]
        scratch = refs[n_in + 2 * n + n_out:]
        step = pl.program_id(0)
        for axis in range(1, len(grid)):
            step = step * grid[axis] + pl.program_id(axis)

        @pl.when(step == 0)
        def _():
            for cp in _exchange_copies(src_refs, dst_refs, *scratch[n_scr:], broadcast):
                cp.start()

        body(*ins, *outs, *scratch[:n_scr])

        @pl.when(step == steps - 1)
        def _():
            for cp in _exchange_copies(src_refs, dst_refs, *scratch[n_scr:], broadcast):
                cp.wait()

    outs = pl.pallas_call(
        hosted, name=name, grid=grid, in_specs=list(in_specs) + [HBM_SPEC] * n, out_specs=list(out_specs) + [HBM_SPEC] * n,
        out_shape=list(out_shape) + _exchange_shapes(srcs, broadcast),
        scratch_shapes=list(scratch_shapes) + _exchange_semaphores(n),
        compiler_params=_cparams(*["arbitrary"] * len(grid)))(*args, *srcs)
    return outs[:n_out], outs[n_out:]


def _regroup_w_in(w):
    wa = w[:, OFF_A:OFF_A + NA]
    seg = lambda off, h, n=HD: w[:, off + h * n: off + (h + 1) * n]
    wb = jnp.concatenate([seg(OFF_B + s * 512, h) for h in range(HGRN_HEADS) for s in range(4)], axis=1)
    parts = []
    for h in range(GDN_QK_HEADS):
        small = jnp.concatenate(
            [w[:, OFF_BETA + 2 * h: OFF_BETA + 2 * h + 2], w[:, OFF_CA + 2 * h: OFF_CA + 2 * h + 2],
             jnp.zeros((w.shape[0], HD - 4), w.dtype)], axis=1)
        parts += [seg(OFF_CQ, h), seg(OFF_CK, h), seg(OFF_CV, h, 2 * HD), seg(OFF_CZ, h, 2 * HD), small]
    wc = jnp.concatenate(parts, axis=1)
    wg = w[:, OFF_G:OFF_G + NG]
    return wa, wb, wc, wg


def _ungroup_dw_in(da, db, dc, dg):
    bq = [jnp.concatenate([db[:, h * 512 + s * HD: h * 512 + (s + 1) * HD] for h in range(HGRN_HEADS)], axis=1)
          for s in range(4)]
    ch = lambda h, lo, hi: dc[:, h * C_HEAD + lo: h * C_HEAD + hi]
    heads = range(GDN_QK_HEADS)
    cq = jnp.concatenate([ch(h, 0, HD) for h in heads], axis=1)
    ck = jnp.concatenate([ch(h, HD, 2 * HD) for h in heads], axis=1)
    cv = jnp.concatenate([ch(h, 2 * HD, 4 * HD) for h in heads], axis=1)
    cz = jnp.concatenate([ch(h, 4 * HD, 6 * HD) for h in heads], axis=1)
    cbeta = jnp.concatenate([ch(h, 6 * HD, 6 * HD + 2) for h in heads], axis=1)
    ca = jnp.concatenate([ch(h, 6 * HD + 2, 6 * HD + 4) for h in heads], axis=1)
    return jnp.concatenate([da] + bq + [cq, ck, cv, cbeta, ca, cz, dg], axis=1)


def _regroup_conv_c(cw):
    parts = []
    for h in range(GDN_QK_HEADS):
        parts += [cw[:, h * HD:(h + 1) * HD], cw[:, 512 + h * HD: 512 + (h + 1) * HD],
                  cw[:, 1024 + 2 * h * HD: 1024 + (2 * h + 2) * HD]]
    return jnp.concatenate(parts, axis=1)


def _ungroup_conv_c(d):
    heads = range(GDN_QK_HEADS)
    q = jnp.concatenate([d[:, h * 512: h * 512 + HD] for h in heads], axis=1)
    k = jnp.concatenate([d[:, h * 512 + HD: h * 512 + 2 * HD] for h in heads], axis=1)
    v = jnp.concatenate([d[:, h * 512 + 2 * HD: h * 512 + 4 * HD] for h in heads], axis=1)
    return jnp.concatenate([q, k, v], axis=1)


def _numel(shape):
    n = 1
    for d in shape:
        n *= d
    return n


def _pack(arrays, rows):
    flat = jnp.concatenate([a.reshape(-1) for a in arrays])
    return jnp.pad(flat, (0, rows * 128 - flat.shape[0])).reshape(rows, 128)


def _unpack(packed, shapes):
    flat = packed.reshape(-1)
    out, off = [], 0
    for s in shapes:
        out.append(flat[off:off + _numel(s)].reshape(s))
        off += _numel(s)
    return out


def _rows_for(shapes):
    return -(-sum(_numel(s) for s in shapes) // 1024) * 8


def kernel(x, norm_w, w_in, b_gate, conv_a, conv_c, a_log, dt_bias, lower_bounds, hgrn_norm_w, gdn_norm_w, w_out_a, w_out_b, w_out_c, w_o, final_norm_w, loss_target, m_norm_w, m_w_in, m_b_gate, m_conv_a, m_conv_c, m_a_log, m_dt_bias, m_lower_bounds, m_hgrn_norm_w, m_gdn_norm_w, m_w_out_a, m_w_out_b, m_w_out_c, m_w_o, m_final_norm_w, v_norm_w, v_w_in, v_b_gate, v_conv_a, v_conv_c, v_a_log, v_dt_bias, v_lower_bounds, v_hgrn_norm_w, v_gdn_norm_w, v_w_out_a, v_w_out_b, v_w_out_c, v_w_o, v_final_norm_w):
    me = 4 * lax.axis_index("x") + 2 * lax.axis_index("y") + lax.axis_index("c")
    xs = x[0]
    target = loss_target[0]
    in_shard = w_in.shape[2]

    big = [w_in, w_out_a, w_out_b, w_out_c, w_o]
    shards_of = lambda l: [w[l].astype(BF16) for w in big]
    conv_shapes = [(DEPTH, 3, CONV_W), (DEPTH, 4, 2048)]
    conv_rows = _rows_for(conv_shapes)
    ca_full = lax.dynamic_update_slice(jnp.zeros(conv_shapes[0], F32), conv_a, (0, 0, me * conv_a.shape[2]))
    cc_full = lax.dynamic_update_slice(jnp.zeros(conv_shapes[1], F32), conv_c, (0, 0, me * conv_c.shape[2]))
    g_in0, conv_parts = _gather_two_level([w_in[0].astype(BF16), _pack([ca_full, cc_full], conv_rows)], "gather_l0")
    conv_a_full, conv_c_full = _unpack(_sum_slots(conv_parts, "sum_conv"), conv_shapes)

    lb_pad = jnp.pad(lower_bounds, ((0, 8 - DEPTH), (0, 0)))
    lbs = _lower_bounds_fwd(lb_pad, "lower_bounds_fwd")

    def input_weights(l, g_in):
        wa, wb, wc, wg = _regroup_w_in(jnp.concatenate([g_in[q] for q in range(N_DEV)], axis=1))
        lanes = lambda vec: jnp.pad(vec.reshape(GDN_QK_HEADS, 1, 2), ((0, 0), (0, 0), (0, HD - 2)))
        cpar = jnp.concatenate([lanes(a_log[l]), lanes(dt_bias[l]), jnp.zeros((GDN_QK_HEADS, 6, HD), F32)], axis=1)
        return dict(
            wa=wa, wb=wb, wc=wc, wg=wg, cpar=cpar,
            nw=norm_w[l:l + 1], bg=b_gate[l:l + 1], cwa=conv_a_full[l], cwc=_regroup_conv_c(conv_c_full[l]),
            lb=lbs[l:l + 1], hnw=hgrn_norm_w[l:l + 1], gnw=gdn_norm_w[l:l + 1])

    def output_weights(g_oa, g_ob, g_oc, g_o):
        return dict(woa=jnp.concatenate([g_oa[q] for q in range(N_DEV)], axis=1),
                    wob=jnp.concatenate([g_ob[q] for q in range(N_DEV)], axis=1), woc=g_oc.reshape(D, D), wo=g_o.reshape(D, D))

    layers = [input_weights(0, g_in0)]

    saved = []
    cur = xs
    for l in range(DEPTH):
        L = layers[l]
        n = f"l{l}_"
        h = _rmsnorm_fwd(cur, L["nw"], n + "rms")
        pa = _matmul(h, L["wa"], "nn", n + "proj_a")
        pb = _matmul(h, L["wb"], "nn", n + "proj_b")
        pc = _matmul(h, L["wc"], "nn", n + "proj_c")
        pg = _matmul(h, L["wg"], "nn", n + "proj_g", out_dtype=BF16)
        ua = _branch_a_fwd(pa, L["cwa"], n + "conv_fwd")
        carry = (shards_of(l)[1:], True) if l == 0 else None
        (ub, sb), gathered = _branch_b_fwd(pb, L["lb"], L["hnw"], n + "hgrn_fwd", exchange=carry)
        if carry is not None:
            L.update(output_weights(*gathered))
        carry = (shards_of(l + 1), True) if l + 1 < DEPTH else None
        (uc, sc, xc), gathered = _branch_c_fwd(pc, L["cwc"], L["cpar"], L["gnw"], n + "gdn_fwd", exchange=carry)
        if carry is not None:
            layers.append(dict(input_weights(l + 1, gathered[0]), **output_weights(*gathered[1:])))
        ya = _matmul(ua, L["woa"], "nn", n + "out_a", out_dtype=BF16)
        yb = _matmul(ub, L["wob"], "nn", n + "out_b", out_dtype=BF16)
        yc = _matmul(uc, L["woc"], "nn", n + "out_c", out_dtype=BF16)
        merged = _merge_fwd(pg, L["bg"], ya, yb, yc, n + "merge")
        nxt = _matmul(merged, L["wo"], "nn", n + "out_o", residual=cur)
        saved.append(dict(x=cur, h=h, pa=pa, pb=pb, pc=pc, pg=pg, ua=ua, ub=ub, uc=uc, sb=sb, sc=sc, xc=xc,
                          ya=ya, yb=yb, yc=yc, merged=merged))
        cur = nxt

    loss_part, dx, d_final = _loss_head(cur, final_norm_w.reshape(1, D), target, "loss_head")

    def outgoing(g):
        cols = lambda a, n: jnp.stack([a[:, p * n:(p + 1) * n] for p in range(N_DEV)]).astype(BF16)
        rows = lambda a: a.reshape(N_DEV, a.shape[0] // N_DEV, a.shape[1]).astype(BF16)
        first = [cols(g["w_in"], in_shard)] if "w_in" in g else [None]
        if "w_o" not in g:
            return first
        return first + [cols(g["w_out_a"], 128), cols(g["w_out_b"], 128), rows(g["w_out_c"]), rows(g["w_o"])]

    grads = [None] * DEPTH
    dlbs_rows = [None] * DEPTH
    incoming = [None] * DEPTH
    for l in reversed(range(DEPTH)):
        L, S = layers[l], saved[l]
        n = f"l{l}_"
        dmerged = _matmul(dx, L["wo"], "nt", n + "d_merged", out_dtype=BF16)
        d_wo = _matmul(S["merged"], dx, "tn", n + "dw_o", out_dtype=BF16)
        dpg, dya, dyb, dyc, d_bg = _merge_bwd(dmerged, S["pg"], L["bg"], S["ya"], S["yb"], S["yc"], n + "merge_bwd")
        dua = _matmul(dya, L["woa"], "nt", n + "d_ua")
        dub = _matmul(dyb, L["wob"], "nt", n + "d_ub")
        duc = _matmul(dyc, L["woc"], "nt", n + "d_uc")
        d_woa = _matmul(S["ua"], dya, "tn", n + "dw_out_a", out_dtype=BF16)
        d_wob = _matmul(S["ub"], dyb, "tn", n + "dw_out_b", out_dtype=BF16)
        d_woc = _matmul(S["uc"], dyc, "tn", n + "dw_out_c", out_dtype=BF16)
        dpa, d_cwa = _branch_a_bwd(S["pa"], L["cwa"], dua, n + "conv_bwd")
        out_grads = dict(w_out_a=d_woa, w_out_b=d_wob, w_out_c=d_woc, w_o=d_wo)
        carry = (outgoing(out_grads)[1:], False) if l == 0 else None
        (dpb, d_lb, d_hnw), arrived_out = _branch_b_bwd(S["pb"], S["sb"], L["lb"], L["hnw"], dub, n + "hgrn_bwd", exchange=carry)
        carry = (outgoing(grads[l + 1]), False) if l + 1 < DEPTH else None
        (dpc, d_cwc, d_cpar, d_gnw), arrived = _branch_c_bwd(S["pc"], S["sc"], S["xc"], L["cwc"], L["cpar"], L["gnw"], duc,
                                                             n + "gdn_bwd", exchange=carry)
        if carry is not None:
            incoming[l + 1] = arrived
        d_win = _ungroup_dw_in(*[_matmul(S["h"], dp, "tn", f"{n}dw_{piece}", out_dtype=BF16)
                                 for dp, piece in ((dpa, "a"), (dpb, "b"), (dpc, "c"), (dpg, "g"))])
        carry = (outgoing(dict(w_in=d_win)), False) if l == 0 else None
        dh, arrived_in = _matmul_nt_sum([(dpa, L["wa"]), (dpb, L["wb"]), (dpc, L["wc"]), (dpg, L["wg"])], n + "dh",
                                        exchange=carry)
        (dx, d_nw), _ = _rmsnorm_bwd(dh, S["x"], L["nw"], dx, n + "rms_bwd")
        dlbs_rows[l] = d_lb
        grads[l] = dict(w_in=d_win, w_out_a=d_woa, w_out_b=d_wob, w_out_c=d_woc, w_o=d_wo, norm_w=d_nw[0],
                        b_gate=d_bg[0], conv_a=d_cwa, conv_c=_ungroup_conv_c(d_cwc),
                        a_log=d_cpar[:, 0, 0:2].reshape(-1), dt_bias=d_cpar[:, 1, 0:2].reshape(-1),
                        hgrn_norm_w=d_hnw[0], gdn_norm_w=d_gnw[0])
    grad_x = dx[None]
    d_lower = _lower_bounds_bwd(lb_pad, jnp.pad(jnp.concatenate(dlbs_rows, axis=0), ((0, 8 - DEPTH), (0, 0))),
                                "lower_bounds_bwd")[:DEPTH]

    incoming[0] = list(arrived_in) + list(arrived_out)
    stack = lambda name: jnp.stack([grads[l][name] for l in range(DEPTH)])
    big_out = {}
    for j, (name, w, m, v) in enumerate((("w_in", w_in, m_w_in, v_w_in), ("w_out_a", w_out_a, m_w_out_a, v_w_out_a),
                                         ("w_out_b", w_out_b, m_w_out_b, v_w_out_b), ("w_out_c", w_out_c, m_w_out_c, v_w_out_c),
                                         ("w_o", w_o, m_w_o, v_w_o))):
        parts = [incoming[l][j] for l in range(DEPTH)]
        r2 = lambda a: a.reshape(DEPTH * parts[0].shape[1], parts[0].shape[2])
        outs = _sum_adamw(parts, r2(w), r2(m), r2(v), "adamw_" + name)
        big_out[name] = [o.reshape(w.shape) for o in outs]

    small_names = ["norm_w", "b_gate", "conv_a", "conv_c", "a_log", "dt_bias", "lower_bounds", "hgrn_norm_w",
                   "gdn_norm_w", "final_norm_w", "loss"]
    small_vals = {k: stack(k) for k in ("norm_w", "b_gate", "conv_a", "conv_c", "a_log", "dt_bias", "hgrn_norm_w", "gdn_norm_w")}
    small_vals.update(lower_bounds=d_lower, final_norm_w=d_final[0], loss=loss_part.reshape(1))
    small_shapes = [small_vals[k].shape for k in small_names]
    small_rows = _rows_for(small_shapes)
    small_parts, = _exchange([_pack([small_vals[k] for k in small_names], small_rows)], "exchange_small", broadcast=True)
    total = dict(zip(small_names, _unpack(_sum_slots(small_parts, "sum_small"), small_shapes)))
    loss = total["loss"][0]
    g_conv_a = lax.dynamic_slice(total["conv_a"], (0, 0, me * conv_a.shape[2]), conv_a.shape)
    g_conv_c = lax.dynamic_slice(total["conv_c"], (0, 0, me * conv_c.shape[2]), conv_c.shape)

    small_w = dict(norm_w=(norm_w, m_norm_w, v_norm_w), b_gate=(b_gate, m_b_gate, v_b_gate),
                   conv_a=(conv_a, m_conv_a, v_conv_a), conv_c=(conv_c, m_conv_c, v_conv_c),
                   a_log=(a_log, m_a_log, v_a_log), dt_bias=(dt_bias, m_dt_bias, v_dt_bias),
                   lower_bounds=(lower_bounds, m_lower_bounds, v_lower_bounds),
                   hgrn_norm_w=(hgrn_norm_w, m_hgrn_norm_w, v_hgrn_norm_w), gdn_norm_w=(gdn_norm_w, m_gdn_norm_w, v_gdn_norm_w),
                   final_norm_w=(final_norm_w, m_final_norm_w, v_final_norm_w))
    small_g = dict(total, conv_a=g_conv_a, conv_c=g_conv_c)
    upd_names = small_names[:-1]
    upd_shapes = [small_w[k][0].shape for k in upd_names]
    upd_rows = _rows_for(upd_shapes)
    pk = lambda j: _pack([small_w[k][j] for k in upd_names], upd_rows)
    s_delta, s_m, s_v = _adamw(_pack([small_g[k] for k in upd_names], upd_rows), pk(0), pk(1), pk(2), "adamw_small")
    small_out = {k: [small_g[k], d, mm, vv] for k, d, mm, vv in
                 zip(upd_names, _unpack(s_delta, upd_shapes), _unpack(s_m, upd_shapes), _unpack(s_v, upd_shapes))}

    order = ["norm_w", "w_in", "b_gate", "conv_a", "conv_c", "a_log", "dt_bias", "lower_bounds", "hgrn_norm_w",
             "gdn_norm_w", "w_out_a", "w_out_b", "w_out_c", "w_o", "final_norm_w"]
    res = {**small_out, **big_out}
    outs = [loss, grad_x]
    for j in range(4):
        outs += [res[k][j] for k in order]
    return tuple(outs)
```

```python
import functools

import jax
import jax.numpy as jnp
from jax import lax
from jax.experimental import pallas as pl
from jax.experimental.pallas import tpu as pltpu

F32 = jnp.float32
BF16 = jnp.bfloat16
MESH = pl.DeviceIdType.MESH

N_DEV = 8
D = 1024
DEPTH = 2
CHUNK = 64
HGRN_BLOCK_CHUNKS = 16
GDN_BLOCK_CHUNKS = 8
GROUP = 128
NORM_EPS = 1e-6
L2_EPS = 1e-6
MIN_F = 1e-30
HD = 128
HGRN_HEADS = 4
GDN_QK_HEADS = 4
CONV_W = 512
IN_COLS = 10256
OFF_A, OFF_B, OFF_CQ, OFF_CK, OFF_CV, OFF_BETA, OFF_CA, OFF_CZ, OFF_G = (
    0, 2048, 4096, 4608, 5120, 6144, 6152, 6160, 7184)
NA, NB, NC_COLS, NG = 2048, 2048, 3584, 3072
C_HEAD = 896

ADAM_LR, ADAM_B1, ADAM_B2, ADAM_EPS, ADAM_WD, ADAM_STEP = 0.001, 0.9, 0.999, 1e-08, 0.01, 10

VMEM_LIMIT = 56 * 1024 * 1024
MM_TILE = 1024


def _cparams(*sem):
    return pltpu.CompilerParams(dimension_semantics=sem, vmem_limit_bytes=VMEM_LIMIT)


def _tile(dim, cap):
    if dim <= cap:
        return dim
    t = (cap // 128) * 128
    while dim % t:
        t -= 128
    return t


def _sigmoid(x):
    return 1.0 / (1.0 + jnp.exp(-x))


def _silu(x):
    return x * _sigmoid(x)


def _softplus(x):
    return jnp.maximum(x, 0.0) + jnp.log(1.0 + jnp.exp(-jnp.abs(x)))


def _dot(a, b, dims, precision=None):
    if precision is None:
        a, b = a.astype(BF16), b.astype(BF16)
    return lax.dot_general(a, b, (dims, ((), ())), precision=precision, preferred_element_type=F32)


def _nn(a, b, precision=None):
    return _dot(a, b, ((1,), (0,)), precision)


def _nt(a, b, precision=None):
    return _dot(a, b, ((1,), (1,)), precision)


def _tn(a, b, precision=None):
    return _dot(a, b, ((0,), (0,)), precision)


def _sum_rows_split(mat01, x):
    m = mat01.astype(BF16)
    hi = x.astype(BF16)
    low = (x - hi.astype(F32)).astype(BF16)
    return _nn(m, hi) + _nn(m, low)


@functools.partial(jax.custom_vjp, nondiff_argnums=(1,))
def _shift_rows(x, d):
    return x if d == 0 else pltpu.roll(x, d, 0)


def _shift_rows_fwd(x, d):
    return _shift_rows(x, d), None


def _shift_rows_bwd(d, _, ct):
    return ((ct if d == 0 else pltpu.roll(ct, ct.shape[0] - d, 0)),)


_shift_rows.defvjp(_shift_rows_fwd, _shift_rows_bwd)


def _iota2(shape):
    return lax.broadcasted_iota(jnp.int32, shape, 0), lax.broadcasted_iota(jnp.int32, shape, 1)


def _lane_pick(x, i):
    lane = lax.broadcasted_iota(jnp.int32, x.shape, 1)
    return jnp.sum(jnp.where(lane == i, x, 0.0), axis=1, keepdims=True)


def _hgrn_block(qr, fr, ir, zr, st0, lb, nw):
    rows = qr.shape[0]
    r, c = _iota2((CHUNK, CHUNK))
    halves = [1 << j for j in range(CHUNK.bit_length() - 1)]
    mats = [c <= r, c > r]
    pairs = []
    for hb in halves:
        same = (r // hb) == (c // hb)
        if hb > 1:
            mats += [(c <= r) & same, (c > r) & same]
        pairs.append(((r // (2 * hb)) == (c // (2 * hb))) & ((r // hb) == (c // hb) + 1))
    stack = jnp.concatenate([m.astype(F32) for m in mats], axis=0)

    q = _silu(qr) * (HD ** -0.5)
    fg = lb + (1.0 - lb) * _sigmoid(fr)
    logf = jnp.log(jnp.maximum(fg, MIN_F))
    kk = 1.0 - fg
    v = ir

    chunks = [slice(s, s + CHUNK) for s in range(0, rows, CHUNK)]
    cums = [_sum_rows_split(stack, logf[sl]) for sl in chunks]
    part = lambda i: jnp.concatenate([cs[i * CHUNK:(i + 1) * CHUNK] for cs in cums], axis=0)
    qg = q * jnp.exp(part(0))
    ks = kk * jnp.exp(part(1))
    q_lv = [q * jnp.exp(logf)] + [q * jnp.exp(part(2 * j)) for j in range(1, len(halves))]
    k_lv = [kk] + [kk * jnp.exp(part(2 * j + 1)) for j in range(1, len(halves))]
    st = st0
    outs = []
    for sl in chunks:
        scores = jnp.where(pairs[0], _nt(q_lv[0][sl], k_lv[0][sl]), 0.0)
        for j in range(1, len(halves)):
            scores += jnp.where(pairs[j], _nt(q_lv[j][sl], k_lv[j][sl]), 0.0)
        outs.append(_nn(scores, v[sl]) + _nt(qg[sl], st))
        st = st * jnp.exp(jnp.sum(logf[sl], axis=0, keepdims=True)) + _tn(v[sl], ks[sl])
    o = jnp.concatenate(outs, axis=0) + jnp.sum(q * kk, axis=1, keepdims=True) * v
    y = o * lax.rsqrt(jnp.mean(o * o, axis=1, keepdims=True) + NORM_EPS) * nw * _silu(zr)
    return y, st


def _unit_lower_inverses(ms):
    r, c = _iota2(ms[0].shape)
    xs = [jnp.where(r == c, 1.0, 0.0) - jnp.where((r // 2) == (c // 2), m, 0.0) for m in ms]
    b = 2
    while b < CHUNK:
        pick = ((r // (2 * b)) == (c // (2 * b))) & ((r // b) != (c // b))
        ts = [_nn(x, jnp.where(pick, m, 0.0)) for x, m in zip(xs, ms)]
        xs = [x - _nn(t, x) for x, t in zip(xs, ts)]
        b *= 2
    return tuple(x.astype(BF16) for x in xs)


@jax.custom_vjp
def _known_inverses(ms, xs):
    return xs


def _known_inverses_fwd(ms, xs):
    return xs, xs


def _known_inverses_bwd(xs, cts):
    r, c = _iota2(xs[0].shape)
    keep = (c < r) & ((r // CHUNK) == (c // CHUNK))
    ts = [_tn(x, ct) for x, ct in zip(xs, cts)]
    return (tuple(jnp.where(keep, -_nt(t, x), 0.0) for t, x in zip(ts, xs)), tuple(jnp.zeros_like(x) for x in xs))


_known_inverses.defvjp(_known_inverses_fwd, _known_inverses_bwd)


def _chunk_cumsum(x):
    row = lax.broadcasted_iota(jnp.int32, x.shape, 0) % CHUNK
    d = 1
    while d < CHUNK:
        x = x + jnp.where(row >= d, _shift_rows(x, d), 0.0)
        d *= 2
    return x


def _gdn_block(x_ext, z, ba, s0a, s0b, w0, w1, w2, w3, alog, dtb, nw, known=None):
    rows = z.shape[0]
    conv = (w0 * _shift_rows(x_ext, 3) + w1 * _shift_rows(x_ext, 2) + w2 * _shift_rows(x_ext, 1) + w3 * x_ext)
    cc = _silu(conv[8:])
    qc, kc = cc[:, 0:HD], cc[:, HD:2 * HD]
    q = qc * lax.rsqrt(jnp.sum(qc * qc, axis=1, keepdims=True) + L2_EPS) * (HD ** -0.5)
    k = kc * lax.rsqrt(jnp.sum(kc * kc, axis=1, keepdims=True) + L2_EPS)

    r, c = _iota2((GROUP, GROUP))
    same = (r // CHUNK) == (c // CHUNK)
    causal, strict, eye = same & (c <= r), same & (c < r), r == c
    heads = (0, 1)
    groups = [slice(lo, lo + GROUP) for lo in range(0, rows, GROUP)]
    chunks = [slice(lo, lo + CHUNK) for lo in range(0, rows, CHUNK)]

    v, loga, g_w, kb, kg, qg = [], [], [], [], [], []
    for i in heads:
        v.append(cc[:, (2 + i) * HD:(3 + i) * HD])
        beta = _sigmoid(_lane_pick(ba, i))
        a_neg = -jnp.exp(_lane_pick(alog, i))
        loga.append(a_neg * _softplus(_lane_pick(ba, 2 + i) + _lane_pick(dtb, i)))
        g_w.append(_chunk_cumsum(jnp.broadcast_to(loga[i], (rows, HD))))
        kb.append(k * beta)
        kg.append(k * jnp.exp(g_w[i]))
        qg.append(q * jnp.exp(g_w[i]))

    systems = [(i, gs) for gs in groups for i in heads]
    dec_c, ms = [], []
    for i, gs in systems:
        g_sq = g_w[i][gs]
        g_row = jnp.sum(jnp.where(eye, g_sq, 0.0), axis=0, keepdims=True)
        diff = g_sq - g_row
        dec_c.append(jnp.where(causal, jnp.exp(jnp.where(causal, diff, 0.0)), 0.0))
        ms.append(jnp.where(strict, _nt(k[gs], kb[i][gs]) * dec_c[-1], 0.0))
    xs = _unit_lower_inverses(tuple(ms)) if known is None else _known_inverses(tuple(ms), known)
    u = [[None] * len(groups) for _ in heads]
    w = [[None] * len(groups) for _ in heads]
    qk = [[None] * len(groups) for _ in heads]
    for n, (i, gs) in enumerate(systems):
        j = n // len(heads)
        u[i][j] = _nn(xs[n], v[i][gs])
        w[i][j] = _nn(xs[n], kg[i][gs])
        qk[i][j] = _nt(q[gs], kb[i][gs]) * dec_c[n]
    u = [jnp.concatenate(p, axis=0) for p in u]
    w = [jnp.concatenate(p, axis=0) for p in w]

    decay, p_mat, q_mat = {}, {}, {}
    for n, sl in enumerate(chunks):
        for i in heads:
            g_last = jnp.sum(loga[i][sl], axis=0, keepdims=True)
            kd = kb[i][sl] * jnp.exp(g_last - g_w[i][sl])
            decay[n, i] = jnp.exp(g_last)
            p_mat[n, i] = -_tn(kd, w[i][sl])
            q_mat[n, i] = _tn(kd, u[i][sl])
    s = [s0a, s0b]
    s_at = {}
    for n in range(len(chunks)):
        for i in heads:
            s_at[n, i] = s[i]
            s[i] = s[i] * decay[n, i] + _nn(p_mat[n, i], s[i]) + q_mat[n, i]

    ys = []
    for i in heads:
        e = jnp.concatenate([u[i][sl] - _nn(w[i][sl], s_at[n, i]) for n, sl in enumerate(chunks)], axis=0)
        o_state = jnp.concatenate([_nn(qg[i][sl], s_at[n, i]) for n, sl in enumerate(chunks)], axis=0)
        o = o_state + jnp.concatenate([_nn(qk[i][j], e[gs]) for j, gs in enumerate(groups)], axis=0)
        zi = z[:, i * HD:(i + 1) * HD]
        ys.append(o * lax.rsqrt(jnp.mean(o * o, axis=1, keepdims=True) + NORM_EPS) * nw * _silu(zi))
    return (jnp.concatenate(ys, axis=1), s[0], s[1]), xs


def _add_to_tail(x, tail):
    return x + jnp.concatenate([jnp.zeros((x.shape[0] - 8, x.shape[1]), x.dtype), tail], axis=0)


def _conv_a_block(ab, ac_ext, ax_ext, az, w0, w1, w2):
    u = ac_ext * ax_ext
    conv = (w0 * _shift_rows(u, 2) + w1 * _shift_rows(u, 1) + w2 * u)[8:]
    return ab * conv * _silu(az)


def _matmul(a, b, mode, name, residual=None, out_dtype=F32):
    if mode == "nn":
        (m, k), n = a.shape, b.shape[1]
    elif mode == "nt":
        (m, k), n = a.shape, b.shape[0]
    else:
        (k, m), n = a.shape, b.shape[1]
    tm, tn, tk = _tile(m, MM_TILE), _tile(n, MM_TILE), _tile(k, MM_TILE)
    if mode == "tn":
        tk = _tile(k, 2 * MM_TILE)
    elif k == tk:
        tm = _tile(m, 2 * MM_TILE)
    nk = k // tk
    dims = {"nn": ((1,), (0,)), "nt": ((1,), (1,)), "tn": ((0,), (0,))}[mode]
    a_spec = pl.BlockSpec((tk, tm), lambda i, j, s: (s, i)) if mode == "tn" else pl.BlockSpec((tm, tk), lambda i, j, s: (i, s))
    b_spec = pl.BlockSpec((tn, tk), lambda i, j, s: (j, s)) if mode == "nt" else pl.BlockSpec((tk, tn), lambda i, j, s: (s, j))
    o_spec = pl.BlockSpec((tm, tn), lambda i, j, s: (i, j))
    has_res = residual is not None

    def finish(out, r_ref, o_ref):
        if has_res:
            out = out + r_ref[...]
        o_ref[...] = out.astype(out_dtype)

    def body_one_pass(*refs):
        finish(_dot(refs[0][...], refs[1][...], dims), refs[2] if has_res else None, refs[-1])

    def body_reduce(*refs):
        a_ref, b_ref = refs[0], refs[1]
        r_ref = refs[2] if has_res else None
        o_ref, acc_ref = refs[-2], refs[-1]
        s = pl.program_id(2)

        @pl.when(s == 0)
        def _():
            acc_ref[...] = jnp.zeros_like(acc_ref)

        acc_ref[...] += _dot(a_ref[...], b_ref[...], dims)

        @pl.when(s == nk - 1)
        def _():
            finish(acc_ref[...], r_ref, o_ref)

    args, specs = [a, b], [a_spec, b_spec]
    if has_res:
        args.append(residual)
        specs.append(o_spec)
    return pl.pallas_call(
        body_one_pass if nk == 1 else body_reduce, name=name, grid=(m // tm, n // tn, nk), in_specs=specs, out_specs=o_spec,
        out_shape=jax.ShapeDtypeStruct((m, n), out_dtype),
        scratch_shapes=[] if nk == 1 else [pltpu.VMEM((tm, tn), F32)],
        compiler_params=_cparams("parallel", "parallel", "arbitrary"))(*args)


def _matmul_nt_sum(pairs, name, exchange=None):
    m, n = pairs[0][0].shape[0], pairs[0][1].shape[0]
    tm, tn = _tile(m, MM_TILE), _tile(n, MM_TILE)
    tks = [_tile(a.shape[1], MM_TILE) for a, _ in pairs]
    nks = [a.shape[1] // tk for (a, _), tk in zip(pairs, tks)]
    offs = [sum(nks[:i]) for i in range(len(pairs))]
    total = sum(nks)

    def body(*refs):
        o_ref, acc_ref = refs[-2], refs[-1]
        s = pl.program_id(2)

        @pl.when(s == 0)
        def _():
            acc_ref[...] = jnp.zeros_like(acc_ref)

        for i, (off, nk) in enumerate(zip(offs, nks)):
            @pl.when((s >= off) & (s < off + nk))
            def _(i=i):
                acc_ref[...] += _dot(refs[2 * i][...], refs[2 * i + 1][...], ((1,), (1,)))

        @pl.when(s == total - 1)
        def _():
            o_ref[...] = acc_ref[...]

    args, specs = [], []
    for (a, b), tk, off, nk in zip(pairs, tks, offs, nks):
        k_of = lambda s, off=off, nk=nk: jnp.clip(s - off, 0, nk - 1)
        args += [a, b]
        specs += [pl.BlockSpec((tm, tk), lambda i, j, s, k_of=k_of: (i, k_of(s))),
                  pl.BlockSpec((tn, tk), lambda i, j, s, k_of=k_of: (j, k_of(s)))]
    (out,), exchanged = _call_with_exchange(
        body, name=name, grid=(m // tm, n // tn, total), in_specs=specs,
        out_specs=[pl.BlockSpec((tm, tn), lambda i, j, s: (i, j))], out_shape=[jax.ShapeDtypeStruct((m, n), F32)],
        scratch_shapes=[pltpu.VMEM((tm, tn), F32)], args=args, exchange=exchange)
    return out, exchanged


def _rmsnorm_fwd(x, w, name):
    t = x.shape[0]
    blk = _tile(t, 1024)

    def body(x_ref, w_ref, h_ref):
        xv = x_ref[...]
        h_ref[...] = (xv * lax.rsqrt(jnp.mean(xv * xv, axis=1, keepdims=True) + NORM_EPS) * w_ref[...]).astype(BF16)

    return pl.pallas_call(
        body, name=name, grid=(t // blk,),
        in_specs=[pl.BlockSpec((blk, D), lambda i: (i, 0)), pl.BlockSpec((1, D), lambda i: (0, 0))],
        out_specs=pl.BlockSpec((blk, D), lambda i: (i, 0)), out_shape=jax.ShapeDtypeStruct((t, D), BF16),
        compiler_params=_cparams("parallel"))(x, w)


def _rmsnorm_bwd(dh, x, w, dxo, name, exchange=None):
    t = x.shape[0]
    blk = _tile(t, 512)

    def body(dh_ref, x_ref, w_ref, dxo_ref, dx_ref, dw_ref):
        @pl.when(pl.program_id(0) == 0)
        def _():
            dw_ref[...] = jnp.zeros_like(dw_ref)

        xv, dhv = x_ref[...], dh_ref[...]
        rs = lax.rsqrt(jnp.mean(xv * xv, axis=1, keepdims=True) + NORM_EPS)
        xh = xv * rs
        dw_ref[...] += jnp.sum(dhv * xh, axis=0, keepdims=True)
        dxh = dhv * w_ref[...]
        dx_ref[...] = rs * (dxh - xh * jnp.mean(dxh * xh, axis=1, keepdims=True)) + dxo_ref[...]

    row = pl.BlockSpec((blk, D), lambda i: (i, 0))
    vec = pl.BlockSpec((1, D), lambda i: (0, 0))
    return _call_with_exchange(
        body, name=name, grid=(t // blk,), in_specs=[row, row, vec, row], out_specs=[row, vec],
        out_shape=[jax.ShapeDtypeStruct((t, D), F32), jax.ShapeDtypeStruct((1, D), F32)],
        scratch_shapes=[], args=(dh, x, w, dxo), exchange=exchange)


def _loss_head(x, w, target, name):
    t = x.shape[0]
    blk = _tile(t, 512)

    def body(x_ref, w_ref, t_ref, loss_ref, dx_ref, dw_ref):
        @pl.when(pl.program_id(0) == 0)
        def _():
            dw_ref[...] = jnp.zeros_like(dw_ref)
            loss_ref[...] = jnp.zeros_like(loss_ref)

        xv = x_ref[...]
        rs = lax.rsqrt(jnp.mean(xv * xv, axis=1, keepdims=True) + NORM_EPS)
        xh = xv * rs
        err = xh * w_ref[...] - t_ref[...]
        loss_ref[...] += 0.5 * jnp.sum(jnp.mean(err * err, axis=1, keepdims=True), axis=0, keepdims=True)
        dy = err * (1.0 / D)
        dw_ref[...] += jnp.sum(dy * xh, axis=0, keepdims=True)
        dxh = dy * w_ref[...]
        dx_ref[...] = rs * (dxh - xh * jnp.mean(dxh * xh, axis=1, keepdims=True))

    row = pl.BlockSpec((blk, D), lambda i: (i, 0))
    vec = pl.BlockSpec((1, D), lambda i: (0, 0))
    return pl.pallas_call(
        body, name=name, grid=(t // blk,), in_specs=[row, vec, row],
        out_specs=[pl.BlockSpec((1, 1), lambda i: (0, 0)), row, vec],
        out_shape=[jax.ShapeDtypeStruct((1, 1), F32), jax.ShapeDtypeStruct((t, D), F32), jax.ShapeDtypeStruct((1, D), F32)],
        compiler_params=_cparams("arbitrary"))(x, w, target)


def _lbs_of(lb):
    r = lax.broadcasted_iota(jnp.int32, lb.shape, 0)
    real = r < DEPTH
    mx = lax.stop_gradient(jnp.max(jnp.where(real, lb, -jnp.inf), axis=0, keepdims=True))
    e = jnp.where(real, jnp.exp(jnp.where(real, lb - mx, 0.0)), 0.0)
    p = e / jnp.sum(e, axis=0, keepdims=True)
    out = jnp.zeros_like(lb)
    run = jnp.zeros_like(mx)
    for l in range(1, DEPTH):
        run = run + jnp.sum(jnp.where(r == l, p, 0.0), axis=0, keepdims=True)
        out = out + jnp.where(r == l, run, 0.0)
    return out


def _lower_bounds_fwd(lbp, name):
    def body(lb_ref, o_ref):
        o_ref[...] = _lbs_of(lb_ref[...])

    return pl.pallas_call(body, name=name, out_shape=jax.ShapeDtypeStruct(lbp.shape, F32))(lbp)


def _lower_bounds_bwd(lbp, dlbs, name):
    def body(lb_ref, d_ref, o_ref):
        _, vjp = jax.vjp(_lbs_of, lb_ref[...])
        o_ref[...] = vjp(d_ref[...])[0]

    return pl.pallas_call(body, name=name, out_shape=jax.ShapeDtypeStruct(lbp.shape, F32))(lbp, dlbs)


def _branch_a_fwd(pa, cw, name):
    t = pa.shape[0]
    blk = _tile(t, 512)
    W = CONV_W

    def body(p_ref, w_ref, y_ref, hc_ref, hx_ref):
        @pl.when(pl.program_id(0) == 0)
        def _():
            hc_ref[...] = jnp.zeros_like(hc_ref)
            hx_ref[...] = jnp.zeros_like(hx_ref)

        ac, ax = p_ref[:, W:2 * W], p_ref[:, 2 * W:3 * W]
        y_ref[...] = _conv_a_block(
            p_ref[:, 0:W], jnp.concatenate([hc_ref[...], ac], axis=0), jnp.concatenate([hx_ref[...], ax], axis=0),
            p_ref[:, 3 * W:4 * W], w_ref[0:1, :], w_ref[1:2, :], w_ref[2:3, :]).astype(BF16)
        hc_ref[...] = p_ref[blk - 8:blk, W:2 * W]
        hx_ref[...] = p_ref[blk - 8:blk, 2 * W:3 * W]

    return pl.pallas_call(
        body, name=name, grid=(t // blk,),
        in_specs=[pl.BlockSpec((blk, NA), lambda i: (i, 0)), pl.BlockSpec((3, W), lambda i: (0, 0))],
        out_specs=pl.BlockSpec((blk, W), lambda i: (i, 0)), out_shape=jax.ShapeDtypeStruct((t, W), BF16),
        scratch_shapes=[pltpu.VMEM((8, W), F32), pltpu.VMEM((8, W), F32)],
        compiler_params=_cparams("arbitrary"))(pa, cw)


def _branch_a_bwd(pa, cw, dy, name):
    t = pa.shape[0]
    blk = _tile(t, 512)
    nt_ = t // blk
    W = CONV_W
    hb = blk // 8

    def body(p_ref, halo_ref, w_ref, dy_ref, dp_ref, dw_ref, chc_ref, chx_ref):
        i = pl.program_id(0)

        @pl.when(i == 0)
        def _():
            chc_ref[...] = jnp.zeros_like(chc_ref)
            chx_ref[...] = jnp.zeros_like(chx_ref)
            dw_ref[...] = jnp.zeros_like(dw_ref)

        keep = 1.0 - (i == nt_ - 1).astype(F32)
        hc = halo_ref[:, W:2 * W] * keep
        hx = halo_ref[:, 2 * W:3 * W] * keep
        ac_ext = jnp.concatenate([hc, p_ref[:, W:2 * W]], axis=0)
        ax_ext = jnp.concatenate([hx, p_ref[:, 2 * W:3 * W]], axis=0)
        _, vjp = jax.vjp(_conv_a_block, p_ref[:, 0:W], ac_ext, ax_ext, p_ref[:, 3 * W:4 * W],
                         w_ref[0:1, :], w_ref[1:2, :], w_ref[2:3, :])
        dab, dac, dax, daz, dw0, dw1, dw2 = vjp(dy_ref[...])
        dp_ref[:, 0:W] = dab.astype(BF16)
        dp_ref[:, W:2 * W] = _add_to_tail(dac[8:], chc_ref[...]).astype(BF16)
        dp_ref[:, 2 * W:3 * W] = _add_to_tail(dax[8:], chx_ref[...]).astype(BF16)
        dp_ref[:, 3 * W:4 * W] = daz.astype(BF16)
        chc_ref[...] = dac[:8] * keep
        chx_ref[...] = dax[:8] * keep
        dw_ref[0:1, :] += dw0
        dw_ref[1:2, :] += dw1
        dw_ref[2:3, :] += dw2

    rev = lambda i: (nt_ - 1 - i, 0)
    return pl.pallas_call(
        body, name=name, grid=(nt_,),
        in_specs=[pl.BlockSpec((blk, NA), rev),
                  pl.BlockSpec((8, NA), lambda i: (jnp.maximum((nt_ - 1 - i) * hb - 1, 0), 0)),
                  pl.BlockSpec((3, W), lambda i: (0, 0)),
                  pl.BlockSpec((blk, W), rev)],
        out_specs=[pl.BlockSpec((blk, NA), rev), pl.BlockSpec((3, W), lambda i: (0, 0))],
        out_shape=[jax.ShapeDtypeStruct((t, NA), BF16), jax.ShapeDtypeStruct((3, W), F32)],
        scratch_shapes=[pltpu.VMEM((8, W), F32), pltpu.VMEM((8, W), F32)],
        compiler_params=_cparams("arbitrary"))(pa, pa, cw, dy)


def _block_rows(t, chunks):
    return min(t, chunks * CHUNK)


def _branch_b_fwd(pb, lbs_row, nw, name, exchange=None):
    t = pb.shape[0]
    rows = _block_rows(t, HGRN_BLOCK_CHUNKS)
    nch = t // rows

    def body(p_ref, lb_ref, nw_ref, y_ref, s_ref, st_ref):
        @pl.when(pl.program_id(1) == 0)
        def _():
            st_ref[...] = jnp.zeros_like(st_ref)

        s_ref[0, 0] = st_ref[...]
        y, st1 = _hgrn_block(p_ref[:, 0:HD], p_ref[:, HD:2 * HD], p_ref[:, 2 * HD:3 * HD], p_ref[:, 3 * HD:4 * HD],
                             st_ref[...], lb_ref[...], nw_ref[...])
        y_ref[...] = y.astype(BF16)
        st_ref[...] = st1

    return _call_with_exchange(
        body, name=name, grid=(HGRN_HEADS, nch),
        in_specs=[pl.BlockSpec((rows, 4 * HD), lambda h, i: (i, h)),
                  pl.BlockSpec((1, HD), lambda h, i: (0, h)),
                  pl.BlockSpec((1, HD), lambda h, i: (0, 0))],
        out_specs=[pl.BlockSpec((rows, HD), lambda h, i: (i, h)),
                   pl.BlockSpec((1, 1, HD, HD), lambda h, i: (h, i, 0, 0))],
        out_shape=[jax.ShapeDtypeStruct((t, HGRN_HEADS * HD), BF16),
                   jax.ShapeDtypeStruct((HGRN_HEADS, nch, HD, HD), F32)],
        scratch_shapes=[pltpu.VMEM((HD, HD), F32)],
        args=(pb, lbs_row, nw), exchange=exchange)


def _branch_b_bwd(pb, states, lbs_row, nw, dy, name, exchange=None):
    t = pb.shape[0]
    rows = _block_rows(t, HGRN_BLOCK_CHUNKS)
    nch = t // rows

    def body(p_ref, s_ref, lb_ref, nw_ref, dy_ref, dp_ref, dlb_ref, dnw_ref, ds_ref):
        h, i = pl.program_id(0), pl.program_id(1)

        @pl.when(i == 0)
        def _():
            ds_ref[...] = jnp.zeros_like(ds_ref)
            dlb_ref[...] = jnp.zeros_like(dlb_ref)

        @pl.when((i == 0) & (h == 0))
        def _():
            dnw_ref[...] = jnp.zeros_like(dnw_ref)

        _, vjp = jax.vjp(_hgrn_block, p_ref[:, 0:HD], p_ref[:, HD:2 * HD], p_ref[:, 2 * HD:3 * HD],
                         p_ref[:, 3 * HD:4 * HD], s_ref[0, 0], lb_ref[...], nw_ref[...])
        dq, df, di, dz, ds0, dlb, dnw = vjp((dy_ref[...], ds_ref[...]))
        dp_ref[:, 0:HD] = dq.astype(BF16)
        dp_ref[:, HD:2 * HD] = df.astype(BF16)
        dp_ref[:, 2 * HD:3 * HD] = di.astype(BF16)
        dp_ref[:, 3 * HD:4 * HD] = dz.astype(BF16)
        ds_ref[...] = ds0
        dlb_ref[...] += dlb
        dnw_ref[...] += dnw

    rev = lambda h, i: (nch - 1 - i, h)
    return _call_with_exchange(
        body, name=name, grid=(HGRN_HEADS, nch),
        in_specs=[pl.BlockSpec((rows, 4 * HD), rev),
                  pl.BlockSpec((1, 1, HD, HD), lambda h, i: (h, nch - 1 - i, 0, 0)),
                  pl.BlockSpec((1, HD), lambda h, i: (0, h)),
                  pl.BlockSpec((1, HD), lambda h, i: (0, 0)),
                  pl.BlockSpec((rows, HD), rev)],
        out_specs=[pl.BlockSpec((rows, 4 * HD), rev),
                   pl.BlockSpec((1, HD), lambda h, i: (0, h)),
                   pl.BlockSpec((1, HD), lambda h, i: (0, 0))],
        out_shape=[jax.ShapeDtypeStruct((t, NB), BF16), jax.ShapeDtypeStruct((1, HGRN_HEADS * HD), F32),
                   jax.ShapeDtypeStruct((1, HD), F32)],
        scratch_shapes=[pltpu.VMEM((HD, HD), F32)],
        args=(pb, states, lbs_row, nw, dy), exchange=exchange)


def _branch_c_fwd(pc, cw, cpar, nw, name, exchange=None):
    t = pc.shape[0]
    rows = _block_rows(t, GDN_BLOCK_CHUNKS)
    nch = t // rows
    XW = 4 * HD

    nsys = 2 * rows // GROUP

    def body(p_ref, w_ref, cp_ref, nw_ref, y_ref, s_ref, x_ref, sa_ref, sb_ref, halo_ref):
        @pl.when(pl.program_id(1) == 0)
        def _():
            sa_ref[...] = jnp.zeros_like(sa_ref)
            sb_ref[...] = jnp.zeros_like(sb_ref)
            halo_ref[...] = jnp.zeros_like(halo_ref)

        s_ref[0, 0, 0] = sa_ref[...]
        s_ref[0, 0, 1] = sb_ref[...]
        x_ext = jnp.concatenate([halo_ref[...], p_ref[:, 0:XW]], axis=0)
        (y, s1a, s1b), xs = _gdn_block(x_ext, p_ref[:, XW:XW + 2 * HD], p_ref[:, XW + 2 * HD:XW + 3 * HD],
                                       sa_ref[...], sb_ref[...], w_ref[0:1, :], w_ref[1:2, :], w_ref[2:3, :], w_ref[3:4, :],
                                       cp_ref[0, 0:1, :], cp_ref[0, 1:2, :], nw_ref[...])
        for n in range(nsys):
            x_ref[0, 0, n] = xs[n]
        y_ref[...] = y.astype(BF16)
        sa_ref[...] = s1a
        sb_ref[...] = s1b
        halo_ref[...] = p_ref[rows - 8:rows, 0:XW]

    return _call_with_exchange(
        body, name=name, grid=(GDN_QK_HEADS, nch),
        in_specs=[pl.BlockSpec((rows, C_HEAD), lambda h, i: (i, h)),
                  pl.BlockSpec((4, XW), lambda h, i: (0, h)),
                  pl.BlockSpec((1, 8, HD), lambda h, i: (h, 0, 0)),
                  pl.BlockSpec((1, HD), lambda h, i: (0, 0))],
        out_specs=[pl.BlockSpec((rows, 2 * HD), lambda h, i: (i, h)),
                   pl.BlockSpec((1, 1, 2, HD, HD), lambda h, i: (h, i, 0, 0, 0)),
                   pl.BlockSpec((1, 1, nsys, GROUP, GROUP), lambda h, i: (h, i, 0, 0, 0))],
        out_shape=[jax.ShapeDtypeStruct((t, 2 * GDN_QK_HEADS * HD), BF16),
                   jax.ShapeDtypeStruct((GDN_QK_HEADS, nch, 2, HD, HD), F32),
                   jax.ShapeDtypeStruct((GDN_QK_HEADS, nch, nsys, GROUP, GROUP), BF16)],
        scratch_shapes=[pltpu.VMEM((HD, HD), F32), pltpu.VMEM((HD, HD), F32), pltpu.VMEM((8, XW), F32)],
        args=(pc, cw, cpar, nw), exchange=exchange)


def _branch_c_bwd(pc, states, inverses, cw, cpar, nw, dy, name, exchange=None):
    t = pc.shape[0]
    rows = _block_rows(t, GDN_BLOCK_CHUNKS)
    nch = t // rows
    XW = 4 * HD
    hb = rows // 8

    nsys = 2 * rows // GROUP

    def body(p_ref, halo_ref, s_ref, x_ref, w_ref, cp_ref, nw_ref, dy_ref, dp_ref, dw_ref, dcp_ref, dnw_ref,
             dsa_ref, dsb_ref, carry_ref):
        h, i = pl.program_id(0), pl.program_id(1)

        @pl.when(i == 0)
        def _():
            dsa_ref[...] = jnp.zeros_like(dsa_ref)
            dsb_ref[...] = jnp.zeros_like(dsb_ref)
            carry_ref[...] = jnp.zeros_like(carry_ref)
            dw_ref[...] = jnp.zeros_like(dw_ref)
            dcp_ref[...] = jnp.zeros_like(dcp_ref)

        @pl.when((i == 0) & (h == 0))
        def _():
            dnw_ref[...] = jnp.zeros_like(dnw_ref)

        keep = 1.0 - (i == nch - 1).astype(F32)
        x_ext = jnp.concatenate([halo_ref[:, 0:XW] * keep, p_ref[:, 0:XW]], axis=0)
        block = functools.partial(_gdn_block, known=tuple(x_ref[0, 0, n] for n in range(nsys)))
        _, vjp, _ = jax.vjp(block, x_ext, p_ref[:, XW:XW + 2 * HD], p_ref[:, XW + 2 * HD:XW + 3 * HD],
                            s_ref[0, 0, 0], s_ref[0, 0, 1], w_ref[0:1, :], w_ref[1:2, :], w_ref[2:3, :], w_ref[3:4, :],
                            cp_ref[0, 0:1, :], cp_ref[0, 1:2, :], nw_ref[...], has_aux=True)
        dx, dz, dba, dsa, dsb, dw0, dw1, dw2, dw3, dal, ddt, dnw = vjp((dy_ref[...], dsa_ref[...], dsb_ref[...]))
        dp_ref[:, 0:XW] = _add_to_tail(dx[8:], carry_ref[...]).astype(BF16)
        dp_ref[:, XW:XW + 2 * HD] = dz.astype(BF16)
        dp_ref[:, XW + 2 * HD:XW + 3 * HD] = dba.astype(BF16)
        carry_ref[...] = dx[:8] * keep
        dsa_ref[...] = dsa
        dsb_ref[...] = dsb
        dw_ref[0:1, :] += dw0
        dw_ref[1:2, :] += dw1
        dw_ref[2:3, :] += dw2
        dw_ref[3:4, :] += dw3
        dcp_ref[0, 0:1, :] += dal
        dcp_ref[0, 1:2, :] += ddt
        dnw_ref[...] += dnw

    rev = lambda h, i: (nch - 1 - i, h)
    return _call_with_exchange(
        body, name=name, grid=(GDN_QK_HEADS, nch),
        in_specs=[pl.BlockSpec((rows, C_HEAD), rev),
                  pl.BlockSpec((8, C_HEAD), lambda h, i: (jnp.maximum((nch - 1 - i) * hb - 1, 0), h)),
                  pl.BlockSpec((1, 1, 2, HD, HD), lambda h, i: (h, nch - 1 - i, 0, 0, 0)),
                  pl.BlockSpec((1, 1, nsys, GROUP, GROUP), lambda h, i: (h, nch - 1 - i, 0, 0, 0)),
                  pl.BlockSpec((4, XW), lambda h, i: (0, h)),
                  pl.BlockSpec((1, 8, HD), lambda h, i: (h, 0, 0)),
                  pl.BlockSpec((1, HD), lambda h, i: (0, 0)),
                  pl.BlockSpec((rows, 2 * HD), rev)],
        out_specs=[pl.BlockSpec((rows, C_HEAD), rev),
                   pl.BlockSpec((4, XW), lambda h, i: (0, h)),
                   pl.BlockSpec((1, 8, HD), lambda h, i: (h, 0, 0)),
                   pl.BlockSpec((1, HD), lambda h, i: (0, 0))],
        out_shape=[jax.ShapeDtypeStruct((t, NC_COLS), BF16), jax.ShapeDtypeStruct((4, GDN_QK_HEADS * XW), F32),
                   jax.ShapeDtypeStruct((GDN_QK_HEADS, 8, HD), F32), jax.ShapeDtypeStruct((1, HD), F32)],
        scratch_shapes=[pltpu.VMEM((HD, HD), F32), pltpu.VMEM((HD, HD), F32), pltpu.VMEM((8, XW), F32)],
        args=(pc, pc, states, inverses, cw, cpar, nw, dy), exchange=exchange)


def _merge_fwd(pg, bg, ya, yb, yc, name):
    t = pg.shape[0]
    blk = _tile(t, 512)

    def body(g_ref, b_ref, a_ref, b2_ref, c_ref, o_ref):
        gate = _sigmoid(g_ref[...] + b_ref[...])
        o_ref[...] = (gate[:, 0:D] * a_ref[...] + gate[:, D:2 * D] * b2_ref[...] + gate[:, 2 * D:3 * D] * c_ref[...]).astype(BF16)

    row = pl.BlockSpec((blk, D), lambda i: (i, 0))
    return pl.pallas_call(
        body, name=name, grid=(t // blk,),
        in_specs=[pl.BlockSpec((blk, NG), lambda i: (i, 0)), pl.BlockSpec((1, NG), lambda i: (0, 0)), row, row, row],
        out_specs=row, out_shape=jax.ShapeDtypeStruct((t, D), BF16),
        compiler_params=_cparams("parallel"))(pg, bg, ya, yb, yc)


def _merge_bwd(dm, pg, bg, ya, yb, yc, name):
    t = pg.shape[0]
    blk = _tile(t, 512)

    def body(dm_ref, g_ref, b_ref, a_ref, b2_ref, c_ref, dg_ref, da_ref, db_ref, dc_ref, dbg_ref):
        @pl.when(pl.program_id(0) == 0)
        def _():
            dbg_ref[...] = jnp.zeros_like(dbg_ref)

        gate = _sigmoid(g_ref[...] + b_ref[...])
        dmv = dm_ref[...].astype(F32)
        for j, (y_ref, dy_ref) in enumerate(((a_ref, da_ref), (b2_ref, db_ref), (c_ref, dc_ref))):
            gj = gate[:, j * D:(j + 1) * D]
            dy_ref[...] = (dmv * gj).astype(BF16)
            dgj = dmv * y_ref[...] * gj * (1.0 - gj)
            dg_ref[:, j * D:(j + 1) * D] = dgj.astype(BF16)
            dbg_ref[:, j * D:(j + 1) * D] += jnp.sum(dgj, axis=0, keepdims=True)

    row = pl.BlockSpec((blk, D), lambda i: (i, 0))
    wide = pl.BlockSpec((blk, NG), lambda i: (i, 0))
    vec = pl.BlockSpec((1, NG), lambda i: (0, 0))
    return pl.pallas_call(
        body, name=name, grid=(t // blk,), in_specs=[row, wide, vec, row, row, row],
        out_specs=[wide, row, row, row, vec],
        out_shape=[jax.ShapeDtypeStruct((t, NG), BF16)] + [jax.ShapeDtypeStruct((t, D), BF16)] * 3
                  + [jax.ShapeDtypeStruct((1, NG), F32)],
        compiler_params=_cparams("arbitrary"))(dm, pg, bg, ya, yb, yc)


def _adamw_math(w, g, m, v):
    m = ADAM_B1 * m + (1.0 - ADAM_B1) * g
    v = ADAM_B2 * v + (1.0 - ADAM_B2) * (g * g)
    m_hat = m / (1.0 - ADAM_B1 ** ADAM_STEP)
    v_hat = v / (1.0 - ADAM_B2 ** ADAM_STEP)
    delta = -ADAM_LR * (m_hat / (jnp.sqrt(v_hat) + ADAM_EPS) + ADAM_WD * w)
    return delta, m, v


def _sum_adamw(parts, w, m, v, name):
    layers = len(parts)
    r, c = parts[0].shape[1:]
    br = r if r <= 256 else 256
    nb = r // br
    assert r % br == 0 and w.shape == (layers * r, c)

    def body(*refs):
        w_ref, m_ref, v_ref, g_ref, d_ref, nm_ref, nv_ref = refs[layers:]
        for l in range(layers):
            @pl.when(pl.program_id(0) == l)
            def _(p_ref=refs[l]):
                g = p_ref[0].astype(F32)
                for k in range(1, N_DEV):
                    g = g + p_ref[k].astype(F32)
                g_ref[...] = g
                d_ref[...], nm_ref[...], nv_ref[...] = _adamw_math(w_ref[...], g, m_ref[...], v_ref[...])

    blk = pl.BlockSpec((br, c), lambda l, i: (l * nb + i, 0))
    part_specs = [pl.BlockSpec((N_DEV, br, c), lambda l, i, q=q: (0, jnp.where(l == q, i, jnp.where(l < q, 0, nb - 1)), 0))
                  for q in range(layers)]
    return pl.pallas_call(
        body, name=name, grid=(layers, nb), in_specs=part_specs + [blk, blk, blk], out_specs=[blk] * 4,
        out_shape=[jax.ShapeDtypeStruct((layers * r, c), F32)] * 4,
        compiler_params=_cparams("arbitrary", "arbitrary"))(*parts, w, m, v)


def _adamw(g, w, m, v, name):
    def body(g_ref, w_ref, m_ref, v_ref, d_ref, nm_ref, nv_ref):
        d_ref[...], nm_ref[...], nv_ref[...] = _adamw_math(w_ref[...], g_ref[...], m_ref[...], v_ref[...])

    return pl.pallas_call(body, name=name, out_shape=[jax.ShapeDtypeStruct(w.shape, F32)] * 3)(g, w, m, v)


def _sum_slots(parts, name):
    def body(p_ref, o_ref):
        g = p_ref[0]
        for k in range(1, N_DEV):
            g = g + p_ref[k]
        o_ref[...] = g

    return pl.pallas_call(body, name=name, out_shape=jax.ShapeDtypeStruct(parts.shape[1:], F32))(parts)


def _exchange(srcs, name, broadcast):
    n = len(srcs)

    def body(*refs):
        copies = _exchange_copies(refs[:n], refs[n:2 * n], *refs[2 * n:], broadcast)
        for cp in copies:
            cp.start()
        for cp in copies:
            cp.wait()

    return pl.pallas_call(
        body, name=name, in_specs=[HBM_SPEC] * n, out_specs=[HBM_SPEC] * n, out_shape=_exchange_shapes(srcs, broadcast),
        scratch_shapes=_exchange_semaphores(n))(*srcs)


def _gather_two_level(srcs, name):
    n = len(srcs)

    def body(*refs):
        src_refs, dst_refs = refs[:n], refs[n:2 * n]
        send_sems, recv_sems, local_sems = refs[2 * n:]
        x, y, c = lax.axis_index("x"), lax.axis_index("y"), lax.axis_index("c")
        index_of = lambda px, py, pc: 4 * px + 2 * py + pc
        me, other_core = index_of(x, y, c), (x, y, 1 - c)
        chips = [(1 - x, y), (x, 1 - y), (1 - x, 1 - y)]

        def copy(k, a, block, to, src=None):
            return pltpu.make_async_remote_copy(
                src_ref=dst_refs[a].at[block] if src is None else src, dst_ref=dst_refs[a].at[block],
                send_sem=send_sems.at[k, a], recv_sem=recv_sems.at[k, a], device_id=to, device_id_type=MESH)

        local = [pltpu.make_async_copy(src_refs[a], dst_refs[a].at[me], local_sems.at[a]) for a in range(n)]
        first = [copy(0, a, me, other_core, src=src_refs[a]) for a in range(n)]
        first += [copy(1 + j, a, me, (*chip, c), src=src_refs[a]) for j, chip in enumerate(chips) for a in range(n)]
        for cp in local + first:
            cp.start()
        passed = []
        for j, chip in enumerate(chips):
            block = index_of(*chip, c)
            for a in range(n):
                copy(1 + j, a, block, (x, y, c)).wait_recv()
            for a in range(n):
                passed.append(copy(4 + j, a, block, other_core))
                passed[-1].start()
        for a in range(n):
            copy(0, a, index_of(x, y, 1 - c), (x, y, c)).wait_recv()
        for j, chip in enumerate(chips):
            for a in range(n):
                copy(4 + j, a, index_of(*chip, 1 - c), (x, y, c)).wait_recv()
        for cp in first + passed:
            cp.wait_send()
        for cp in local:
            cp.wait()

    return pl.pallas_call(
        body, name=name, in_specs=[HBM_SPEC] * n, out_specs=[HBM_SPEC] * n, out_shape=_exchange_shapes(srcs, True),
        scratch_shapes=_exchange_semaphores(n))(*srcs)


HBM_SPEC = pl.BlockSpec(memory_space=pltpu.HBM)


def _exchange_shapes(srcs, broadcast):
    return [jax.ShapeDtypeStruct((N_DEV,) + (s.shape if broadcast else s.shape[1:]), s.dtype) for s in srcs]


def _exchange_semaphores(n):
    return [pltpu.SemaphoreType.DMA((N_DEV - 1, n)), pltpu.SemaphoreType.DMA((N_DEV - 1, n)), pltpu.SemaphoreType.DMA((n,))]


def _exchange_copies(src_refs, dst_refs, send_sems, recv_sems, local_sems, broadcast):
    x, y, c = lax.axis_index("x"), lax.axis_index("y"), lax.axis_index("c")
    me = 4 * x + 2 * y + c
    copies = []
    for k in range(1, N_DEV):
        px = 1 - x if (k >> 2) & 1 else x
        py = 1 - y if (k >> 1) & 1 else y
        pc = 1 - c if k & 1 else c
        peer = 4 * px + 2 * py + pc
        for a, (src, dst) in enumerate(zip(src_refs, dst_refs)):
            copies.append(pltpu.make_async_remote_copy(
                src_ref=src if broadcast else src.at[peer], dst_ref=dst.at[me],
                send_sem=send_sems.at[k - 1, a], recv_sem=recv_sems.at[k - 1, a],
                device_id=(px, py, pc), device_id_type=MESH))
    for a, (src, dst) in enumerate(zip(src_refs, dst_refs)):
        copies.append(pltpu.make_async_copy(src if broadcast else src.at[me], dst.at[me], local_sems.at[a]))
    return copies


def _call_with_exchange(body, *, name, grid, in_specs, out_specs, out_shape, scratch_shapes, args, exchange):
    if exchange is None:
        outs = pl.pallas_call(body, name=name, grid=grid, in_specs=in_specs, out_specs=out_specs, out_shape=out_shape,
                              scratch_shapes=scratch_shapes,
                              compiler_params=_cparams(*["arbitrary"] * len(grid)))(*args)
        return outs, None
    srcs, broadcast = exchange
    n, n_in, n_out, n_scr = len(srcs), len(args), len(out_shape), len(scratch_shapes)
    steps = 1
    for g in grid:
        steps *= g

    def hosted(*refs):
        ins, src_refs = refs[:n_in], refs[n_in:n_in + n]
        outs, dst_refs = refs[n_in + n:n_in + n + n_out], refs[n_in + n + n_out:n_in + 2 * n + n_out]
        scratch = refs[n_in + 2 * n + n_out:]
        step = pl.program_id(0)
        for axis in range(1, len(grid)):
            step = step * grid[axis] + pl.program_id(axis)

        @pl.when(step == 0)
        def _():
            for cp in _exchange_copies(src_refs, dst_refs, *scratch[n_scr:], broadcast):
                cp.start()

        body(*ins, *outs, *scratch[:n_scr])

        @pl.when(step == steps - 1)
        def _():
            for cp in _exchange_copies(src_refs, dst_refs, *scratch[n_scr:], broadcast):
                cp.wait()

    outs = pl.pallas_call(
        hosted, name=name, grid=grid, in_specs=list(in_specs) + [HBM_SPEC] * n, out_specs=list(out_specs) + [HBM_SPEC] * n,
        out_shape=list(out_shape) + _exchange_shapes(srcs, broadcast),
        scratch_shapes=list(scratch_shapes) + _exchange_semaphores(n),
        compiler_params=_cparams(*["arbitrary"] * len(grid)))(*args, *srcs)
    return outs[:n_out], outs[n_out:]


def _regroup_w_in(w):
    wa = w[:, OFF_A:OFF_A + NA]
    seg = lambda off, h, n=HD: w[:, off + h * n: off + (h + 1) * n]
    wb = jnp.concatenate([seg(OFF_B + s * 512, h) for h in range(HGRN_HEADS) for s in range(4)], axis=1)
    parts = []
    for h in range(GDN_QK_HEADS):
        small = jnp.concatenate(
            [w[:, OFF_BETA + 2 * h: OFF_BETA + 2 * h + 2], w[:, OFF_CA + 2 * h: OFF_CA + 2 * h + 2],
             jnp.zeros((w.shape[0], HD - 4), w.dtype)], axis=1)
        parts += [seg(OFF_CQ, h), seg(OFF_CK, h), seg(OFF_CV, h, 2 * HD), seg(OFF_CZ, h, 2 * HD), small]
    wc = jnp.concatenate(parts, axis=1)
    wg = w[:, OFF_G:OFF_G + NG]
    return wa, wb, wc, wg


def _ungroup_dw_in(da, db, dc, dg):
    bq = [jnp.concatenate([db[:, h * 512 + s * HD: h * 512 + (s + 1) * HD] for h in range(HGRN_HEADS)], axis=1)
          for s in range(4)]
    ch = lambda h, lo, hi: dc[:, h * C_HEAD + lo: h * C_HEAD + hi]
    heads = range(GDN_QK_HEADS)
    cq = jnp.concatenate([ch(h, 0, HD) for h in heads], axis=1)
    ck = jnp.concatenate([ch(h, HD, 2 * HD) for h in heads], axis=1)
    cv = jnp.concatenate([ch(h, 2 * HD, 4 * HD) for h in heads], axis=1)
    cz = jnp.concatenate([ch(h, 4 * HD, 6 * HD) for h in heads], axis=1)
    cbeta = jnp.concatenate([ch(h, 6 * HD, 6 * HD + 2) for h in heads], axis=1)
    ca = jnp.concatenate([ch(h, 6 * HD + 2, 6 * HD + 4) for h in heads], axis=1)
    return jnp.concatenate([da] + bq + [cq, ck, cv, cbeta, ca, cz, dg], axis=1)


def _regroup_conv_c(cw):
    parts = []
    for h in range(GDN_QK_HEADS):
        parts += [cw[:, h * HD:(h + 1) * HD], cw[:, 512 + h * HD: 512 + (h + 1) * HD],
                  cw[:, 1024 + 2 * h * HD: 1024 + (2 * h + 2) * HD]]
    return jnp.concatenate(parts, axis=1)


def _ungroup_conv_c(d):
    heads = range(GDN_QK_HEADS)
    q = jnp.concatenate([d[:, h * 512: h * 512 + HD] for h in heads], axis=1)
    k = jnp.concatenate([d[:, h * 512 + HD: h * 512 + 2 * HD] for h in heads], axis=1)
    v = jnp.concatenate([d[:, h * 512 + 2 * HD: h * 512 + 4 * HD] for h in heads], axis=1)
    return jnp.concatenate([q, k, v], axis=1)


def _numel(shape):
    n = 1
    for d in shape:
        n *= d
    return n


def _pack(arrays, rows):
    flat = jnp.concatenate([a.reshape(-1) for a in arrays])
    return jnp.pad(flat, (0, rows * 128 - flat.shape[0])).reshape(rows, 128)


def _unpack(packed, shapes):
    flat = packed.reshape(-1)
    out, off = [], 0
    for s in shapes:
        out.append(flat[off:off + _numel(s)].reshape(s))
        off += _numel(s)
    return out


def _rows_for(shapes):
    return -(-sum(_numel(s) for s in shapes) // 1024) * 8


def kernel(x, norm_w, w_in, b_gate, conv_a, conv_c, a_log, dt_bias, lower_bounds, hgrn_norm_w, gdn_norm_w, w_out_a, w_out_b, w_out_c, w_o, final_norm_w, loss_target, m_norm_w, m_w_in, m_b_gate, m_conv_a, m_conv_c, m_a_log, m_dt_bias, m_lower_bounds, m_hgrn_norm_w, m_gdn_norm_w, m_w_out_a, m_w_out_b, m_w_out_c, m_w_o, m_final_norm_w, v_norm_w, v_w_in, v_b_gate, v_conv_a, v_conv_c, v_a_log, v_dt_bias, v_lower_bounds, v_hgrn_norm_w, v_gdn_norm_w, v_w_out_a, v_w_out_b, v_w_out_c, v_w_o, v_final_norm_w):
    me = 4 * lax.axis_index("x") + 2 * lax.axis_index("y") + lax.axis_index("c")
    xs = x[0]
    target = loss_target[0]
    in_shard = w_in.shape[2]

    big = [w_in, w_out_a, w_out_b, w_out_c, w_o]
    shards_of = lambda l: [w[l].astype(BF16) for w in big]
    conv_shapes = [(DEPTH, 3, CONV_W), (DEPTH, 4, 2048)]
    conv_rows = _rows_for(conv_shapes)
    ca_full = lax.dynamic_update_slice(jnp.zeros(conv_shapes[0], F32), conv_a, (0, 0, me * conv_a.shape[2]))
    cc_full = lax.dynamic_update_slice(jnp.zeros(conv_shapes[1], F32), conv_c, (0, 0, me * conv_c.shape[2]))
    g_in0, conv_parts = _gather_two_level([w_in[0].astype(BF16), _pack([ca_full, cc_full], conv_rows)], "gather_l0")
    conv_a_full, conv_c_full = _unpack(_sum_slots(conv_parts, "sum_conv"), conv_shapes)

    lb_pad = jnp.pad(lower_bounds, ((0, 8 - DEPTH), (0, 0)))
    lbs = _lower_bounds_fwd(lb_pad, "lower_bounds_fwd")

    def input_weights(l, g_in):
        wa, wb, wc, wg = _regroup_w_in(jnp.concatenate([g_in[q] for q in range(N_DEV)], axis=1))
        lanes = lambda vec: jnp.pad(vec.reshape(GDN_QK_HEADS, 1, 2), ((0, 0), (0, 0), (0, HD - 2)))
        cpar = jnp.concatenate([lanes(a_log[l]), lanes(dt_bias[l]), jnp.zeros((GDN_QK_HEADS, 6, HD), F32)], axis=1)
        return dict(
            wa=wa, wb=wb, wc=wc, wg=wg, cpar=cpar,
            nw=norm_w[l:l + 1], bg=b_gate[l:l + 1], cwa=conv_a_full[l], cwc=_regroup_conv_c(conv_c_full[l]),
            lb=lbs[l:l + 1], hnw=hgrn_norm_w[l:l + 1], gnw=gdn_norm_w[l:l + 1])

    def output_weights(g_oa, g_ob, g_oc, g_o):
        return dict(woa=jnp.concatenate([g_oa[q] for q in range(N_DEV)], axis=1),
                    wob=jnp.concatenate([g_ob[q] for q in range(N_DEV)], axis=1), woc=g_oc.reshape(D, D), wo=g_o.reshape(D, D))

    layers = [input_weights(0, g_in0)]

    saved = []
    cur = xs
    for l in range(DEPTH):
        L = layers[l]
        n = f"l{l}_"
        h = _rmsnorm_fwd(cur, L["nw"], n + "rms")
        pa = _matmul(h, L["wa"], "nn", n + "proj_a")
        pb = _matmul(h, L["wb"], "nn", n + "proj_b")
        pc = _matmul(h, L["wc"], "nn", n + "proj_c")
        pg = _matmul(h, L["wg"], "nn", n + "proj_g", out_dtype=BF16)
        ua = _branch_a_fwd(pa, L["cwa"], n + "conv_fwd")
        carry = (shards_of(l)[1:], True) if l == 0 else None
        (ub, sb), gathered = _branch_b_fwd(pb, L["lb"], L["hnw"], n + "hgrn_fwd", exchange=carry)
        if carry is not None:
            L.update(output_weights(*gathered))
        carry = (shards_of(l + 1), True) if l + 1 < DEPTH else None
        (uc, sc, xc), gathered = _branch_c_fwd(pc, L["cwc"], L["cpar"], L["gnw"], n + "gdn_fwd", exchange=carry)
        if carry is not None:
            layers.append(dict(input_weights(l + 1, gathered[0]), **output_weights(*gathered[1:])))
        ya = _matmul(ua, L["woa"], "nn", n + "out_a", out_dtype=BF16)
        yb = _matmul(ub, L["wob"], "nn", n + "out_b", out_dtype=BF16)
        yc = _matmul(uc, L["woc"], "nn", n + "out_c", out_dtype=BF16)
        merged = _merge_fwd(pg, L["bg"], ya, yb, yc, n + "merge")
        nxt = _matmul(merged, L["wo"], "nn", n + "out_o", residual=cur)
        saved.append(dict(x=cur, h=h, pa=pa, pb=pb, pc=pc, pg=pg, ua=ua, ub=ub, uc=uc, sb=sb, sc=sc, xc=xc,
                          ya=ya, yb=yb, yc=yc, merged=merged))
        cur = nxt

    loss_part, dx, d_final = _loss_head(cur, final_norm_w.reshape(1, D), target, "loss_head")

    def outgoing(g):
        cols = lambda a, n: jnp.stack([a[:, p * n:(p + 1) * n] for p in range(N_DEV)]).astype(BF16)
        rows = lambda a: a.reshape(N_DEV, a.shape[0] // N_DEV, a.shape[1]).astype(BF16)
        first = [cols(g["w_in"], in_shard)] if "w_in" in g else [None]
        if "w_o" not in g:
            return first
        return first + [cols(g["w_out_a"], 128), cols(g["w_out_b"], 128), rows(g["w_out_c"]), rows(g["w_o"])]

    grads = [None] * DEPTH
    dlbs_rows = [None] * DEPTH
    incoming = [None] * DEPTH
    for l in reversed(range(DEPTH)):
        L, S = layers[l], saved[l]
        n = f"l{l}_"
        dmerged = _matmul(dx, L["wo"], "nt", n + "d_merged", out_dtype=BF16)
        d_wo = _matmul(S["merged"], dx, "tn", n + "dw_o", out_dtype=BF16)
        dpg, dya, dyb, dyc, d_bg = _merge_bwd(dmerged, S["pg"], L["bg"], S["ya"], S["yb"], S["yc"], n + "merge_bwd")
        dua = _matmul(dya, L["woa"], "nt", n + "d_ua")
        dub = _matmul(dyb, L["wob"], "nt", n + "d_ub")
        duc = _matmul(dyc, L["woc"], "nt", n + "d_uc")
        d_woa = _matmul(S["ua"], dya, "tn", n + "dw_out_a", out_dtype=BF16)
        d_wob = _matmul(S["ub"], dyb, "tn", n + "dw_out_b", out_dtype=BF16)
        d_woc = _matmul(S["uc"], dyc, "tn", n + "dw_out_c", out_dtype=BF16)
        dpa, d_cwa = _branch_a_bwd(S["pa"], L["cwa"], dua, n + "conv_bwd")
        out_grads = dict(w_out_a=d_woa, w_out_b=d_wob, w_out_c=d_woc, w_o=d_wo)
        carry = (outgoing(out_grads)[1:], False) if l == 0 else None
        (dpb, d_lb, d_hnw), arrived_out = _branch_b_bwd(S["pb"], S["sb"], L["lb"], L["hnw"], dub, n + "hgrn_bwd", exchange=carry)
        carry = (outgoing(grads[l + 1]), False) if l + 1 < DEPTH else None
        (dpc, d_cwc, d_cpar, d_gnw), arrived = _branch_c_bwd(S["pc"], S["sc"], S["xc"], L["cwc"], L["cpar"], L["gnw"], duc,
                                                             n + "gdn_bwd", exchange=carry)
        if carry is not None:
            incoming[l + 1] = arrived
        d_win = _ungroup_dw_in(*[_matmul(S["h"], dp, "tn", f"{n}dw_{piece}", out_dtype=BF16)
                                 for dp, piece in ((dpa, "a"), (dpb, "b"), (dpc, "c"), (dpg, "g"))])
        carry = (outgoing(dict(w_in=d_win)), False) if l == 0 else None
        dh, arrived_in = _matmul_nt_sum([(dpa, L["wa"]), (dpb, L["wb"]), (dpc, L["wc"]), (dpg, L["wg"])], n + "dh",
                                        exchange=carry)
        (dx, d_nw), _ = _rmsnorm_bwd(dh, S["x"], L["nw"], dx, n + "rms_bwd")
        dlbs_rows[l] = d_lb
        grads[l] = dict(w_in=d_win, w_out_a=d_woa, w_out_b=d_wob, w_out_c=d_woc, w_o=d_wo, norm_w=d_nw[0],
                        b_gate=d_bg[0], conv_a=d_cwa, conv_c=_ungroup_conv_c(d_cwc),
                        a_log=d_cpar[:, 0, 0:2].reshape(-1), dt_bias=d_cpar[:, 1, 0:2].reshape(-1),
                        hgrn_norm_w=d_hnw[0], gdn_norm_w=d_gnw[0])
    grad_x = dx[None]
    d_lower = _lower_bounds_bwd(lb_pad, jnp.pad(jnp.concatenate(dlbs_rows, axis=0), ((0, 8 - DEPTH), (0, 0))),
                                "lower_bounds_bwd")[:DEPTH]

    incoming[0] = list(arrived_in) + list(arrived_out)
    stack = lambda name: jnp.stack([grads[l][name] for l in range(DEPTH)])
    big_out = {}
    for j, (name, w, m, v) in enumerate((("w_in", w_in, m_w_in, v_w_in), ("w_out_a", w_out_a, m_w_out_a, v_w_out_a),
                                         ("w_out_b", w_out_b, m_w_out_b, v_w_out_b), ("w_out_c", w_out_c, m_w_out_c, v_w_out_c),
                                         ("w_o", w_o, m_w_o, v_w_o))):
        parts = [incoming[l][j] for l in range(DEPTH)]
        r2 = lambda a: a.reshape(DEPTH * parts[0].shape[1], parts[0].shape[2])
        outs = _sum_adamw(parts, r2(w), r2(m), r2(v), "adamw_" + name)
        big_out[name] = [o.reshape(w.shape) for o in outs]

    small_names = ["norm_w", "b_gate", "conv_a", "conv_c", "a_log", "dt_bias", "lower_bounds", "hgrn_norm_w",
                   "gdn_norm_w", "final_norm_w", "loss"]
    small_vals = {k: stack(k) for k in ("norm_w", "b_gate", "conv_a", "conv_c", "a_log", "dt_bias", "hgrn_norm_w", "gdn_norm_w")}
    small_vals.update(lower_bounds=d_lower, final_norm_w=d_final[0], loss=loss_part.reshape(1))
    small_shapes = [small_vals[k].shape for k in small_names]
    small_rows = _rows_for(small_shapes)
    small_parts, = _exchange([_pack([small_vals[k] for k in small_names], small_rows)], "exchange_small", broadcast=True)
    total = dict(zip(small_names, _unpack(_sum_slots(small_parts, "sum_small"), small_shapes)))
    loss = total["loss"][0]
    g_conv_a = lax.dynamic_slice(total["conv_a"], (0, 0, me * conv_a.shape[2]), conv_a.shape)
    g_conv_c = lax.dynamic_slice(total["conv_c"], (0, 0, me * conv_c.shape[2]), conv_c.shape)

    small_w = dict(norm_w=(norm_w, m_norm_w, v_norm_w), b_gate=(b_gate, m_b_gate, v_b_gate),
                   conv_a=(conv_a, m_conv_a, v_conv_a), conv_c=(conv_c, m_conv_c, v_conv_c),
                   a_log=(a_log, m_a_log, v_a_log), dt_bias=(dt_bias, m_dt_bias, v_dt_bias),
                   lower_bounds=(lower_bounds, m_lower_bounds, v_lower_bounds),
                   hgrn_norm_w=(hgrn_norm_w, m_hgrn_norm_w, v_hgrn_norm_w), gdn_norm_w=(gdn_norm_w, m_gdn_norm_w, v_gdn_norm_w),
                   final_norm_w=(final_norm_w, m_final_norm_w, v_final_norm_w))
    small_g = dict(total, conv_a=g_conv_a, conv_c=g_conv_c)
    upd_names = small_names[:-1]
    upd_shapes = [small_w[k][0].shape for k in upd_names]
    upd_rows = _rows_for(upd_shapes)
    pk = lambda j: _pack([small_w[k][j] for k in upd_names], upd_rows)
    s_delta, s_m, s_v = _adamw(_pack([small_g[k] for k in upd_names], upd_rows), pk(0), pk(1), pk(2), "adamw_small")
    small_out = {k: [small_g[k], d, mm, vv] for k, d, mm, vv in
                 zip(upd_names, _unpack(s_delta, upd_shapes), _unpack(s_m, upd_shapes), _unpack(s_v, upd_shapes))}

    order = ["norm_w", "w_in", "b_gate", "conv_a", "conv_c", "a_log", "dt_bias", "lower_bounds", "hgrn_norm_w",
             "gdn_norm_w", "w_out_a", "w_out_b", "w_out_c", "w_o", "final_norm_w"]
    res = {**small_out, **big_out}
    outs = [loss, grad_x]
    for j in range(4):
        outs += [res[k][j] for k in order]
    return tuple(outs)
```

```python
import functools

import jax
import jax.numpy as jnp
from jax import lax
from jax.experimental import pallas as pl
from jax.experimental.pallas import tpu as pltpu

F32 = jnp.float32
BF16 = jnp.bfloat16
MESH = pl.DeviceIdType.MESH

N_DEV = 8
D = 1024
DEPTH = 2
CHUNK = 64
HGRN_BLOCK_CHUNKS = 16
GDN_BLOCK_CHUNKS = 8
GROUP = 128
NORM_EPS = 1e-6
L2_EPS = 1e-6
MIN_F = 1e-30
HD = 128
HGRN_HEADS = 4
GDN_QK_HEADS = 4
CONV_W = 512
IN_COLS = 10256
OFF_A, OFF_B, OFF_CQ, OFF_CK, OFF_CV, OFF_BETA, OFF_CA, OFF_CZ, OFF_G = (
    0, 2048, 4096, 4608, 5120, 6144, 6152, 6160, 7184)
NA, NB, NC_COLS, NG = 2048, 2048, 3584, 3072
C_HEAD = 896

ADAM_LR, ADAM_B1, ADAM_B2, ADAM_EPS, ADAM_WD, ADAM_STEP = 0.001, 0.9, 0.999, 1e-08, 0.01, 10

VMEM_LIMIT = 56 * 1024 * 1024
MM_TILE = 1024


def _cparams(*sem):
    return pltpu.CompilerParams(dimension_semantics=sem, vmem_limit_bytes=VMEM_LIMIT)


def _tile(dim, cap):
    if dim <= cap:
        return dim
    t = (cap // 128) * 128
    while dim % t:
        t -= 128
    return t


def _sigmoid(x):
    return 1.0 / (1.0 + jnp.exp(-x))


def _silu(x):
    return x * _sigmoid(x)


def _softplus(x):
    return jnp.maximum(x, 0.0) + jnp.log(1.0 + jnp.exp(-jnp.abs(x)))


def _dot(a, b, dims, precision=None):
    if precision is None:
        a, b = a.astype(BF16), b.astype(BF16)
    return lax.dot_general(a, b, (dims, ((), ())), precision=precision, preferred_element_type=F32)


def _nn(a, b, precision=None):
    return _dot(a, b, ((1,), (0,)), precision)


def _nt(a, b, precision=None):
    return _dot(a, b, ((1,), (1,)), precision)


def _tn(a, b, precision=None):
    return _dot(a, b, ((0,), (0,)), precision)


def _sum_rows_split(mat01, x):
    m = mat01.astype(BF16)
    hi = x.astype(BF16)
    low = (x - hi.astype(F32)).astype(BF16)
    return _nn(m, hi) + _nn(m, low)


@functools.partial(jax.custom_vjp, nondiff_argnums=(1,))
def _shift_rows(x, d):
    return x if d == 0 else pltpu.roll(x, d, 0)


def _shift_rows_fwd(x, d):
    return _shift_rows(x, d), None


def _shift_rows_bwd(d, _, ct):
    return ((ct if d == 0 else pltpu.roll(ct, ct.shape[0] - d, 0)),)


_shift_rows.defvjp(_shift_rows_fwd, _shift_rows_bwd)


def _iota2(shape):
    return lax.broadcasted_iota(jnp.int32, shape, 0), lax.broadcasted_iota(jnp.int32, shape, 1)


def _lane_pick(x, i):
    lane = lax.broadcasted_iota(jnp.int32, x.shape, 1)
    return jnp.sum(jnp.where(lane == i, x, 0.0), axis=1, keepdims=True)


def _hgrn_block(qr, fr, ir, zr, st0, lb, nw):
    rows = qr.shape[0]
    r, c = _iota2((CHUNK, CHUNK))
    halves = [1 << j for j in range(CHUNK.bit_length() - 1)]
    mats = [c <= r, c > r]
    pairs = []
    for hb in halves:
        same = (r // hb) == (c // hb)
        if hb > 1:
            mats += [(c <= r) & same, (c > r) & same]
        pairs.append(((r // (2 * hb)) == (c // (2 * hb))) & ((r // hb) == (c // hb) + 1))
    stack = jnp.concatenate([m.astype(F32) for m in mats], axis=0)

    q = _silu(qr) * (HD ** -0.5)
    fg = lb + (1.0 - lb) * _sigmoid(fr)
    logf = jnp.log(jnp.maximum(fg, MIN_F))
    kk = 1.0 - fg
    v = ir

    chunks = [slice(s, s + CHUNK) for s in range(0, rows, CHUNK)]
    cums = [_sum_rows_split(stack, logf[sl]) for sl in chunks]
    part = lambda i: jnp.concatenate([cs[i * CHUNK:(i + 1) * CHUNK] for cs in cums], axis=0)
    qg = q * jnp.exp(part(0))
    ks = kk * jnp.exp(part(1))
    q_lv = [q * jnp.exp(logf)] + [q * jnp.exp(part(2 * j)) for j in range(1, len(halves))]
    k_lv = [kk] + [kk * jnp.exp(part(2 * j + 1)) for j in range(1, len(halves))]
    st = st0
    outs = []
    for sl in chunks:
        scores = jnp.where(pairs[0], _nt(q_lv[0][sl], k_lv[0][sl]), 0.0)
        for j in range(1, len(halves)):
            scores += jnp.where(pairs[j], _nt(q_lv[j][sl], k_lv[j][sl]), 0.0)
        outs.append(_nn(scores, v[sl]) + _nt(qg[sl], st))
        st = st * jnp.exp(jnp.sum(logf[sl], axis=0, keepdims=True)) + _tn(v[sl], ks[sl])
    o = jnp.concatenate(outs, axis=0) + jnp.sum(q * kk, axis=1, keepdims=True) * v
    y = o * lax.rsqrt(jnp.mean(o * o, axis=1, keepdims=True) + NORM_EPS) * nw * _silu(zr)
    return y, st


def _unit_lower_inverses(ms):
    r, c = _iota2(ms[0].shape)
    xs = [jnp.where(r == c, 1.0, 0.0) - jnp.where((r // 2) == (c // 2), m, 0.0) for m in ms]
    b = 2
    while b < CHUNK:
        pick = ((r // (2 * b)) == (c // (2 * b))) & ((r // b) != (c // b))
        ts = [_nn(x, jnp.where(pick, m, 0.0)) for x, m in zip(xs, ms)]
        xs = [x - _nn(t, x) for x, t in zip(xs, ts)]
        b *= 2
    return tuple(x.astype(BF16) for x in xs)


@jax.custom_vjp
def _known_inverses(ms, xs):
    return xs


def _known_inverses_fwd(ms, xs):
    return xs, xs


def _known_inverses_bwd(xs, cts):
    r, c = _iota2(xs[0].shape)
    keep = (c < r) & ((r // CHUNK) == (c // CHUNK))
    ts = [_tn(x, ct) for x, ct in zip(xs, cts)]
    return (tuple(jnp.where(keep, -_nt(t, x), 0.0) for t, x in zip(ts, xs)), tuple(jnp.zeros_like(x) for x in xs))


_known_inverses.defvjp(_known_inverses_fwd, _known_inverses_bwd)


def _chunk_cumsum(x):
    row = lax.broadcasted_iota(jnp.int32, x.shape, 0) % CHUNK
    d = 1
    while d < CHUNK:
        x = x + jnp.where(row >= d, _shift_rows(x, d), 0.0)
        d *= 2
    return x


def _gdn_block(x_ext, z, ba, s0a, s0b, w0, w1, w2, w3, alog, dtb, nw, known=None):
    rows = z.shape[0]
    conv = (w0 * _shift_rows(x_ext, 3) + w1 * _shift_rows(x_ext, 2) + w2 * _shift_rows(x_ext, 1) + w3 * x_ext)
    cc = _silu(conv[8:])
    qc, kc = cc[:, 0:HD], cc[:, HD:2 * HD]
    q = qc * lax.rsqrt(jnp.sum(qc * qc, axis=1, keepdims=True) + L2_EPS) * (HD ** -0.5)
    k = kc * lax.rsqrt(jnp.sum(kc * kc, axis=1, keepdims=True) + L2_EPS)

    r, c = _iota2((GROUP, GROUP))
    same = (r // CHUNK) == (c // CHUNK)
    causal, strict, eye = same & (c <= r), same & (c < r), r == c
    heads = (0, 1)
    groups = [slice(lo, lo + GROUP) for lo in range(0, rows, GROUP)]
    chunks = [slice(lo, lo + CHUNK) for lo in range(0, rows, CHUNK)]

    v, loga, g_w, kb, kg, qg = [], [], [], [], [], []
    for i in heads:
        v.append(cc[:, (2 + i) * HD:(3 + i) * HD])
        beta = _sigmoid(_lane_pick(ba, i))
        a_neg = -jnp.exp(_lane_pick(alog, i))
        loga.append(a_neg * _softplus(_lane_pick(ba, 2 + i) + _lane_pick(dtb, i)))
        g_w.append(_chunk_cumsum(jnp.broadcast_to(loga[i], (rows, HD))))
        kb.append(k * beta)
        kg.append(k * jnp.exp(g_w[i]))
        qg.append(q * jnp.exp(g_w[i]))

    systems = [(i, gs) for gs in groups for i in heads]
    dec_c, ms = [], []
    for i, gs in systems:
        g_sq = g_w[i][gs]
        g_row = jnp.sum(jnp.where(eye, g_sq, 0.0), axis=0, keepdims=True)
        diff = g_sq - g_row
        dec_c.append(jnp.where(causal, jnp.exp(jnp.where(causal, diff, 0.0)), 0.0))
        ms.append(jnp.where(strict, _nt(k[gs], kb[i][gs]) * dec_c[-1], 0.0))
    xs = _unit_lower_inverses(tuple(ms)) if known is None else _known_inverses(tuple(ms), known)
    u = [[None] * len(groups) for _ in heads]
    w = [[None] * len(groups) for _ in heads]
    qk = [[None] * len(groups) for _ in heads]
    for n, (i, gs) in enumerate(systems):
        j = n // len(heads)
        u[i][j] = _nn(xs[n], v[i][gs])
        w[i][j] = _nn(xs[n], kg[i][gs])
        qk[i][j] = _nt(q[gs], kb[i][gs]) * dec_c[n]
    u = [jnp.concatenate(p, axis=0) for p in u]
    w = [jnp.concatenate(p, axis=0) for p in w]

    decay, p_mat, q_mat = {}, {}, {}
    for n, sl in enumerate(chunks):
        for i in heads:
            g_last = jnp.sum(loga[i][sl], axis=0, keepdims=True)
            kd = kb[i][sl] * jnp.exp(g_last - g_w[i][sl])
            decay[n, i] = jnp.exp(g_last)
            p_mat[n, i] = -_tn(kd, w[i][sl])
            q_mat[n, i] = _tn(kd, u[i][sl])
    s = [s0a, s0b]
    s_at = {}
    for n in range(len(chunks)):
        for i in heads:
            s_at[n, i] = s[i]
            s[i] = s[i] * decay[n, i] + _nn(p_mat[n, i], s[i]) + q_mat[n, i]

    ys = []
    for i in heads:
        e = jnp.concatenate([u[i][sl] - _nn(w[i][sl], s_at[n, i]) for n, sl in enumerate(chunks)], axis=0)
        o_state = jnp.concatenate([_nn(qg[i][sl], s_at[n, i]) for n, sl in enumerate(chunks)], axis=0)
        o = o_state + jnp.concatenate([_nn(qk[i][j], e[gs]) for j, gs in enumerate(groups)], axis=0)
        zi = z[:, i * HD:(i + 1) * HD]
        ys.append(o * lax.rsqrt(jnp.mean(o * o, axis=1, keepdims=True) + NORM_EPS) * nw * _silu(zi))
    return (jnp.concatenate(ys, axis=1), s[0], s[1]), xs


def _add_to_tail(x, tail):
    return x + jnp.concatenate([jnp.zeros((x.shape[0] - 8, x.shape[1]), x.dtype), tail], axis=0)


def _conv_a_block(ab, ac_ext, ax_ext, az, w0, w1, w2):
    u = ac_ext * ax_ext
    conv = (w0 * _shift_rows(u, 2) + w1 * _shift_rows(u, 1) + w2 * u)[8:]
    return ab * conv * _silu(az)


def _matmul(a, b, mode, name, residual=None, out_dtype=F32):
    if mode == "nn":
        (m, k), n = a.shape, b.shape[1]
    elif mode == "nt":
        (m, k), n = a.shape, b.shape[0]
    else:
        (k, m), n = a.shape, b.shape[1]
    tm, tn, tk = _tile(m, MM_TILE), _tile(n, MM_TILE), _tile(k, MM_TILE)
    if mode == "tn":
        tk = _tile(k, 2 * MM_TILE)
    elif k == tk:
        tm = _tile(m, 2 * MM_TILE)
    nk = k // tk
    dims = {"nn": ((1,), (0,)), "nt": ((1,), (1,)), "tn": ((0,), (0,))}[mode]
    a_spec = pl.BlockSpec((tk, tm), lambda i, j, s: (s, i)) if mode == "tn" else pl.BlockSpec((tm, tk), lambda i, j, s: (i, s))
    b_spec = pl.BlockSpec((tn, tk), lambda i, j, s: (j, s)) if mode == "nt" else pl.BlockSpec((tk, tn), lambda i, j, s: (s, j))
    o_spec = pl.BlockSpec((tm, tn), lambda i, j, s: (i, j))
    has_res = residual is not None

    def finish(out, r_ref, o_ref):
        if has_res:
            out = out + r_ref[...]
        o_ref[...] = out.astype(out_dtype)

    def body_one_pass(*refs):
        finish(_dot(refs[0][...], refs[1][...], dims), refs[2] if has_res else None, refs[-1])

    def body_reduce(*refs):
        a_ref, b_ref = refs[0], refs[1]
        r_ref = refs[2] if has_res else None
        o_ref, acc_ref = refs[-2], refs[-1]
        s = pl.program_id(2)

        @pl.when(s == 0)
        def _():
            acc_ref[...] = jnp.zeros_like(acc_ref)

        acc_ref[...] += _dot(a_ref[...], b_ref[...], dims)

        @pl.when(s == nk - 1)
        def _():
            finish(acc_ref[...], r_ref, o_ref)

    args, specs = [a, b], [a_spec, b_spec]
    if has_res:
        args.append(residual)
        specs.append(o_spec)
    return pl.pallas_call(
        body_one_pass if nk == 1 else body_reduce, name=name, grid=(m // tm, n // tn, nk), in_specs=specs, out_specs=o_spec,
        out_shape=jax.ShapeDtypeStruct((m, n), out_dtype),
        scratch_shapes=[] if nk == 1 else [pltpu.VMEM((tm, tn), F32)],
        compiler_params=_cparams("parallel", "parallel", "arbitrary"))(*args)


def _matmul_nt_sum(pairs, name, exchange=None):
    m, n = pairs[0][0].shape[0], pairs[0][1].shape[0]
    tm, tn = _tile(m, MM_TILE), _tile(n, MM_TILE)
    tks = [_tile(a.shape[1], MM_TILE) for a, _ in pairs]
    nks = [a.shape[1] // tk for (a, _), tk in zip(pairs, tks)]
    offs = [sum(nks[:i]) for i in range(len(pairs))]
    total = sum(nks)

    def body(*refs):
        o_ref, acc_ref = refs[-2], refs[-1]
        s = pl.program_id(2)

        @pl.when(s == 0)
        def _():
            acc_ref[...] = jnp.zeros_like(acc_ref)

        for i, (off, nk) in enumerate(zip(offs, nks)):
            @pl.when((s >= off) & (s < off + nk))
            def _(i=i):
                acc_ref[...] += _dot(refs[2 * i][...], refs[2 * i + 1][...], ((1,), (1,)))

        @pl.when(s == total - 1)
        def _():
            o_ref[...] = acc_ref[...]

    args, specs = [], []
    for (a, b), tk, off, nk in zip(pairs, tks, offs, nks):
        k_of = lambda s, off=off, nk=nk: jnp.clip(s - off, 0, nk - 1)
        args += [a, b]
        specs += [pl.BlockSpec((tm, tk), lambda i, j, s, k_of=k_of: (i, k_of(s))),
                  pl.BlockSpec((tn, tk), lambda i, j, s, k_of=k_of: (j, k_of(s)))]
    (out,), exchanged = _call_with_exchange(
        body, name=name, grid=(m // tm, n // tn, total), in_specs=specs,
        out_specs=[pl.BlockSpec((tm, tn), lambda i, j, s: (i, j))], out_shape=[jax.ShapeDtypeStruct((m, n), F32)],
        scratch_shapes=[pltpu.VMEM((tm, tn), F32)], args=args, exchange=exchange)
    return out, exchanged


def _rmsnorm_fwd(x, w, name):
    t = x.shape[0]
    blk = _tile(t, 1024)

    def body(x_ref, w_ref, h_ref):
        xv = x_ref[...]
        h_ref[...] = (xv * lax.rsqrt(jnp.mean(xv * xv, axis=1, keepdims=True) + NORM_EPS) * w_ref[...]).astype(BF16)

    return pl.pallas_call(
        body, name=name, grid=(t // blk,),
        in_specs=[pl.BlockSpec((blk, D), lambda i: (i, 0)), pl.BlockSpec((1, D), lambda i: (0, 0))],
        out_specs=pl.BlockSpec((blk, D), lambda i: (i, 0)), out_shape=jax.ShapeDtypeStruct((t, D), BF16),
        compiler_params=_cparams("parallel"))(x, w)


def _rmsnorm_bwd(dh, x, w, dxo, name, exchange=None):
    t = x.shape[0]
    blk = _tile(t, 512)

    def body(dh_ref, x_ref, w_ref, dxo_ref, dx_ref, dw_ref):
        @pl.when(pl.program_id(0) == 0)
        def _():
            dw_ref[...] = jnp.zeros_like(dw_ref)

        xv, dhv = x_ref[...], dh_ref[...]
        rs = lax.rsqrt(jnp.mean(xv * xv, axis=1, keepdims=True) + NORM_EPS)
        xh = xv * rs
        dw_ref[...] += jnp.sum(dhv * xh, axis=0, keepdims=True)
        dxh = dhv * w_ref[...]
        dx_ref[...] = rs * (dxh - xh * jnp.mean(dxh * xh, axis=1, keepdims=True)) + dxo_ref[...]

    row = pl.BlockSpec((blk, D), lambda i: (i, 0))
    vec = pl.BlockSpec((1, D), lambda i: (0, 0))
    return _call_with_exchange(
        body, name=name, grid=(t // blk,), in_specs=[row, row, vec, row], out_specs=[row, vec],
        out_shape=[jax.ShapeDtypeStruct((t, D), F32), jax.ShapeDtypeStruct((1, D), F32)],
        scratch_shapes=[], args=(dh, x, w, dxo), exchange=exchange)


def _loss_head(x, w, target, name):
    t = x.shape[0]
    blk = _tile(t, 512)

    def body(x_ref, w_ref, t_ref, loss_ref, dx_ref, dw_ref):
        @pl.when(pl.program_id(0) == 0)
        def _():
            dw_ref[...] = jnp.zeros_like(dw_ref)
            loss_ref[...] = jnp.zeros_like(loss_ref)

        xv = x_ref[...]
        rs = lax.rsqrt(jnp.mean(xv * xv, axis=1, keepdims=True) + NORM_EPS)
        xh = xv * rs
        err = xh * w_ref[...] - t_ref[...]
        loss_ref[...] += 0.5 * jnp.sum(jnp.mean(err * err, axis=1, keepdims=True), axis=0, keepdims=True)
        dy = err * (1.0 / D)
        dw_ref[...] += jnp.sum(dy * xh, axis=0, keepdims=True)
        dxh = dy * w_ref[...]
        dx_ref[...] = rs * (dxh - xh * jnp.mean(dxh * xh, axis=1, keepdims=True))

    row = pl.BlockSpec((blk, D), lambda i: (i, 0))
    vec = pl.BlockSpec((1, D), lambda i: (0, 0))
    return pl.pallas_call(
        body, name=name, grid=(t // blk,), in_specs=[row, vec, row],
        out_specs=[pl.BlockSpec((1, 1), lambda i: (0, 0)), row, vec],
        out_shape=[jax.ShapeDtypeStruct((1, 1), F32), jax.ShapeDtypeStruct((t, D), F32), jax.ShapeDtypeStruct((1, D), F32)],
        compiler_params=_cparams("arbitrary"))(x, w, target)


def _lbs_of(lb):
    r = lax.broadcasted_iota(jnp.int32, lb.shape, 0)
    real = r < DEPTH
    mx = lax.stop_gradient(jnp.max(jnp.where(real, lb, -jnp.inf), axis=0, keepdims=True))
    e = jnp.where(real, jnp.exp(jnp.where(real, lb - mx, 0.0)), 0.0)
    p = e / jnp.sum(e, axis=0, keepdims=True)
    out = jnp.zeros_like(lb)
    run = jnp.zeros_like(mx)
    for l in range(1, DEPTH):
        run = run + jnp.sum(jnp.where(r == l, p, 0.0), axis=0, keepdims=True)
        out = out + jnp.where(r == l, run, 0.0)
    return out


def _lower_bounds_fwd(lbp, name):
    def body(lb_ref, o_ref):
        o_ref[...] = _lbs_of(lb_ref[...])

    return pl.pallas_call(body, name=name, out_shape=jax.ShapeDtypeStruct(lbp.shape, F32))(lbp)


def _lower_bounds_bwd(lbp, dlbs, name):
    def body(lb_ref, d_ref, o_ref):
        _, vjp = jax.vjp(_lbs_of, lb_ref[...])
        o_ref[...] = vjp(d_ref[...])[0]

    return pl.pallas_call(body, name=name, out_shape=jax.ShapeDtypeStruct(lbp.shape, F32))(lbp, dlbs)


def _branch_a_fwd(pa, cw, name):
    t = pa.shape[0]
    blk = _tile(t, 512)
    W = CONV_W

    def body(p_ref, w_ref, y_ref, hc_ref, hx_ref):
        @pl.when(pl.program_id(0) == 0)
        def _():
            hc_ref[...] = jnp.zeros_like(hc_ref)
            hx_ref[...] = jnp.zeros_like(hx_ref)

        ac, ax = p_ref[:, W:2 * W], p_ref[:, 2 * W:3 * W]
        y_ref[...] = _conv_a_block(
            p_ref[:, 0:W], jnp.concatenate([hc_ref[...], ac], axis=0), jnp.concatenate([hx_ref[...], ax], axis=0),
            p_ref[:, 3 * W:4 * W], w_ref[0:1, :], w_ref[1:2, :], w_ref[2:3, :]).astype(BF16)
        hc_ref[...] = p_ref[blk - 8:blk, W:2 * W]
        hx_ref[...] = p_ref[blk - 8:blk, 2 * W:3 * W]

    return pl.pallas_call(
        body, name=name, grid=(t // blk,),
        in_specs=[pl.BlockSpec((blk, NA), lambda i: (i, 0)), pl.BlockSpec((3, W), lambda i: (0, 0))],
        out_specs=pl.BlockSpec((blk, W), lambda i: (i, 0)), out_shape=jax.ShapeDtypeStruct((t, W), BF16),
        scratch_shapes=[pltpu.VMEM((8, W), F32), pltpu.VMEM((8, W), F32)],
        compiler_params=_cparams("arbitrary"))(pa, cw)


def _branch_a_bwd(pa, cw, dy, name):
    t = pa.shape[0]
    blk = _tile(t, 512)
    nt_ = t // blk
    W = CONV_W
    hb = blk // 8

    def body(p_ref, halo_ref, w_ref, dy_ref, dp_ref, dw_ref, chc_ref, chx_ref):
        i = pl.program_id(0)

        @pl.when(i == 0)
        def _():
            chc_ref[...] = jnp.zeros_like(chc_ref)
            chx_ref[...] = jnp.zeros_like(chx_ref)
            dw_ref[...] = jnp.zeros_like(dw_ref)

        keep = 1.0 - (i == nt_ - 1).astype(F32)
        hc = halo_ref[:, W:2 * W] * keep
        hx = halo_ref[:, 2 * W:3 * W] * keep
        ac_ext = jnp.concatenate([hc, p_ref[:, W:2 * W]], axis=0)
        ax_ext = jnp.concatenate([hx, p_ref[:, 2 * W:3 * W]], axis=0)
        _, vjp = jax.vjp(_conv_a_block, p_ref[:, 0:W], ac_ext, ax_ext, p_ref[:, 3 * W:4 * W],
                         w_ref[0:1, :], w_ref[1:2, :], w_ref[2:3, :])
        dab, dac, dax, daz, dw0, dw1, dw2 = vjp(dy_ref[...].astype(F32))
        dp_ref[:, 0:W] = dab.astype(BF16)
        dp_ref[:, W:2 * W] = _add_to_tail(dac[8:], chc_ref[...]).astype(BF16)
        dp_ref[:, 2 * W:3 * W] = _add_to_tail(dax[8:], chx_ref[...]).astype(BF16)
        dp_ref[:, 3 * W:4 * W] = daz.astype(BF16)
        chc_ref[...] = dac[:8] * keep
        chx_ref[...] = dax[:8] * keep
        dw_ref[0:1, :] += dw0
        dw_ref[1:2, :] += dw1
        dw_ref[2:3, :] += dw2

    rev = lambda i: (nt_ - 1 - i, 0)
    return pl.pallas_call(
        body, name=name, grid=(nt_,),
        in_specs=[pl.BlockSpec((blk, NA), rev),
                  pl.BlockSpec((8, NA), lambda i: (jnp.maximum((nt_ - 1 - i) * hb - 1, 0), 0)),
                  pl.BlockSpec((3, W), lambda i: (0, 0)),
                  pl.BlockSpec((blk, W), rev)],
        out_specs=[pl.BlockSpec((blk, NA), rev), pl.BlockSpec((3, W), lambda i: (0, 0))],
        out_shape=[jax.ShapeDtypeStruct((t, NA), BF16), jax.ShapeDtypeStruct((3, W), F32)],
        scratch_shapes=[pltpu.VMEM((8, W), F32), pltpu.VMEM((8, W), F32)],
        compiler_params=_cparams("arbitrary"))(pa, pa, cw, dy)


def _block_rows(t, chunks):
    return min(t, chunks * CHUNK)


def _branch_b_fwd(pb, lbs_row, nw, name, exchange=None):
    t = pb.shape[0]
    rows = _block_rows(t, HGRN_BLOCK_CHUNKS)
    nch = t // rows

    def body(p_ref, lb_ref, nw_ref, y_ref, s_ref, st_ref):
        @pl.when(pl.program_id(1) == 0)
        def _():
            st_ref[...] = jnp.zeros_like(st_ref)

        s_ref[0, 0] = st_ref[...]
        y, st1 = _hgrn_block(p_ref[:, 0:HD], p_ref[:, HD:2 * HD], p_ref[:, 2 * HD:3 * HD], p_ref[:, 3 * HD:4 * HD],
                             st_ref[...], lb_ref[...], nw_ref[...])
        y_ref[...] = y.astype(BF16)
        st_ref[...] = st1

    return _call_with_exchange(
        body, name=name, grid=(HGRN_HEADS, nch),
        in_specs=[pl.BlockSpec((rows, 4 * HD), lambda h, i: (i, h)),
                  pl.BlockSpec((1, HD), lambda h, i: (0, h)),
                  pl.BlockSpec((1, HD), lambda h, i: (0, 0))],
        out_specs=[pl.BlockSpec((rows, HD), lambda h, i: (i, h)),
                   pl.BlockSpec((1, 1, HD, HD), lambda h, i: (h, i, 0, 0))],
        out_shape=[jax.ShapeDtypeStruct((t, HGRN_HEADS * HD), BF16),
                   jax.ShapeDtypeStruct((HGRN_HEADS, nch, HD, HD), F32)],
        scratch_shapes=[pltpu.VMEM((HD, HD), F32)],
        args=(pb, lbs_row, nw), exchange=exchange)


def _branch_b_bwd(pb, states, lbs_row, nw, dy, name, exchange=None):
    t = pb.shape[0]
    rows = _block_rows(t, HGRN_BLOCK_CHUNKS)
    nch = t // rows

    def body(p_ref, s_ref, lb_ref, nw_ref, dy_ref, dp_ref, dlb_ref, dnw_ref, ds_ref):
        h, i = pl.program_id(0), pl.program_id(1)

        @pl.when(i == 0)
        def _():
            ds_ref[...] = jnp.zeros_like(ds_ref)
            dlb_ref[...] = jnp.zeros_like(dlb_ref)

        @pl.when((i == 0) & (h == 0))
        def _():
            dnw_ref[...] = jnp.zeros_like(dnw_ref)

        _, vjp = jax.vjp(_hgrn_block, p_ref[:, 0:HD], p_ref[:, HD:2 * HD], p_ref[:, 2 * HD:3 * HD],
                         p_ref[:, 3 * HD:4 * HD], s_ref[0, 0], lb_ref[...], nw_ref[...])
        dq, df, di, dz, ds0, dlb, dnw = vjp((dy_ref[...].astype(F32), ds_ref[...]))
        dp_ref[:, 0:HD] = dq.astype(BF16)
        dp_ref[:, HD:2 * HD] = df.astype(BF16)
        dp_ref[:, 2 * HD:3 * HD] = di.astype(BF16)
        dp_ref[:, 3 * HD:4 * HD] = dz.astype(BF16)
        ds_ref[...] = ds0
        dlb_ref[...] += dlb
        dnw_ref[...] += dnw

    rev = lambda h, i: (nch - 1 - i, h)
    return _call_with_exchange(
        body, name=name, grid=(HGRN_HEADS, nch),
        in_specs=[pl.BlockSpec((rows, 4 * HD), rev),
                  pl.BlockSpec((1, 1, HD, HD), lambda h, i: (h, nch - 1 - i, 0, 0)),
                  pl.BlockSpec((1, HD), lambda h, i: (0, h)),
                  pl.BlockSpec((1, HD), lambda h, i: (0, 0)),
                  pl.BlockSpec((rows, HD), rev)],
        out_specs=[pl.BlockSpec((rows, 4 * HD), rev),
                   pl.BlockSpec((1, HD), lambda h, i: (0, h)),
                   pl.BlockSpec((1, HD), lambda h, i: (0, 0))],
        out_shape=[jax.ShapeDtypeStruct((t, NB), BF16), jax.ShapeDtypeStruct((1, HGRN_HEADS * HD), F32),
                   jax.ShapeDtypeStruct((1, HD), F32)],
        scratch_shapes=[pltpu.VMEM((HD, HD), F32)],
        args=(pb, states, lbs_row, nw, dy), exchange=exchange)


def _branch_c_fwd(pc, cw, cpar, nw, name, exchange=None):
    t = pc.shape[0]
    rows = _block_rows(t, GDN_BLOCK_CHUNKS)
    nch = t // rows
    XW = 4 * HD

    nsys = 2 * rows // GROUP

    def body(p_ref, w_ref, cp_ref, nw_ref, y_ref, s_ref, x_ref, sa_ref, sb_ref, halo_ref):
        @pl.when(pl.program_id(1) == 0)
        def _():
            sa_ref[...] = jnp.zeros_like(sa_ref)
            sb_ref[...] = jnp.zeros_like(sb_ref)
            halo_ref[...] = jnp.zeros_like(halo_ref)

        s_ref[0, 0, 0] = sa_ref[...]
        s_ref[0, 0, 1] = sb_ref[...]
        x_ext = jnp.concatenate([halo_ref[...], p_ref[:, 0:XW]], axis=0)
        (y, s1a, s1b), xs = _gdn_block(x_ext, p_ref[:, XW:XW + 2 * HD], p_ref[:, XW + 2 * HD:XW + 3 * HD],
                                       sa_ref[...], sb_ref[...], w_ref[0:1, :], w_ref[1:2, :], w_ref[2:3, :], w_ref[3:4, :],
                                       cp_ref[0, 0:1, :], cp_ref[0, 1:2, :], nw_ref[...])
        for n in range(nsys):
            x_ref[0, 0, n] = xs[n]
        y_ref[...] = y.astype(BF16)
        sa_ref[...] = s1a
        sb_ref[...] = s1b
        halo_ref[...] = p_ref[rows - 8:rows, 0:XW]

    return _call_with_exchange(
        body, name=name, grid=(GDN_QK_HEADS, nch),
        in_specs=[pl.BlockSpec((rows, C_HEAD), lambda h, i: (i, h)),
                  pl.BlockSpec((4, XW), lambda h, i: (0, h)),
                  pl.BlockSpec((1, 8, HD), lambda h, i: (h, 0, 0)),
                  pl.BlockSpec((1, HD), lambda h, i: (0, 0))],
        out_specs=[pl.BlockSpec((rows, 2 * HD), lambda h, i: (i, h)),
                   pl.BlockSpec((1, 1, 2, HD, HD), lambda h, i: (h, i, 0, 0, 0)),
                   pl.BlockSpec((1, 1, nsys, GROUP, GROUP), lambda h, i: (h, i, 0, 0, 0))],
        out_shape=[jax.ShapeDtypeStruct((t, 2 * GDN_QK_HEADS * HD), BF16),
                   jax.ShapeDtypeStruct((GDN_QK_HEADS, nch, 2, HD, HD), F32),
                   jax.ShapeDtypeStruct((GDN_QK_HEADS, nch, nsys, GROUP, GROUP), BF16)],
        scratch_shapes=[pltpu.VMEM((HD, HD), F32), pltpu.VMEM((HD, HD), F32), pltpu.VMEM((8, XW), F32)],
        args=(pc, cw, cpar, nw), exchange=exchange)


def _branch_c_bwd(pc, states, inverses, cw, cpar, nw, dy, name, exchange=None):
    t = pc.shape[0]
    rows = _block_rows(t, GDN_BLOCK_CHUNKS)
    nch = t // rows
    XW = 4 * HD
    hb = rows // 8

    nsys = 2 * rows // GROUP

    def body(p_ref, halo_ref, s_ref, x_ref, w_ref, cp_ref, nw_ref, dy_ref, dp_ref, dw_ref, dcp_ref, dnw_ref,
             dsa_ref, dsb_ref, carry_ref):
        h, i = pl.program_id(0), pl.program_id(1)

        @pl.when(i == 0)
        def _():
            dsa_ref[...] = jnp.zeros_like(dsa_ref)
            dsb_ref[...] = jnp.zeros_like(dsb_ref)
            carry_ref[...] = jnp.zeros_like(carry_ref)
            dw_ref[...] = jnp.zeros_like(dw_ref)
            dcp_ref[...] = jnp.zeros_like(dcp_ref)

        @pl.when((i == 0) & (h == 0))
        def _():
            dnw_ref[...] = jnp.zeros_like(dnw_ref)

        keep = 1.0 - (i == nch - 1).astype(F32)
        x_ext = jnp.concatenate([halo_ref[:, 0:XW] * keep, p_ref[:, 0:XW]], axis=0)
        block = functools.partial(_gdn_block, known=tuple(x_ref[0, 0, n] for n in range(nsys)))
        _, vjp, _ = jax.vjp(block, x_ext, p_ref[:, XW:XW + 2 * HD], p_ref[:, XW + 2 * HD:XW + 3 * HD],
                            s_ref[0, 0, 0], s_ref[0, 0, 1], w_ref[0:1, :], w_ref[1:2, :], w_ref[2:3, :], w_ref[3:4, :],
                            cp_ref[0, 0:1, :], cp_ref[0, 1:2, :], nw_ref[...], has_aux=True)
        dx, dz, dba, dsa, dsb, dw0, dw1, dw2, dw3, dal, ddt, dnw = vjp((dy_ref[...].astype(F32), dsa_ref[...], dsb_ref[...]))
        dp_ref[:, 0:XW] = _add_to_tail(dx[8:], carry_ref[...]).astype(BF16)
        dp_ref[:, XW:XW + 2 * HD] = dz.astype(BF16)
        dp_ref[:, XW + 2 * HD:XW + 3 * HD] = dba.astype(BF16)
        carry_ref[...] = dx[:8] * keep
        dsa_ref[...] = dsa
        dsb_ref[...] = dsb
        dw_ref[0:1, :] += dw0
        dw_ref[1:2, :] += dw1
        dw_ref[2:3, :] += dw2
        dw_ref[3:4, :] += dw3
        dcp_ref[0, 0:1, :] += dal
        dcp_ref[0, 1:2, :] += ddt
        dnw_ref[...] += dnw

    rev = lambda h, i: (nch - 1 - i, h)
    return _call_with_exchange(
        body, name=name, grid=(GDN_QK_HEADS, nch),
        in_specs=[pl.BlockSpec((rows, C_HEAD), rev),
                  pl.BlockSpec((8, C_HEAD), lambda h, i: (jnp.maximum((nch - 1 - i) * hb - 1, 0), h)),
                  pl.BlockSpec((1, 1, 2, HD, HD), lambda h, i: (h, nch - 1 - i, 0, 0, 0)),
                  pl.BlockSpec((1, 1, nsys, GROUP, GROUP), lambda h, i: (h, nch - 1 - i, 0, 0, 0)),
                  pl.BlockSpec((4, XW), lambda h, i: (0, h)),
                  pl.BlockSpec((1, 8, HD), lambda h, i: (h, 0, 0)),
                  pl.BlockSpec((1, HD), lambda h, i: (0, 0)),
                  pl.BlockSpec((rows, 2 * HD), rev)],
        out_specs=[pl.BlockSpec((rows, C_HEAD), rev),
                   pl.BlockSpec((4, XW), lambda h, i: (0, h)),
                   pl.BlockSpec((1, 8, HD), lambda h, i: (h, 0, 0)),
                   pl.BlockSpec((1, HD), lambda h, i: (0, 0))],
        out_shape=[jax.ShapeDtypeStruct((t, NC_COLS), BF16), jax.ShapeDtypeStruct((4, GDN_QK_HEADS * XW), F32),
                   jax.ShapeDtypeStruct((GDN_QK_HEADS, 8, HD), F32), jax.ShapeDtypeStruct((1, HD), F32)],
        scratch_shapes=[pltpu.VMEM((HD, HD), F32), pltpu.VMEM((HD, HD), F32), pltpu.VMEM((8, XW), F32)],
        args=(pc, pc, states, inverses, cw, cpar, nw, dy), exchange=exchange)


def _merge_fwd(pg, bg, ya, yb, yc, name):
    t = pg.shape[0]
    blk = _tile(t, 512)

    def body(g_ref, b_ref, a_ref, b2_ref, c_ref, o_ref):
        gate = _sigmoid(g_ref[...] + b_ref[...])
        o_ref[...] = (gate[:, 0:D] * a_ref[...] + gate[:, D:2 * D] * b2_ref[...] + gate[:, 2 * D:3 * D] * c_ref[...]).astype(BF16)

    row = pl.BlockSpec((blk, D), lambda i: (i, 0))
    return pl.pallas_call(
        body, name=name, grid=(t // blk,),
        in_specs=[pl.BlockSpec((blk, NG), lambda i: (i, 0)), pl.BlockSpec((1, NG), lambda i: (0, 0)), row, row, row],
        out_specs=row, out_shape=jax.ShapeDtypeStruct((t, D), BF16),
        compiler_params=_cparams("parallel"))(pg, bg, ya, yb, yc)


def _merge_bwd(dm, pg, bg, ya, yb, yc, name):
    t = pg.shape[0]
    blk = _tile(t, 512)

    def body(dm_ref, g_ref, b_ref, a_ref, b2_ref, c_ref, dg_ref, da_ref, db_ref, dc_ref, dbg_ref):
        @pl.when(pl.program_id(0) == 0)
        def _():
            dbg_ref[...] = jnp.zeros_like(dbg_ref)

        gate = _sigmoid(g_ref[...] + b_ref[...])
        dmv = dm_ref[...].astype(F32)
        for j, (y_ref, dy_ref) in enumerate(((a_ref, da_ref), (b2_ref, db_ref), (c_ref, dc_ref))):
            gj = gate[:, j * D:(j + 1) * D]
            dy_ref[...] = (dmv * gj).astype(BF16)
            dgj = dmv * y_ref[...] * gj * (1.0 - gj)
            dg_ref[:, j * D:(j + 1) * D] = dgj.astype(BF16)
            dbg_ref[:, j * D:(j + 1) * D] += jnp.sum(dgj, axis=0, keepdims=True)

    row = pl.BlockSpec((blk, D), lambda i: (i, 0))
    wide = pl.BlockSpec((blk, NG), lambda i: (i, 0))
    vec = pl.BlockSpec((1, NG), lambda i: (0, 0))
    return pl.pallas_call(
        body, name=name, grid=(t // blk,), in_specs=[row, wide, vec, row, row, row],
        out_specs=[wide, row, row, row, vec],
        out_shape=[jax.ShapeDtypeStruct((t, NG), BF16)] + [jax.ShapeDtypeStruct((t, D), BF16)] * 3
                  + [jax.ShapeDtypeStruct((1, NG), F32)],
        compiler_params=_cparams("arbitrary"))(dm, pg, bg, ya, yb, yc)


def _adamw_math(w, g, m, v):
    m = ADAM_B1 * m + (1.0 - ADAM_B1) * g
    v = ADAM_B2 * v + (1.0 - ADAM_B2) * (g * g)
    m_hat = m / (1.0 - ADAM_B1 ** ADAM_STEP)
    v_hat = v / (1.0 - ADAM_B2 ** ADAM_STEP)
    delta = -ADAM_LR * (m_hat / (jnp.sqrt(v_hat) + ADAM_EPS) + ADAM_WD * w)
    return delta, m, v


def _sum_adamw(parts, w, m, v, name):
    layers = len(parts)
    r, c = parts[0].shape[1:]
    br = r if r <= 256 else 256
    nb = r // br
    assert r % br == 0 and w.shape == (layers * r, c)

    def body(*refs):
        w_ref, m_ref, v_ref, g_ref, d_ref, nm_ref, nv_ref = refs[layers:]
        for l in range(layers):
            @pl.when(pl.program_id(0) == l)
            def _(p_ref=refs[l]):
                g = p_ref[0].astype(F32)
                for k in range(1, N_DEV):
                    g = g + p_ref[k].astype(F32)
                g_ref[...] = g
                d_ref[...], nm_ref[...], nv_ref[...] = _adamw_math(w_ref[...], g, m_ref[...], v_ref[...])

    blk = pl.BlockSpec((br, c), lambda l, i: (l * nb + i, 0))
    part_specs = [pl.BlockSpec((N_DEV, br, c), lambda l, i, q=q: (0, jnp.where(l == q, i, jnp.where(l < q, 0, nb - 1)), 0))
                  for q in range(layers)]
    return pl.pallas_call(
        body, name=name, grid=(layers, nb), in_specs=part_specs + [blk, blk, blk], out_specs=[blk] * 4,
        out_shape=[jax.ShapeDtypeStruct((layers * r, c), F32)] * 4,
        compiler_params=_cparams("arbitrary", "arbitrary"))(*parts, w, m, v)


def _adamw(g, w, m, v, name):
    def body(g_ref, w_ref, m_ref, v_ref, d_ref, nm_ref, nv_ref):
        d_ref[...], nm_ref[...], nv_ref[...] = _adamw_math(w_ref[...], g_ref[...], m_ref[...], v_ref[...])

    return pl.pallas_call(body, name=name, out_shape=[jax.ShapeDtypeStruct(w.shape, F32)] * 3)(g, w, m, v)


def _sum_slots(parts, name):
    def body(p_ref, o_ref):
        g = p_ref[0]
        for k in range(1, N_DEV):
            g = g + p_ref[k]
        o_ref[...] = g

    return pl.pallas_call(body, name=name, out_shape=jax.ShapeDtypeStruct(parts.shape[1:], F32))(parts)


def _exchange(srcs, name, broadcast):
    n = len(srcs)

    def body(*refs):
        copies = _exchange_copies(refs[:n], refs[n:2 * n], *refs[2 * n:], broadcast)
        for cp in copies:
            cp.start()
        for cp in copies:
            cp.wait()

    return pl.pallas_call(
        body, name=name, in_specs=[HBM_SPEC] * n, out_specs=[HBM_SPEC] * n, out_shape=_exchange_shapes(srcs, broadcast),
        scratch_shapes=_exchange_semaphores(n))(*srcs)


def _gather_two_level(srcs, name):
    n = len(srcs)

    def body(*refs):
        src_refs, dst_refs = refs[:n], refs[n:2 * n]
        send_sems, recv_sems, local_sems = refs[2 * n:]
        x, y, c = lax.axis_index("x"), lax.axis_index("y"), lax.axis_index("c")
        index_of = lambda px, py, pc: 4 * px + 2 * py + pc
        me, other_core = index_of(x, y, c), (x, y, 1 - c)
        chips = [(1 - x, y), (x, 1 - y), (1 - x, 1 - y)]

        def copy(k, a, block, to, src=None):
            return pltpu.make_async_remote_copy(
                src_ref=dst_refs[a].at[block] if src is None else src, dst_ref=dst_refs[a].at[block],
                send_sem=send_sems.at[k, a], recv_sem=recv_sems.at[k, a], device_id=to, device_id_type=MESH)

        local = [pltpu.make_async_copy(src_refs[a], dst_refs[a].at[me], local_sems.at[a]) for a in range(n)]
        first = [copy(0, a, me, other_core, src=src_refs[a]) for a in range(n)]
        first += [copy(1 + j, a, me, (*chip, c), src=src_refs[a]) for j, chip in enumerate(chips) for a in range(n)]
        for cp in local + first:
            cp.start()
        passed = []
        for j, chip in enumerate(chips):
            block = index_of(*chip, c)
            for a in range(n):
                copy(1 + j, a, block, (x, y, c)).wait_recv()
            for a in range(n):
                passed.append(copy(4 + j, a, block, other_core))
                passed[-1].start()
        for a in range(n):
            copy(0, a, index_of(x, y, 1 - c), (x, y, c)).wait_recv()
        for j, chip in enumerate(chips):
            for a in range(n):
                copy(4 + j, a, index_of(*chip, 1 - c), (x, y, c)).wait_recv()
        for cp in first + passed:
            cp.wait_send()
        for cp in local:
            cp.wait()

    return pl.pallas_call(
        body, name=name, in_specs=[HBM_SPEC] * n, out_specs=[HBM_SPEC] * n, out_shape=_exchange_shapes(srcs, True),
        scratch_shapes=_exchange_semaphores(n))(*srcs)


HBM_SPEC = pl.BlockSpec(memory_space=pltpu.HBM)


def _exchange_shapes(srcs, broadcast):
    return [jax.ShapeDtypeStruct((N_DEV,) + (s.shape if broadcast else s.shape[1:]), s.dtype) for s in srcs]


def _exchange_semaphores(n):
    return [pltpu.SemaphoreType.DMA((N_DEV - 1, n)), pltpu.SemaphoreType.DMA((N_DEV - 1, n)), pltpu.SemaphoreType.DMA((n,))]


def _exchange_copies(src_refs, dst_refs, send_sems, recv_sems, local_sems, broadcast):
    x, y, c = lax.axis_index("x"), lax.axis_index("y"), lax.axis_index("c")
    me = 4 * x + 2 * y + c
    copies = []
    for k in range(1, N_DEV):
        px = 1 - x if (k >> 2) & 1 else x
        py = 1 - y if (k >> 1) & 1 else y
        pc = 1 - c if k & 1 else c
        peer = 4 * px + 2 * py + pc
        for a, (src, dst) in enumerate(zip(src_refs, dst_refs)):
            copies.append(pltpu.make_async_remote_copy(
                src_ref=src if broadcast else src.at[peer], dst_ref=dst.at[me],
                send_sem=send_sems.at[k - 1, a], recv_sem=recv_sems.at[k - 1, a],
                device_id=(px, py, pc), device_id_type=MESH))
    for a, (src, dst) in enumerate(zip(src_refs, dst_refs)):
        copies.append(pltpu.make_async_copy(src if broadcast else src.at[me], dst.at[me], local_sems.at[a]))
    return copies


def _call_with_exchange(body, *, name, grid, in_specs, out_specs, out_shape, scratch_shapes, args, exchange):
    if exchange is None:
        outs = pl.pallas_call(body, name=name, grid=grid, in_specs=in_specs, out_specs=out_specs, out_shape=out_shape,
                              scratch_shapes=scratch_shapes,
                              compiler_params=_cparams(*["arbitrary"] * len(grid)))(*args)
        return outs, None
    srcs, broadcast = exchange
    n, n_in, n_out, n_scr = len(srcs), len(args), len(out_shape), len(scratch_shapes)
    steps = 1
    for g in grid:
        steps *= g

    def hosted(*refs):
        ins, src_refs = refs[:n_in], refs[n_in:n_in + n]
        outs, dst_refs = refs[n_in + n:n_in + n + n_out], refs[n_in + n + n_out:n_in + 2 * n + n_out]
        scratch = refs[n_in + 2 * n + n_out:]
        step = pl.program_id(0)
        for axis in range(1, len(grid)):
            step = step * grid[axis] + pl.program_id(axis)

        @pl.when(step == 0)
        def _():
            for cp in _exchange_copies(src_refs, dst_refs, *scratch[n_scr:], broadcast):
                cp.start()

        body(*ins, *outs, *scratch[:n_scr])

        @pl.when(step == steps - 1)
        def _():
            for cp in _exchange_copies(src_refs, dst_refs, *scratch[n_scr:], broadcast):
                cp.wait()

    outs = pl.pallas_call(
        hosted, name=name, grid=grid, in_specs=list(in_specs) + [HBM_SPEC] * n, out_specs=list(out_specs) + [HBM_SPEC] * n,
        out_shape=list(out_shape) + _exchange_shapes(srcs, broadcast),
        scratch_shapes=list(scratch_shapes) + _exchange_semaphores(n),
        compiler_params=_cparams(*["arbitrary"] * len(grid)))(*args, *srcs)
    return outs[:n_out], outs[n_out:]


def _regroup_w_in(w):
    wa = w[:, OFF_A:OFF_A + NA]
    seg = lambda off, h, n=HD: w[:, off + h * n: off + (h + 1) * n]
    wb = jnp.concatenate([seg(OFF_B + s * 512, h) for h in range(HGRN_HEADS) for s in range(4)], axis=1)
    parts = []
    for h in range(GDN_QK_HEADS):
        small = jnp.concatenate(
            [w[:, OFF_BETA + 2 * h: OFF_BETA + 2 * h + 2], w[:, OFF_CA + 2 * h: OFF_CA + 2 * h + 2],
             jnp.zeros((w.shape[0], HD - 4), w.dtype)], axis=1)
        parts += [seg(OFF_CQ, h), seg(OFF_CK, h), seg(OFF_CV, h, 2 * HD), seg(OFF_CZ, h, 2 * HD), small]
    wc = jnp.concatenate(parts, axis=1)
    wg = w[:, OFF_G:OFF_G + NG]
    return wa, wb, wc, wg


def _ungroup_dw_in(da, db, dc, dg):
    bq = [jnp.concatenate([db[:, h * 512 + s * HD: h * 512 + (s + 1) * HD] for h in range(HGRN_HEADS)], axis=1)
          for s in range(4)]
    ch = lambda h, lo, hi: dc[:, h * C_HEAD + lo: h * C_HEAD + hi]
    heads = range(GDN_QK_HEADS)
    cq = jnp.concatenate([ch(h, 0, HD) for h in heads], axis=1)
    ck = jnp.concatenate([ch(h, HD, 2 * HD) for h in heads], axis=1)
    cv = jnp.concatenate([ch(h, 2 * HD, 4 * HD) for h in heads], axis=1)
    cz = jnp.concatenate([ch(h, 4 * HD, 6 * HD) for h in heads], axis=1)
    cbeta = jnp.concatenate([ch(h, 6 * HD, 6 * HD + 2) for h in heads], axis=1)
    ca = jnp.concatenate([ch(h, 6 * HD + 2, 6 * HD + 4) for h in heads], axis=1)
    return jnp.concatenate([da] + bq + [cq, ck, cv, cbeta, ca, cz, dg], axis=1)


def _regroup_conv_c(cw):
    parts = []
    for h in range(GDN_QK_HEADS):
        parts += [cw[:, h * HD:(h + 1) * HD], cw[:, 512 + h * HD: 512 + (h + 1) * HD],
                  cw[:, 1024 + 2 * h * HD: 1024 + (2 * h + 2) * HD]]
    return jnp.concatenate(parts, axis=1)


def _ungroup_conv_c(d):
    heads = range(GDN_QK_HEADS)
    q = jnp.concatenate([d[:, h * 512: h * 512 + HD] for h in heads], axis=1)
    k = jnp.concatenate([d[:, h * 512 + HD: h * 512 + 2 * HD] for h in heads], axis=1)
    v = jnp.concatenate([d[:, h * 512 + 2 * HD: h * 512 + 4 * HD] for h in heads], axis=1)
    return jnp.concatenate([q, k, v], axis=1)


def _numel(shape):
    n = 1
    for d in shape:
        n *= d
    return n


def _pack(arrays, rows):
    flat = jnp.concatenate([a.reshape(-1) for a in arrays])
    return jnp.pad(flat, (0, rows * 128 - flat.shape[0])).reshape(rows, 128)


def _unpack(packed, shapes):
    flat = packed.reshape(-1)
    out, off = [], 0
    for s in shapes:
        out.append(flat[off:off + _numel(s)].reshape(s))
        off += _numel(s)
    return out


def _rows_for(shapes):
    return -(-sum(_numel(s) for s in shapes) // 1024) * 8


def kernel(x, norm_w, w_in, b_gate, conv_a, conv_c, a_log, dt_bias, lower_bounds, hgrn_norm_w, gdn_norm_w, w_out_a, w_out_b, w_out_c, w_o, final_norm_w, loss_target, m_norm_w, m_w_in, m_b_gate, m_conv_a, m_conv_c, m_a_log, m_dt_bias, m_lower_bounds, m_hgrn_norm_w, m_gdn_norm_w, m_w_out_a, m_w_out_b, m_w_out_c, m_w_o, m_final_norm_w, v_norm_w, v_w_in, v_b_gate, v_conv_a, v_conv_c, v_a_log, v_dt_bias, v_lower_bounds, v_hgrn_norm_w, v_gdn_norm_w, v_w_out_a, v_w_out_b, v_w_out_c, v_w_o, v_final_norm_w):
    me = 4 * lax.axis_index("x") + 2 * lax.axis_index("y") + lax.axis_index("c")
    xs = x[0]
    target = loss_target[0]
    in_shard = w_in.shape[2]

    big = [w_in, w_out_a, w_out_b, w_out_c, w_o]
    shards_of = lambda l: [w[l].astype(BF16) for w in big]
    conv_shapes = [(DEPTH, 3, CONV_W), (DEPTH, 4, 2048)]
    conv_rows = _rows_for(conv_shapes)
    ca_full = lax.dynamic_update_slice(jnp.zeros(conv_shapes[0], F32), conv_a, (0, 0, me * conv_a.shape[2]))
    cc_full = lax.dynamic_update_slice(jnp.zeros(conv_shapes[1], F32), conv_c, (0, 0, me * conv_c.shape[2]))
    g_in0, conv_parts = _gather_two_level([w_in[0].astype(BF16), _pack([ca_full, cc_full], conv_rows)], "gather_l0")
    conv_a_full, conv_c_full = _unpack(_sum_slots(conv_parts, "sum_conv"), conv_shapes)

    lb_pad = jnp.pad(lower_bounds, ((0, 8 - DEPTH), (0, 0)))
    lbs = _lower_bounds_fwd(lb_pad, "lower_bounds_fwd")

    def input_weights(l, g_in):
        wa, wb, wc, wg = _regroup_w_in(jnp.concatenate([g_in[q] for q in range(N_DEV)], axis=1))
        lanes = lambda vec: jnp.pad(vec.reshape(GDN_QK_HEADS, 1, 2), ((0, 0), (0, 0), (0, HD - 2)))
        cpar = jnp.concatenate([lanes(a_log[l]), lanes(dt_bias[l]), jnp.zeros((GDN_QK_HEADS, 6, HD), F32)], axis=1)
        return dict(
            wa=wa, wb=wb, wc=wc, wg=wg, cpar=cpar,
            nw=norm_w[l:l + 1], bg=b_gate[l:l + 1], cwa=conv_a_full[l], cwc=_regroup_conv_c(conv_c_full[l]),
            lb=lbs[l:l + 1], hnw=hgrn_norm_w[l:l + 1], gnw=gdn_norm_w[l:l + 1])

    def output_weights(g_oa, g_ob, g_oc, g_o):
        return dict(woa=jnp.concatenate([g_oa[q] for q in range(N_DEV)], axis=1),
                    wob=jnp.concatenate([g_ob[q] for q in range(N_DEV)], axis=1), woc=g_oc.reshape(D, D), wo=g_o.reshape(D, D))

    layers = [input_weights(0, g_in0)]

    saved = []
    cur = xs
    for l in range(DEPTH):
        L = layers[l]
        n = f"l{l}_"
        h = _rmsnorm_fwd(cur, L["nw"], n + "rms")
        pa = _matmul(h, L["wa"], "nn", n + "proj_a")
        pb = _matmul(h, L["wb"], "nn", n + "proj_b")
        pc = _matmul(h, L["wc"], "nn", n + "proj_c")
        pg = _matmul(h, L["wg"], "nn", n + "proj_g", out_dtype=BF16)
        ua = _branch_a_fwd(pa, L["cwa"], n + "conv_fwd")
        carry = (shards_of(l)[1:], True) if l == 0 else None
        (ub, sb), gathered = _branch_b_fwd(pb, L["lb"], L["hnw"], n + "hgrn_fwd", exchange=carry)
        if carry is not None:
            L.update(output_weights(*gathered))
        carry = (shards_of(l + 1), True) if l + 1 < DEPTH else None
        (uc, sc, xc), gathered = _branch_c_fwd(pc, L["cwc"], L["cpar"], L["gnw"], n + "gdn_fwd", exchange=carry)
        if carry is not None:
            layers.append(dict(input_weights(l + 1, gathered[0]), **output_weights(*gathered[1:])))
        ya = _matmul(ua, L["woa"], "nn", n + "out_a", out_dtype=BF16)
        yb = _matmul(ub, L["wob"], "nn", n + "out_b", out_dtype=BF16)
        yc = _matmul(uc, L["woc"], "nn", n + "out_c", out_dtype=BF16)
        merged = _merge_fwd(pg, L["bg"], ya, yb, yc, n + "merge")
        nxt = _matmul(merged, L["wo"], "nn", n + "out_o", residual=cur)
        saved.append(dict(x=cur, h=h, pa=pa, pb=pb, pc=pc, pg=pg, ua=ua, ub=ub, uc=uc, sb=sb, sc=sc, xc=xc,
                          ya=ya, yb=yb, yc=yc, merged=merged))
        cur = nxt

    loss_part, dx, d_final = _loss_head(cur, final_norm_w.reshape(1, D), target, "loss_head")

    def outgoing(g):
        cols = lambda a, n: jnp.stack([a[:, p * n:(p + 1) * n] for p in range(N_DEV)]).astype(BF16)
        rows = lambda a: a.reshape(N_DEV, a.shape[0] // N_DEV, a.shape[1]).astype(BF16)
        first = [cols(g["w_in"], in_shard)] if "w_in" in g else [None]
        if "w_o" not in g:
            return first
        return first + [cols(g["w_out_a"], 128), cols(g["w_out_b"], 128), rows(g["w_out_c"]), rows(g["w_o"])]

    grads = [None] * DEPTH
    dlbs_rows = [None] * DEPTH
    incoming = [None] * DEPTH
    for l in reversed(range(DEPTH)):
        L, S = layers[l], saved[l]
        n = f"l{l}_"
        dmerged = _matmul(dx, L["wo"], "nt", n + "d_merged", out_dtype=BF16)
        d_wo = _matmul(S["merged"], dx, "tn", n + "dw_o", out_dtype=BF16)
        dpg, dya, dyb, dyc, d_bg = _merge_bwd(dmerged, S["pg"], L["bg"], S["ya"], S["yb"], S["yc"], n + "merge_bwd")
        dua = _matmul(dya, L["woa"], "nt", n + "d_ua", out_dtype=BF16)
        dub = _matmul(dyb, L["wob"], "nt", n + "d_ub", out_dtype=BF16)
        duc = _matmul(dyc, L["woc"], "nt", n + "d_uc", out_dtype=BF16)
        d_woa = _matmul(S["ua"], dya, "tn", n + "dw_out_a", out_dtype=BF16)
        d_wob = _matmul(S["ub"], dyb, "tn", n + "dw_out_b", out_dtype=BF16)
        d_woc = _matmul(S["uc"], dyc, "tn", n + "dw_out_c", out_dtype=BF16)
        dpa, d_cwa = _branch_a_bwd(S["pa"], L["cwa"], dua, n + "conv_bwd")
        out_grads = dict(w_out_a=d_woa, w_out_b=d_wob, w_out_c=d_woc, w_o=d_wo)
        carry = (outgoing(out_grads)[1:], False) if l == 0 else None
        (dpb, d_lb, d_hnw), arrived_out = _branch_b_bwd(S["pb"], S["sb"], L["lb"], L["hnw"], dub, n + "hgrn_bwd", exchange=carry)
        carry = (outgoing(grads[l + 1]), False) if l + 1 < DEPTH else None
        (dpc, d_cwc, d_cpar, d_gnw), arrived = _branch_c_bwd(S["pc"], S["sc"], S["xc"], L["cwc"], L["cpar"], L["gnw"], duc,
                                                             n + "gdn_bwd", exchange=carry)
        if carry is not None:
            incoming[l + 1] = arrived
        d_win = _ungroup_dw_in(*[_matmul(S["h"], dp, "tn", f"{n}dw_{piece}", out_dtype=BF16)
                                 for dp, piece in ((dpa, "a"), (dpb, "b"), (dpc, "c"), (dpg, "g"))])
        carry = (outgoing(dict(w_in=d_win)), False) if l == 0 else None
        dh, arrived_in = _matmul_nt_sum([(dpa, L["wa"]), (dpb, L["wb"]), (dpc, L["wc"]), (dpg, L["wg"])], n + "dh",
                                        exchange=carry)
        (dx, d_nw), _ = _rmsnorm_bwd(dh, S["x"], L["nw"], dx, n + "rms_bwd")
        dlbs_rows[l] = d_lb
        grads[l] = dict(w_in=d_win, w_out_a=d_woa, w_out_b=d_wob, w_out_c=d_woc, w_o=d_wo, norm_w=d_nw[0],
                        b_gate=d_bg[0], conv_a=d_cwa, conv_c=_ungroup_conv_c(d_cwc),
                        a_log=d_cpar[:, 0, 0:2].reshape(-1), dt_bias=d_cpar[:, 1, 0:2].reshape(-1),
                        hgrn_norm_w=d_hnw[0], gdn_norm_w=d_gnw[0])
    grad_x = dx[None]
    d_lower = _lower_bounds_bwd(lb_pad, jnp.pad(jnp.concatenate(dlbs_rows, axis=0), ((0, 8 - DEPTH), (0, 0))),
                                "lower_bounds_bwd")[:DEPTH]

    incoming[0] = list(arrived_in) + list(arrived_out)
    stack = lambda name: jnp.stack([grads[l][name] for l in range(DEPTH)])
    big_out = {}
    for j, (name, w, m, v) in enumerate((("w_in", w_in, m_w_in, v_w_in), ("w_out_a", w_out_a, m_w_out_a, v_w_out_a),
                                         ("w_out_b", w_out_b, m_w_out_b, v_w_out_b), ("w_out_c", w_out_c, m_w_out_c, v_w_out_c),
                                         ("w_o", w_o, m_w_o, v_w_o))):
        parts = [incoming[l][j] for l in range(DEPTH)]
        r2 = lambda a: a.reshape(DEPTH * parts[0].shape[1], parts[0].shape[2])
        outs = _sum_adamw(parts, r2(w), r2(m), r2(v), "adamw_" + name)
        big_out[name] = [o.reshape(w.shape) for o in outs]

    small_names = ["norm_w", "b_gate", "conv_a", "conv_c", "a_log", "dt_bias", "lower_bounds", "hgrn_norm_w",
                   "gdn_norm_w", "final_norm_w", "loss"]
    small_vals = {k: stack(k) for k in ("norm_w", "b_gate", "conv_a", "conv_c", "a_log", "dt_bias", "hgrn_norm_w", "gdn_norm_w")}
    small_vals.update(lower_bounds=d_lower, final_norm_w=d_final[0], loss=loss_part.reshape(1))
    small_shapes = [small_vals[k].shape for k in small_names]
    small_rows = _rows_for(small_shapes)
    small_parts, = _exchange([_pack([small_vals[k] for k in small_names], small_rows)], "exchange_small", broadcast=True)
    total = dict(zip(small_names, _unpack(_sum_slots(small_parts, "sum_small"), small_shapes)))
    loss = total["loss"][0]
    g_conv_a = lax.dynamic_slice(total["conv_a"], (0, 0, me * conv_a.shape[2]), conv_a.shape)
    g_conv_c = lax.dynamic_slice(total["conv_c"], (0, 0, me * conv_c.shape[2]), conv_c.shape)

    small_w = dict(norm_w=(norm_w, m_norm_w, v_norm_w), b_gate=(b_gate, m_b_gate, v_b_gate),
                   conv_a=(conv_a, m_conv_a, v_conv_a), conv_c=(conv_c, m_conv_c, v_conv_c),
                   a_log=(a_log, m_a_log, v_a_log), dt_bias=(dt_bias, m_dt_bias, v_dt_bias),
                   lower_bounds=(lower_bounds, m_lower_bounds, v_lower_bounds),
                   hgrn_norm_w=(hgrn_norm_w, m_hgrn_norm_w, v_hgrn_norm_w), gdn_norm_w=(gdn_norm_w, m_gdn_norm_w, v_gdn_norm_w),
                   final_norm_w=(final_norm_w, m_final_norm_w, v_final_norm_w))
    small_g = dict(total, conv_a=g_conv_a, conv_c=g_conv_c)
    upd_names = small_names[:-1]
    upd_shapes = [small_w[k][0].shape for k in upd_names]
    upd_rows = _rows_for(upd_shapes)
    pk = lambda j: _pack([small_w[k][j] for k in upd_names], upd_rows)
    s_delta, s_m, s_v = _adamw(_pack([small_g[k] for k in upd_names], upd_rows), pk(0), pk(1), pk(2), "adamw_small")
    small_out = {k: [small_g[k], d, mm, vv] for k, d, mm, vv in
                 zip(upd_names, _unpack(s_delta, upd_shapes), _unpack(s_m, upd_shapes), _unpack(s_v, upd_shapes))}

    order = ["norm_w", "w_in", "b_gate", "conv_a", "conv_c", "a_log", "dt_bias", "lower_bounds", "hgrn_norm_w",
             "gdn_norm_w", "w_out_a", "w_out_b", "w_out_c", "w_o", "final_norm_w"]
    res = {**small_out, **big_out}
    outs = [loss, grad_x]
    for j in range(4):
        outs += [res[k][j] for k in order]
    return tuple(outs)
```

```python
import functools

import jax
import jax.numpy as jnp
from jax import lax
from jax.experimental import pallas as pl
from jax.experimental.pallas import tpu as pltpu

F32 = jnp.float32
BF16 = jnp.bfloat16
MESH = pl.DeviceIdType.MESH

N_DEV = 8
D = 1024
DEPTH = 2
CHUNK = 64
HGRN_BLOCK_CHUNKS = 16
GDN_BLOCK_CHUNKS = 8
GDN_FORWARD_CHUNKS = 16
GROUP = 128
NORM_EPS = 1e-6
L2_EPS = 1e-6
MIN_F = 1e-30
HD = 128
HGRN_HEADS = 4
GDN_QK_HEADS = 4
CONV_W = 512
IN_COLS = 10256
OFF_A, OFF_B, OFF_CQ, OFF_CK, OFF_CV, OFF_BETA, OFF_CA, OFF_CZ, OFF_G = (
    0, 2048, 4096, 4608, 5120, 6144, 6152, 6160, 7184)
NA, NB, NC_COLS, NG = 2048, 2048, 3584, 3072
C_HEAD = 896

ADAM_LR, ADAM_B1, ADAM_B2, ADAM_EPS, ADAM_WD, ADAM_STEP = 0.001, 0.9, 0.999, 1e-08, 0.01, 10

VMEM_LIMIT = 56 * 1024 * 1024
MM_TILE = 1024


def _cparams(*sem):
    return pltpu.CompilerParams(dimension_semantics=sem, vmem_limit_bytes=VMEM_LIMIT)


def _tile(dim, cap):
    if dim <= cap:
        return dim
    t = (cap // 128) * 128
    while dim % t:
        t -= 128
    return t


def _sigmoid(x):
    return 1.0 / (1.0 + jnp.exp(-x))


def _silu(x):
    return x * _sigmoid(x)


def _softplus(x):
    return jnp.maximum(x, 0.0) + jnp.log(1.0 + jnp.exp(-jnp.abs(x)))


def _dot(a, b, dims, precision=None):
    if precision is None:
        a, b = a.astype(BF16), b.astype(BF16)
    return lax.dot_general(a, b, (dims, ((), ())), precision=precision, preferred_element_type=F32)


def _nn(a, b, precision=None):
    return _dot(a, b, ((1,), (0,)), precision)


def _nt(a, b, precision=None):
    return _dot(a, b, ((1,), (1,)), precision)


def _tn(a, b, precision=None):
    return _dot(a, b, ((0,), (0,)), precision)


def _sum_rows_split(mat01, x):
    m = mat01.astype(BF16)
    hi = x.astype(BF16)
    low = (x - hi.astype(F32)).astype(BF16)
    return _nn(m, hi) + _nn(m, low)


@functools.partial(jax.custom_vjp, nondiff_argnums=(1,))
def _shift_rows(x, d):
    return x if d == 0 else pltpu.roll(x, d, 0)


def _shift_rows_fwd(x, d):
    return _shift_rows(x, d), None


def _shift_rows_bwd(d, _, ct):
    return ((ct if d == 0 else pltpu.roll(ct, ct.shape[0] - d, 0)),)


_shift_rows.defvjp(_shift_rows_fwd, _shift_rows_bwd)


def _iota2(shape):
    return lax.broadcasted_iota(jnp.int32, shape, 0), lax.broadcasted_iota(jnp.int32, shape, 1)


def _lane_pick(x, i):
    lane = lax.broadcasted_iota(jnp.int32, x.shape, 1)
    return jnp.sum(jnp.where(lane == i, x, 0.0), axis=1, keepdims=True)


def _hgrn_block(qr, fr, ir, zr, st0, lb, nw):
    rows = qr.shape[0]
    r, c = _iota2((CHUNK, CHUNK))
    halves = [1 << j for j in range(CHUNK.bit_length() - 1)]
    mats = [c <= r, c > r]
    pairs = []
    for hb in halves:
        same = (r // hb) == (c // hb)
        if hb > 1:
            mats += [(c <= r) & same, (c > r) & same]
        pairs.append(((r // (2 * hb)) == (c // (2 * hb))) & ((r // hb) == (c // hb) + 1))
    stack = jnp.concatenate([m.astype(F32) for m in mats], axis=0)

    q = _silu(qr) * (HD ** -0.5)
    fg = lb + (1.0 - lb) * _sigmoid(fr)
    logf = jnp.log(jnp.maximum(fg, MIN_F))
    kk = 1.0 - fg
    v = ir

    chunks = [slice(s, s + CHUNK) for s in range(0, rows, CHUNK)]
    cums = [_sum_rows_split(stack, logf[sl]) for sl in chunks]
    part = lambda i: jnp.concatenate([cs[i * CHUNK:(i + 1) * CHUNK] for cs in cums], axis=0)
    qg = q * jnp.exp(part(0))
    ks = kk * jnp.exp(part(1))
    q_lv = [q * jnp.exp(logf)] + [q * jnp.exp(part(2 * j)) for j in range(1, len(halves))]
    k_lv = [kk] + [kk * jnp.exp(part(2 * j + 1)) for j in range(1, len(halves))]
    st = st0
    outs = []
    for sl in chunks:
        scores = jnp.where(pairs[0], _nt(q_lv[0][sl], k_lv[0][sl]), 0.0)
        for j in range(1, len(halves)):
            scores += jnp.where(pairs[j], _nt(q_lv[j][sl], k_lv[j][sl]), 0.0)
        outs.append(_nn(scores, v[sl]) + _nt(qg[sl], st))
        st = st * jnp.exp(jnp.sum(logf[sl], axis=0, keepdims=True)) + _tn(v[sl], ks[sl])
    o = jnp.concatenate(outs, axis=0) + jnp.sum(q * kk, axis=1, keepdims=True) * v
    y = o * lax.rsqrt(jnp.mean(o * o, axis=1, keepdims=True) + NORM_EPS) * nw * _silu(zr)
    return y, st


def _unit_lower_inverses(ms):
    r, c = _iota2(ms[0].shape)
    xs = [jnp.where(r == c, 1.0, 0.0) - jnp.where((r // 2) == (c // 2), m, 0.0) for m in ms]
    b = 2
    while b < CHUNK:
        pick = ((r // (2 * b)) == (c // (2 * b))) & ((r // b) != (c // b))
        ts = [_nn(x, jnp.where(pick, m, 0.0)) for x, m in zip(xs, ms)]
        xs = [x - _nn(t, x) for x, t in zip(xs, ts)]
        b *= 2
    return tuple(x.astype(BF16) for x in xs)


@jax.custom_vjp
def _known_inverses(ms, xs):
    return xs


def _known_inverses_fwd(ms, xs):
    return xs, xs


def _known_inverses_bwd(xs, cts):
    r, c = _iota2(xs[0].shape)
    keep = (c < r) & ((r // CHUNK) == (c // CHUNK))
    ts = [_tn(x, ct) for x, ct in zip(xs, cts)]
    return (tuple(jnp.where(keep, -_nt(t, x), 0.0) for t, x in zip(ts, xs)), tuple(jnp.zeros_like(x) for x in xs))


_known_inverses.defvjp(_known_inverses_fwd, _known_inverses_bwd)


def _chunk_cumsum(x):
    row = lax.broadcasted_iota(jnp.int32, x.shape, 0) % CHUNK
    d = 1
    while d < CHUNK:
        x = x + jnp.where(row >= d, _shift_rows(x, d), 0.0)
        d *= 2
    return x


def _gdn_block(x_ext, z, ba, s0a, s0b, w0, w1, w2, w3, alog, dtb, nw, known=None):
    rows = z.shape[0]
    conv = (w0 * _shift_rows(x_ext, 3) + w1 * _shift_rows(x_ext, 2) + w2 * _shift_rows(x_ext, 1) + w3 * x_ext)
    cc = _silu(conv[8:])
    qc, kc = cc[:, 0:HD], cc[:, HD:2 * HD]
    q = qc * lax.rsqrt(jnp.sum(qc * qc, axis=1, keepdims=True) + L2_EPS) * (HD ** -0.5)
    k = kc * lax.rsqrt(jnp.sum(kc * kc, axis=1, keepdims=True) + L2_EPS)

    r, c = _iota2((GROUP, GROUP))
    same = (r // CHUNK) == (c // CHUNK)
    causal, strict, eye = same & (c <= r), same & (c < r), r == c
    heads = (0, 1)
    groups = [slice(lo, lo + GROUP) for lo in range(0, rows, GROUP)]
    chunks = [slice(lo, lo + CHUNK) for lo in range(0, rows, CHUNK)]

    v, loga, g_w, kb, kg, qg = [], [], [], [], [], []
    for i in heads:
        v.append(cc[:, (2 + i) * HD:(3 + i) * HD])
        beta = _sigmoid(_lane_pick(ba, i))
        a_neg = -jnp.exp(_lane_pick(alog, i))
        loga.append(a_neg * _softplus(_lane_pick(ba, 2 + i) + _lane_pick(dtb, i)))
        g_w.append(_chunk_cumsum(jnp.broadcast_to(loga[i], (rows, HD))))
        kb.append(k * beta)
        kg.append(k * jnp.exp(g_w[i]))
        qg.append(q * jnp.exp(g_w[i]))

    systems = [(i, gs) for gs in groups for i in heads]
    dec_c, ms = [], []
    for i, gs in systems:
        g_sq = g_w[i][gs]
        g_row = jnp.sum(jnp.where(eye, g_sq, 0.0), axis=0, keepdims=True)
        diff = g_sq - g_row
        dec_c.append(jnp.where(causal, jnp.exp(jnp.where(causal, diff, 0.0)), 0.0))
        ms.append(jnp.where(strict, _nt(k[gs], kb[i][gs]) * dec_c[-1], 0.0))
    xs = _unit_lower_inverses(tuple(ms)) if known is None else _known_inverses(tuple(ms), known)
    u = [[None] * len(groups) for _ in heads]
    w = [[None] * len(groups) for _ in heads]
    qk = [[None] * len(groups) for _ in heads]
    for n, (i, gs) in enumerate(systems):
        j = n // len(heads)
        u[i][j] = _nn(xs[n], v[i][gs])
        w[i][j] = _nn(xs[n], kg[i][gs])
        qk[i][j] = _nt(q[gs], kb[i][gs]) * dec_c[n]
    u = [jnp.concatenate(p, axis=0) for p in u]
    w = [jnp.concatenate(p, axis=0) for p in w]

    decay, p_mat, q_mat = {}, {}, {}
    for n, sl in enumerate(chunks):
        for i in heads:
            g_last = jnp.sum(loga[i][sl], axis=0, keepdims=True)
            kd = kb[i][sl] * jnp.exp(g_last - g_w[i][sl])
            decay[n, i] = jnp.exp(g_last)
            p_mat[n, i] = -_tn(kd, w[i][sl])
            q_mat[n, i] = _tn(kd, u[i][sl])
    s = [s0a, s0b]
    s_at = {}
    for n in range(len(chunks)):
        for i in heads:
            s_at[n, i] = s[i]
            s[i] = s[i] * decay[n, i] + _nn(p_mat[n, i], s[i]) + q_mat[n, i]

    ys = []
    for i in heads:
        e = jnp.concatenate([u[i][sl] - _nn(w[i][sl], s_at[n, i]) for n, sl in enumerate(chunks)], axis=0)
        o_state = jnp.concatenate([_nn(qg[i][sl], s_at[n, i]) for n, sl in enumerate(chunks)], axis=0)
        o = o_state + jnp.concatenate([_nn(qk[i][j], e[gs]) for j, gs in enumerate(groups)], axis=0)
        zi = z[:, i * HD:(i + 1) * HD]
        ys.append(o * lax.rsqrt(jnp.mean(o * o, axis=1, keepdims=True) + NORM_EPS) * nw * _silu(zi))
    return (jnp.concatenate(ys, axis=1), s[0], s[1]), (xs, s_at)


def _add_to_tail(x, tail):
    return x + jnp.concatenate([jnp.zeros((x.shape[0] - 8, x.shape[1]), x.dtype), tail], axis=0)


def _conv_a_block(ab, ac_ext, ax_ext, az, w0, w1, w2):
    u = ac_ext * ax_ext
    conv = (w0 * _shift_rows(u, 2) + w1 * _shift_rows(u, 1) + w2 * u)[8:]
    return ab * conv * _silu(az)


def _matmul(a, b, mode, name, residual=None, out_dtype=F32):
    if mode == "nn":
        (m, k), n = a.shape, b.shape[1]
    elif mode == "nt":
        (m, k), n = a.shape, b.shape[0]
    else:
        (k, m), n = a.shape, b.shape[1]
    tm, tn, tk = _tile(m, MM_TILE), _tile(n, MM_TILE), _tile(k, MM_TILE)
    if mode == "tn":
        tk = _tile(k, 2 * MM_TILE)
    elif k == tk:
        tm = _tile(m, 2 * MM_TILE)
    nk = k // tk
    dims = {"nn": ((1,), (0,)), "nt": ((1,), (1,)), "tn": ((0,), (0,))}[mode]
    a_spec = pl.BlockSpec((tk, tm), lambda i, j, s: (s, i)) if mode == "tn" else pl.BlockSpec((tm, tk), lambda i, j, s: (i, s))
    b_spec = pl.BlockSpec((tn, tk), lambda i, j, s: (j, s)) if mode == "nt" else pl.BlockSpec((tk, tn), lambda i, j, s: (s, j))
    o_spec = pl.BlockSpec((tm, tn), lambda i, j, s: (i, j))
    has_res = residual is not None

    def finish(out, r_ref, o_ref):
        if has_res:
            out = out + r_ref[...]
        o_ref[...] = out.astype(out_dtype)

    def body_one_pass(*refs):
        finish(_dot(refs[0][...], refs[1][...], dims), refs[2] if has_res else None, refs[-1])

    def body_reduce(*refs):
        a_ref, b_ref = refs[0], refs[1]
        r_ref = refs[2] if has_res else None
        o_ref, acc_ref = refs[-2], refs[-1]
        s = pl.program_id(2)

        @pl.when(s == 0)
        def _():
            acc_ref[...] = jnp.zeros_like(acc_ref)

        acc_ref[...] += _dot(a_ref[...], b_ref[...], dims)

        @pl.when(s == nk - 1)
        def _():
            finish(acc_ref[...], r_ref, o_ref)

    args, specs = [a, b], [a_spec, b_spec]
    if has_res:
        args.append(residual)
        specs.append(o_spec)
    return pl.pallas_call(
        body_one_pass if nk == 1 else body_reduce, name=name, grid=(m // tm, n // tn, nk), in_specs=specs, out_specs=o_spec,
        out_shape=jax.ShapeDtypeStruct((m, n), out_dtype),
        scratch_shapes=[] if nk == 1 else [pltpu.VMEM((tm, tn), F32)],
        compiler_params=_cparams("parallel", "parallel", "arbitrary"))(*args)


def _matmul_nt_sum(pairs, name, exchange=None):
    m, n = pairs[0][0].shape[0], pairs[0][1].shape[0]
    tm, tn = _tile(m, MM_TILE), _tile(n, MM_TILE)
    tks = [_tile(a.shape[1], MM_TILE) for a, _ in pairs]
    nks = [a.shape[1] // tk for (a, _), tk in zip(pairs, tks)]
    offs = [sum(nks[:i]) for i in range(len(pairs))]
    total = sum(nks)

    def body(*refs):
        o_ref, acc_ref = refs[-2], refs[-1]
        s = pl.program_id(2)

        @pl.when(s == 0)
        def _():
            acc_ref[...] = jnp.zeros_like(acc_ref)

        for i, (off, nk) in enumerate(zip(offs, nks)):
            @pl.when((s >= off) & (s < off + nk))
            def _(i=i):
                acc_ref[...] += _dot(refs[2 * i][...], refs[2 * i + 1][...], ((1,), (1,)))

        @pl.when(s == total - 1)
        def _():
            o_ref[...] = acc_ref[...]

    args, specs = [], []
    for (a, b), tk, off, nk in zip(pairs, tks, offs, nks):
        k_of = lambda s, off=off, nk=nk: jnp.clip(s - off, 0, nk - 1)
        args += [a, b]
        specs += [pl.BlockSpec((tm, tk), lambda i, j, s, k_of=k_of: (i, k_of(s))),
                  pl.BlockSpec((tn, tk), lambda i, j, s, k_of=k_of: (j, k_of(s)))]
    (out,), exchanged = _call_with_exchange(
        body, name=name, grid=(m // tm, n // tn, total), in_specs=specs,
        out_specs=[pl.BlockSpec((tm, tn), lambda i, j, s: (i, j))], out_shape=[jax.ShapeDtypeStruct((m, n), F32)],
        scratch_shapes=[pltpu.VMEM((tm, tn), F32)], args=args, exchange=exchange)
    return out, exchanged


def _rmsnorm_fwd(x, w, name):
    t = x.shape[0]
    blk = _tile(t, 1024)

    def body(x_ref, w_ref, h_ref):
        xv = x_ref[...]
        h_ref[...] = (xv * lax.rsqrt(jnp.mean(xv * xv, axis=1, keepdims=True) + NORM_EPS) * w_ref[...]).astype(BF16)

    return pl.pallas_call(
        body, name=name, grid=(t // blk,),
        in_specs=[pl.BlockSpec((blk, D), lambda i: (i, 0)), pl.BlockSpec((1, D), lambda i: (0, 0))],
        out_specs=pl.BlockSpec((blk, D), lambda i: (i, 0)), out_shape=jax.ShapeDtypeStruct((t, D), BF16),
        compiler_params=_cparams("parallel"))(x, w)


def _rmsnorm_bwd(dh, x, w, dxo, name, exchange=None):
    t = x.shape[0]
    blk = _tile(t, 512)

    def body(dh_ref, x_ref, w_ref, dxo_ref, dx_ref, dw_ref):
        @pl.when(pl.program_id(0) == 0)
        def _():
            dw_ref[...] = jnp.zeros_like(dw_ref)

        xv, dhv = x_ref[...], dh_ref[...]
        rs = lax.rsqrt(jnp.mean(xv * xv, axis=1, keepdims=True) + NORM_EPS)
        xh = xv * rs
        dw_ref[...] += jnp.sum(dhv * xh, axis=0, keepdims=True)
        dxh = dhv * w_ref[...]
        dx_ref[...] = rs * (dxh - xh * jnp.mean(dxh * xh, axis=1, keepdims=True)) + dxo_ref[...]

    row = pl.BlockSpec((blk, D), lambda i: (i, 0))
    vec = pl.BlockSpec((1, D), lambda i: (0, 0))
    return _call_with_exchange(
        body, name=name, grid=(t // blk,), in_specs=[row, row, vec, row], out_specs=[row, vec],
        out_shape=[jax.ShapeDtypeStruct((t, D), F32), jax.ShapeDtypeStruct((1, D), F32)],
        scratch_shapes=[], args=(dh, x, w, dxo), exchange=exchange)


def _loss_head(x, w, target, name):
    t = x.shape[0]
    blk = _tile(t, 512)

    def body(x_ref, w_ref, t_ref, loss_ref, dx_ref, dw_ref):
        @pl.when(pl.program_id(0) == 0)
        def _():
            dw_ref[...] = jnp.zeros_like(dw_ref)
            loss_ref[...] = jnp.zeros_like(loss_ref)

        xv = x_ref[...]
        rs = lax.rsqrt(jnp.mean(xv * xv, axis=1, keepdims=True) + NORM_EPS)
        xh = xv * rs
        err = xh * w_ref[...] - t_ref[...]
        loss_ref[...] += 0.5 * jnp.sum(jnp.mean(err * err, axis=1, keepdims=True), axis=0, keepdims=True)
        dy = err * (1.0 / D)
        dw_ref[...] += jnp.sum(dy * xh, axis=0, keepdims=True)
        dxh = dy * w_ref[...]
        dx_ref[...] = rs * (dxh - xh * jnp.mean(dxh * xh, axis=1, keepdims=True))

    row = pl.BlockSpec((blk, D), lambda i: (i, 0))
    vec = pl.BlockSpec((1, D), lambda i: (0, 0))
    return pl.pallas_call(
        body, name=name, grid=(t // blk,), in_specs=[row, vec, row],
        out_specs=[pl.BlockSpec((1, 1), lambda i: (0, 0)), row, vec],
        out_shape=[jax.ShapeDtypeStruct((1, 1), F32), jax.ShapeDtypeStruct((t, D), F32), jax.ShapeDtypeStruct((1, D), F32)],
        compiler_params=_cparams("arbitrary"))(x, w, target)


def _lbs_of(lb):
    r = lax.broadcasted_iota(jnp.int32, lb.shape, 0)
    real = r < DEPTH
    mx = lax.stop_gradient(jnp.max(jnp.where(real, lb, -jnp.inf), axis=0, keepdims=True))
    e = jnp.where(real, jnp.exp(jnp.where(real, lb - mx, 0.0)), 0.0)
    p = e / jnp.sum(e, axis=0, keepdims=True)
    out = jnp.zeros_like(lb)
    run = jnp.zeros_like(mx)
    for l in range(1, DEPTH):
        run = run + jnp.sum(jnp.where(r == l, p, 0.0), axis=0, keepdims=True)
        out = out + jnp.where(r == l, run, 0.0)
    return out


def _lower_bounds_fwd(lbp, name):
    def body(lb_ref, o_ref):
        o_ref[...] = _lbs_of(lb_ref[...])

    return pl.pallas_call(body, name=name, out_shape=jax.ShapeDtypeStruct(lbp.shape, F32))(lbp)


def _lower_bounds_bwd(lbp, dlbs, name):
    def body(lb_ref, d_ref, o_ref):
        _, vjp = jax.vjp(_lbs_of, lb_ref[...])
        o_ref[...] = vjp(d_ref[...])[0]

    return pl.pallas_call(body, name=name, out_shape=jax.ShapeDtypeStruct(lbp.shape, F32))(lbp, dlbs)


def _branch_a_fwd(pa, cw, name):
    t = pa.shape[0]
    blk = _tile(t, 512)
    W = CONV_W

    def body(p_ref, w_ref, y_ref, hc_ref, hx_ref):
        @pl.when(pl.program_id(0) == 0)
        def _():
            hc_ref[...] = jnp.zeros_like(hc_ref)
            hx_ref[...] = jnp.zeros_like(hx_ref)

        ac, ax = p_ref[:, W:2 * W], p_ref[:, 2 * W:3 * W]
        y_ref[...] = _conv_a_block(
            p_ref[:, 0:W], jnp.concatenate([hc_ref[...], ac], axis=0), jnp.concatenate([hx_ref[...], ax], axis=0),
            p_ref[:, 3 * W:4 * W], w_ref[0:1, :], w_ref[1:2, :], w_ref[2:3, :]).astype(BF16)
        hc_ref[...] = p_ref[blk - 8:blk, W:2 * W]
        hx_ref[...] = p_ref[blk - 8:blk, 2 * W:3 * W]

    return pl.pallas_call(
        body, name=name, grid=(t // blk,),
        in_specs=[pl.BlockSpec((blk, NA), lambda i: (i, 0)), pl.BlockSpec((3, W), lambda i: (0, 0))],
        out_specs=pl.BlockSpec((blk, W), lambda i: (i, 0)), out_shape=jax.ShapeDtypeStruct((t, W), BF16),
        scratch_shapes=[pltpu.VMEM((8, W), F32), pltpu.VMEM((8, W), F32)],
        compiler_params=_cparams("arbitrary"))(pa, cw)


def _branch_a_bwd(pa, cw, dy, name):
    t = pa.shape[0]
    blk = _tile(t, 512)
    nt_ = t // blk
    W = CONV_W
    hb = blk // 8

    def body(p_ref, halo_ref, w_ref, dy_ref, dp_ref, dw_ref, chc_ref, chx_ref):
        i = pl.program_id(0)

        @pl.when(i == 0)
        def _():
            chc_ref[...] = jnp.zeros_like(chc_ref)
            chx_ref[...] = jnp.zeros_like(chx_ref)
            dw_ref[...] = jnp.zeros_like(dw_ref)

        keep = 1.0 - (i == nt_ - 1).astype(F32)
        hc = halo_ref[:, W:2 * W] * keep
        hx = halo_ref[:, 2 * W:3 * W] * keep
        ac_ext = jnp.concatenate([hc, p_ref[:, W:2 * W]], axis=0)
        ax_ext = jnp.concatenate([hx, p_ref[:, 2 * W:3 * W]], axis=0)
        _, vjp = jax.vjp(_conv_a_block, p_ref[:, 0:W], ac_ext, ax_ext, p_ref[:, 3 * W:4 * W],
                         w_ref[0:1, :], w_ref[1:2, :], w_ref[2:3, :])
        dab, dac, dax, daz, dw0, dw1, dw2 = vjp(dy_ref[...].astype(F32))
        dp_ref[:, 0:W] = dab.astype(BF16)
        dp_ref[:, W:2 * W] = _add_to_tail(dac[8:], chc_ref[...]).astype(BF16)
        dp_ref[:, 2 * W:3 * W] = _add_to_tail(dax[8:], chx_ref[...]).astype(BF16)
        dp_ref[:, 3 * W:4 * W] = daz.astype(BF16)
        chc_ref[...] = dac[:8] * keep
        chx_ref[...] = dax[:8] * keep
        dw_ref[0:1, :] += dw0
        dw_ref[1:2, :] += dw1
        dw_ref[2:3, :] += dw2

    rev = lambda i: (nt_ - 1 - i, 0)
    return pl.pallas_call(
        body, name=name, grid=(nt_,),
        in_specs=[pl.BlockSpec((blk, NA), rev),
                  pl.BlockSpec((8, NA), lambda i: (jnp.maximum((nt_ - 1 - i) * hb - 1, 0), 0)),
                  pl.BlockSpec((3, W), lambda i: (0, 0)),
                  pl.BlockSpec((blk, W), rev)],
        out_specs=[pl.BlockSpec((blk, NA), rev), pl.BlockSpec((3, W), lambda i: (0, 0))],
        out_shape=[jax.ShapeDtypeStruct((t, NA), BF16), jax.ShapeDtypeStruct((3, W), F32)],
        scratch_shapes=[pltpu.VMEM((8, W), F32), pltpu.VMEM((8, W), F32)],
        compiler_params=_cparams("arbitrary"))(pa, pa, cw, dy)


def _block_rows(t, chunks):
    return min(t, chunks * CHUNK)


def _branch_b_fwd(pb, lbs_row, nw, name, exchange=None):
    t = pb.shape[0]
    rows = _block_rows(t, HGRN_BLOCK_CHUNKS)
    nch = t // rows

    def body(p_ref, lb_ref, nw_ref, y_ref, s_ref, st_ref):
        @pl.when(pl.program_id(1) == 0)
        def _():
            st_ref[...] = jnp.zeros_like(st_ref)

        s_ref[0, 0] = st_ref[...]
        y, st1 = _hgrn_block(p_ref[:, 0:HD], p_ref[:, HD:2 * HD], p_ref[:, 2 * HD:3 * HD], p_ref[:, 3 * HD:4 * HD],
                             st_ref[...], lb_ref[...], nw_ref[...])
        y_ref[...] = y.astype(BF16)
        st_ref[...] = st1

    return _call_with_exchange(
        body, name=name, grid=(HGRN_HEADS, nch),
        in_specs=[pl.BlockSpec((rows, 4 * HD), lambda h, i: (i, h)),
                  pl.BlockSpec((1, HD), lambda h, i: (0, h)),
                  pl.BlockSpec((1, HD), lambda h, i: (0, 0))],
        out_specs=[pl.BlockSpec((rows, HD), lambda h, i: (i, h)),
                   pl.BlockSpec((1, 1, HD, HD), lambda h, i: (h, i, 0, 0))],
        out_shape=[jax.ShapeDtypeStruct((t, HGRN_HEADS * HD), BF16),
                   jax.ShapeDtypeStruct((HGRN_HEADS, nch, HD, HD), F32)],
        scratch_shapes=[pltpu.VMEM((HD, HD), F32)],
        args=(pb, lbs_row, nw), exchange=exchange)


def _branch_b_bwd(pb, states, lbs_row, nw, dy, name, exchange=None):
    t = pb.shape[0]
    rows = _block_rows(t, HGRN_BLOCK_CHUNKS)
    nch = t // rows

    def body(p_ref, s_ref, lb_ref, nw_ref, dy_ref, dp_ref, dlb_ref, dnw_ref, ds_ref):
        h, i = pl.program_id(0), pl.program_id(1)

        @pl.when(i == 0)
        def _():
            ds_ref[...] = jnp.zeros_like(ds_ref)
            dlb_ref[...] = jnp.zeros_like(dlb_ref)

        @pl.when((i == 0) & (h == 0))
        def _():
            dnw_ref[...] = jnp.zeros_like(dnw_ref)

        _, vjp = jax.vjp(_hgrn_block, p_ref[:, 0:HD], p_ref[:, HD:2 * HD], p_ref[:, 2 * HD:3 * HD],
                         p_ref[:, 3 * HD:4 * HD], s_ref[0, 0], lb_ref[...], nw_ref[...])
        dq, df, di, dz, ds0, dlb, dnw = vjp((dy_ref[...].astype(F32), ds_ref[...]))
        dp_ref[:, 0:HD] = dq.astype(BF16)
        dp_ref[:, HD:2 * HD] = df.astype(BF16)
        dp_ref[:, 2 * HD:3 * HD] = di.astype(BF16)
        dp_ref[:, 3 * HD:4 * HD] = dz.astype(BF16)
        ds_ref[...] = ds0
        dlb_ref[...] += dlb
        dnw_ref[...] += dnw

    rev = lambda h, i: (nch - 1 - i, h)
    return _call_with_exchange(
        body, name=name, grid=(HGRN_HEADS, nch),
        in_specs=[pl.BlockSpec((rows, 4 * HD), rev),
                  pl.BlockSpec((1, 1, HD, HD), lambda h, i: (h, nch - 1 - i, 0, 0)),
                  pl.BlockSpec((1, HD), lambda h, i: (0, h)),
                  pl.BlockSpec((1, HD), lambda h, i: (0, 0)),
                  pl.BlockSpec((rows, HD), rev)],
        out_specs=[pl.BlockSpec((rows, 4 * HD), rev),
                   pl.BlockSpec((1, HD), lambda h, i: (0, h)),
                   pl.BlockSpec((1, HD), lambda h, i: (0, 0))],
        out_shape=[jax.ShapeDtypeStruct((t, NB), BF16), jax.ShapeDtypeStruct((1, HGRN_HEADS * HD), F32),
                   jax.ShapeDtypeStruct((1, HD), F32)],
        scratch_shapes=[pltpu.VMEM((HD, HD), F32)],
        args=(pb, states, lbs_row, nw, dy), exchange=exchange)


def _branch_c_fwd(pc, cw, cpar, nw, name, exchange=None):
    t = pc.shape[0]
    back_rows = _block_rows(t, GDN_BLOCK_CHUNKS)
    rows = _block_rows(t, GDN_FORWARD_CHUNKS)
    nch, per_step = t // rows, rows // back_rows
    XW = 4 * HD
    nsys = 2 * back_rows // GROUP

    def body(p_ref, w_ref, cp_ref, nw_ref, y_ref, s_ref, x_ref, sa_ref, sb_ref, halo_ref):
        @pl.when(pl.program_id(1) == 0)
        def _():
            sa_ref[...] = jnp.zeros_like(sa_ref)
            sb_ref[...] = jnp.zeros_like(sb_ref)
            halo_ref[...] = jnp.zeros_like(halo_ref)

        x_ext = jnp.concatenate([halo_ref[...], p_ref[:, 0:XW]], axis=0)
        (y, s1a, s1b), (xs, s_at) = _gdn_block(
            x_ext, p_ref[:, XW:XW + 2 * HD], p_ref[:, XW + 2 * HD:XW + 3 * HD],
            sa_ref[...], sb_ref[...], w_ref[0:1, :], w_ref[1:2, :], w_ref[2:3, :], w_ref[3:4, :],
            cp_ref[0, 0:1, :], cp_ref[0, 1:2, :], nw_ref[...])
        for b in range(per_step):
            for i in range(2):
                s_ref[0, b, i] = s_at[b * (back_rows // CHUNK), i]
            for n in range(nsys):
                x_ref[0, b, n] = xs[b * nsys + n]
        y_ref[...] = y.astype(BF16)
        sa_ref[...] = s1a
        sb_ref[...] = s1b
        halo_ref[...] = p_ref[rows - 8:rows, 0:XW]

    return _call_with_exchange(
        body, name=name, grid=(GDN_QK_HEADS, nch),
        in_specs=[pl.BlockSpec((rows, C_HEAD), lambda h, i: (i, h)),
                  pl.BlockSpec((4, XW), lambda h, i: (0, h)),
                  pl.BlockSpec((1, 8, HD), lambda h, i: (h, 0, 0)),
                  pl.BlockSpec((1, HD), lambda h, i: (0, 0))],
        out_specs=[pl.BlockSpec((rows, 2 * HD), lambda h, i: (i, h)),
                   pl.BlockSpec((1, per_step, 2, HD, HD), lambda h, i: (h, i, 0, 0, 0)),
                   pl.BlockSpec((1, per_step, nsys, GROUP, GROUP), lambda h, i: (h, i, 0, 0, 0))],
        out_shape=[jax.ShapeDtypeStruct((t, 2 * GDN_QK_HEADS * HD), BF16),
                   jax.ShapeDtypeStruct((GDN_QK_HEADS, nch * per_step, 2, HD, HD), F32),
                   jax.ShapeDtypeStruct((GDN_QK_HEADS, nch * per_step, nsys, GROUP, GROUP), BF16)],
        scratch_shapes=[pltpu.VMEM((HD, HD), F32), pltpu.VMEM((HD, HD), F32), pltpu.VMEM((8, XW), F32)],
        args=(pc, cw, cpar, nw), exchange=exchange)


def _branch_c_bwd(pc, states, inverses, cw, cpar, nw, dy, name, exchange=None):
    t = pc.shape[0]
    rows = _block_rows(t, GDN_BLOCK_CHUNKS)
    nch = t // rows
    XW = 4 * HD
    hb = rows // 8

    nsys = 2 * rows // GROUP

    def body(p_ref, halo_ref, s_ref, x_ref, w_ref, cp_ref, nw_ref, dy_ref, dp_ref, dw_ref, dcp_ref, dnw_ref,
             dsa_ref, dsb_ref, carry_ref):
        h, i = pl.program_id(0), pl.program_id(1)

        @pl.when(i == 0)
        def _():
            dsa_ref[...] = jnp.zeros_like(dsa_ref)
            dsb_ref[...] = jnp.zeros_like(dsb_ref)
            carry_ref[...] = jnp.zeros_like(carry_ref)
            dw_ref[...] = jnp.zeros_like(dw_ref)
            dcp_ref[...] = jnp.zeros_like(dcp_ref)

        @pl.when((i == 0) & (h == 0))
        def _():
            dnw_ref[...] = jnp.zeros_like(dnw_ref)

        keep = 1.0 - (i == nch - 1).astype(F32)
        x_ext = jnp.concatenate([halo_ref[:, 0:XW] * keep, p_ref[:, 0:XW]], axis=0)
        block = functools.partial(_gdn_block, known=tuple(x_ref[0, 0, n] for n in range(nsys)))
        _, vjp, _ = jax.vjp(block, x_ext, p_ref[:, XW:XW + 2 * HD], p_ref[:, XW + 2 * HD:XW + 3 * HD],
                            s_ref[0, 0, 0], s_ref[0, 0, 1], w_ref[0:1, :], w_ref[1:2, :], w_ref[2:3, :], w_ref[3:4, :],
                            cp_ref[0, 0:1, :], cp_ref[0, 1:2, :], nw_ref[...], has_aux=True)
        dx, dz, dba, dsa, dsb, dw0, dw1, dw2, dw3, dal, ddt, dnw = vjp((dy_ref[...].astype(F32), dsa_ref[...], dsb_ref[...]))
        dp_ref[:, 0:XW] = _add_to_tail(dx[8:], carry_ref[...]).astype(BF16)
        dp_ref[:, XW:XW + 2 * HD] = dz.astype(BF16)
        dp_ref[:, XW + 2 * HD:XW + 3 * HD] = dba.astype(BF16)
        carry_ref[...] = dx[:8] * keep
        dsa_ref[...] = dsa
        dsb_ref[...] = dsb
        dw_ref[0:1, :] += dw0
        dw_ref[1:2, :] += dw1
        dw_ref[2:3, :] += dw2
        dw_ref[3:4, :] += dw3
        dcp_ref[0, 0:1, :] += dal
        dcp_ref[0, 1:2, :] += ddt
        dnw_ref[...] += dnw

    rev = lambda h, i: (nch - 1 - i, h)
    return _call_with_exchange(
        body, name=name, grid=(GDN_QK_HEADS, nch),
        in_specs=[pl.BlockSpec((rows, C_HEAD), rev),
                  pl.BlockSpec((8, C_HEAD), lambda h, i: (jnp.maximum((nch - 1 - i) * hb - 1, 0), h)),
                  pl.BlockSpec((1, 1, 2, HD, HD), lambda h, i: (h, nch - 1 - i, 0, 0, 0)),
                  pl.BlockSpec((1, 1, nsys, GROUP, GROUP), lambda h, i: (h, nch - 1 - i, 0, 0, 0)),
                  pl.BlockSpec((4, XW), lambda h, i: (0, h)),
                  pl.BlockSpec((1, 8, HD), lambda h, i: (h, 0, 0)),
                  pl.BlockSpec((1, HD), lambda h, i: (0, 0)),
                  pl.BlockSpec((rows, 2 * HD), rev)],
        out_specs=[pl.BlockSpec((rows, C_HEAD), rev),
                   pl.BlockSpec((4, XW), lambda h, i: (0, h)),
                   pl.BlockSpec((1, 8, HD), lambda h, i: (h, 0, 0)),
                   pl.BlockSpec((1, HD), lambda h, i: (0, 0))],
        out_shape=[jax.ShapeDtypeStruct((t, NC_COLS), BF16), jax.ShapeDtypeStruct((4, GDN_QK_HEADS * XW), F32),
                   jax.ShapeDtypeStruct((GDN_QK_HEADS, 8, HD), F32), jax.ShapeDtypeStruct((1, HD), F32)],
        scratch_shapes=[pltpu.VMEM((HD, HD), F32), pltpu.VMEM((HD, HD), F32), pltpu.VMEM((8, XW), F32)],
        args=(pc, pc, states, inverses, cw, cpar, nw, dy), exchange=exchange)


def _merge_fwd(pg, bg, ya, yb, yc, name):
    t = pg.shape[0]
    blk = _tile(t, 512)

    def body(g_ref, b_ref, a_ref, b2_ref, c_ref, o_ref):
        gate = _sigmoid(g_ref[...] + b_ref[...])
        o_ref[...] = (gate[:, 0:D] * a_ref[...] + gate[:, D:2 * D] * b2_ref[...] + gate[:, 2 * D:3 * D] * c_ref[...]).astype(BF16)

    row = pl.BlockSpec((blk, D), lambda i: (i, 0))
    return pl.pallas_call(
        body, name=name, grid=(t // blk,),
        in_specs=[pl.BlockSpec((blk, NG), lambda i: (i, 0)), pl.BlockSpec((1, NG), lambda i: (0, 0)), row, row, row],
        out_specs=row, out_shape=jax.ShapeDtypeStruct((t, D), BF16),
        compiler_params=_cparams("parallel"))(pg, bg, ya, yb, yc)


def _merge_bwd(dm, pg, bg, ya, yb, yc, name):
    t = pg.shape[0]
    blk = _tile(t, 512)

    def body(dm_ref, g_ref, b_ref, a_ref, b2_ref, c_ref, dg_ref, da_ref, db_ref, dc_ref, dbg_ref):
        @pl.when(pl.program_id(0) == 0)
        def _():
            dbg_ref[...] = jnp.zeros_like(dbg_ref)

        gate = _sigmoid(g_ref[...] + b_ref[...])
        dmv = dm_ref[...].astype(F32)
        for j, (y_ref, dy_ref) in enumerate(((a_ref, da_ref), (b2_ref, db_ref), (c_ref, dc_ref))):
            gj = gate[:, j * D:(j + 1) * D]
            dy_ref[...] = (dmv * gj).astype(BF16)
            dgj = dmv * y_ref[...] * gj * (1.0 - gj)
            dg_ref[:, j * D:(j + 1) * D] = dgj.astype(BF16)
            dbg_ref[:, j * D:(j + 1) * D] += jnp.sum(dgj, axis=0, keepdims=True)

    row = pl.BlockSpec((blk, D), lambda i: (i, 0))
    wide = pl.BlockSpec((blk, NG), lambda i: (i, 0))
    vec = pl.BlockSpec((1, NG), lambda i: (0, 0))
    return pl.pallas_call(
        body, name=name, grid=(t // blk,), in_specs=[row, wide, vec, row, row, row],
        out_specs=[wide, row, row, row, vec],
        out_shape=[jax.ShapeDtypeStruct((t, NG), BF16)] + [jax.ShapeDtypeStruct((t, D), BF16)] * 3
                  + [jax.ShapeDtypeStruct((1, NG), F32)],
        compiler_params=_cparams("arbitrary"))(dm, pg, bg, ya, yb, yc)


def _adamw_math(w, g, m, v):
    m = ADAM_B1 * m + (1.0 - ADAM_B1) * g
    v = ADAM_B2 * v + (1.0 - ADAM_B2) * (g * g)
    m_hat = m / (1.0 - ADAM_B1 ** ADAM_STEP)
    v_hat = v / (1.0 - ADAM_B2 ** ADAM_STEP)
    delta = -ADAM_LR * (m_hat / (jnp.sqrt(v_hat) + ADAM_EPS) + ADAM_WD * w)
    return delta, m, v


def _sum_adamw(parts, w, m, v, name):
    layers = len(parts)
    r, c = parts[0].shape[1:]
    br = r if r <= 256 else 256
    nb = r // br
    assert r % br == 0 and w.shape == (layers * r, c)

    def body(*refs):
        w_ref, m_ref, v_ref, g_ref, d_ref, nm_ref, nv_ref = refs[layers:]
        for l in range(layers):
            @pl.when(pl.program_id(0) == l)
            def _(p_ref=refs[l]):
                g = p_ref[0].astype(F32)
                for k in range(1, N_DEV):
                    g = g + p_ref[k].astype(F32)
                g_ref[...] = g
                d_ref[...], nm_ref[...], nv_ref[...] = _adamw_math(w_ref[...], g, m_ref[...], v_ref[...])

    blk = pl.BlockSpec((br, c), lambda l, i: (l * nb + i, 0))
    part_specs = [pl.BlockSpec((N_DEV, br, c), lambda l, i, q=q: (0, jnp.where(l == q, i, jnp.where(l < q, 0, nb - 1)), 0))
                  for q in range(layers)]
    return pl.pallas_call(
        body, name=name, grid=(layers, nb), in_specs=part_specs + [blk, blk, blk], out_specs=[blk] * 4,
        out_shape=[jax.ShapeDtypeStruct((layers * r, c), F32)] * 4,
        compiler_params=_cparams("arbitrary", "arbitrary"))(*parts, w, m, v)


def _adamw(g, w, m, v, name):
    def body(g_ref, w_ref, m_ref, v_ref, d_ref, nm_ref, nv_ref):
        d_ref[...], nm_ref[...], nv_ref[...] = _adamw_math(w_ref[...], g_ref[...], m_ref[...], v_ref[...])

    return pl.pallas_call(body, name=name, out_shape=[jax.ShapeDtypeStruct(w.shape, F32)] * 3)(g, w, m, v)


def _sum_slots(parts, name):
    def body(p_ref, o_ref):
        g = p_ref[0]
        for k in range(1, N_DEV):
            g = g + p_ref[k]
        o_ref[...] = g

    return pl.pallas_call(body, name=name, out_shape=jax.ShapeDtypeStruct(parts.shape[1:], F32))(parts)


def _exchange(srcs, name, broadcast):
    n = len(srcs)

    def body(*refs):
        copies = _exchange_copies(refs[:n], refs[n:2 * n], *refs[2 * n:], broadcast)
        for cp in copies:
            cp.start()
        for cp in copies:
            cp.wait()

    return pl.pallas_call(
        body, name=name, in_specs=[HBM_SPEC] * n, out_specs=[HBM_SPEC] * n, out_shape=_exchange_shapes(srcs, broadcast),
        scratch_shapes=_exchange_semaphores(n))(*srcs)


def _gather_two_level(srcs, name):
    n = len(srcs)

    def body(*refs):
        src_refs, dst_refs = refs[:n], refs[n:2 * n]
        send_sems, recv_sems, local_sems = refs[2 * n:]
        x, y, c = lax.axis_index("x"), lax.axis_index("y"), lax.axis_index("c")
        index_of = lambda px, py, pc: 4 * px + 2 * py + pc
        me, other_core = index_of(x, y, c), (x, y, 1 - c)
        chips = [(1 - x, y), (x, 1 - y), (1 - x, 1 - y)]

        def copy(k, a, block, to, src=None):
            return pltpu.make_async_remote_copy(
                src_ref=dst_refs[a].at[block] if src is None else src, dst_ref=dst_refs[a].at[block],
                send_sem=send_sems.at[k, a], recv_sem=recv_sems.at[k, a], device_id=to, device_id_type=MESH)

        local = [pltpu.make_async_copy(src_refs[a], dst_refs[a].at[me], local_sems.at[a]) for a in range(n)]
        first = [copy(0, a, me, other_core, src=src_refs[a]) for a in range(n)]
        first += [copy(1 + j, a, me, (*chip, c), src=src_refs[a]) for j, chip in enumerate(chips) for a in range(n)]
        for cp in local + first:
            cp.start()
        passed = []
        for j, chip in enumerate(chips):
            block = index_of(*chip, c)
            for a in range(n):
                copy(1 + j, a, block, (x, y, c)).wait_recv()
            for a in range(n):
                passed.append(copy(4 + j, a, block, other_core))
                passed[-1].start()
        for a in range(n):
            copy(0, a, index_of(x, y, 1 - c), (x, y, c)).wait_recv()
        for j, chip in enumerate(chips):
            for a in range(n):
                copy(4 + j, a, index_of(*chip, 1 - c), (x, y, c)).wait_recv()
        for cp in first + passed:
            cp.wait_send()
        for cp in local:
            cp.wait()

    return pl.pallas_call(
        body, name=name, in_specs=[HBM_SPEC] * n, out_specs=[HBM_SPEC] * n, out_shape=_exchange_shapes(srcs, True),
        scratch_shapes=_exchange_semaphores(n))(*srcs)


HBM_SPEC = pl.BlockSpec(memory_space=pltpu.HBM)


def _exchange_shapes(srcs, broadcast):
    return [jax.ShapeDtypeStruct((N_DEV,) + (s.shape if broadcast else s.shape[1:]), s.dtype) for s in srcs]


def _exchange_semaphores(n):
    return [pltpu.SemaphoreType.DMA((N_DEV - 1, n)), pltpu.SemaphoreType.DMA((N_DEV - 1, n)), pltpu.SemaphoreType.DMA((n,))]


def _exchange_copies(src_refs, dst_refs, send_sems, recv_sems, local_sems, broadcast):
    x, y, c = lax.axis_index("x"), lax.axis_index("y"), lax.axis_index("c")
    me = 4 * x + 2 * y + c
    copies = []
    for k in range(1, N_DEV):
        px = 1 - x if (k >> 2) & 1 else x
        py = 1 - y if (k >> 1) & 1 else y
        pc = 1 - c if k & 1 else c
        peer = 4 * px + 2 * py + pc
        for a, (src, dst) in enumerate(zip(src_refs, dst_refs)):
            copies.append(pltpu.make_async_remote_copy(
                src_ref=src if broadcast else src.at[peer], dst_ref=dst.at[me],
                send_sem=send_sems.at[k - 1, a], recv_sem=recv_sems.at[k - 1, a],
                device_id=(px, py, pc), device_id_type=MESH))
    for a, (src, dst) in enumerate(zip(src_refs, dst_refs)):
        copies.append(pltpu.make_async_copy(src if broadcast else src.at[me], dst.at[me], local_sems.at[a]))
    return copies


def _call_with_exchange(body, *, name, grid, in_specs, out_specs, out_shape, scratch_shapes, args, exchange):
    if exchange is None:
        outs = pl.pallas_call(body, name=name, grid=grid, in_specs=in_specs, out_specs=out_specs, out_shape=out_shape,
                              scratch_shapes=scratch_shapes,
                              compiler_params=_cparams(*["arbitrary"] * len(grid)))(*args)
        return outs, None
    srcs, broadcast = exchange
    n, n_in, n_out, n_scr = len(srcs), len(args), len(out_shape), len(scratch_shapes)
    steps = 1
    for g in grid:
        steps *= g

    def hosted(*refs):
        ins, src_refs = refs[:n_in], refs[n_in:n_in + n]
        outs, dst_refs = refs[n_in + n:n_in + n + n_out], refs[n_in + n + n_out:n_in + 2 * n + n_out]
        scratch = refs[n_in + 2 * n + n_out:]
        step = pl.program_id(0)
        for axis in range(1, len(grid)):
            step = step * grid[axis] + pl.program_id(axis)

        @pl.when(step == 0)
        def _():
            for cp in _exchange_copies(src_refs, dst_refs, *scratch[n_scr:], broadcast):
                cp.start()

        body(*ins, *outs, *scratch[:n_scr])

        @pl.when(step == steps - 1)
        def _():
            for cp in _exchange_copies(src_refs, dst_refs, *scratch[n_scr:], broadcast):
                cp.wait()

    outs = pl.pallas_call(
        hosted, name=name, grid=grid, in_specs=list(in_specs) + [HBM_SPEC] * n, out_specs=list(out_specs) + [HBM_SPEC] * n,
        out_shape=list(out_shape) + _exchange_shapes(srcs, broadcast),
        scratch_shapes=list(scratch_shapes) + _exchange_semaphores(n),
        compiler_params=_cparams(*["arbitrary"] * len(grid)))(*args, *srcs)
    return outs[:n_out], outs[n_out:]


def _regroup_w_in(w):
    wa = w[:, OFF_A:OFF_A + NA]
    seg = lambda off, h, n=HD: w[:, off + h * n: off + (h + 1) * n]
    wb = jnp.concatenate([seg(OFF_B + s * 512, h) for h in range(HGRN_HEADS) for s in range(4)], axis=1)
    parts = []
    for h in range(GDN_QK_HEADS):
        small = jnp.concatenate(
            [w[:, OFF_BETA + 2 * h: OFF_BETA + 2 * h + 2], w[:, OFF_CA + 2 * h: OFF_CA + 2 * h + 2],
             jnp.zeros((w.shape[0], HD - 4), w.dtype)], axis=1)
        parts += [seg(OFF_CQ, h), seg(OFF_CK, h), seg(OFF_CV, h, 2 * HD), seg(OFF_CZ, h, 2 * HD), small]
    wc = jnp.concatenate(parts, axis=1)
    wg = w[:, OFF_G:OFF_G + NG]
    return wa, wb, wc, wg


def _ungroup_dw_in(da, db, dc, dg):
    bq = [jnp.concatenate([db[:, h * 512 + s * HD: h * 512 + (s + 1) * HD] for h in range(HGRN_HEADS)], axis=1)
          for s in range(4)]
    ch = lambda h, lo, hi: dc[:, h * C_HEAD + lo: h * C_HEAD + hi]
    heads = range(GDN_QK_HEADS)
    cq = jnp.concatenate([ch(h, 0, HD) for h in heads], axis=1)
    ck = jnp.concatenate([ch(h, HD, 2 * HD) for h in heads], axis=1)
    cv = jnp.concatenate([ch(h, 2 * HD, 4 * HD) for h in heads], axis=1)
    cz = jnp.concatenate([ch(h, 4 * HD, 6 * HD) for h in heads], axis=1)
    cbeta = jnp.concatenate([ch(h, 6 * HD, 6 * HD + 2) for h in heads], axis=1)
    ca = jnp.concatenate([ch(h, 6 * HD + 2, 6 * HD + 4) for h in heads], axis=1)
    return jnp.concatenate([da] + bq + [cq, ck, cv, cbeta, ca, cz, dg], axis=1)


def _regroup_conv_c(cw):
    parts = []
    for h in range(GDN_QK_HEADS):
        parts += [cw[:, h * HD:(h + 1) * HD], cw[:, 512 + h * HD: 512 + (h + 1) * HD],
                  cw[:, 1024 + 2 * h * HD: 1024 + (2 * h + 2) * HD]]
    return jnp.concatenate(parts, axis=1)


def _ungroup_conv_c(d):
    heads = range(GDN_QK_HEADS)
    q = jnp.concatenate([d[:, h * 512: h * 512 + HD] for h in heads], axis=1)
    k = jnp.concatenate([d[:, h * 512 + HD: h * 512 + 2 * HD] for h in heads], axis=1)
    v = jnp.concatenate([d[:, h * 512 + 2 * HD: h * 512 + 4 * HD] for h in heads], axis=1)
    return jnp.concatenate([q, k, v], axis=1)


def _numel(shape):
    n = 1
    for d in shape:
        n *= d
    return n


def _pack(arrays, rows):
    flat = jnp.concatenate([a.reshape(-1) for a in arrays])
    return jnp.pad(flat, (0, rows * 128 - flat.shape[0])).reshape(rows, 128)


def _unpack(packed, shapes):
    flat = packed.reshape(-1)
    out, off = [], 0
    for s in shapes:
        out.append(flat[off:off + _numel(s)].reshape(s))
        off += _numel(s)
    return out


def _rows_for(shapes):
    return -(-sum(_numel(s) for s in shapes) // 1024) * 8


def kernel(x, norm_w, w_in, b_gate, conv_a, conv_c, a_log, dt_bias, lower_bounds, hgrn_norm_w, gdn_norm_w, w_out_a, w_out_b, w_out_c, w_o, final_norm_w, loss_target, m_norm_w, m_w_in, m_b_gate, m_conv_a, m_conv_c, m_a_log, m_dt_bias, m_lower_bounds, m_hgrn_norm_w, m_gdn_norm_w, m_w_out_a, m_w_out_b, m_w_out_c, m_w_o, m_final_norm_w, v_norm_w, v_w_in, v_b_gate, v_conv_a, v_conv_c, v_a_log, v_dt_bias, v_lower_bounds, v_hgrn_norm_w, v_gdn_norm_w, v_w_out_a, v_w_out_b, v_w_out_c, v_w_o, v_final_norm_w):
    me = 4 * lax.axis_index("x") + 2 * lax.axis_index("y") + lax.axis_index("c")
    xs = x[0]
    target = loss_target[0]
    in_shard = w_in.shape[2]

    big = [w_in, w_out_a, w_out_b, w_out_c, w_o]
    shards_of = lambda l: [w[l].astype(BF16) for w in big]
    conv_shapes = [(DEPTH, 3, CONV_W), (DEPTH, 4, 2048)]
    conv_rows = _rows_for(conv_shapes)
    ca_full = lax.dynamic_update_slice(jnp.zeros(conv_shapes[0], F32), conv_a, (0, 0, me * conv_a.shape[2]))
    cc_full = lax.dynamic_update_slice(jnp.zeros(conv_shapes[1], F32), conv_c, (0, 0, me * conv_c.shape[2]))
    g_in0, conv_parts = _gather_two_level([w_in[0].astype(BF16), _pack([ca_full, cc_full], conv_rows)], "gather_l0")
    conv_a_full, conv_c_full = _unpack(_sum_slots(conv_parts, "sum_conv"), conv_shapes)

    lb_pad = jnp.pad(lower_bounds, ((0, 8 - DEPTH), (0, 0)))
    lbs = _lower_bounds_fwd(lb_pad, "lower_bounds_fwd")

    def input_weights(l, g_in):
        wa, wb, wc, wg = _regroup_w_in(jnp.concatenate([g_in[q] for q in range(N_DEV)], axis=1))
        lanes = lambda vec: jnp.pad(vec.reshape(GDN_QK_HEADS, 1, 2), ((0, 0), (0, 0), (0, HD - 2)))
        cpar = jnp.concatenate([lanes(a_log[l]), lanes(dt_bias[l]), jnp.zeros((GDN_QK_HEADS, 6, HD), F32)], axis=1)
        return dict(
            wa=wa, wb=wb, wc=wc, wg=wg, cpar=cpar,
            nw=norm_w[l:l + 1], bg=b_gate[l:l + 1], cwa=conv_a_full[l], cwc=_regroup_conv_c(conv_c_full[l]),
            lb=lbs[l:l + 1], hnw=hgrn_norm_w[l:l + 1], gnw=gdn_norm_w[l:l + 1])

    def output_weights(g_oa, g_ob, g_oc, g_o):
        return dict(woa=jnp.concatenate([g_oa[q] for q in range(N_DEV)], axis=1),
                    wob=jnp.concatenate([g_ob[q] for q in range(N_DEV)], axis=1), woc=g_oc.reshape(D, D), wo=g_o.reshape(D, D))

    layers = [input_weights(0, g_in0)]

    saved = []
    cur = xs
    for l in range(DEPTH):
        L = layers[l]
        n = f"l{l}_"
        h = _rmsnorm_fwd(cur, L["nw"], n + "rms")
        pa = _matmul(h, L["wa"], "nn", n + "proj_a")
        pb = _matmul(h, L["wb"], "nn", n + "proj_b")
        pc = _matmul(h, L["wc"], "nn", n + "proj_c")
        pg = _matmul(h, L["wg"], "nn", n + "proj_g", out_dtype=BF16)
        ua = _branch_a_fwd(pa, L["cwa"], n + "conv_fwd")
        carry = (shards_of(l)[1:], True) if l == 0 else None
        (ub, sb), gathered = _branch_b_fwd(pb, L["lb"], L["hnw"], n + "hgrn_fwd", exchange=carry)
        if carry is not None:
            L.update(output_weights(*gathered))
        carry = (shards_of(l + 1), True) if l + 1 < DEPTH else None
        (uc, sc, xc), gathered = _branch_c_fwd(pc, L["cwc"], L["cpar"], L["gnw"], n + "gdn_fwd", exchange=carry)
        if carry is not None:
            layers.append(dict(input_weights(l + 1, gathered[0]), **output_weights(*gathered[1:])))
        ya = _matmul(ua, L["woa"], "nn", n + "out_a", out_dtype=BF16)
        yb = _matmul(ub, L["wob"], "nn", n + "out_b", out_dtype=BF16)
        yc = _matmul(uc, L["woc"], "nn", n + "out_c", out_dtype=BF16)
        merged = _merge_fwd(pg, L["bg"], ya, yb, yc, n + "merge")
        nxt = _matmul(merged, L["wo"], "nn", n + "out_o", residual=cur)
        saved.append(dict(x=cur, h=h, pa=pa, pb=pb, pc=pc, pg=pg, ua=ua, ub=ub, uc=uc, sb=sb, sc=sc, xc=xc,
                          ya=ya, yb=yb, yc=yc, merged=merged))
        cur = nxt

    loss_part, dx, d_final = _loss_head(cur, final_norm_w.reshape(1, D), target, "loss_head")

    def outgoing(g):
        cols = lambda a, n: jnp.stack([a[:, p * n:(p + 1) * n] for p in range(N_DEV)]).astype(BF16)
        rows = lambda a: a.reshape(N_DEV, a.shape[0] // N_DEV, a.shape[1]).astype(BF16)
        first = [cols(g["w_in"], in_shard)] if "w_in" in g else [None]
        if "w_o" not in g:
            return first
        return first + [cols(g["w_out_a"], 128), cols(g["w_out_b"], 128), rows(g["w_out_c"]), rows(g["w_o"])]

    grads = [None] * DEPTH
    dlbs_rows = [None] * DEPTH
    incoming = [None] * DEPTH
    for l in reversed(range(DEPTH)):
        L, S = layers[l], saved[l]
        n = f"l{l}_"
        dmerged = _matmul(dx, L["wo"], "nt", n + "d_merged", out_dtype=BF16)
        d_wo = _matmul(S["merged"], dx, "tn", n + "dw_o", out_dtype=BF16)
        dpg, dya, dyb, dyc, d_bg = _merge_bwd(dmerged, S["pg"], L["bg"], S["ya"], S["yb"], S["yc"], n + "merge_bwd")
        dua = _matmul(dya, L["woa"], "nt", n + "d_ua", out_dtype=BF16)
        dub = _matmul(dyb, L["wob"], "nt", n + "d_ub", out_dtype=BF16)
        duc = _matmul(dyc, L["woc"], "nt", n + "d_uc", out_dtype=BF16)
        d_woa = _matmul(S["ua"], dya, "tn", n + "dw_out_a", out_dtype=BF16)
        d_wob = _matmul(S["ub"], dyb, "tn", n + "dw_out_b", out_dtype=BF16)
        d_woc = _matmul(S["uc"], dyc, "tn", n + "dw_out_c", out_dtype=BF16)
        dpa, d_cwa = _branch_a_bwd(S["pa"], L["cwa"], dua, n + "conv_bwd")
        out_grads = dict(w_out_a=d_woa, w_out_b=d_wob, w_out_c=d_woc, w_o=d_wo)
        carry = (outgoing(out_grads)[1:], False) if l == 0 else None
        (dpb, d_lb, d_hnw), arrived_out = _branch_b_bwd(S["pb"], S["sb"], L["lb"], L["hnw"], dub, n + "hgrn_bwd", exchange=carry)
        carry = (outgoing(grads[l + 1]), False) if l + 1 < DEPTH else None
        (dpc, d_cwc, d_cpar, d_gnw), arrived = _branch_c_bwd(S["pc"], S["sc"], S["xc"], L["cwc"], L["cpar"], L["gnw"], duc,
                                                             n + "gdn_bwd", exchange=carry)
        if carry is not None:
            incoming[l + 1] = arrived
        d_win = _ungroup_dw_in(*[_matmul(S["h"], dp, "tn", f"{n}dw_{piece}", out_dtype=BF16)
                                 for dp, piece in ((dpa, "a"), (dpb, "b"), (dpc, "c"), (dpg, "g"))])
        carry = (outgoing(dict(w_in=d_win)), False) if l == 0 else None
        dh, arrived_in = _matmul_nt_sum([(dpa, L["wa"]), (dpb, L["wb"]), (dpc, L["wc"]), (dpg, L["wg"])], n + "dh",
                                        exchange=carry)
        (dx, d_nw), _ = _rmsnorm_bwd(dh, S["x"], L["nw"], dx, n + "rms_bwd")
        dlbs_rows[l] = d_lb
        grads[l] = dict(w_in=d_win, w_out_a=d_woa, w_out_b=d_wob, w_out_c=d_woc, w_o=d_wo, norm_w=d_nw[0],
                        b_gate=d_bg[0], conv_a=d_cwa, conv_c=_ungroup_conv_c(d_cwc),
                        a_log=d_cpar[:, 0, 0:2].reshape(-1), dt_bias=d_cpar[:, 1, 0:2].reshape(-1),
                        hgrn_norm_w=d_hnw[0], gdn_norm_w=d_gnw[0])
    grad_x = dx[None]
    d_lower = _lower_bounds_bwd(lb_pad, jnp.pad(jnp.concatenate(dlbs_rows, axis=0), ((0, 8 - DEPTH), (0, 0))),
                                "lower_bounds_bwd")[:DEPTH]

    incoming[0] = list(arrived_in) + list(arrived_out)
    stack = lambda name: jnp.stack([grads[l][name] for l in range(DEPTH)])
    big_out = {}
    for j, (name, w, m, v) in enumerate((("w_in", w_in, m_w_in, v_w_in), ("w_out_a", w_out_a, m_w_out_a, v_w_out_a),
                                         ("w_out_b", w_out_b, m_w_out_b, v_w_out_b), ("w_out_c", w_out_c, m_w_out_c, v_w_out_c),
                                         ("w_o", w_o, m_w_o, v_w_o))):
        parts = [incoming[l][j] for l in range(DEPTH)]
        r2 = lambda a: a.reshape(DEPTH * parts[0].shape[1], parts[0].shape[2])
        outs = _sum_adamw(parts, r2(w), r2(m), r2(v), "adamw_" + name)
        big_out[name] = [o.reshape(w.shape) for o in outs]

    small_names = ["norm_w", "b_gate", "conv_a", "conv_c", "a_log", "dt_bias", "lower_bounds", "hgrn_norm_w",
                   "gdn_norm_w", "final_norm_w", "loss"]
    small_vals = {k: stack(k) for k in ("norm_w", "b_gate", "conv_a", "conv_c", "a_log", "dt_bias", "hgrn_norm_w", "gdn_norm_w")}
    small_vals.update(lower_bounds=d_lower, final_norm_w=d_final[0], loss=loss_part.reshape(1))
    small_shapes = [small_vals[k].shape for k in small_names]
    small_rows = _rows_for(small_shapes)
    small_parts, = _exchange([_pack([small_vals[k] for k in small_names], small_rows)], "exchange_small", broadcast=True)
    total = dict(zip(small_names, _unpack(_sum_slots(small_parts, "sum_small"), small_shapes)))
    loss = total["loss"][0]
    g_conv_a = lax.dynamic_slice(total["conv_a"], (0, 0, me * conv_a.shape[2]), conv_a.shape)
    g_conv_c = lax.dynamic_slice(total["conv_c"], (0, 0, me * conv_c.shape[2]), conv_c.shape)

    small_w = dict(norm_w=(norm_w, m_norm_w, v_norm_w), b_gate=(b_gate, m_b_gate, v_b_gate),
                   conv_a=(conv_a, m_conv_a, v_conv_a), conv_c=(conv_c, m_conv_c, v_conv_c),
                   a_log=(a_log, m_a_log, v_a_log), dt_bias=(dt_bias, m_dt_bias, v_dt_bias),
                   lower_bounds=(lower_bounds, m_lower_bounds, v_lower_bounds),
                   hgrn_norm_w=(hgrn_norm_w, m_hgrn_norm_w, v_hgrn_norm_w), gdn_norm_w=(gdn_norm_w, m_gdn_norm_w, v_gdn_norm_w),
                   final_norm_w=(final_norm_w, m_final_norm_w, v_final_norm_w))
    small_g = dict(total, conv_a=g_conv_a, conv_c=g_conv_c)
    upd_names = small_names[:-1]
    upd_shapes = [small_w[k][0].shape for k in upd_names]
    upd_rows = _rows_for(upd_shapes)
    pk = lambda j: _pack([small_w[k][j] for k in upd_names], upd_rows)
    s_delta, s_m, s_v = _adamw(_pack([small_g[k] for k in upd_names], upd_rows), pk(0), pk(1), pk(2), "adamw_small")
    small_out = {k: [small_g[k], d, mm, vv] for k, d, mm, vv in
                 zip(upd_names, _unpack(s_delta, upd_shapes), _unpack(s_m, upd_shapes), _unpack(s_v, upd_shapes))}

    order = ["norm_w", "w_in", "b_gate", "conv_a", "conv_c", "a_log", "dt_bias", "lower_bounds", "hgrn_norm_w",
             "gdn_norm_w", "w_out_a", "w_out_b", "w_out_c", "w_o", "final_norm_w"]
    res = {**small_out, **big_out}
    outs = [loss, grad_x]
    for j in range(4):
        outs += [res[k][j] for k in order]
    return tuple(outs)
```

```python
import functools

import jax
import jax.numpy as jnp
from jax import lax
from jax.experimental import pallas as pl
from jax.experimental.pallas import tpu as pltpu

F32 = jnp.float32
BF16 = jnp.bfloat16
MESH = pl.DeviceIdType.MESH

N_DEV = 8
D = 1024
DEPTH = 2
CHUNK = 64
HGRN_BLOCK_CHUNKS = 16
GDN_BLOCK_CHUNKS = 8
GDN_FORWARD_CHUNKS = 16
GROUP = 128
NORM_EPS = 1e-6
L2_EPS = 1e-6
MIN_F = 1e-30
HD = 128
HGRN_HEADS = 4
GDN_QK_HEADS = 4
CONV_W = 512
IN_COLS = 10256
OFF_A, OFF_B, OFF_CQ, OFF_CK, OFF_CV, OFF_BETA, OFF_CA, OFF_CZ, OFF_G = (
    0, 2048, 4096, 4608, 5120, 6144, 6152, 6160, 7184)
NA, NB, NC_COLS, NG = 2048, 2048, 3584, 3072
C_HEAD = 896

ADAM_LR, ADAM_B1, ADAM_B2, ADAM_EPS, ADAM_WD, ADAM_STEP = 0.001, 0.9, 0.999, 1e-08, 0.01, 10

VMEM_LIMIT = 56 * 1024 * 1024
MM_TILE = 1024


def _cparams(*sem):
    return pltpu.CompilerParams(dimension_semantics=sem, vmem_limit_bytes=VMEM_LIMIT)


def _tile(dim, cap):
    if dim <= cap:
        return dim
    t = (cap // 128) * 128
    while dim % t:
        t -= 128
    return t


def _sigmoid(x):
    return 1.0 / (1.0 + jnp.exp(-x))


def _silu(x):
    return x * _sigmoid(x)


def _softplus(x):
    return jnp.maximum(x, 0.0) + jnp.log(1.0 + jnp.exp(-jnp.abs(x)))


def _dot(a, b, dims, precision=None):
    if precision is None:
        a, b = a.astype(BF16), b.astype(BF16)
    return lax.dot_general(a, b, (dims, ((), ())), precision=precision, preferred_element_type=F32)


def _nn(a, b, precision=None):
    return _dot(a, b, ((1,), (0,)), precision)


def _nt(a, b, precision=None):
    return _dot(a, b, ((1,), (1,)), precision)


def _tn(a, b, precision=None):
    return _dot(a, b, ((0,), (0,)), precision)


def _sum_rows_split(mat01, x):
    m = mat01.astype(BF16)
    hi = x.astype(BF16)
    low = (x - hi.astype(F32)).astype(BF16)
    return _nn(m, hi) + _nn(m, low)


@functools.partial(jax.custom_vjp, nondiff_argnums=(1,))
def _shift_rows(x, d):
    return x if d == 0 else pltpu.roll(x, d, 0)


def _shift_rows_fwd(x, d):
    return _shift_rows(x, d), None


def _shift_rows_bwd(d, _, ct):
    return ((ct if d == 0 else pltpu.roll(ct, ct.shape[0] - d, 0)),)


_shift_rows.defvjp(_shift_rows_fwd, _shift_rows_bwd)


def _iota2(shape):
    return lax.broadcasted_iota(jnp.int32, shape, 0), lax.broadcasted_iota(jnp.int32, shape, 1)


def _lane_pick(x, i):
    lane = lax.broadcasted_iota(jnp.int32, x.shape, 1)
    return jnp.sum(jnp.where(lane == i, x, 0.0), axis=1, keepdims=True)


def _hgrn_block(qr, fr, ir, zr, st0, lb, nw):
    rows = qr.shape[0]
    r, c = _iota2((CHUNK, CHUNK))
    halves = [1 << j for j in range(CHUNK.bit_length() - 1)]
    mats = [c <= r, c > r]
    pairs = []
    for hb in halves:
        same = (r // hb) == (c // hb)
        if hb > 1:
            mats += [(c <= r) & same, (c > r) & same]
        pairs.append(((r // (2 * hb)) == (c // (2 * hb))) & ((r // hb) == (c // hb) + 1))
    stack = jnp.concatenate([m.astype(F32) for m in mats], axis=0)

    q = _silu(qr) * (HD ** -0.5)
    fg = lb + (1.0 - lb) * _sigmoid(fr)
    logf = jnp.log(jnp.maximum(fg, MIN_F))
    kk = 1.0 - fg
    v = ir

    chunks = [slice(s, s + CHUNK) for s in range(0, rows, CHUNK)]
    cums = [_sum_rows_split(stack, logf[sl]) for sl in chunks]
    part = lambda i: jnp.concatenate([cs[i * CHUNK:(i + 1) * CHUNK] for cs in cums], axis=0)
    qg = q * jnp.exp(part(0))
    ks = kk * jnp.exp(part(1))
    q_lv = [q * jnp.exp(logf)] + [q * jnp.exp(part(2 * j)) for j in range(1, len(halves))]
    k_lv = [kk] + [kk * jnp.exp(part(2 * j + 1)) for j in range(1, len(halves))]
    st = st0
    outs = []
    for sl in chunks:
        scores = jnp.where(pairs[0], _nt(q_lv[0][sl], k_lv[0][sl]), 0.0)
        for j in range(1, len(halves)):
            scores += jnp.where(pairs[j], _nt(q_lv[j][sl], k_lv[j][sl]), 0.0)
        outs.append(_nn(scores, v[sl]) + _nt(qg[sl], st))
        st = st * jnp.exp(jnp.sum(logf[sl], axis=0, keepdims=True)) + _tn(v[sl], ks[sl])
    o = jnp.concatenate(outs, axis=0) + jnp.sum(q * kk, axis=1, keepdims=True) * v
    y = o * lax.rsqrt(jnp.mean(o * o, axis=1, keepdims=True) + NORM_EPS) * nw * _silu(zr)
    return y, st


def _unit_lower_inverses(ms):
    r, c = _iota2(ms[0].shape)
    xs = [jnp.where(r == c, 1.0, 0.0) - jnp.where((r // 2) == (c // 2), m, 0.0) for m in ms]
    b = 2
    while b < CHUNK:
        pick = ((r // (2 * b)) == (c // (2 * b))) & ((r // b) != (c // b))
        ts = [_nn(x, jnp.where(pick, m, 0.0)) for x, m in zip(xs, ms)]
        xs = [x - _nn(t, x) for x, t in zip(xs, ts)]
        b *= 2
    return tuple(x.astype(BF16) for x in xs)


@jax.custom_vjp
def _known_inverses(ms, xs):
    return xs


def _known_inverses_fwd(ms, xs):
    return xs, xs


def _known_inverses_bwd(xs, cts):
    r, c = _iota2(xs[0].shape)
    keep = (c < r) & ((r // CHUNK) == (c // CHUNK))
    ts = [_tn(x, ct) for x, ct in zip(xs, cts)]
    return (tuple(jnp.where(keep, -_nt(t, x), 0.0) for t, x in zip(ts, xs)), tuple(jnp.zeros_like(x) for x in xs))


_known_inverses.defvjp(_known_inverses_fwd, _known_inverses_bwd)


def _chunk_cumsum(x):
    row = lax.broadcasted_iota(jnp.int32, x.shape, 0) % CHUNK
    d = 1
    while d < CHUNK:
        x = x + jnp.where(row >= d, _shift_rows(x, d), 0.0)
        d *= 2
    return x


def _gdn_block(x_ext, z, ba, s0a, s0b, w0, w1, w2, w3, alog, dtb, nw, known=None):
    rows = z.shape[0]
    conv = (w0 * _shift_rows(x_ext, 3) + w1 * _shift_rows(x_ext, 2) + w2 * _shift_rows(x_ext, 1) + w3 * x_ext)
    cc = _silu(conv[8:])
    qc, kc = cc[:, 0:HD], cc[:, HD:2 * HD]
    q = qc * lax.rsqrt(jnp.sum(qc * qc, axis=1, keepdims=True) + L2_EPS) * (HD ** -0.5)
    k = kc * lax.rsqrt(jnp.sum(kc * kc, axis=1, keepdims=True) + L2_EPS)

    r, c = _iota2((GROUP, GROUP))
    same = (r // CHUNK) == (c // CHUNK)
    causal, strict, eye = same & (c <= r), same & (c < r), r == c
    heads = (0, 1)
    groups = [slice(lo, lo + GROUP) for lo in range(0, rows, GROUP)]
    chunks = [slice(lo, lo + CHUNK) for lo in range(0, rows, CHUNK)]

    v, loga, g_w, kb, kg, qg = [], [], [], [], [], []
    for i in heads:
        v.append(cc[:, (2 + i) * HD:(3 + i) * HD])
        beta = _sigmoid(_lane_pick(ba, i))
        a_neg = -jnp.exp(_lane_pick(alog, i))
        loga.append(a_neg * _softplus(_lane_pick(ba, 2 + i) + _lane_pick(dtb, i)))
        g_w.append(_chunk_cumsum(jnp.broadcast_to(loga[i], (rows, HD))))
        kb.append(k * beta)
        kg.append(k * jnp.exp(g_w[i]))
        qg.append(q * jnp.exp(g_w[i]))

    systems = [(i, gs) for gs in groups for i in heads]
    dec_c, ms = [], []
    for i, gs in systems:
        g_sq = g_w[i][gs]
        g_row = jnp.sum(jnp.where(eye, g_sq, 0.0), axis=0, keepdims=True)
        diff = g_sq - g_row
        dec_c.append(jnp.where(causal, jnp.exp(jnp.where(causal, diff, 0.0)), 0.0))
        ms.append(jnp.where(strict, _nt(k[gs], kb[i][gs]) * dec_c[-1], 0.0))
    xs = _unit_lower_inverses(tuple(ms)) if known is None else _known_inverses(tuple(ms), known)
    u = [[None] * len(groups) for _ in heads]
    w = [[None] * len(groups) for _ in heads]
    qk = [[None] * len(groups) for _ in heads]
    for n, (i, gs) in enumerate(systems):
        j = n // len(heads)
        u[i][j] = _nn(xs[n], v[i][gs])
        w[i][j] = _nn(xs[n], kg[i][gs])
        qk[i][j] = _nt(q[gs], kb[i][gs]) * dec_c[n]
    u = [jnp.concatenate(p, axis=0) for p in u]
    w = [jnp.concatenate(p, axis=0) for p in w]

    decay, p_mat, q_mat = {}, {}, {}
    for n, sl in enumerate(chunks):
        for i in heads:
            g_last = jnp.sum(loga[i][sl], axis=0, keepdims=True)
            kd = kb[i][sl] * jnp.exp(g_last - g_w[i][sl])
            decay[n, i] = jnp.exp(g_last)
            p_mat[n, i] = -_tn(kd, w[i][sl])
            q_mat[n, i] = _tn(kd, u[i][sl])
    s = [s0a, s0b]
    s_at = {}
    for n in range(len(chunks)):
        for i in heads:
            s_at[n, i] = s[i]
            s[i] = s[i] * decay[n, i] + _nn(p_mat[n, i], s[i]) + q_mat[n, i]

    ys = []
    for i in heads:
        e = jnp.concatenate([u[i][sl] - _nn(w[i][sl], s_at[n, i]) for n, sl in enumerate(chunks)], axis=0)
        o_state = jnp.concatenate([_nn(qg[i][sl], s_at[n, i]) for n, sl in enumerate(chunks)], axis=0)
        o = o_state + jnp.concatenate([_nn(qk[i][j], e[gs]) for j, gs in enumerate(groups)], axis=0)
        zi = z[:, i * HD:(i + 1) * HD]
        ys.append(o * lax.rsqrt(jnp.mean(o * o, axis=1, keepdims=True) + NORM_EPS) * nw * _silu(zi))
    return (jnp.concatenate(ys, axis=1), s[0], s[1]), (xs, s_at)


def _add_to_tail(x, tail):
    return x + jnp.concatenate([jnp.zeros((x.shape[0] - 8, x.shape[1]), x.dtype), tail], axis=0)


def _conv_a_block(ab, ac_ext, ax_ext, az, w0, w1, w2):
    u = ac_ext * ax_ext
    conv = (w0 * _shift_rows(u, 2) + w1 * _shift_rows(u, 1) + w2 * u)[8:]
    return ab * conv * _silu(az)


def _matmul(a, b, mode, name, residual=None, out_dtype=F32):
    if mode == "nn":
        (m, k), n = a.shape, b.shape[1]
    elif mode == "nt":
        (m, k), n = a.shape, b.shape[0]
    else:
        (k, m), n = a.shape, b.shape[1]
    tm, tn, tk = _tile(m, MM_TILE), _tile(n, MM_TILE), _tile(k, MM_TILE)
    if mode == "tn":
        tk = _tile(k, 2 * MM_TILE)
    elif k == tk:
        tm = _tile(m, 2 * MM_TILE)
    nk = k // tk
    dims = {"nn": ((1,), (0,)), "nt": ((1,), (1,)), "tn": ((0,), (0,))}[mode]
    a_spec = pl.BlockSpec((tk, tm), lambda i, j, s: (s, i)) if mode == "tn" else pl.BlockSpec((tm, tk), lambda i, j, s: (i, s))
    b_spec = pl.BlockSpec((tn, tk), lambda i, j, s: (j, s)) if mode == "nt" else pl.BlockSpec((tk, tn), lambda i, j, s: (s, j))
    o_spec = pl.BlockSpec((tm, tn), lambda i, j, s: (i, j))
    has_res = residual is not None

    def finish(out, r_ref, o_ref):
        if has_res:
            out = out + r_ref[...]
        o_ref[...] = out.astype(out_dtype)

    def body_one_pass(*refs):
        finish(_dot(refs[0][...], refs[1][...], dims), refs[2] if has_res else None, refs[-1])

    def body_reduce(*refs):
        a_ref, b_ref = refs[0], refs[1]
        r_ref = refs[2] if has_res else None
        o_ref, acc_ref = refs[-2], refs[-1]
        s = pl.program_id(2)

        @pl.when(s == 0)
        def _():
            acc_ref[...] = jnp.zeros_like(acc_ref)

        acc_ref[...] += _dot(a_ref[...], b_ref[...], dims)

        @pl.when(s == nk - 1)
        def _():
            finish(acc_ref[...], r_ref, o_ref)

    args, specs = [a, b], [a_spec, b_spec]
    if has_res:
        args.append(residual)
        specs.append(o_spec)
    return pl.pallas_call(
        body_one_pass if nk == 1 else body_reduce, name=name, grid=(m // tm, n // tn, nk), in_specs=specs, out_specs=o_spec,
        out_shape=jax.ShapeDtypeStruct((m, n), out_dtype),
        scratch_shapes=[] if nk == 1 else [pltpu.VMEM((tm, tn), F32)],
        compiler_params=_cparams("parallel", "parallel", "arbitrary"))(*args)


def _matmul_nt_sum(pairs, name, exchange=None):
    m, n = pairs[0][0].shape[0], pairs[0][1].shape[0]
    tm, tn = _tile(m, MM_TILE), _tile(n, MM_TILE)
    tks = [_tile(a.shape[1], MM_TILE) for a, _ in pairs]
    nks = [a.shape[1] // tk for (a, _), tk in zip(pairs, tks)]
    offs = [sum(nks[:i]) for i in range(len(pairs))]
    total = sum(nks)

    def body(*refs):
        o_ref, acc_ref = refs[-2], refs[-1]
        s = pl.program_id(2)

        @pl.when(s == 0)
        def _():
            acc_ref[...] = jnp.zeros_like(acc_ref)

        for i, (off, nk) in enumerate(zip(offs, nks)):
            @pl.when((s >= off) & (s < off + nk))
            def _(i=i):
                acc_ref[...] += _dot(refs[2 * i][...], refs[2 * i + 1][...], ((1,), (1,)))

        @pl.when(s == total - 1)
        def _():
            o_ref[...] = acc_ref[...]

    args, specs = [], []
    for (a, b), tk, off, nk in zip(pairs, tks, offs, nks):
        k_of = lambda s, off=off, nk=nk: jnp.clip(s - off, 0, nk - 1)
        args += [a, b]
        specs += [pl.BlockSpec((tm, tk), lambda i, j, s, k_of=k_of: (i, k_of(s))),
                  pl.BlockSpec((tn, tk), lambda i, j, s, k_of=k_of: (j, k_of(s)))]
    (out,), exchanged = _call_with_exchange(
        body, name=name, grid=(m // tm, n // tn, total), in_specs=specs,
        out_specs=[pl.BlockSpec((tm, tn), lambda i, j, s: (i, j))], out_shape=[jax.ShapeDtypeStruct((m, n), F32)],
        scratch_shapes=[pltpu.VMEM((tm, tn), F32)], args=args, exchange=exchange)
    return out, exchanged


def _rmsnorm_fwd(x, w, name):
    t = x.shape[0]
    blk = _tile(t, 1024)

    def body(x_ref, w_ref, h_ref):
        xv = x_ref[...]
        h_ref[...] = (xv * lax.rsqrt(jnp.mean(xv * xv, axis=1, keepdims=True) + NORM_EPS) * w_ref[...]).astype(BF16)

    return pl.pallas_call(
        body, name=name, grid=(t // blk,),
        in_specs=[pl.BlockSpec((blk, D), lambda i: (i, 0)), pl.BlockSpec((1, D), lambda i: (0, 0))],
        out_specs=pl.BlockSpec((blk, D), lambda i: (i, 0)), out_shape=jax.ShapeDtypeStruct((t, D), BF16),
        compiler_params=_cparams("parallel"))(x, w)


def _rmsnorm_bwd(dh, x, w, dxo, name, exchange=None):
    t = x.shape[0]
    blk = _tile(t, 512)

    def body(dh_ref, x_ref, w_ref, dxo_ref, dx_ref, dw_ref):
        @pl.when(pl.program_id(0) == 0)
        def _():
            dw_ref[...] = jnp.zeros_like(dw_ref)

        xv, dhv = x_ref[...], dh_ref[...]
        rs = lax.rsqrt(jnp.mean(xv * xv, axis=1, keepdims=True) + NORM_EPS)
        xh = xv * rs
        dw_ref[...] += jnp.sum(dhv * xh, axis=0, keepdims=True)
        dxh = dhv * w_ref[...]
        dx_ref[...] = rs * (dxh - xh * jnp.mean(dxh * xh, axis=1, keepdims=True)) + dxo_ref[...]

    row = pl.BlockSpec((blk, D), lambda i: (i, 0))
    vec = pl.BlockSpec((1, D), lambda i: (0, 0))
    return _call_with_exchange(
        body, name=name, grid=(t // blk,), in_specs=[row, row, vec, row], out_specs=[row, vec],
        out_shape=[jax.ShapeDtypeStruct((t, D), F32), jax.ShapeDtypeStruct((1, D), F32)],
        scratch_shapes=[], args=(dh, x, w, dxo), exchange=exchange)


def _loss_head(x, w, target, name):
    t = x.shape[0]
    blk = _tile(t, 512)

    def body(x_ref, w_ref, t_ref, loss_ref, dx_ref, dw_ref):
        @pl.when(pl.program_id(0) == 0)
        def _():
            dw_ref[...] = jnp.zeros_like(dw_ref)
            loss_ref[...] = jnp.zeros_like(loss_ref)

        xv = x_ref[...]
        rs = lax.rsqrt(jnp.mean(xv * xv, axis=1, keepdims=True) + NORM_EPS)
        xh = xv * rs
        err = xh * w_ref[...] - t_ref[...]
        loss_ref[...] += 0.5 * jnp.sum(jnp.mean(err * err, axis=1, keepdims=True), axis=0, keepdims=True)
        dy = err * (1.0 / D)
        dw_ref[...] += jnp.sum(dy * xh, axis=0, keepdims=True)
        dxh = dy * w_ref[...]
        dx_ref[...] = rs * (dxh - xh * jnp.mean(dxh * xh, axis=1, keepdims=True))

    row = pl.BlockSpec((blk, D), lambda i: (i, 0))
    vec = pl.BlockSpec((1, D), lambda i: (0, 0))
    return pl.pallas_call(
        body, name=name, grid=(t // blk,), in_specs=[row, vec, row],
        out_specs=[pl.BlockSpec((1, 1), lambda i: (0, 0)), row, vec],
        out_shape=[jax.ShapeDtypeStruct((1, 1), F32), jax.ShapeDtypeStruct((t, D), F32), jax.ShapeDtypeStruct((1, D), F32)],
        compiler_params=_cparams("arbitrary"))(x, w, target)


def _lbs_of(lb):
    r = lax.broadcasted_iota(jnp.int32, lb.shape, 0)
    real = r < DEPTH
    mx = lax.stop_gradient(jnp.max(jnp.where(real, lb, -jnp.inf), axis=0, keepdims=True))
    e = jnp.where(real, jnp.exp(jnp.where(real, lb - mx, 0.0)), 0.0)
    p = e / jnp.sum(e, axis=0, keepdims=True)
    out = jnp.zeros_like(lb)
    run = jnp.zeros_like(mx)
    for l in range(1, DEPTH):
        run = run + jnp.sum(jnp.where(r == l, p, 0.0), axis=0, keepdims=True)
        out = out + jnp.where(r == l, run, 0.0)
    return out


def _lower_bounds_fwd(lbp, name):
    def body(lb_ref, o_ref):
        o_ref[...] = _lbs_of(lb_ref[...])

    return pl.pallas_call(body, name=name, out_shape=jax.ShapeDtypeStruct(lbp.shape, F32))(lbp)


def _lower_bounds_bwd(lbp, dlbs, name):
    def body(lb_ref, d_ref, o_ref):
        _, vjp = jax.vjp(_lbs_of, lb_ref[...])
        o_ref[...] = vjp(d_ref[...])[0]

    return pl.pallas_call(body, name=name, out_shape=jax.ShapeDtypeStruct(lbp.shape, F32))(lbp, dlbs)


def _branch_a_fwd(pa, cw, name):
    t = pa.shape[0]
    blk = _tile(t, 512)
    W = CONV_W

    def body(p_ref, w_ref, y_ref, hc_ref, hx_ref):
        @pl.when(pl.program_id(0) == 0)
        def _():
            hc_ref[...] = jnp.zeros_like(hc_ref)
            hx_ref[...] = jnp.zeros_like(hx_ref)

        ac, ax = p_ref[:, W:2 * W], p_ref[:, 2 * W:3 * W]
        y_ref[...] = _conv_a_block(
            p_ref[:, 0:W], jnp.concatenate([hc_ref[...], ac], axis=0), jnp.concatenate([hx_ref[...], ax], axis=0),
            p_ref[:, 3 * W:4 * W], w_ref[0:1, :], w_ref[1:2, :], w_ref[2:3, :]).astype(BF16)
        hc_ref[...] = p_ref[blk - 8:blk, W:2 * W]
        hx_ref[...] = p_ref[blk - 8:blk, 2 * W:3 * W]

    return pl.pallas_call(
        body, name=name, grid=(t // blk,),
        in_specs=[pl.BlockSpec((blk, NA), lambda i: (i, 0)), pl.BlockSpec((3, W), lambda i: (0, 0))],
        out_specs=pl.BlockSpec((blk, W), lambda i: (i, 0)), out_shape=jax.ShapeDtypeStruct((t, W), BF16),
        scratch_shapes=[pltpu.VMEM((8, W), F32), pltpu.VMEM((8, W), F32)],
        compiler_params=_cparams("arbitrary"))(pa, cw)


def _branch_a_bwd(pa, cw, dy, name):
    t = pa.shape[0]
    blk = _tile(t, 512)
    nt_ = t // blk
    W = CONV_W
    hb = blk // 8

    def body(p_ref, halo_ref, w_ref, dy_ref, dp_ref, dw_ref, chc_ref, chx_ref):
        i = pl.program_id(0)

        @pl.when(i == 0)
        def _():
            chc_ref[...] = jnp.zeros_like(chc_ref)
            chx_ref[...] = jnp.zeros_like(chx_ref)
            dw_ref[...] = jnp.zeros_like(dw_ref)

        keep = 1.0 - (i == nt_ - 1).astype(F32)
        hc = halo_ref[:, W:2 * W] * keep
        hx = halo_ref[:, 2 * W:3 * W] * keep
        ac_ext = jnp.concatenate([hc, p_ref[:, W:2 * W]], axis=0)
        ax_ext = jnp.concatenate([hx, p_ref[:, 2 * W:3 * W]], axis=0)
        _, vjp = jax.vjp(_conv_a_block, p_ref[:, 0:W], ac_ext, ax_ext, p_ref[:, 3 * W:4 * W],
                         w_ref[0:1, :], w_ref[1:2, :], w_ref[2:3, :])
        dab, dac, dax, daz, dw0, dw1, dw2 = vjp(dy_ref[...].astype(F32))
        dp_ref[:, 0:W] = dab.astype(BF16)
        dp_ref[:, W:2 * W] = _add_to_tail(dac[8:], chc_ref[...]).astype(BF16)
        dp_ref[:, 2 * W:3 * W] = _add_to_tail(dax[8:], chx_ref[...]).astype(BF16)
        dp_ref[:, 3 * W:4 * W] = daz.astype(BF16)
        chc_ref[...] = dac[:8] * keep
        chx_ref[...] = dax[:8] * keep
        dw_ref[0:1, :] += dw0
        dw_ref[1:2, :] += dw1
        dw_ref[2:3, :] += dw2

    rev = lambda i: (nt_ - 1 - i, 0)
    return pl.pallas_call(
        body, name=name, grid=(nt_,),
        in_specs=[pl.BlockSpec((blk, NA), rev),
                  pl.BlockSpec((8, NA), lambda i: (jnp.maximum((nt_ - 1 - i) * hb - 1, 0), 0)),
                  pl.BlockSpec((3, W), lambda i: (0, 0)),
                  pl.BlockSpec((blk, W), rev)],
        out_specs=[pl.BlockSpec((blk, NA), rev), pl.BlockSpec((3, W), lambda i: (0, 0))],
        out_shape=[jax.ShapeDtypeStruct((t, NA), BF16), jax.ShapeDtypeStruct((3, W), F32)],
        scratch_shapes=[pltpu.VMEM((8, W), F32), pltpu.VMEM((8, W), F32)],
        compiler_params=_cparams("arbitrary"))(pa, pa, cw, dy)


def _block_rows(t, chunks):
    return min(t, chunks * CHUNK)


def _branch_b_fwd(pb, lbs_row, nw, name, exchange=None):
    t = pb.shape[0]
    rows = _block_rows(t, HGRN_BLOCK_CHUNKS)
    nch = t // rows

    def body(p_ref, lb_ref, nw_ref, y_ref, s_ref, st_ref):
        @pl.when(pl.program_id(1) == 0)
        def _():
            st_ref[...] = jnp.zeros_like(st_ref)

        s_ref[0, 0] = st_ref[...]
        y, st1 = _hgrn_block(p_ref[:, 0:HD], p_ref[:, HD:2 * HD], p_ref[:, 2 * HD:3 * HD], p_ref[:, 3 * HD:4 * HD],
                             st_ref[...], lb_ref[...], nw_ref[...])
        y_ref[...] = y.astype(BF16)
        st_ref[...] = st1

    return _call_with_exchange(
        body, name=name, grid=(HGRN_HEADS, nch),
        in_specs=[pl.BlockSpec((rows, 4 * HD), lambda h, i: (i, h)),
                  pl.BlockSpec((1, HD), lambda h, i: (0, h)),
                  pl.BlockSpec((1, HD), lambda h, i: (0, 0))],
        out_specs=[pl.BlockSpec((rows, HD), lambda h, i: (i, h)),
                   pl.BlockSpec((1, 1, HD, HD), lambda h, i: (h, i, 0, 0))],
        out_shape=[jax.ShapeDtypeStruct((t, HGRN_HEADS * HD), BF16),
                   jax.ShapeDtypeStruct((HGRN_HEADS, nch, HD, HD), F32)],
        scratch_shapes=[pltpu.VMEM((HD, HD), F32)],
        args=(pb, lbs_row, nw), exchange=exchange)


def _branch_b_bwd(pb, states, lbs_row, nw, dy, name, exchange=None):
    t = pb.shape[0]
    rows = _block_rows(t, HGRN_BLOCK_CHUNKS)
    nch = t // rows

    def body(p_ref, s_ref, lb_ref, nw_ref, dy_ref, dp_ref, dlb_ref, dnw_ref, ds_ref):
        h, i = pl.program_id(0), pl.program_id(1)

        @pl.when(i == 0)
        def _():
            ds_ref[...] = jnp.zeros_like(ds_ref)
            dlb_ref[...] = jnp.zeros_like(dlb_ref)

        @pl.when((i == 0) & (h == 0))
        def _():
            dnw_ref[...] = jnp.zeros_like(dnw_ref)

        _, vjp = jax.vjp(_hgrn_block, p_ref[:, 0:HD], p_ref[:, HD:2 * HD], p_ref[:, 2 * HD:3 * HD],
                         p_ref[:, 3 * HD:4 * HD], s_ref[0, 0], lb_ref[...], nw_ref[...])
        dq, df, di, dz, ds0, dlb, dnw = vjp((dy_ref[...].astype(F32), ds_ref[...]))
        dp_ref[:, 0:HD] = dq.astype(BF16)
        dp_ref[:, HD:2 * HD] = df.astype(BF16)
        dp_ref[:, 2 * HD:3 * HD] = di.astype(BF16)
        dp_ref[:, 3 * HD:4 * HD] = dz.astype(BF16)
        ds_ref[...] = ds0
        dlb_ref[...] += dlb
        dnw_ref[...] += dnw

    rev = lambda h, i: (nch - 1 - i, h)
    return _call_with_exchange(
        body, name=name, grid=(HGRN_HEADS, nch),
        in_specs=[pl.BlockSpec((rows, 4 * HD), rev),
                  pl.BlockSpec((1, 1, HD, HD), lambda h, i: (h, nch - 1 - i, 0, 0)),
                  pl.BlockSpec((1, HD), lambda h, i: (0, h)),
                  pl.BlockSpec((1, HD), lambda h, i: (0, 0)),
                  pl.BlockSpec((rows, HD), rev)],
        out_specs=[pl.BlockSpec((rows, 4 * HD), rev),
                   pl.BlockSpec((1, HD), lambda h, i: (0, h)),
                   pl.BlockSpec((1, HD), lambda h, i: (0, 0))],
        out_shape=[jax.ShapeDtypeStruct((t, NB), BF16), jax.ShapeDtypeStruct((1, HGRN_HEADS * HD), F32),
                   jax.ShapeDtypeStruct((1, HD), F32)],
        scratch_shapes=[pltpu.VMEM((HD, HD), F32)],
        args=(pb, states, lbs_row, nw, dy), exchange=exchange)


def _branch_c_fwd(pc, cw, cpar, nw, name, exchange=None):
    t = pc.shape[0]
    back_rows = _block_rows(t, GDN_BLOCK_CHUNKS)
    rows = _block_rows(t, GDN_FORWARD_CHUNKS)
    nch, per_step = t // rows, rows // back_rows
    XW = 4 * HD
    nsys = 2 * back_rows // GROUP

    def body(p_ref, w_ref, cp_ref, nw_ref, y_ref, s_ref, x_ref, sa_ref, sb_ref, halo_ref):
        @pl.when(pl.program_id(1) == 0)
        def _():
            sa_ref[...] = jnp.zeros_like(sa_ref)
            sb_ref[...] = jnp.zeros_like(sb_ref)
            halo_ref[...] = jnp.zeros_like(halo_ref)

        x_ext = jnp.concatenate([halo_ref[...], p_ref[:, 0:XW]], axis=0)
        (y, s1a, s1b), (xs, s_at) = _gdn_block(
            x_ext, p_ref[:, XW:XW + 2 * HD], p_ref[:, XW + 2 * HD:XW + 3 * HD],
            sa_ref[...], sb_ref[...], w_ref[0:1, :], w_ref[1:2, :], w_ref[2:3, :], w_ref[3:4, :],
            cp_ref[0, 0:1, :], cp_ref[0, 1:2, :], nw_ref[...])
        for b in range(per_step):
            for i in range(2):
                s_ref[0, b, i] = s_at[b * (back_rows // CHUNK), i]
            for n in range(nsys):
                x_ref[0, b, n] = xs[b * nsys + n]
        y_ref[...] = y.astype(BF16)
        sa_ref[...] = s1a
        sb_ref[...] = s1b
        halo_ref[...] = p_ref[rows - 8:rows, 0:XW]

    return _call_with_exchange(
        body, name=name, grid=(GDN_QK_HEADS, nch),
        in_specs=[pl.BlockSpec((rows, C_HEAD), lambda h, i: (i, h)),
                  pl.BlockSpec((4, XW), lambda h, i: (0, h)),
                  pl.BlockSpec((1, 8, HD), lambda h, i: (h, 0, 0)),
                  pl.BlockSpec((1, HD), lambda h, i: (0, 0))],
        out_specs=[pl.BlockSpec((rows, 2 * HD), lambda h, i: (i, h)),
                   pl.BlockSpec((1, per_step, 2, HD, HD), lambda h, i: (h, i, 0, 0, 0)),
                   pl.BlockSpec((1, per_step, nsys, GROUP, GROUP), lambda h, i: (h, i, 0, 0, 0))],
        out_shape=[jax.ShapeDtypeStruct((t, 2 * GDN_QK_HEADS * HD), BF16),
                   jax.ShapeDtypeStruct((GDN_QK_HEADS, nch * per_step, 2, HD, HD), F32),
                   jax.ShapeDtypeStruct((GDN_QK_HEADS, nch * per_step, nsys, GROUP, GROUP), BF16)],
        scratch_shapes=[pltpu.VMEM((HD, HD), F32), pltpu.VMEM((HD, HD), F32), pltpu.VMEM((8, XW), F32)],
        args=(pc, cw, cpar, nw), exchange=exchange)


def _branch_c_bwd(pc, states, inverses, cw, cpar, nw, dy, name, exchange=None):
    t = pc.shape[0]
    rows = _block_rows(t, GDN_BLOCK_CHUNKS)
    nch = t // rows
    XW = 4 * HD
    hb = rows // 8

    nsys = 2 * rows // GROUP

    def body(p_ref, halo_ref, s_ref, x_ref, w_ref, cp_ref, nw_ref, dy_ref, dp_ref, dw_ref, dcp_ref, dnw_ref,
             dsa_ref, dsb_ref, carry_ref):
        h, i = pl.program_id(0), pl.program_id(1)

        @pl.when(i == 0)
        def _():
            dsa_ref[...] = jnp.zeros_like(dsa_ref)
            dsb_ref[...] = jnp.zeros_like(dsb_ref)
            carry_ref[...] = jnp.zeros_like(carry_ref)
            dw_ref[...] = jnp.zeros_like(dw_ref)
            dcp_ref[...] = jnp.zeros_like(dcp_ref)

        @pl.when((i == 0) & (h == 0))
        def _():
            dnw_ref[...] = jnp.zeros_like(dnw_ref)

        keep = 1.0 - (i == nch - 1).astype(F32)
        x_ext = jnp.concatenate([halo_ref[:, 0:XW] * keep, p_ref[:, 0:XW]], axis=0)
        block = functools.partial(_gdn_block, known=tuple(x_ref[0, 0, n] for n in range(nsys)))
        _, vjp, _ = jax.vjp(block, x_ext, p_ref[:, XW:XW + 2 * HD], p_ref[:, XW + 2 * HD:XW + 3 * HD],
                            s_ref[0, 0, 0], s_ref[0, 0, 1], w_ref[0:1, :], w_ref[1:2, :], w_ref[2:3, :], w_ref[3:4, :],
                            cp_ref[0, 0:1, :], cp_ref[0, 1:2, :], nw_ref[...], has_aux=True)
        dx, dz, dba, dsa, dsb, dw0, dw1, dw2, dw3, dal, ddt, dnw = vjp((dy_ref[...].astype(F32), dsa_ref[...], dsb_ref[...]))
        dp_ref[:, 0:XW] = _add_to_tail(dx[8:], carry_ref[...]).astype(BF16)
        dp_ref[:, XW:XW + 2 * HD] = dz.astype(BF16)
        dp_ref[:, XW + 2 * HD:XW + 3 * HD] = dba.astype(BF16)
        carry_ref[...] = dx[:8] * keep
        dsa_ref[...] = dsa
        dsb_ref[...] = dsb
        dw_ref[0:1, :] += dw0
        dw_ref[1:2, :] += dw1
        dw_ref[2:3, :] += dw2
        dw_ref[3:4, :] += dw3
        dcp_ref[0, 0:1, :] += dal
        dcp_ref[0, 1:2, :] += ddt
        dnw_ref[...] += dnw

    rev = lambda h, i: (nch - 1 - i, h)
    return _call_with_exchange(
        body, name=name, grid=(GDN_QK_HEADS, nch),
        in_specs=[pl.BlockSpec((rows, C_HEAD), rev),
                  pl.BlockSpec((8, C_HEAD), lambda h, i: (jnp.maximum((nch - 1 - i) * hb - 1, 0), h)),
                  pl.BlockSpec((1, 1, 2, HD, HD), lambda h, i: (h, nch - 1 - i, 0, 0, 0)),
                  pl.BlockSpec((1, 1, nsys, GROUP, GROUP), lambda h, i: (h, nch - 1 - i, 0, 0, 0)),
                  pl.BlockSpec((4, XW), lambda h, i: (0, h)),
                  pl.BlockSpec((1, 8, HD), lambda h, i: (h, 0, 0)),
                  pl.BlockSpec((1, HD), lambda h, i: (0, 0)),
                  pl.BlockSpec((rows, 2 * HD), rev)],
        out_specs=[pl.BlockSpec((rows, C_HEAD), rev),
                   pl.BlockSpec((4, XW), lambda h, i: (0, h)),
                   pl.BlockSpec((1, 8, HD), lambda h, i: (h, 0, 0)),
                   pl.BlockSpec((1, HD), lambda h, i: (0, 0))],
        out_shape=[jax.ShapeDtypeStruct((t, NC_COLS), BF16), jax.ShapeDtypeStruct((4, GDN_QK_HEADS * XW), F32),
                   jax.ShapeDtypeStruct((GDN_QK_HEADS, 8, HD), F32), jax.ShapeDtypeStruct((1, HD), F32)],
        scratch_shapes=[pltpu.VMEM((HD, HD), F32), pltpu.VMEM((HD, HD), F32), pltpu.VMEM((8, XW), F32)],
        args=(pc, pc, states, inverses, cw, cpar, nw, dy), exchange=exchange)


def _merge_fwd(pg, bg, ya, yb, yc, name):
    t = pg.shape[0]
    blk = _tile(t, 512)

    def body(g_ref, b_ref, a_ref, b2_ref, c_ref, o_ref):
        gate = _sigmoid(g_ref[...] + b_ref[...])
        o_ref[...] = (gate[:, 0:D] * a_ref[...] + gate[:, D:2 * D] * b2_ref[...] + gate[:, 2 * D:3 * D] * c_ref[...]).astype(BF16)

    row = pl.BlockSpec((blk, D), lambda i: (i, 0))
    return pl.pallas_call(
        body, name=name, grid=(t // blk,),
        in_specs=[pl.BlockSpec((blk, NG), lambda i: (i, 0)), pl.BlockSpec((1, NG), lambda i: (0, 0)), row, row, row],
        out_specs=row, out_shape=jax.ShapeDtypeStruct((t, D), BF16),
        compiler_params=_cparams("parallel"))(pg, bg, ya, yb, yc)


def _merge_bwd(dm, pg, bg, ya, yb, yc, name):
    t = pg.shape[0]
    blk = _tile(t, 512)

    def body(dm_ref, g_ref, b_ref, a_ref, b2_ref, c_ref, dg_ref, da_ref, db_ref, dc_ref, dbg_ref):
        @pl.when(pl.program_id(0) == 0)
        def _():
            dbg_ref[...] = jnp.zeros_like(dbg_ref)

        gate = _sigmoid(g_ref[...] + b_ref[...])
        dmv = dm_ref[...].astype(F32)
        for j, (y_ref, dy_ref) in enumerate(((a_ref, da_ref), (b2_ref, db_ref), (c_ref, dc_ref))):
            gj = gate[:, j * D:(j + 1) * D]
            dy_ref[...] = (dmv * gj).astype(BF16)
            dgj = dmv * y_ref[...] * gj * (1.0 - gj)
            dg_ref[:, j * D:(j + 1) * D] = dgj.astype(BF16)
            dbg_ref[:, j * D:(j + 1) * D] += jnp.sum(dgj, axis=0, keepdims=True)

    row = pl.BlockSpec((blk, D), lambda i: (i, 0))
    wide = pl.BlockSpec((blk, NG), lambda i: (i, 0))
    vec = pl.BlockSpec((1, NG), lambda i: (0, 0))
    return pl.pallas_call(
        body, name=name, grid=(t // blk,), in_specs=[row, wide, vec, row, row, row],
        out_specs=[wide, row, row, row, vec],
        out_shape=[jax.ShapeDtypeStruct((t, NG), BF16)] + [jax.ShapeDtypeStruct((t, D), BF16)] * 3
                  + [jax.ShapeDtypeStruct((1, NG), F32)],
        compiler_params=_cparams("arbitrary"))(dm, pg, bg, ya, yb, yc)


def _adamw_math(w, g, m, v):
    m = ADAM_B1 * m + (1.0 - ADAM_B1) * g
    v = ADAM_B2 * v + (1.0 - ADAM_B2) * (g * g)
    m_hat = m / (1.0 - ADAM_B1 ** ADAM_STEP)
    v_hat = v / (1.0 - ADAM_B2 ** ADAM_STEP)
    delta = -ADAM_LR * (m_hat / (jnp.sqrt(v_hat) + ADAM_EPS) + ADAM_WD * w)
    return delta, m, v


def _sum_adamw(parts, w, m, v, name):
    layers = len(parts)
    r, c = parts[0].shape[1:]
    br = r if r <= 256 else 256
    nb = r // br
    assert r % br == 0 and w.shape == (layers * r, c)

    def body(*refs):
        w_ref, m_ref, v_ref, g_ref, d_ref, nm_ref, nv_ref = refs[layers:]
        for l in range(layers):
            @pl.when(pl.program_id(0) == l)
            def _(p_ref=refs[l]):
                g = p_ref[0].astype(F32)
                for k in range(1, N_DEV):
                    g = g + p_ref[k].astype(F32)
                g_ref[...] = g
                d_ref[...], nm_ref[...], nv_ref[...] = _adamw_math(w_ref[...], g, m_ref[...], v_ref[...])

    blk = pl.BlockSpec((br, c), lambda l, i: (l * nb + i, 0))
    part_specs = [pl.BlockSpec((N_DEV, br, c), lambda l, i, q=q: (0, jnp.where(l == q, i, jnp.where(l < q, 0, nb - 1)), 0))
                  for q in range(layers)]
    return pl.pallas_call(
        body, name=name, grid=(layers, nb), in_specs=part_specs + [blk, blk, blk], out_specs=[blk] * 4,
        out_shape=[jax.ShapeDtypeStruct((layers * r, c), F32)] * 4,
        compiler_params=_cparams("arbitrary", "arbitrary"))(*parts, w, m, v)


def _adamw(g, w, m, v, name):
    def body(g_ref, w_ref, m_ref, v_ref, d_ref, nm_ref, nv_ref):
        d_ref[...], nm_ref[...], nv_ref[...] = _adamw_math(w_ref[...], g_ref[...], m_ref[...], v_ref[...])

    return pl.pallas_call(body, name=name, out_shape=[jax.ShapeDtypeStruct(w.shape, F32)] * 3)(g, w, m, v)


def _sum_slots(parts, name):
    def body(p_ref, o_ref):
        g = p_ref[0]
        for k in range(1, N_DEV):
            g = g + p_ref[k]
        o_ref[...] = g

    return pl.pallas_call(body, name=name, out_shape=jax.ShapeDtypeStruct(parts.shape[1:], F32))(parts)


def _exchange(srcs, name, broadcast):
    n = len(srcs)

    def body(*refs):
        copies = _exchange_copies(refs[:n], refs[n:2 * n], *refs[2 * n:], broadcast)
        for cp in copies:
            cp.start()
        for cp in copies:
            cp.wait()

    return pl.pallas_call(
        body, name=name, in_specs=[HBM_SPEC] * n, out_specs=[HBM_SPEC] * n, out_shape=_exchange_shapes(srcs, broadcast),
        scratch_shapes=_exchange_semaphores(n))(*srcs)


def _gather_two_level(srcs, name):
    n = len(srcs)

    def body(*refs):
        src_refs, dst_refs = refs[:n], refs[n:2 * n]
        send_sems, recv_sems, local_sems = refs[2 * n:]
        x, y, c = lax.axis_index("x"), lax.axis_index("y"), lax.axis_index("c")
        index_of = lambda px, py, pc: 4 * px + 2 * py + pc
        me, other_core = index_of(x, y, c), (x, y, 1 - c)
        chips = [(1 - x, y), (x, 1 - y), (1 - x, 1 - y)]

        def copy(k, a, block, to, src=None):
            return pltpu.make_async_remote_copy(
                src_ref=dst_refs[a].at[block] if src is None else src, dst_ref=dst_refs[a].at[block],
                send_sem=send_sems.at[k, a], recv_sem=recv_sems.at[k, a], device_id=to, device_id_type=MESH)

        local = [pltpu.make_async_copy(src_refs[a], dst_refs[a].at[me], local_sems.at[a]) for a in range(n)]
        first = [copy(0, a, me, other_core, src=src_refs[a]) for a in range(n)]
        first += [copy(1 + j, a, me, (*chip, c), src=src_refs[a]) for j, chip in enumerate(chips) for a in range(n)]
        for cp in local + first:
            cp.start()
        passed = []
        for j, chip in enumerate(chips):
            block = index_of(*chip, c)
            for a in range(n):
                copy(1 + j, a, block, (x, y, c)).wait_recv()
            for a in range(n):
                passed.append(copy(4 + j, a, block, other_core))
                passed[-1].start()
        for a in range(n):
            copy(0, a, index_of(x, y, 1 - c), (x, y, c)).wait_recv()
        for j, chip in enumerate(chips):
            for a in range(n):
                copy(4 + j, a, index_of(*chip, 1 - c), (x, y, c)).wait_recv()
        for cp in first + passed:
            cp.wait_send()
        for cp in local:
            cp.wait()

    return pl.pallas_call(
        body, name=name, in_specs=[HBM_SPEC] * n, out_specs=[HBM_SPEC] * n, out_shape=_exchange_shapes(srcs, True),
        scratch_shapes=_exchange_semaphores(n))(*srcs)


HBM_SPEC = pl.BlockSpec(memory_space=pltpu.HBM)


def _exchange_shapes(srcs, broadcast):
    return [jax.ShapeDtypeStruct((N_DEV,) + (s.shape if broadcast else s.shape[1:]), s.dtype) for s in srcs]


def _exchange_semaphores(n):
    return [pltpu.SemaphoreType.DMA((N_DEV - 1, n)), pltpu.SemaphoreType.DMA((N_DEV - 1, n)), pltpu.SemaphoreType.DMA((n,))]


def _exchange_copies(src_refs, dst_refs, send_sems, recv_sems, local_sems, broadcast):
    x, y, c = lax.axis_index("x"), lax.axis_index("y"), lax.axis_index("c")
    me = 4 * x + 2 * y + c
    copies = []
    for k in range(1, N_DEV):
        px = 1 - x if (k >> 2) & 1 else x
        py = 1 - y if (k >> 1) & 1 else y
        pc = 1 - c if k & 1 else c
        peer = 4 * px + 2 * py + pc
        for a, (src, dst) in enumerate(zip(src_refs, dst_refs)):
            copies.append(pltpu.make_async_remote_copy(
                src_ref=src if broadcast else src.at[peer], dst_ref=dst.at[me],
                send_sem=send_sems.at[k - 1, a], recv_sem=recv_sems.at[k - 1, a],
                device_id=(px, py, pc), device_id_type=MESH))
    for a, (src, dst) in enumerate(zip(src_refs, dst_refs)):
        copies.append(pltpu.make_async_copy(src if broadcast else src.at[me], dst.at[me], local_sems.at[a]))
    return copies


def _call_with_exchange(body, *, name, grid, in_specs, out_specs, out_shape, scratch_shapes, args, exchange):
    if exchange is None:
        outs = pl.pallas_call(body, name=name, grid=grid, in_specs=in_specs, out_specs=out_specs, out_shape=out_shape,
                              scratch_shapes=scratch_shapes,
                              compiler_params=_cparams(*["arbitrary"] * len(grid)))(*args)
        return outs, None
    srcs, broadcast = exchange
    n, n_in, n_out, n_scr = len(srcs), len(args), len(out_shape), len(scratch_shapes)
    steps = 1
    for g in grid:
        steps *= g

    def hosted(*refs):
        ins, src_refs = refs[:n_in], refs[n_in:n_in + n]
        outs, dst_refs = refs[n_in + n:n_in + n + n_out], refs[n_in + n + n_out:n_in + 2 * n + n_out]
        scratch = refs[n_in + 2 * n + n_out:]
        step = pl.program_id(0)
        for axis in range(1, len(grid)):
            step = step * grid[axis] + pl.program_id(axis)

        @pl.when(step == 0)
        def _():
            for cp in _exchange_copies(src_refs, dst_refs, *scratch[n_scr:], broadcast):
                cp.start()

        body(*ins, *outs, *scratch[:n_scr])

        @pl.when(step == steps - 1)
        def _():
            for cp in _exchange_copies(src_refs, dst_refs, *scratch[n_scr:], broadcast):
                cp.wait()

    outs = pl.pallas_call(
        hosted, name=name, grid=grid, in_specs=list(in_specs) + [HBM_SPEC] * n, out_specs=list(out_specs) + [HBM_SPEC] * n,
        out_shape=list(out_shape) + _exchange_shapes(srcs, broadcast),
        scratch_shapes=list(scratch_shapes) + _exchange_semaphores(n),
        compiler_params=_cparams(*["arbitrary"] * len(grid)))(*args, *srcs)
    return outs[:n_out], outs[n_out:]


def _regroup_w_in(w):
    wa = w[:, OFF_A:OFF_A + NA]
    seg = lambda off, h, n=HD: w[:, off + h * n: off + (h + 1) * n]
    wb = jnp.concatenate([seg(OFF_B + s * 512, h) for h in range(HGRN_HEADS) for s in range(4)], axis=1)
    parts = []
    for h in range(GDN_QK_HEADS):
        small = jnp.concatenate(
            [w[:, OFF_BETA + 2 * h: OFF_BETA + 2 * h + 2], w[:, OFF_CA + 2 * h: OFF_CA + 2 * h + 2],
             jnp.zeros((w.shape[0], HD - 4), w.dtype)], axis=1)
        parts += [seg(OFF_CQ, h), seg(OFF_CK, h), seg(OFF_CV, h, 2 * HD), seg(OFF_CZ, h, 2 * HD), small]
    wc = jnp.concatenate(parts, axis=1)
    wg = w[:, OFF_G:OFF_G + NG]
    return wa, wb, wc, wg


def _ungroup_dw_in(da, db, dc, dg):
    bq = [jnp.concatenate([db[:, h * 512 + s * HD: h * 512 + (s + 1) * HD] for h in range(HGRN_HEADS)], axis=1)
          for s in range(4)]
    ch = lambda h, lo, hi: dc[:, h * C_HEAD + lo: h * C_HEAD + hi]
    heads = range(GDN_QK_HEADS)
    cq = jnp.concatenate([ch(h, 0, HD) for h in heads], axis=1)
    ck = jnp.concatenate([ch(h, HD, 2 * HD) for h in heads], axis=1)
    cv = jnp.concatenate([ch(h, 2 * HD, 4 * HD) for h in heads], axis=1)
    cz = jnp.concatenate([ch(h, 4 * HD, 6 * HD) for h in heads], axis=1)
    cbeta = jnp.concatenate([ch(h, 6 * HD, 6 * HD + 2) for h in heads], axis=1)
    ca = jnp.concatenate([ch(h, 6 * HD + 2, 6 * HD + 4) for h in heads], axis=1)
    return jnp.concatenate([da] + bq + [cq, ck, cv, cbeta, ca, cz, dg], axis=1)


def _regroup_conv_c(cw):
    parts = []
    for h in range(GDN_QK_HEADS):
        parts += [cw[:, h * HD:(h + 1) * HD], cw[:, 512 + h * HD: 512 + (h + 1) * HD],
                  cw[:, 1024 + 2 * h * HD: 1024 + (2 * h + 2) * HD]]
    return jnp.concatenate(parts, axis=1)


def _ungroup_conv_c(d):
    heads = range(GDN_QK_HEADS)
    q = jnp.concatenate([d[:, h * 512: h * 512 + HD] for h in heads], axis=1)
    k = jnp.concatenate([d[:, h * 512 + HD: h * 512 + 2 * HD] for h in heads], axis=1)
    v = jnp.concatenate([d[:, h * 512 + 2 * HD: h * 512 + 4 * HD] for h in heads], axis=1)
    return jnp.concatenate([q, k, v], axis=1)


def _numel(shape):
    n = 1
    for d in shape:
        n *= d
    return n


def _pack(arrays, rows):
    flat = jnp.concatenate([a.reshape(-1) for a in arrays])
    return jnp.pad(flat, (0, rows * 128 - flat.shape[0])).reshape(rows, 128)


def _unpack(packed, shapes):
    flat = packed.reshape(-1)
    out, off = [], 0
    for s in shapes:
        out.append(flat[off:off + _numel(s)].reshape(s))
        off += _numel(s)
    return out


def _rows_for(shapes):
    return -(-sum(_numel(s) for s in shapes) // 1024) * 8


def kernel(x, norm_w, w_in, b_gate, conv_a, conv_c, a_log, dt_bias, lower_bounds, hgrn_norm_w, gdn_norm_w, w_out_a, w_out_b, w_out_c, w_o, final_norm_w, loss_target, m_norm_w, m_w_in, m_b_gate, m_conv_a, m_conv_c, m_a_log, m_dt_bias, m_lower_bounds, m_hgrn_norm_w, m_gdn_norm_w, m_w_out_a, m_w_out_b, m_w_out_c, m_w_o, m_final_norm_w, v_norm_w, v_w_in, v_b_gate, v_conv_a, v_conv_c, v_a_log, v_dt_bias, v_lower_bounds, v_hgrn_norm_w, v_gdn_norm_w, v_w_out_a, v_w_out_b, v_w_out_c, v_w_o, v_final_norm_w):
    me = 4 * lax.axis_index("x") + 2 * lax.axis_index("y") + lax.axis_index("c")
    xs = x[0]
    target = loss_target[0]
    in_shard = w_in.shape[2]

    big = [w_in, w_out_a, w_out_b, w_out_c, w_o]
    shards_of = lambda l: [w[l].astype(BF16) for w in big]
    conv_shapes = [(DEPTH, 3, CONV_W), (DEPTH, 4, 2048)]
    conv_rows = _rows_for(conv_shapes)
    ca_full = lax.dynamic_update_slice(jnp.zeros(conv_shapes[0], F32), conv_a, (0, 0, me * conv_a.shape[2]))
    cc_full = lax.dynamic_update_slice(jnp.zeros(conv_shapes[1], F32), conv_c, (0, 0, me * conv_c.shape[2]))
    g_in0, conv_parts = _gather_two_level([w_in[0].astype(BF16), _pack([ca_full, cc_full], conv_rows)], "gather_l0")
    conv_a_full, conv_c_full = _unpack(_sum_slots(conv_parts, "sum_conv"), conv_shapes)

    lb_pad = jnp.pad(lower_bounds, ((0, 8 - DEPTH), (0, 0)))
    lbs = _lower_bounds_fwd(lb_pad, "lower_bounds_fwd")

    def input_weights(l, g_in):
        wa, wb, wc, wg = _regroup_w_in(jnp.concatenate([g_in[q] for q in range(N_DEV)], axis=1))
        lanes = lambda vec: jnp.pad(vec.reshape(GDN_QK_HEADS, 1, 2), ((0, 0), (0, 0), (0, HD - 2)))
        cpar = jnp.concatenate([lanes(a_log[l]), lanes(dt_bias[l]), jnp.zeros((GDN_QK_HEADS, 6, HD), F32)], axis=1)
        return dict(
            wa=wa, wb=wb, wc=wc, wg=wg, cpar=cpar,
            nw=norm_w[l:l + 1], bg=b_gate[l:l + 1], cwa=conv_a_full[l], cwc=_regroup_conv_c(conv_c_full[l]),
            lb=lbs[l:l + 1], hnw=hgrn_norm_w[l:l + 1], gnw=gdn_norm_w[l:l + 1])

    def output_weights(g_oa, g_ob, g_oc, g_o):
        return dict(woa=jnp.concatenate([g_oa[q] for q in range(N_DEV)], axis=1),
                    wob=jnp.concatenate([g_ob[q] for q in range(N_DEV)], axis=1), woc=g_oc.reshape(D, D), wo=g_o.reshape(D, D))

    layers = [input_weights(0, g_in0)]

    saved = []
    cur = xs
    for l in range(DEPTH):
        L = layers[l]
        n = f"l{l}_"
        h = _rmsnorm_fwd(cur, L["nw"], n + "rms")
        pa = _matmul(h, L["wa"], "nn", n + "proj_a")
        pb = _matmul(h, L["wb"], "nn", n + "proj_b")
        pc = _matmul(h, L["wc"], "nn", n + "proj_c")
        pg = _matmul(h, L["wg"], "nn", n + "proj_g", out_dtype=BF16)
        ua = _branch_a_fwd(pa, L["cwa"], n + "conv_fwd")
        carry = (shards_of(0)[1:] + shards_of(1)[1:], True) if l == 0 else None
        (ub, sb), gathered = _branch_b_fwd(pb, L["lb"], L["hnw"], n + "hgrn_fwd", exchange=carry)
        if carry is not None:
            L.update(output_weights(*gathered[:4]))
            next_outputs = output_weights(*gathered[4:])
        carry = (shards_of(1)[:1], True) if l == 0 else None
        (uc, sc, xc), gathered = _branch_c_fwd(pc, L["cwc"], L["cpar"], L["gnw"], n + "gdn_fwd", exchange=carry)
        if carry is not None:
            layers.append(dict(input_weights(1, gathered[0]), **next_outputs))
        ya = _matmul(ua, L["woa"], "nn", n + "out_a", out_dtype=BF16)
        yb = _matmul(ub, L["wob"], "nn", n + "out_b", out_dtype=BF16)
        yc = _matmul(uc, L["woc"], "nn", n + "out_c", out_dtype=BF16)
        merged = _merge_fwd(pg, L["bg"], ya, yb, yc, n + "merge")
        nxt = _matmul(merged, L["wo"], "nn", n + "out_o", residual=cur)
        saved.append(dict(x=cur, h=h, pa=pa, pb=pb, pc=pc, pg=pg, ua=ua, ub=ub, uc=uc, sb=sb, sc=sc, xc=xc,
                          ya=ya, yb=yb, yc=yc, merged=merged))
        cur = nxt

    loss_part, dx, d_final = _loss_head(cur, final_norm_w.reshape(1, D), target, "loss_head")

    def outgoing(g):
        cols = lambda a, n: jnp.stack([a[:, p * n:(p + 1) * n] for p in range(N_DEV)]).astype(BF16)
        rows = lambda a: a.reshape(N_DEV, a.shape[0] // N_DEV, a.shape[1]).astype(BF16)
        first = [cols(g["w_in"], in_shard)] if "w_in" in g else [None]
        if "w_o" not in g:
            return first
        return first + [cols(g["w_out_a"], 128), cols(g["w_out_b"], 128), rows(g["w_out_c"]), rows(g["w_o"])]

    grads = [None] * DEPTH
    dlbs_rows = [None] * DEPTH
    incoming = [None] * DEPTH
    for l in reversed(range(DEPTH)):
        L, S = layers[l], saved[l]
        n = f"l{l}_"
        dmerged = _matmul(dx, L["wo"], "nt", n + "d_merged", out_dtype=BF16)
        d_wo = _matmul(S["merged"], dx, "tn", n + "dw_o", out_dtype=BF16)
        dpg, dya, dyb, dyc, d_bg = _merge_bwd(dmerged, S["pg"], L["bg"], S["ya"], S["yb"], S["yc"], n + "merge_bwd")
        dua = _matmul(dya, L["woa"], "nt", n + "d_ua", out_dtype=BF16)
        dub = _matmul(dyb, L["wob"], "nt", n + "d_ub", out_dtype=BF16)
        duc = _matmul(dyc, L["woc"], "nt", n + "d_uc", out_dtype=BF16)
        d_woa = _matmul(S["ua"], dya, "tn", n + "dw_out_a", out_dtype=BF16)
        d_wob = _matmul(S["ub"], dyb, "tn", n + "dw_out_b", out_dtype=BF16)
        d_woc = _matmul(S["uc"], dyc, "tn", n + "dw_out_c", out_dtype=BF16)
        dpa, d_cwa = _branch_a_bwd(S["pa"], L["cwa"], dua, n + "conv_bwd")
        out_grads = dict(w_out_a=d_woa, w_out_b=d_wob, w_out_c=d_woc, w_o=d_wo)
        carry = (outgoing(out_grads)[1:], False) if l == 0 else None
        (dpb, d_lb, d_hnw), arrived_out = _branch_b_bwd(S["pb"], S["sb"], L["lb"], L["hnw"], dub, n + "hgrn_bwd", exchange=carry)
        carry = (outgoing(grads[l + 1]), False) if l + 1 < DEPTH else None
        (dpc, d_cwc, d_cpar, d_gnw), arrived = _branch_c_bwd(S["pc"], S["sc"], S["xc"], L["cwc"], L["cpar"], L["gnw"], duc,
                                                             n + "gdn_bwd", exchange=carry)
        if carry is not None:
            incoming[l + 1] = arrived
        d_win = _ungroup_dw_in(*[_matmul(S["h"], dp, "tn", f"{n}dw_{piece}", out_dtype=BF16)
                                 for dp, piece in ((dpa, "a"), (dpb, "b"), (dpc, "c"), (dpg, "g"))])
        carry = (outgoing(dict(w_in=d_win)), False) if l == 0 else None
        dh, arrived_in = _matmul_nt_sum([(dpa, L["wa"]), (dpb, L["wb"]), (dpc, L["wc"]), (dpg, L["wg"])], n + "dh",
                                        exchange=carry)
        (dx, d_nw), _ = _rmsnorm_bwd(dh, S["x"], L["nw"], dx, n + "rms_bwd")
        dlbs_rows[l] = d_lb
        grads[l] = dict(w_in=d_win, w_out_a=d_woa, w_out_b=d_wob, w_out_c=d_woc, w_o=d_wo, norm_w=d_nw[0],
                        b_gate=d_bg[0], conv_a=d_cwa, conv_c=_ungroup_conv_c(d_cwc),
                        a_log=d_cpar[:, 0, 0:2].reshape(-1), dt_bias=d_cpar[:, 1, 0:2].reshape(-1),
                        hgrn_norm_w=d_hnw[0], gdn_norm_w=d_gnw[0])
    grad_x = dx[None]
    d_lower = _lower_bounds_bwd(lb_pad, jnp.pad(jnp.concatenate(dlbs_rows, axis=0), ((0, 8 - DEPTH), (0, 0))),
                                "lower_bounds_bwd")[:DEPTH]

    incoming[0] = list(arrived_in) + list(arrived_out)
    stack = lambda name: jnp.stack([grads[l][name] for l in range(DEPTH)])
    big_out = {}
    for j, (name, w, m, v) in enumerate((("w_in", w_in, m_w_in, v_w_in), ("w_out_a", w_out_a, m_w_out_a, v_w_out_a),
                                         ("w_out_b", w_out_b, m_w_out_b, v_w_out_b), ("w_out_c", w_out_c, m_w_out_c, v_w_out_c),
                                         ("w_o", w_o, m_w_o, v_w_o))):
        parts = [incoming[l][j] for l in range(DEPTH)]
        r2 = lambda a: a.reshape(DEPTH * parts[0].shape[1], parts[0].shape[2])
        outs = _sum_adamw(parts, r2(w), r2(m), r2(v), "adamw_" + name)
        big_out[name] = [o.reshape(w.shape) for o in outs]

    small_names = ["norm_w", "b_gate", "conv_a", "conv_c", "a_log", "dt_bias", "lower_bounds", "hgrn_norm_w",
                   "gdn_norm_w", "final_norm_w", "loss"]
    small_vals = {k: stack(k) for k in ("norm_w", "b_gate", "conv_a", "conv_c", "a_log", "dt_bias", "hgrn_norm_w", "gdn_norm_w")}
    small_vals.update(lower_bounds=d_lower, final_norm_w=d_final[0], loss=loss_part.reshape(1))
    small_shapes = [small_vals[k].shape for k in small_names]
    small_rows = _rows_for(small_shapes)
    small_parts, = _exchange([_pack([small_vals[k] for k in small_names], small_rows)], "exchange_small", broadcast=True)
    total = dict(zip(small_names, _unpack(_sum_slots(small_parts, "sum_small"), small_shapes)))
    loss = total["loss"][0]
    g_conv_a = lax.dynamic_slice(total["conv_a"], (0, 0, me * conv_a.shape[2]), conv_a.shape)
    g_conv_c = lax.dynamic_slice(total["conv_c"], (0, 0, me * conv_c.shape[2]), conv_c.shape)

    small_w = dict(norm_w=(norm_w, m_norm_w, v_norm_w), b_gate=(b_gate, m_b_gate, v_b_gate),
                   conv_a=(conv_a, m_conv_a, v_conv_a), conv_c=(conv_c, m_conv_c, v_conv_c),
                   a_log=(a_log, m_a_log, v_a_log), dt_bias=(dt_bias, m_dt_bias, v_dt_bias),
                   lower_bounds=(lower_bounds, m_lower_bounds, v_lower_bounds),
                   hgrn_norm_w=(hgrn_norm_w, m_hgrn_norm_w, v_hgrn_norm_w), gdn_norm_w=(gdn_norm_w, m_gdn_norm_w, v_gdn_norm_w),
                   final_norm_w=(final_norm_w, m_final_norm_w, v_final_norm_w))
    small_g = dict(total, conv_a=g_conv_a, conv_c=g_conv_c)
    upd_names = small_names[:-1]
    upd_shapes = [small_w[k][0].shape for k in upd_names]
    upd_rows = _rows_for(upd_shapes)
    pk = lambda j: _pack([small_w[k][j] for k in upd_names], upd_rows)
    s_delta, s_m, s_v = _adamw(_pack([small_g[k] for k in upd_names], upd_rows), pk(0), pk(1), pk(2), "adamw_small")
    small_out = {k: [small_g[k], d, mm, vv] for k, d, mm, vv in
                 zip(upd_names, _unpack(s_delta, upd_shapes), _unpack(s_m, upd_shapes), _unpack(s_v, upd_shapes))}

    order = ["norm_w", "w_in", "b_gate", "conv_a", "conv_c", "a_log", "dt_bias", "lower_bounds", "hgrn_norm_w",
             "gdn_norm_w", "w_out_a", "w_out_b", "w_out_c", "w_o", "final_norm_w"]
    res = {**small_out, **big_out}
    outs = [loss, grad_x]
    for j in range(4):
        outs += [res[k][j] for k in order]
    return tuple(outs)
```

```python
import functools

import jax
import jax.numpy as jnp
from jax import lax
from jax.experimental import pallas as pl
from jax.experimental.pallas import tpu as pltpu

F32 = jnp.float32
BF16 = jnp.bfloat16
MESH = pl.DeviceIdType.MESH

N_DEV = 8
D = 1024
DEPTH = 2
CHUNK = 64
HGRN_BLOCK_CHUNKS = 16
GDN_BLOCK_CHUNKS = 8
GDN_FORWARD_CHUNKS = 16
GROUP = 128
NORM_EPS = 1e-6
L2_EPS = 1e-6
MIN_F = 1e-30
HD = 128
HGRN_HEADS = 4
GDN_QK_HEADS = 4
CONV_W = 512
IN_COLS = 10256
OFF_A, OFF_B, OFF_CQ, OFF_CK, OFF_CV, OFF_BETA, OFF_CA, OFF_CZ, OFF_G = (
    0, 2048, 4096, 4608, 5120, 6144, 6152, 6160, 7184)
NA, NB, NC_COLS, NG = 2048, 2048, 3584, 3072
C_HEAD = 896

ADAM_LR, ADAM_B1, ADAM_B2, ADAM_EPS, ADAM_WD, ADAM_STEP = 0.001, 0.9, 0.999, 1e-08, 0.01, 10

VMEM_LIMIT = 56 * 1024 * 1024
MM_TILE = 1024


def _cparams(*sem):
    return pltpu.CompilerParams(dimension_semantics=sem, vmem_limit_bytes=VMEM_LIMIT)


def _tile(dim, cap):
    if dim <= cap:
        return dim
    t = (cap // 128) * 128
    while dim % t:
        t -= 128
    return t


def _sigmoid(x):
    return 1.0 / (1.0 + jnp.exp(-x))


def _silu(x):
    return x * _sigmoid(x)


def _softplus(x):
    return jnp.maximum(x, 0.0) + jnp.log(1.0 + jnp.exp(-jnp.abs(x)))


def _dot(a, b, dims, precision=None):
    if precision is None:
        a, b = a.astype(BF16), b.astype(BF16)
    return lax.dot_general(a, b, (dims, ((), ())), precision=precision, preferred_element_type=F32)


def _nn(a, b, precision=None):
    return _dot(a, b, ((1,), (0,)), precision)


def _nt(a, b, precision=None):
    return _dot(a, b, ((1,), (1,)), precision)


def _tn(a, b, precision=None):
    return _dot(a, b, ((0,), (0,)), precision)


def _sum_rows_split(mat01, x):
    m = mat01.astype(BF16)
    hi = x.astype(BF16)
    low = (x - hi.astype(F32)).astype(BF16)
    return _nn(m, hi) + _nn(m, low)


@functools.partial(jax.custom_vjp, nondiff_argnums=(1,))
def _shift_rows(x, d):
    return x if d == 0 else pltpu.roll(x, d, 0)


def _shift_rows_fwd(x, d):
    return _shift_rows(x, d), None


def _shift_rows_bwd(d, _, ct):
    return ((ct if d == 0 else pltpu.roll(ct, ct.shape[0] - d, 0)),)


_shift_rows.defvjp(_shift_rows_fwd, _shift_rows_bwd)


def _iota2(shape):
    return lax.broadcasted_iota(jnp.int32, shape, 0), lax.broadcasted_iota(jnp.int32, shape, 1)


def _lane_pick(x, i):
    lane = lax.broadcasted_iota(jnp.int32, x.shape, 1)
    return jnp.sum(jnp.where(lane == i, x, 0.0), axis=1, keepdims=True)


def _hgrn_block(qr, fr, ir, zr, st0, lb, nw):
    rows = qr.shape[0]
    r, c = _iota2((CHUNK, CHUNK))
    halves = [1 << j for j in range(CHUNK.bit_length() - 1)]
    mats = [c <= r, c > r]
    pairs = []
    for hb in halves:
        same = (r // hb) == (c // hb)
        if hb > 1:
            mats += [(c <= r) & same, (c > r) & same]
        pairs.append(((r // (2 * hb)) == (c // (2 * hb))) & ((r // hb) == (c // hb) + 1))
    stack = jnp.concatenate([m.astype(F32) for m in mats], axis=0)

    q = _silu(qr) * (HD ** -0.5)
    fg = lb + (1.0 - lb) * _sigmoid(fr)
    logf = jnp.log(jnp.maximum(fg, MIN_F))
    kk = 1.0 - fg
    v = ir

    chunks = [slice(s, s + CHUNK) for s in range(0, rows, CHUNK)]
    cums = [_sum_rows_split(stack, logf[sl]) for sl in chunks]
    part = lambda i: jnp.concatenate([cs[i * CHUNK:(i + 1) * CHUNK] for cs in cums], axis=0)
    qg = q * jnp.exp(part(0))
    ks = kk * jnp.exp(part(1))
    q_lv = [q * jnp.exp(logf)] + [q * jnp.exp(part(2 * j)) for j in range(1, len(halves))]
    k_lv = [kk] + [kk * jnp.exp(part(2 * j + 1)) for j in range(1, len(halves))]
    st = st0
    outs = []
    for sl in chunks:
        scores = jnp.where(pairs[0], _nt(q_lv[0][sl], k_lv[0][sl]), 0.0)
        for j in range(1, len(halves)):
            scores += jnp.where(pairs[j], _nt(q_lv[j][sl], k_lv[j][sl]), 0.0)
        outs.append(_nn(scores, v[sl]) + _nt(qg[sl], st))
        st = st * jnp.exp(jnp.sum(logf[sl], axis=0, keepdims=True)) + _tn(v[sl], ks[sl])
    o = jnp.concatenate(outs, axis=0) + jnp.sum(q * kk, axis=1, keepdims=True) * v
    y = o * lax.rsqrt(jnp.mean(o * o, axis=1, keepdims=True) + NORM_EPS) * nw * _silu(zr)
    return y, st


def _unit_lower_inverses(ms):
    r, c = _iota2(ms[0].shape)
    xs = [jnp.where(r == c, 1.0, 0.0) - jnp.where((r // 2) == (c // 2), m, 0.0) for m in ms]
    b = 2
    while b < CHUNK:
        pick = ((r // (2 * b)) == (c // (2 * b))) & ((r // b) != (c // b))
        ts = [_nn(x, jnp.where(pick, m, 0.0)) for x, m in zip(xs, ms)]
        xs = [x - _nn(t, x) for x, t in zip(xs, ts)]
        b *= 2
    return tuple(x.astype(BF16) for x in xs)


@jax.custom_vjp
def _known_inverses(ms, xs):
    return xs


def _known_inverses_fwd(ms, xs):
    return xs, xs


def _known_inverses_bwd(xs, cts):
    r, c = _iota2(xs[0].shape)
    keep = (c < r) & ((r // CHUNK) == (c // CHUNK))
    ts = [_tn(x, ct) for x, ct in zip(xs, cts)]
    return (tuple(jnp.where(keep, -_nt(t, x), 0.0) for t, x in zip(ts, xs)), tuple(jnp.zeros_like(x) for x in xs))


_known_inverses.defvjp(_known_inverses_fwd, _known_inverses_bwd)


def _chunk_cumsum(x):
    row = lax.broadcasted_iota(jnp.int32, x.shape, 0) % CHUNK
    d = 1
    while d < CHUNK:
        x = x + jnp.where(row >= d, _shift_rows(x, d), 0.0)
        d *= 2
    return x


def _gdn_block(x_ext, z, ba, s0a, s0b, w0, w1, w2, w3, alog, dtb, nw, known=None):
    rows = z.shape[0]
    conv = (w0 * _shift_rows(x_ext, 3) + w1 * _shift_rows(x_ext, 2) + w2 * _shift_rows(x_ext, 1) + w3 * x_ext)
    cc = _silu(conv[8:])
    qc, kc = cc[:, 0:HD], cc[:, HD:2 * HD]
    q = qc * lax.rsqrt(jnp.sum(qc * qc, axis=1, keepdims=True) + L2_EPS) * (HD ** -0.5)
    k = kc * lax.rsqrt(jnp.sum(kc * kc, axis=1, keepdims=True) + L2_EPS)

    r, c = _iota2((GROUP, GROUP))
    same = (r // CHUNK) == (c // CHUNK)
    causal, strict, eye = same & (c <= r), same & (c < r), r == c
    heads = (0, 1)
    groups = [slice(lo, lo + GROUP) for lo in range(0, rows, GROUP)]
    chunks = [slice(lo, lo + CHUNK) for lo in range(0, rows, CHUNK)]

    v, loga, g_w, kb, kg, qg = [], [], [], [], [], []
    for i in heads:
        v.append(cc[:, (2 + i) * HD:(3 + i) * HD])
        beta = _sigmoid(_lane_pick(ba, i))
        a_neg = -jnp.exp(_lane_pick(alog, i))
        loga.append(a_neg * _softplus(_lane_pick(ba, 2 + i) + _lane_pick(dtb, i)))
        g_w.append(_chunk_cumsum(jnp.broadcast_to(loga[i], (rows, HD))))
        kb.append(k * beta)
        kg.append(k * jnp.exp(g_w[i]))
        qg.append(q * jnp.exp(g_w[i]))

    systems = [(i, gs) for gs in groups for i in heads]
    dec_c, ms = [], []
    for i, gs in systems:
        g_sq = g_w[i][gs]
        g_row = jnp.sum(jnp.where(eye, g_sq, 0.0), axis=0, keepdims=True)
        diff = g_sq - g_row
        dec_c.append(jnp.where(causal, jnp.exp(jnp.where(causal, diff, 0.0)), 0.0))
        ms.append(jnp.where(strict, _nt(k[gs], kb[i][gs]) * dec_c[-1], 0.0))
    xs = _unit_lower_inverses(tuple(ms)) if known is None else _known_inverses(tuple(ms), known)
    u = [[None] * len(groups) for _ in heads]
    w = [[None] * len(groups) for _ in heads]
    qk = [[None] * len(groups) for _ in heads]
    for n, (i, gs) in enumerate(systems):
        j = n // len(heads)
        u[i][j] = _nn(xs[n], v[i][gs])
        w[i][j] = _nn(xs[n], kg[i][gs])
        qk[i][j] = _nt(q[gs], kb[i][gs]) * dec_c[n]
    u = [jnp.concatenate(p, axis=0) for p in u]
    w = [jnp.concatenate(p, axis=0) for p in w]

    decay, p_mat, q_mat = {}, {}, {}
    for n, sl in enumerate(chunks):
        for i in heads:
            g_last = jnp.sum(loga[i][sl], axis=0, keepdims=True)
            kd = kb[i][sl] * jnp.exp(g_last - g_w[i][sl])
            decay[n, i] = jnp.exp(g_last)
            p_mat[n, i] = -_tn(kd, w[i][sl])
            q_mat[n, i] = _tn(kd, u[i][sl])
    s = [s0a, s0b]
    s_at = {}
    for n in range(len(chunks)):
        for i in heads:
            s_at[n, i] = s[i]
            s[i] = s[i] * decay[n, i] + _nn(p_mat[n, i], s[i]) + q_mat[n, i]

    ys = []
    for i in heads:
        e = jnp.concatenate([u[i][sl] - _nn(w[i][sl], s_at[n, i]) for n, sl in enumerate(chunks)], axis=0)
        o_state = jnp.concatenate([_nn(qg[i][sl], s_at[n, i]) for n, sl in enumerate(chunks)], axis=0)
        o = o_state + jnp.concatenate([_nn(qk[i][j], e[gs]) for j, gs in enumerate(groups)], axis=0)
        zi = z[:, i * HD:(i + 1) * HD]
        ys.append(o * lax.rsqrt(jnp.mean(o * o, axis=1, keepdims=True) + NORM_EPS) * nw * _silu(zi))
    return (jnp.concatenate(ys, axis=1), s[0], s[1]), (xs, s_at)


def _add_to_tail(x, tail):
    return x + jnp.concatenate([jnp.zeros((x.shape[0] - 8, x.shape[1]), x.dtype), tail], axis=0)


def _conv_a_block(ab, ac_ext, ax_ext, az, w0, w1, w2):
    u = ac_ext * ax_ext
    conv = (w0 * _shift_rows(u, 2) + w1 * _shift_rows(u, 1) + w2 * u)[8:]
    return ab * conv * _silu(az)


def _matmul(a, b, mode, name, residual=None, out_dtype=F32):
    if mode == "nn":
        (m, k), n = a.shape, b.shape[1]
    elif mode == "nt":
        (m, k), n = a.shape, b.shape[0]
    else:
        (k, m), n = a.shape, b.shape[1]
    tm, tn, tk = _tile(m, MM_TILE), _tile(n, MM_TILE), _tile(k, MM_TILE)
    if mode == "tn":
        tk = _tile(k, 2 * MM_TILE)
    elif k == tk:
        tm = _tile(m, 2 * MM_TILE)
    nk = k // tk
    dims = {"nn": ((1,), (0,)), "nt": ((1,), (1,)), "tn": ((0,), (0,))}[mode]
    a_spec = pl.BlockSpec((tk, tm), lambda i, j, s: (s, i)) if mode == "tn" else pl.BlockSpec((tm, tk), lambda i, j, s: (i, s))
    b_spec = pl.BlockSpec((tn, tk), lambda i, j, s: (j, s)) if mode == "nt" else pl.BlockSpec((tk, tn), lambda i, j, s: (s, j))
    o_spec = pl.BlockSpec((tm, tn), lambda i, j, s: (i, j))
    has_res = residual is not None

    def finish(out, r_ref, o_ref):
        if has_res:
            out = out + r_ref[...]
        o_ref[...] = out.astype(out_dtype)

    def body_one_pass(*refs):
        finish(_dot(refs[0][...], refs[1][...], dims), refs[2] if has_res else None, refs[-1])

    def body_reduce(*refs):
        a_ref, b_ref = refs[0], refs[1]
        r_ref = refs[2] if has_res else None
        o_ref, acc_ref = refs[-2], refs[-1]
        s = pl.program_id(2)

        @pl.when(s == 0)
        def _():
            acc_ref[...] = jnp.zeros_like(acc_ref)

        acc_ref[...] += _dot(a_ref[...], b_ref[...], dims)

        @pl.when(s == nk - 1)
        def _():
            finish(acc_ref[...], r_ref, o_ref)

    args, specs = [a, b], [a_spec, b_spec]
    if has_res:
        args.append(residual)
        specs.append(o_spec)
    return pl.pallas_call(
        body_one_pass if nk == 1 else body_reduce, name=name, grid=(m // tm, n // tn, nk), in_specs=specs, out_specs=o_spec,
        out_shape=jax.ShapeDtypeStruct((m, n), out_dtype),
        scratch_shapes=[] if nk == 1 else [pltpu.VMEM((tm, tn), F32)],
        compiler_params=_cparams("parallel", "parallel", "arbitrary"))(*args)


def _matmul_nt_sum(pairs, name, exchange=None):
    m, n = pairs[0][0].shape[0], pairs[0][1].shape[0]
    tm, tn = _tile(m, MM_TILE), _tile(n, MM_TILE)
    tks = [_tile(a.shape[1], MM_TILE) for a, _ in pairs]
    nks = [a.shape[1] // tk for (a, _), tk in zip(pairs, tks)]
    offs = [sum(nks[:i]) for i in range(len(pairs))]
    total = sum(nks)

    def body(*refs):
        o_ref, acc_ref = refs[-2], refs[-1]
        s = pl.program_id(2)

        @pl.when(s == 0)
        def _():
            acc_ref[...] = jnp.zeros_like(acc_ref)

        for i, (off, nk) in enumerate(zip(offs, nks)):
            @pl.when((s >= off) & (s < off + nk))
            def _(i=i):
                acc_ref[...] += _dot(refs[2 * i][...], refs[2 * i + 1][...], ((1,), (1,)))

        @pl.when(s == total - 1)
        def _():
            o_ref[...] = acc_ref[...]

    args, specs = [], []
    for (a, b), tk, off, nk in zip(pairs, tks, offs, nks):
        k_of = lambda s, off=off, nk=nk: jnp.clip(s - off, 0, nk - 1)
        args += [a, b]
        specs += [pl.BlockSpec((tm, tk), lambda i, j, s, k_of=k_of: (i, k_of(s))),
                  pl.BlockSpec((tn, tk), lambda i, j, s, k_of=k_of: (j, k_of(s)))]
    (out,), exchanged = _call_with_exchange(
        body, name=name, grid=(m // tm, n // tn, total), in_specs=specs,
        out_specs=[pl.BlockSpec((tm, tn), lambda i, j, s: (i, j))], out_shape=[jax.ShapeDtypeStruct((m, n), F32)],
        scratch_shapes=[pltpu.VMEM((tm, tn), F32)], args=args, exchange=exchange)
    return out, exchanged


def _rmsnorm_fwd(x, w, name):
    t = x.shape[0]
    blk = _tile(t, 1024)

    def body(x_ref, w_ref, h_ref):
        xv = x_ref[...]
        h_ref[...] = (xv * lax.rsqrt(jnp.mean(xv * xv, axis=1, keepdims=True) + NORM_EPS) * w_ref[...]).astype(BF16)

    return pl.pallas_call(
        body, name=name, grid=(t // blk,),
        in_specs=[pl.BlockSpec((blk, D), lambda i: (i, 0)), pl.BlockSpec((1, D), lambda i: (0, 0))],
        out_specs=pl.BlockSpec((blk, D), lambda i: (i, 0)), out_shape=jax.ShapeDtypeStruct((t, D), BF16),
        compiler_params=_cparams("parallel"))(x, w)


def _rmsnorm_bwd(dh, x, w, dxo, name, exchange=None):
    t = x.shape[0]
    blk = _tile(t, 512)

    def body(dh_ref, x_ref, w_ref, dxo_ref, dx_ref, dw_ref):
        @pl.when(pl.program_id(0) == 0)
        def _():
            dw_ref[...] = jnp.zeros_like(dw_ref)

        xv, dhv = x_ref[...], dh_ref[...]
        rs = lax.rsqrt(jnp.mean(xv * xv, axis=1, keepdims=True) + NORM_EPS)
        xh = xv * rs
        dw_ref[...] += jnp.sum(dhv * xh, axis=0, keepdims=True)
        dxh = dhv * w_ref[...]
        dx_ref[...] = rs * (dxh - xh * jnp.mean(dxh * xh, axis=1, keepdims=True)) + dxo_ref[...]

    row = pl.BlockSpec((blk, D), lambda i: (i, 0))
    vec = pl.BlockSpec((1, D), lambda i: (0, 0))
    return _call_with_exchange(
        body, name=name, grid=(t // blk,), in_specs=[row, row, vec, row], out_specs=[row, vec],
        out_shape=[jax.ShapeDtypeStruct((t, D), F32), jax.ShapeDtypeStruct((1, D), F32)],
        scratch_shapes=[], args=(dh, x, w, dxo), exchange=exchange)


def _loss_head(x, w, target, name):
    t = x.shape[0]
    blk = _tile(t, 512)

    def body(x_ref, w_ref, t_ref, loss_ref, dx_ref, dw_ref):
        @pl.when(pl.program_id(0) == 0)
        def _():
            dw_ref[...] = jnp.zeros_like(dw_ref)
            loss_ref[...] = jnp.zeros_like(loss_ref)

        xv = x_ref[...]
        rs = lax.rsqrt(jnp.mean(xv * xv, axis=1, keepdims=True) + NORM_EPS)
        xh = xv * rs
        err = xh * w_ref[...] - t_ref[...]
        loss_ref[...] += 0.5 * jnp.sum(jnp.mean(err * err, axis=1, keepdims=True), axis=0, keepdims=True)
        dy = err * (1.0 / D)
        dw_ref[...] += jnp.sum(dy * xh, axis=0, keepdims=True)
        dxh = dy * w_ref[...]
        dx_ref[...] = rs * (dxh - xh * jnp.mean(dxh * xh, axis=1, keepdims=True))

    row = pl.BlockSpec((blk, D), lambda i: (i, 0))
    vec = pl.BlockSpec((1, D), lambda i: (0, 0))
    return pl.pallas_call(
        body, name=name, grid=(t // blk,), in_specs=[row, vec, row],
        out_specs=[pl.BlockSpec((1, 1), lambda i: (0, 0)), row, vec],
        out_shape=[jax.ShapeDtypeStruct((1, 1), F32), jax.ShapeDtypeStruct((t, D), F32), jax.ShapeDtypeStruct((1, D), F32)],
        compiler_params=_cparams("arbitrary"))(x, w, target)


def _lbs_of(lb):
    r = lax.broadcasted_iota(jnp.int32, lb.shape, 0)
    real = r < DEPTH
    mx = lax.stop_gradient(jnp.max(jnp.where(real, lb, -jnp.inf), axis=0, keepdims=True))
    e = jnp.where(real, jnp.exp(jnp.where(real, lb - mx, 0.0)), 0.0)
    p = e / jnp.sum(e, axis=0, keepdims=True)
    out = jnp.zeros_like(lb)
    run = jnp.zeros_like(mx)
    for l in range(1, DEPTH):
        run = run + jnp.sum(jnp.where(r == l, p, 0.0), axis=0, keepdims=True)
        out = out + jnp.where(r == l, run, 0.0)
    return out


def _lower_bounds_fwd(lbp, name):
    def body(lb_ref, o_ref):
        o_ref[...] = _lbs_of(lb_ref[...])

    return pl.pallas_call(body, name=name, out_shape=jax.ShapeDtypeStruct(lbp.shape, F32))(lbp)


def _lower_bounds_bwd(lbp, dlbs, name):
    def body(lb_ref, d_ref, o_ref):
        _, vjp = jax.vjp(_lbs_of, lb_ref[...])
        o_ref[...] = vjp(d_ref[...])[0]

    return pl.pallas_call(body, name=name, out_shape=jax.ShapeDtypeStruct(lbp.shape, F32))(lbp, dlbs)


def _branch_a_fwd(pa, cw, name):
    t = pa.shape[0]
    blk = _tile(t, 512)
    W = CONV_W

    def body(p_ref, w_ref, y_ref, hc_ref, hx_ref):
        @pl.when(pl.program_id(0) == 0)
        def _():
            hc_ref[...] = jnp.zeros_like(hc_ref)
            hx_ref[...] = jnp.zeros_like(hx_ref)

        ac, ax = p_ref[:, W:2 * W], p_ref[:, 2 * W:3 * W]
        y_ref[...] = _conv_a_block(
            p_ref[:, 0:W], jnp.concatenate([hc_ref[...], ac], axis=0), jnp.concatenate([hx_ref[...], ax], axis=0),
            p_ref[:, 3 * W:4 * W], w_ref[0:1, :], w_ref[1:2, :], w_ref[2:3, :]).astype(BF16)
        hc_ref[...] = p_ref[blk - 8:blk, W:2 * W]
        hx_ref[...] = p_ref[blk - 8:blk, 2 * W:3 * W]

    return pl.pallas_call(
        body, name=name, grid=(t // blk,),
        in_specs=[pl.BlockSpec((blk, NA), lambda i: (i, 0)), pl.BlockSpec((3, W), lambda i: (0, 0))],
        out_specs=pl.BlockSpec((blk, W), lambda i: (i, 0)), out_shape=jax.ShapeDtypeStruct((t, W), BF16),
        scratch_shapes=[pltpu.VMEM((8, W), F32), pltpu.VMEM((8, W), F32)],
        compiler_params=_cparams("arbitrary"))(pa, cw)


def _branch_a_bwd(pa, cw, dy, name):
    t = pa.shape[0]
    blk = _tile(t, 512)
    nt_ = t // blk
    W = CONV_W
    hb = blk // 8

    def body(p_ref, halo_ref, w_ref, dy_ref, dp_ref, dw_ref, chc_ref, chx_ref):
        i = pl.program_id(0)

        @pl.when(i == 0)
        def _():
            chc_ref[...] = jnp.zeros_like(chc_ref)
            chx_ref[...] = jnp.zeros_like(chx_ref)
            dw_ref[...] = jnp.zeros_like(dw_ref)

        keep = 1.0 - (i == nt_ - 1).astype(F32)
        hc = halo_ref[:, W:2 * W] * keep
        hx = halo_ref[:, 2 * W:3 * W] * keep
        ac_ext = jnp.concatenate([hc, p_ref[:, W:2 * W]], axis=0)
        ax_ext = jnp.concatenate([hx, p_ref[:, 2 * W:3 * W]], axis=0)
        _, vjp = jax.vjp(_conv_a_block, p_ref[:, 0:W], ac_ext, ax_ext, p_ref[:, 3 * W:4 * W],
                         w_ref[0:1, :], w_ref[1:2, :], w_ref[2:3, :])
        dab, dac, dax, daz, dw0, dw1, dw2 = vjp(dy_ref[...].astype(F32))
        dp_ref[:, 0:W] = dab.astype(BF16)
        dp_ref[:, W:2 * W] = _add_to_tail(dac[8:], chc_ref[...]).astype(BF16)
        dp_ref[:, 2 * W:3 * W] = _add_to_tail(dax[8:], chx_ref[...]).astype(BF16)
        dp_ref[:, 3 * W:4 * W] = daz.astype(BF16)
        chc_ref[...] = dac[:8] * keep
        chx_ref[...] = dax[:8] * keep
        dw_ref[0:1, :] += dw0
        dw_ref[1:2, :] += dw1
        dw_ref[2:3, :] += dw2

    rev = lambda i: (nt_ - 1 - i, 0)
    return pl.pallas_call(
        body, name=name, grid=(nt_,),
        in_specs=[pl.BlockSpec((blk, NA), rev),
                  pl.BlockSpec((8, NA), lambda i: (jnp.maximum((nt_ - 1 - i) * hb - 1, 0), 0)),
                  pl.BlockSpec((3, W), lambda i: (0, 0)),
                  pl.BlockSpec((blk, W), rev)],
        out_specs=[pl.BlockSpec((blk, NA), rev), pl.BlockSpec((3, W), lambda i: (0, 0))],
        out_shape=[jax.ShapeDtypeStruct((t, NA), BF16), jax.ShapeDtypeStruct((3, W), F32)],
        scratch_shapes=[pltpu.VMEM((8, W), F32), pltpu.VMEM((8, W), F32)],
        compiler_params=_cparams("arbitrary"))(pa, pa, cw, dy)


def _block_rows(t, chunks):
    return min(t, chunks * CHUNK)


def _branch_b_fwd(pb, lbs_row, nw, name, exchange=None):
    t = pb.shape[0]
    rows = _block_rows(t, HGRN_BLOCK_CHUNKS)
    nch = t // rows

    def body(p_ref, lb_ref, nw_ref, y_ref, s_ref, st_ref):
        @pl.when(pl.program_id(1) == 0)
        def _():
            st_ref[...] = jnp.zeros_like(st_ref)

        s_ref[0, 0] = st_ref[...]
        y, st1 = _hgrn_block(p_ref[:, 0:HD], p_ref[:, HD:2 * HD], p_ref[:, 2 * HD:3 * HD], p_ref[:, 3 * HD:4 * HD],
                             st_ref[...], lb_ref[...], nw_ref[...])
        y_ref[...] = y.astype(BF16)
        st_ref[...] = st1

    return _call_with_exchange(
        body, name=name, grid=(HGRN_HEADS, nch),
        in_specs=[pl.BlockSpec((rows, 4 * HD), lambda h, i: (i, h)),
                  pl.BlockSpec((1, HD), lambda h, i: (0, h)),
                  pl.BlockSpec((1, HD), lambda h, i: (0, 0))],
        out_specs=[pl.BlockSpec((rows, HD), lambda h, i: (i, h)),
                   pl.BlockSpec((1, 1, HD, HD), lambda h, i: (h, i, 0, 0))],
        out_shape=[jax.ShapeDtypeStruct((t, HGRN_HEADS * HD), BF16),
                   jax.ShapeDtypeStruct((HGRN_HEADS, nch, HD, HD), F32)],
        scratch_shapes=[pltpu.VMEM((HD, HD), F32)],
        args=(pb, lbs_row, nw), exchange=exchange)


def _branch_b_bwd(pb, states, lbs_row, nw, dy, name, exchange=None):
    t = pb.shape[0]
    rows = _block_rows(t, HGRN_BLOCK_CHUNKS)
    nch = t // rows

    def body(p_ref, s_ref, lb_ref, nw_ref, dy_ref, dp_ref, dlb_ref, dnw_ref, ds_ref):
        h, i = pl.program_id(0), pl.program_id(1)

        @pl.when(i == 0)
        def _():
            ds_ref[...] = jnp.zeros_like(ds_ref)
            dlb_ref[...] = jnp.zeros_like(dlb_ref)

        @pl.when((i == 0) & (h == 0))
        def _():
            dnw_ref[...] = jnp.zeros_like(dnw_ref)

        _, vjp = jax.vjp(_hgrn_block, p_ref[:, 0:HD], p_ref[:, HD:2 * HD], p_ref[:, 2 * HD:3 * HD],
                         p_ref[:, 3 * HD:4 * HD], s_ref[0, 0], lb_ref[...], nw_ref[...])
        dq, df, di, dz, ds0, dlb, dnw = vjp((dy_ref[...].astype(F32), ds_ref[...]))
        dp_ref[:, 0:HD] = dq.astype(BF16)
        dp_ref[:, HD:2 * HD] = df.astype(BF16)
        dp_ref[:, 2 * HD:3 * HD] = di.astype(BF16)
        dp_ref[:, 3 * HD:4 * HD] = dz.astype(BF16)
        ds_ref[...] = ds0
        dlb_ref[...] += dlb
        dnw_ref[...] += dnw

    rev = lambda h, i: (nch - 1 - i, h)
    return _call_with_exchange(
        body, name=name, grid=(HGRN_HEADS, nch),
        in_specs=[pl.BlockSpec((rows, 4 * HD), rev),
                  pl.BlockSpec((1, 1, HD, HD), lambda h, i: (h, nch - 1 - i, 0, 0)),
                  pl.BlockSpec((1, HD), lambda h, i: (0, h)),
                  pl.BlockSpec((1, HD), lambda h, i: (0, 0)),
                  pl.BlockSpec((rows, HD), rev)],
        out_specs=[pl.BlockSpec((rows, 4 * HD), rev),
                   pl.BlockSpec((1, HD), lambda h, i: (0, h)),
                   pl.BlockSpec((1, HD), lambda h, i: (0, 0))],
        out_shape=[jax.ShapeDtypeStruct((t, NB), BF16), jax.ShapeDtypeStruct((1, HGRN_HEADS * HD), F32),
                   jax.ShapeDtypeStruct((1, HD), F32)],
        scratch_shapes=[pltpu.VMEM((HD, HD), F32)],
        args=(pb, states, lbs_row, nw, dy), exchange=exchange)


def _branch_c_fwd(pc, cw, cpar, nw, name, exchange=None):
    t = pc.shape[0]
    back_rows = _block_rows(t, GDN_BLOCK_CHUNKS)
    rows = _block_rows(t, GDN_FORWARD_CHUNKS)
    nch, per_step = t // rows, rows // back_rows
    XW = 4 * HD
    nsys = 2 * back_rows // GROUP

    def body(p_ref, w_ref, cp_ref, nw_ref, y_ref, s_ref, x_ref, sa_ref, sb_ref, halo_ref):
        @pl.when(pl.program_id(1) == 0)
        def _():
            sa_ref[...] = jnp.zeros_like(sa_ref)
            sb_ref[...] = jnp.zeros_like(sb_ref)
            halo_ref[...] = jnp.zeros_like(halo_ref)

        x_ext = jnp.concatenate([halo_ref[...], p_ref[:, 0:XW]], axis=0)
        (y, s1a, s1b), (xs, s_at) = _gdn_block(
            x_ext, p_ref[:, XW:XW + 2 * HD], p_ref[:, XW + 2 * HD:XW + 3 * HD],
            sa_ref[...], sb_ref[...], w_ref[0:1, :], w_ref[1:2, :], w_ref[2:3, :], w_ref[3:4, :],
            cp_ref[0, 0:1, :], cp_ref[0, 1:2, :], nw_ref[...])
        for b in range(per_step):
            for i in range(2):
                s_ref[0, b, i] = s_at[b * (back_rows // CHUNK), i]
            for n in range(nsys):
                x_ref[0, b, n] = xs[b * nsys + n]
        y_ref[...] = y.astype(BF16)
        sa_ref[...] = s1a
        sb_ref[...] = s1b
        halo_ref[...] = p_ref[rows - 8:rows, 0:XW]

    return _call_with_exchange(
        body, name=name, grid=(GDN_QK_HEADS, nch),
        in_specs=[pl.BlockSpec((rows, C_HEAD), lambda h, i: (i, h)),
                  pl.BlockSpec((4, XW), lambda h, i: (0, h)),
                  pl.BlockSpec((1, 8, HD), lambda h, i: (h, 0, 0)),
                  pl.BlockSpec((1, HD), lambda h, i: (0, 0))],
        out_specs=[pl.BlockSpec((rows, 2 * HD), lambda h, i: (i, h)),
                   pl.BlockSpec((1, per_step, 2, HD, HD), lambda h, i: (h, i, 0, 0, 0)),
                   pl.BlockSpec((1, per_step, nsys, GROUP, GROUP), lambda h, i: (h, i, 0, 0, 0))],
        out_shape=[jax.ShapeDtypeStruct((t, 2 * GDN_QK_HEADS * HD), BF16),
                   jax.ShapeDtypeStruct((GDN_QK_HEADS, nch * per_step, 2, HD, HD), F32),
                   jax.ShapeDtypeStruct((GDN_QK_HEADS, nch * per_step, nsys, GROUP, GROUP), BF16)],
        scratch_shapes=[pltpu.VMEM((HD, HD), F32), pltpu.VMEM((HD, HD), F32), pltpu.VMEM((8, XW), F32)],
        args=(pc, cw, cpar, nw), exchange=exchange)


def _branch_c_bwd(pc, states, inverses, cw, cpar, nw, dy, name, exchange=None):
    t = pc.shape[0]
    rows = _block_rows(t, GDN_BLOCK_CHUNKS)
    nch = t // rows
    XW = 4 * HD
    hb = rows // 8

    nsys = 2 * rows // GROUP

    def body(p_ref, halo_ref, s_ref, x_ref, w_ref, cp_ref, nw_ref, dy_ref, dp_ref, dw_ref, dcp_ref, dnw_ref,
             dsa_ref, dsb_ref, carry_ref):
        h, i = pl.program_id(0), pl.program_id(1)

        @pl.when(i == 0)
        def _():
            dsa_ref[...] = jnp.zeros_like(dsa_ref)
            dsb_ref[...] = jnp.zeros_like(dsb_ref)
            carry_ref[...] = jnp.zeros_like(carry_ref)
            dw_ref[...] = jnp.zeros_like(dw_ref)
            dcp_ref[...] = jnp.zeros_like(dcp_ref)

        @pl.when((i == 0) & (h == 0))
        def _():
            dnw_ref[...] = jnp.zeros_like(dnw_ref)

        keep = 1.0 - (i == nch - 1).astype(F32)
        x_ext = jnp.concatenate([halo_ref[:, 0:XW] * keep, p_ref[:, 0:XW]], axis=0)
        block = functools.partial(_gdn_block, known=tuple(x_ref[0, 0, n] for n in range(nsys)))
        _, vjp, _ = jax.vjp(block, x_ext, p_ref[:, XW:XW + 2 * HD], p_ref[:, XW + 2 * HD:XW + 3 * HD],
                            s_ref[0, 0, 0], s_ref[0, 0, 1], w_ref[0:1, :], w_ref[1:2, :], w_ref[2:3, :], w_ref[3:4, :],
                            cp_ref[0, 0:1, :], cp_ref[0, 1:2, :], nw_ref[...], has_aux=True)
        dx, dz, dba, dsa, dsb, dw0, dw1, dw2, dw3, dal, ddt, dnw = vjp((dy_ref[...].astype(F32), dsa_ref[...], dsb_ref[...]))
        dp_ref[:, 0:XW] = _add_to_tail(dx[8:], carry_ref[...]).astype(BF16)
        dp_ref[:, XW:XW + 2 * HD] = dz.astype(BF16)
        dp_ref[:, XW + 2 * HD:XW + 3 * HD] = dba.astype(BF16)
        carry_ref[...] = dx[:8] * keep
        dsa_ref[...] = dsa
        dsb_ref[...] = dsb
        dw_ref[0:1, :] += dw0
        dw_ref[1:2, :] += dw1
        dw_ref[2:3, :] += dw2
        dw_ref[3:4, :] += dw3
        dcp_ref[0, 0:1, :] += dal
        dcp_ref[0, 1:2, :] += ddt
        dnw_ref[...] += dnw

    rev = lambda h, i: (nch - 1 - i, h)
    return _call_with_exchange(
        body, name=name, grid=(GDN_QK_HEADS, nch),
        in_specs=[pl.BlockSpec((rows, C_HEAD), rev),
                  pl.BlockSpec((8, C_HEAD), lambda h, i: (jnp.maximum((nch - 1 - i) * hb - 1, 0), h)),
                  pl.BlockSpec((1, 1, 2, HD, HD), lambda h, i: (h, nch - 1 - i, 0, 0, 0)),
                  pl.BlockSpec((1, 1, nsys, GROUP, GROUP), lambda h, i: (h, nch - 1 - i, 0, 0, 0)),
                  pl.BlockSpec((4, XW), lambda h, i: (0, h)),
                  pl.BlockSpec((1, 8, HD), lambda h, i: (h, 0, 0)),
                  pl.BlockSpec((1, HD), lambda h, i: (0, 0)),
                  pl.BlockSpec((rows, 2 * HD), rev)],
        out_specs=[pl.BlockSpec((rows, C_HEAD), rev),
                   pl.BlockSpec((4, XW), lambda h, i: (0, h)),
                   pl.BlockSpec((1, 8, HD), lambda h, i: (h, 0, 0)),
                   pl.BlockSpec((1, HD), lambda h, i: (0, 0))],
        out_shape=[jax.ShapeDtypeStruct((t, NC_COLS), BF16), jax.ShapeDtypeStruct((4, GDN_QK_HEADS * XW), F32),
                   jax.ShapeDtypeStruct((GDN_QK_HEADS, 8, HD), F32), jax.ShapeDtypeStruct((1, HD), F32)],
        scratch_shapes=[pltpu.VMEM((HD, HD), F32), pltpu.VMEM((HD, HD), F32), pltpu.VMEM((8, XW), F32)],
        args=(pc, pc, states, inverses, cw, cpar, nw, dy), exchange=exchange)


def _merge_fwd(pg, bg, ya, yb, yc, name):
    t = pg.shape[0]
    blk = _tile(t, 512)

    def body(g_ref, b_ref, a_ref, b2_ref, c_ref, o_ref):
        gate = _sigmoid(g_ref[...] + b_ref[...])
        o_ref[...] = (gate[:, 0:D] * a_ref[...] + gate[:, D:2 * D] * b2_ref[...] + gate[:, 2 * D:3 * D] * c_ref[...]).astype(BF16)

    row = pl.BlockSpec((blk, D), lambda i: (i, 0))
    return pl.pallas_call(
        body, name=name, grid=(t // blk,),
        in_specs=[pl.BlockSpec((blk, NG), lambda i: (i, 0)), pl.BlockSpec((1, NG), lambda i: (0, 0)), row, row, row],
        out_specs=row, out_shape=jax.ShapeDtypeStruct((t, D), BF16),
        compiler_params=_cparams("parallel"))(pg, bg, ya, yb, yc)


def _merge_bwd(dm, pg, bg, ya, yb, yc, name):
    t = pg.shape[0]
    blk = _tile(t, 512)

    def body(dm_ref, g_ref, b_ref, a_ref, b2_ref, c_ref, dg_ref, da_ref, db_ref, dc_ref, dbg_ref):
        @pl.when(pl.program_id(0) == 0)
        def _():
            dbg_ref[...] = jnp.zeros_like(dbg_ref)

        gate = _sigmoid(g_ref[...] + b_ref[...])
        dmv = dm_ref[...].astype(F32)
        for j, (y_ref, dy_ref) in enumerate(((a_ref, da_ref), (b2_ref, db_ref), (c_ref, dc_ref))):
            gj = gate[:, j * D:(j + 1) * D]
            dy_ref[...] = (dmv * gj).astype(BF16)
            dgj = dmv * y_ref[...] * gj * (1.0 - gj)
            dg_ref[:, j * D:(j + 1) * D] = dgj.astype(BF16)
            dbg_ref[:, j * D:(j + 1) * D] += jnp.sum(dgj, axis=0, keepdims=True)

    row = pl.BlockSpec((blk, D), lambda i: (i, 0))
    wide = pl.BlockSpec((blk, NG), lambda i: (i, 0))
    vec = pl.BlockSpec((1, NG), lambda i: (0, 0))
    return pl.pallas_call(
        body, name=name, grid=(t // blk,), in_specs=[row, wide, vec, row, row, row],
        out_specs=[wide, row, row, row, vec],
        out_shape=[jax.ShapeDtypeStruct((t, NG), BF16)] + [jax.ShapeDtypeStruct((t, D), BF16)] * 3
                  + [jax.ShapeDtypeStruct((1, NG), F32)],
        compiler_params=_cparams("arbitrary"))(dm, pg, bg, ya, yb, yc)


def _adamw_math(w, g, m, v):
    m = ADAM_B1 * m + (1.0 - ADAM_B1) * g
    v = ADAM_B2 * v + (1.0 - ADAM_B2) * (g * g)
    m_hat = m / (1.0 - ADAM_B1 ** ADAM_STEP)
    v_hat = v / (1.0 - ADAM_B2 ** ADAM_STEP)
    delta = -ADAM_LR * (m_hat / (jnp.sqrt(v_hat) + ADAM_EPS) + ADAM_WD * w)
    return delta, m, v


def _sum_adamw(parts, w, m, v, name, exchange=None):
    layers = len(parts)
    r, c = parts[0].shape[1:]
    br = r if r <= 256 else 256
    nb = r // br
    assert r % br == 0 and w.shape == (layers * r, c)

    def body(*refs):
        w_ref, m_ref, v_ref, g_ref, d_ref, nm_ref, nv_ref = refs[layers:]
        for l in range(layers):
            @pl.when(pl.program_id(0) == l)
            def _(p_ref=refs[l]):
                g = p_ref[0].astype(F32)
                for k in range(1, N_DEV):
                    g = g + p_ref[k].astype(F32)
                g_ref[...] = g
                d_ref[...], nm_ref[...], nv_ref[...] = _adamw_math(w_ref[...], g, m_ref[...], v_ref[...])

    blk = pl.BlockSpec((br, c), lambda l, i: (l * nb + i, 0))
    part_specs = [pl.BlockSpec((N_DEV, br, c), lambda l, i, q=q: (0, jnp.where(l == q, i, jnp.where(l < q, 0, nb - 1)), 0))
                  for q in range(layers)]
    return _call_with_exchange(
        body, name=name, grid=(layers, nb), in_specs=part_specs + [blk, blk, blk], out_specs=[blk] * 4,
        out_shape=[jax.ShapeDtypeStruct((layers * r, c), F32)] * 4, scratch_shapes=[],
        args=(*parts, w, m, v), exchange=exchange)


def _adamw(g, w, m, v, name):
    def body(g_ref, w_ref, m_ref, v_ref, d_ref, nm_ref, nv_ref):
        d_ref[...], nm_ref[...], nv_ref[...] = _adamw_math(w_ref[...], g_ref[...], m_ref[...], v_ref[...])

    return pl.pallas_call(body, name=name, out_shape=[jax.ShapeDtypeStruct(w.shape, F32)] * 3)(g, w, m, v)


def _sum_slots(parts, name):
    def body(p_ref, o_ref):
        g = p_ref[0]
        for k in range(1, N_DEV):
            g = g + p_ref[k]
        o_ref[...] = g

    return pl.pallas_call(body, name=name, out_shape=jax.ShapeDtypeStruct(parts.shape[1:], F32))(parts)


def _exchange(srcs, name, broadcast):
    n = len(srcs)

    def body(*refs):
        copies = _exchange_copies(refs[:n], refs[n:2 * n], *refs[2 * n:], broadcast)
        for cp in copies:
            cp.start()
        for cp in copies:
            cp.wait()

    return pl.pallas_call(
        body, name=name, in_specs=[HBM_SPEC] * n, out_specs=[HBM_SPEC] * n, out_shape=_exchange_shapes(srcs, broadcast),
        scratch_shapes=_exchange_semaphores(n))(*srcs)


def _gather_two_level(srcs, name):
    n = len(srcs)

    def body(*refs):
        src_refs, dst_refs = refs[:n], refs[n:2 * n]
        send_sems, recv_sems, local_sems = refs[2 * n:]
        x, y, c = lax.axis_index("x"), lax.axis_index("y"), lax.axis_index("c")
        index_of = lambda px, py, pc: 4 * px + 2 * py + pc
        me, other_core = index_of(x, y, c), (x, y, 1 - c)
        chips = [(1 - x, y), (x, 1 - y), (1 - x, 1 - y)]

        def copy(k, a, block, to, src=None):
            return pltpu.make_async_remote_copy(
                src_ref=dst_refs[a].at[block] if src is None else src, dst_ref=dst_refs[a].at[block],
                send_sem=send_sems.at[k, a], recv_sem=recv_sems.at[k, a], device_id=to, device_id_type=MESH)

        local = [pltpu.make_async_copy(src_refs[a], dst_refs[a].at[me], local_sems.at[a]) for a in range(n)]
        first = [copy(0, a, me, other_core, src=src_refs[a]) for a in range(n)]
        first += [copy(1 + j, a, me, (*chip, c), src=src_refs[a]) for j, chip in enumerate(chips) for a in range(n)]
        for cp in local + first:
            cp.start()
        passed = []
        for j, chip in enumerate(chips):
            block = index_of(*chip, c)
            for a in range(n):
                copy(1 + j, a, block, (x, y, c)).wait_recv()
            for a in range(n):
                passed.append(copy(4 + j, a, block, other_core))
                passed[-1].start()
        for a in range(n):
            copy(0, a, index_of(x, y, 1 - c), (x, y, c)).wait_recv()
        for j, chip in enumerate(chips):
            for a in range(n):
                copy(4 + j, a, index_of(*chip, 1 - c), (x, y, c)).wait_recv()
        for cp in first + passed:
            cp.wait_send()
        for cp in local:
            cp.wait()

    return pl.pallas_call(
        body, name=name, in_specs=[HBM_SPEC] * n, out_specs=[HBM_SPEC] * n, out_shape=_exchange_shapes(srcs, True),
        scratch_shapes=_exchange_semaphores(n))(*srcs)


HBM_SPEC = pl.BlockSpec(memory_space=pltpu.HBM)


def _exchange_shapes(srcs, broadcast):
    return [jax.ShapeDtypeStruct((N_DEV,) + (s.shape if broadcast else s.shape[1:]), s.dtype) for s in srcs]


def _exchange_semaphores(n):
    return [pltpu.SemaphoreType.DMA((N_DEV - 1, n)), pltpu.SemaphoreType.DMA((N_DEV - 1, n)), pltpu.SemaphoreType.DMA((n,))]


def _exchange_copies(src_refs, dst_refs, send_sems, recv_sems, local_sems, broadcast):
    x, y, c = lax.axis_index("x"), lax.axis_index("y"), lax.axis_index("c")
    me = 4 * x + 2 * y + c
    copies = []
    for k in range(1, N_DEV):
        px = 1 - x if (k >> 2) & 1 else x
        py = 1 - y if (k >> 1) & 1 else y
        pc = 1 - c if k & 1 else c
        peer = 4 * px + 2 * py + pc
        for a, (src, dst) in enumerate(zip(src_refs, dst_refs)):
            copies.append(pltpu.make_async_remote_copy(
                src_ref=src if broadcast else src.at[peer], dst_ref=dst.at[me],
                send_sem=send_sems.at[k - 1, a], recv_sem=recv_sems.at[k - 1, a],
                device_id=(px, py, pc), device_id_type=MESH))
    for a, (src, dst) in enumerate(zip(src_refs, dst_refs)):
        copies.append(pltpu.make_async_copy(src if broadcast else src.at[me], dst.at[me], local_sems.at[a]))
    return copies


def _call_with_exchange(body, *, name, grid, in_specs, out_specs, out_shape, scratch_shapes, args, exchange):
    if exchange is None:
        outs = pl.pallas_call(body, name=name, grid=grid, in_specs=in_specs, out_specs=out_specs, out_shape=out_shape,
                              scratch_shapes=scratch_shapes,
                              compiler_params=_cparams(*["arbitrary"] * len(grid)))(*args)
        return outs, None
    srcs, broadcast = exchange
    n, n_in, n_out, n_scr = len(srcs), len(args), len(out_shape), len(scratch_shapes)
    steps = 1
    for g in grid:
        steps *= g

    def hosted(*refs):
        ins, src_refs = refs[:n_in], refs[n_in:n_in + n]
        outs, dst_refs = refs[n_in + n:n_in + n + n_out], refs[n_in + n + n_out:n_in + 2 * n + n_out]
        scratch = refs[n_in + 2 * n + n_out:]
        step = pl.program_id(0)
        for axis in range(1, len(grid)):
            step = step * grid[axis] + pl.program_id(axis)

        @pl.when(step == 0)
        def _():
            for cp in _exchange_copies(src_refs, dst_refs, *scratch[n_scr:], broadcast):
                cp.start()

        body(*ins, *outs, *scratch[:n_scr])

        @pl.when(step == steps - 1)
        def _():
            for cp in _exchange_copies(src_refs, dst_refs, *scratch[n_scr:], broadcast):
                cp.wait()

    outs = pl.pallas_call(
        hosted, name=name, grid=grid, in_specs=list(in_specs) + [HBM_SPEC] * n, out_specs=list(out_specs) + [HBM_SPEC] * n,
        out_shape=list(out_shape) + _exchange_shapes(srcs, broadcast),
        scratch_shapes=list(scratch_shapes) + _exchange_semaphores(n),
        compiler_params=_cparams(*["arbitrary"] * len(grid)))(*args, *srcs)
    return outs[:n_out], outs[n_out:]


def _regroup_w_in(w):
    wa = w[:, OFF_A:OFF_A + NA]
    seg = lambda off, h, n=HD: w[:, off + h * n: off + (h + 1) * n]
    wb = jnp.concatenate([seg(OFF_B + s * 512, h) for h in range(HGRN_HEADS) for s in range(4)], axis=1)
    parts = []
    for h in range(GDN_QK_HEADS):
        small = jnp.concatenate(
            [w[:, OFF_BETA + 2 * h: OFF_BETA + 2 * h + 2], w[:, OFF_CA + 2 * h: OFF_CA + 2 * h + 2],
             jnp.zeros((w.shape[0], HD - 4), w.dtype)], axis=1)
        parts += [seg(OFF_CQ, h), seg(OFF_CK, h), seg(OFF_CV, h, 2 * HD), seg(OFF_CZ, h, 2 * HD), small]
    wc = jnp.concatenate(parts, axis=1)
    wg = w[:, OFF_G:OFF_G + NG]
    return wa, wb, wc, wg


def _ungroup_dw_in(da, db, dc, dg):
    bq = [jnp.concatenate([db[:, h * 512 + s * HD: h * 512 + (s + 1) * HD] for h in range(HGRN_HEADS)], axis=1)
          for s in range(4)]
    ch = lambda h, lo, hi: dc[:, h * C_HEAD + lo: h * C_HEAD + hi]
    heads = range(GDN_QK_HEADS)
    cq = jnp.concatenate([ch(h, 0, HD) for h in heads], axis=1)
    ck = jnp.concatenate([ch(h, HD, 2 * HD) for h in heads], axis=1)
    cv = jnp.concatenate([ch(h, 2 * HD, 4 * HD) for h in heads], axis=1)
    cz = jnp.concatenate([ch(h, 4 * HD, 6 * HD) for h in heads], axis=1)
    cbeta = jnp.concatenate([ch(h, 6 * HD, 6 * HD + 2) for h in heads], axis=1)
    ca = jnp.concatenate([ch(h, 6 * HD + 2, 6 * HD + 4) for h in heads], axis=1)
    return jnp.concatenate([da] + bq + [cq, ck, cv, cbeta, ca, cz, dg], axis=1)


def _regroup_conv_c(cw):
    parts = []
    for h in range(GDN_QK_HEADS):
        parts += [cw[:, h * HD:(h + 1) * HD], cw[:, 512 + h * HD: 512 + (h + 1) * HD],
                  cw[:, 1024 + 2 * h * HD: 1024 + (2 * h + 2) * HD]]
    return jnp.concatenate(parts, axis=1)


def _ungroup_conv_c(d):
    heads = range(GDN_QK_HEADS)
    q = jnp.concatenate([d[:, h * 512: h * 512 + HD] for h in heads], axis=1)
    k = jnp.concatenate([d[:, h * 512 + HD: h * 512 + 2 * HD] for h in heads], axis=1)
    v = jnp.concatenate([d[:, h * 512 + 2 * HD: h * 512 + 4 * HD] for h in heads], axis=1)
    return jnp.concatenate([q, k, v], axis=1)


def _numel(shape):
    n = 1
    for d in shape:
        n *= d
    return n


def _pack(arrays, rows):
    flat = jnp.concatenate([a.reshape(-1) for a in arrays])
    return jnp.pad(flat, (0, rows * 128 - flat.shape[0])).reshape(rows, 128)


def _unpack(packed, shapes):
    flat = packed.reshape(-1)
    out, off = [], 0
    for s in shapes:
        out.append(flat[off:off + _numel(s)].reshape(s))
        off += _numel(s)
    return out


def _rows_for(shapes):
    return -(-sum(_numel(s) for s in shapes) // 1024) * 8


def kernel(x, norm_w, w_in, b_gate, conv_a, conv_c, a_log, dt_bias, lower_bounds, hgrn_norm_w, gdn_norm_w, w_out_a, w_out_b, w_out_c, w_o, final_norm_w, loss_target, m_norm_w, m_w_in, m_b_gate, m_conv_a, m_conv_c, m_a_log, m_dt_bias, m_lower_bounds, m_hgrn_norm_w, m_gdn_norm_w, m_w_out_a, m_w_out_b, m_w_out_c, m_w_o, m_final_norm_w, v_norm_w, v_w_in, v_b_gate, v_conv_a, v_conv_c, v_a_log, v_dt_bias, v_lower_bounds, v_hgrn_norm_w, v_gdn_norm_w, v_w_out_a, v_w_out_b, v_w_out_c, v_w_o, v_final_norm_w):
    me = 4 * lax.axis_index("x") + 2 * lax.axis_index("y") + lax.axis_index("c")
    xs = x[0]
    target = loss_target[0]
    in_shard = w_in.shape[2]

    big = [w_in, w_out_a, w_out_b, w_out_c, w_o]
    shards_of = lambda l: [w[l].astype(BF16) for w in big]
    conv_shapes = [(DEPTH, 3, CONV_W), (DEPTH, 4, 2048)]
    conv_rows = _rows_for(conv_shapes)
    ca_full = lax.dynamic_update_slice(jnp.zeros(conv_shapes[0], F32), conv_a, (0, 0, me * conv_a.shape[2]))
    cc_full = lax.dynamic_update_slice(jnp.zeros(conv_shapes[1], F32), conv_c, (0, 0, me * conv_c.shape[2]))
    g_in0, conv_parts = _gather_two_level([w_in[0].astype(BF16), _pack([ca_full, cc_full], conv_rows)], "gather_l0")
    conv_a_full, conv_c_full = _unpack(_sum_slots(conv_parts, "sum_conv"), conv_shapes)

    lb_pad = jnp.pad(lower_bounds, ((0, 8 - DEPTH), (0, 0)))
    lbs = _lower_bounds_fwd(lb_pad, "lower_bounds_fwd")

    def input_weights(l, g_in):
        wa, wb, wc, wg = _regroup_w_in(jnp.concatenate([g_in[q] for q in range(N_DEV)], axis=1))
        lanes = lambda vec: jnp.pad(vec.reshape(GDN_QK_HEADS, 1, 2), ((0, 0), (0, 0), (0, HD - 2)))
        cpar = jnp.concatenate([lanes(a_log[l]), lanes(dt_bias[l]), jnp.zeros((GDN_QK_HEADS, 6, HD), F32)], axis=1)
        return dict(
            wa=wa, wb=wb, wc=wc, wg=wg, cpar=cpar,
            nw=norm_w[l:l + 1], bg=b_gate[l:l + 1], cwa=conv_a_full[l], cwc=_regroup_conv_c(conv_c_full[l]),
            lb=lbs[l:l + 1], hnw=hgrn_norm_w[l:l + 1], gnw=gdn_norm_w[l:l + 1])

    def output_weights(g_oa, g_ob, g_oc, g_o):
        return dict(woa=jnp.concatenate([g_oa[q] for q in range(N_DEV)], axis=1),
                    wob=jnp.concatenate([g_ob[q] for q in range(N_DEV)], axis=1), woc=g_oc.reshape(D, D), wo=g_o.reshape(D, D))

    layers = [input_weights(0, g_in0)]

    saved = []
    cur = xs
    for l in range(DEPTH):
        L = layers[l]
        n = f"l{l}_"
        h = _rmsnorm_fwd(cur, L["nw"], n + "rms")
        pa = _matmul(h, L["wa"], "nn", n + "proj_a")
        pb = _matmul(h, L["wb"], "nn", n + "proj_b")
        pc = _matmul(h, L["wc"], "nn", n + "proj_c")
        pg = _matmul(h, L["wg"], "nn", n + "proj_g", out_dtype=BF16)
        ua = _branch_a_fwd(pa, L["cwa"], n + "conv_fwd")
        carry = (shards_of(0)[1:] + shards_of(1)[1:], True) if l == 0 else None
        (ub, sb), gathered = _branch_b_fwd(pb, L["lb"], L["hnw"], n + "hgrn_fwd", exchange=carry)
        if carry is not None:
            L.update(output_weights(*gathered[:4]))
            next_outputs = output_weights(*gathered[4:])
        carry = (shards_of(1)[:1], True) if l == 0 else None
        (uc, sc, xc), gathered = _branch_c_fwd(pc, L["cwc"], L["cpar"], L["gnw"], n + "gdn_fwd", exchange=carry)
        if carry is not None:
            layers.append(dict(input_weights(1, gathered[0]), **next_outputs))
        ya = _matmul(ua, L["woa"], "nn", n + "out_a", out_dtype=BF16)
        yb = _matmul(ub, L["wob"], "nn", n + "out_b", out_dtype=BF16)
        yc = _matmul(uc, L["woc"], "nn", n + "out_c", out_dtype=BF16)
        merged = _merge_fwd(pg, L["bg"], ya, yb, yc, n + "merge")
        nxt = _matmul(merged, L["wo"], "nn", n + "out_o", residual=cur)
        saved.append(dict(x=cur, h=h, pa=pa, pb=pb, pc=pc, pg=pg, ua=ua, ub=ub, uc=uc, sb=sb, sc=sc, xc=xc,
                          ya=ya, yb=yb, yc=yc, merged=merged))
        cur = nxt

    loss_part, dx, d_final = _loss_head(cur, final_norm_w.reshape(1, D), target, "loss_head")

    def outgoing(g):
        cols = lambda a, n: jnp.stack([a[:, p * n:(p + 1) * n] for p in range(N_DEV)]).astype(BF16)
        rows = lambda a: a.reshape(N_DEV, a.shape[0] // N_DEV, a.shape[1]).astype(BF16)
        first = [cols(g["w_in"], in_shard)] if "w_in" in g else [None]
        if "w_o" not in g:
            return first
        return first + [cols(g["w_out_a"], 128), cols(g["w_out_b"], 128), rows(g["w_out_c"]), rows(g["w_o"])]

    grads = [None] * DEPTH
    dlbs_rows = [None] * DEPTH
    incoming = [None] * DEPTH
    for l in reversed(range(DEPTH)):
        L, S = layers[l], saved[l]
        n = f"l{l}_"
        dmerged = _matmul(dx, L["wo"], "nt", n + "d_merged", out_dtype=BF16)
        d_wo = _matmul(S["merged"], dx, "tn", n + "dw_o", out_dtype=BF16)
        dpg, dya, dyb, dyc, d_bg = _merge_bwd(dmerged, S["pg"], L["bg"], S["ya"], S["yb"], S["yc"], n + "merge_bwd")
        dua = _matmul(dya, L["woa"], "nt", n + "d_ua", out_dtype=BF16)
        dub = _matmul(dyb, L["wob"], "nt", n + "d_ub", out_dtype=BF16)
        duc = _matmul(dyc, L["woc"], "nt", n + "d_uc", out_dtype=BF16)
        d_woa = _matmul(S["ua"], dya, "tn", n + "dw_out_a", out_dtype=BF16)
        d_wob = _matmul(S["ub"], dyb, "tn", n + "dw_out_b", out_dtype=BF16)
        d_woc = _matmul(S["uc"], dyc, "tn", n + "dw_out_c", out_dtype=BF16)
        dpa, d_cwa = _branch_a_bwd(S["pa"], L["cwa"], dua, n + "conv_bwd")
        out_grads = dict(w_out_a=d_woa, w_out_b=d_wob, w_out_c=d_woc, w_o=d_wo)
        carry = (outgoing(out_grads)[1:], False) if l == 0 else None
        (dpb, d_lb, d_hnw), arrived_out = _branch_b_bwd(S["pb"], S["sb"], L["lb"], L["hnw"], dub, n + "hgrn_bwd", exchange=carry)
        carry = (outgoing(grads[l + 1]), False) if l + 1 < DEPTH else None
        (dpc, d_cwc, d_cpar, d_gnw), arrived = _branch_c_bwd(S["pc"], S["sc"], S["xc"], L["cwc"], L["cpar"], L["gnw"], duc,
                                                             n + "gdn_bwd", exchange=carry)
        if carry is not None:
            incoming[l + 1] = arrived
        d_win = _ungroup_dw_in(*[_matmul(S["h"], dp, "tn", f"{n}dw_{piece}", out_dtype=BF16)
                                 for dp, piece in ((dpa, "a"), (dpb, "b"), (dpc, "c"), (dpg, "g"))])
        carry = (outgoing(dict(w_in=d_win)), False) if l == 0 else None
        dh, arrived_in = _matmul_nt_sum([(dpa, L["wa"]), (dpb, L["wb"]), (dpc, L["wc"]), (dpg, L["wg"])], n + "dh",
                                        exchange=carry)
        (dx, d_nw), _ = _rmsnorm_bwd(dh, S["x"], L["nw"], dx, n + "rms_bwd")
        dlbs_rows[l] = d_lb
        grads[l] = dict(w_in=d_win, w_out_a=d_woa, w_out_b=d_wob, w_out_c=d_woc, w_o=d_wo, norm_w=d_nw[0],
                        b_gate=d_bg[0], conv_a=d_cwa, conv_c=_ungroup_conv_c(d_cwc),
                        a_log=d_cpar[:, 0, 0:2].reshape(-1), dt_bias=d_cpar[:, 1, 0:2].reshape(-1),
                        hgrn_norm_w=d_hnw[0], gdn_norm_w=d_gnw[0])
    grad_x = dx[None]
    d_lower = _lower_bounds_bwd(lb_pad, jnp.pad(jnp.concatenate(dlbs_rows, axis=0), ((0, 8 - DEPTH), (0, 0))),
                                "lower_bounds_bwd")[:DEPTH]

    stack = lambda name: jnp.stack([grads[l][name] for l in range(DEPTH)])
    small_names = ["norm_w", "b_gate", "conv_a", "conv_c", "a_log", "dt_bias", "lower_bounds", "hgrn_norm_w",
                   "gdn_norm_w", "final_norm_w", "loss"]
    small_vals = {k: stack(k) for k in ("norm_w", "b_gate", "conv_a", "conv_c", "a_log", "dt_bias", "hgrn_norm_w", "gdn_norm_w")}
    small_vals.update(lower_bounds=d_lower, final_norm_w=d_final[0], loss=loss_part.reshape(1))
    small_shapes = [small_vals[k].shape for k in small_names]
    small_rows = _rows_for(small_shapes)
    small_pack = _pack([small_vals[k] for k in small_names], small_rows)

    incoming[0] = list(arrived_in) + list(arrived_out)
    big_out = {}
    for j, (name, w, m, v) in enumerate((("w_in", w_in, m_w_in, v_w_in), ("w_out_a", w_out_a, m_w_out_a, v_w_out_a),
                                         ("w_out_b", w_out_b, m_w_out_b, v_w_out_b), ("w_out_c", w_out_c, m_w_out_c, v_w_out_c),
                                         ("w_o", w_o, m_w_o, v_w_o))):
        parts = [incoming[l][j] for l in range(DEPTH)]
        r2 = lambda a: a.reshape(DEPTH * parts[0].shape[1], parts[0].shape[2])
        carry = ([small_pack], True) if name == "w_in" else None
        outs, arrived = _sum_adamw(parts, r2(w), r2(m), r2(v), "adamw_" + name, exchange=carry)
        if carry is not None:
            small_parts, = arrived
        big_out[name] = [o.reshape(w.shape) for o in outs]

    total = dict(zip(small_names, _unpack(_sum_slots(small_parts, "sum_small"), small_shapes)))
    loss = total["loss"][0]
    g_conv_a = lax.dynamic_slice(total["conv_a"], (0, 0, me * conv_a.shape[2]), conv_a.shape)
    g_conv_c = lax.dynamic_slice(total["conv_c"], (0, 0, me * conv_c.shape[2]), conv_c.shape)

    small_w = dict(norm_w=(norm_w, m_norm_w, v_norm_w), b_gate=(b_gate, m_b_gate, v_b_gate),
                   conv_a=(conv_a, m_conv_a, v_conv_a), conv_c=(conv_c, m_conv_c, v_conv_c),
                   a_log=(a_log, m_a_log, v_a_log), dt_bias=(dt_bias, m_dt_bias, v_dt_bias),
                   lower_bounds=(lower_bounds, m_lower_bounds, v_lower_bounds),
                   hgrn_norm_w=(hgrn_norm_w, m_hgrn_norm_w, v_hgrn_norm_w), gdn_norm_w=(gdn_norm_w, m_gdn_norm_w, v_gdn_norm_w),
                   final_norm_w=(final_norm_w, m_final_norm_w, v_final_norm_w))
    small_g = dict(total, conv_a=g_conv_a, conv_c=g_conv_c)
    upd_names = small_names[:-1]
    upd_shapes = [small_w[k][0].shape for k in upd_names]
    upd_rows = _rows_for(upd_shapes)
    pk = lambda j: _pack([small_w[k][j] for k in upd_names], upd_rows)
    s_delta, s_m, s_v = _adamw(_pack([small_g[k] for k in upd_names], upd_rows), pk(0), pk(1), pk(2), "adamw_small")
    small_out = {k: [small_g[k], d, mm, vv] for k, d, mm, vv in
                 zip(upd_names, _unpack(s_delta, upd_shapes), _unpack(s_m, upd_shapes), _unpack(s_v, upd_shapes))}

    order = ["norm_w", "w_in", "b_gate", "conv_a", "conv_c", "a_log", "dt_bias", "lower_bounds", "hgrn_norm_w",
             "gdn_norm_w", "w_out_a", "w_out_b", "w_out_c", "w_o", "final_norm_w"]
    res = {**small_out, **big_out}
    outs = [loss, grad_x]
    for j in range(4):
        outs += [res[k][j] for k in order]
    return tuple(outs)
```

```python
import functools

import jax
import jax.numpy as jnp
from jax import lax
from jax.experimental import pallas as pl
from jax.experimental.pallas import tpu as pltpu

F32 = jnp.float32
BF16 = jnp.bfloat16
MESH = pl.DeviceIdType.MESH

N_DEV = 8
D = 1024
DEPTH = 2
CHUNK = 64
HGRN_BLOCK_CHUNKS = 16
GDN_BLOCK_CHUNKS = 8
GDN_FORWARD_CHUNKS = 16
GROUP = 128
NORM_EPS = 1e-6
L2_EPS = 1e-6
MIN_F = 1e-30
HD = 128
HGRN_HEADS = 4
GDN_QK_HEADS = 4
CONV_W = 512
IN_COLS = 10256
OFF_A, OFF_B, OFF_CQ, OFF_CK, OFF_CV, OFF_BETA, OFF_CA, OFF_CZ, OFF_G = (
    0, 2048, 4096, 4608, 5120, 6144, 6152, 6160, 7184)
NA, NB, NC_COLS, NG = 2048, 2048, 3584, 3072
C_HEAD = 896

ADAM_LR, ADAM_B1, ADAM_B2, ADAM_EPS, ADAM_WD, ADAM_STEP = 0.001, 0.9, 0.999, 1e-08, 0.01, 10

VMEM_LIMIT = 56 * 1024 * 1024
MM_TILE = 1024


def _cparams(*sem):
    return pltpu.CompilerParams(dimension_semantics=sem, vmem_limit_bytes=VMEM_LIMIT)


def _tile(dim, cap):
    if dim <= cap:
        return dim
    t = (cap // 128) * 128
    while dim % t:
        t -= 128
    return t


def _sigmoid(x):
    return 1.0 / (1.0 + jnp.exp(-x))


def _silu(x):
    return x * _sigmoid(x)


def _softplus(x):
    return jnp.maximum(x, 0.0) + jnp.log(1.0 + jnp.exp(-jnp.abs(x)))


def _dot(a, b, dims, precision=None):
    if precision is None:
        a, b = a.astype(BF16), b.astype(BF16)
    return lax.dot_general(a, b, (dims, ((), ())), precision=precision, preferred_element_type=F32)


def _nn(a, b, precision=None):
    return _dot(a, b, ((1,), (0,)), precision)


def _nt(a, b, precision=None):
    return _dot(a, b, ((1,), (1,)), precision)


def _tn(a, b, precision=None):
    return _dot(a, b, ((0,), (0,)), precision)


def _sum_rows_split(mat01, x):
    m = mat01.astype(BF16)
    hi = x.astype(BF16)
    low = (x - hi.astype(F32)).astype(BF16)
    return _nn(m, hi) + _nn(m, low)


@functools.partial(jax.custom_vjp, nondiff_argnums=(1,))
def _shift_rows(x, d):
    return x if d == 0 else pltpu.roll(x, d, 0)


def _shift_rows_fwd(x, d):
    return _shift_rows(x, d), None


def _shift_rows_bwd(d, _, ct):
    return ((ct if d == 0 else pltpu.roll(ct, ct.shape[0] - d, 0)),)


_shift_rows.defvjp(_shift_rows_fwd, _shift_rows_bwd)


def _iota2(shape):
    return lax.broadcasted_iota(jnp.int32, shape, 0), lax.broadcasted_iota(jnp.int32, shape, 1)


def _lane_pick(x, i):
    lane = lax.broadcasted_iota(jnp.int32, x.shape, 1)
    return jnp.sum(jnp.where(lane == i, x, 0.0), axis=1, keepdims=True)


def _hgrn_block(qr, fr, ir, zr, st0, lb, nw):
    rows = qr.shape[0]
    r, c = _iota2((CHUNK, CHUNK))
    halves = [1 << j for j in range(CHUNK.bit_length() - 1)]
    mats = [c <= r, c > r]
    pairs = []
    for hb in halves:
        same = (r // hb) == (c // hb)
        if hb > 1:
            mats += [(c <= r) & same, (c > r) & same]
        pairs.append(((r // (2 * hb)) == (c // (2 * hb))) & ((r // hb) == (c // hb) + 1))
    stack = jnp.concatenate([m.astype(F32) for m in mats], axis=0)

    q = _silu(qr) * (HD ** -0.5)
    fg = lb + (1.0 - lb) * _sigmoid(fr)
    logf = jnp.log(jnp.maximum(fg, MIN_F))
    kk = 1.0 - fg
    v = ir

    chunks = [slice(s, s + CHUNK) for s in range(0, rows, CHUNK)]
    cums = [_sum_rows_split(stack, logf[sl]) for sl in chunks]
    part = lambda i: jnp.concatenate([cs[i * CHUNK:(i + 1) * CHUNK] for cs in cums], axis=0)
    qg = q * jnp.exp(part(0))
    ks = kk * jnp.exp(part(1))
    q_lv = [q * jnp.exp(logf)] + [q * jnp.exp(part(2 * j)) for j in range(1, len(halves))]
    k_lv = [kk] + [kk * jnp.exp(part(2 * j + 1)) for j in range(1, len(halves))]
    st = st0
    outs = []
    for sl in chunks:
        scores = jnp.where(pairs[0], _nt(q_lv[0][sl], k_lv[0][sl]), 0.0)
        for j in range(1, len(halves)):
            scores += jnp.where(pairs[j], _nt(q_lv[j][sl], k_lv[j][sl]), 0.0)
        outs.append(_nn(scores, v[sl]) + _nt(qg[sl], st))
        st = st * jnp.exp(jnp.sum(logf[sl], axis=0, keepdims=True)) + _tn(v[sl], ks[sl])
    o = jnp.concatenate(outs, axis=0) + jnp.sum(q * kk, axis=1, keepdims=True) * v
    y = o * lax.rsqrt(jnp.mean(o * o, axis=1, keepdims=True) + NORM_EPS) * nw * _silu(zr)
    return y, st


def _unit_lower_inverses(ms):
    r, c = _iota2(ms[0].shape)
    xs = [jnp.where(r == c, 1.0, 0.0) - jnp.where((r // 2) == (c // 2), m, 0.0) for m in ms]
    b = 2
    while b < CHUNK:
        pick = ((r // (2 * b)) == (c // (2 * b))) & ((r // b) != (c // b))
        ts = [_nn(x, jnp.where(pick, m, 0.0)) for x, m in zip(xs, ms)]
        xs = [x - _nn(t, x) for x, t in zip(xs, ts)]
        b *= 2
    return tuple(x.astype(BF16) for x in xs)


@jax.custom_vjp
def _known_inverses(ms, xs):
    return xs


def _known_inverses_fwd(ms, xs):
    return xs, xs


def _known_inverses_bwd(xs, cts):
    r, c = _iota2(xs[0].shape)
    keep = (c < r) & ((r // CHUNK) == (c // CHUNK))
    ts = [_tn(x, ct) for x, ct in zip(xs, cts)]
    return (tuple(jnp.where(keep, -_nt(t, x), 0.0) for t, x in zip(ts, xs)), tuple(jnp.zeros_like(x) for x in xs))


_known_inverses.defvjp(_known_inverses_fwd, _known_inverses_bwd)


def _chunk_cumsum(x):
    row = lax.broadcasted_iota(jnp.int32, x.shape, 0) % CHUNK
    d = 1
    while d < CHUNK:
        x = x + jnp.where(row >= d, _shift_rows(x, d), 0.0)
        d *= 2
    return x


def _gdn_block(x_ext, z, ba, s0a, s0b, w0, w1, w2, w3, alog, dtb, nw, known=None):
    rows = z.shape[0]
    conv = (w0 * _shift_rows(x_ext, 3) + w1 * _shift_rows(x_ext, 2) + w2 * _shift_rows(x_ext, 1) + w3 * x_ext)
    cc = _silu(conv[8:])
    qc, kc = cc[:, 0:HD], cc[:, HD:2 * HD]
    q = qc * lax.rsqrt(jnp.sum(qc * qc, axis=1, keepdims=True) + L2_EPS) * (HD ** -0.5)
    k = kc * lax.rsqrt(jnp.sum(kc * kc, axis=1, keepdims=True) + L2_EPS)

    r, c = _iota2((GROUP, GROUP))
    same = (r // CHUNK) == (c // CHUNK)
    causal, strict, eye = same & (c <= r), same & (c < r), r == c
    heads = (0, 1)
    groups = [slice(lo, lo + GROUP) for lo in range(0, rows, GROUP)]
    chunks = [slice(lo, lo + CHUNK) for lo in range(0, rows, CHUNK)]

    v, loga, g_w, kb, kg, qg = [], [], [], [], [], []
    for i in heads:
        v.append(cc[:, (2 + i) * HD:(3 + i) * HD])
        beta = _sigmoid(_lane_pick(ba, i))
        a_neg = -jnp.exp(_lane_pick(alog, i))
        loga.append(a_neg * _softplus(_lane_pick(ba, 2 + i) + _lane_pick(dtb, i)))
        g_w.append(_chunk_cumsum(jnp.broadcast_to(loga[i], (rows, HD))))
        kb.append(k * beta)
        kg.append(k * jnp.exp(g_w[i]))
        qg.append(q * jnp.exp(g_w[i]))

    systems = [(i, gs) for gs in groups for i in heads]
    dec_c, ms = [], []
    for i, gs in systems:
        g_sq = g_w[i][gs]
        g_row = jnp.sum(jnp.where(eye, g_sq, 0.0), axis=0, keepdims=True)
        diff = g_sq - g_row
        dec_c.append(jnp.where(causal, jnp.exp(jnp.where(causal, diff, 0.0)), 0.0))
        ms.append(jnp.where(strict, _nt(k[gs], kb[i][gs]) * dec_c[-1], 0.0))
    xs = _unit_lower_inverses(tuple(ms)) if known is None else _known_inverses(tuple(ms), known)
    u = [[None] * len(groups) for _ in heads]
    w = [[None] * len(groups) for _ in heads]
    qk = [[None] * len(groups) for _ in heads]
    for n, (i, gs) in enumerate(systems):
        j = n // len(heads)
        u[i][j] = _nn(xs[n], v[i][gs])
        w[i][j] = _nn(xs[n], kg[i][gs])
        qk[i][j] = _nt(q[gs], kb[i][gs]) * dec_c[n]
    u = [jnp.concatenate(p, axis=0) for p in u]
    w = [jnp.concatenate(p, axis=0) for p in w]

    decay, p_mat, q_mat = {}, {}, {}
    for n, sl in enumerate(chunks):
        for i in heads:
            g_last = jnp.sum(loga[i][sl], axis=0, keepdims=True)
            kd = kb[i][sl] * jnp.exp(g_last - g_w[i][sl])
            decay[n, i] = jnp.exp(g_last)
            p_mat[n, i] = -_tn(kd, w[i][sl])
            q_mat[n, i] = _tn(kd, u[i][sl])
    s = [s0a, s0b]
    s_at = {}
    for n in range(len(chunks)):
        for i in heads:
            s_at[n, i] = s[i]
            s[i] = s[i] * decay[n, i] + _nn(p_mat[n, i], s[i]) + q_mat[n, i]

    ys = []
    for i in heads:
        e = jnp.concatenate([u[i][sl] - _nn(w[i][sl], s_at[n, i]) for n, sl in enumerate(chunks)], axis=0)
        o_state = jnp.concatenate([_nn(qg[i][sl], s_at[n, i]) for n, sl in enumerate(chunks)], axis=0)
        o = o_state + jnp.concatenate([_nn(qk[i][j], e[gs]) for j, gs in enumerate(groups)], axis=0)
        zi = z[:, i * HD:(i + 1) * HD]
        ys.append(o * lax.rsqrt(jnp.mean(o * o, axis=1, keepdims=True) + NORM_EPS) * nw * _silu(zi))
    return (jnp.concatenate(ys, axis=1), s[0], s[1]), (xs, s_at)


def _add_to_tail(x, tail):
    return x + jnp.concatenate([jnp.zeros((x.shape[0] - 8, x.shape[1]), x.dtype), tail], axis=0)


def _conv_a_block(ab, ac_ext, ax_ext, az, w0, w1, w2):
    u = ac_ext * ax_ext
    conv = (w0 * _shift_rows(u, 2) + w1 * _shift_rows(u, 1) + w2 * u)[8:]
    return ab * conv * _silu(az)


def _matmul(a, b, mode, name, residual=None, out_dtype=F32):
    if mode == "nn":
        (m, k), n = a.shape, b.shape[1]
    elif mode == "nt":
        (m, k), n = a.shape, b.shape[0]
    else:
        (k, m), n = a.shape, b.shape[1]
    tm, tn, tk = _tile(m, MM_TILE), _tile(n, MM_TILE), _tile(k, MM_TILE)
    if mode == "tn":
        tk = _tile(k, 2 * MM_TILE)
    elif k == tk:
        tm = _tile(m, 2 * MM_TILE)
    nk = k // tk
    dims = {"nn": ((1,), (0,)), "nt": ((1,), (1,)), "tn": ((0,), (0,))}[mode]
    a_spec = pl.BlockSpec((tk, tm), lambda i, j, s: (s, i)) if mode == "tn" else pl.BlockSpec((tm, tk), lambda i, j, s: (i, s))
    b_spec = pl.BlockSpec((tn, tk), lambda i, j, s: (j, s)) if mode == "nt" else pl.BlockSpec((tk, tn), lambda i, j, s: (s, j))
    o_spec = pl.BlockSpec((tm, tn), lambda i, j, s: (i, j))
    has_res = residual is not None

    def finish(out, r_ref, o_ref):
        if has_res:
            out = out + r_ref[...]
        o_ref[...] = out.astype(out_dtype)

    def body_one_pass(*refs):
        finish(_dot(refs[0][...], refs[1][...], dims), refs[2] if has_res else None, refs[-1])

    def body_reduce(*refs):
        a_ref, b_ref = refs[0], refs[1]
        r_ref = refs[2] if has_res else None
        o_ref, acc_ref = refs[-2], refs[-1]
        s = pl.program_id(2)

        @pl.when(s == 0)
        def _():
            acc_ref[...] = jnp.zeros_like(acc_ref)

        acc_ref[...] += _dot(a_ref[...], b_ref[...], dims)

        @pl.when(s == nk - 1)
        def _():
            finish(acc_ref[...], r_ref, o_ref)

    args, specs = [a, b], [a_spec, b_spec]
    if has_res:
        args.append(residual)
        specs.append(o_spec)
    return pl.pallas_call(
        body_one_pass if nk == 1 else body_reduce, name=name, grid=(m // tm, n // tn, nk), in_specs=specs, out_specs=o_spec,
        out_shape=jax.ShapeDtypeStruct((m, n), out_dtype),
        scratch_shapes=[] if nk == 1 else [pltpu.VMEM((tm, tn), F32)],
        compiler_params=_cparams("parallel", "parallel", "arbitrary"))(*args)


def _matmul_nt_sum(pairs, name, exchange=None):
    m, n = pairs[0][0].shape[0], pairs[0][1].shape[0]
    tm, tn = _tile(m, MM_TILE), _tile(n, MM_TILE)
    tks = [_tile(a.shape[1], MM_TILE) for a, _ in pairs]
    nks = [a.shape[1] // tk for (a, _), tk in zip(pairs, tks)]
    offs = [sum(nks[:i]) for i in range(len(pairs))]
    total = sum(nks)

    def body(*refs):
        o_ref, acc_ref = refs[-2], refs[-1]
        s = pl.program_id(2)

        @pl.when(s == 0)
        def _():
            acc_ref[...] = jnp.zeros_like(acc_ref)

        for i, (off, nk) in enumerate(zip(offs, nks)):
            @pl.when((s >= off) & (s < off + nk))
            def _(i=i):
                acc_ref[...] += _dot(refs[2 * i][...], refs[2 * i + 1][...], ((1,), (1,)))

        @pl.when(s == total - 1)
        def _():
            o_ref[...] = acc_ref[...]

    args, specs = [], []
    for (a, b), tk, off, nk in zip(pairs, tks, offs, nks):
        k_of = lambda s, off=off, nk=nk: jnp.clip(s - off, 0, nk - 1)
        args += [a, b]
        specs += [pl.BlockSpec((tm, tk), lambda i, j, s, k_of=k_of: (i, k_of(s))),
                  pl.BlockSpec((tn, tk), lambda i, j, s, k_of=k_of: (j, k_of(s)))]
    (out,), exchanged = _call_with_exchange(
        body, name=name, grid=(m // tm, n // tn, total), in_specs=specs,
        out_specs=[pl.BlockSpec((tm, tn), lambda i, j, s: (i, j))], out_shape=[jax.ShapeDtypeStruct((m, n), F32)],
        scratch_shapes=[pltpu.VMEM((tm, tn), F32)], args=args, exchange=exchange)
    return out, exchanged


def _rmsnorm_fwd(x, w, name):
    t = x.shape[0]
    blk = _tile(t, 1024)

    def body(x_ref, w_ref, h_ref):
        xv = x_ref[...]
        h_ref[...] = (xv * lax.rsqrt(jnp.mean(xv * xv, axis=1, keepdims=True) + NORM_EPS) * w_ref[...]).astype(BF16)

    return pl.pallas_call(
        body, name=name, grid=(t // blk,),
        in_specs=[pl.BlockSpec((blk, D), lambda i: (i, 0)), pl.BlockSpec((1, D), lambda i: (0, 0))],
        out_specs=pl.BlockSpec((blk, D), lambda i: (i, 0)), out_shape=jax.ShapeDtypeStruct((t, D), BF16),
        compiler_params=_cparams("parallel"))(x, w)


def _rmsnorm_bwd(dh, x, w, dxo, name, exchange=None):
    t = x.shape[0]
    blk = _tile(t, 512)

    def body(dh_ref, x_ref, w_ref, dxo_ref, dx_ref, dw_ref):
        @pl.when(pl.program_id(0) == 0)
        def _():
            dw_ref[...] = jnp.zeros_like(dw_ref)

        xv, dhv = x_ref[...], dh_ref[...]
        rs = lax.rsqrt(jnp.mean(xv * xv, axis=1, keepdims=True) + NORM_EPS)
        xh = xv * rs
        dw_ref[...] += jnp.sum(dhv * xh, axis=0, keepdims=True)
        dxh = dhv * w_ref[...]
        dx_ref[...] = rs * (dxh - xh * jnp.mean(dxh * xh, axis=1, keepdims=True)) + dxo_ref[...]

    row = pl.BlockSpec((blk, D), lambda i: (i, 0))
    vec = pl.BlockSpec((1, D), lambda i: (0, 0))
    return _call_with_exchange(
        body, name=name, grid=(t // blk,), in_specs=[row, row, vec, row], out_specs=[row, vec],
        out_shape=[jax.ShapeDtypeStruct((t, D), F32), jax.ShapeDtypeStruct((1, D), F32)],
        scratch_shapes=[], args=(dh, x, w, dxo), exchange=exchange)


def _loss_head(x, w, target, name):
    t = x.shape[0]
    blk = _tile(t, 512)

    def body(x_ref, w_ref, t_ref, loss_ref, dx_ref, dw_ref):
        @pl.when(pl.program_id(0) == 0)
        def _():
            dw_ref[...] = jnp.zeros_like(dw_ref)
            loss_ref[...] = jnp.zeros_like(loss_ref)

        xv = x_ref[...]
        rs = lax.rsqrt(jnp.mean(xv * xv, axis=1, keepdims=True) + NORM_EPS)
        xh = xv * rs
        err = xh * w_ref[...] - t_ref[...]
        loss_ref[...] += 0.5 * jnp.sum(jnp.mean(err * err, axis=1, keepdims=True), axis=0, keepdims=True)
        dy = err * (1.0 / D)
        dw_ref[...] += jnp.sum(dy * xh, axis=0, keepdims=True)
        dxh = dy * w_ref[...]
        dx_ref[...] = rs * (dxh - xh * jnp.mean(dxh * xh, axis=1, keepdims=True))

    row = pl.BlockSpec((blk, D), lambda i: (i, 0))
    vec = pl.BlockSpec((1, D), lambda i: (0, 0))
    return pl.pallas_call(
        body, name=name, grid=(t // blk,), in_specs=[row, vec, row],
        out_specs=[pl.BlockSpec((1, 1), lambda i: (0, 0)), row, vec],
        out_shape=[jax.ShapeDtypeStruct((1, 1), F32), jax.ShapeDtypeStruct((t, D), F32), jax.ShapeDtypeStruct((1, D), F32)],
        compiler_params=_cparams("arbitrary"))(x, w, target)


def _lbs_of(lb):
    r = lax.broadcasted_iota(jnp.int32, lb.shape, 0)
    real = r < DEPTH
    mx = lax.stop_gradient(jnp.max(jnp.where(real, lb, -jnp.inf), axis=0, keepdims=True))
    e = jnp.where(real, jnp.exp(jnp.where(real, lb - mx, 0.0)), 0.0)
    p = e / jnp.sum(e, axis=0, keepdims=True)
    out = jnp.zeros_like(lb)
    run = jnp.zeros_like(mx)
    for l in range(1, DEPTH):
        run = run + jnp.sum(jnp.where(r == l, p, 0.0), axis=0, keepdims=True)
        out = out + jnp.where(r == l, run, 0.0)
    return out


def _lower_bounds_fwd(lbp, name):
    def body(lb_ref, o_ref):
        o_ref[...] = _lbs_of(lb_ref[...])

    return pl.pallas_call(body, name=name, out_shape=jax.ShapeDtypeStruct(lbp.shape, F32))(lbp)


def _lower_bounds_bwd(lbp, dlbs, name):
    def body(lb_ref, d_ref, o_ref):
        _, vjp = jax.vjp(_lbs_of, lb_ref[...])
        o_ref[...] = vjp(d_ref[...])[0]

    return pl.pallas_call(body, name=name, out_shape=jax.ShapeDtypeStruct(lbp.shape, F32))(lbp, dlbs)


def _branch_a_fwd(pa, cw, name):
    t = pa.shape[0]
    blk = _tile(t, 512)
    W = CONV_W

    def body(p_ref, w_ref, y_ref, hc_ref, hx_ref):
        @pl.when(pl.program_id(0) == 0)
        def _():
            hc_ref[...] = jnp.zeros_like(hc_ref)
            hx_ref[...] = jnp.zeros_like(hx_ref)

        ac, ax = p_ref[:, W:2 * W], p_ref[:, 2 * W:3 * W]
        y_ref[...] = _conv_a_block(
            p_ref[:, 0:W], jnp.concatenate([hc_ref[...], ac], axis=0), jnp.concatenate([hx_ref[...], ax], axis=0),
            p_ref[:, 3 * W:4 * W], w_ref[0:1, :], w_ref[1:2, :], w_ref[2:3, :]).astype(BF16)
        hc_ref[...] = p_ref[blk - 8:blk, W:2 * W]
        hx_ref[...] = p_ref[blk - 8:blk, 2 * W:3 * W]

    return pl.pallas_call(
        body, name=name, grid=(t // blk,),
        in_specs=[pl.BlockSpec((blk, NA), lambda i: (i, 0)), pl.BlockSpec((3, W), lambda i: (0, 0))],
        out_specs=pl.BlockSpec((blk, W), lambda i: (i, 0)), out_shape=jax.ShapeDtypeStruct((t, W), BF16),
        scratch_shapes=[pltpu.VMEM((8, W), F32), pltpu.VMEM((8, W), F32)],
        compiler_params=_cparams("arbitrary"))(pa, cw)


def _branch_a_bwd(pa, cw, dy, name):
    t = pa.shape[0]
    blk = _tile(t, 512)
    nt_ = t // blk
    W = CONV_W
    hb = blk // 8

    def body(p_ref, halo_ref, w_ref, dy_ref, dp_ref, dw_ref, chc_ref, chx_ref):
        i = pl.program_id(0)

        @pl.when(i == 0)
        def _():
            chc_ref[...] = jnp.zeros_like(chc_ref)
            chx_ref[...] = jnp.zeros_like(chx_ref)
            dw_ref[...] = jnp.zeros_like(dw_ref)

        keep = 1.0 - (i == nt_ - 1).astype(F32)
        hc = halo_ref[:, W:2 * W] * keep
        hx = halo_ref[:, 2 * W:3 * W] * keep
        ac_ext = jnp.concatenate([hc, p_ref[:, W:2 * W]], axis=0)
        ax_ext = jnp.concatenate([hx, p_ref[:, 2 * W:3 * W]], axis=0)
        _, vjp = jax.vjp(_conv_a_block, p_ref[:, 0:W], ac_ext, ax_ext, p_ref[:, 3 * W:4 * W],
                         w_ref[0:1, :], w_ref[1:2, :], w_ref[2:3, :])
        dab, dac, dax, daz, dw0, dw1, dw2 = vjp(dy_ref[...].astype(F32))
        dp_ref[:, 0:W] = dab.astype(BF16)
        dp_ref[:, W:2 * W] = _add_to_tail(dac[8:], chc_ref[...]).astype(BF16)
        dp_ref[:, 2 * W:3 * W] = _add_to_tail(dax[8:], chx_ref[...]).astype(BF16)
        dp_ref[:, 3 * W:4 * W] = daz.astype(BF16)
        chc_ref[...] = dac[:8] * keep
        chx_ref[...] = dax[:8] * keep
        dw_ref[0:1, :] += dw0
        dw_ref[1:2, :] += dw1
        dw_ref[2:3, :] += dw2

    rev = lambda i: (nt_ - 1 - i, 0)
    return pl.pallas_call(
        body, name=name, grid=(nt_,),
        in_specs=[pl.BlockSpec((blk, NA), rev),
                  pl.BlockSpec((8, NA), lambda i: (jnp.maximum((nt_ - 1 - i) * hb - 1, 0), 0)),
                  pl.BlockSpec((3, W), lambda i: (0, 0)),
                  pl.BlockSpec((blk, W), rev)],
        out_specs=[pl.BlockSpec((blk, NA), rev), pl.BlockSpec((3, W), lambda i: (0, 0))],
        out_shape=[jax.ShapeDtypeStruct((t, NA), BF16), jax.ShapeDtypeStruct((3, W), F32)],
        scratch_shapes=[pltpu.VMEM((8, W), F32), pltpu.VMEM((8, W), F32)],
        compiler_params=_cparams("arbitrary"))(pa, pa, cw, dy)


def _block_rows(t, chunks):
    return min(t, chunks * CHUNK)


def _branch_b_fwd(pb, lbs_row, nw, name, exchange=None):
    t = pb.shape[0]
    rows = _block_rows(t, HGRN_BLOCK_CHUNKS)
    nch = t // rows

    def body(p_ref, lb_ref, nw_ref, y_ref, s_ref, st_ref):
        @pl.when(pl.program_id(1) == 0)
        def _():
            st_ref[...] = jnp.zeros_like(st_ref)

        s_ref[0, 0] = st_ref[...]
        y, st1 = _hgrn_block(p_ref[:, 0:HD], p_ref[:, HD:2 * HD], p_ref[:, 2 * HD:3 * HD], p_ref[:, 3 * HD:4 * HD],
                             st_ref[...], lb_ref[...], nw_ref[...])
        y_ref[...] = y.astype(BF16)
        st_ref[...] = st1

    return _call_with_exchange(
        body, name=name, grid=(HGRN_HEADS, nch),
        in_specs=[pl.BlockSpec((rows, 4 * HD), lambda h, i: (i, h)),
                  pl.BlockSpec((1, HD), lambda h, i: (0, h)),
                  pl.BlockSpec((1, HD), lambda h, i: (0, 0))],
        out_specs=[pl.BlockSpec((rows, HD), lambda h, i: (i, h)),
                   pl.BlockSpec((1, 1, HD, HD), lambda h, i: (h, i, 0, 0))],
        out_shape=[jax.ShapeDtypeStruct((t, HGRN_HEADS * HD), BF16),
                   jax.ShapeDtypeStruct((HGRN_HEADS, nch, HD, HD), F32)],
        scratch_shapes=[pltpu.VMEM((HD, HD), F32)],
        args=(pb, lbs_row, nw), exchange=exchange)


def _branch_b_bwd(pb, states, lbs_row, nw, dy, name, exchange=None):
    t = pb.shape[0]
    rows = _block_rows(t, HGRN_BLOCK_CHUNKS)
    nch = t // rows

    def body(p_ref, s_ref, lb_ref, nw_ref, dy_ref, dp_ref, dlb_ref, dnw_ref, ds_ref):
        h, i = pl.program_id(0), pl.program_id(1)

        @pl.when(i == 0)
        def _():
            ds_ref[...] = jnp.zeros_like(ds_ref)
            dlb_ref[...] = jnp.zeros_like(dlb_ref)

        @pl.when((i == 0) & (h == 0))
        def _():
            dnw_ref[...] = jnp.zeros_like(dnw_ref)

        _, vjp = jax.vjp(_hgrn_block, p_ref[:, 0:HD], p_ref[:, HD:2 * HD], p_ref[:, 2 * HD:3 * HD],
                         p_ref[:, 3 * HD:4 * HD], s_ref[0, 0], lb_ref[...], nw_ref[...])
        dq, df, di, dz, ds0, dlb, dnw = vjp((dy_ref[...].astype(F32), ds_ref[...]))
        dp_ref[:, 0:HD] = dq.astype(BF16)
        dp_ref[:, HD:2 * HD] = df.astype(BF16)
        dp_ref[:, 2 * HD:3 * HD] = di.astype(BF16)
        dp_ref[:, 3 * HD:4 * HD] = dz.astype(BF16)
        ds_ref[...] = ds0
        dlb_ref[...] += dlb
        dnw_ref[...] += dnw

    rev = lambda h, i: (nch - 1 - i, h)
    return _call_with_exchange(
        body, name=name, grid=(HGRN_HEADS, nch),
        in_specs=[pl.BlockSpec((rows, 4 * HD), rev),
                  pl.BlockSpec((1, 1, HD, HD), lambda h, i: (h, nch - 1 - i, 0, 0)),
                  pl.BlockSpec((1, HD), lambda h, i: (0, h)),
                  pl.BlockSpec((1, HD), lambda h, i: (0, 0)),
                  pl.BlockSpec((rows, HD), rev)],
        out_specs=[pl.BlockSpec((rows, 4 * HD), rev),
                   pl.BlockSpec((1, HD), lambda h, i: (0, h)),
                   pl.BlockSpec((1, HD), lambda h, i: (0, 0))],
        out_shape=[jax.ShapeDtypeStruct((t, NB), BF16), jax.ShapeDtypeStruct((1, HGRN_HEADS * HD), F32),
                   jax.ShapeDtypeStruct((1, HD), F32)],
        scratch_shapes=[pltpu.VMEM((HD, HD), F32)],
        args=(pb, states, lbs_row, nw, dy), exchange=exchange)


def _branch_c_fwd(pc, cw, cpar, nw, name, exchange=None):
    t = pc.shape[0]
    back_rows = _block_rows(t, GDN_BLOCK_CHUNKS)
    rows = _block_rows(t, GDN_FORWARD_CHUNKS)
    nch, per_step = t // rows, rows // back_rows
    XW = 4 * HD
    nsys = 2 * back_rows // GROUP

    def body(p_ref, w_ref, cp_ref, nw_ref, y_ref, s_ref, x_ref, sa_ref, sb_ref, halo_ref):
        @pl.when(pl.program_id(1) == 0)
        def _():
            sa_ref[...] = jnp.zeros_like(sa_ref)
            sb_ref[...] = jnp.zeros_like(sb_ref)
            halo_ref[...] = jnp.zeros_like(halo_ref)

        x_ext = jnp.concatenate([halo_ref[...], p_ref[:, 0:XW]], axis=0)
        (y, s1a, s1b), (xs, s_at) = _gdn_block(
            x_ext, p_ref[:, XW:XW + 2 * HD], p_ref[:, XW + 2 * HD:XW + 3 * HD],
            sa_ref[...], sb_ref[...], w_ref[0:1, :], w_ref[1:2, :], w_ref[2:3, :], w_ref[3:4, :],
            cp_ref[0, 0:1, :], cp_ref[0, 1:2, :], nw_ref[...])
        for b in range(per_step):
            for i in range(2):
                s_ref[0, b, i] = s_at[b * (back_rows // CHUNK), i]
            for n in range(nsys):
                x_ref[0, b, n] = xs[b * nsys + n]
        y_ref[...] = y.astype(BF16)
        sa_ref[...] = s1a
        sb_ref[...] = s1b
        halo_ref[...] = p_ref[rows - 8:rows, 0:XW]

    return _call_with_exchange(
        body, name=name, grid=(GDN_QK_HEADS, nch),
        in_specs=[pl.BlockSpec((rows, C_HEAD), lambda h, i: (i, h)),
                  pl.BlockSpec((4, XW), lambda h, i: (0, h)),
                  pl.BlockSpec((1, 8, HD), lambda h, i: (h, 0, 0)),
                  pl.BlockSpec((1, HD), lambda h, i: (0, 0))],
        out_specs=[pl.BlockSpec((rows, 2 * HD), lambda h, i: (i, h)),
                   pl.BlockSpec((1, per_step, 2, HD, HD), lambda h, i: (h, i, 0, 0, 0)),
                   pl.BlockSpec((1, per_step, nsys, GROUP, GROUP), lambda h, i: (h, i, 0, 0, 0))],
        out_shape=[jax.ShapeDtypeStruct((t, 2 * GDN_QK_HEADS * HD), BF16),
                   jax.ShapeDtypeStruct((GDN_QK_HEADS, nch * per_step, 2, HD, HD), F32),
                   jax.ShapeDtypeStruct((GDN_QK_HEADS, nch * per_step, nsys, GROUP, GROUP), BF16)],
        scratch_shapes=[pltpu.VMEM((HD, HD), F32), pltpu.VMEM((HD, HD), F32), pltpu.VMEM((8, XW), F32)],
        args=(pc, cw, cpar, nw), exchange=exchange)


def _branch_c_bwd(pc, states, inverses, cw, cpar, nw, dy, name, exchange=None):
    t = pc.shape[0]
    rows = _block_rows(t, GDN_BLOCK_CHUNKS)
    nch = t // rows
    XW = 4 * HD
    hb = rows // 8

    nsys = 2 * rows // GROUP

    def body(p_ref, halo_ref, s_ref, x_ref, w_ref, cp_ref, nw_ref, dy_ref, dp_ref, dw_ref, dcp_ref, dnw_ref,
             dsa_ref, dsb_ref, carry_ref):
        h, i = pl.program_id(0), pl.program_id(1)

        @pl.when(i == 0)
        def _():
            dsa_ref[...] = jnp.zeros_like(dsa_ref)
            dsb_ref[...] = jnp.zeros_like(dsb_ref)
            carry_ref[...] = jnp.zeros_like(carry_ref)
            dw_ref[...] = jnp.zeros_like(dw_ref)
            dcp_ref[...] = jnp.zeros_like(dcp_ref)

        @pl.when((i == 0) & (h == 0))
        def _():
            dnw_ref[...] = jnp.zeros_like(dnw_ref)

        keep = 1.0 - (i == nch - 1).astype(F32)
        x_ext = jnp.concatenate([halo_ref[:, 0:XW] * keep, p_ref[:, 0:XW]], axis=0)
        block = functools.partial(_gdn_block, known=tuple(x_ref[0, 0, n] for n in range(nsys)))
        _, vjp, _ = jax.vjp(block, x_ext, p_ref[:, XW:XW + 2 * HD], p_ref[:, XW + 2 * HD:XW + 3 * HD],
                            s_ref[0, 0, 0], s_ref[0, 0, 1], w_ref[0:1, :], w_ref[1:2, :], w_ref[2:3, :], w_ref[3:4, :],
                            cp_ref[0, 0:1, :], cp_ref[0, 1:2, :], nw_ref[...], has_aux=True)
        dx, dz, dba, dsa, dsb, dw0, dw1, dw2, dw3, dal, ddt, dnw = vjp((dy_ref[...].astype(F32), dsa_ref[...], dsb_ref[...]))
        dp_ref[:, 0:XW] = _add_to_tail(dx[8:], carry_ref[...]).astype(BF16)
        dp_ref[:, XW:XW + 2 * HD] = dz.astype(BF16)
        dp_ref[:, XW + 2 * HD:XW + 3 * HD] = dba.astype(BF16)
        carry_ref[...] = dx[:8] * keep
        dsa_ref[...] = dsa
        dsb_ref[...] = dsb
        dw_ref[0:1, :] += dw0
        dw_ref[1:2, :] += dw1
        dw_ref[2:3, :] += dw2
        dw_ref[3:4, :] += dw3
        dcp_ref[0, 0:1, :] += dal
        dcp_ref[0, 1:2, :] += ddt
        dnw_ref[...] += dnw

    rev = lambda h, i: (nch - 1 - i, h)
    return _call_with_exchange(
        body, name=name, grid=(GDN_QK_HEADS, nch),
        in_specs=[pl.BlockSpec((rows, C_HEAD), rev),
                  pl.BlockSpec((8, C_HEAD), lambda h, i: (jnp.maximum((nch - 1 - i) * hb - 1, 0), h)),
                  pl.BlockSpec((1, 1, 2, HD, HD), lambda h, i: (h, nch - 1 - i, 0, 0, 0)),
                  pl.BlockSpec((1, 1, nsys, GROUP, GROUP), lambda h, i: (h, nch - 1 - i, 0, 0, 0)),
                  pl.BlockSpec((4, XW), lambda h, i: (0, h)),
                  pl.BlockSpec((1, 8, HD), lambda h, i: (h, 0, 0)),
                  pl.BlockSpec((1, HD), lambda h, i: (0, 0)),
                  pl.BlockSpec((rows, 2 * HD), rev)],
        out_specs=[pl.BlockSpec((rows, C_HEAD), rev),
                   pl.BlockSpec((4, XW), lambda h, i: (0, h)),
                   pl.BlockSpec((1, 8, HD), lambda h, i: (h, 0, 0)),
                   pl.BlockSpec((1, HD), lambda h, i: (0, 0))],
        out_shape=[jax.ShapeDtypeStruct((t, NC_COLS), BF16), jax.ShapeDtypeStruct((4, GDN_QK_HEADS * XW), F32),
                   jax.ShapeDtypeStruct((GDN_QK_HEADS, 8, HD), F32), jax.ShapeDtypeStruct((1, HD), F32)],
        scratch_shapes=[pltpu.VMEM((HD, HD), F32), pltpu.VMEM((HD, HD), F32), pltpu.VMEM((8, XW), F32)],
        args=(pc, pc, states, inverses, cw, cpar, nw, dy), exchange=exchange)


def _merge_fwd(pg, bg, ya, yb, yc, name):
    t = pg.shape[0]
    blk = _tile(t, 512)

    def body(g_ref, b_ref, a_ref, b2_ref, c_ref, o_ref):
        gate = _sigmoid(g_ref[...] + b_ref[...])
        o_ref[...] = (gate[:, 0:D] * a_ref[...] + gate[:, D:2 * D] * b2_ref[...] + gate[:, 2 * D:3 * D] * c_ref[...]).astype(BF16)

    row = pl.BlockSpec((blk, D), lambda i: (i, 0))
    return pl.pallas_call(
        body, name=name, grid=(t // blk,),
        in_specs=[pl.BlockSpec((blk, NG), lambda i: (i, 0)), pl.BlockSpec((1, NG), lambda i: (0, 0)), row, row, row],
        out_specs=row, out_shape=jax.ShapeDtypeStruct((t, D), BF16),
        compiler_params=_cparams("parallel"))(pg, bg, ya, yb, yc)


def _merge_project(pg, bg, ya, yb, yc, wo, x, name):
    t = pg.shape[0]
    blk = _tile(t, 512)

    def body(g_ref, b_ref, a_ref, b2_ref, c_ref, w_ref, x_ref, m_ref, o_ref):
        gate = _sigmoid(g_ref[...] + b_ref[...])
        merged = (gate[:, 0:D] * a_ref[...] + gate[:, D:2 * D] * b2_ref[...] + gate[:, 2 * D:3 * D] * c_ref[...]).astype(BF16)
        m_ref[...] = merged
        o_ref[...] = _nn(merged, w_ref[...]) + x_ref[...]

    row = pl.BlockSpec((blk, D), lambda i: (i, 0))
    return pl.pallas_call(
        body, name=name, grid=(t // blk,),
        in_specs=[pl.BlockSpec((blk, NG), lambda i: (i, 0)), pl.BlockSpec((1, NG), lambda i: (0, 0)), row, row, row,
                  pl.BlockSpec((D, D), lambda i: (0, 0)), row],
        out_specs=[row, row], out_shape=[jax.ShapeDtypeStruct((t, D), BF16), jax.ShapeDtypeStruct((t, D), F32)],
        compiler_params=_cparams("parallel"))(pg, bg, ya, yb, yc, wo, x)


def _merge_bwd(dm, pg, bg, ya, yb, yc, name):
    t = pg.shape[0]
    blk = _tile(t, 512)

    def body(dm_ref, g_ref, b_ref, a_ref, b2_ref, c_ref, dg_ref, da_ref, db_ref, dc_ref, dbg_ref):
        @pl.when(pl.program_id(0) == 0)
        def _():
            dbg_ref[...] = jnp.zeros_like(dbg_ref)

        gate = _sigmoid(g_ref[...] + b_ref[...])
        dmv = dm_ref[...].astype(F32)
        for j, (y_ref, dy_ref) in enumerate(((a_ref, da_ref), (b2_ref, db_ref), (c_ref, dc_ref))):
            gj = gate[:, j * D:(j + 1) * D]
            dy_ref[...] = (dmv * gj).astype(BF16)
            dgj = dmv * y_ref[...] * gj * (1.0 - gj)
            dg_ref[:, j * D:(j + 1) * D] = dgj.astype(BF16)
            dbg_ref[:, j * D:(j + 1) * D] += jnp.sum(dgj, axis=0, keepdims=True)

    row = pl.BlockSpec((blk, D), lambda i: (i, 0))
    wide = pl.BlockSpec((blk, NG), lambda i: (i, 0))
    vec = pl.BlockSpec((1, NG), lambda i: (0, 0))
    return pl.pallas_call(
        body, name=name, grid=(t // blk,), in_specs=[row, wide, vec, row, row, row],
        out_specs=[wide, row, row, row, vec],
        out_shape=[jax.ShapeDtypeStruct((t, NG), BF16)] + [jax.ShapeDtypeStruct((t, D), BF16)] * 3
                  + [jax.ShapeDtypeStruct((1, NG), F32)],
        compiler_params=_cparams("arbitrary"))(dm, pg, bg, ya, yb, yc)


def _adamw_math(w, g, m, v):
    m = ADAM_B1 * m + (1.0 - ADAM_B1) * g
    v = ADAM_B2 * v + (1.0 - ADAM_B2) * (g * g)
    m_hat = m / (1.0 - ADAM_B1 ** ADAM_STEP)
    v_hat = v / (1.0 - ADAM_B2 ** ADAM_STEP)
    delta = -ADAM_LR * (m_hat / (jnp.sqrt(v_hat) + ADAM_EPS) + ADAM_WD * w)
    return delta, m, v


def _sum_adamw(parts, w, m, v, name, exchange=None):
    layers = len(parts)
    r, c = parts[0].shape[1:]
    br = r if r <= 256 else 256
    nb = r // br
    assert r % br == 0 and w.shape == (layers * r, c)

    def body(*refs):
        w_ref, m_ref, v_ref, g_ref, d_ref, nm_ref, nv_ref = refs[layers:]
        for l in range(layers):
            @pl.when(pl.program_id(0) == l)
            def _(p_ref=refs[l]):
                g = p_ref[0].astype(F32)
                for k in range(1, N_DEV):
                    g = g + p_ref[k].astype(F32)
                g_ref[...] = g
                d_ref[...], nm_ref[...], nv_ref[...] = _adamw_math(w_ref[...], g, m_ref[...], v_ref[...])

    blk = pl.BlockSpec((br, c), lambda l, i: (l * nb + i, 0))
    part_specs = [pl.BlockSpec((N_DEV, br, c), lambda l, i, q=q: (0, jnp.where(l == q, i, jnp.where(l < q, 0, nb - 1)), 0))
                  for q in range(layers)]
    return _call_with_exchange(
        body, name=name, grid=(layers, nb), in_specs=part_specs + [blk, blk, blk], out_specs=[blk] * 4,
        out_shape=[jax.ShapeDtypeStruct((layers * r, c), F32)] * 4, scratch_shapes=[],
        args=(*parts, w, m, v), exchange=exchange)


def _adamw(g, w, m, v, name):
    def body(g_ref, w_ref, m_ref, v_ref, d_ref, nm_ref, nv_ref):
        d_ref[...], nm_ref[...], nv_ref[...] = _adamw_math(w_ref[...], g_ref[...], m_ref[...], v_ref[...])

    return pl.pallas_call(body, name=name, out_shape=[jax.ShapeDtypeStruct(w.shape, F32)] * 3)(g, w, m, v)


def _sum_slots(parts, name):
    def body(p_ref, o_ref):
        g = p_ref[0]
        for k in range(1, N_DEV):
            g = g + p_ref[k]
        o_ref[...] = g

    return pl.pallas_call(body, name=name, out_shape=jax.ShapeDtypeStruct(parts.shape[1:], F32))(parts)


def _exchange(srcs, name, broadcast):
    n = len(srcs)

    def body(*refs):
        copies = _exchange_copies(refs[:n], refs[n:2 * n], *refs[2 * n:], broadcast)
        for cp in copies:
            cp.start()
        for cp in copies:
            cp.wait()

    return pl.pallas_call(
        body, name=name, in_specs=[HBM_SPEC] * n, out_specs=[HBM_SPEC] * n, out_shape=_exchange_shapes(srcs, broadcast),
        scratch_shapes=_exchange_semaphores(n))(*srcs)


def _gather_two_level(srcs, name):
    n = len(srcs)

    def body(*refs):
        src_refs, dst_refs = refs[:n], refs[n:2 * n]
        send_sems, recv_sems, local_sems = refs[2 * n:]
        x, y, c = lax.axis_index("x"), lax.axis_index("y"), lax.axis_index("c")
        index_of = lambda px, py, pc: 4 * px + 2 * py + pc
        me, other_core = index_of(x, y, c), (x, y, 1 - c)
        chips = [(1 - x, y), (x, 1 - y), (1 - x, 1 - y)]

        def copy(k, a, block, to, src=None):
            return pltpu.make_async_remote_copy(
                src_ref=dst_refs[a].at[block] if src is None else src, dst_ref=dst_refs[a].at[block],
                send_sem=send_sems.at[k, a], recv_sem=recv_sems.at[k, a], device_id=to, device_id_type=MESH)

        local = [pltpu.make_async_copy(src_refs[a], dst_refs[a].at[me], local_sems.at[a]) for a in range(n)]
        first = [copy(0, a, me, other_core, src=src_refs[a]) for a in range(n)]
        first += [copy(1 + j, a, me, (*chip, c), src=src_refs[a]) for j, chip in enumerate(chips) for a in range(n)]
        for cp in local + first:
            cp.start()
        passed = []
        for j, chip in enumerate(chips):
            block = index_of(*chip, c)
            for a in range(n):
                copy(1 + j, a, block, (x, y, c)).wait_recv()
            for a in range(n):
                passed.append(copy(4 + j, a, block, other_core))
                passed[-1].start()
        for a in range(n):
            copy(0, a, index_of(x, y, 1 - c), (x, y, c)).wait_recv()
        for j, chip in enumerate(chips):
            for a in range(n):
                copy(4 + j, a, index_of(*chip, 1 - c), (x, y, c)).wait_recv()
        for cp in first + passed:
            cp.wait_send()
        for cp in local:
            cp.wait()

    return pl.pallas_call(
        body, name=name, in_specs=[HBM_SPEC] * n, out_specs=[HBM_SPEC] * n, out_shape=_exchange_shapes(srcs, True),
        scratch_shapes=_exchange_semaphores(n))(*srcs)


HBM_SPEC = pl.BlockSpec(memory_space=pltpu.HBM)


def _exchange_shapes(srcs, broadcast):
    return [jax.ShapeDtypeStruct((N_DEV,) + (s.shape if broadcast else s.shape[1:]), s.dtype) for s in srcs]


def _exchange_semaphores(n):
    return [pltpu.SemaphoreType.DMA((N_DEV - 1, n)), pltpu.SemaphoreType.DMA((N_DEV - 1, n)), pltpu.SemaphoreType.DMA((n,))]


def _exchange_copies(src_refs, dst_refs, send_sems, recv_sems, local_sems, broadcast):
    x, y, c = lax.axis_index("x"), lax.axis_index("y"), lax.axis_index("c")
    me = 4 * x + 2 * y + c
    copies = []
    for k in range(1, N_DEV):
        px = 1 - x if (k >> 2) & 1 else x
        py = 1 - y if (k >> 1) & 1 else y
        pc = 1 - c if k & 1 else c
        peer = 4 * px + 2 * py + pc
        for a, (src, dst) in enumerate(zip(src_refs, dst_refs)):
            copies.append(pltpu.make_async_remote_copy(
                src_ref=src if broadcast else src.at[peer], dst_ref=dst.at[me],
                send_sem=send_sems.at[k - 1, a], recv_sem=recv_sems.at[k - 1, a],
                device_id=(px, py, pc), device_id_type=MESH))
    for a, (src, dst) in enumerate(zip(src_refs, dst_refs)):
        copies.append(pltpu.make_async_copy(src if broadcast else src.at[me], dst.at[me], local_sems.at[a]))
    return copies


def _call_with_exchange(body, *, name, grid, in_specs, out_specs, out_shape, scratch_shapes, args, exchange):
    if exchange is None:
        outs = pl.pallas_call(body, name=name, grid=grid, in_specs=in_specs, out_specs=out_specs, out_shape=out_shape,
                              scratch_shapes=scratch_shapes,
                              compiler_params=_cparams(*["arbitrary"] * len(grid)))(*args)
        return outs, None
    srcs, broadcast = exchange
    n, n_in, n_out, n_scr = len(srcs), len(args), len(out_shape), len(scratch_shapes)
    steps = 1
    for g in grid:
        steps *= g

    def hosted(*refs):
        ins, src_refs = refs[:n_in], refs[n_in:n_in + n]
        outs, dst_refs = refs[n_in + n:n_in + n + n_out], refs[n_in + n + n_out:n_in + 2 * n + n_out]
        scratch = refs[n_in + 2 * n + n_out:]
        step = pl.program_id(0)
        for axis in range(1, len(grid)):
            step = step * grid[axis] + pl.program_id(axis)

        @pl.when(step == 0)
        def _():
            for cp in _exchange_copies(src_refs, dst_refs, *scratch[n_scr:], broadcast):
                cp.start()

        body(*ins, *outs, *scratch[:n_scr])

        @pl.when(step == steps - 1)
        def _():
            for cp in _exchange_copies(src_refs, dst_refs, *scratch[n_scr:], broadcast):
                cp.wait()

    outs = pl.pallas_call(
        hosted, name=name, grid=grid, in_specs=list(in_specs) + [HBM_SPEC] * n, out_specs=list(out_specs) + [HBM_SPEC] * n,
        out_shape=list(out_shape) + _exchange_shapes(srcs, broadcast),
        scratch_shapes=list(scratch_shapes) + _exchange_semaphores(n),
        compiler_params=_cparams(*["arbitrary"] * len(grid)))(*args, *srcs)
    return outs[:n_out], outs[n_out:]


def _regroup_w_in(w):
    wa = w[:, OFF_A:OFF_A + NA]
    seg = lambda off, h, n=HD: w[:, off + h * n: off + (h + 1) * n]
    wb = jnp.concatenate([seg(OFF_B + s * 512, h) for h in range(HGRN_HEADS) for s in range(4)], axis=1)
    parts = []
    for h in range(GDN_QK_HEADS):
        small = jnp.concatenate(
            [w[:, OFF_BETA + 2 * h: OFF_BETA + 2 * h + 2], w[:, OFF_CA + 2 * h: OFF_CA + 2 * h + 2],
             jnp.zeros((w.shape[0], HD - 4), w.dtype)], axis=1)
        parts += [seg(OFF_CQ, h), seg(OFF_CK, h), seg(OFF_CV, h, 2 * HD), seg(OFF_CZ, h, 2 * HD), small]
    wc = jnp.concatenate(parts, axis=1)
    wg = w[:, OFF_G:OFF_G + NG]
    return wa, wb, wc, wg


def _ungroup_dw_in(da, db, dc, dg):
    bq = [jnp.concatenate([db[:, h * 512 + s * HD: h * 512 + (s + 1) * HD] for h in range(HGRN_HEADS)], axis=1)
          for s in range(4)]
    ch = lambda h, lo, hi: dc[:, h * C_HEAD + lo: h * C_HEAD + hi]
    heads = range(GDN_QK_HEADS)
    cq = jnp.concatenate([ch(h, 0, HD) for h in heads], axis=1)
    ck = jnp.concatenate([ch(h, HD, 2 * HD) for h in heads], axis=1)
    cv = jnp.concatenate([ch(h, 2 * HD, 4 * HD) for h in heads], axis=1)
    cz = jnp.concatenate([ch(h, 4 * HD, 6 * HD) for h in heads], axis=1)
    cbeta = jnp.concatenate([ch(h, 6 * HD, 6 * HD + 2) for h in heads], axis=1)
    ca = jnp.concatenate([ch(h, 6 * HD + 2, 6 * HD + 4) for h in heads], axis=1)
    return jnp.concatenate([da] + bq + [cq, ck, cv, cbeta, ca, cz, dg], axis=1)


def _regroup_conv_c(cw):
    parts = []
    for h in range(GDN_QK_HEADS):
        parts += [cw[:, h * HD:(h + 1) * HD], cw[:, 512 + h * HD: 512 + (h + 1) * HD],
                  cw[:, 1024 + 2 * h * HD: 1024 + (2 * h + 2) * HD]]
    return jnp.concatenate(parts, axis=1)


def _ungroup_conv_c(d):
    heads = range(GDN_QK_HEADS)
    q = jnp.concatenate([d[:, h * 512: h * 512 + HD] for h in heads], axis=1)
    k = jnp.concatenate([d[:, h * 512 + HD: h * 512 + 2 * HD] for h in heads], axis=1)
    v = jnp.concatenate([d[:, h * 512 + 2 * HD: h * 512 + 4 * HD] for h in heads], axis=1)
    return jnp.concatenate([q, k, v], axis=1)


def _numel(shape):
    n = 1
    for d in shape:
        n *= d
    return n


def _pack(arrays, rows):
    flat = jnp.concatenate([a.reshape(-1) for a in arrays])
    return jnp.pad(flat, (0, rows * 128 - flat.shape[0])).reshape(rows, 128)


def _unpack(packed, shapes):
    flat = packed.reshape(-1)
    out, off = [], 0
    for s in shapes:
        out.append(flat[off:off + _numel(s)].reshape(s))
        off += _numel(s)
    return out


def _rows_for(shapes):
    return -(-sum(_numel(s) for s in shapes) // 1024) * 8


def kernel(x, norm_w, w_in, b_gate, conv_a, conv_c, a_log, dt_bias, lower_bounds, hgrn_norm_w, gdn_norm_w, w_out_a, w_out_b, w_out_c, w_o, final_norm_w, loss_target, m_norm_w, m_w_in, m_b_gate, m_conv_a, m_conv_c, m_a_log, m_dt_bias, m_lower_bounds, m_hgrn_norm_w, m_gdn_norm_w, m_w_out_a, m_w_out_b, m_w_out_c, m_w_o, m_final_norm_w, v_norm_w, v_w_in, v_b_gate, v_conv_a, v_conv_c, v_a_log, v_dt_bias, v_lower_bounds, v_hgrn_norm_w, v_gdn_norm_w, v_w_out_a, v_w_out_b, v_w_out_c, v_w_o, v_final_norm_w):
    me = 4 * lax.axis_index("x") + 2 * lax.axis_index("y") + lax.axis_index("c")
    xs = x[0]
    target = loss_target[0]
    in_shard = w_in.shape[2]

    big = [w_in, w_out_a, w_out_b, w_out_c, w_o]
    shards_of = lambda l: [w[l].astype(BF16) for w in big]
    conv_shapes = [(DEPTH, 3, CONV_W), (DEPTH, 4, 2048)]
    conv_rows = _rows_for(conv_shapes)
    ca_full = lax.dynamic_update_slice(jnp.zeros(conv_shapes[0], F32), conv_a, (0, 0, me * conv_a.shape[2]))
    cc_full = lax.dynamic_update_slice(jnp.zeros(conv_shapes[1], F32), conv_c, (0, 0, me * conv_c.shape[2]))
    g_in0, conv_parts = _gather_two_level([w_in[0].astype(BF16), _pack([ca_full, cc_full], conv_rows)], "gather_l0")
    conv_a_full, conv_c_full = _unpack(_sum_slots(conv_parts, "sum_conv"), conv_shapes)

    lb_pad = jnp.pad(lower_bounds, ((0, 8 - DEPTH), (0, 0)))
    lbs = _lower_bounds_fwd(lb_pad, "lower_bounds_fwd")

    def input_weights(l, g_in):
        wa, wb, wc, wg = _regroup_w_in(jnp.concatenate([g_in[q] for q in range(N_DEV)], axis=1))
        lanes = lambda vec: jnp.pad(vec.reshape(GDN_QK_HEADS, 1, 2), ((0, 0), (0, 0), (0, HD - 2)))
        cpar = jnp.concatenate([lanes(a_log[l]), lanes(dt_bias[l]), jnp.zeros((GDN_QK_HEADS, 6, HD), F32)], axis=1)
        return dict(
            wa=wa, wb=wb, wc=wc, wg=wg, cpar=cpar,
            nw=norm_w[l:l + 1], bg=b_gate[l:l + 1], cwa=conv_a_full[l], cwc=_regroup_conv_c(conv_c_full[l]),
            lb=lbs[l:l + 1], hnw=hgrn_norm_w[l:l + 1], gnw=gdn_norm_w[l:l + 1])

    def output_weights(g_oa, g_ob, g_oc, g_o):
        return dict(woa=jnp.concatenate([g_oa[q] for q in range(N_DEV)], axis=1),
                    wob=jnp.concatenate([g_ob[q] for q in range(N_DEV)], axis=1), woc=g_oc.reshape(D, D), wo=g_o.reshape(D, D))

    layers = [input_weights(0, g_in0)]

    saved = []
    cur = xs
    for l in range(DEPTH):
        L = layers[l]
        n = f"l{l}_"
        h = _rmsnorm_fwd(cur, L["nw"], n + "rms")
        pa = _matmul(h, L["wa"], "nn", n + "proj_a")
        pb = _matmul(h, L["wb"], "nn", n + "proj_b")
        pc = _matmul(h, L["wc"], "nn", n + "proj_c")
        pg = _matmul(h, L["wg"], "nn", n + "proj_g", out_dtype=BF16)
        ua = _branch_a_fwd(pa, L["cwa"], n + "conv_fwd")
        carry = (shards_of(0)[1:] + shards_of(1)[1:], True) if l == 0 else None
        (ub, sb), gathered = _branch_b_fwd(pb, L["lb"], L["hnw"], n + "hgrn_fwd", exchange=carry)
        if carry is not None:
            L.update(output_weights(*gathered[:4]))
            next_outputs = output_weights(*gathered[4:])
        carry = (shards_of(1)[:1], True) if l == 0 else None
        (uc, sc, xc), gathered = _branch_c_fwd(pc, L["cwc"], L["cpar"], L["gnw"], n + "gdn_fwd", exchange=carry)
        if carry is not None:
            layers.append(dict(input_weights(1, gathered[0]), **next_outputs))
        ya = _matmul(ua, L["woa"], "nn", n + "out_a", out_dtype=BF16)
        yb = _matmul(ub, L["wob"], "nn", n + "out_b", out_dtype=BF16)
        yc = _matmul(uc, L["woc"], "nn", n + "out_c", out_dtype=BF16)
        merged, nxt = _merge_project(pg, L["bg"], ya, yb, yc, L["wo"], cur, n + "merge_out_o")
        saved.append(dict(x=cur, h=h, pa=pa, pb=pb, pc=pc, pg=pg, ua=ua, ub=ub, uc=uc, sb=sb, sc=sc, xc=xc,
                          ya=ya, yb=yb, yc=yc, merged=merged))
        cur = nxt

    loss_part, dx, d_final = _loss_head(cur, final_norm_w.reshape(1, D), target, "loss_head")

    def outgoing(g):
        cols = lambda a, n: jnp.stack([a[:, p * n:(p + 1) * n] for p in range(N_DEV)]).astype(BF16)
        rows = lambda a: a.reshape(N_DEV, a.shape[0] // N_DEV, a.shape[1]).astype(BF16)
        first = [cols(g["w_in"], in_shard)] if "w_in" in g else [None]
        if "w_o" not in g:
            return first
        return first + [cols(g["w_out_a"], 128), cols(g["w_out_b"], 128), rows(g["w_out_c"]), rows(g["w_o"])]

    grads = [None] * DEPTH
    dlbs_rows = [None] * DEPTH
    incoming = [None] * DEPTH
    for l in reversed(range(DEPTH)):
        L, S = layers[l], saved[l]
        n = f"l{l}_"
        dmerged = _matmul(dx, L["wo"], "nt", n + "d_merged", out_dtype=BF16)
        d_wo = _matmul(S["merged"], dx, "tn", n + "dw_o", out_dtype=BF16)
        dpg, dya, dyb, dyc, d_bg = _merge_bwd(dmerged, S["pg"], L["bg"], S["ya"], S["yb"], S["yc"], n + "merge_bwd")
        dua = _matmul(dya, L["woa"], "nt", n + "d_ua", out_dtype=BF16)
        dub = _matmul(dyb, L["wob"], "nt", n + "d_ub", out_dtype=BF16)
        duc = _matmul(dyc, L["woc"], "nt", n + "d_uc", out_dtype=BF16)
        d_woa = _matmul(S["ua"], dya, "tn", n + "dw_out_a", out_dtype=BF16)
        d_wob = _matmul(S["ub"], dyb, "tn", n + "dw_out_b", out_dtype=BF16)
        d_woc = _matmul(S["uc"], dyc, "tn", n + "dw_out_c", out_dtype=BF16)
        dpa, d_cwa = _branch_a_bwd(S["pa"], L["cwa"], dua, n + "conv_bwd")
        out_grads = dict(w_out_a=d_woa, w_out_b=d_wob, w_out_c=d_woc, w_o=d_wo)
        carry = (outgoing(out_grads)[1:], False) if l == 0 else None
        (dpb, d_lb, d_hnw), arrived_out = _branch_b_bwd(S["pb"], S["sb"], L["lb"], L["hnw"], dub, n + "hgrn_bwd", exchange=carry)
        carry = (outgoing(grads[l + 1]), False) if l + 1 < DEPTH else None
        (dpc, d_cwc, d_cpar, d_gnw), arrived = _branch_c_bwd(S["pc"], S["sc"], S["xc"], L["cwc"], L["cpar"], L["gnw"], duc,
                                                             n + "gdn_bwd", exchange=carry)
        if carry is not None:
            incoming[l + 1] = arrived
        d_win = _ungroup_dw_in(*[_matmul(S["h"], dp, "tn", f"{n}dw_{piece}", out_dtype=BF16)
                                 for dp, piece in ((dpa, "a"), (dpb, "b"), (dpc, "c"), (dpg, "g"))])
        carry = (outgoing(dict(w_in=d_win)), False) if l == 0 else None
        dh, arrived_in = _matmul_nt_sum([(dpa, L["wa"]), (dpb, L["wb"]), (dpc, L["wc"]), (dpg, L["wg"])], n + "dh",
                                        exchange=carry)
        (dx, d_nw), _ = _rmsnorm_bwd(dh, S["x"], L["nw"], dx, n + "rms_bwd")
        dlbs_rows[l] = d_lb
        grads[l] = dict(w_in=d_win, w_out_a=d_woa, w_out_b=d_wob, w_out_c=d_woc, w_o=d_wo, norm_w=d_nw[0],
                        b_gate=d_bg[0], conv_a=d_cwa, conv_c=_ungroup_conv_c(d_cwc),
                        a_log=d_cpar[:, 0, 0:2].reshape(-1), dt_bias=d_cpar[:, 1, 0:2].reshape(-1),
                        hgrn_norm_w=d_hnw[0], gdn_norm_w=d_gnw[0])
    grad_x = dx[None]
    d_lower = _lower_bounds_bwd(lb_pad, jnp.pad(jnp.concatenate(dlbs_rows, axis=0), ((0, 8 - DEPTH), (0, 0))),
                                "lower_bounds_bwd")[:DEPTH]

    stack = lambda name: jnp.stack([grads[l][name] for l in range(DEPTH)])
    small_names = ["norm_w", "b_gate", "conv_a", "conv_c", "a_log", "dt_bias", "lower_bounds", "hgrn_norm_w",
                   "gdn_norm_w", "final_norm_w", "loss"]
    small_vals = {k: stack(k) for k in ("norm_w", "b_gate", "conv_a", "conv_c", "a_log", "dt_bias", "hgrn_norm_w", "gdn_norm_w")}
    small_vals.update(lower_bounds=d_lower, final_norm_w=d_final[0], loss=loss_part.reshape(1))
    small_shapes = [small_vals[k].shape for k in small_names]
    small_rows = _rows_for(small_shapes)
    small_pack = _pack([small_vals[k] for k in small_names], small_rows)

    incoming[0] = list(arrived_in) + list(arrived_out)
    big_out = {}
    for j, (name, w, m, v) in enumerate((("w_in", w_in, m_w_in, v_w_in), ("w_out_a", w_out_a, m_w_out_a, v_w_out_a),
                                         ("w_out_b", w_out_b, m_w_out_b, v_w_out_b), ("w_out_c", w_out_c, m_w_out_c, v_w_out_c),
                                         ("w_o", w_o, m_w_o, v_w_o))):
        parts = [incoming[l][j] for l in range(DEPTH)]
        r2 = lambda a: a.reshape(DEPTH * parts[0].shape[1], parts[0].shape[2])
        carry = ([small_pack], True) if name == "w_in" else None
        outs, arrived = _sum_adamw(parts, r2(w), r2(m), r2(v), "adamw_" + name, exchange=carry)
        if carry is not None:
            small_parts, = arrived
        big_out[name] = [o.reshape(w.shape) for o in outs]

    total = dict(zip(small_names, _unpack(_sum_slots(small_parts, "sum_small"), small_shapes)))
    loss = total["loss"][0]
    g_conv_a = lax.dynamic_slice(total["conv_a"], (0, 0, me * conv_a.shape[2]), conv_a.shape)
    g_conv_c = lax.dynamic_slice(total["conv_c"], (0, 0, me * conv_c.shape[2]), conv_c.shape)

    small_w = dict(norm_w=(norm_w, m_norm_w, v_norm_w), b_gate=(b_gate, m_b_gate, v_b_gate),
                   conv_a=(conv_a, m_conv_a, v_conv_a), conv_c=(conv_c, m_conv_c, v_conv_c),
                   a_log=(a_log, m_a_log, v_a_log), dt_bias=(dt_bias, m_dt_bias, v_dt_bias),
                   lower_bounds=(lower_bounds, m_lower_bounds, v_lower_bounds),
                   hgrn_norm_w=(hgrn_norm_w, m_hgrn_norm_w, v_hgrn_norm_w), gdn_norm_w=(gdn_norm_w, m_gdn_norm_w, v_gdn_norm_w),
                   final_norm_w=(final_norm_w, m_final_norm_w, v_final_norm_w))
    small_g = dict(total, conv_a=g_conv_a, conv_c=g_conv_c)
    upd_names = small_names[:-1]
    upd_shapes = [small_w[k][0].shape for k in upd_names]
    upd_rows = _rows_for(upd_shapes)
    pk = lambda j: _pack([small_w[k][j] for k in upd_names], upd_rows)
    s_delta, s_m, s_v = _adamw(_pack([small_g[k] for k in upd_names], upd_rows), pk(0), pk(1), pk(2), "adamw_small")
    small_out = {k: [small_g[k], d, mm, vv] for k, d, mm, vv in
                 zip(upd_names, _unpack(s_delta, upd_shapes), _unpack(s_m, upd_shapes), _unpack(s_v, upd_shapes))}

    order = ["norm_w", "w_in", "b_gate", "conv_a", "conv_c", "a_log", "dt_bias", "lower_bounds", "hgrn_norm_w",
             "gdn_norm_w", "w_out_a", "w_out_b", "w_out_c", "w_o", "final_norm_w"]
    res = {**small_out, **big_out}
    outs = [loss, grad_x]
    for j in range(4):
        outs += [res[k][j] for k in order]
    return tuple(outs)
```

```python
import functools

import jax
import jax.numpy as jnp
from jax import lax
from jax.experimental import pallas as pl
from jax.experimental.pallas import tpu as pltpu

F32 = jnp.float32
BF16 = jnp.bfloat16
MESH = pl.DeviceIdType.MESH

N_DEV = 8
D = 1024
DEPTH = 2
CHUNK = 64
HGRN_BLOCK_CHUNKS = 16
GDN_BLOCK_CHUNKS = 8
GDN_FORWARD_CHUNKS = 16
GROUP = 128
NORM_EPS = 1e-6
L2_EPS = 1e-6
MIN_F = 1e-30
HD = 128
HGRN_HEADS = 4
GDN_QK_HEADS = 4
CONV_W = 512
IN_COLS = 10256
OFF_A, OFF_B, OFF_CQ, OFF_CK, OFF_CV, OFF_BETA, OFF_CA, OFF_CZ, OFF_G = (
    0, 2048, 4096, 4608, 5120, 6144, 6152, 6160, 7184)
NA, NB, NC_COLS, NG = 2048, 2048, 3584, 3072
C_HEAD = 896

ADAM_LR, ADAM_B1, ADAM_B2, ADAM_EPS, ADAM_WD, ADAM_STEP = 0.001, 0.9, 0.999, 1e-08, 0.01, 10

VMEM_LIMIT = 56 * 1024 * 1024
MM_TILE = 1024


def _cparams(*sem):
    return pltpu.CompilerParams(dimension_semantics=sem, vmem_limit_bytes=VMEM_LIMIT)


def _tile(dim, cap):
    if dim <= cap:
        return dim
    t = (cap // 128) * 128
    while dim % t:
        t -= 128
    return t


def _sigmoid(x):
    return 1.0 / (1.0 + jnp.exp(-x))


def _silu(x):
    return x * _sigmoid(x)


def _softplus(x):
    return jnp.maximum(x, 0.0) + jnp.log(1.0 + jnp.exp(-jnp.abs(x)))


def _dot(a, b, dims, precision=None):
    if precision is None:
        a, b = a.astype(BF16), b.astype(BF16)
    return lax.dot_general(a, b, (dims, ((), ())), precision=precision, preferred_element_type=F32)


def _nn(a, b, precision=None):
    return _dot(a, b, ((1,), (0,)), precision)


def _nt(a, b, precision=None):
    return _dot(a, b, ((1,), (1,)), precision)


def _tn(a, b, precision=None):
    return _dot(a, b, ((0,), (0,)), precision)


def _sum_rows_split(mat01, x):
    m = mat01.astype(BF16)
    hi = x.astype(BF16)
    low = (x - hi.astype(F32)).astype(BF16)
    return _nn(m, hi) + _nn(m, low)


@functools.partial(jax.custom_vjp, nondiff_argnums=(1,))
def _shift_rows(x, d):
    return x if d == 0 else pltpu.roll(x, d, 0)


def _shift_rows_fwd(x, d):
    return _shift_rows(x, d), None


def _shift_rows_bwd(d, _, ct):
    return ((ct if d == 0 else pltpu.roll(ct, ct.shape[0] - d, 0)),)


_shift_rows.defvjp(_shift_rows_fwd, _shift_rows_bwd)


def _iota2(shape):
    return lax.broadcasted_iota(jnp.int32, shape, 0), lax.broadcasted_iota(jnp.int32, shape, 1)


def _lane_pick(x, i):
    lane = lax.broadcasted_iota(jnp.int32, x.shape, 1)
    return jnp.sum(jnp.where(lane == i, x, 0.0), axis=1, keepdims=True)


def _hgrn_block(qr, fr, ir, zr, st0, lb, nw):
    rows = qr.shape[0]
    r, c = _iota2((CHUNK, CHUNK))
    halves = [1 << j for j in range(CHUNK.bit_length() - 1)]
    mats = [c <= r, c > r]
    pairs = []
    for hb in halves:
        same = (r // hb) == (c // hb)
        if hb > 1:
            mats += [(c <= r) & same, (c > r) & same]
        pairs.append(((r // (2 * hb)) == (c // (2 * hb))) & ((r // hb) == (c // hb) + 1))
    stack = jnp.concatenate([m.astype(F32) for m in mats], axis=0)

    q = _silu(qr) * (HD ** -0.5)
    fg = lb + (1.0 - lb) * _sigmoid(fr)
    logf = jnp.log(jnp.maximum(fg, MIN_F))
    kk = 1.0 - fg
    v = ir

    chunks = [slice(s, s + CHUNK) for s in range(0, rows, CHUNK)]
    cums = [_sum_rows_split(stack, logf[sl]) for sl in chunks]
    part = lambda i: jnp.concatenate([cs[i * CHUNK:(i + 1) * CHUNK] for cs in cums], axis=0)
    qg = q * jnp.exp(part(0))
    ks = kk * jnp.exp(part(1))
    q_lv = [q * jnp.exp(logf)] + [q * jnp.exp(part(2 * j)) for j in range(1, len(halves))]
    k_lv = [kk] + [kk * jnp.exp(part(2 * j + 1)) for j in range(1, len(halves))]
    st = st0
    outs = []
    for sl in chunks:
        scores = jnp.where(pairs[0], _nt(q_lv[0][sl], k_lv[0][sl]), 0.0)
        for j in range(1, len(halves)):
            scores += jnp.where(pairs[j], _nt(q_lv[j][sl], k_lv[j][sl]), 0.0)
        outs.append(_nn(scores, v[sl]) + _nt(qg[sl], st))
        st = st * jnp.exp(jnp.sum(logf[sl], axis=0, keepdims=True)) + _tn(v[sl], ks[sl])
    o = jnp.concatenate(outs, axis=0) + jnp.sum(q * kk, axis=1, keepdims=True) * v
    y = o * lax.rsqrt(jnp.mean(o * o, axis=1, keepdims=True) + NORM_EPS) * nw * _silu(zr)
    return y, st


def _unit_lower_inverses(ms):
    r, c = _iota2(ms[0].shape)
    xs = [jnp.where(r == c, 1.0, 0.0) - jnp.where((r // 2) == (c // 2), m, 0.0) for m in ms]
    b = 2
    while b < CHUNK:
        pick = ((r // (2 * b)) == (c // (2 * b))) & ((r // b) != (c // b))
        ts = [_nn(x, jnp.where(pick, m, 0.0)) for x, m in zip(xs, ms)]
        xs = [x - _nn(t, x) for x, t in zip(xs, ts)]
        b *= 2
    return tuple(x.astype(BF16) for x in xs)


@jax.custom_vjp
def _known_inverses(ms, xs):
    return xs


def _known_inverses_fwd(ms, xs):
    return xs, xs


def _known_inverses_bwd(xs, cts):
    r, c = _iota2(xs[0].shape)
    keep = (c < r) & ((r // CHUNK) == (c // CHUNK))
    ts = [_tn(x, ct) for x, ct in zip(xs, cts)]
    return (tuple(jnp.where(keep, -_nt(t, x), 0.0) for t, x in zip(ts, xs)), tuple(jnp.zeros_like(x) for x in xs))


_known_inverses.defvjp(_known_inverses_fwd, _known_inverses_bwd)


def _chunk_cumsum(x):
    row = lax.broadcasted_iota(jnp.int32, x.shape, 0) % CHUNK
    d = 1
    while d < CHUNK:
        x = x + jnp.where(row >= d, _shift_rows(x, d), 0.0)
        d *= 2
    return x


def _gdn_block(x_ext, z, ba, s0a, s0b, w0, w1, w2, w3, alog, dtb, nw, known=None):
    rows = z.shape[0]
    conv = (w0 * _shift_rows(x_ext, 3) + w1 * _shift_rows(x_ext, 2) + w2 * _shift_rows(x_ext, 1) + w3 * x_ext)
    cc = _silu(conv[8:])
    qc, kc = cc[:, 0:HD], cc[:, HD:2 * HD]
    q = qc * lax.rsqrt(jnp.sum(qc * qc, axis=1, keepdims=True) + L2_EPS) * (HD ** -0.5)
    k = kc * lax.rsqrt(jnp.sum(kc * kc, axis=1, keepdims=True) + L2_EPS)

    r, c = _iota2((GROUP, GROUP))
    same = (r // CHUNK) == (c // CHUNK)
    causal, strict, eye = same & (c <= r), same & (c < r), r == c
    heads = (0, 1)
    groups = [slice(lo, lo + GROUP) for lo in range(0, rows, GROUP)]
    chunks = [slice(lo, lo + CHUNK) for lo in range(0, rows, CHUNK)]

    v, loga, g_w, kb, kg, qg = [], [], [], [], [], []
    for i in heads:
        v.append(cc[:, (2 + i) * HD:(3 + i) * HD])
        beta = _sigmoid(_lane_pick(ba, i))
        a_neg = -jnp.exp(_lane_pick(alog, i))
        loga.append(a_neg * _softplus(_lane_pick(ba, 2 + i) + _lane_pick(dtb, i)))
        g_w.append(_chunk_cumsum(jnp.broadcast_to(loga[i], (rows, HD))))
        kb.append(k * beta)
        kg.append(k * jnp.exp(g_w[i]))
        qg.append(q * jnp.exp(g_w[i]))

    systems = [(i, gs) for gs in groups for i in heads]
    dec_c, ms = [], []
    for i, gs in systems:
        g_sq = g_w[i][gs]
        g_row = jnp.sum(jnp.where(eye, g_sq, 0.0), axis=0, keepdims=True)
        diff = g_sq - g_row
        dec_c.append(jnp.where(causal, jnp.exp(jnp.where(causal, diff, 0.0)), 0.0))
        ms.append(jnp.where(strict, _nt(k[gs], kb[i][gs]) * dec_c[-1], 0.0))
    xs = _unit_lower_inverses(tuple(ms)) if known is None else _known_inverses(tuple(ms), known)
    u = [[None] * len(groups) for _ in heads]
    w = [[None] * len(groups) for _ in heads]
    qk = [[None] * len(groups) for _ in heads]
    for n, (i, gs) in enumerate(systems):
        j = n // len(heads)
        u[i][j] = _nn(xs[n], v[i][gs])
        w[i][j] = _nn(xs[n], kg[i][gs])
        qk[i][j] = _nt(q[gs], kb[i][gs]) * dec_c[n]
    u = [jnp.concatenate(p, axis=0) for p in u]
    w = [jnp.concatenate(p, axis=0) for p in w]

    decay, p_mat, q_mat = {}, {}, {}
    for n, sl in enumerate(chunks):
        for i in heads:
            g_last = jnp.sum(loga[i][sl], axis=0, keepdims=True)
            kd = kb[i][sl] * jnp.exp(g_last - g_w[i][sl])
            decay[n, i] = jnp.exp(g_last)
            p_mat[n, i] = -_tn(kd, w[i][sl])
            q_mat[n, i] = _tn(kd, u[i][sl])
    s = [s0a, s0b]
    s_at = {}
    for n in range(len(chunks)):
        for i in heads:
            s_at[n, i] = s[i]
            s[i] = s[i] * decay[n, i] + _nn(p_mat[n, i], s[i]) + q_mat[n, i]

    ys = []
    for i in heads:
        e = jnp.concatenate([u[i][sl] - _nn(w[i][sl], s_at[n, i]) for n, sl in enumerate(chunks)], axis=0)
        o_state = jnp.concatenate([_nn(qg[i][sl], s_at[n, i]) for n, sl in enumerate(chunks)], axis=0)
        o = o_state + jnp.concatenate([_nn(qk[i][j], e[gs]) for j, gs in enumerate(groups)], axis=0)
        zi = z[:, i * HD:(i + 1) * HD]
        ys.append(o * lax.rsqrt(jnp.mean(o * o, axis=1, keepdims=True) + NORM_EPS) * nw * _silu(zi))
    return (jnp.concatenate(ys, axis=1), s[0], s[1]), (xs, s_at)


def _add_to_tail(x, tail):
    return x + jnp.concatenate([jnp.zeros((x.shape[0] - 8, x.shape[1]), x.dtype), tail], axis=0)


def _conv_a_block(ab, ac_ext, ax_ext, az, w0, w1, w2):
    u = ac_ext * ax_ext
    conv = (w0 * _shift_rows(u, 2) + w1 * _shift_rows(u, 1) + w2 * u)[8:]
    return ab * conv * _silu(az)


def _matmul(a, b, mode, name, residual=None, out_dtype=F32):
    if mode == "nn":
        (m, k), n = a.shape, b.shape[1]
    elif mode == "nt":
        (m, k), n = a.shape, b.shape[0]
    else:
        (k, m), n = a.shape, b.shape[1]
    tm, tn, tk = _tile(m, MM_TILE), _tile(n, MM_TILE), _tile(k, MM_TILE)
    if mode == "tn":
        tk = _tile(k, 2 * MM_TILE)
    elif k == tk:
        tm = _tile(m, 2 * MM_TILE)
    nk = k // tk
    dims = {"nn": ((1,), (0,)), "nt": ((1,), (1,)), "tn": ((0,), (0,))}[mode]
    a_spec = pl.BlockSpec((tk, tm), lambda i, j, s: (s, i)) if mode == "tn" else pl.BlockSpec((tm, tk), lambda i, j, s: (i, s))
    b_spec = pl.BlockSpec((tn, tk), lambda i, j, s: (j, s)) if mode == "nt" else pl.BlockSpec((tk, tn), lambda i, j, s: (s, j))
    o_spec = pl.BlockSpec((tm, tn), lambda i, j, s: (i, j))
    has_res = residual is not None

    def finish(out, r_ref, o_ref):
        if has_res:
            out = out + r_ref[...]
        o_ref[...] = out.astype(out_dtype)

    def body_one_pass(*refs):
        finish(_dot(refs[0][...], refs[1][...], dims), refs[2] if has_res else None, refs[-1])

    def body_reduce(*refs):
        a_ref, b_ref = refs[0], refs[1]
        r_ref = refs[2] if has_res else None
        o_ref, acc_ref = refs[-2], refs[-1]
        s = pl.program_id(2)

        @pl.when(s == 0)
        def _():
            acc_ref[...] = jnp.zeros_like(acc_ref)

        acc_ref[...] += _dot(a_ref[...], b_ref[...], dims)

        @pl.when(s == nk - 1)
        def _():
            finish(acc_ref[...], r_ref, o_ref)

    args, specs = [a, b], [a_spec, b_spec]
    if has_res:
        args.append(residual)
        specs.append(o_spec)
    return pl.pallas_call(
        body_one_pass if nk == 1 else body_reduce, name=name, grid=(m // tm, n // tn, nk), in_specs=specs, out_specs=o_spec,
        out_shape=jax.ShapeDtypeStruct((m, n), out_dtype),
        scratch_shapes=[] if nk == 1 else [pltpu.VMEM((tm, tn), F32)],
        compiler_params=_cparams("parallel", "parallel", "arbitrary"))(*args)


def _matmul_nt_sum(pairs, name, exchange=None):
    m, n = pairs[0][0].shape[0], pairs[0][1].shape[0]
    tm, tn = _tile(m, MM_TILE), _tile(n, MM_TILE)
    tks = [_tile(a.shape[1], MM_TILE) for a, _ in pairs]
    nks = [a.shape[1] // tk for (a, _), tk in zip(pairs, tks)]
    offs = [sum(nks[:i]) for i in range(len(pairs))]
    total = sum(nks)

    def body(*refs):
        o_ref, acc_ref = refs[-2], refs[-1]
        s = pl.program_id(2)

        @pl.when(s == 0)
        def _():
            acc_ref[...] = jnp.zeros_like(acc_ref)

        for i, (off, nk) in enumerate(zip(offs, nks)):
            @pl.when((s >= off) & (s < off + nk))
            def _(i=i):
                acc_ref[...] += _dot(refs[2 * i][...], refs[2 * i + 1][...], ((1,), (1,)))

        @pl.when(s == total - 1)
        def _():
            o_ref[...] = acc_ref[...]

    args, specs = [], []
    for (a, b), tk, off, nk in zip(pairs, tks, offs, nks):
        k_of = lambda s, off=off, nk=nk: jnp.clip(s - off, 0, nk - 1)
        args += [a, b]
        specs += [pl.BlockSpec((tm, tk), lambda i, j, s, k_of=k_of: (i, k_of(s))),
                  pl.BlockSpec((tn, tk), lambda i, j, s, k_of=k_of: (j, k_of(s)))]
    (out,), exchanged = _call_with_exchange(
        body, name=name, grid=(m // tm, n // tn, total), in_specs=specs,
        out_specs=[pl.BlockSpec((tm, tn), lambda i, j, s: (i, j))], out_shape=[jax.ShapeDtypeStruct((m, n), F32)],
        scratch_shapes=[pltpu.VMEM((tm, tn), F32)], args=args, exchange=exchange)
    return out, exchanged


def _rmsnorm_fwd(x, w, name):
    t = x.shape[0]
    blk = _tile(t, 1024)

    def body(x_ref, w_ref, h_ref):
        xv = x_ref[...]
        h_ref[...] = (xv * lax.rsqrt(jnp.mean(xv * xv, axis=1, keepdims=True) + NORM_EPS) * w_ref[...]).astype(BF16)

    return pl.pallas_call(
        body, name=name, grid=(t // blk,),
        in_specs=[pl.BlockSpec((blk, D), lambda i: (i, 0)), pl.BlockSpec((1, D), lambda i: (0, 0))],
        out_specs=pl.BlockSpec((blk, D), lambda i: (i, 0)), out_shape=jax.ShapeDtypeStruct((t, D), BF16),
        compiler_params=_cparams("parallel"))(x, w)


def _rmsnorm_bwd(dh, x, w, dxo, name, exchange=None):
    t = x.shape[0]
    blk = _tile(t, 512)

    def body(dh_ref, x_ref, w_ref, dxo_ref, dx_ref, dw_ref):
        @pl.when(pl.program_id(0) == 0)
        def _():
            dw_ref[...] = jnp.zeros_like(dw_ref)

        xv, dhv = x_ref[...], dh_ref[...]
        rs = lax.rsqrt(jnp.mean(xv * xv, axis=1, keepdims=True) + NORM_EPS)
        xh = xv * rs
        dw_ref[...] += jnp.sum(dhv * xh, axis=0, keepdims=True)
        dxh = dhv * w_ref[...]
        dx_ref[...] = rs * (dxh - xh * jnp.mean(dxh * xh, axis=1, keepdims=True)) + dxo_ref[...]

    row = pl.BlockSpec((blk, D), lambda i: (i, 0))
    vec = pl.BlockSpec((1, D), lambda i: (0, 0))
    return _call_with_exchange(
        body, name=name, grid=(t // blk,), in_specs=[row, row, vec, row], out_specs=[row, vec],
        out_shape=[jax.ShapeDtypeStruct((t, D), F32), jax.ShapeDtypeStruct((1, D), F32)],
        scratch_shapes=[], args=(dh, x, w, dxo), exchange=exchange)


def _loss_head(x, w, target, name):
    t = x.shape[0]
    blk = _tile(t, 512)

    def body(x_ref, w_ref, t_ref, loss_ref, dx_ref, dw_ref):
        @pl.when(pl.program_id(0) == 0)
        def _():
            dw_ref[...] = jnp.zeros_like(dw_ref)
            loss_ref[...] = jnp.zeros_like(loss_ref)

        xv = x_ref[...]
        rs = lax.rsqrt(jnp.mean(xv * xv, axis=1, keepdims=True) + NORM_EPS)
        xh = xv * rs
        err = xh * w_ref[...] - t_ref[...]
        loss_ref[...] += 0.5 * jnp.sum(jnp.mean(err * err, axis=1, keepdims=True), axis=0, keepdims=True)
        dy = err * (1.0 / D)
        dw_ref[...] += jnp.sum(dy * xh, axis=0, keepdims=True)
        dxh = dy * w_ref[...]
        dx_ref[...] = rs * (dxh - xh * jnp.mean(dxh * xh, axis=1, keepdims=True))

    row = pl.BlockSpec((blk, D), lambda i: (i, 0))
    vec = pl.BlockSpec((1, D), lambda i: (0, 0))
    return pl.pallas_call(
        body, name=name, grid=(t // blk,), in_specs=[row, vec, row],
        out_specs=[pl.BlockSpec((1, 1), lambda i: (0, 0)), row, vec],
        out_shape=[jax.ShapeDtypeStruct((1, 1), F32), jax.ShapeDtypeStruct((t, D), F32), jax.ShapeDtypeStruct((1, D), F32)],
        compiler_params=_cparams("arbitrary"))(x, w, target)


def _lbs_of(lb):
    r = lax.broadcasted_iota(jnp.int32, lb.shape, 0)
    real = r < DEPTH
    mx = lax.stop_gradient(jnp.max(jnp.where(real, lb, -jnp.inf), axis=0, keepdims=True))
    e = jnp.where(real, jnp.exp(jnp.where(real, lb - mx, 0.0)), 0.0)
    p = e / jnp.sum(e, axis=0, keepdims=True)
    out = jnp.zeros_like(lb)
    run = jnp.zeros_like(mx)
    for l in range(1, DEPTH):
        run = run + jnp.sum(jnp.where(r == l, p, 0.0), axis=0, keepdims=True)
        out = out + jnp.where(r == l, run, 0.0)
    return out


def _lower_bounds_fwd(lbp, name):
    def body(lb_ref, o_ref):
        o_ref[...] = _lbs_of(lb_ref[...])

    return pl.pallas_call(body, name=name, out_shape=jax.ShapeDtypeStruct(lbp.shape, F32))(lbp)


def _lower_bounds_bwd(lbp, dlbs, name):
    def body(lb_ref, d_ref, o_ref):
        _, vjp = jax.vjp(_lbs_of, lb_ref[...])
        o_ref[...] = vjp(d_ref[...])[0]

    return pl.pallas_call(body, name=name, out_shape=jax.ShapeDtypeStruct(lbp.shape, F32))(lbp, dlbs)


def _branch_a_fwd(pa, cw, name):
    t = pa.shape[0]
    blk = _tile(t, 512)
    W = CONV_W

    def body(p_ref, w_ref, y_ref, hc_ref, hx_ref):
        @pl.when(pl.program_id(0) == 0)
        def _():
            hc_ref[...] = jnp.zeros_like(hc_ref)
            hx_ref[...] = jnp.zeros_like(hx_ref)

        ac, ax = p_ref[:, W:2 * W], p_ref[:, 2 * W:3 * W]
        y_ref[...] = _conv_a_block(
            p_ref[:, 0:W], jnp.concatenate([hc_ref[...], ac], axis=0), jnp.concatenate([hx_ref[...], ax], axis=0),
            p_ref[:, 3 * W:4 * W], w_ref[0:1, :], w_ref[1:2, :], w_ref[2:3, :]).astype(BF16)
        hc_ref[...] = p_ref[blk - 8:blk, W:2 * W]
        hx_ref[...] = p_ref[blk - 8:blk, 2 * W:3 * W]

    return pl.pallas_call(
        body, name=name, grid=(t // blk,),
        in_specs=[pl.BlockSpec((blk, NA), lambda i: (i, 0)), pl.BlockSpec((3, W), lambda i: (0, 0))],
        out_specs=pl.BlockSpec((blk, W), lambda i: (i, 0)), out_shape=jax.ShapeDtypeStruct((t, W), BF16),
        scratch_shapes=[pltpu.VMEM((8, W), F32), pltpu.VMEM((8, W), F32)],
        compiler_params=_cparams("arbitrary"))(pa, cw)


def _branch_a_bwd(pa, cw, dy, name):
    t = pa.shape[0]
    blk = _tile(t, 512)
    nt_ = t // blk
    W = CONV_W
    hb = blk // 8

    def body(p_ref, halo_ref, w_ref, dy_ref, dp_ref, dw_ref, chc_ref, chx_ref):
        i = pl.program_id(0)

        @pl.when(i == 0)
        def _():
            chc_ref[...] = jnp.zeros_like(chc_ref)
            chx_ref[...] = jnp.zeros_like(chx_ref)
            dw_ref[...] = jnp.zeros_like(dw_ref)

        keep = 1.0 - (i == nt_ - 1).astype(F32)
        hc = halo_ref[:, W:2 * W] * keep
        hx = halo_ref[:, 2 * W:3 * W] * keep
        ac_ext = jnp.concatenate([hc, p_ref[:, W:2 * W]], axis=0)
        ax_ext = jnp.concatenate([hx, p_ref[:, 2 * W:3 * W]], axis=0)
        _, vjp = jax.vjp(_conv_a_block, p_ref[:, 0:W], ac_ext, ax_ext, p_ref[:, 3 * W:4 * W],
                         w_ref[0:1, :], w_ref[1:2, :], w_ref[2:3, :])
        dab, dac, dax, daz, dw0, dw1, dw2 = vjp(dy_ref[...].astype(F32))
        dp_ref[:, 0:W] = dab.astype(BF16)
        dp_ref[:, W:2 * W] = _add_to_tail(dac[8:], chc_ref[...]).astype(BF16)
        dp_ref[:, 2 * W:3 * W] = _add_to_tail(dax[8:], chx_ref[...]).astype(BF16)
        dp_ref[:, 3 * W:4 * W] = daz.astype(BF16)
        chc_ref[...] = dac[:8] * keep
        chx_ref[...] = dax[:8] * keep
        dw_ref[0:1, :] += dw0
        dw_ref[1:2, :] += dw1
        dw_ref[2:3, :] += dw2

    rev = lambda i: (nt_ - 1 - i, 0)
    return pl.pallas_call(
        body, name=name, grid=(nt_,),
        in_specs=[pl.BlockSpec((blk, NA), rev),
                  pl.BlockSpec((8, NA), lambda i: (jnp.maximum((nt_ - 1 - i) * hb - 1, 0), 0)),
                  pl.BlockSpec((3, W), lambda i: (0, 0)),
                  pl.BlockSpec((blk, W), rev)],
        out_specs=[pl.BlockSpec((blk, NA), rev), pl.BlockSpec((3, W), lambda i: (0, 0))],
        out_shape=[jax.ShapeDtypeStruct((t, NA), BF16), jax.ShapeDtypeStruct((3, W), F32)],
        scratch_shapes=[pltpu.VMEM((8, W), F32), pltpu.VMEM((8, W), F32)],
        compiler_params=_cparams("arbitrary"))(pa, pa, cw, dy)


def _block_rows(t, chunks):
    return min(t, chunks * CHUNK)


def _branch_b_fwd(pb, lbs_row, nw, name, exchange=None):
    t = pb.shape[0]
    rows = _block_rows(t, HGRN_BLOCK_CHUNKS)
    nch = t // rows

    def body(p_ref, lb_ref, nw_ref, y_ref, s_ref, st_ref):
        @pl.when(pl.program_id(1) == 0)
        def _():
            st_ref[...] = jnp.zeros_like(st_ref)

        s_ref[0, 0] = st_ref[...]
        y, st1 = _hgrn_block(p_ref[:, 0:HD], p_ref[:, HD:2 * HD], p_ref[:, 2 * HD:3 * HD], p_ref[:, 3 * HD:4 * HD],
                             st_ref[...], lb_ref[...], nw_ref[...])
        y_ref[...] = y.astype(BF16)
        st_ref[...] = st1

    return _call_with_exchange(
        body, name=name, grid=(HGRN_HEADS, nch),
        in_specs=[pl.BlockSpec((rows, 4 * HD), lambda h, i: (i, h)),
                  pl.BlockSpec((1, HD), lambda h, i: (0, h)),
                  pl.BlockSpec((1, HD), lambda h, i: (0, 0))],
        out_specs=[pl.BlockSpec((rows, HD), lambda h, i: (i, h)),
                   pl.BlockSpec((1, 1, HD, HD), lambda h, i: (h, i, 0, 0))],
        out_shape=[jax.ShapeDtypeStruct((t, HGRN_HEADS * HD), BF16),
                   jax.ShapeDtypeStruct((HGRN_HEADS, nch, HD, HD), F32)],
        scratch_shapes=[pltpu.VMEM((HD, HD), F32)],
        args=(pb, lbs_row, nw), exchange=exchange)


def _branch_b_bwd(pb, states, lbs_row, nw, dy, name, exchange=None):
    t = pb.shape[0]
    rows = _block_rows(t, HGRN_BLOCK_CHUNKS)
    nch = t // rows

    def body(p_ref, s_ref, lb_ref, nw_ref, dy_ref, dp_ref, dlb_ref, dnw_ref, ds_ref):
        h, i = pl.program_id(0), pl.program_id(1)

        @pl.when(i == 0)
        def _():
            ds_ref[...] = jnp.zeros_like(ds_ref)
            dlb_ref[...] = jnp.zeros_like(dlb_ref)

        @pl.when((i == 0) & (h == 0))
        def _():
            dnw_ref[...] = jnp.zeros_like(dnw_ref)

        _, vjp = jax.vjp(_hgrn_block, p_ref[:, 0:HD], p_ref[:, HD:2 * HD], p_ref[:, 2 * HD:3 * HD],
                         p_ref[:, 3 * HD:4 * HD], s_ref[0, 0], lb_ref[...], nw_ref[...])
        dq, df, di, dz, ds0, dlb, dnw = vjp((dy_ref[...].astype(F32), ds_ref[...]))
        dp_ref[:, 0:HD] = dq.astype(BF16)
        dp_ref[:, HD:2 * HD] = df.astype(BF16)
        dp_ref[:, 2 * HD:3 * HD] = di.astype(BF16)
        dp_ref[:, 3 * HD:4 * HD] = dz.astype(BF16)
        ds_ref[...] = ds0
        dlb_ref[...] += dlb
        dnw_ref[...] += dnw

    rev = lambda h, i: (nch - 1 - i, h)
    return _call_with_exchange(
        body, name=name, grid=(HGRN_HEADS, nch),
        in_specs=[pl.BlockSpec((rows, 4 * HD), rev),
                  pl.BlockSpec((1, 1, HD, HD), lambda h, i: (h, nch - 1 - i, 0, 0)),
                  pl.BlockSpec((1, HD), lambda h, i: (0, h)),
                  pl.BlockSpec((1, HD), lambda h, i: (0, 0)),
                  pl.BlockSpec((rows, HD), rev)],
        out_specs=[pl.BlockSpec((rows, 4 * HD), rev),
                   pl.BlockSpec((1, HD), lambda h, i: (0, h)),
                   pl.BlockSpec((1, HD), lambda h, i: (0, 0))],
        out_shape=[jax.ShapeDtypeStruct((t, NB), BF16), jax.ShapeDtypeStruct((1, HGRN_HEADS * HD), F32),
                   jax.ShapeDtypeStruct((1, HD), F32)],
        scratch_shapes=[pltpu.VMEM((HD, HD), F32)],
        args=(pb, states, lbs_row, nw, dy), exchange=exchange)


def _branch_c_fwd(pc, cw, cpar, nw, name, exchange=None):
    t = pc.shape[0]
    back_rows = _block_rows(t, GDN_BLOCK_CHUNKS)
    rows = _block_rows(t, GDN_FORWARD_CHUNKS)
    nch, per_step = t // rows, rows // back_rows
    XW = 4 * HD
    nsys = 2 * back_rows // GROUP

    def body(p_ref, w_ref, cp_ref, nw_ref, y_ref, s_ref, x_ref, sa_ref, sb_ref, halo_ref):
        @pl.when(pl.program_id(1) == 0)
        def _():
            sa_ref[...] = jnp.zeros_like(sa_ref)
            sb_ref[...] = jnp.zeros_like(sb_ref)
            halo_ref[...] = jnp.zeros_like(halo_ref)

        x_ext = jnp.concatenate([halo_ref[...], p_ref[:, 0:XW]], axis=0)
        (y, s1a, s1b), (xs, s_at) = _gdn_block(
            x_ext, p_ref[:, XW:XW + 2 * HD], p_ref[:, XW + 2 * HD:XW + 3 * HD],
            sa_ref[...], sb_ref[...], w_ref[0:1, :], w_ref[1:2, :], w_ref[2:3, :], w_ref[3:4, :],
            cp_ref[0, 0:1, :], cp_ref[0, 1:2, :], nw_ref[...])
        for b in range(per_step):
            for i in range(2):
                s_ref[0, b, i] = s_at[b * (back_rows // CHUNK), i]
            for n in range(nsys):
                x_ref[0, b, n] = xs[b * nsys + n]
        y_ref[...] = y.astype(BF16)
        sa_ref[...] = s1a
        sb_ref[...] = s1b
        halo_ref[...] = p_ref[rows - 8:rows, 0:XW]

    return _call_with_exchange(
        body, name=name, grid=(GDN_QK_HEADS, nch),
        in_specs=[pl.BlockSpec((rows, C_HEAD), lambda h, i: (i, h)),
                  pl.BlockSpec((4, XW), lambda h, i: (0, h)),
                  pl.BlockSpec((1, 8, HD), lambda h, i: (h, 0, 0)),
                  pl.BlockSpec((1, HD), lambda h, i: (0, 0))],
        out_specs=[pl.BlockSpec((rows, 2 * HD), lambda h, i: (i, h)),
                   pl.BlockSpec((1, per_step, 2, HD, HD), lambda h, i: (h, i, 0, 0, 0)),
                   pl.BlockSpec((1, per_step, nsys, GROUP, GROUP), lambda h, i: (h, i, 0, 0, 0))],
        out_shape=[jax.ShapeDtypeStruct((t, 2 * GDN_QK_HEADS * HD), BF16),
                   jax.ShapeDtypeStruct((GDN_QK_HEADS, nch * per_step, 2, HD, HD), F32),
                   jax.ShapeDtypeStruct((GDN_QK_HEADS, nch * per_step, nsys, GROUP, GROUP), BF16)],
        scratch_shapes=[pltpu.VMEM((HD, HD), F32), pltpu.VMEM((HD, HD), F32), pltpu.VMEM((8, XW), F32)],
        args=(pc, cw, cpar, nw), exchange=exchange)


def _branch_c_bwd(pc, states, inverses, cw, cpar, nw, dy, name, exchange=None):
    t = pc.shape[0]
    rows = _block_rows(t, GDN_BLOCK_CHUNKS)
    nch = t // rows
    XW = 4 * HD
    hb = rows // 8

    nsys = 2 * rows // GROUP

    def body(p_ref, halo_ref, s_ref, x_ref, w_ref, cp_ref, nw_ref, dy_ref, dp_ref, dw_ref, dcp_ref, dnw_ref,
             dsa_ref, dsb_ref, carry_ref):
        h, i = pl.program_id(0), pl.program_id(1)

        @pl.when(i == 0)
        def _():
            dsa_ref[...] = jnp.zeros_like(dsa_ref)
            dsb_ref[...] = jnp.zeros_like(dsb_ref)
            carry_ref[...] = jnp.zeros_like(carry_ref)
            dw_ref[...] = jnp.zeros_like(dw_ref)
            dcp_ref[...] = jnp.zeros_like(dcp_ref)

        @pl.when((i == 0) & (h == 0))
        def _():
            dnw_ref[...] = jnp.zeros_like(dnw_ref)

        keep = 1.0 - (i == nch - 1).astype(F32)
        x_ext = jnp.concatenate([halo_ref[:, 0:XW] * keep, p_ref[:, 0:XW]], axis=0)
        block = functools.partial(_gdn_block, known=tuple(x_ref[0, 0, n] for n in range(nsys)))
        _, vjp, _ = jax.vjp(block, x_ext, p_ref[:, XW:XW + 2 * HD], p_ref[:, XW + 2 * HD:XW + 3 * HD],
                            s_ref[0, 0, 0], s_ref[0, 0, 1], w_ref[0:1, :], w_ref[1:2, :], w_ref[2:3, :], w_ref[3:4, :],
                            cp_ref[0, 0:1, :], cp_ref[0, 1:2, :], nw_ref[...], has_aux=True)
        dx, dz, dba, dsa, dsb, dw0, dw1, dw2, dw3, dal, ddt, dnw = vjp((dy_ref[...].astype(F32), dsa_ref[...], dsb_ref[...]))
        dp_ref[:, 0:XW] = _add_to_tail(dx[8:], carry_ref[...]).astype(BF16)
        dp_ref[:, XW:XW + 2 * HD] = dz.astype(BF16)
        dp_ref[:, XW + 2 * HD:XW + 3 * HD] = dba.astype(BF16)
        carry_ref[...] = dx[:8] * keep
        dsa_ref[...] = dsa
        dsb_ref[...] = dsb
        dw_ref[0:1, :] += dw0
        dw_ref[1:2, :] += dw1
        dw_ref[2:3, :] += dw2
        dw_ref[3:4, :] += dw3
        dcp_ref[0, 0:1, :] += dal
        dcp_ref[0, 1:2, :] += ddt
        dnw_ref[...] += dnw

    rev = lambda h, i: (nch - 1 - i, h)
    return _call_with_exchange(
        body, name=name, grid=(GDN_QK_HEADS, nch),
        in_specs=[pl.BlockSpec((rows, C_HEAD), rev),
                  pl.BlockSpec((8, C_HEAD), lambda h, i: (jnp.maximum((nch - 1 - i) * hb - 1, 0), h)),
                  pl.BlockSpec((1, 1, 2, HD, HD), lambda h, i: (h, nch - 1 - i, 0, 0, 0)),
                  pl.BlockSpec((1, 1, nsys, GROUP, GROUP), lambda h, i: (h, nch - 1 - i, 0, 0, 0)),
                  pl.BlockSpec((4, XW), lambda h, i: (0, h)),
                  pl.BlockSpec((1, 8, HD), lambda h, i: (h, 0, 0)),
                  pl.BlockSpec((1, HD), lambda h, i: (0, 0)),
                  pl.BlockSpec((rows, 2 * HD), rev)],
        out_specs=[pl.BlockSpec((rows, C_HEAD), rev),
                   pl.BlockSpec((4, XW), lambda h, i: (0, h)),
                   pl.BlockSpec((1, 8, HD), lambda h, i: (h, 0, 0)),
                   pl.BlockSpec((1, HD), lambda h, i: (0, 0))],
        out_shape=[jax.ShapeDtypeStruct((t, NC_COLS), BF16), jax.ShapeDtypeStruct((4, GDN_QK_HEADS * XW), F32),
                   jax.ShapeDtypeStruct((GDN_QK_HEADS, 8, HD), F32), jax.ShapeDtypeStruct((1, HD), F32)],
        scratch_shapes=[pltpu.VMEM((HD, HD), F32), pltpu.VMEM((HD, HD), F32), pltpu.VMEM((8, XW), F32)],
        args=(pc, pc, states, inverses, cw, cpar, nw, dy), exchange=exchange)


def _merge_fwd(pg, bg, ya, yb, yc, name):
    t = pg.shape[0]
    blk = _tile(t, 512)

    def body(g_ref, b_ref, a_ref, b2_ref, c_ref, o_ref):
        gate = _sigmoid(g_ref[...] + b_ref[...])
        o_ref[...] = (gate[:, 0:D] * a_ref[...] + gate[:, D:2 * D] * b2_ref[...] + gate[:, 2 * D:3 * D] * c_ref[...]).astype(BF16)

    row = pl.BlockSpec((blk, D), lambda i: (i, 0))
    return pl.pallas_call(
        body, name=name, grid=(t // blk,),
        in_specs=[pl.BlockSpec((blk, NG), lambda i: (i, 0)), pl.BlockSpec((1, NG), lambda i: (0, 0)), row, row, row],
        out_specs=row, out_shape=jax.ShapeDtypeStruct((t, D), BF16),
        compiler_params=_cparams("parallel"))(pg, bg, ya, yb, yc)


def _merge_project(pg, bg, ya, yb, yc, wo, x, name):
    t = pg.shape[0]
    blk = _tile(t, 512)

    def body(g_ref, b_ref, a_ref, b2_ref, c_ref, w_ref, x_ref, m_ref, o_ref):
        gate = _sigmoid(g_ref[...] + b_ref[...])
        merged = (gate[:, 0:D] * a_ref[...] + gate[:, D:2 * D] * b2_ref[...] + gate[:, 2 * D:3 * D] * c_ref[...]).astype(BF16)
        m_ref[...] = merged
        o_ref[...] = _nn(merged, w_ref[...]) + x_ref[...]

    row = pl.BlockSpec((blk, D), lambda i: (i, 0))
    return pl.pallas_call(
        body, name=name, grid=(t // blk,),
        in_specs=[pl.BlockSpec((blk, NG), lambda i: (i, 0)), pl.BlockSpec((1, NG), lambda i: (0, 0)), row, row, row,
                  pl.BlockSpec((D, D), lambda i: (0, 0)), row],
        out_specs=[row, row], out_shape=[jax.ShapeDtypeStruct((t, D), BF16), jax.ShapeDtypeStruct((t, D), F32)],
        compiler_params=_cparams("parallel"))(pg, bg, ya, yb, yc, wo, x)


def _merge_bwd(dx, wo, pg, bg, ya, yb, yc, name):
    t = pg.shape[0]
    blk = _tile(t, 512)

    def body(dx_ref, w_ref, g_ref, b_ref, a_ref, b2_ref, c_ref, dg_ref, da_ref, db_ref, dc_ref, dbg_ref):
        @pl.when(pl.program_id(0) == 0)
        def _():
            dbg_ref[...] = jnp.zeros_like(dbg_ref)

        gate = _sigmoid(g_ref[...] + b_ref[...])
        dmv = _nt(dx_ref[...], w_ref[...]).astype(BF16).astype(F32)
        for j, (y_ref, dy_ref) in enumerate(((a_ref, da_ref), (b2_ref, db_ref), (c_ref, dc_ref))):
            gj = gate[:, j * D:(j + 1) * D]
            dy_ref[...] = (dmv * gj).astype(BF16)
            dgj = dmv * y_ref[...] * gj * (1.0 - gj)
            dg_ref[:, j * D:(j + 1) * D] = dgj.astype(BF16)
            dbg_ref[:, j * D:(j + 1) * D] += jnp.sum(dgj, axis=0, keepdims=True)

    row = pl.BlockSpec((blk, D), lambda i: (i, 0))
    wide = pl.BlockSpec((blk, NG), lambda i: (i, 0))
    vec = pl.BlockSpec((1, NG), lambda i: (0, 0))
    return pl.pallas_call(
        body, name=name, grid=(t // blk,),
        in_specs=[row, pl.BlockSpec((D, D), lambda i: (0, 0)), wide, vec, row, row, row],
        out_specs=[wide, row, row, row, vec],
        out_shape=[jax.ShapeDtypeStruct((t, NG), BF16)] + [jax.ShapeDtypeStruct((t, D), BF16)] * 3
                  + [jax.ShapeDtypeStruct((1, NG), F32)],
        compiler_params=_cparams("arbitrary"))(dx, wo, pg, bg, ya, yb, yc)


def _adamw_math(w, g, m, v):
    m = ADAM_B1 * m + (1.0 - ADAM_B1) * g
    v = ADAM_B2 * v + (1.0 - ADAM_B2) * (g * g)
    m_hat = m / (1.0 - ADAM_B1 ** ADAM_STEP)
    v_hat = v / (1.0 - ADAM_B2 ** ADAM_STEP)
    delta = -ADAM_LR * (m_hat / (jnp.sqrt(v_hat) + ADAM_EPS) + ADAM_WD * w)
    return delta, m, v


def _sum_adamw(parts, w, m, v, name, exchange=None):
    layers = len(parts)
    r, c = parts[0].shape[1:]
    br = r if r <= 256 else 256
    nb = r // br
    assert r % br == 0 and w.shape == (layers * r, c)

    def body(*refs):
        w_ref, m_ref, v_ref, g_ref, d_ref, nm_ref, nv_ref = refs[layers:]
        for l in range(layers):
            @pl.when(pl.program_id(0) == l)
            def _(p_ref=refs[l]):
                g = p_ref[0].astype(F32)
                for k in range(1, N_DEV):
                    g = g + p_ref[k].astype(F32)
                g_ref[...] = g
                d_ref[...], nm_ref[...], nv_ref[...] = _adamw_math(w_ref[...], g, m_ref[...], v_ref[...])

    blk = pl.BlockSpec((br, c), lambda l, i: (l * nb + i, 0))
    part_specs = [pl.BlockSpec((N_DEV, br, c), lambda l, i, q=q: (0, jnp.where(l == q, i, jnp.where(l < q, 0, nb - 1)), 0))
                  for q in range(layers)]
    return _call_with_exchange(
        body, name=name, grid=(layers, nb), in_specs=part_specs + [blk, blk, blk], out_specs=[blk] * 4,
        out_shape=[jax.ShapeDtypeStruct((layers * r, c), F32)] * 4, scratch_shapes=[],
        args=(*parts, w, m, v), exchange=exchange)


def _adamw(g, w, m, v, name):
    def body(g_ref, w_ref, m_ref, v_ref, d_ref, nm_ref, nv_ref):
        d_ref[...], nm_ref[...], nv_ref[...] = _adamw_math(w_ref[...], g_ref[...], m_ref[...], v_ref[...])

    return pl.pallas_call(body, name=name, out_shape=[jax.ShapeDtypeStruct(w.shape, F32)] * 3)(g, w, m, v)


def _sum_slots(parts, name):
    def body(p_ref, o_ref):
        g = p_ref[0]
        for k in range(1, N_DEV):
            g = g + p_ref[k]
        o_ref[...] = g

    return pl.pallas_call(body, name=name, out_shape=jax.ShapeDtypeStruct(parts.shape[1:], F32))(parts)


def _exchange(srcs, name, broadcast):
    n = len(srcs)

    def body(*refs):
        copies = _exchange_copies(refs[:n], refs[n:2 * n], *refs[2 * n:], broadcast)
        for cp in copies:
            cp.start()
        for cp in copies:
            cp.wait()

    return pl.pallas_call(
        body, name=name, in_specs=[HBM_SPEC] * n, out_specs=[HBM_SPEC] * n, out_shape=_exchange_shapes(srcs, broadcast),
        scratch_shapes=_exchange_semaphores(n))(*srcs)


def _gather_two_level(srcs, name):
    n = len(srcs)

    def body(*refs):
        src_refs, dst_refs = refs[:n], refs[n:2 * n]
        send_sems, recv_sems, local_sems = refs[2 * n:]
        x, y, c = lax.axis_index("x"), lax.axis_index("y"), lax.axis_index("c")
        index_of = lambda px, py, pc: 4 * px + 2 * py + pc
        me, other_core = index_of(x, y, c), (x, y, 1 - c)
        chips = [(1 - x, y), (x, 1 - y), (1 - x, 1 - y)]

        def copy(k, a, block, to, src=None):
            return pltpu.make_async_remote_copy(
                src_ref=dst_refs[a].at[block] if src is None else src, dst_ref=dst_refs[a].at[block],
                send_sem=send_sems.at[k, a], recv_sem=recv_sems.at[k, a], device_id=to, device_id_type=MESH)

        local = [pltpu.make_async_copy(src_refs[a], dst_refs[a].at[me], local_sems.at[a]) for a in range(n)]
        first = [copy(0, a, me, other_core, src=src_refs[a]) for a in range(n)]
        first += [copy(1 + j, a, me, (*chip, c), src=src_refs[a]) for j, chip in enumerate(chips) for a in range(n)]
        for cp in local + first:
            cp.start()
        passed = []
        for j, chip in enumerate(chips):
            block = index_of(*chip, c)
            for a in range(n):
                copy(1 + j, a, block, (x, y, c)).wait_recv()
            for a in range(n):
                passed.append(copy(4 + j, a, block, other_core))
                passed[-1].start()
        for a in range(n):
            copy(0, a, index_of(x, y, 1 - c), (x, y, c)).wait_recv()
        for j, chip in enumerate(chips):
            for a in range(n):
                copy(4 + j, a, index_of(*chip, 1 - c), (x, y, c)).wait_recv()
        for cp in first + passed:
            cp.wait_send()
        for cp in local:
            cp.wait()

    return pl.pallas_call(
        body, name=name, in_specs=[HBM_SPEC] * n, out_specs=[HBM_SPEC] * n, out_shape=_exchange_shapes(srcs, True),
        scratch_shapes=_exchange_semaphores(n))(*srcs)


HBM_SPEC = pl.BlockSpec(memory_space=pltpu.HBM)


def _exchange_shapes(srcs, broadcast):
    return [jax.ShapeDtypeStruct((N_DEV,) + (s.shape if broadcast else s.shape[1:]), s.dtype) for s in srcs]


def _exchange_semaphores(n):
    return [pltpu.SemaphoreType.DMA((N_DEV - 1, n)), pltpu.SemaphoreType.DMA((N_DEV - 1, n)), pltpu.SemaphoreType.DMA((n,))]


def _exchange_copies(src_refs, dst_refs, send_sems, recv_sems, local_sems, broadcast):
    x, y, c = lax.axis_index("x"), lax.axis_index("y"), lax.axis_index("c")
    me = 4 * x + 2 * y + c
    copies = []
    for k in range(1, N_DEV):
        px = 1 - x if (k >> 2) & 1 else x
        py = 1 - y if (k >> 1) & 1 else y
        pc = 1 - c if k & 1 else c
        peer = 4 * px + 2 * py + pc
        for a, (src, dst) in enumerate(zip(src_refs, dst_refs)):
            copies.append(pltpu.make_async_remote_copy(
                src_ref=src if broadcast else src.at[peer], dst_ref=dst.at[me],
                send_sem=send_sems.at[k - 1, a], recv_sem=recv_sems.at[k - 1, a],
                device_id=(px, py, pc), device_id_type=MESH))
    for a, (src, dst) in enumerate(zip(src_refs, dst_refs)):
        copies.append(pltpu.make_async_copy(src if broadcast else src.at[me], dst.at[me], local_sems.at[a]))
    return copies


def _call_with_exchange(body, *, name, grid, in_specs, out_specs, out_shape, scratch_shapes, args, exchange):
    if exchange is None:
        outs = pl.pallas_call(body, name=name, grid=grid, in_specs=in_specs, out_specs=out_specs, out_shape=out_shape,
                              scratch_shapes=scratch_shapes,
                              compiler_params=_cparams(*["arbitrary"] * len(grid)))(*args)
        return outs, None
    srcs, broadcast = exchange
    n, n_in, n_out, n_scr = len(srcs), len(args), len(out_shape), len(scratch_shapes)
    steps = 1
    for g in grid:
        steps *= g

    def hosted(*refs):
        ins, src_refs = refs[:n_in], refs[n_in:n_in + n]
        outs, dst_refs = refs[n_in + n:n_in + n + n_out], refs[n_in + n + n_out:n_in + 2 * n + n_out]
        scratch = refs[n_in + 2 * n + n_out:]
        step = pl.program_id(0)
        for axis in range(1, len(grid)):
            step = step * grid[axis] + pl.program_id(axis)

        @pl.when(step == 0)
        def _():
            for cp in _exchange_copies(src_refs, dst_refs, *scratch[n_scr:], broadcast):
                cp.start()

        body(*ins, *outs, *scratch[:n_scr])

        @pl.when(step == steps - 1)
        def _():
            for cp in _exchange_copies(src_refs, dst_refs, *scratch[n_scr:], broadcast):
                cp.wait()

    outs = pl.pallas_call(
        hosted, name=name, grid=grid, in_specs=list(in_specs) + [HBM_SPEC] * n, out_specs=list(out_specs) + [HBM_SPEC] * n,
        out_shape=list(out_shape) + _exchange_shapes(srcs, broadcast),
        scratch_shapes=list(scratch_shapes) + _exchange_semaphores(n),
        compiler_params=_cparams(*["arbitrary"] * len(grid)))(*args, *srcs)
    return outs[:n_out], outs[n_out:]


def _regroup_w_in(w):
    wa = w[:, OFF_A:OFF_A + NA]
    seg = lambda off, h, n=HD: w[:, off + h * n: off + (h + 1) * n]
    wb = jnp.concatenate([seg(OFF_B + s * 512, h) for h in range(HGRN_HEADS) for s in range(4)], axis=1)
    parts = []
    for h in range(GDN_QK_HEADS):
        small = jnp.concatenate(
            [w[:, OFF_BETA + 2 * h: OFF_BETA + 2 * h + 2], w[:, OFF_CA + 2 * h: OFF_CA + 2 * h + 2],
             jnp.zeros((w.shape[0], HD - 4), w.dtype)], axis=1)
        parts += [seg(OFF_CQ, h), seg(OFF_CK, h), seg(OFF_CV, h, 2 * HD), seg(OFF_CZ, h, 2 * HD), small]
    wc = jnp.concatenate(parts, axis=1)
    wg = w[:, OFF_G:OFF_G + NG]
    return wa, wb, wc, wg


def _ungroup_dw_in(da, db, dc, dg):
    bq = [jnp.concatenate([db[:, h * 512 + s * HD: h * 512 + (s + 1) * HD] for h in range(HGRN_HEADS)], axis=1)
          for s in range(4)]
    ch = lambda h, lo, hi: dc[:, h * C_HEAD + lo: h * C_HEAD + hi]
    heads = range(GDN_QK_HEADS)
    cq = jnp.concatenate([ch(h, 0, HD) for h in heads], axis=1)
    ck = jnp.concatenate([ch(h, HD, 2 * HD) for h in heads], axis=1)
    cv = jnp.concatenate([ch(h, 2 * HD, 4 * HD) for h in heads], axis=1)
    cz = jnp.concatenate([ch(h, 4 * HD, 6 * HD) for h in heads], axis=1)
    cbeta = jnp.concatenate([ch(h, 6 * HD, 6 * HD + 2) for h in heads], axis=1)
    ca = jnp.concatenate([ch(h, 6 * HD + 2, 6 * HD + 4) for h in heads], axis=1)
    return jnp.concatenate([da] + bq + [cq, ck, cv, cbeta, ca, cz, dg], axis=1)


def _regroup_conv_c(cw):
    parts = []
    for h in range(GDN_QK_HEADS):
        parts += [cw[:, h * HD:(h + 1) * HD], cw[:, 512 + h * HD: 512 + (h + 1) * HD],
                  cw[:, 1024 + 2 * h * HD: 1024 + (2 * h + 2) * HD]]
    return jnp.concatenate(parts, axis=1)


def _ungroup_conv_c(d):
    heads = range(GDN_QK_HEADS)
    q = jnp.concatenate([d[:, h * 512: h * 512 + HD] for h in heads], axis=1)
    k = jnp.concatenate([d[:, h * 512 + HD: h * 512 + 2 * HD] for h in heads], axis=1)
    v = jnp.concatenate([d[:, h * 512 + 2 * HD: h * 512 + 4 * HD] for h in heads], axis=1)
    return jnp.concatenate([q, k, v], axis=1)


def _numel(shape):
    n = 1
    for d in shape:
        n *= d
    return n


def _pack(arrays, rows):
    flat = jnp.concatenate([a.reshape(-1) for a in arrays])
    return jnp.pad(flat, (0, rows * 128 - flat.shape[0])).reshape(rows, 128)


def _unpack(packed, shapes):
    flat = packed.reshape(-1)
    out, off = [], 0
    for s in shapes:
        out.append(flat[off:off + _numel(s)].reshape(s))
        off += _numel(s)
    return out


def _rows_for(shapes):
    return -(-sum(_numel(s) for s in shapes) // 1024) * 8


def kernel(x, norm_w, w_in, b_gate, conv_a, conv_c, a_log, dt_bias, lower_bounds, hgrn_norm_w, gdn_norm_w, w_out_a, w_out_b, w_out_c, w_o, final_norm_w, loss_target, m_norm_w, m_w_in, m_b_gate, m_conv_a, m_conv_c, m_a_log, m_dt_bias, m_lower_bounds, m_hgrn_norm_w, m_gdn_norm_w, m_w_out_a, m_w_out_b, m_w_out_c, m_w_o, m_final_norm_w, v_norm_w, v_w_in, v_b_gate, v_conv_a, v_conv_c, v_a_log, v_dt_bias, v_lower_bounds, v_hgrn_norm_w, v_gdn_norm_w, v_w_out_a, v_w_out_b, v_w_out_c, v_w_o, v_final_norm_w):
    me = 4 * lax.axis_index("x") + 2 * lax.axis_index("y") + lax.axis_index("c")
    xs = x[0]
    target = loss_target[0]
    in_shard = w_in.shape[2]

    big = [w_in, w_out_a, w_out_b, w_out_c, w_o]
    shards_of = lambda l: [w[l].astype(BF16) for w in big]
    conv_shapes = [(DEPTH, 3, CONV_W), (DEPTH, 4, 2048)]
    conv_rows = _rows_for(conv_shapes)
    ca_full = lax.dynamic_update_slice(jnp.zeros(conv_shapes[0], F32), conv_a, (0, 0, me * conv_a.shape[2]))
    cc_full = lax.dynamic_update_slice(jnp.zeros(conv_shapes[1], F32), conv_c, (0, 0, me * conv_c.shape[2]))
    g_in0, conv_parts = _gather_two_level([w_in[0].astype(BF16), _pack([ca_full, cc_full], conv_rows)], "gather_l0")
    conv_a_full, conv_c_full = _unpack(_sum_slots(conv_parts, "sum_conv"), conv_shapes)

    lb_pad = jnp.pad(lower_bounds, ((0, 8 - DEPTH), (0, 0)))
    lbs = _lower_bounds_fwd(lb_pad, "lower_bounds_fwd")

    def input_weights(l, g_in):
        wa, wb, wc, wg = _regroup_w_in(jnp.concatenate([g_in[q] for q in range(N_DEV)], axis=1))
        lanes = lambda vec: jnp.pad(vec.reshape(GDN_QK_HEADS, 1, 2), ((0, 0), (0, 0), (0, HD - 2)))
        cpar = jnp.concatenate([lanes(a_log[l]), lanes(dt_bias[l]), jnp.zeros((GDN_QK_HEADS, 6, HD), F32)], axis=1)
        return dict(
            wa=wa, wb=wb, wc=wc, wg=wg, cpar=cpar,
            nw=norm_w[l:l + 1], bg=b_gate[l:l + 1], cwa=conv_a_full[l], cwc=_regroup_conv_c(conv_c_full[l]),
            lb=lbs[l:l + 1], hnw=hgrn_norm_w[l:l + 1], gnw=gdn_norm_w[l:l + 1])

    def output_weights(g_oa, g_ob, g_oc, g_o):
        return dict(woa=jnp.concatenate([g_oa[q] for q in range(N_DEV)], axis=1),
                    wob=jnp.concatenate([g_ob[q] for q in range(N_DEV)], axis=1), woc=g_oc.reshape(D, D), wo=g_o.reshape(D, D))

    layers = [input_weights(0, g_in0)]

    saved = []
    cur = xs
    for l in range(DEPTH):
        L = layers[l]
        n = f"l{l}_"
        h = _rmsnorm_fwd(cur, L["nw"], n + "rms")
        pa = _matmul(h, L["wa"], "nn", n + "proj_a")
        pb = _matmul(h, L["wb"], "nn", n + "proj_b")
        pc = _matmul(h, L["wc"], "nn", n + "proj_c")
        pg = _matmul(h, L["wg"], "nn", n + "proj_g", out_dtype=BF16)
        ua = _branch_a_fwd(pa, L["cwa"], n + "conv_fwd")
        carry = (shards_of(0)[1:] + shards_of(1)[1:], True) if l == 0 else None
        (ub, sb), gathered = _branch_b_fwd(pb, L["lb"], L["hnw"], n + "hgrn_fwd", exchange=carry)
        if carry is not None:
            L.update(output_weights(*gathered[:4]))
            next_outputs = output_weights(*gathered[4:])
        carry = (shards_of(1)[:1], True) if l == 0 else None
        (uc, sc, xc), gathered = _branch_c_fwd(pc, L["cwc"], L["cpar"], L["gnw"], n + "gdn_fwd", exchange=carry)
        if carry is not None:
            layers.append(dict(input_weights(1, gathered[0]), **next_outputs))
        ya = _matmul(ua, L["woa"], "nn", n + "out_a", out_dtype=BF16)
        yb = _matmul(ub, L["wob"], "nn", n + "out_b", out_dtype=BF16)
        yc = _matmul(uc, L["woc"], "nn", n + "out_c", out_dtype=BF16)
        merged, nxt = _merge_project(pg, L["bg"], ya, yb, yc, L["wo"], cur, n + "merge_out_o")
        saved.append(dict(x=cur, h=h, pa=pa, pb=pb, pc=pc, pg=pg, ua=ua, ub=ub, uc=uc, sb=sb, sc=sc, xc=xc,
                          ya=ya, yb=yb, yc=yc, merged=merged))
        cur = nxt

    loss_part, dx, d_final = _loss_head(cur, final_norm_w.reshape(1, D), target, "loss_head")

    def outgoing(g):
        cols = lambda a, n: jnp.stack([a[:, p * n:(p + 1) * n] for p in range(N_DEV)]).astype(BF16)
        rows = lambda a: a.reshape(N_DEV, a.shape[0] // N_DEV, a.shape[1]).astype(BF16)
        first = [cols(g["w_in"], in_shard)] if "w_in" in g else [None]
        if "w_o" not in g:
            return first
        return first + [cols(g["w_out_a"], 128), cols(g["w_out_b"], 128), rows(g["w_out_c"]), rows(g["w_o"])]

    grads = [None] * DEPTH
    dlbs_rows = [None] * DEPTH
    incoming = [None] * DEPTH
    for l in reversed(range(DEPTH)):
        L, S = layers[l], saved[l]
        n = f"l{l}_"
        d_wo = _matmul(S["merged"], dx, "tn", n + "dw_o", out_dtype=BF16)
        dpg, dya, dyb, dyc, d_bg = _merge_bwd(dx, L["wo"], S["pg"], L["bg"], S["ya"], S["yb"], S["yc"], n + "merge_bwd")
        dua = _matmul(dya, L["woa"], "nt", n + "d_ua", out_dtype=BF16)
        dub = _matmul(dyb, L["wob"], "nt", n + "d_ub", out_dtype=BF16)
        duc = _matmul(dyc, L["woc"], "nt", n + "d_uc", out_dtype=BF16)
        d_woa = _matmul(S["ua"], dya, "tn", n + "dw_out_a", out_dtype=BF16)
        d_wob = _matmul(S["ub"], dyb, "tn", n + "dw_out_b", out_dtype=BF16)
        d_woc = _matmul(S["uc"], dyc, "tn", n + "dw_out_c", out_dtype=BF16)
        dpa, d_cwa = _branch_a_bwd(S["pa"], L["cwa"], dua, n + "conv_bwd")
        out_grads = dict(w_out_a=d_woa, w_out_b=d_wob, w_out_c=d_woc, w_o=d_wo)
        carry = (outgoing(out_grads)[1:], False) if l == 0 else None
        (dpb, d_lb, d_hnw), arrived_out = _branch_b_bwd(S["pb"], S["sb"], L["lb"], L["hnw"], dub, n + "hgrn_bwd", exchange=carry)
        carry = (outgoing(grads[l + 1]), False) if l + 1 < DEPTH else None
        (dpc, d_cwc, d_cpar, d_gnw), arrived = _branch_c_bwd(S["pc"], S["sc"], S["xc"], L["cwc"], L["cpar"], L["gnw"], duc,
                                                             n + "gdn_bwd", exchange=carry)
        if carry is not None:
            incoming[l + 1] = arrived
        d_win = _ungroup_dw_in(*[_matmul(S["h"], dp, "tn", f"{n}dw_{piece}", out_dtype=BF16)
                                 for dp, piece in ((dpa, "a"), (dpb, "b"), (dpc, "c"), (dpg, "g"))])
        carry = (outgoing(dict(w_in=d_win)), False) if l == 0 else None
        dh, arrived_in = _matmul_nt_sum([(dpa, L["wa"]), (dpb, L["wb"]), (dpc, L["wc"]), (dpg, L["wg"])], n + "dh",
                                        exchange=carry)
        (dx, d_nw), _ = _rmsnorm_bwd(dh, S["x"], L["nw"], dx, n + "rms_bwd")
        dlbs_rows[l] = d_lb
        grads[l] = dict(w_in=d_win, w_out_a=d_woa, w_out_b=d_wob, w_out_c=d_woc, w_o=d_wo, norm_w=d_nw[0],
                        b_gate=d_bg[0], conv_a=d_cwa, conv_c=_ungroup_conv_c(d_cwc),
                        a_log=d_cpar[:, 0, 0:2].reshape(-1), dt_bias=d_cpar[:, 1, 0:2].reshape(-1),
                        hgrn_norm_w=d_hnw[0], gdn_norm_w=d_gnw[0])
    grad_x = dx[None]
    d_lower = _lower_bounds_bwd(lb_pad, jnp.pad(jnp.concatenate(dlbs_rows, axis=0), ((0, 8 - DEPTH), (0, 0))),
                                "lower_bounds_bwd")[:DEPTH]

    stack = lambda name: jnp.stack([grads[l][name] for l in range(DEPTH)])
    small_names = ["norm_w", "b_gate", "conv_a", "conv_c", "a_log", "dt_bias", "lower_bounds", "hgrn_norm_w",
                   "gdn_norm_w", "final_norm_w", "loss"]
    small_vals = {k: stack(k) for k in ("norm_w", "b_gate", "conv_a", "conv_c", "a_log", "dt_bias", "hgrn_norm_w", "gdn_norm_w")}
    small_vals.update(lower_bounds=d_lower, final_norm_w=d_final[0], loss=loss_part.reshape(1))
    small_shapes = [small_vals[k].shape for k in small_names]
    small_rows = _rows_for(small_shapes)
    small_pack = _pack([small_vals[k] for k in small_names], small_rows)

    incoming[0] = list(arrived_in) + list(arrived_out)
    big_out = {}
    for j, (name, w, m, v) in enumerate((("w_in", w_in, m_w_in, v_w_in), ("w_out_a", w_out_a, m_w_out_a, v_w_out_a),
                                         ("w_out_b", w_out_b, m_w_out_b, v_w_out_b), ("w_out_c", w_out_c, m_w_out_c, v_w_out_c),
                                         ("w_o", w_o, m_w_o, v_w_o))):
        parts = [incoming[l][j] for l in range(DEPTH)]
        r2 = lambda a: a.reshape(DEPTH * parts[0].shape[1], parts[0].shape[2])
        carry = ([small_pack], True) if name == "w_in" else None
        outs, arrived = _sum_adamw(parts, r2(w), r2(m), r2(v), "adamw_" + name, exchange=carry)
        if carry is not None:
            small_parts, = arrived
        big_out[name] = [o.reshape(w.shape) for o in outs]

    total = dict(zip(small_names, _unpack(_sum_slots(small_parts, "sum_small"), small_shapes)))
    loss = total["loss"][0]
    g_conv_a = lax.dynamic_slice(total["conv_a"], (0, 0, me * conv_a.shape[2]), conv_a.shape)
    g_conv_c = lax.dynamic_slice(total["conv_c"], (0, 0, me * conv_c.shape[2]), conv_c.shape)

    small_w = dict(norm_w=(norm_w, m_norm_w, v_norm_w), b_gate=(b_gate, m_b_gate, v_b_gate),
                   conv_a=(conv_a, m_conv_a, v_conv_a), conv_c=(conv_c, m_conv_c, v_conv_c),
                   a_log=(a_log, m_a_log, v_a_log), dt_bias=(dt_bias, m_dt_bias, v_dt_bias),
                   lower_bounds=(lower_bounds, m_lower_bounds, v_lower_bounds),
                   hgrn_norm_w=(hgrn_norm_w, m_hgrn_norm_w, v_hgrn_norm_w), gdn_norm_w=(gdn_norm_w, m_gdn_norm_w, v_gdn_norm_w),
                   final_norm_w=(final_norm_w, m_final_norm_w, v_final_norm_w))
    small_g = dict(total, conv_a=g_conv_a, conv_c=g_conv_c)
    upd_names = small_names[:-1]
    upd_shapes = [small_w[k][0].shape for k in upd_names]
    upd_rows = _rows_for(upd_shapes)
    pk = lambda j: _pack([small_w[k][j] for k in upd_names], upd_rows)
    s_delta, s_m, s_v = _adamw(_pack([small_g[k] for k in upd_names], upd_rows), pk(0), pk(1), pk(2), "adamw_small")
    small_out = {k: [small_g[k], d, mm, vv] for k, d, mm, vv in
                 zip(upd_names, _unpack(s_delta, upd_shapes), _unpack(s_m, upd_shapes), _unpack(s_v, upd_shapes))}

    order = ["norm_w", "w_in", "b_gate", "conv_a", "conv_c", "a_log", "dt_bias", "lower_bounds", "hgrn_norm_w",
             "gdn_norm_w", "w_out_a", "w_out_b", "w_out_c", "w_o", "final_norm_w"]
    res = {**small_out, **big_out}
    outs = [loss, grad_x]
    for j in range(4):
        outs += [res[k][j] for k in order]
    return tuple(outs)
```
